```python
import math
import jax
import jax.numpy as jnp
from jax import lax
import numpy as np

D_MODEL = 1024
BATCH = 8
SEQ = 4096
DEPTH = 2

N_META = 16
CONV_K = 4
RMS_EPS = 1e-6
L2_EPS = 1e-6
D_FF = 4 * D_MODEL
N_BRANCH = 3

GDN_HEADS = 8
GDN_DK = 128
GDN_DV = 128
GDN_CHUNK = 64
GDN_QK_W = GDN_HEADS * GDN_DK
GDN_V_W = GDN_HEADS * GDN_DV

SSD_HEADS = 16
SSD_HEAD_DIM = 64
SSD_INNER = SSD_HEADS * SSD_HEAD_DIM
SSD_GROUPS = 4
SSD_HPG = SSD_HEADS // SSD_GROUPS
SSD_STATE = 128
SSD_CHUNK = 128
SSD_CONV_DIM = SSD_INNER + 2 * SSD_GROUPS * SSD_STATE

SWA_Q_HEADS = 16
SWA_KV_HEADS = 4
SWA_REP = SWA_Q_HEADS // SWA_KV_HEADS
SWA_HEAD_DIM = 64
SWA_WINDOW = 128
SWA_Q_W = SWA_Q_HEADS * SWA_HEAD_DIM
SWA_KV_W = SWA_KV_HEADS * SWA_HEAD_DIM

IN_SIZES = (GDN_QK_W, GDN_QK_W, GDN_V_W, GDN_V_W, GDN_HEADS, GDN_HEADS,
            SSD_INNER, SSD_CONV_DIM, SSD_HEADS,
            SWA_Q_W, SWA_KV_W, SWA_KV_W,
            N_BRANCH * D_MODEL)
IN_W = sum(IN_SIZES)

kernel_name = "hybrid_gdn_ssd_swa_sink_block"


def rmsnorm(x, w):
    xf = x.astype(jnp.float32)
    y = xf * lax.rsqrt(jnp.mean(xf * xf, axis=-1, keepdims=True) + RMS_EPS)
    return (y * w.astype(jnp.float32)).astype(x.dtype)


def l2norm(x):
    return x * lax.rsqrt(jnp.sum(x * x, axis=-1, keepdims=True) + L2_EPS)


def split_in(u):
    offs = np.cumsum(np.array(IN_SIZES))[:-1].tolist()
    return jnp.split(u, offs, axis=-1)


def causal_dwconv(x, w, b=None):
    y = lax.conv_general_dilated(
        x, w[:, None, :].astype(x.dtype), window_strides=(1,), padding=[(CONV_K - 1, 0)],
        dimension_numbers=("NWC", "WIO", "NWC"), feature_group_count=x.shape[-1])
    if b is not None:
        y = y + b.astype(x.dtype)
    return y


def pad_front(t, pad):
    return jnp.pad(t, [(0, 0), (pad, 0)] + [(0, 0)] * (t.ndim - 2))


def softmax_with_sink(s, sink):
    m = jnp.maximum(jnp.max(s, axis=-1, keepdims=True), sink)
    e = jnp.exp(s - m)
    return e / (jnp.sum(e, axis=-1, keepdims=True) + jnp.exp(sink - m))


def gated_delta_chunked(q, k, v, g, beta):
    bsz, t_len, nh, dk = q.shape
    dv = v.shape[-1]
    c = GDN_CHUNK
    nc = t_len // c
    q = q.reshape(bsz, nc, c, nh, dk).transpose(0, 3, 1, 2, 4) * (dk ** -0.5)
    k = k.reshape(bsz, nc, c, nh, dk).transpose(0, 3, 1, 2, 4)
    v = v.reshape(bsz, nc, c, nh, dv).transpose(0, 3, 1, 2, 4)
    g = g.reshape(bsz, nc, c, nh).transpose(0, 3, 1, 2)
    beta = beta.reshape(bsz, nc, c, nh).transpose(0, 3, 1, 2)
    gam = jnp.cumsum(g, axis=-1)
    tri_incl = jnp.tril(jnp.ones((c, c), dtype=bool))
    tri_strict = jnp.tril(jnp.ones((c, c), dtype=bool), -1)
    decay = jnp.exp(jnp.where(tri_incl, gam[..., :, None] - gam[..., None, :], -jnp.inf))
    kb = k * beta[..., None]
    a_low = jnp.where(tri_strict, jnp.einsum("bhncd,bhnsd->bhncs", kb, k) * decay, 0.0)
    eye = jnp.eye(c, dtype=jnp.float32)
    rhs = jnp.concatenate([v * beta[..., None], kb * jnp.exp(gam)[..., None]], axis=-1)
    sol = lax.linalg.triangular_solve(a_low + eye, rhs, left_side=True, lower=True, unit_diagonal=True)
    u = sol[..., :dv]
    w = sol[..., dv:]
    attn_qk = jnp.einsum("bhncd,bhnsd->bhncs", q, k) * decay
    q_dec = q * jnp.exp(gam)[..., None]
    g_last = gam[..., -1]
    k_tail = k * jnp.exp(g_last[..., None] - gam)[..., None]

    def step(s_state, inp):
        u_c, w_c, qd_c, a_c, kt_c, gl_c = inp
        v_new = u_c - jnp.einsum("bhcd,bhde->bhce", w_c, s_state)
        o_c = jnp.einsum("bhcd,bhde->bhce", qd_c, s_state) + jnp.einsum("bhcs,bhse->bhce", a_c, v_new)
        s_state = s_state * jnp.exp(gl_c)[..., None, None] + jnp.einsum("bhcd,bhce->bhde", kt_c, v_new)
        return s_state, o_c

    xs = (jnp.moveaxis(u, 2, 0), jnp.moveaxis(w, 2, 0), jnp.moveaxis(q_dec, 2, 0),
          jnp.moveaxis(attn_qk, 2, 0), jnp.moveaxis(k_tail, 2, 0), jnp.moveaxis(g_last, -1, 0))
    s0 = jnp.zeros((bsz, nh, dk, dv), jnp.float32)
    _, o = lax.scan(step, s0, xs)
    return o.transpose(1, 0, 3, 2, 4).reshape(bsz, t_len, nh, dv)


def gdn_branch(q, k, v, gate, b, a, conv_w, a_log, dt_bias, norm_w):
    dtype = q.dtype
    bsz, seq_len = q.shape[:2]
    qkv = jax.nn.silu(causal_dwconv(jnp.concatenate([q, k, v], axis=-1), conv_w))
    q, k, v = jnp.split(qkv, [GDN_QK_W, 2 * GDN_QK_W], axis=-1)
    q = l2norm(q.astype(jnp.float32).reshape(bsz, seq_len, GDN_HEADS, GDN_DK))
    k = l2norm(k.astype(jnp.float32).reshape(bsz, seq_len, GDN_HEADS, GDN_DK))
    v = v.astype(jnp.float32).reshape(bsz, seq_len, GDN_HEADS, GDN_DV)
    beta = jax.nn.sigmoid(b.astype(jnp.float32))
    g = -jnp.exp(a_log.astype(jnp.float32)) * jax.nn.softplus(a.astype(jnp.float32) + dt_bias.astype(jnp.float32))
    pad = GDN_CHUNK - N_META
    o = gated_delta_chunked(pad_front(q, pad), pad_front(k, pad), pad_front(v, pad),
                            pad_front(g, pad), pad_front(beta, pad))[:, pad:]
    gate = gate.astype(jnp.float32).reshape(bsz, seq_len, GDN_HEADS, GDN_DV)
    o = rmsnorm(o, norm_w) * jax.nn.silu(gate)
    return o.reshape(bsz, seq_len, GDN_V_W).astype(dtype)


def ssd_chunked(xdt, adt, bm, cm):
    bsz, t_len, ng, nr, hp = xdt.shape
    c = SSD_CHUNK
    nc = t_len // c
    xdt = xdt.reshape(bsz, nc, c, ng, nr, hp)
    bm = bm.reshape(bsz, nc, c, ng, -1)
    cm = cm.reshape(bsz, nc, c, ng, -1)
    acum = jnp.cumsum(adt.reshape(bsz, nc, c, ng, nr).transpose(0, 3, 4, 1, 2), axis=-1)
    tri = jnp.tril(jnp.ones((c, c), dtype=bool))
    lmat = jnp.exp(jnp.where(tri, acum[..., :, None] - acum[..., None, :], -jnp.inf))
    cb = jnp.einsum("bclgn,bcsgn->bgcls", cm, bm)
    y_diag = jnp.einsum("bgcls,bgrcls,bcsgrp->bclgrp", cb, lmat, xdt)
    decay_states = jnp.exp(acum[..., -1:] - acum)
    states = jnp.einsum("bclgn,bgrcl,bclgrp->bcgrpn", bm, decay_states, xdt)
    chunk_decay = jnp.exp(acum[..., -1])

    def step(h, inp):
        st, dec = inp
        return h * dec[..., None, None] + st, h

    h0 = jnp.zeros((bsz, ng, nr, hp, states.shape[-1]), jnp.float32)
    _, h_in = lax.scan(step, h0, (jnp.moveaxis(states, 1, 0), jnp.moveaxis(chunk_decay, -1, 0)))
    h_in = jnp.moveaxis(h_in, 0, 1)
    y_off = jnp.einsum("bclgn,bcgrpn,bgrcl->bclgrp", cm, h_in, jnp.exp(acum))
    return (y_diag + y_off).reshape(bsz, t_len, ng, nr, hp)


def ssd_branch(z, xbc, dt, conv_w, conv_b, dt_bias, a_log, d_skip, norm_w):
    dtype = z.dtype
    bsz, seq_len = z.shape[:2]
    xbc = jax.nn.silu(causal_dwconv(xbc, conv_w, conv_b)).astype(jnp.float32)
    xs, bm, cm = jnp.split(xbc, [SSD_INNER, SSD_INNER + SSD_GROUPS * SSD_STATE], axis=-1)
    xs = xs.reshape(bsz, seq_len, SSD_GROUPS, SSD_HPG, SSD_HEAD_DIM)
    bm = bm.reshape(bsz, seq_len, SSD_GROUPS, SSD_STATE)
    cm = cm.reshape(bsz, seq_len, SSD_GROUPS, SSD_STATE)
    dtp = jax.nn.softplus(dt.astype(jnp.float32) + dt_bias.astype(jnp.float32))
    dtp = dtp.reshape(bsz, seq_len, SSD_GROUPS, SSD_HPG)
    a = -jnp.exp(a_log.astype(jnp.float32)).reshape(SSD_GROUPS, SSD_HPG)
    pad = SSD_CHUNK - N_META
    y = ssd_chunked(pad_front(xs * dtp[..., None], pad), pad_front(dtp * a, pad),
                    pad_front(bm, pad), pad_front(cm, pad))[:, pad:]
    y = y + d_skip.astype(jnp.float32).reshape(SSD_GROUPS, SSD_HPG)[:, :, None] * xs
    y = y.reshape(bsz, seq_len, SSD_INNER) * jax.nn.silu(z.astype(jnp.float32))
    y = y.reshape(bsz, seq_len, SSD_GROUPS, SSD_INNER // SSD_GROUPS)
    y = y * lax.rsqrt(jnp.mean(y * y, axis=-1, keepdims=True) + RMS_EPS)
    y = y * norm_w.astype(jnp.float32).reshape(SSD_GROUPS, SSD_INNER // SSD_GROUPS)
    return y.reshape(bsz, seq_len, SSD_INNER).astype(dtype)


def swa_branch(q, k, v, sinks):
    dtype = q.dtype
    bsz, seq_len = q.shape[:2]
    s_len = seq_len - N_META
    wdw = SWA_WINDOW
    nb = s_len // wdw
    q = q.astype(jnp.float32).reshape(bsz, seq_len, SWA_KV_HEADS, SWA_REP, SWA_HEAD_DIM) * (SWA_HEAD_DIM ** -0.5)
    k = k.astype(jnp.float32).reshape(bsz, seq_len, SWA_KV_HEADS, SWA_HEAD_DIM)
    v = v.astype(jnp.float32).reshape(bsz, seq_len, SWA_KV_HEADS, SWA_HEAD_DIM)
    sink = sinks.astype(jnp.float32).reshape(SWA_KV_HEADS, SWA_REP)
    qm, km, vm = q[:, :N_META], k[:, :N_META], v[:, :N_META]
    s_m = jnp.einsum("bqhrd,bkhd->bhrqk", qm, km)
    s_m = jnp.where(jnp.tril(jnp.ones((N_META, N_META), dtype=bool)), s_m, -jnp.inf)
    p_m = softmax_with_sink(s_m, sink[None, :, :, None, None])
    o_m = jnp.einsum("bhrqk,bkhd->bqhrd", p_m, vm).reshape(bsz, N_META, SWA_Q_W)
    qr = q[:, N_META:].reshape(bsz, nb, wdw, SWA_KV_HEADS, SWA_REP, SWA_HEAD_DIM)
    kr = k[:, N_META:].reshape(bsz, nb, wdw, SWA_KV_HEADS, SWA_HEAD_DIM)
    vr = v[:, N_META:].reshape(bsz, nb, wdw, SWA_KV_HEADS, SWA_HEAD_DIM)
    zpad = [(0, 0), (1, 0), (0, 0), (0, 0), (0, 0)]
    kband = jnp.concatenate([jnp.pad(kr, zpad)[:, :-1], kr], axis=2)
    vband = jnp.concatenate([jnp.pad(vr, zpad)[:, :-1], vr], axis=2)
    qpos = jnp.arange(wdw)[:, None] + wdw
    kpos = jnp.arange(2 * wdw)[None, :]
    band = (kpos <= qpos) & (kpos > qpos - wdw)
    prev_ok = (jnp.arange(nb)[:, None] > 0) | (jnp.arange(2 * wdw)[None, :] >= wdw)
    mask = band[None, :, :] & prev_ok[:, None, :]
    s_loc = jnp.einsum("bnqhrd,bnkhd->bnhrqk", qr, kband)
    s_loc = jnp.where(mask[None, :, None, None, :, :], s_loc, -jnp.inf)
    s_meta = jnp.einsum("bnqhrd,bkhd->bnhrqk", qr, km)
    p = softmax_with_sink(jnp.concatenate([s_loc, s_meta], axis=-1), sink[None, None, :, :, None, None])
    o_r = (jnp.einsum("bnhrqk,bnkhd->bnqhrd", p[..., :2 * wdw], vband)
           + jnp.einsum("bnhrqk,bkhd->bnqhrd", p[..., 2 * wdw:], vm))
    o_r = o_r.reshape(bsz, s_len, SWA_Q_W)
    return jnp.concatenate([o_m, o_r], axis=1).astype(dtype)


def _fwd_setup_inputs(seed: int = 0) -> dict:
    key = jax.random.key(seed)
    ks = jax.random.split(key, 23)
    f32 = jnp.float32

    def nrm(k, shape, scale):
        return jax.random.normal(k, shape, f32) * scale

    def gain(k, shape):
        return 1.0 + 0.02 * jax.random.normal(k, shape, f32)

    def dt_bias(k, shape):
        dt = jnp.exp(jax.random.uniform(k, shape, f32, math.log(1e-3), math.log(1e-1)))
        return dt + jnp.log(-jnp.expm1(-dt))

    def a_log(k, shape):
        return jnp.log(jax.random.uniform(k, shape, f32, 1.0, 16.0))

    return {
        "x": nrm(ks[0], (BATCH, SEQ, D_MODEL), 1.0),
        "meta_tokens": nrm(ks[1], (N_META, D_MODEL), 1.0),
        "norm1_w": gain(ks[2], (DEPTH, D_MODEL)),
        "w_in": nrm(ks[3], (DEPTH, D_MODEL, IN_W), D_MODEL ** -0.5),
        "gdn_conv_w": nrm(ks[4], (DEPTH, CONV_K, 2 * GDN_QK_W + GDN_V_W), CONV_K ** -0.5),
        "gdn_a_log": a_log(ks[5], (DEPTH, GDN_HEADS)),
        "gdn_dt_bias": dt_bias(ks[6], (DEPTH, GDN_HEADS)),
        "gdn_norm_w": gain(ks[7], (DEPTH, GDN_DV)),
        "ssd_conv_w": nrm(ks[8], (DEPTH, CONV_K, SSD_CONV_DIM), CONV_K ** -0.5),
        "ssd_conv_b": nrm(ks[9], (DEPTH, SSD_CONV_DIM), 0.02),
        "ssd_dt_bias": dt_bias(ks[10], (DEPTH, SSD_HEADS)),
        "ssd_a_log": a_log(ks[11], (DEPTH, SSD_HEADS)),
        "ssd_d": 1.0 + 0.1 * jax.random.normal(ks[12], (DEPTH, SSD_HEADS), f32),
        "ssd_norm_w": gain(ks[13], (DEPTH, SSD_INNER)),
        "swa_sinks": nrm(ks[14], (DEPTH, SWA_Q_HEADS), 0.5),
        "w_proj_gdn": nrm(ks[15], (DEPTH, GDN_V_W, D_MODEL), GDN_V_W ** -0.5),
        "w_proj_ssd": nrm(ks[16], (DEPTH, SSD_INNER, D_MODEL), SSD_INNER ** -0.5),
        "w_proj_swa": nrm(ks[17], (DEPTH, SWA_Q_W, D_MODEL), SWA_Q_W ** -0.5),
        "w_out": nrm(ks[18], (DEPTH, D_MODEL, D_MODEL), D_MODEL ** -0.5),
        "norm2_w": gain(ks[19], (DEPTH, D_MODEL)),
        "w_up": nrm(ks[20], (DEPTH, D_MODEL, D_FF), D_MODEL ** -0.5),
        "w_down": nrm(ks[21], (DEPTH, D_FF, D_MODEL), D_FF ** -0.5),
        "final_norm_w": gain(ks[22], (D_MODEL,)),
    }


def _fwd_reference(x, meta_tokens, norm1_w, w_in, gdn_conv_w, gdn_a_log, gdn_dt_bias, gdn_norm_w,
              ssd_conv_w, ssd_conv_b, ssd_dt_bias, ssd_a_log, ssd_d, ssd_norm_w, swa_sinks,
              w_proj_gdn, w_proj_ssd, w_proj_swa, w_out, norm2_w, w_up, w_down, final_norm_w):
    bsz = x.shape[0]
    meta = jnp.broadcast_to(meta_tokens.astype(x.dtype)[None], (bsz, N_META, D_MODEL))
    h = jnp.concatenate([meta, x], axis=1)
    for l in range(DEPTH):
        u = rmsnorm(h, norm1_w[l]) @ w_in[l]
        (a_q, a_k, a_v, a_gate, a_b, a_a, b_z, b_xbc, b_dt, c_q, c_k, c_v, gate_logits) = split_in(u)
        y_gdn = gdn_branch(a_q, a_k, a_v, a_gate, a_b, a_a, gdn_conv_w[l], gdn_a_log[l], gdn_dt_bias[l], gdn_norm_w[l])
        y_ssd = ssd_branch(b_z, b_xbc, b_dt, ssd_conv_w[l], ssd_conv_b[l], ssd_dt_bias[l], ssd_a_log[l], ssd_d[l], ssd_norm_w[l])
        y_swa = swa_branch(c_q, c_k, c_v, swa_sinks[l])
        gates = jax.nn.sigmoid(gate_logits.astype(jnp.float32)).astype(h.dtype)
        g_a, g_b, g_c = jnp.split(gates, N_BRANCH, axis=-1)
        merged = (g_a * (y_gdn @ w_proj_gdn[l]) + g_b * (y_ssd @ w_proj_ssd[l])
                  + g_c * (y_swa @ w_proj_swa[l]))
        h = h + merged @ w_out[l]
        hn = rmsnorm(h, norm2_w[l])
        h = h + jnp.square(jax.nn.relu(hn @ w_up[l])) @ w_down[l]
    return rmsnorm(h, final_norm_w)[:, N_META:]


import jax as _jax
import jax.numpy as _jnp

TWIN_FORMAT = 'train_step'
FWD_PARAMS = ['x', 'meta_tokens', 'norm1_w', 'w_in', 'gdn_conv_w', 'gdn_a_log', 'gdn_dt_bias', 'gdn_norm_w', 'ssd_conv_w', 'ssd_conv_b', 'ssd_dt_bias', 'ssd_a_log', 'ssd_d', 'ssd_norm_w', 'swa_sinks', 'w_proj_gdn', 'w_proj_ssd', 'w_proj_swa', 'w_out', 'norm2_w', 'w_up', 'w_down', 'final_norm_w']
TWIN_WEIGHTS = ['meta_tokens', 'norm1_w', 'w_in', 'gdn_conv_w', 'gdn_a_log', 'gdn_dt_bias', 'gdn_norm_w', 'ssd_conv_w', 'ssd_conv_b', 'ssd_dt_bias', 'ssd_a_log', 'ssd_d', 'ssd_norm_w', 'swa_sinks', 'w_proj_gdn', 'w_proj_ssd', 'w_proj_swa', 'w_out', 'norm2_w', 'w_up', 'w_down', 'final_norm_w']
TWIN_DIFF_INPUT = 'x'
TWIN_INPUTS = ['x', 'meta_tokens', 'norm1_w', 'w_in', 'gdn_conv_w', 'gdn_a_log', 'gdn_dt_bias', 'gdn_norm_w', 'ssd_conv_w', 'ssd_conv_b', 'ssd_dt_bias', 'ssd_a_log', 'ssd_d', 'ssd_norm_w', 'swa_sinks', 'w_proj_gdn', 'w_proj_ssd', 'w_proj_swa', 'w_out', 'norm2_w', 'w_up', 'w_down', 'final_norm_w', 'loss_target', 'm_meta_tokens', 'm_norm1_w', 'm_w_in', 'm_gdn_conv_w', 'm_gdn_a_log', 'm_gdn_dt_bias', 'm_gdn_norm_w', 'm_ssd_conv_w', 'm_ssd_conv_b', 'm_ssd_dt_bias', 'm_ssd_a_log', 'm_ssd_d', 'm_ssd_norm_w', 'm_swa_sinks', 'm_w_proj_gdn', 'm_w_proj_ssd', 'm_w_proj_swa', 'm_w_out', 'm_norm2_w', 'm_w_up', 'm_w_down', 'm_final_norm_w', 'v_meta_tokens', 'v_norm1_w', 'v_w_in', 'v_gdn_conv_w', 'v_gdn_a_log', 'v_gdn_dt_bias', 'v_gdn_norm_w', 'v_ssd_conv_w', 'v_ssd_conv_b', 'v_ssd_dt_bias', 'v_ssd_a_log', 'v_ssd_d', 'v_ssd_norm_w', 'v_swa_sinks', 'v_w_proj_gdn', 'v_w_proj_ssd', 'v_w_proj_swa', 'v_w_out', 'v_norm2_w', 'v_w_up', 'v_w_down', 'v_final_norm_w']
TWIN_OUTPUTS = ['loss', 'grad_x', 'grad_meta_tokens', 'grad_norm1_w', 'grad_w_in', 'grad_gdn_conv_w', 'grad_gdn_a_log', 'grad_gdn_dt_bias', 'grad_gdn_norm_w', 'grad_ssd_conv_w', 'grad_ssd_conv_b', 'grad_ssd_dt_bias', 'grad_ssd_a_log', 'grad_ssd_d', 'grad_ssd_norm_w', 'grad_swa_sinks', 'grad_w_proj_gdn', 'grad_w_proj_ssd', 'grad_w_proj_swa', 'grad_w_out', 'grad_norm2_w', 'grad_w_up', 'grad_w_down', 'grad_final_norm_w', 'delta_meta_tokens', 'delta_norm1_w', 'delta_w_in', 'delta_gdn_conv_w', 'delta_gdn_a_log', 'delta_gdn_dt_bias', 'delta_gdn_norm_w', 'delta_ssd_conv_w', 'delta_ssd_conv_b', 'delta_ssd_dt_bias', 'delta_ssd_a_log', 'delta_ssd_d', 'delta_ssd_norm_w', 'delta_swa_sinks', 'delta_w_proj_gdn', 'delta_w_proj_ssd', 'delta_w_proj_swa', 'delta_w_out', 'delta_norm2_w', 'delta_w_up', 'delta_w_down', 'delta_final_norm_w', 'new_m_meta_tokens', 'new_m_norm1_w', 'new_m_w_in', 'new_m_gdn_conv_w', 'new_m_gdn_a_log', 'new_m_gdn_dt_bias', 'new_m_gdn_norm_w', 'new_m_ssd_conv_w', 'new_m_ssd_conv_b', 'new_m_ssd_dt_bias', 'new_m_ssd_a_log', 'new_m_ssd_d', 'new_m_ssd_norm_w', 'new_m_swa_sinks', 'new_m_w_proj_gdn', 'new_m_w_proj_ssd', 'new_m_w_proj_swa', 'new_m_w_out', 'new_m_norm2_w', 'new_m_w_up', 'new_m_w_down', 'new_m_final_norm_w', 'new_v_meta_tokens', 'new_v_norm1_w', 'new_v_w_in', 'new_v_gdn_conv_w', 'new_v_gdn_a_log', 'new_v_gdn_dt_bias', 'new_v_gdn_norm_w', 'new_v_ssd_conv_w', 'new_v_ssd_conv_b', 'new_v_ssd_dt_bias', 'new_v_ssd_a_log', 'new_v_ssd_d', 'new_v_ssd_norm_w', 'new_v_swa_sinks', 'new_v_w_proj_gdn', 'new_v_w_proj_ssd', 'new_v_w_proj_swa', 'new_v_w_out', 'new_v_norm2_w', 'new_v_w_up', 'new_v_w_down', 'new_v_final_norm_w']
TWIN_LEAF_KINDS = {'loss': 'loss', 'grad_x': 'grad_x', 'grad_meta_tokens': 'grad_w', 'grad_norm1_w': 'grad_w', 'grad_w_in': 'grad_w', 'grad_gdn_conv_w': 'grad_w', 'grad_gdn_a_log': 'grad_w', 'grad_gdn_dt_bias': 'grad_w', 'grad_gdn_norm_w': 'grad_w', 'grad_ssd_conv_w': 'grad_w', 'grad_ssd_conv_b': 'grad_w', 'grad_ssd_dt_bias': 'grad_w', 'grad_ssd_a_log': 'grad_w', 'grad_ssd_d': 'grad_w', 'grad_ssd_norm_w': 'grad_w', 'grad_swa_sinks': 'grad_w', 'grad_w_proj_gdn': 'grad_w', 'grad_w_proj_ssd': 'grad_w', 'grad_w_proj_swa': 'grad_w', 'grad_w_out': 'grad_w', 'grad_norm2_w': 'grad_w', 'grad_w_up': 'grad_w', 'grad_w_down': 'grad_w', 'grad_final_norm_w': 'grad_w', 'delta_meta_tokens': 'delta_w', 'delta_norm1_w': 'delta_w', 'delta_w_in': 'delta_w', 'delta_gdn_conv_w': 'delta_w', 'delta_gdn_a_log': 'delta_w', 'delta_gdn_dt_bias': 'delta_w', 'delta_gdn_norm_w': 'delta_w', 'delta_ssd_conv_w': 'delta_w', 'delta_ssd_conv_b': 'delta_w', 'delta_ssd_dt_bias': 'delta_w', 'delta_ssd_a_log': 'delta_w', 'delta_ssd_d': 'delta_w', 'delta_ssd_norm_w': 'delta_w', 'delta_swa_sinks': 'delta_w', 'delta_w_proj_gdn': 'delta_w', 'delta_w_proj_ssd': 'delta_w', 'delta_w_proj_swa': 'delta_w', 'delta_w_out': 'delta_w', 'delta_norm2_w': 'delta_w', 'delta_w_up': 'delta_w', 'delta_w_down': 'delta_w', 'delta_final_norm_w': 'delta_w', 'new_m_meta_tokens': 'new_m', 'new_m_norm1_w': 'new_m', 'new_m_w_in': 'new_m', 'new_m_gdn_conv_w': 'new_m', 'new_m_gdn_a_log': 'new_m', 'new_m_gdn_dt_bias': 'new_m', 'new_m_gdn_norm_w': 'new_m', 'new_m_ssd_conv_w': 'new_m', 'new_m_ssd_conv_b': 'new_m', 'new_m_ssd_dt_bias': 'new_m', 'new_m_ssd_a_log': 'new_m', 'new_m_ssd_d': 'new_m', 'new_m_ssd_norm_w': 'new_m', 'new_m_swa_sinks': 'new_m', 'new_m_w_proj_gdn': 'new_m', 'new_m_w_proj_ssd': 'new_m', 'new_m_w_proj_swa': 'new_m', 'new_m_w_out': 'new_m', 'new_m_norm2_w': 'new_m', 'new_m_w_up': 'new_m', 'new_m_w_down': 'new_m', 'new_m_final_norm_w': 'new_m', 'new_v_meta_tokens': 'new_v', 'new_v_norm1_w': 'new_v', 'new_v_w_in': 'new_v', 'new_v_gdn_conv_w': 'new_v', 'new_v_gdn_a_log': 'new_v', 'new_v_gdn_dt_bias': 'new_v', 'new_v_gdn_norm_w': 'new_v', 'new_v_ssd_conv_w': 'new_v', 'new_v_ssd_conv_b': 'new_v', 'new_v_ssd_dt_bias': 'new_v', 'new_v_ssd_a_log': 'new_v', 'new_v_ssd_d': 'new_v', 'new_v_ssd_norm_w': 'new_v', 'new_v_swa_sinks': 'new_v', 'new_v_w_proj_gdn': 'new_v', 'new_v_w_proj_ssd': 'new_v', 'new_v_w_proj_swa': 'new_v', 'new_v_w_out': 'new_v', 'new_v_norm2_w': 'new_v', 'new_v_w_up': 'new_v', 'new_v_w_down': 'new_v', 'new_v_final_norm_w': 'new_v'}


def _forward(args):
    return _fwd_reference(*[args[k] for k in FWD_PARAMS])


def _output_shape():
    out = _jax.eval_shape(lambda: _forward(_fwd_setup_inputs(0)))
    return out.shape, out.dtype

N_MICROBATCH = 1
ADAM_LR = 0.001
ADAM_B1 = 0.9
ADAM_B2 = 0.999
ADAM_EPS = 1e-08
ADAM_WD = 0.01
ADAM_STEP = 10
PER_EXAMPLE_BATCH_AXIS = {'x': 0, 'loss_target': 0}
SHARED_INPUTS = []
_WEIGHT_DTYPES = {'meta_tokens': _jnp.float32, 'norm1_w': _jnp.float32, 'w_in': _jnp.float32, 'gdn_conv_w': _jnp.float32, 'gdn_a_log': _jnp.float32, 'gdn_dt_bias': _jnp.float32, 'gdn_norm_w': _jnp.float32, 'ssd_conv_w': _jnp.float32, 'ssd_conv_b': _jnp.float32, 'ssd_dt_bias': _jnp.float32, 'ssd_a_log': _jnp.float32, 'ssd_d': _jnp.float32, 'ssd_norm_w': _jnp.float32, 'swa_sinks': _jnp.float32, 'w_proj_gdn': _jnp.float32, 'w_proj_ssd': _jnp.float32, 'w_proj_swa': _jnp.float32, 'w_out': _jnp.float32, 'norm2_w': _jnp.float32, 'w_up': _jnp.float32, 'w_down': _jnp.float32, 'final_norm_w': _jnp.float32}
MOMENT_SCALE = {'meta_tokens': 5.827359e-03, 'norm1_w': 1.506076e-01, 'w_in': 4.493732e-02, 'gdn_conv_w': 3.680570e-02, 'gdn_a_log': 1.756579e-01, 'gdn_dt_bias': 1.634124e-01, 'gdn_norm_w': 1.404351e-01, 'ssd_conv_w': 6.175956e-02, 'ssd_conv_b': 9.026831e-02, 'ssd_dt_bias': 1.708552e-01, 'ssd_a_log': 1.212700e-01, 'ssd_d': 3.684731e-01, 'ssd_norm_w': 8.439484e-02, 'swa_sinks': 3.212931e-03, 'w_proj_gdn': 4.722877e-02, 'w_proj_ssd': 8.330702e-02, 'w_proj_swa': 2.092294e-02, 'w_out': 9.762651e-02, 'norm2_w': 1.453542e-01, 'w_up': 7.233321e-02, 'w_down': 1.421360e-01, 'final_norm_w': 3.245536e+01}


def _to_microbatches(a, axis):
    t = _jnp.moveaxis(a, axis, 0)
    t = t.reshape((N_MICROBATCH, t.shape[0] // N_MICROBATCH) + t.shape[1:])
    return _jnp.moveaxis(t, 1, axis + 1)


def setup_inputs(seed: int = 0) -> dict:
    inp = _fwd_setup_inputs(seed)
    key = _jax.random.fold_in(_jax.random.key(seed), 7919)
    shape, _ = _output_shape()
    out = dict(inp)
    out["loss_target"] = _jax.random.normal(_jax.random.fold_in(key, 0), shape, _jnp.float32)
    for i, name in enumerate(TWIN_WEIGHTS):
        w = inp[name].astype(_jnp.float32)
        if MOMENT_SCALE is None:
            s = _jnp.sqrt(_jnp.mean(_jnp.square(w)) + 1e-30)
        else:
            s = MOMENT_SCALE[name]
        km, kv = _jax.random.split(_jax.random.fold_in(key, i + 1))
        out[name] = w
        out["m_" + name] = s * _jax.random.normal(km, w.shape, _jnp.float32)
        out["v_" + name] = (s * s) * _jax.random.uniform(kv, w.shape, _jnp.float32, 0.5, 1.5)
    if N_MICROBATCH > 1:
        for name, axis in PER_EXAMPLE_BATCH_AXIS.items():
            out[name] = _to_microbatches(out[name], axis)
    return {'x': out['x'], 'meta_tokens': out['meta_tokens'], 'norm1_w': out['norm1_w'], 'w_in': out['w_in'], 'gdn_conv_w': out['gdn_conv_w'], 'gdn_a_log': out['gdn_a_log'], 'gdn_dt_bias': out['gdn_dt_bias'], 'gdn_norm_w': out['gdn_norm_w'], 'ssd_conv_w': out['ssd_conv_w'], 'ssd_conv_b': out['ssd_conv_b'], 'ssd_dt_bias': out['ssd_dt_bias'], 'ssd_a_log': out['ssd_a_log'], 'ssd_d': out['ssd_d'], 'ssd_norm_w': out['ssd_norm_w'], 'swa_sinks': out['swa_sinks'], 'w_proj_gdn': out['w_proj_gdn'], 'w_proj_ssd': out['w_proj_ssd'], 'w_proj_swa': out['w_proj_swa'], 'w_out': out['w_out'], 'norm2_w': out['norm2_w'], 'w_up': out['w_up'], 'w_down': out['w_down'], 'final_norm_w': out['final_norm_w'], 'loss_target': out['loss_target'], 'm_meta_tokens': out['m_meta_tokens'], 'm_norm1_w': out['m_norm1_w'], 'm_w_in': out['m_w_in'], 'm_gdn_conv_w': out['m_gdn_conv_w'], 'm_gdn_a_log': out['m_gdn_a_log'], 'm_gdn_dt_bias': out['m_gdn_dt_bias'], 'm_gdn_norm_w': out['m_gdn_norm_w'], 'm_ssd_conv_w': out['m_ssd_conv_w'], 'm_ssd_conv_b': out['m_ssd_conv_b'], 'm_ssd_dt_bias': out['m_ssd_dt_bias'], 'm_ssd_a_log': out['m_ssd_a_log'], 'm_ssd_d': out['m_ssd_d'], 'm_ssd_norm_w': out['m_ssd_norm_w'], 'm_swa_sinks': out['m_swa_sinks'], 'm_w_proj_gdn': out['m_w_proj_gdn'], 'm_w_proj_ssd': out['m_w_proj_ssd'], 'm_w_proj_swa': out['m_w_proj_swa'], 'm_w_out': out['m_w_out'], 'm_norm2_w': out['m_norm2_w'], 'm_w_up': out['m_w_up'], 'm_w_down': out['m_w_down'], 'm_final_norm_w': out['m_final_norm_w'], 'v_meta_tokens': out['v_meta_tokens'], 'v_norm1_w': out['v_norm1_w'], 'v_w_in': out['v_w_in'], 'v_gdn_conv_w': out['v_gdn_conv_w'], 'v_gdn_a_log': out['v_gdn_a_log'], 'v_gdn_dt_bias': out['v_gdn_dt_bias'], 'v_gdn_norm_w': out['v_gdn_norm_w'], 'v_ssd_conv_w': out['v_ssd_conv_w'], 'v_ssd_conv_b': out['v_ssd_conv_b'], 'v_ssd_dt_bias': out['v_ssd_dt_bias'], 'v_ssd_a_log': out['v_ssd_a_log'], 'v_ssd_d': out['v_ssd_d'], 'v_ssd_norm_w': out['v_ssd_norm_w'], 'v_swa_sinks': out['v_swa_sinks'], 'v_w_proj_gdn': out['v_w_proj_gdn'], 'v_w_proj_ssd': out['v_w_proj_ssd'], 'v_w_proj_swa': out['v_w_proj_swa'], 'v_w_out': out['v_w_out'], 'v_norm2_w': out['v_norm2_w'], 'v_w_up': out['v_w_up'], 'v_w_down': out['v_w_down'], 'v_final_norm_w': out['v_final_norm_w']}


def _loss(weights, diff, rest, loss_target):
    with _jax.named_scope("forward"):
        args = {**rest, TWIN_DIFF_INPUT: diff, **{k: w.astype(_WEIGHT_DTYPES[k]) for k, w in weights.items()}}
        y = _forward(args)
    with _jax.named_scope("loss_head"):
        err = _jnp.square(y.astype(_jnp.float32) - loss_target)
        return 0.5 * _jnp.sum(_jnp.mean(err, axis=-1)) if err.ndim else 0.5 * err


def _adamw(w, g, m, v):
    m = ADAM_B1 * m + (1.0 - ADAM_B1) * g
    v = ADAM_B2 * v + (1.0 - ADAM_B2) * _jnp.square(g)
    m_hat = m / (1.0 - ADAM_B1 ** ADAM_STEP)
    v_hat = v / (1.0 - ADAM_B2 ** ADAM_STEP)
    delta = -ADAM_LR * (m_hat / (_jnp.sqrt(v_hat) + ADAM_EPS) + ADAM_WD * w)
    return delta, m, v


def reference(x, meta_tokens, norm1_w, w_in, gdn_conv_w, gdn_a_log, gdn_dt_bias, gdn_norm_w, ssd_conv_w, ssd_conv_b, ssd_dt_bias, ssd_a_log, ssd_d, ssd_norm_w, swa_sinks, w_proj_gdn, w_proj_ssd, w_proj_swa, w_out, norm2_w, w_up, w_down, final_norm_w, loss_target, m_meta_tokens, m_norm1_w, m_w_in, m_gdn_conv_w, m_gdn_a_log, m_gdn_dt_bias, m_gdn_norm_w, m_ssd_conv_w, m_ssd_conv_b, m_ssd_dt_bias, m_ssd_a_log, m_ssd_d, m_ssd_norm_w, m_swa_sinks, m_w_proj_gdn, m_w_proj_ssd, m_w_proj_swa, m_w_out, m_norm2_w, m_w_up, m_w_down, m_final_norm_w, v_meta_tokens, v_norm1_w, v_w_in, v_gdn_conv_w, v_gdn_a_log, v_gdn_dt_bias, v_gdn_norm_w, v_ssd_conv_w, v_ssd_conv_b, v_ssd_dt_bias, v_ssd_a_log, v_ssd_d, v_ssd_norm_w, v_swa_sinks, v_w_proj_gdn, v_w_proj_ssd, v_w_proj_swa, v_w_out, v_norm2_w, v_w_up, v_w_down, v_final_norm_w):
    given = dict(x=x, meta_tokens=meta_tokens, norm1_w=norm1_w, w_in=w_in, gdn_conv_w=gdn_conv_w, gdn_a_log=gdn_a_log, gdn_dt_bias=gdn_dt_bias, gdn_norm_w=gdn_norm_w, ssd_conv_w=ssd_conv_w, ssd_conv_b=ssd_conv_b, ssd_dt_bias=ssd_dt_bias, ssd_a_log=ssd_a_log, ssd_d=ssd_d, ssd_norm_w=ssd_norm_w, swa_sinks=swa_sinks, w_proj_gdn=w_proj_gdn, w_proj_ssd=w_proj_ssd, w_proj_swa=w_proj_swa, w_out=w_out, norm2_w=norm2_w, w_up=w_up, w_down=w_down, final_norm_w=final_norm_w, loss_target=loss_target, m_meta_tokens=m_meta_tokens, m_norm1_w=m_norm1_w, m_w_in=m_w_in, m_gdn_conv_w=m_gdn_conv_w, m_gdn_a_log=m_gdn_a_log, m_gdn_dt_bias=m_gdn_dt_bias, m_gdn_norm_w=m_gdn_norm_w, m_ssd_conv_w=m_ssd_conv_w, m_ssd_conv_b=m_ssd_conv_b, m_ssd_dt_bias=m_ssd_dt_bias, m_ssd_a_log=m_ssd_a_log, m_ssd_d=m_ssd_d, m_ssd_norm_w=m_ssd_norm_w, m_swa_sinks=m_swa_sinks, m_w_proj_gdn=m_w_proj_gdn, m_w_proj_ssd=m_w_proj_ssd, m_w_proj_swa=m_w_proj_swa, m_w_out=m_w_out, m_norm2_w=m_norm2_w, m_w_up=m_w_up, m_w_down=m_w_down, m_final_norm_w=m_final_norm_w, v_meta_tokens=v_meta_tokens, v_norm1_w=v_norm1_w, v_w_in=v_w_in, v_gdn_conv_w=v_gdn_conv_w, v_gdn_a_log=v_gdn_a_log, v_gdn_dt_bias=v_gdn_dt_bias, v_gdn_norm_w=v_gdn_norm_w, v_ssd_conv_w=v_ssd_conv_w, v_ssd_conv_b=v_ssd_conv_b, v_ssd_dt_bias=v_ssd_dt_bias, v_ssd_a_log=v_ssd_a_log, v_ssd_d=v_ssd_d, v_ssd_norm_w=v_ssd_norm_w, v_swa_sinks=v_swa_sinks, v_w_proj_gdn=v_w_proj_gdn, v_w_proj_ssd=v_w_proj_ssd, v_w_proj_swa=v_w_proj_swa, v_w_out=v_w_out, v_norm2_w=v_norm2_w, v_w_up=v_w_up, v_w_down=v_w_down, v_final_norm_w=v_final_norm_w)
    weights = {n: given[n] for n in TWIN_WEIGHTS}
    shared = {n: given[n] for n in SHARED_INPUTS}
    per_example = {n: given[n] for n in ['x']}
    grad_fn = _jax.value_and_grad(_loss, argnums=(0, 1))

    def one_microbatch(ex, loss_target):
        ex = dict(ex)
        diff = ex.pop(TWIN_DIFF_INPUT)
        return grad_fn(weights, diff, {**shared, **ex}, loss_target)

    if N_MICROBATCH == 1:
        loss, (grad_w, grad_x) = one_microbatch(per_example, given["loss_target"])
    else:
        def body(carry, xs):
            loss_sum, grad_sum = carry
            l_k, (gw_k, gx_k) = one_microbatch(xs[0], xs[1])
            with _jax.named_scope("update"):
                return (loss_sum + l_k, _jax.tree.map(_jnp.add, grad_sum, gw_k)), gx_k

        init = (_jnp.zeros((), _jnp.float32), _jax.tree.map(_jnp.zeros_like, weights))
        (loss, grad_w), grad_x = _jax.lax.scan(body, init, (per_example, given["loss_target"]))
    with _jax.named_scope("update"):
        delta_w, new_m, new_v = {}, {}, {}
        for n in TWIN_WEIGHTS:
            delta_w[n], new_m[n], new_v[n] = _adamw(weights[n], grad_w[n], given["m_" + n], given["v_" + n])
    return (loss, grad_x, *[grad_w[n] for n in TWIN_WEIGHTS], *[delta_w[n] for n in TWIN_WEIGHTS],
            *[new_m[n] for n in TWIN_WEIGHTS], *[new_v[n] for n in TWIN_WEIGHTS])
```

```python
import functools
import math

import jax
import jax.numpy as jnp
from jax import lax
from jax.experimental import pallas as pl
from jax.experimental.pallas import tpu as pltpu

F32 = jnp.float32
BF16 = jnp.bfloat16
HI = lax.Precision.HIGHEST
NEG = -1e30

D_MODEL = 1024
N_META = 16
BLK = 128
NPAD = BLK - N_META
RMS_EPS = 1e-6
L2_EPS = 1e-6
CONV_K = 4

GDN_H, GDN_D, GDN_C = 8, 128, 64
SSD_H, SSD_P, SSD_G, SSD_N = 16, 64, 4, 128
SSD_HPG = SSD_H // SSD_G
SWA_QH, SWA_KVH, SWA_D = 16, 4, 64
SWA_REP = SWA_QH // SWA_KVH
D_FF = 4 * D_MODEL

N_DEV = 8
MESH = pl.DeviceIdType.MESH

ADAM_LR, ADAM_B1, ADAM_B2, ADAM_EPS, ADAM_WD, ADAM_STEP = 0.001, 0.9, 0.999, 1e-08, 0.01, 10

VMEM_LIMIT = 56 * 1024 * 1024


def _cp(sem=None):
    return pltpu.CompilerParams(dimension_semantics=sem, vmem_limit_bytes=VMEM_LIMIT)


def _dot(a, b, ca, cb, prec=HI):
    return lax.dot_general(a, b, (((ca,), (cb,)), ((), ())), precision=prec, preferred_element_type=F32)


def _nn(a, b, prec=HI):
    return _dot(a, b, 1, 0, prec)


def _nt(a, b, prec=HI):
    return _dot(a, b, 1, 1, prec)


def _tn(a, b, prec=HI):
    return _dot(a, b, 0, 0, prec)


def _iota2(n, m, axis):
    return lax.broadcasted_iota(jnp.int32, (n, m), axis)


def _silu(x):
    return x * jax.nn.sigmoid(x)


def _softplus(x):
    return jnp.maximum(x, 0.0) + jnp.log(1.0 + jnp.exp(-jnp.abs(x)))


def _row_of(col):
    n = col.shape[0]
    eye = (_iota2(n, n, 0) == _iota2(n, n, 1)).astype(F32)
    return _nn(jnp.ones((n, n), F32), eye * col)


def _cumsum_col(col):
    n = col.shape[0]
    tril = (_iota2(n, n, 0) >= _iota2(n, n, 1)).astype(F32)
    return _nn(tril, col)


def _tri_inv(a):
    n = a.shape[0]
    r, c = _iota2(n, n, 0), _iota2(n, n, 1)
    eye = (r == c).astype(F32)
    blk = jnp.right_shift(r, 4) == jnp.right_shift(c, 4)
    d = jnp.where(blk, a, 0.0)
    off = a - d
    d2 = _nn(d, d)
    d4 = _nn(d2, d2)
    d8 = _nn(d4, d4)
    td = _nn(_nn(_nn(eye - d, eye + d2), eye + d4), eye + d8)
    m = _nn(td, off)
    m2 = _nn(m, m)
    return _nn(_nn(eye - m, eye + m2), td)


@jax.custom_vjp
def _tri_solve(a, a_t, rhs):
    return _nn(_tri_inv(a), rhs)


def _tri_solve_fwd(a, a_t, rhs):
    sol = _nn(_tri_inv(a), rhs)
    return sol, (a_t, sol)


def _tri_solve_bwd(res, dsol):
    a_t, sol = res
    drhs = _nn(_tri_inv(a_t), dsol)
    return -_nt(drhs, sol), jnp.zeros_like(a_t), drhs


_tri_solve.defvjp(_tri_solve_fwd, _tri_solve_bwd)


def _gdn_chunk(qa, ka, va, gate, a_raw, b_raw, s, a_log, dt_bias, norm_w, valid):
    c = qa.shape[0]
    q = qa * lax.rsqrt(jnp.sum(qa * qa, axis=-1, keepdims=True) + L2_EPS) * (GDN_D ** -0.5)
    k = ka * lax.rsqrt(jnp.sum(ka * ka, axis=-1, keepdims=True) + L2_EPS)
    beta = jax.nn.sigmoid(b_raw)
    g = -jnp.exp(a_log) * _softplus(a_raw + dt_bias) * valid
    gam = _cumsum_col(g)
    gam_row = _row_of(gam)
    r, cc = _iota2(c, c, 0), _iota2(c, c, 1)
    decay = jnp.exp(jnp.where(r >= cc, gam - gam_row, NEG))
    kb = k * beta
    a = jnp.where(r > cc, _nt(kb, k) * decay, 0.0)
    a_t = lax.stop_gradient(jnp.where(cc > r, _nt(k, kb) * jnp.exp(jnp.where(cc >= r, gam_row - gam, NEG)), 0.0))
    egam = jnp.exp(gam)
    sol = _tri_solve(a, a_t, jnp.concatenate([va * beta, kb * egam], axis=1))
    u = sol[:, :GDN_D]
    w = sol[:, GDN_D:]
    attn = _nt(q, k) * decay
    g_last = jnp.sum(g, axis=0, keepdims=True)
    k_tail = k * jnp.exp(g_last - gam)
    v_new = u - _nn(w, s)
    o = _nn(q * egam, s) + _nn(attn, v_new)
    s_new = s * jnp.exp(g_last) + _tn(k_tail, v_new)
    y = o * lax.rsqrt(jnp.mean(o * o, axis=-1, keepdims=True) + RMS_EPS) * norm_w * _silu(gate)
    return y, s_new


def _valid_col(row0, n):
    return (row0 + _iota2(n, 1, 0) >= NPAD).astype(F32)


def _gdn_specs(nc, rev):
    ci = (lambda i: nc - 1 - i) if rev else (lambda i: i)
    tile = lambda off: pl.BlockSpec((GDN_C, GDN_D), lambda h, i: (ci(i), h + off))
    col = pl.BlockSpec((1, 1, GDN_C, 1), lambda h, i: (h, ci(i), 0, 0))
    scal = pl.BlockSpec((1, 1, 1), lambda h, i: (h, 0, 0))
    nw = pl.BlockSpec((1, GDN_D), lambda h, i: (0, 0))
    st = pl.BlockSpec((1, 1, GDN_D, GDN_D), lambda h, i: (h, ci(i), 0, 0))
    return tile, col, scal, nw, st


def _gdn_fwd_call(q, k, v, gate, a_col, b_col, a_log, dt_bias, norm_w):
    seq = q.shape[0]
    nc = seq // GDN_C
    tile, col, scal, nw, st = _gdn_specs(nc, False)

    def body(q_ref, k_ref, v_ref, g_ref, a_ref, b_ref, al_ref, dt_ref, nw_ref, y_ref, st_ref, s_scr):
        i = pl.program_id(1)

        @pl.when(i == 0)
        def _():
            s_scr[...] = jnp.zeros_like(s_scr)

        s = s_scr[...]
        st_ref[0, 0] = s
        y, s_new = _gdn_chunk(q_ref[...], k_ref[...], v_ref[...], g_ref[...], a_ref[0, 0], b_ref[0, 0], s,
                              al_ref[0], dt_ref[0], nw_ref[...], _valid_col(i * GDN_C, GDN_C))
        y_ref[...] = y
        s_scr[...] = s_new

    return pl.pallas_call(
        body, name="gdn_fwd", grid=(GDN_H, nc),
        in_specs=[tile(0), tile(0), tile(0), tile(0), col, col, scal, scal, nw],
        out_specs=[tile(0), st],
        out_shape=[jax.ShapeDtypeStruct((seq, GDN_H * GDN_D), F32),
                   jax.ShapeDtypeStruct((GDN_H, nc, GDN_D, GDN_D), F32)],
        scratch_shapes=[pltpu.VMEM((GDN_D, GDN_D), F32)],
        compiler_params=_cp(("parallel", "arbitrary")),
    )(q, k, v, gate, a_col, b_col, a_log, dt_bias, norm_w)


def _gdn_bwd_call(q, k, v, gate, a_col, b_col, a_log, dt_bias, norm_w, states, dy):
    seq = q.shape[0]
    nc = seq // GDN_C
    tile, col, scal, nw, st = _gdn_specs(nc, True)
    nwh = pl.BlockSpec((1, 1, GDN_D), lambda h, i: (h, 0, 0))

    def body(q_ref, k_ref, v_ref, g_ref, a_ref, b_ref, al_ref, dt_ref, nw_ref, st_ref, dy_ref,
             dq_ref, dk_ref, dv_ref, dg_ref, da_ref, db_ref, dal_ref, ddt_ref, dnw_ref, ds_scr):
        i = pl.program_id(1)

        @pl.when(i == 0)
        def _():
            ds_scr[...] = jnp.zeros_like(ds_scr)
            dal_ref[...] = jnp.zeros_like(dal_ref)
            ddt_ref[...] = jnp.zeros_like(ddt_ref)
            dnw_ref[...] = jnp.zeros_like(dnw_ref)

        valid = _valid_col((nc - 1 - i) * GDN_C, GDN_C)
        fn = functools.partial(_gdn_chunk, valid=valid)
        _, vjp = jax.vjp(fn, q_ref[...], k_ref[...], v_ref[...], g_ref[...], a_ref[0, 0], b_ref[0, 0],
                         st_ref[0, 0], al_ref[0], dt_ref[0], nw_ref[...])
        dq, dk, dv, dg, da, db, ds, dal, ddt, dnw = vjp((dy_ref[...], ds_scr[...]))
        dq_ref[...] = dq
        dk_ref[...] = dk
        dv_ref[...] = dv
        dg_ref[...] = dg
        da_ref[0, 0] = da
        db_ref[0, 0] = db
        ds_scr[...] = ds
        dal_ref[0] += dal
        ddt_ref[0] += ddt
        dnw_ref[0] += dnw

    big = jax.ShapeDtypeStruct((seq, GDN_H * GDN_D), F32)
    return pl.pallas_call(
        body, name="gdn_bwd", grid=(GDN_H, nc),
        in_specs=[tile(0), tile(0), tile(0), tile(0), col, col, scal, scal, nw, st, tile(0)],
        out_specs=[tile(0), tile(0), tile(0), tile(0), col, col, scal, scal, nwh],
        out_shape=[big, big, big, big,
                   jax.ShapeDtypeStruct(a_col.shape, F32), jax.ShapeDtypeStruct(b_col.shape, F32),
                   jax.ShapeDtypeStruct((GDN_H, 1, 1), F32), jax.ShapeDtypeStruct((GDN_H, 1, 1), F32),
                   jax.ShapeDtypeStruct((GDN_H, 1, GDN_D), F32)],
        scratch_shapes=[pltpu.VMEM((GDN_D, GDN_D), F32)],
        compiler_params=_cp(("parallel", "arbitrary")),
    )(q, k, v, gate, a_col, b_col, a_log, dt_bias, norm_w, states, dy)


@jax.custom_vjp
def gdn_core(q, k, v, gate, a_col, b_col, a_log, dt_bias, norm_w):
    return _gdn_fwd_call(q, k, v, gate, a_col, b_col, a_log, dt_bias, norm_w)[0]


def _gdn_core_fwd(q, k, v, gate, a_col, b_col, a_log, dt_bias, norm_w):
    y, states = _gdn_fwd_call(q, k, v, gate, a_col, b_col, a_log, dt_bias, norm_w)
    return y, (q, k, v, gate, a_col, b_col, a_log, dt_bias, norm_w, states)


def _gdn_core_bwd(res, dy):
    dq, dk, dv, dg, da, db, dal, ddt, dnw = _gdn_bwd_call(*res, dy)
    return dq, dk, dv, dg, da, db, dal, ddt, jnp.sum(dnw, axis=0)


gdn_core.defvjp(_gdn_core_fwd, _gdn_core_bwd)


def _ssd_chunk(xs, z, bm, cm, dt_raw, h, dt_bias, a_log, d_skip, norm_w, valid):
    c = bm.shape[0]
    r, cc = _iota2(c, c, 0), _iota2(c, c, 1)
    bm = bm * valid
    cm = cm * valid
    cb = _nt(cm, bm)
    ys, hs = [], []
    for i in range(SSD_HPG):
        dtp = _softplus(dt_raw[i] + dt_bias[i])
        x = xs[i] * valid
        adt = -jnp.exp(a_log[i]) * dtp * valid
        xdt = x * dtp
        acum = _cumsum_col(adt)
        lmat = jnp.exp(jnp.where(r >= cc, acum - _row_of(acum), NEG))
        a_last = jnp.sum(adt, axis=0, keepdims=True)
        y = _nn(cb * lmat, xdt) + _nt(cm * jnp.exp(acum), h[i]) + d_skip[i] * x
        hs.append(h[i] * jnp.exp(a_last) + _tn(xdt * jnp.exp(a_last - acum), bm))
        ys.append(y * _silu(z[i]))
    ss = sum(jnp.sum(y * y, axis=-1, keepdims=True) for y in ys)
    rstd = lax.rsqrt(ss / (SSD_HPG * SSD_P) + RMS_EPS)
    return jnp.stack([ys[i] * rstd * norm_w[i] for i in range(SSD_HPG)]), jnp.stack(hs)


def _ssd_specs(nc, rev):
    ci = (lambda i: nc - 1 - i) if rev else (lambda i: i)
    hp = SSD_HPG
    head = pl.BlockSpec((hp, BLK, SSD_P), lambda g, i: (g, ci(i), 0))
    grp = pl.BlockSpec((BLK, SSD_N), lambda g, i: (ci(i), g))
    col = pl.BlockSpec((hp, 1, BLK, 1), lambda g, i: (g, ci(i), 0, 0))
    scal = pl.BlockSpec((hp, 1, 1), lambda g, i: (g, 0, 0))
    nw = pl.BlockSpec((hp, 1, SSD_P), lambda g, i: (g, 0, 0))
    st = pl.BlockSpec((hp, 1, SSD_P, SSD_N), lambda g, i: (g, ci(i), 0, 0))
    return head, grp, col, scal, nw, st


def _ssd_fwd_call(xs, z, bm, cm, dt_col, dt_bias, a_log, d_skip, norm_w):
    seq = xs.shape[1]
    nc = seq // BLK
    head, grp, col, scal, nw, st = _ssd_specs(nc, False)

    def body(x_ref, z_ref, b_ref, c_ref, dt_ref, db_ref, al_ref, ds_ref, nw_ref, y_ref, st_ref, h_scr):
        i = pl.program_id(1)

        @pl.when(i == 0)
        def _():
            h_scr[...] = jnp.zeros_like(h_scr)

        h = h_scr[...]
        st_ref[:, 0] = h
        y, h_new = _ssd_chunk(x_ref[...], z_ref[...], b_ref[...], c_ref[...], dt_ref[:, 0], h, db_ref[...],
                              al_ref[...], ds_ref[...], nw_ref[...], _valid_col(i * BLK, BLK))
        y_ref[...] = y
        h_scr[...] = h_new

    return pl.pallas_call(
        body, name="ssd_fwd", grid=(SSD_G, nc),
        in_specs=[head, head, grp, grp, col, scal, scal, scal, nw],
        out_specs=[head, st],
        out_shape=[jax.ShapeDtypeStruct((SSD_H, seq, SSD_P), F32),
                   jax.ShapeDtypeStruct((SSD_H, nc, SSD_P, SSD_N), F32)],
        scratch_shapes=[pltpu.VMEM((SSD_HPG, SSD_P, SSD_N), F32)],
        compiler_params=_cp(("parallel", "arbitrary")),
    )(xs, z, bm, cm, dt_col, dt_bias, a_log, d_skip, norm_w)


def _ssd_bwd_call(xs, z, bm, cm, dt_col, dt_bias, a_log, d_skip, norm_w, states, dy):
    seq = xs.shape[1]
    nc = seq // BLK
    head, grp, col, scal, nw, st = _ssd_specs(nc, True)

    def body(x_ref, z_ref, b_ref, c_ref, dt_ref, db_ref, al_ref, ds_ref, nw_ref, st_ref, dy_ref,
             dx_ref, dz_ref, dbm_ref, dcm_ref, ddt_ref, ddb_ref, dal_ref, dds_ref, dnw_ref, dh_scr):
        i = pl.program_id(1)

        @pl.when(i == 0)
        def _():
            dh_scr[...] = jnp.zeros_like(dh_scr)
            ddb_ref[...] = jnp.zeros_like(ddb_ref)
            dal_ref[...] = jnp.zeros_like(dal_ref)
            dds_ref[...] = jnp.zeros_like(dds_ref)
            dnw_ref[...] = jnp.zeros_like(dnw_ref)

        fn = functools.partial(_ssd_chunk, valid=_valid_col((nc - 1 - i) * BLK, BLK))
        _, vjp = jax.vjp(fn, x_ref[...], z_ref[...], b_ref[...], c_ref[...], dt_ref[:, 0], st_ref[:, 0],
                         db_ref[...], al_ref[...], ds_ref[...], nw_ref[...])
        dx, dz, dbm, dcm, ddt, dh, ddb, dal, dds, dnw = vjp((dy_ref[...], dh_scr[...]))
        dx_ref[...] = dx
        dz_ref[...] = dz
        dbm_ref[...] = dbm
        dcm_ref[...] = dcm
        ddt_ref[:, 0] = ddt
        dh_scr[...] = dh
        ddb_ref[...] += ddb
        dal_ref[...] += dal
        dds_ref[...] += dds
        dnw_ref[...] += dnw

    hshape = jax.ShapeDtypeStruct(xs.shape, F32)
    gshape = jax.ShapeDtypeStruct(bm.shape, F32)
    sshape = jax.ShapeDtypeStruct((SSD_H, 1, 1), F32)
    return pl.pallas_call(
        body, name="ssd_bwd", grid=(SSD_G, nc),
        in_specs=[head, head, grp, grp, col, scal, scal, scal, nw, st, head],
        out_specs=[head, head, grp, grp, col, scal, scal, scal, nw],
        out_shape=[hshape, hshape, gshape, gshape, jax.ShapeDtypeStruct(dt_col.shape, F32),
                   sshape, sshape, sshape, jax.ShapeDtypeStruct((SSD_H, 1, SSD_P), F32)],
        scratch_shapes=[pltpu.VMEM((SSD_HPG, SSD_P, SSD_N), F32)],
        compiler_params=_cp(("parallel", "arbitrary")),
    )(xs, z, bm, cm, dt_col, dt_bias, a_log, d_skip, norm_w, states, dy)


@jax.custom_vjp
def ssd_core(xs, z, bm, cm, dt_col, dt_bias, a_log, d_skip, norm_w):
    return _ssd_fwd_call(xs, z, bm, cm, dt_col, dt_bias, a_log, d_skip, norm_w)[0]


def _ssd_core_fwd(*args):
    y, states = _ssd_fwd_call(*args)
    return y, (*args, states)


def _ssd_core_bwd(res, dy):
    return tuple(_ssd_bwd_call(*res, dy))


ssd_core.defvjp(_ssd_core_fwd, _ssd_core_bwd)


def _swa_block(q, km, kp, kc, vm, vp, vc, sink, n):
    rows = SWA_REP * BLK
    qs = q.reshape(rows, SWA_D) * (SWA_D ** -0.5)
    s = _nt(qs, jnp.concatenate([km, kp, kc], axis=0))
    i = jnp.bitwise_and(_iota2(rows, 3 * BLK, 0), BLK - 1)
    col = _iota2(rows, 3 * BLK, 1)
    j = jnp.bitwise_and(col, BLK - 1)
    part = jnp.right_shift(col, 7)
    ok_m = (part == 0) & (j >= NPAD) & ((n >= 1) | (j <= i))
    ok_p = (part == 1) & (n >= 2) & (j > i)
    ok_c = (part == 2) & (n >= 1) & (j <= i)
    ok = ok_m | ok_p | ok_c
    s = jnp.where(ok, s, NEG)
    snk = jnp.concatenate([jnp.broadcast_to(sink[r], (BLK, 1)) for r in range(SWA_REP)], axis=0)
    m = lax.stop_gradient(jnp.maximum(jnp.max(s, axis=-1, keepdims=True), snk))
    e = jnp.exp(s - m)
    p = e / (jnp.sum(e, axis=-1, keepdims=True) + jnp.exp(snk - m))
    o = _nn(p, jnp.concatenate([vm, vp, vc], axis=0))
    return o.reshape(SWA_REP, BLK, SWA_D)


def _swa_specs(nb, rev):
    ci = (lambda i: nb - 1 - i) if rev else (lambda i: i)
    qsp = pl.BlockSpec((SWA_REP, BLK, SWA_D), lambda g, i: (g, ci(i), 0))
    cur = pl.BlockSpec((1, BLK, SWA_D), lambda g, i: (g, ci(i), 0))
    prev = pl.BlockSpec((1, BLK, SWA_D), lambda g, i: (g, jnp.maximum(ci(i) - 1, 0), 0))
    meta = pl.BlockSpec((1, BLK, SWA_D), lambda g, i: (g, 0, 0))
    scal = pl.BlockSpec((SWA_REP, 1, 1), lambda g, i: (g, 0, 0))
    return qsp, cur, prev, meta, scal


def _swa_fwd_call(q, k, v, sink):
    seq = q.shape[1]
    nb = seq // BLK
    qsp, cur, prev, meta, scal = _swa_specs(nb, False)

    def body(q_ref, km_ref, kp_ref, kc_ref, vm_ref, vp_ref, vc_ref, s_ref, o_ref):
        o_ref[...] = _swa_block(q_ref[...], km_ref[0], kp_ref[0], kc_ref[0], vm_ref[0], vp_ref[0], vc_ref[0],
                                s_ref[...], pl.program_id(1))

    return pl.pallas_call(
        body, name="swa_fwd", grid=(SWA_KVH, nb),
        in_specs=[qsp, meta, prev, cur, meta, prev, cur, scal],
        out_specs=qsp,
        out_shape=jax.ShapeDtypeStruct(q.shape, F32),
        compiler_params=_cp(("parallel", "arbitrary")),
    )(q, k, k, k, v, v, v, sink)


def _swa_bwd_call(q, k, v, sink, do):
    seq = q.shape[1]
    nb = seq // BLK
    qsp, cur, prev, meta, scal = _swa_specs(nb, True)

    def body(q_ref, km_ref, kp_ref, kc_ref, vm_ref, vp_ref, vc_ref, s_ref, do_ref,
             dq_ref, dk_ref, dv_ref, ds_ref, kp_scr, vp_scr, km_scr, vm_scr):
        i = pl.program_id(1)
        n = nb - 1 - i

        @pl.when(i == 0)
        def _():
            for scr in (kp_scr, vp_scr, km_scr, vm_scr):
                scr[...] = jnp.zeros_like(scr)
            ds_ref[...] = jnp.zeros_like(ds_ref)

        fn = functools.partial(_swa_block, n=n)
        _, vjp = jax.vjp(fn, q_ref[...], km_ref[0], kp_ref[0], kc_ref[0], vm_ref[0], vp_ref[0], vc_ref[0], s_ref[...])
        dq, dkm, dkp, dkc, dvm, dvp, dvc, dsk = vjp(do_ref[...])
        dq_ref[...] = dq
        ds_ref[...] += dsk
        km_scr[...] += dkm
        vm_scr[...] += dvm
        first = (n == 0).astype(F32)
        dk_ref[0] = dkc + kp_scr[...] + first * km_scr[...]
        dv_ref[0] = dvc + vp_scr[...] + first * vm_scr[...]
        kp_scr[...] = dkp
        vp_scr[...] = dvp

    kv = jax.ShapeDtypeStruct(k.shape, F32)
    return pl.pallas_call(
        body, name="swa_bwd", grid=(SWA_KVH, nb),
        in_specs=[qsp, meta, prev, cur, meta, prev, cur, scal, qsp],
        out_specs=[qsp, cur, cur, scal],
        out_shape=[jax.ShapeDtypeStruct(q.shape, F32), kv, kv, jax.ShapeDtypeStruct(sink.shape, F32)],
        scratch_shapes=[pltpu.VMEM((BLK, SWA_D), F32)] * 4,
        compiler_params=_cp(("parallel", "arbitrary")),
    )(q, k, k, k, v, v, v, sink, do)


@jax.custom_vjp
def swa_core(q, k, v, sink):
    return _swa_fwd_call(q, k, v, sink)


def _swa_core_fwd(q, k, v, sink):
    return _swa_fwd_call(q, k, v, sink), (q, k, v, sink)


def _swa_core_bwd(res, do):
    return tuple(_swa_bwd_call(*res, do))


swa_core.defvjp(_swa_core_fwd, _swa_core_bwd)


def _tile(n, pref):
    if n <= pref:
        return n
    best = None
    for t in range(128, pref + 1, 128):
        if n % t == 0:
            best = t
    assert best is not None, (n, pref)
    return best


def _mm_call(a, b, mode, name):
    if mode == "nn":
        (m, kk), n = a.shape, b.shape[1]
    elif mode == "nt":
        (m, kk), n = a.shape, b.shape[0]
    else:
        (kk, m), n = a.shape, b.shape[1]
    tm, tn, tk = _tile(m, 1408), _tile(n, 512), _tile(kk, 1408)
    nk = kk // tk
    if mode == "nn":
        a_spec = pl.BlockSpec((tm, tk), lambda i, j, k: (i, k))
        b_spec = pl.BlockSpec((tk, tn), lambda i, j, k: (k, j))
        dims = (((1,), (0,)), ((), ()))
    elif mode == "nt":
        a_spec = pl.BlockSpec((tm, tk), lambda i, j, k: (i, k))
        b_spec = pl.BlockSpec((tn, tk), lambda i, j, k: (j, k))
        dims = (((1,), (1,)), ((), ()))
    else:
        a_spec = pl.BlockSpec((tk, tm), lambda i, j, k: (k, i))
        b_spec = pl.BlockSpec((tk, tn), lambda i, j, k: (k, j))
        dims = (((0,), (0,)), ((), ()))

    def body(a_ref, b_ref, o_ref, acc_ref):
        k = pl.program_id(2)
        part = lax.dot_general(a_ref[...].astype(BF16), b_ref[...].astype(BF16), dims, preferred_element_type=F32)

        @pl.when(k == 0)
        def _():
            acc_ref[...] = part

        @pl.when(k > 0)
        def _():
            acc_ref[...] += part

        @pl.when(k == nk - 1)
        def _():
            o_ref[...] = acc_ref[...]

    return pl.pallas_call(
        body, name=name, grid=(m // tm, n // tn, nk),
        in_specs=[a_spec, b_spec],
        out_specs=pl.BlockSpec((tm, tn), lambda i, j, k: (i, j)),
        out_shape=jax.ShapeDtypeStruct((m, n), F32),
        scratch_shapes=[pltpu.VMEM((tm, tn), F32)],
        compiler_params=_cp(("parallel", "parallel", "arbitrary")),
    )(a, b)


@jax.custom_vjp
def mm(a, b):
    return _mm_call(a, b, "nn", "mm_nn")


def _mm_fwd(a, b):
    return _mm_call(a, b, "nn", "mm_nn"), (a, b)


def _mm_bwd(res, dc):
    a, b = res
    return _mm_call(dc, b, "nt", "mm_nt"), _mm_call(a, dc, "tn", "mm_tn")


mm.defvjp(_mm_fwd, _mm_bwd)


def _row_specs(arrs, tr):
    return [pl.BlockSpec((tr, a.shape[1]), lambda i: (i, 0)) for a in arrs]


def _par_specs(arrs):
    return [pl.BlockSpec(a.shape, lambda i: (0, 0)) for a in arrs]


def _row_fwd_call(fn, rows, params, out_cols, tr, name):
    seq = rows[0].shape[0]
    nr = len(rows)

    def body(*refs):
        vals = [r[...] for r in refs[:-1]]
        refs[-1][...] = fn(*vals)

    return pl.pallas_call(
        body, name=name, grid=(seq // tr,),
        in_specs=_row_specs(rows, tr) + _par_specs(params),
        out_specs=pl.BlockSpec((tr, out_cols), lambda i: (i, 0)),
        out_shape=jax.ShapeDtypeStruct((seq, out_cols), F32),
        compiler_params=_cp(("parallel",)),
    )(*rows, *params)


def _row_bwd_call(fn, rows, params, dy, tr, name):
    seq = rows[0].shape[0]
    nr, npar = len(rows), len(params)

    def body(*refs):
        ins = refs[:nr + npar]
        dy_ref = refs[nr + npar]
        outs = refs[nr + npar + 1:]
        _, vjp = jax.vjp(fn, *[r[...] for r in ins])
        cts = vjp(dy_ref[...])
        for o_ref, ct in zip(outs[:nr], cts[:nr]):
            o_ref[...] = ct

        @pl.when(pl.program_id(0) == 0)
        def _():
            for o_ref in outs[nr:]:
                o_ref[...] = jnp.zeros_like(o_ref)

        for o_ref, ct in zip(outs[nr:], cts[nr:]):
            o_ref[...] += ct

    return pl.pallas_call(
        body, name=name, grid=(seq // tr,),
        in_specs=_row_specs(rows, tr) + _par_specs(params) + _row_specs([dy], tr),
        out_specs=_row_specs(rows, tr) + _par_specs(params),
        out_shape=[jax.ShapeDtypeStruct(a.shape, F32) for a in (*rows, *params)],
        compiler_params=_cp(("arbitrary",)),
    )(*rows, *params, dy)


def _make_rowop(fn, nrows, out_cols, tr, name):
    @jax.custom_vjp
    def op(*args):
        return _row_fwd_call(fn, args[:nrows], args[nrows:], out_cols, tr, name + "_fwd")

    def fwd(*args):
        return op(*args), args

    def bwd(args, dy):
        return tuple(_row_bwd_call(fn, args[:nrows], args[nrows:], dy, tr, name + "_bwd"))

    op.defvjp(fwd, bwd)
    return op


def _rms_fn(x, w):
    return x * lax.rsqrt(jnp.mean(x * x, axis=-1, keepdims=True) + RMS_EPS) * w


def _merge_fn(pa, pb, pc, gl):
    d = D_MODEL
    return (jax.nn.sigmoid(gl[:, :d]) * pa + jax.nn.sigmoid(gl[:, d:2 * d]) * pb
            + jax.nn.sigmoid(gl[:, 2 * d:]) * pc)


def _relu2_fn(a):
    r = jnp.maximum(a, 0.0)
    return r * r


rms_op = _make_rowop(_rms_fn, 1, D_MODEL, 384, "rms")
merge_op = _make_rowop(_merge_fn, 4, D_MODEL, 192, "merge")
relu2_op = _make_rowop(_relu2_fn, 1, D_FF, 192, "relu2")


def _conv_taps(xext, w, nrows):
    z = None
    for j in range(CONV_K):
        sh = CONV_K - 1 - j
        xs = pltpu.roll(xext, sh, 0) if sh else xext
        term = w[j:j + 1, :] * xs[8:8 + nrows, :]
        z = term if z is None else z + term
    return z


def _halo(ref, start, ok):
    return jnp.where(ok, ref[pl.ds(pl.multiple_of(start, 8), 8), :], 0.0)


def _conv_fwd_call(x, w, b):
    seq, ch = x.shape
    nb = seq // BLK

    def body(x_ref, w_ref, b_ref, o_ref):
        w = w_ref[...]
        bias = b_ref[...]

        def step(i, carry):
            r0 = pl.multiple_of(i * BLK, BLK)
            xext = jnp.concatenate([_halo(x_ref, jnp.maximum(r0 - 8, 0), i > 0), x_ref[pl.ds(r0, BLK), :]], axis=0)
            o_ref[pl.ds(r0, BLK), :] = _silu(_conv_taps(xext, w, BLK) + bias)
            return carry

        lax.fori_loop(0, nb, step, 0)

    strip = pl.BlockSpec((seq, 128), lambda c: (0, c))
    return pl.pallas_call(
        body, name="conv_fwd", grid=(ch // 128,),
        in_specs=[strip, pl.BlockSpec((CONV_K, 128), lambda c: (0, c)), pl.BlockSpec((1, 128), lambda c: (0, c))],
        out_specs=strip, out_shape=jax.ShapeDtypeStruct(x.shape, F32),
        compiler_params=_cp(("parallel",)),
    )(x, w, b)


def _conv_bwd_call(x, w, b, dy):
    seq, ch = x.shape
    nb = seq // BLK

    def body(x_ref, w_ref, b_ref, dy_ref, dx_ref, dw_ref, db_ref):
        w = w_ref[...]
        bias = b_ref[...]

        def step(i, carry):
            r0 = pl.multiple_of(i * BLK, BLK)
            last = i == nb - 1
            nxt = jnp.minimum(r0 + BLK, seq - 8)
            xext = jnp.concatenate([_halo(x_ref, jnp.maximum(r0 - 8, 0), i > 0), x_ref[pl.ds(r0, BLK), :],
                                    _halo(x_ref, nxt, jnp.logical_not(last))], axis=0)
            dyext = jnp.concatenate([dy_ref[pl.ds(r0, BLK), :], _halo(dy_ref, nxt, jnp.logical_not(last))], axis=0)
            z = _conv_taps(xext, w, BLK + 8) + bias
            sg = jax.nn.sigmoid(z)
            dz = dyext * (sg * (1.0 + z * (1.0 - sg)))
            dx = None
            for j in range(CONV_K):
                sh = CONV_K - 1 - j
                dzs = pltpu.roll(dz, BLK + 8 - sh, 0) if sh else dz
                term = w[j:j + 1, :] * dzs[:BLK, :]
                dx = term if dx is None else dx + term
            dx_ref[pl.ds(r0, BLK), :] = dx
            dzm = dz[:BLK, :]
            out = []
            for j in range(CONV_K):
                sh = CONV_K - 1 - j
                xs = pltpu.roll(xext, sh, 0) if sh else xext
                out.append(carry[j] + jnp.sum(dzm * xs[8:8 + BLK, :], axis=0, keepdims=True))
            out.append(carry[CONV_K] + jnp.sum(dzm, axis=0, keepdims=True))
            return tuple(out)

        zero = jnp.zeros((1, 128), F32)
        acc = lax.fori_loop(0, nb, step, (zero,) * (CONV_K + 1))
        dw_ref[...] = jnp.concatenate(acc[:CONV_K], axis=0)
        db_ref[...] = acc[CONV_K]

    strip = pl.BlockSpec((seq, 128), lambda c: (0, c))
    wsp = pl.BlockSpec((CONV_K, 128), lambda c: (0, c))
    bsp = pl.BlockSpec((1, 128), lambda c: (0, c))
    return pl.pallas_call(
        body, name="conv_bwd", grid=(ch // 128,),
        in_specs=[strip, wsp, bsp, strip],
        out_specs=[strip, wsp, bsp],
        out_shape=[jax.ShapeDtypeStruct(x.shape, F32), jax.ShapeDtypeStruct(w.shape, F32),
                   jax.ShapeDtypeStruct(b.shape, F32)],
        compiler_params=_cp(("parallel",)),
    )(x, w, b, dy)


@jax.custom_vjp
def conv_silu(x, w, b):
    return _conv_fwd_call(x, w, b)


def _conv_silu_fwd(x, w, b):
    return _conv_fwd_call(x, w, b), (x, w, b)


def _conv_silu_bwd(res, dy):
    return tuple(_conv_bwd_call(*res, dy))


conv_silu.defvjp(_conv_silu_fwd, _conv_silu_bwd)


def _loss_call(h, wf, target):
    seq, d = h.shape
    nb = seq // BLK

    def body(h_ref, w_ref, t_ref, loss_ref, dh_ref, dw_ref):
        i = pl.program_id(0)
        live = (i > 0).astype(F32)
        tgt = t_ref[...]

        def fn(hh, ww):
            err = _rms_fn(hh, ww) - tgt
            return 0.5 * live * jnp.sum(jnp.mean(err * err, axis=-1, keepdims=True), axis=0, keepdims=True)

        val, vjp = jax.vjp(fn, h_ref[...], w_ref[...])
        dh, dw = vjp(jnp.ones((1, 1), F32))
        dh_ref[...] = dh

        @pl.when(i == 0)
        def _():
            loss_ref[...] = jnp.zeros_like(loss_ref)
            dw_ref[...] = jnp.zeros_like(dw_ref)

        loss_ref[...] += val
        dw_ref[...] += dw

    return pl.pallas_call(
        body, name="loss_head", grid=(nb,),
        in_specs=[pl.BlockSpec((BLK, d), lambda i: (i, 0)), pl.BlockSpec((1, d), lambda i: (0, 0)),
                  pl.BlockSpec((BLK, d), lambda i: (jnp.maximum(i - 1, 0), 0))],
        out_specs=[pl.BlockSpec((1, 1), lambda i: (0, 0)), pl.BlockSpec((BLK, d), lambda i: (i, 0)),
                   pl.BlockSpec((1, d), lambda i: (0, 0))],
        out_shape=[jax.ShapeDtypeStruct((1, 1), F32), jax.ShapeDtypeStruct(h.shape, F32),
                   jax.ShapeDtypeStruct((1, d), F32)],
        compiler_params=_cp(("arbitrary",)),
    )(h, wf, target)


def _make_loss_head(target):
    @jax.custom_vjp
    def head(h, wf):
        return _loss_call(h, wf, target)[0][0, 0]

    def fwd(h, wf):
        loss, dh, dw = _loss_call(h, wf, target)
        return loss[0, 0], (dh, dw)

    def bwd(res, g):
        return g * res[0], g * res[1]

    head.defvjp(fwd, bwd)
    return head


_IN_SEGS = (("q", 0, 1024), ("k", 1024, 1024), ("v", 2048, 1024), ("gate", 3072, 1024), ("z", 4112, 1024),
            ("xbc", 5136, 2048), ("cq", 7200, 1024), ("ck", 8224, 256), ("cv", 8480, 256), ("gl", 8736, 3072),
            ("b", 4096, 8), ("a", 4104, 8), ("dt", 7184, 16))
_IN_PAD = 96
_SPLIT = (1024, 1024, 1024, 1024, 1024, 2048, 1024, 512, 3072, 128)


@jax.custom_vjp
def split_cols(u):
    offs = [sum(_SPLIT[:i]) for i in range(len(_SPLIT))]
    return tuple(u[:, o:o + s] for o, s in zip(offs, _SPLIT))


def _split_fwd(u):
    return split_cols(u), None


def _split_bwd(_, cts):
    return (jnp.concatenate(cts, axis=1),)


split_cols.defvjp(_split_fwd, _split_bwd)


def _heads(t, nh):
    return t.reshape(t.shape[0], nh, 64).transpose(1, 0, 2)


def _unheads(t):
    return t.transpose(1, 0, 2).reshape(t.shape[1], t.shape[0] * t.shape[2])


def _cols(t, chunk):
    return t.T.reshape(t.shape[1], t.shape[0] // chunk, chunk, 1)


def _layer(h, p):
    w_in = jnp.concatenate([p["w_in"][:, s:s + n] for _, s, n in _IN_SEGS]
                           + [jnp.zeros((D_MODEL, _IN_PAD), F32)], axis=1)
    u = mm(rms_op(h, p["norm1_w"].reshape(1, -1)), w_in)
    q_pre, k_pre, v_pre, gate, z, xbc_pre, cq, ckv, gl, small = split_cols(u)

    gcw = p["gdn_conv_w"]
    nob = jnp.zeros((1, GDN_H * GDN_D), F32)
    qa = conv_silu(q_pre, gcw[:, :1024], nob)
    ka = conv_silu(k_pre, gcw[:, 1024:2048], nob)
    va = conv_silu(v_pre, gcw[:, 2048:], nob)
    y_gdn = gdn_core(qa, ka, va, gate, _cols(small[:, 8:16], GDN_C), _cols(small[:, 0:8], GDN_C),
                     p["gdn_a_log"].reshape(GDN_H, 1, 1), p["gdn_dt_bias"].reshape(GDN_H, 1, 1),
                     p["gdn_norm_w"].reshape(1, GDN_D))

    xbc = conv_silu(xbc_pre, p["ssd_conv_w"], p["ssd_conv_b"].reshape(1, -1))
    y_ssd = _unheads(ssd_core(_heads(xbc[:, :1024], SSD_H), _heads(z, SSD_H), xbc[:, 1024:1536], xbc[:, 1536:],
                              _cols(small[:, 16:32], BLK), p["ssd_dt_bias"].reshape(SSD_H, 1, 1),
                              p["ssd_a_log"].reshape(SSD_H, 1, 1), p["ssd_d"].reshape(SSD_H, 1, 1),
                              p["ssd_norm_w"].reshape(SSD_H, 1, SSD_P)))

    y_swa = _unheads(swa_core(_heads(cq, SWA_QH), _heads(ckv[:, :256], SWA_KVH), _heads(ckv[:, 256:], SWA_KVH),
                              p["swa_sinks"].reshape(SWA_QH, 1, 1)))

    merged = merge_op(mm(y_gdn, p["w_proj_gdn"]), mm(y_ssd, p["w_proj_ssd"]), mm(y_swa, p["w_proj_swa"]), gl)
    h = h + mm(merged, p["w_out"])
    a1 = mm(rms_op(h, p["norm2_w"].reshape(1, -1)), p["w_up"])
    return h + mm(relu2_op(a1), p["w_down"])


_PER_LAYER = ("norm1_w", "w_in", "gdn_conv_w", "gdn_a_log", "gdn_dt_bias", "gdn_norm_w", "ssd_conv_w", "ssd_conv_b",
              "ssd_dt_bias", "ssd_a_log", "ssd_d", "ssd_norm_w", "swa_sinks", "w_proj_gdn", "w_proj_ssd",
              "w_proj_swa", "w_out", "norm2_w", "w_up", "w_down")


def _local_loss(x, weights, loss_head):
    h = jnp.concatenate([jnp.zeros((NPAD, D_MODEL), F32), weights["meta_tokens"], x], axis=0)
    for l in range(weights["w_in"].shape[0]):
        h = _layer(h, {n: weights[n][l] for n in _PER_LAYER})
    return loss_head(h, weights["final_norm_w"].reshape(1, -1))


def _position():
    return lax.axis_index("x"), lax.axis_index("y"), lax.axis_index("c")


def _allgather_call(shard, name):
    m_per, n = shard.shape

    def body(x_ref, out_ref, send_sems, recv_sems, local_sem):
        x, y, c = _position()
        me, sibling = (x, y, c), (x, y, 1 - c)
        chips = [(1 - x, y), (x, 1 - y), (1 - x, 1 - y)]

        def rows(px, py, pc):
            return out_ref.at[pl.ds((4 * px + 2 * py + pc) * m_per, m_per), :]

        def copy(k, block, to, src=None):
            return pltpu.make_async_remote_copy(
                src_ref=rows(*block) if src is None else src, dst_ref=rows(*block),
                send_sem=send_sems.at[k], recv_sem=recv_sems.at[k], device_id=to, device_id_type=MESH)

        mine = pltpu.make_async_copy(x_ref, rows(*me), local_sem)
        mine.start()
        first = [copy(0, me, sibling, src=x_ref)]
        first += [copy(1 + j, me, (*chip, c), src=x_ref) for j, chip in enumerate(chips)]
        for cp in first:
            cp.start()
        passed = [copy(4 + j, (*chip, c), sibling) for j, chip in enumerate(chips)]
        for j, chip in enumerate(chips):
            copy(1 + j, (*chip, c), me).wait_recv()
            passed[j].start()
        copy(0, sibling, me).wait_recv()
        for j, chip in enumerate(chips):
            copy(4 + j, (*chip, 1 - c), me).wait_recv()
        for cp in first + passed:
            cp.wait_send()
        mine.wait()

    return pl.pallas_call(
        body, name=name,
        out_shape=jax.ShapeDtypeStruct((N_DEV * m_per, n), shard.dtype),
        in_specs=[pl.BlockSpec(memory_space=pl.ANY)],
        out_specs=pl.BlockSpec(memory_space=pl.ANY),
        scratch_shapes=[pltpu.SemaphoreType.DMA((7,)), pltpu.SemaphoreType.DMA((7,)), pltpu.SemaphoreType.DMA],
    )(shard)


def _exchange_call(g, name):
    def body(g_ref, out_ref, send_sems, recv_sems, local_sem):
        x, y, c = _position()
        me = 4 * x + 2 * y + c
        mine = pltpu.make_async_copy(g_ref.at[me], out_ref.at[me], local_sem)
        mine.start()
        copies = []
        for k in range(1, N_DEV):
            px = 1 - x if k & 4 else x
            py = 1 - y if k & 2 else y
            pc = 1 - c if k & 1 else c
            copies.append(pltpu.make_async_remote_copy(
                src_ref=g_ref.at[4 * px + 2 * py + pc], dst_ref=out_ref.at[me],
                send_sem=send_sems.at[k - 1], recv_sem=recv_sems.at[k - 1],
                device_id=(px, py, pc), device_id_type=MESH))
        for cp in copies:
            cp.start()
        for cp in copies:
            cp.wait_recv()
        for cp in copies:
            cp.wait_send()
        mine.wait()

    return pl.pallas_call(
        body, name=name,
        out_shape=jax.ShapeDtypeStruct(g.shape, g.dtype),
        in_specs=[pl.BlockSpec(memory_space=pl.ANY)],
        out_specs=pl.BlockSpec(memory_space=pl.ANY),
        scratch_shapes=[pltpu.SemaphoreType.DMA((7,)), pltpu.SemaphoreType.DMA((7,)), pltpu.SemaphoreType.DMA],
    )(g)


def _adamw_call(parts, w, m, v, tr, name):
    r, c = w.shape

    def body(p_ref, w_ref, m_ref, v_ref, g_ref, d_ref, nm_ref, nv_ref):
        g = p_ref[0]
        for s in range(1, N_DEV):
            g = g + p_ref[s]
        nm = ADAM_B1 * m_ref[...] + (1.0 - ADAM_B1) * g
        nv = ADAM_B2 * v_ref[...] + (1.0 - ADAM_B2) * (g * g)
        m_hat = nm / (1.0 - ADAM_B1 ** ADAM_STEP)
        v_hat = nv / (1.0 - ADAM_B2 ** ADAM_STEP)
        g_ref[...] = g
        d_ref[...] = -ADAM_LR * (m_hat / (jnp.sqrt(v_hat) + ADAM_EPS) + ADAM_WD * w_ref[...])
        nm_ref[...] = nm
        nv_ref[...] = nv

    flat = pl.BlockSpec((tr, c), lambda i: (i, 0))
    return pl.pallas_call(
        body, name=name, grid=(r // tr,),
        in_specs=[pl.BlockSpec((N_DEV, tr, c), lambda i: (0, i, 0)), flat, flat, flat],
        out_specs=[flat] * 4,
        out_shape=[jax.ShapeDtypeStruct((r, c), F32)] * 4,
        compiler_params=_cp(("parallel",)),
    )(parts, w, m, v)


_WEIGHTS = ("meta_tokens", "norm1_w", "w_in", "gdn_conv_w", "gdn_a_log", "gdn_dt_bias", "gdn_norm_w", "ssd_conv_w",
            "ssd_conv_b", "ssd_dt_bias", "ssd_a_log", "ssd_d", "ssd_norm_w", "swa_sinks", "w_proj_gdn", "w_proj_ssd",
            "w_proj_swa", "w_out", "norm2_w", "w_up", "w_down", "final_norm_w")
_SHARD_AXIS = {"meta_tokens": 1, "w_in": 2, "gdn_conv_w": 2, "ssd_conv_w": 2, "w_proj_gdn": 1, "w_proj_ssd": 1,
               "w_proj_swa": 1, "w_out": 1, "w_up": 2, "w_down": 1}
_BIG = tuple(n for n in _WEIGHTS if n in _SHARD_AXIS)
_SMALL = tuple(n for n in _WEIGHTS if n not in _SHARD_AXIS)
FLAT_C = 1024


def _pack(arrs, rows, lead=()):
    flat = jnp.concatenate([a.reshape(*lead, -1) for a in arrs], axis=-1)
    pad = rows * FLAT_C - flat.shape[-1]
    flat = jnp.pad(flat, [(0, 0)] * len(lead) + [(0, pad)])
    return flat.reshape(*lead, rows, FLAT_C)


def _unpack(flat, shapes, lead=()):
    flat = flat.reshape(*lead, -1)
    out, off = [], 0
    for s in shapes:
        n = math.prod(s)
        out.append(flat[..., off:off + n].reshape(*lead, *s))
        off += n
    return out


def _rows_for(shapes):
    n = sum(math.prod(s) for s in shapes)
    return -(-n // (FLAT_C * 256)) * 256 if n > 16 * FLAT_C else 16


def _join(stacked, axis):
    moved = jnp.moveaxis(stacked, 0, axis)
    return moved.reshape(*moved.shape[:axis], -1, *moved.shape[axis + 2:])


def kernel(x, meta_tokens, norm1_w, w_in, gdn_conv_w, gdn_a_log, gdn_dt_bias, gdn_norm_w, ssd_conv_w, ssd_conv_b,
           ssd_dt_bias, ssd_a_log, ssd_d, ssd_norm_w, swa_sinks, w_proj_gdn, w_proj_ssd, w_proj_swa, w_out, norm2_w,
           w_up, w_down, final_norm_w, loss_target, m_meta_tokens, m_norm1_w, m_w_in, m_gdn_conv_w, m_gdn_a_log,
           m_gdn_dt_bias, m_gdn_norm_w, m_ssd_conv_w, m_ssd_conv_b, m_ssd_dt_bias, m_ssd_a_log, m_ssd_d, m_ssd_norm_w,
           m_swa_sinks, m_w_proj_gdn, m_w_proj_ssd, m_w_proj_swa, m_w_out, m_norm2_w, m_w_up, m_w_down,
           m_final_norm_w, v_meta_tokens, v_norm1_w, v_w_in, v_gdn_conv_w, v_gdn_a_log, v_gdn_dt_bias, v_gdn_norm_w,
           v_ssd_conv_w, v_ssd_conv_b, v_ssd_dt_bias, v_ssd_a_log, v_ssd_d, v_ssd_norm_w, v_swa_sinks, v_w_proj_gdn,
           v_w_proj_ssd, v_w_proj_swa, v_w_out, v_norm2_w, v_w_up, v_w_down, v_final_norm_w):
    args = (meta_tokens, norm1_w, w_in, gdn_conv_w, gdn_a_log, gdn_dt_bias, gdn_norm_w, ssd_conv_w, ssd_conv_b,
            ssd_dt_bias, ssd_a_log, ssd_d, ssd_norm_w, swa_sinks, w_proj_gdn, w_proj_ssd, w_proj_swa, w_out, norm2_w,
            w_up, w_down, final_norm_w, m_meta_tokens, m_norm1_w, m_w_in, m_gdn_conv_w, m_gdn_a_log,
            m_gdn_dt_bias, m_gdn_norm_w, m_ssd_conv_w, m_ssd_conv_b, m_ssd_dt_bias, m_ssd_a_log, m_ssd_d, m_ssd_norm_w,
            m_swa_sinks, m_w_proj_gdn, m_w_proj_ssd, m_w_proj_swa, m_w_out, m_norm2_w, m_w_up, m_w_down,
            m_final_norm_w, v_meta_tokens, v_norm1_w, v_w_in, v_gdn_conv_w, v_gdn_a_log, v_gdn_dt_bias, v_gdn_norm_w,
            v_ssd_conv_w, v_ssd_conv_b, v_ssd_dt_bias, v_ssd_a_log, v_ssd_d, v_ssd_norm_w, v_swa_sinks, v_w_proj_gdn,
            v_w_proj_ssd, v_w_proj_swa, v_w_out, v_norm2_w, v_w_up, v_w_down, v_final_norm_w)
    nw = len(_WEIGHTS)
    w = dict(zip(_WEIGHTS, args[:nw]))
    m = dict(zip(_WEIGHTS, args[nw:2 * nw]))
    v = dict(zip(_WEIGHTS, args[2 * nw:]))

    big_shapes = [w[n].shape for n in _BIG]
    small_shapes = [w[n].shape for n in _SMALL]
    big_rows, small_rows = _rows_for(big_shapes), _rows_for(small_shapes)

    gathered = _allgather_call(_pack([w[n] for n in _BIG], big_rows), "gather_weights")
    stacked = dict(zip(_BIG, _unpack(gathered.reshape(N_DEV, big_rows, FLAT_C), big_shapes, lead=(N_DEV,))))
    small = {n: w[n] for n in _SMALL}
    loss_head = _make_loss_head(loss_target[0])

    def local_loss(x_rows, stacked, small):
        full = {n: _join(stacked[n], _SHARD_AXIS[n]) for n in _BIG}
        return _local_loss(x_rows, {**full, **small}, loss_head)

    loss, (gx, g_stacked, g_small) = jax.value_and_grad(local_loss, argnums=(0, 1, 2))(x[0], stacked, small)
    loss = lax.psum(loss, ("x", "y", "c"))

    parts = _exchange_call(_pack([g_stacked[n] for n in _BIG], big_rows, lead=(N_DEV,)), "scatter_grads")
    big_out = _adamw_call(parts, *[_pack([d[n] for n in _BIG], big_rows) for d in (w, m, v)], 256, "adamw_sharded")
    small_parts = _allgather_call(_pack([g_small[n] for n in _SMALL], small_rows), "gather_small_grads")
    small_out = _adamw_call(small_parts.reshape(N_DEV, small_rows, FLAT_C),
                            *[_pack([d[n] for n in _SMALL], small_rows) for d in (w, m, v)], small_rows,
                            "adamw_replicated")

    outs = []
    for kind in range(4):
        by_name = dict(zip(_BIG, _unpack(big_out[kind], big_shapes)))
        by_name.update(zip(_SMALL, _unpack(small_out[kind], small_shapes)))
        outs.extend(by_name[n] for n in _WEIGHTS)
    return (loss, gx[None], *outs)
```

```python
import functools
import math

import jax
import jax.numpy as jnp
from jax import lax
from jax.experimental import pallas as pl
from jax.experimental.pallas import tpu as pltpu

F32 = jnp.float32
BF16 = jnp.bfloat16
HI = lax.Precision.HIGHEST
NEG = -1e30

D_MODEL = 1024
N_META = 16
BLK = 128
NPAD = BLK - N_META
RMS_EPS = 1e-6
L2_EPS = 1e-6
CONV_K = 4

GDN_H, GDN_D, GDN_C = 8, 128, 64
SSD_H, SSD_P, SSD_G, SSD_N = 16, 64, 4, 128
SSD_HPG = SSD_H // SSD_G
SWA_QH, SWA_KVH, SWA_D = 16, 4, 64
SWA_REP = SWA_QH // SWA_KVH
D_FF = 4 * D_MODEL

N_DEV = 8
MESH = pl.DeviceIdType.MESH

ADAM_LR, ADAM_B1, ADAM_B2, ADAM_EPS, ADAM_WD, ADAM_STEP = 0.001, 0.9, 0.999, 1e-08, 0.01, 10

VMEM_LIMIT = 56 * 1024 * 1024


def _cp(sem=None):
    return pltpu.CompilerParams(dimension_semantics=sem, vmem_limit_bytes=VMEM_LIMIT)


def _dot(a, b, ca, cb, prec=HI):
    return lax.dot_general(a, b, (((ca,), (cb,)), ((), ())), precision=prec, preferred_element_type=F32)


def _nn(a, b, prec=HI):
    return _dot(a, b, 1, 0, prec)


def _nt(a, b, prec=HI):
    return _dot(a, b, 1, 1, prec)


def _tn(a, b, prec=HI):
    return _dot(a, b, 0, 0, prec)


def _bdot(a, b, ca, cb):
    return lax.dot_general(a.astype(BF16), b.astype(BF16), (((ca,), (cb,)), ((), ())), preferred_element_type=F32)


@jax.custom_vjp
def _lo_nn(a, b):
    return _bdot(a, b, 1, 0)


_lo_nn.defvjp(lambda a, b: (_bdot(a, b, 1, 0), (a, b)),
              lambda r, d: (_bdot(d, r[1], 1, 1), _bdot(r[0], d, 0, 0)))


@jax.custom_vjp
def _lo_nt(a, b):
    return _bdot(a, b, 1, 1)


_lo_nt.defvjp(lambda a, b: (_bdot(a, b, 1, 1), (a, b)),
              lambda r, d: (_bdot(d, r[1], 1, 0), _bdot(d, r[0], 0, 0)))


@jax.custom_vjp
def _lo_tn(a, b):
    return _bdot(a, b, 0, 0)


_lo_tn.defvjp(lambda a, b: (_bdot(a, b, 0, 0), (a, b)),
              lambda r, d: (_bdot(r[1], d, 1, 1), _bdot(r[0], d, 1, 0)))


def _iota2(n, m, axis):
    return lax.broadcasted_iota(jnp.int32, (n, m), axis)


def _silu(x):
    return x * jax.nn.sigmoid(x)


def _softplus(x):
    return jnp.maximum(x, 0.0) + jnp.log(1.0 + jnp.exp(-jnp.abs(x)))


def _row_of(col):
    n = col.shape[0]
    eye = (_iota2(n, n, 0) == _iota2(n, n, 1)).astype(F32)
    return _nn(jnp.ones((n, n), F32), eye * col)


def _cumsum_col(col):
    n = col.shape[0]
    tril = (_iota2(n, n, 0) >= _iota2(n, n, 1)).astype(F32)
    return _nn(tril, col)


def _tri_inv(a):
    n = a.shape[0]
    r, c = _iota2(n, n, 0), _iota2(n, n, 1)
    eye = (r == c).astype(F32)
    blk = jnp.right_shift(r, 4) == jnp.right_shift(c, 4)
    d = jnp.where(blk, a, 0.0)
    off = a - d
    d2 = _nn(d, d)
    d4 = _nn(d2, d2)
    d8 = _nn(d4, d4)
    td = _nn(_nn(_nn(eye - d, eye + d2), eye + d4), eye + d8)
    m = _nn(td, off)
    m2 = _nn(m, m)
    return _nn(_nn(eye - m, eye + m2), td)


@jax.custom_vjp
def _tri_solve(a, a_t, rhs):
    return _nn(_tri_inv(a), rhs)


def _tri_solve_fwd(a, a_t, rhs):
    sol = _nn(_tri_inv(a), rhs)
    return sol, (a_t, sol)


def _tri_solve_bwd(res, dsol):
    a_t, sol = res
    drhs = _nn(_tri_inv(a_t), dsol)
    return -_nt(drhs, sol), jnp.zeros_like(a_t), drhs


_tri_solve.defvjp(_tri_solve_fwd, _tri_solve_bwd)


def _gdn_chunk(qa, ka, va, gate, a_raw, b_raw, s, a_log, dt_bias, norm_w, valid):
    c = qa.shape[0]
    q = qa * lax.rsqrt(jnp.sum(qa * qa, axis=-1, keepdims=True) + L2_EPS) * (GDN_D ** -0.5)
    k = ka * lax.rsqrt(jnp.sum(ka * ka, axis=-1, keepdims=True) + L2_EPS)
    beta = jax.nn.sigmoid(b_raw)
    g = -jnp.exp(a_log) * _softplus(a_raw + dt_bias) * valid
    gam = _cumsum_col(g)
    gam_row = _row_of(gam)
    r, cc = _iota2(c, c, 0), _iota2(c, c, 1)
    decay = jnp.exp(jnp.where(r >= cc, gam - gam_row, NEG))
    kb = k * beta
    a = jnp.where(r > cc, _lo_nt(kb, k) * decay, 0.0)
    a_t = lax.stop_gradient(jnp.where(cc > r, _bdot(k, kb, 1, 1) * jnp.exp(jnp.where(cc >= r, gam_row - gam, NEG)), 0.0))
    egam = jnp.exp(gam)
    sol = _tri_solve(a, a_t, jnp.concatenate([va * beta, kb * egam], axis=1))
    u = sol[:, :GDN_D]
    w = sol[:, GDN_D:]
    attn = _lo_nt(q, k) * decay
    g_last = jnp.sum(g, axis=0, keepdims=True)
    k_tail = k * jnp.exp(g_last - gam)
    v_new = u - _lo_nn(w, s)
    o = _lo_nn(q * egam, s) + _lo_nn(attn, v_new)
    s_new = s * jnp.exp(g_last) + _lo_tn(k_tail, v_new)
    y = o * lax.rsqrt(jnp.mean(o * o, axis=-1, keepdims=True) + RMS_EPS) * norm_w * _silu(gate)
    return y, s_new


def _valid_col(row0, n):
    return (row0 + _iota2(n, 1, 0) >= NPAD).astype(F32)


GDN_HB = 4


def _gdn_specs(nc, rev):
    ci = (lambda i: nc - 1 - i) if rev else (lambda i: i)
    hb = GDN_HB
    tile = pl.BlockSpec((GDN_C, hb * GDN_D), lambda h, i: (ci(i), h))
    col = pl.BlockSpec((hb, 1, GDN_C, 1), lambda h, i: (h, ci(i), 0, 0))
    scal = pl.BlockSpec((hb, 1, 1), lambda h, i: (h, 0, 0))
    nw = pl.BlockSpec((1, GDN_D), lambda h, i: (0, 0))
    st = pl.BlockSpec((hb, 1, GDN_D, GDN_D), lambda h, i: (h, ci(i), 0, 0))
    return tile, col, scal, nw, st


def _lanes(j):
    return slice(j * GDN_D, (j + 1) * GDN_D)


def _gdn_fwd_call(q, k, v, gate, a_col, b_col, a_log, dt_bias, norm_w):
    seq = q.shape[0]
    nc = seq // GDN_C
    tile, col, scal, nw, st = _gdn_specs(nc, False)

    def body(q_ref, k_ref, v_ref, g_ref, a_ref, b_ref, al_ref, dt_ref, nw_ref, y_ref, st_ref, s_scr):
        i = pl.program_id(1)

        @pl.when(i == 0)
        def _():
            s_scr[...] = jnp.zeros_like(s_scr)

        valid = _valid_col(i * GDN_C, GDN_C)
        for j in range(GDN_HB):
            s = s_scr[j]
            st_ref[j, 0] = s
            y, s_new = _gdn_chunk(q_ref[:, _lanes(j)], k_ref[:, _lanes(j)], v_ref[:, _lanes(j)], g_ref[:, _lanes(j)],
                                  a_ref[j, 0], b_ref[j, 0], s, al_ref[j], dt_ref[j], nw_ref[...], valid)
            y_ref[:, _lanes(j)] = y
            s_scr[j] = s_new

    return pl.pallas_call(
        body, name="gdn_fwd", grid=(GDN_H // GDN_HB, nc),
        in_specs=[tile, tile, tile, tile, col, col, scal, scal, nw],
        out_specs=[tile, st],
        out_shape=[jax.ShapeDtypeStruct((seq, GDN_H * GDN_D), F32),
                   jax.ShapeDtypeStruct((GDN_H, nc, GDN_D, GDN_D), F32)],
        scratch_shapes=[pltpu.VMEM((GDN_HB, GDN_D, GDN_D), F32)],
        compiler_params=_cp(("parallel", "arbitrary")),
    )(q, k, v, gate, a_col, b_col, a_log, dt_bias, norm_w)


def _gdn_bwd_call(q, k, v, gate, a_col, b_col, a_log, dt_bias, norm_w, states, dy):
    seq = q.shape[0]
    nc = seq // GDN_C
    tile, col, scal, nw, st = _gdn_specs(nc, True)
    nwh = pl.BlockSpec((GDN_HB, 1, GDN_D), lambda h, i: (h, 0, 0))

    def body(q_ref, k_ref, v_ref, g_ref, a_ref, b_ref, al_ref, dt_ref, nw_ref, st_ref, dy_ref,
             dq_ref, dk_ref, dv_ref, dg_ref, da_ref, db_ref, dal_ref, ddt_ref, dnw_ref, ds_scr):
        i = pl.program_id(1)

        @pl.when(i == 0)
        def _():
            ds_scr[...] = jnp.zeros_like(ds_scr)
            dal_ref[...] = jnp.zeros_like(dal_ref)
            ddt_ref[...] = jnp.zeros_like(ddt_ref)
            dnw_ref[...] = jnp.zeros_like(dnw_ref)

        fn = functools.partial(_gdn_chunk, valid=_valid_col((nc - 1 - i) * GDN_C, GDN_C))
        for j in range(GDN_HB):
            _, vjp = jax.vjp(fn, q_ref[:, _lanes(j)], k_ref[:, _lanes(j)], v_ref[:, _lanes(j)], g_ref[:, _lanes(j)],
                             a_ref[j, 0], b_ref[j, 0], st_ref[j, 0], al_ref[j], dt_ref[j], nw_ref[...])
            dq, dk, dv, dg, da, db, ds, dal, ddt, dnw = vjp((dy_ref[:, _lanes(j)], ds_scr[j]))
            dq_ref[:, _lanes(j)] = dq
            dk_ref[:, _lanes(j)] = dk
            dv_ref[:, _lanes(j)] = dv
            dg_ref[:, _lanes(j)] = dg
            da_ref[j, 0] = da
            db_ref[j, 0] = db
            ds_scr[j] = ds
            dal_ref[j] += dal
            ddt_ref[j] += ddt
            dnw_ref[j] += dnw

    big = jax.ShapeDtypeStruct((seq, GDN_H * GDN_D), F32)
    return pl.pallas_call(
        body, name="gdn_bwd", grid=(GDN_H // GDN_HB, nc),
        in_specs=[tile, tile, tile, tile, col, col, scal, scal, nw, st, tile],
        out_specs=[tile, tile, tile, tile, col, col, scal, scal, nwh],
        out_shape=[big, big, big, big,
                   jax.ShapeDtypeStruct(a_col.shape, F32), jax.ShapeDtypeStruct(b_col.shape, F32),
                   jax.ShapeDtypeStruct((GDN_H, 1, 1), F32), jax.ShapeDtypeStruct((GDN_H, 1, 1), F32),
                   jax.ShapeDtypeStruct((GDN_H, 1, GDN_D), F32)],
        scratch_shapes=[pltpu.VMEM((GDN_HB, GDN_D, GDN_D), F32)],
        compiler_params=_cp(("parallel", "arbitrary")),
    )(q, k, v, gate, a_col, b_col, a_log, dt_bias, norm_w, states, dy)


@jax.custom_vjp
def gdn_core(q, k, v, gate, a_col, b_col, a_log, dt_bias, norm_w):
    return _gdn_fwd_call(q, k, v, gate, a_col, b_col, a_log, dt_bias, norm_w)[0]


def _gdn_core_fwd(q, k, v, gate, a_col, b_col, a_log, dt_bias, norm_w):
    y, states = _gdn_fwd_call(q, k, v, gate, a_col, b_col, a_log, dt_bias, norm_w)
    return y, (q, k, v, gate, a_col, b_col, a_log, dt_bias, norm_w, states)


def _gdn_core_bwd(res, dy):
    dq, dk, dv, dg, da, db, dal, ddt, dnw = _gdn_bwd_call(*res, dy)
    return dq, dk, dv, dg, da, db, dal, ddt, jnp.sum(dnw, axis=0)


gdn_core.defvjp(_gdn_core_fwd, _gdn_core_bwd)


def _ssd_chunk(xs, z, bm, cm, dt_raw, h, dt_bias, a_log, d_skip, norm_w, valid):
    c = bm.shape[0]
    r, cc = _iota2(c, c, 0), _iota2(c, c, 1)
    bm = bm * valid
    cm = cm * valid
    cb = _lo_nt(cm, bm)
    ys, hs = [], []
    for i in range(SSD_HPG):
        dtp = _softplus(dt_raw[i] + dt_bias[i])
        x = xs[i] * valid
        adt = -jnp.exp(a_log[i]) * dtp * valid
        xdt = x * dtp
        acum = _cumsum_col(adt)
        lmat = jnp.exp(jnp.where(r >= cc, acum - _row_of(acum), NEG))
        a_last = jnp.sum(adt, axis=0, keepdims=True)
        y = _lo_nn(cb * lmat, xdt) + _lo_nt(cm * jnp.exp(acum), h[i]) + d_skip[i] * x
        hs.append(h[i] * jnp.exp(a_last) + _lo_tn(xdt * jnp.exp(a_last - acum), bm))
        ys.append(y * _silu(z[i]))
    ss = sum(jnp.sum(y * y, axis=-1, keepdims=True) for y in ys)
    rstd = lax.rsqrt(ss / (SSD_HPG * SSD_P) + RMS_EPS)
    return jnp.stack([ys[i] * rstd * norm_w[i] for i in range(SSD_HPG)]), jnp.stack(hs)


def _ssd_specs(nc, rev):
    ci = (lambda i: nc - 1 - i) if rev else (lambda i: i)
    hp = SSD_HPG
    head = pl.BlockSpec((hp, BLK, SSD_P), lambda g, i: (g, ci(i), 0))
    grp = pl.BlockSpec((BLK, SSD_N), lambda g, i: (ci(i), g))
    col = pl.BlockSpec((hp, 1, BLK, 1), lambda g, i: (g, ci(i), 0, 0))
    scal = pl.BlockSpec((hp, 1, 1), lambda g, i: (g, 0, 0))
    nw = pl.BlockSpec((hp, 1, SSD_P), lambda g, i: (g, 0, 0))
    st = pl.BlockSpec((hp, 1, SSD_P, SSD_N), lambda g, i: (g, ci(i), 0, 0))
    return head, grp, col, scal, nw, st


def _ssd_fwd_call(xs, z, bm, cm, dt_col, dt_bias, a_log, d_skip, norm_w):
    seq = xs.shape[1]
    nc = seq // BLK
    head, grp, col, scal, nw, st = _ssd_specs(nc, False)

    def body(x_ref, z_ref, b_ref, c_ref, dt_ref, db_ref, al_ref, ds_ref, nw_ref, y_ref, st_ref, h_scr):
        i = pl.program_id(1)

        @pl.when(i == 0)
        def _():
            h_scr[...] = jnp.zeros_like(h_scr)

        h = h_scr[...]
        st_ref[:, 0] = h
        y, h_new = _ssd_chunk(x_ref[...], z_ref[...], b_ref[...], c_ref[...], dt_ref[:, 0], h, db_ref[...],
                              al_ref[...], ds_ref[...], nw_ref[...], _valid_col(i * BLK, BLK))
        y_ref[...] = y
        h_scr[...] = h_new

    return pl.pallas_call(
        body, name="ssd_fwd", grid=(SSD_G, nc),
        in_specs=[head, head, grp, grp, col, scal, scal, scal, nw],
        out_specs=[head, st],
        out_shape=[jax.ShapeDtypeStruct((SSD_H, seq, SSD_P), F32),
                   jax.ShapeDtypeStruct((SSD_H, nc, SSD_P, SSD_N), F32)],
        scratch_shapes=[pltpu.VMEM((SSD_HPG, SSD_P, SSD_N), F32)],
        compiler_params=_cp(("parallel", "arbitrary")),
    )(xs, z, bm, cm, dt_col, dt_bias, a_log, d_skip, norm_w)


def _ssd_bwd_call(xs, z, bm, cm, dt_col, dt_bias, a_log, d_skip, norm_w, states, dy):
    seq = xs.shape[1]
    nc = seq // BLK
    head, grp, col, scal, nw, st = _ssd_specs(nc, True)

    def body(x_ref, z_ref, b_ref, c_ref, dt_ref, db_ref, al_ref, ds_ref, nw_ref, st_ref, dy_ref,
             dx_ref, dz_ref, dbm_ref, dcm_ref, ddt_ref, ddb_ref, dal_ref, dds_ref, dnw_ref, dh_scr):
        i = pl.program_id(1)

        @pl.when(i == 0)
        def _():
            dh_scr[...] = jnp.zeros_like(dh_scr)
            ddb_ref[...] = jnp.zeros_like(ddb_ref)
            dal_ref[...] = jnp.zeros_like(dal_ref)
            dds_ref[...] = jnp.zeros_like(dds_ref)
            dnw_ref[...] = jnp.zeros_like(dnw_ref)

        fn = functools.partial(_ssd_chunk, valid=_valid_col((nc - 1 - i) * BLK, BLK))
        _, vjp = jax.vjp(fn, x_ref[...], z_ref[...], b_ref[...], c_ref[...], dt_ref[:, 0], st_ref[:, 0],
                         db_ref[...], al_ref[...], ds_ref[...], nw_ref[...])
        dx, dz, dbm, dcm, ddt, dh, ddb, dal, dds, dnw = vjp((dy_ref[...], dh_scr[...]))
        dx_ref[...] = dx
        dz_ref[...] = dz
        dbm_ref[...] = dbm
        dcm_ref[...] = dcm
        ddt_ref[:, 0] = ddt
        dh_scr[...] = dh
        ddb_ref[...] += ddb
        dal_ref[...] += dal
        dds_ref[...] += dds
        dnw_ref[...] += dnw

    hshape = jax.ShapeDtypeStruct(xs.shape, F32)
    gshape = jax.ShapeDtypeStruct(bm.shape, F32)
    sshape = jax.ShapeDtypeStruct((SSD_H, 1, 1), F32)
    return pl.pallas_call(
        body, name="ssd_bwd", grid=(SSD_G, nc),
        in_specs=[head, head, grp, grp, col, scal, scal, scal, nw, st, head],
        out_specs=[head, head, grp, grp, col, scal, scal, scal, nw],
        out_shape=[hshape, hshape, gshape, gshape, jax.ShapeDtypeStruct(dt_col.shape, F32),
                   sshape, sshape, sshape, jax.ShapeDtypeStruct((SSD_H, 1, SSD_P), F32)],
        scratch_shapes=[pltpu.VMEM((SSD_HPG, SSD_P, SSD_N), F32)],
        compiler_params=_cp(("parallel", "arbitrary")),
    )(xs, z, bm, cm, dt_col, dt_bias, a_log, d_skip, norm_w, states, dy)


@jax.custom_vjp
def ssd_core(xs, z, bm, cm, dt_col, dt_bias, a_log, d_skip, norm_w):
    return _ssd_fwd_call(xs, z, bm, cm, dt_col, dt_bias, a_log, d_skip, norm_w)[0]


def _ssd_core_fwd(*args):
    y, states = _ssd_fwd_call(*args)
    return y, (*args, states)


def _ssd_core_bwd(res, dy):
    return tuple(_ssd_bwd_call(*res, dy))


ssd_core.defvjp(_ssd_core_fwd, _ssd_core_bwd)


def _swa_block(q, km, kp, kc, vm, vp, vc, sink, n):
    rows = SWA_REP * BLK
    qs = q.reshape(rows, SWA_D) * (SWA_D ** -0.5)
    s = _lo_nt(qs, jnp.concatenate([km, kp, kc], axis=0))
    i = jnp.bitwise_and(_iota2(rows, 3 * BLK, 0), BLK - 1)
    col = _iota2(rows, 3 * BLK, 1)
    j = jnp.bitwise_and(col, BLK - 1)
    part = jnp.right_shift(col, 7)
    ok_m = (part == 0) & (j >= NPAD) & ((n >= 1) | (j <= i))
    ok_p = (part == 1) & (n >= 2) & (j > i)
    ok_c = (part == 2) & (n >= 1) & (j <= i)
    ok = ok_m | ok_p | ok_c
    s = jnp.where(ok, s, NEG)
    snk = jnp.concatenate([jnp.broadcast_to(sink[r], (BLK, 1)) for r in range(SWA_REP)], axis=0)
    m = lax.stop_gradient(jnp.maximum(jnp.max(s, axis=-1, keepdims=True), snk))
    e = jnp.exp(s - m)
    p = e / (jnp.sum(e, axis=-1, keepdims=True) + jnp.exp(snk - m))
    o = _lo_nn(p, jnp.concatenate([vm, vp, vc], axis=0))
    return o.reshape(SWA_REP, BLK, SWA_D)


def _swa_specs(nb, rev):
    ci = (lambda i: nb - 1 - i) if rev else (lambda i: i)
    qsp = pl.BlockSpec((SWA_REP, BLK, SWA_D), lambda g, i: (g, ci(i), 0))
    cur = pl.BlockSpec((1, BLK, SWA_D), lambda g, i: (g, ci(i), 0))
    prev = pl.BlockSpec((1, BLK, SWA_D), lambda g, i: (g, jnp.maximum(ci(i) - 1, 0), 0))
    meta = pl.BlockSpec((1, BLK, SWA_D), lambda g, i: (g, 0, 0))
    scal = pl.BlockSpec((SWA_REP, 1, 1), lambda g, i: (g, 0, 0))
    return qsp, cur, prev, meta, scal


def _swa_fwd_call(q, k, v, sink):
    seq = q.shape[1]
    nb = seq // BLK
    qsp, cur, prev, meta, scal = _swa_specs(nb, False)

    def body(q_ref, km_ref, kp_ref, kc_ref, vm_ref, vp_ref, vc_ref, s_ref, o_ref):
        o_ref[...] = _swa_block(q_ref[...], km_ref[0], kp_ref[0], kc_ref[0], vm_ref[0], vp_ref[0], vc_ref[0],
                                s_ref[...], pl.program_id(1))

    return pl.pallas_call(
        body, name="swa_fwd", grid=(SWA_KVH, nb),
        in_specs=[qsp, meta, prev, cur, meta, prev, cur, scal],
        out_specs=qsp,
        out_shape=jax.ShapeDtypeStruct(q.shape, F32),
        compiler_params=_cp(("parallel", "arbitrary")),
    )(q, k, k, k, v, v, v, sink)


def _swa_bwd_call(q, k, v, sink, do):
    seq = q.shape[1]
    nb = seq // BLK
    qsp, cur, prev, meta, scal = _swa_specs(nb, True)

    def body(q_ref, km_ref, kp_ref, kc_ref, vm_ref, vp_ref, vc_ref, s_ref, do_ref,
             dq_ref, dk_ref, dv_ref, ds_ref, kp_scr, vp_scr, km_scr, vm_scr):
        i = pl.program_id(1)
        n = nb - 1 - i

        @pl.when(i == 0)
        def _():
            for scr in (kp_scr, vp_scr, km_scr, vm_scr):
                scr[...] = jnp.zeros_like(scr)
            ds_ref[...] = jnp.zeros_like(ds_ref)

        fn = functools.partial(_swa_block, n=n)
        _, vjp = jax.vjp(fn, q_ref[...], km_ref[0], kp_ref[0], kc_ref[0], vm_ref[0], vp_ref[0], vc_ref[0], s_ref[...])
        dq, dkm, dkp, dkc, dvm, dvp, dvc, dsk = vjp(do_ref[...])
        dq_ref[...] = dq
        ds_ref[...] += dsk
        km_scr[...] += dkm
        vm_scr[...] += dvm
        first = (n == 0).astype(F32)
        dk_ref[0] = dkc + kp_scr[...] + first * km_scr[...]
        dv_ref[0] = dvc + vp_scr[...] + first * vm_scr[...]
        kp_scr[...] = dkp
        vp_scr[...] = dvp

    kv = jax.ShapeDtypeStruct(k.shape, F32)
    return pl.pallas_call(
        body, name="swa_bwd", grid=(SWA_KVH, nb),
        in_specs=[qsp, meta, prev, cur, meta, prev, cur, scal, qsp],
        out_specs=[qsp, cur, cur, scal],
        out_shape=[jax.ShapeDtypeStruct(q.shape, F32), kv, kv, jax.ShapeDtypeStruct(sink.shape, F32)],
        scratch_shapes=[pltpu.VMEM((BLK, SWA_D), F32)] * 4,
        compiler_params=_cp(("parallel", "arbitrary")),
    )(q, k, k, k, v, v, v, sink, do)


@jax.custom_vjp
def swa_core(q, k, v, sink):
    return _swa_fwd_call(q, k, v, sink)


def _swa_core_fwd(q, k, v, sink):
    return _swa_fwd_call(q, k, v, sink), (q, k, v, sink)


def _swa_core_bwd(res, do):
    return tuple(_swa_bwd_call(*res, do))


swa_core.defvjp(_swa_core_fwd, _swa_core_bwd)


def _tile(n, pref):
    if n <= pref:
        return n
    best = None
    for t in range(128, pref + 1, 128):
        if n % t == 0:
            best = t
    assert best is not None, (n, pref)
    return best


def _mm_call(a, b, mode, name):
    if mode == "nn":
        (m, kk), n = a.shape, b.shape[1]
    elif mode == "nt":
        (m, kk), n = a.shape, b.shape[0]
    else:
        (kk, m), n = a.shape, b.shape[1]
    tm, tn, tk = _tile(m, 1408), _tile(n, 512), _tile(kk, 1408)
    nk = kk // tk
    if mode == "nn":
        a_spec = pl.BlockSpec((tm, tk), lambda i, j, k: (i, k))
        b_spec = pl.BlockSpec((tk, tn), lambda i, j, k: (k, j))
        dims = (((1,), (0,)), ((), ()))
    elif mode == "nt":
        a_spec = pl.BlockSpec((tm, tk), lambda i, j, k: (i, k))
        b_spec = pl.BlockSpec((tn, tk), lambda i, j, k: (j, k))
        dims = (((1,), (1,)), ((), ()))
    else:
        a_spec = pl.BlockSpec((tk, tm), lambda i, j, k: (k, i))
        b_spec = pl.BlockSpec((tk, tn), lambda i, j, k: (k, j))
        dims = (((0,), (0,)), ((), ()))

    def body(a_ref, b_ref, o_ref, acc_ref):
        k = pl.program_id(2)
        part = lax.dot_general(a_ref[...].astype(BF16), b_ref[...].astype(BF16), dims, preferred_element_type=F32)

        @pl.when(k == 0)
        def _():
            acc_ref[...] = part

        @pl.when(k > 0)
        def _():
            acc_ref[...] += part

        @pl.when(k == nk - 1)
        def _():
            o_ref[...] = acc_ref[...]

    return pl.pallas_call(
        body, name=name, grid=(m // tm, n // tn, nk),
        in_specs=[a_spec, b_spec],
        out_specs=pl.BlockSpec((tm, tn), lambda i, j, k: (i, j)),
        out_shape=jax.ShapeDtypeStruct((m, n), F32),
        scratch_shapes=[pltpu.VMEM((tm, tn), F32)],
        compiler_params=_cp(("parallel", "parallel", "arbitrary")),
    )(a, b)


@jax.custom_vjp
def mm(a, b, grad_slot):
    return _mm_call(a, b, "nn", "mm_nn")


def _mm_fwd(a, b, grad_slot):
    return _mm_call(a, b, "nn", "mm_nn"), (a, b)


def _mm_bwd(res, dc):
    a, b = res
    return _mm_call(dc, b, "nt", "mm_nt"), jnp.zeros_like(b), _mm_call(a, dc, "tn", "mm_tn")


mm.defvjp(_mm_fwd, _mm_bwd)


def _row_specs(arrs, tr):
    return [pl.BlockSpec((tr, a.shape[1]), lambda i: (i, 0)) for a in arrs]


def _par_specs(arrs):
    return [pl.BlockSpec(a.shape, lambda i: (0, 0)) for a in arrs]


def _row_fwd_call(fn, rows, params, out_cols, tr, name):
    seq = rows[0].shape[0]
    nr = len(rows)

    def body(*refs):
        vals = [r[...] for r in refs[:-1]]
        refs[-1][...] = fn(*vals)

    return pl.pallas_call(
        body, name=name, grid=(seq // tr,),
        in_specs=_row_specs(rows, tr) + _par_specs(params),
        out_specs=pl.BlockSpec((tr, out_cols), lambda i: (i, 0)),
        out_shape=jax.ShapeDtypeStruct((seq, out_cols), F32),
        compiler_params=_cp(("parallel",)),
    )(*rows, *params)


def _row_bwd_call(fn, rows, params, dy, tr, name):
    seq = rows[0].shape[0]
    nr, npar = len(rows), len(params)

    def body(*refs):
        ins = refs[:nr + npar]
        dy_ref = refs[nr + npar]
        outs = refs[nr + npar + 1:]
        _, vjp = jax.vjp(fn, *[r[...] for r in ins])
        cts = vjp(dy_ref[...])
        for o_ref, ct in zip(outs[:nr], cts[:nr]):
            o_ref[...] = ct

        @pl.when(pl.program_id(0) == 0)
        def _():
            for o_ref in outs[nr:]:
                o_ref[...] = jnp.zeros_like(o_ref)

        for o_ref, ct in zip(outs[nr:], cts[nr:]):
            o_ref[...] += ct

    return pl.pallas_call(
        body, name=name, grid=(seq // tr,),
        in_specs=_row_specs(rows, tr) + _par_specs(params) + _row_specs([dy], tr),
        out_specs=_row_specs(rows, tr) + _par_specs(params),
        out_shape=[jax.ShapeDtypeStruct(a.shape, F32) for a in (*rows, *params)],
        compiler_params=_cp(("arbitrary",)),
    )(*rows, *params, dy)


def _make_rowop(fn, nrows, out_cols, tr, name):
    @jax.custom_vjp
    def op(*args):
        return _row_fwd_call(fn, args[:nrows], args[nrows:], out_cols, tr, name + "_fwd")

    def fwd(*args):
        return op(*args), args

    def bwd(args, dy):
        return tuple(_row_bwd_call(fn, args[:nrows], args[nrows:], dy, tr, name + "_bwd"))

    op.defvjp(fwd, bwd)
    return op


def _rms_fn(x, w):
    return x * lax.rsqrt(jnp.mean(x * x, axis=-1, keepdims=True) + RMS_EPS) * w


def _merge_fn(pa, pb, pc, gl):
    d = D_MODEL
    return (jax.nn.sigmoid(gl[:, :d]) * pa + jax.nn.sigmoid(gl[:, d:2 * d]) * pb
            + jax.nn.sigmoid(gl[:, 2 * d:]) * pc)


def _relu2_fn(a):
    r = jnp.maximum(a, 0.0)
    return r * r


rms_op = _make_rowop(_rms_fn, 1, D_MODEL, 384, "rms")
merge_op = _make_rowop(_merge_fn, 4, D_MODEL, 192, "merge")
relu2_op = _make_rowop(_relu2_fn, 1, D_FF, 192, "relu2")


def _conv_taps(xext, w, nrows):
    z = None
    for j in range(CONV_K):
        sh = CONV_K - 1 - j
        xs = pltpu.roll(xext, sh, 0) if sh else xext
        term = w[j:j + 1, :] * xs[8:8 + nrows, :]
        z = term if z is None else z + term
    return z


def _halo(ref, start, ok):
    return jnp.where(ok, ref[pl.ds(pl.multiple_of(start, 8), 8), :], 0.0)


def _conv_fwd_call(x, w, b):
    seq, ch = x.shape
    nb = seq // BLK

    def body(x_ref, w_ref, b_ref, o_ref):
        w = w_ref[...]
        bias = b_ref[...]

        def step(i, carry):
            r0 = pl.multiple_of(i * BLK, BLK)
            xext = jnp.concatenate([_halo(x_ref, jnp.maximum(r0 - 8, 0), i > 0), x_ref[pl.ds(r0, BLK), :]], axis=0)
            o_ref[pl.ds(r0, BLK), :] = _silu(_conv_taps(xext, w, BLK) + bias)
            return carry

        lax.fori_loop(0, nb, step, 0)

    strip = pl.BlockSpec((seq, 128), lambda c: (0, c))
    return pl.pallas_call(
        body, name="conv_fwd", grid=(ch // 128,),
        in_specs=[strip, pl.BlockSpec((CONV_K, 128), lambda c: (0, c)), pl.BlockSpec((1, 128), lambda c: (0, c))],
        out_specs=strip, out_shape=jax.ShapeDtypeStruct(x.shape, F32),
        compiler_params=_cp(("parallel",)),
    )(x, w, b)


def _conv_bwd_call(x, w, b, dy):
    seq, ch = x.shape
    nb = seq // BLK

    def body(x_ref, w_ref, b_ref, dy_ref, dx_ref, dw_ref, db_ref):
        w = w_ref[...]
        bias = b_ref[...]

        def step(i, carry):
            r0 = pl.multiple_of(i * BLK, BLK)
            last = i == nb - 1
            nxt = jnp.minimum(r0 + BLK, seq - 8)
            xext = jnp.concatenate([_halo(x_ref, jnp.maximum(r0 - 8, 0), i > 0), x_ref[pl.ds(r0, BLK), :],
                                    _halo(x_ref, nxt, jnp.logical_not(last))], axis=0)
            dyext = jnp.concatenate([dy_ref[pl.ds(r0, BLK), :], _halo(dy_ref, nxt, jnp.logical_not(last))], axis=0)
            z = _conv_taps(xext, w, BLK + 8) + bias
            sg = jax.nn.sigmoid(z)
            dz = dyext * (sg * (1.0 + z * (1.0 - sg)))
            dx = None
            for j in range(CONV_K):
                sh = CONV_K - 1 - j
                dzs = pltpu.roll(dz, BLK + 8 - sh, 0) if sh else dz
                term = w[j:j + 1, :] * dzs[:BLK, :]
                dx = term if dx is None else dx + term
            dx_ref[pl.ds(r0, BLK), :] = dx
            dzm = dz[:BLK, :]
            out = []
            for j in range(CONV_K):
                sh = CONV_K - 1 - j
                xs = pltpu.roll(xext, sh, 0) if sh else xext
                out.append(carry[j] + jnp.sum(dzm * xs[8:8 + BLK, :], axis=0, keepdims=True))
            out.append(carry[CONV_K] + jnp.sum(dzm, axis=0, keepdims=True))
            return tuple(out)

        zero = jnp.zeros((1, 128), F32)
        acc = lax.fori_loop(0, nb, step, (zero,) * (CONV_K + 1))
        dw_ref[...] = jnp.concatenate(acc[:CONV_K], axis=0)
        db_ref[...] = acc[CONV_K]

    strip = pl.BlockSpec((seq, 128), lambda c: (0, c))
    wsp = pl.BlockSpec((CONV_K, 128), lambda c: (0, c))
    bsp = pl.BlockSpec((1, 128), lambda c: (0, c))
    return pl.pallas_call(
        body, name="conv_bwd", grid=(ch // 128,),
        in_specs=[strip, wsp, bsp, strip],
        out_specs=[strip, wsp, bsp],
        out_shape=[jax.ShapeDtypeStruct(x.shape, F32), jax.ShapeDtypeStruct(w.shape, F32),
                   jax.ShapeDtypeStruct(b.shape, F32)],
        compiler_params=_cp(("parallel",)),
    )(x, w, b, dy)


@jax.custom_vjp
def conv_silu(x, w, b):
    return _conv_fwd_call(x, w, b)


def _conv_silu_fwd(x, w, b):
    return _conv_fwd_call(x, w, b), (x, w, b)


def _conv_silu_bwd(res, dy):
    return tuple(_conv_bwd_call(*res, dy))


conv_silu.defvjp(_conv_silu_fwd, _conv_silu_bwd)


def _loss_call(h, wf, target):
    seq, d = h.shape
    nb = seq // BLK

    def body(h_ref, w_ref, t_ref, loss_ref, dh_ref, dw_ref):
        i = pl.program_id(0)
        live = (i > 0).astype(F32)
        tgt = t_ref[...]

        def fn(hh, ww):
            err = _rms_fn(hh, ww) - tgt
            return 0.5 * live * jnp.sum(jnp.mean(err * err, axis=-1, keepdims=True), axis=0, keepdims=True)

        val, vjp = jax.vjp(fn, h_ref[...], w_ref[...])
        dh, dw = vjp(jnp.ones((1, 1), F32))
        dh_ref[...] = dh

        @pl.when(i == 0)
        def _():
            loss_ref[...] = jnp.zeros_like(loss_ref)
            dw_ref[...] = jnp.zeros_like(dw_ref)

        loss_ref[...] += val
        dw_ref[...] += dw

    return pl.pallas_call(
        body, name="loss_head", grid=(nb,),
        in_specs=[pl.BlockSpec((BLK, d), lambda i: (i, 0)), pl.BlockSpec((1, d), lambda i: (0, 0)),
                  pl.BlockSpec((BLK, d), lambda i: (jnp.maximum(i - 1, 0), 0))],
        out_specs=[pl.BlockSpec((1, 1), lambda i: (0, 0)), pl.BlockSpec((BLK, d), lambda i: (i, 0)),
                   pl.BlockSpec((1, d), lambda i: (0, 0))],
        out_shape=[jax.ShapeDtypeStruct((1, 1), F32), jax.ShapeDtypeStruct(h.shape, F32),
                   jax.ShapeDtypeStruct((1, d), F32)],
        compiler_params=_cp(("arbitrary",)),
    )(h, wf, target)


def _make_loss_head(target):
    @jax.custom_vjp
    def head(h, wf):
        return _loss_call(h, wf, target)[0][0, 0]

    def fwd(h, wf):
        loss, dh, dw = _loss_call(h, wf, target)
        return loss[0, 0], (dh, dw)

    def bwd(res, g):
        return g * res[0], g * res[1]

    head.defvjp(fwd, bwd)
    return head


_IN_SEGS = (("q", 0, 1024), ("k", 1024, 1024), ("v", 2048, 1024), ("gate", 3072, 1024), ("z", 4112, 1024),
            ("xbc", 5136, 2048), ("cq", 7200, 1024), ("ck", 8224, 256), ("cv", 8480, 256), ("gl", 8736, 3072),
            ("b", 4096, 8), ("a", 4104, 8), ("dt", 7184, 16))
_IN_PAD = 96
_SPLIT = (1024, 1024, 1024, 1024, 1024, 2048, 1024, 512, 3072, 128)


@jax.custom_vjp
def split_cols(u):
    offs = [sum(_SPLIT[:i]) for i in range(len(_SPLIT))]
    return tuple(u[:, o:o + s] for o, s in zip(offs, _SPLIT))


def _split_fwd(u):
    return split_cols(u), None


def _split_bwd(_, cts):
    return (jnp.concatenate(cts, axis=1),)


split_cols.defvjp(_split_fwd, _split_bwd)


def _heads(t, nh):
    return t.reshape(t.shape[0], nh, 64).transpose(1, 0, 2)


def _unheads(t):
    return t.transpose(1, 0, 2).reshape(t.shape[1], t.shape[0] * t.shape[2])


def _cols(t, chunk):
    return t.T.reshape(t.shape[1], t.shape[0] // chunk, chunk, 1)


def _layer(h, p, wb, slot):
    u = mm(rms_op(h, p["norm1_w"].reshape(1, -1)), wb["w_in"], slot["w_in"])
    q_pre, k_pre, v_pre, gate, z, xbc_pre, cq, ckv, gl, small = split_cols(u)

    gcw = p["gdn_conv_w"]
    nob = jnp.zeros((1, GDN_H * GDN_D), F32)
    qa = conv_silu(q_pre, gcw[:, :1024], nob)
    ka = conv_silu(k_pre, gcw[:, 1024:2048], nob)
    va = conv_silu(v_pre, gcw[:, 2048:], nob)
    y_gdn = gdn_core(qa, ka, va, gate, _cols(small[:, 8:16], GDN_C), _cols(small[:, 0:8], GDN_C),
                     p["gdn_a_log"].reshape(GDN_H, 1, 1), p["gdn_dt_bias"].reshape(GDN_H, 1, 1),
                     p["gdn_norm_w"].reshape(1, GDN_D))

    xbc = conv_silu(xbc_pre, p["ssd_conv_w"], p["ssd_conv_b"].reshape(1, -1))
    y_ssd = _unheads(ssd_core(_heads(xbc[:, :1024], SSD_H), _heads(z, SSD_H), xbc[:, 1024:1536], xbc[:, 1536:],
                              _cols(small[:, 16:32], BLK), p["ssd_dt_bias"].reshape(SSD_H, 1, 1),
                              p["ssd_a_log"].reshape(SSD_H, 1, 1), p["ssd_d"].reshape(SSD_H, 1, 1),
                              p["ssd_norm_w"].reshape(SSD_H, 1, SSD_P)))

    y_swa = _unheads(swa_core(_heads(cq, SWA_QH), _heads(ckv[:, :256], SWA_KVH), _heads(ckv[:, 256:], SWA_KVH),
                              p["swa_sinks"].reshape(SWA_QH, 1, 1)))

    def proj(t, name):
        return mm(t, wb[name], slot[name])

    merged = merge_op(proj(y_gdn, "w_proj_gdn"), proj(y_ssd, "w_proj_ssd"), proj(y_swa, "w_proj_swa"), gl)
    h = h + proj(merged, "w_out")
    a1 = proj(rms_op(h, p["norm2_w"].reshape(1, -1)), "w_up")
    return h + proj(relu2_op(a1), "w_down")


_MATMUL = ("w_in", "w_proj_gdn", "w_proj_ssd", "w_proj_swa", "w_out", "w_up", "w_down")
_PER_LAYER = ("norm1_w", "gdn_conv_w", "gdn_a_log", "gdn_dt_bias", "gdn_norm_w", "ssd_conv_w", "ssd_conv_b",
              "ssd_dt_bias", "ssd_a_log", "ssd_d", "ssd_norm_w", "swa_sinks", "norm2_w")


def _local_loss(x, params, slots, wb, loss_head):
    h = jnp.concatenate([jnp.zeros((NPAD, D_MODEL), F32), params["meta_tokens"], x], axis=0)
    for l in range(len(wb)):
        h = _layer(h, {n: params[n][l] for n in _PER_LAYER}, wb[l], slots[l])
    return loss_head(h, params["final_norm_w"].reshape(1, -1))


_IN_SHARD = 1476


def _in_pieces():
    out = []
    for _, s, n in _IN_SEGS:
        c = s
        while c < s + n:
            d = c // _IN_SHARD
            e = min(s + n, (d + 1) * _IN_SHARD)
            out.append((d, c - d * _IN_SHARD, e - d * _IN_SHARD))
            c = e
    return out


def _in_pieces_back():
    start, off = {}, 0
    for _, s, n in _IN_SEGS:
        start[s] = off
        off += n
    out = [[] for _ in range(N_DEV)]
    for _, s, n in sorted(_IN_SEGS, key=lambda t: t[1]):
        c = s
        while c < s + n:
            d = c // _IN_SHARD
            e = min(s + n, (d + 1) * _IN_SHARD)
            out[d].append((start[s] + c - s, start[s] + e - s))
            c = e
    return out


def _regroup_w_in(stacked):
    parts = [stacked[d, :, lo:hi] for d, lo, hi in _in_pieces()]
    return jnp.concatenate(parts + [jnp.zeros((D_MODEL, _IN_PAD), stacked.dtype)], axis=1)


def _ungroup_w_in(g):
    return jnp.stack([jnp.concatenate([g[:, lo:hi] for lo, hi in pieces], axis=1) for pieces in _in_pieces_back()])


def _position():
    return lax.axis_index("x"), lax.axis_index("y"), lax.axis_index("c")


_ANY = pl.BlockSpec(memory_space=pl.ANY)


def _chip_of(x, y, k):
    return (1 - x if k & 1 else x, 1 - y if k & 2 else y)


def _allgather_call(shards, name):
    n = len(shards)

    def body(*refs):
        x_refs, out_refs = refs[:n], refs[n:2 * n]
        send_sems, recv_sems, local_sems = refs[2 * n:]
        x, y, c = _position()
        me, sibling = (x, y, c), (x, y, 1 - c)
        chips = [_chip_of(x, y, k) for k in (1, 2, 3)]

        def slab(a, px, py, pc):
            return out_refs[a].at[4 * px + 2 * py + pc]

        def copy(a, k, block, to, src=None):
            return pltpu.make_async_remote_copy(
                src_ref=slab(a, *block) if src is None else src, dst_ref=slab(a, *block),
                send_sem=send_sems.at[7 * a + k], recv_sem=recv_sems.at[7 * a + k], device_id=to, device_id_type=MESH)

        mine = [pltpu.make_async_copy(x_refs[a], slab(a, *me), local_sems.at[a]) for a in range(n)]
        first = []
        for a in range(n):
            mine[a].start()
            first.append(copy(a, 0, me, sibling, src=x_refs[a]))
            first += [copy(a, 1 + j, me, (*chip, c), src=x_refs[a]) for j, chip in enumerate(chips)]
        for cp in first:
            cp.start()
        passed = []
        for j, chip in enumerate(chips):
            for a in range(n):
                copy(a, 1 + j, (*chip, c), me).wait_recv()
                passed.append(copy(a, 4 + j, (*chip, c), sibling))
                passed[-1].start()
        for a in range(n):
            copy(a, 0, sibling, me).wait_recv()
        for j, chip in enumerate(chips):
            for a in range(n):
                copy(a, 4 + j, (*chip, 1 - c), me).wait_recv()
        for cp in first + passed:
            cp.wait_send()
        for cp in mine:
            cp.wait()

    return pl.pallas_call(
        body, name=name,
        out_shape=[jax.ShapeDtypeStruct((N_DEV, *s.shape), s.dtype) for s in shards],
        in_specs=[_ANY] * n, out_specs=[_ANY] * n,
        scratch_shapes=[pltpu.SemaphoreType.DMA((7 * n,)), pltpu.SemaphoreType.DMA((7 * n,)),
                        pltpu.SemaphoreType.DMA((n,))],
    )(*shards)


def _sibling_exchange_call(grads, name):
    n = len(grads)

    def body(*refs):
        g_refs, own_refs, out_refs = refs[:n], refs[n:2 * n], refs[2 * n:3 * n]
        send_sems, recv_sems, local_sems = refs[3 * n:]
        x, y, c = _position()
        copies, local = [], []
        for a in range(n):
            for k in range(4):
                px, py = _chip_of(x, y, k)
                copies.append(pltpu.make_async_remote_copy(
                    src_ref=g_refs[a].at[4 * px + 2 * py + (1 - c)], dst_ref=out_refs[a].at[k],
                    send_sem=send_sems.at[4 * a + k], recv_sem=recv_sems.at[4 * a + k],
                    device_id=(x, y, 1 - c), device_id_type=MESH))
                local.append(pltpu.make_async_copy(g_refs[a].at[4 * px + 2 * py + c], own_refs[a].at[k],
                                                   local_sems.at[4 * a + k]))
        for cp in copies + local:
            cp.start()
        for cp in copies:
            cp.wait_recv()
        for cp in copies:
            cp.wait_send()
        for cp in local:
            cp.wait()

    shapes = [jax.ShapeDtypeStruct((4, *g.shape[1:]), g.dtype) for g in grads]
    res = pl.pallas_call(
        body, name=name,
        out_shape=shapes + shapes,
        in_specs=[_ANY] * n, out_specs=[_ANY] * (2 * n),
        scratch_shapes=[pltpu.SemaphoreType.DMA((4 * n,)), pltpu.SemaphoreType.DMA((4 * n,)),
                        pltpu.SemaphoreType.DMA((4 * n,))],
    )(*grads)
    return res[:n], res[n:]


def _chip_exchange_call(partials, name):
    n = len(partials)

    def body(*refs):
        p_refs, out_refs = refs[:n], refs[n:2 * n]
        send_sems, recv_sems = refs[2 * n:]
        x, y, c = _position()
        copies = []
        for a in range(n):
            for k in (1, 2, 3):
                copies.append(pltpu.make_async_remote_copy(
                    src_ref=p_refs[a].at[k - 1], dst_ref=out_refs[a].at[k - 1],
                    send_sem=send_sems.at[3 * a + k - 1], recv_sem=recv_sems.at[3 * a + k - 1],
                    device_id=(*_chip_of(x, y, k), c), device_id_type=MESH))
        for cp in copies:
            cp.start()
        for cp in copies:
            cp.wait_recv()
        for cp in copies:
            cp.wait_send()

    return pl.pallas_call(
        body, name=name,
        out_shape=[jax.ShapeDtypeStruct(p.shape, p.dtype) for p in partials],
        in_specs=[_ANY] * n, out_specs=[_ANY] * n,
        scratch_shapes=[pltpu.SemaphoreType.DMA((3 * n,)), pltpu.SemaphoreType.DMA((3 * n,))],
    )(*partials)


def _chip_partial_call(own, sib, tr, name):
    _, r, c = own.shape

    def body(g_ref, s_ref, own_ref, out_ref):
        own_ref[...] = g_ref[0] + s_ref[0]
        for k in (1, 2, 3):
            out_ref[k - 1] = (g_ref[k] + s_ref[k]).astype(BF16)

    four = pl.BlockSpec((4, tr, c), lambda i: (0, i, 0))
    return pl.pallas_call(
        body, name=name, grid=(r // tr,),
        in_specs=[four, four],
        out_specs=[pl.BlockSpec((tr, c), lambda i: (i, 0)), pl.BlockSpec((3, tr, c), lambda i: (0, i, 0))],
        out_shape=[jax.ShapeDtypeStruct((r, c), F32), jax.ShapeDtypeStruct((3, r, c), BF16)],
        compiler_params=_cp(("parallel",)),
    )(own, sib)


def _adamw_call(parts, w, m, v, tr, name):
    r, c = w.shape
    npart = len(parts)

    def body(*refs):
        p_refs = refs[:npart]
        w_ref, m_ref, v_ref, g_ref, d_ref, nm_ref, nv_ref = refs[npart:]
        g = None
        for p_ref in p_refs:
            for s in range(p_ref.shape[0]):
                term = p_ref[s].astype(F32)
                g = term if g is None else g + term
        nm = ADAM_B1 * m_ref[...] + (1.0 - ADAM_B1) * g
        nv = ADAM_B2 * v_ref[...] + (1.0 - ADAM_B2) * (g * g)
        m_hat = nm / (1.0 - ADAM_B1 ** ADAM_STEP)
        v_hat = nv / (1.0 - ADAM_B2 ** ADAM_STEP)
        g_ref[...] = g
        d_ref[...] = -ADAM_LR * (m_hat / (jnp.sqrt(v_hat) + ADAM_EPS) + ADAM_WD * w_ref[...])
        nm_ref[...] = nm
        nv_ref[...] = nv

    flat = pl.BlockSpec((tr, c), lambda i: (i, 0))
    return pl.pallas_call(
        body, name=name, grid=(r // tr,),
        in_specs=[pl.BlockSpec((p.shape[0], tr, c), lambda i: (0, i, 0)) for p in parts] + [flat, flat, flat],
        out_specs=[flat] * 4,
        out_shape=[jax.ShapeDtypeStruct((r, c), F32)] * 4,
        compiler_params=_cp(("parallel",)),
    )(*parts, w, m, v)


_WEIGHTS = ("meta_tokens", "norm1_w", "w_in", "gdn_conv_w", "gdn_a_log", "gdn_dt_bias", "gdn_norm_w", "ssd_conv_w",
            "ssd_conv_b", "ssd_dt_bias", "ssd_a_log", "ssd_d", "ssd_norm_w", "swa_sinks", "w_proj_gdn", "w_proj_ssd",
            "w_proj_swa", "w_out", "norm2_w", "w_up", "w_down", "final_norm_w")
_SHARD_AXIS = {"meta_tokens": 1, "w_in": 2, "gdn_conv_w": 2, "ssd_conv_w": 2, "w_proj_gdn": 1, "w_proj_ssd": 1,
               "w_proj_swa": 1, "w_out": 1, "w_up": 2, "w_down": 1}
_BIG = tuple(n for n in _WEIGHTS if n in _SHARD_AXIS)
_SMALL = tuple(n for n in _WEIGHTS if n not in _SHARD_AXIS)
FLAT_C = 1024


def _pack(arrs, rows, lead=()):
    flat = jnp.concatenate([a.reshape(*lead, -1) for a in arrs], axis=-1)
    pad = rows * FLAT_C - flat.shape[-1]
    flat = jnp.pad(flat, [(0, 0)] * len(lead) + [(0, pad)])
    return flat.reshape(*lead, rows, FLAT_C)


def _unpack(flat, shapes, lead=()):
    flat = flat.reshape(*lead, -1)
    out, off = [], 0
    for s in shapes:
        n = math.prod(s)
        out.append(flat[..., off:off + n].reshape(*lead, *s))
        off += n
    return out


def _rows_for(shapes):
    n = sum(math.prod(s) for s in shapes)
    return -(-n // (FLAT_C * 8)) * 8


def _rows_tile(r, c):
    if r <= 256:
        return r
    return 128 if c > 1024 else 256


def _join(stacked, axis):
    moved = jnp.moveaxis(stacked, 0, axis)
    return moved.reshape(*moved.shape[:axis], -1, *moved.shape[axis + 2:])


def kernel(x, meta_tokens, norm1_w, w_in, gdn_conv_w, gdn_a_log, gdn_dt_bias, gdn_norm_w, ssd_conv_w, ssd_conv_b,
           ssd_dt_bias, ssd_a_log, ssd_d, ssd_norm_w, swa_sinks, w_proj_gdn, w_proj_ssd, w_proj_swa, w_out, norm2_w,
           w_up, w_down, final_norm_w, loss_target, m_meta_tokens, m_norm1_w, m_w_in, m_gdn_conv_w, m_gdn_a_log,
           m_gdn_dt_bias, m_gdn_norm_w, m_ssd_conv_w, m_ssd_conv_b, m_ssd_dt_bias, m_ssd_a_log, m_ssd_d, m_ssd_norm_w,
           m_swa_sinks, m_w_proj_gdn, m_w_proj_ssd, m_w_proj_swa, m_w_out, m_norm2_w, m_w_up, m_w_down,
           m_final_norm_w, v_meta_tokens, v_norm1_w, v_w_in, v_gdn_conv_w, v_gdn_a_log, v_gdn_dt_bias, v_gdn_norm_w,
           v_ssd_conv_w, v_ssd_conv_b, v_ssd_dt_bias, v_ssd_a_log, v_ssd_d, v_ssd_norm_w, v_swa_sinks, v_w_proj_gdn,
           v_w_proj_ssd, v_w_proj_swa, v_w_out, v_norm2_w, v_w_up, v_w_down, v_final_norm_w):
    args = (meta_tokens, norm1_w, w_in, gdn_conv_w, gdn_a_log, gdn_dt_bias, gdn_norm_w, ssd_conv_w, ssd_conv_b,
            ssd_dt_bias, ssd_a_log, ssd_d, ssd_norm_w, swa_sinks, w_proj_gdn, w_proj_ssd, w_proj_swa, w_out, norm2_w,
            w_up, w_down, final_norm_w, m_meta_tokens, m_norm1_w, m_w_in, m_gdn_conv_w, m_gdn_a_log,
            m_gdn_dt_bias, m_gdn_norm_w, m_ssd_conv_w, m_ssd_conv_b, m_ssd_dt_bias, m_ssd_a_log, m_ssd_d, m_ssd_norm_w,
            m_swa_sinks, m_w_proj_gdn, m_w_proj_ssd, m_w_proj_swa, m_w_out, m_norm2_w, m_w_up, m_w_down,
            m_final_norm_w, v_meta_tokens, v_norm1_w, v_w_in, v_gdn_conv_w, v_gdn_a_log, v_gdn_dt_bias, v_gdn_norm_w,
            v_ssd_conv_w, v_ssd_conv_b, v_ssd_dt_bias, v_ssd_a_log, v_ssd_d, v_ssd_norm_w, v_swa_sinks, v_w_proj_gdn,
            v_w_proj_ssd, v_w_proj_swa, v_w_out, v_norm2_w, v_w_up, v_w_down, v_final_norm_w)
    nw = len(_WEIGHTS)
    w = dict(zip(_WEIGHTS, args[:nw]))
    m = dict(zip(_WEIGHTS, args[nw:2 * nw]))
    v = dict(zip(_WEIGHTS, args[2 * nw:]))

    depth = w["w_in"].shape[0]
    small_shapes = [w[n].shape for n in _SMALL]
    small_rows = _rows_for(small_shapes)

    def flat2(t):
        return t.reshape(-1, t.shape[-1])

    sent = [flat2(w[n]).astype(BF16) if n in _MATMUL else flat2(w[n]) for n in _BIG]
    gathered = dict(zip(_BIG, _allgather_call(sent, "gather_weights")))

    def layer_of(name, l):
        t = gathered[name]
        return t.reshape(N_DEV, depth, t.shape[1] // depth, t.shape[2])[:, l]

    wb, slots = [], []
    for l in range(depth):
        full = {"w_in": _regroup_w_in(layer_of("w_in", l)),
                "w_up": layer_of("w_up", l).transpose(1, 0, 2).reshape(D_MODEL, D_FF),
                "w_down": layer_of("w_down", l).reshape(D_FF, D_MODEL)}
        for n in ("w_proj_gdn", "w_proj_ssd", "w_proj_swa", "w_out"):
            full[n] = layer_of(n, l).reshape(D_MODEL, D_MODEL)
        wb.append(full)
        slots.append({n: jnp.zeros(t.shape, F32) for n, t in full.items()})

    tiny = {n: gathered[n].reshape(N_DEV, *w[n].shape) for n in _BIG if n not in _MATMUL}
    small = {n: w[n] for n in _SMALL}
    loss_head = _make_loss_head(loss_target[0])

    def local_loss(x_rows, slots, tiny, small):
        joined = {n: _join(tiny[n], _SHARD_AXIS[n]) for n in tiny}
        return _local_loss(x_rows, {**joined, **small}, slots, wb, loss_head)

    loss, (gx, g_slots, g_tiny, g_small) = jax.value_and_grad(local_loss, argnums=(0, 1, 2, 3))(
        x[0], slots, tiny, small)
    loss = lax.psum(loss, ("x", "y", "c"))

    def per_device(name, g):
        if name == "w_in":
            return _ungroup_w_in(g)
        if name == "w_up":
            return g.reshape(D_MODEL, N_DEV, D_FF // N_DEV).transpose(1, 0, 2)
        return g.reshape(N_DEV, g.shape[0] // N_DEV, g.shape[1])

    contrib = []
    for n in _BIG:
        if n in _MATMUL:
            t = jnp.stack([per_device(n, g_slots[l][n]) for l in range(depth)], axis=1)
        else:
            t = g_tiny[n]
        contrib.append(t.reshape(N_DEV, -1, t.shape[-1]))

    mine, from_sibling = _sibling_exchange_call(contrib, "grads_to_sibling")
    own, outgoing = [], []
    for n, g, s in zip(_BIG, mine, from_sibling):
        o, p = _chip_partial_call(g, s, _rows_tile(g.shape[1], g.shape[2]), "chip_partial_" + n)
        own.append(o)
        outgoing.append(p)
    incoming = _chip_exchange_call(outgoing, "grads_to_chips")

    by_name = {}
    for n, o, r in zip(_BIG, own, incoming):
        res = _adamw_call([o[None], r], flat2(w[n]), flat2(m[n]), flat2(v[n]), _rows_tile(o.shape[0], o.shape[1]),
                          "adamw_" + n)
        by_name[n] = [t.reshape(w[n].shape) for t in res]

    small_parts = _allgather_call([_pack([g_small[n] for n in _SMALL], small_rows)], "gather_small_grads")
    small_out = _adamw_call(small_parts, *[_pack([d[n] for n in _SMALL], small_rows) for d in (w, m, v)], small_rows,
                            "adamw_replicated")
    for kind in range(4):
        for n, t in zip(_SMALL, _unpack(small_out[kind], small_shapes)):
            by_name.setdefault(n, [None] * 4)[kind] = t

    outs = [by_name[n][kind] for kind in range(4) for n in _WEIGHTS]
    return (loss, gx[None], *outs)
```

```python
import functools
import math

import jax
import jax.numpy as jnp
from jax import lax
from jax.experimental import pallas as pl
from jax.experimental.pallas import tpu as pltpu

F32 = jnp.float32
BF16 = jnp.bfloat16
HI = lax.Precision.HIGHEST
NEG = -1e30

D_MODEL = 1024
N_META = 16
BLK = 128
NPAD = BLK - N_META
RMS_EPS = 1e-6
L2_EPS = 1e-6
CONV_K = 4

GDN_H, GDN_D, GDN_C = 8, 128, 64
SSD_H, SSD_P, SSD_G, SSD_N = 16, 64, 4, 128
SSD_HPG = SSD_H // SSD_G
SWA_QH, SWA_KVH, SWA_D = 16, 4, 64
SWA_REP = SWA_QH // SWA_KVH
D_FF = 4 * D_MODEL

N_DEV = 8
MESH = pl.DeviceIdType.MESH

ADAM_LR, ADAM_B1, ADAM_B2, ADAM_EPS, ADAM_WD, ADAM_STEP = 0.001, 0.9, 0.999, 1e-08, 0.01, 10

VMEM_LIMIT = 56 * 1024 * 1024


def _cp(sem=None):
    return pltpu.CompilerParams(dimension_semantics=sem, vmem_limit_bytes=VMEM_LIMIT)


def _dot(a, b, ca, cb, prec=HI):
    return lax.dot_general(a, b, (((ca,), (cb,)), ((), ())), precision=prec, preferred_element_type=F32)


def _nn(a, b, prec=HI):
    return _dot(a, b, 1, 0, prec)


def _nt(a, b, prec=HI):
    return _dot(a, b, 1, 1, prec)


def _tn(a, b, prec=HI):
    return _dot(a, b, 0, 0, prec)


def _bdot(a, b, ca, cb):
    return lax.dot_general(a.astype(BF16), b.astype(BF16), (((ca,), (cb,)), ((), ())), preferred_element_type=F32)


@jax.custom_vjp
def _lo_nn(a, b):
    return _bdot(a, b, 1, 0)


_lo_nn.defvjp(lambda a, b: (_bdot(a, b, 1, 0), (a, b)),
              lambda r, d: (_bdot(d, r[1], 1, 1), _bdot(r[0], d, 0, 0)))


@jax.custom_vjp
def _lo_nt(a, b):
    return _bdot(a, b, 1, 1)


_lo_nt.defvjp(lambda a, b: (_bdot(a, b, 1, 1), (a, b)),
              lambda r, d: (_bdot(d, r[1], 1, 0), _bdot(d, r[0], 0, 0)))


@jax.custom_vjp
def _lo_tn(a, b):
    return _bdot(a, b, 0, 0)


_lo_tn.defvjp(lambda a, b: (_bdot(a, b, 0, 0), (a, b)),
              lambda r, d: (_bdot(r[1], d, 1, 1), _bdot(r[0], d, 1, 0)))


def _iota2(n, m, axis):
    return lax.broadcasted_iota(jnp.int32, (n, m), axis)


def _silu(x):
    return x * jax.nn.sigmoid(x)


def _softplus(x):
    return jnp.maximum(x, 0.0) + jnp.log(1.0 + jnp.exp(-jnp.abs(x)))


def _row_of(col):
    n = col.shape[0]
    eye = (_iota2(n, n, 0) == _iota2(n, n, 1)).astype(F32)
    return _nn(jnp.ones((n, n), F32), eye * col)


def _cumsum_col(col):
    n = col.shape[0]
    tril = (_iota2(n, n, 0) >= _iota2(n, n, 1)).astype(F32)
    return _nn(tril, col)


def _tri_inv(a):
    n = a.shape[0]
    r, c = _iota2(n, n, 0), _iota2(n, n, 1)
    eye = (r == c).astype(F32)
    blk = jnp.right_shift(r, 4) == jnp.right_shift(c, 4)
    d = jnp.where(blk, a, 0.0)
    off = a - d
    d2 = _nn(d, d)
    d4 = _nn(d2, d2)
    d8 = _nn(d4, d4)
    td = _nn(_nn(_nn(eye - d, eye + d2), eye + d4), eye + d8)
    m = _nn(td, off)
    m2 = _nn(m, m)
    return _nn(_nn(eye - m, eye + m2), td)


@jax.custom_vjp
def _tri_solve(a, a_t, rhs):
    return _nn(_tri_inv(a), rhs)


def _tri_solve_fwd(a, a_t, rhs):
    sol = _nn(_tri_inv(a), rhs)
    return sol, (a_t, sol)


def _tri_solve_bwd(res, dsol):
    a_t, sol = res
    drhs = _nn(_tri_inv(a_t), dsol)
    return -_nt(drhs, sol), jnp.zeros_like(a_t), drhs


_tri_solve.defvjp(_tri_solve_fwd, _tri_solve_bwd)


def _gdn_chunk(qa, ka, va, gate, a_raw, b_raw, s, a_log, dt_bias, norm_w, valid):
    c = qa.shape[0]
    q = qa * lax.rsqrt(jnp.sum(qa * qa, axis=-1, keepdims=True) + L2_EPS) * (GDN_D ** -0.5)
    k = ka * lax.rsqrt(jnp.sum(ka * ka, axis=-1, keepdims=True) + L2_EPS)
    beta = jax.nn.sigmoid(b_raw)
    g = -jnp.exp(a_log) * _softplus(a_raw + dt_bias) * valid
    gam = _cumsum_col(g)
    gam_row = _row_of(gam)
    r, cc = _iota2(c, c, 0), _iota2(c, c, 1)
    decay = jnp.exp(jnp.where(r >= cc, gam - gam_row, NEG))
    kb = k * beta
    a = jnp.where(r > cc, _lo_nt(kb, k) * decay, 0.0)
    a_t = lax.stop_gradient(jnp.where(cc > r, _bdot(k, kb, 1, 1) * jnp.exp(jnp.where(cc >= r, gam_row - gam, NEG)), 0.0))
    egam = jnp.exp(gam)
    sol = _tri_solve(a, a_t, jnp.concatenate([va * beta, kb * egam], axis=1))
    u = sol[:, :GDN_D]
    w = sol[:, GDN_D:]
    attn = _lo_nt(q, k) * decay
    g_last = jnp.sum(g, axis=0, keepdims=True)
    k_tail = k * jnp.exp(g_last - gam)
    v_new = u - _lo_nn(w, s)
    o = _lo_nn(q * egam, s) + _lo_nn(attn, v_new)
    s_new = s * jnp.exp(g_last) + _lo_tn(k_tail, v_new)
    y = o * lax.rsqrt(jnp.mean(o * o, axis=-1, keepdims=True) + RMS_EPS) * norm_w * _silu(gate)
    return y, s_new


def _valid_col(row0, n):
    return (row0 + _iota2(n, 1, 0) >= NPAD).astype(F32)


GDN_HB = 8


def _gdn_specs(nc, rev):
    ci = (lambda i: nc - 1 - i) if rev else (lambda i: i)
    hb = GDN_HB
    tile = pl.BlockSpec((GDN_C, hb * GDN_D), lambda h, i: (ci(i), h))
    col = pl.BlockSpec((hb, 1, GDN_C, 1), lambda h, i: (h, ci(i), 0, 0))
    scal = pl.BlockSpec((hb, 1, 1), lambda h, i: (h, 0, 0))
    nw = pl.BlockSpec((1, GDN_D), lambda h, i: (0, 0))
    st = pl.BlockSpec((hb, 1, GDN_D, GDN_D), lambda h, i: (h, ci(i), 0, 0))
    return tile, col, scal, nw, st


def _lanes(j):
    return slice(j * GDN_D, (j + 1) * GDN_D)


def _by_head(ref):
    return jnp.stack([ref[:, _lanes(j)] for j in range(GDN_HB)])


def _gdn_fwd_call(q, k, v, gate, a_col, b_col, a_log, dt_bias, norm_w):
    seq = q.shape[0]
    nc = seq // GDN_C
    tile, col, scal, nw, st = _gdn_specs(nc, False)

    def body(q_ref, k_ref, v_ref, g_ref, a_ref, b_ref, al_ref, dt_ref, nw_ref, y_ref, st_ref, s_scr):
        i = pl.program_id(1)

        @pl.when(i == 0)
        def _():
            s_scr[...] = jnp.zeros_like(s_scr)

        s = s_scr[...]
        st_ref[:, 0] = s
        fn = jax.vmap(functools.partial(_gdn_chunk, valid=_valid_col(i * GDN_C, GDN_C)))
        y, s_new = fn(_by_head(q_ref), _by_head(k_ref), _by_head(v_ref), _by_head(g_ref), a_ref[:, 0], b_ref[:, 0], s,
                      al_ref[...], dt_ref[...], jnp.broadcast_to(nw_ref[...], (GDN_HB, 1, GDN_D)))
        for j in range(GDN_HB):
            y_ref[:, _lanes(j)] = y[j]
        s_scr[...] = s_new

    return pl.pallas_call(
        body, name="gdn_fwd", grid=(GDN_H // GDN_HB, nc),
        in_specs=[tile, tile, tile, tile, col, col, scal, scal, nw],
        out_specs=[tile, st],
        out_shape=[jax.ShapeDtypeStruct((seq, GDN_H * GDN_D), F32),
                   jax.ShapeDtypeStruct((GDN_H, nc, GDN_D, GDN_D), F32)],
        scratch_shapes=[pltpu.VMEM((GDN_HB, GDN_D, GDN_D), F32)],
        compiler_params=_cp(("parallel", "arbitrary")),
    )(q, k, v, gate, a_col, b_col, a_log, dt_bias, norm_w)


def _gdn_bwd_call(q, k, v, gate, a_col, b_col, a_log, dt_bias, norm_w, states, dy):
    seq = q.shape[0]
    nc = seq // GDN_C
    tile, col, scal, nw, st = _gdn_specs(nc, True)
    nwh = pl.BlockSpec((GDN_HB, 1, GDN_D), lambda h, i: (h, 0, 0))

    def body(q_ref, k_ref, v_ref, g_ref, a_ref, b_ref, al_ref, dt_ref, nw_ref, st_ref, dy_ref,
             dq_ref, dk_ref, dv_ref, dg_ref, da_ref, db_ref, dal_ref, ddt_ref, dnw_ref, ds_scr):
        i = pl.program_id(1)

        @pl.when(i == 0)
        def _():
            ds_scr[...] = jnp.zeros_like(ds_scr)
            dal_ref[...] = jnp.zeros_like(dal_ref)
            ddt_ref[...] = jnp.zeros_like(ddt_ref)
            dnw_ref[...] = jnp.zeros_like(dnw_ref)

        fn = jax.vmap(functools.partial(_gdn_chunk, valid=_valid_col((nc - 1 - i) * GDN_C, GDN_C)))
        _, vjp = jax.vjp(fn, _by_head(q_ref), _by_head(k_ref), _by_head(v_ref), _by_head(g_ref), a_ref[:, 0],
                         b_ref[:, 0], st_ref[:, 0], al_ref[...], dt_ref[...],
                         jnp.broadcast_to(nw_ref[...], (GDN_HB, 1, GDN_D)))
        dq, dk, dv, dg, da, db, ds, dal, ddt, dnw = vjp((_by_head(dy_ref), ds_scr[...]))
        for j in range(GDN_HB):
            dq_ref[:, _lanes(j)] = dq[j]
            dk_ref[:, _lanes(j)] = dk[j]
            dv_ref[:, _lanes(j)] = dv[j]
            dg_ref[:, _lanes(j)] = dg[j]
        da_ref[:, 0] = da
        db_ref[:, 0] = db
        ds_scr[...] = ds
        dal_ref[...] += dal
        ddt_ref[...] += ddt
        dnw_ref[...] += dnw

    big = jax.ShapeDtypeStruct((seq, GDN_H * GDN_D), F32)
    return pl.pallas_call(
        body, name="gdn_bwd", grid=(GDN_H // GDN_HB, nc),
        in_specs=[tile, tile, tile, tile, col, col, scal, scal, nw, st, tile],
        out_specs=[tile, tile, tile, tile, col, col, scal, scal, nwh],
        out_shape=[big, big, big, big,
                   jax.ShapeDtypeStruct(a_col.shape, F32), jax.ShapeDtypeStruct(b_col.shape, F32),
                   jax.ShapeDtypeStruct((GDN_H, 1, 1), F32), jax.ShapeDtypeStruct((GDN_H, 1, 1), F32),
                   jax.ShapeDtypeStruct((GDN_H, 1, GDN_D), F32)],
        scratch_shapes=[pltpu.VMEM((GDN_HB, GDN_D, GDN_D), F32)],
        compiler_params=_cp(("parallel", "arbitrary")),
    )(q, k, v, gate, a_col, b_col, a_log, dt_bias, norm_w, states, dy)


@jax.custom_vjp
def gdn_core(q, k, v, gate, a_col, b_col, a_log, dt_bias, norm_w):
    return _gdn_fwd_call(q, k, v, gate, a_col, b_col, a_log, dt_bias, norm_w)[0]


def _gdn_core_fwd(q, k, v, gate, a_col, b_col, a_log, dt_bias, norm_w):
    y, states = _gdn_fwd_call(q, k, v, gate, a_col, b_col, a_log, dt_bias, norm_w)
    return y, (q, k, v, gate, a_col, b_col, a_log, dt_bias, norm_w, states)


def _gdn_core_bwd(res, dy):
    dq, dk, dv, dg, da, db, dal, ddt, dnw = _gdn_bwd_call(*res, dy)
    return dq, dk, dv, dg, da, db, dal, ddt, jnp.sum(dnw, axis=0)


gdn_core.defvjp(_gdn_core_fwd, _gdn_core_bwd)


def _ssd_head(x, z, dt_raw, h, dt_bias, a_log, d_skip, bm, cm, cb, valid):
    c = bm.shape[0]
    r, cc = _iota2(c, c, 0), _iota2(c, c, 1)
    dtp = _softplus(dt_raw + dt_bias)
    x = x * valid
    adt = -jnp.exp(a_log) * dtp * valid
    xdt = x * dtp
    acum = _cumsum_col(adt)
    lmat = jnp.exp(jnp.where(r >= cc, acum - _row_of(acum), NEG))
    a_last = jnp.sum(adt, axis=0, keepdims=True)
    y = _lo_nn(cb * lmat, xdt) + _lo_nt(cm * jnp.exp(acum), h) + d_skip * x
    h_new = h * jnp.exp(a_last) + _lo_tn(xdt * jnp.exp(a_last - acum), bm)
    return y * _silu(z), h_new


def _ssd_chunk(xs, z, bm, cm, dt_raw, h, dt_bias, a_log, d_skip, norm_w, valid):
    nh, c, p = xs.shape
    ng = bm.shape[0]
    hpg = nh // ng
    bm = bm * valid
    cm = cm * valid
    cb = jax.vmap(_lo_nt)(cm, bm)
    per_head = lambda t: jnp.repeat(t, hpg, axis=0)
    ys, hs = jax.vmap(functools.partial(_ssd_head, valid=valid))(
        xs, z, dt_raw, h, dt_bias, a_log, d_skip, per_head(bm), per_head(cm), per_head(cb))
    ss = jnp.sum(jnp.sum(ys * ys, axis=-1, keepdims=True).reshape(ng, hpg, c, 1), axis=1, keepdims=True)
    rstd = lax.rsqrt(ss / (hpg * p) + RMS_EPS)
    return (ys.reshape(ng, hpg, c, p) * rstd).reshape(nh, c, p) * norm_w, hs


SSD_GB = 4
SSD_HB = SSD_GB * SSD_HPG


def _ssd_specs(nc, rev):
    ci = (lambda i: nc - 1 - i) if rev else (lambda i: i)
    hp = SSD_HB
    head = pl.BlockSpec((hp, BLK, SSD_P), lambda g, i: (g, ci(i), 0))
    grp = pl.BlockSpec((BLK, SSD_GB * SSD_N), lambda g, i: (ci(i), g))
    col = pl.BlockSpec((hp, 1, BLK, 1), lambda g, i: (g, ci(i), 0, 0))
    scal = pl.BlockSpec((hp, 1, 1), lambda g, i: (g, 0, 0))
    nw = pl.BlockSpec((hp, 1, SSD_P), lambda g, i: (g, 0, 0))
    st = pl.BlockSpec((hp, 1, SSD_P, SSD_N), lambda g, i: (g, ci(i), 0, 0))
    return head, grp, col, scal, nw, st


def _by_group(ref):
    return jnp.stack([ref[:, j * SSD_N:(j + 1) * SSD_N] for j in range(SSD_GB)])


def _ssd_fwd_call(xs, z, bm, cm, dt_col, dt_bias, a_log, d_skip, norm_w):
    seq = xs.shape[1]
    nc = seq // BLK
    head, grp, col, scal, nw, st = _ssd_specs(nc, False)

    def body(x_ref, z_ref, b_ref, c_ref, dt_ref, db_ref, al_ref, ds_ref, nw_ref, y_ref, st_ref, h_scr):
        i = pl.program_id(1)

        @pl.when(i == 0)
        def _():
            h_scr[...] = jnp.zeros_like(h_scr)

        h = h_scr[...]
        st_ref[:, 0] = h
        y, h_new = _ssd_chunk(x_ref[...], z_ref[...], _by_group(b_ref), _by_group(c_ref), dt_ref[:, 0], h, db_ref[...],
                              al_ref[...], ds_ref[...], nw_ref[...], _valid_col(i * BLK, BLK))
        y_ref[...] = y
        h_scr[...] = h_new

    return pl.pallas_call(
        body, name="ssd_fwd", grid=(SSD_G // SSD_GB, nc),
        in_specs=[head, head, grp, grp, col, scal, scal, scal, nw],
        out_specs=[head, st],
        out_shape=[jax.ShapeDtypeStruct((SSD_H, seq, SSD_P), F32),
                   jax.ShapeDtypeStruct((SSD_H, nc, SSD_P, SSD_N), F32)],
        scratch_shapes=[pltpu.VMEM((SSD_HB, SSD_P, SSD_N), F32)],
        compiler_params=_cp(("parallel", "arbitrary")),
    )(xs, z, bm, cm, dt_col, dt_bias, a_log, d_skip, norm_w)


def _ssd_bwd_call(xs, z, bm, cm, dt_col, dt_bias, a_log, d_skip, norm_w, states, dy):
    seq = xs.shape[1]
    nc = seq // BLK
    head, grp, col, scal, nw, st = _ssd_specs(nc, True)

    def body(x_ref, z_ref, b_ref, c_ref, dt_ref, db_ref, al_ref, ds_ref, nw_ref, st_ref, dy_ref,
             dx_ref, dz_ref, dbm_ref, dcm_ref, ddt_ref, ddb_ref, dal_ref, dds_ref, dnw_ref, dh_scr):
        i = pl.program_id(1)

        @pl.when(i == 0)
        def _():
            dh_scr[...] = jnp.zeros_like(dh_scr)
            ddb_ref[...] = jnp.zeros_like(ddb_ref)
            dal_ref[...] = jnp.zeros_like(dal_ref)
            dds_ref[...] = jnp.zeros_like(dds_ref)
            dnw_ref[...] = jnp.zeros_like(dnw_ref)

        fn = functools.partial(_ssd_chunk, valid=_valid_col((nc - 1 - i) * BLK, BLK))
        _, vjp = jax.vjp(fn, x_ref[...], z_ref[...], _by_group(b_ref), _by_group(c_ref), dt_ref[:, 0], st_ref[:, 0],
                         db_ref[...], al_ref[...], ds_ref[...], nw_ref[...])
        dx, dz, dbm, dcm, ddt, dh, ddb, dal, dds, dnw = vjp((dy_ref[...], dh_scr[...]))
        dx_ref[...] = dx
        dz_ref[...] = dz
        for j in range(SSD_GB):
            dbm_ref[:, j * SSD_N:(j + 1) * SSD_N] = dbm[j]
            dcm_ref[:, j * SSD_N:(j + 1) * SSD_N] = dcm[j]
        ddt_ref[:, 0] = ddt
        dh_scr[...] = dh
        ddb_ref[...] += ddb
        dal_ref[...] += dal
        dds_ref[...] += dds
        dnw_ref[...] += dnw

    hshape = jax.ShapeDtypeStruct(xs.shape, F32)
    gshape = jax.ShapeDtypeStruct(bm.shape, F32)
    sshape = jax.ShapeDtypeStruct((SSD_H, 1, 1), F32)
    return pl.pallas_call(
        body, name="ssd_bwd", grid=(SSD_G // SSD_GB, nc),
        in_specs=[head, head, grp, grp, col, scal, scal, scal, nw, st, head],
        out_specs=[head, head, grp, grp, col, scal, scal, scal, nw],
        out_shape=[hshape, hshape, gshape, gshape, jax.ShapeDtypeStruct(dt_col.shape, F32),
                   sshape, sshape, sshape, jax.ShapeDtypeStruct((SSD_H, 1, SSD_P), F32)],
        scratch_shapes=[pltpu.VMEM((SSD_HB, SSD_P, SSD_N), F32)],
        compiler_params=_cp(("parallel", "arbitrary")),
    )(xs, z, bm, cm, dt_col, dt_bias, a_log, d_skip, norm_w, states, dy)


@jax.custom_vjp
def ssd_core(xs, z, bm, cm, dt_col, dt_bias, a_log, d_skip, norm_w):
    return _ssd_fwd_call(xs, z, bm, cm, dt_col, dt_bias, a_log, d_skip, norm_w)[0]


def _ssd_core_fwd(*args):
    y, states = _ssd_fwd_call(*args)
    return y, (*args, states)


def _ssd_core_bwd(res, dy):
    return tuple(_ssd_bwd_call(*res, dy))


ssd_core.defvjp(_ssd_core_fwd, _ssd_core_bwd)


def _swa_block(q, km, kp, kc, vm, vp, vc, sink, n):
    rows = SWA_REP * BLK
    qs = q.reshape(rows, SWA_D) * (SWA_D ** -0.5)
    s = _lo_nt(qs, jnp.concatenate([km, kp, kc], axis=0))
    i = jnp.bitwise_and(_iota2(rows, 3 * BLK, 0), BLK - 1)
    col = _iota2(rows, 3 * BLK, 1)
    j = jnp.bitwise_and(col, BLK - 1)
    part = jnp.right_shift(col, 7)
    ok_m = (part == 0) & (j >= NPAD) & ((n >= 1) | (j <= i))
    ok_p = (part == 1) & (n >= 2) & (j > i)
    ok_c = (part == 2) & (n >= 1) & (j <= i)
    ok = ok_m | ok_p | ok_c
    s = jnp.where(ok, s, NEG)
    snk = jnp.concatenate([jnp.broadcast_to(sink[r], (BLK, 1)) for r in range(SWA_REP)], axis=0)
    m = lax.stop_gradient(jnp.maximum(jnp.max(s, axis=-1, keepdims=True), snk))
    e = jnp.exp(s - m)
    p = e / (jnp.sum(e, axis=-1, keepdims=True) + jnp.exp(snk - m))
    o = _lo_nn(p, jnp.concatenate([vm, vp, vc], axis=0))
    return o.reshape(SWA_REP, BLK, SWA_D)


def _swa_specs(nb, rev):
    ci = (lambda i: nb - 1 - i) if rev else (lambda i: i)
    qsp = pl.BlockSpec((SWA_REP, BLK, SWA_D), lambda g, i: (g, ci(i), 0))
    cur = pl.BlockSpec((1, BLK, SWA_D), lambda g, i: (g, ci(i), 0))
    prev = pl.BlockSpec((1, BLK, SWA_D), lambda g, i: (g, jnp.maximum(ci(i) - 1, 0), 0))
    meta = pl.BlockSpec((1, BLK, SWA_D), lambda g, i: (g, 0, 0))
    scal = pl.BlockSpec((SWA_REP, 1, 1), lambda g, i: (g, 0, 0))
    return qsp, cur, prev, meta, scal


def _swa_fwd_call(q, k, v, sink):
    seq = q.shape[1]
    nb = seq // BLK
    qsp, cur, prev, meta, scal = _swa_specs(nb, False)

    def body(q_ref, km_ref, kp_ref, kc_ref, vm_ref, vp_ref, vc_ref, s_ref, o_ref):
        o_ref[...] = _swa_block(q_ref[...], km_ref[0], kp_ref[0], kc_ref[0], vm_ref[0], vp_ref[0], vc_ref[0],
                                s_ref[...], pl.program_id(1))

    return pl.pallas_call(
        body, name="swa_fwd", grid=(SWA_KVH, nb),
        in_specs=[qsp, meta, prev, cur, meta, prev, cur, scal],
        out_specs=qsp,
        out_shape=jax.ShapeDtypeStruct(q.shape, F32),
        compiler_params=_cp(("parallel", "arbitrary")),
    )(q, k, k, k, v, v, v, sink)


def _swa_bwd_call(q, k, v, sink, do):
    seq = q.shape[1]
    nb = seq // BLK
    qsp, cur, prev, meta, scal = _swa_specs(nb, True)

    def body(q_ref, km_ref, kp_ref, kc_ref, vm_ref, vp_ref, vc_ref, s_ref, do_ref,
             dq_ref, dk_ref, dv_ref, ds_ref, kp_scr, vp_scr, km_scr, vm_scr):
        i = pl.program_id(1)
        n = nb - 1 - i

        @pl.when(i == 0)
        def _():
            for scr in (kp_scr, vp_scr, km_scr, vm_scr):
                scr[...] = jnp.zeros_like(scr)
            ds_ref[...] = jnp.zeros_like(ds_ref)

        fn = functools.partial(_swa_block, n=n)
        _, vjp = jax.vjp(fn, q_ref[...], km_ref[0], kp_ref[0], kc_ref[0], vm_ref[0], vp_ref[0], vc_ref[0], s_ref[...])
        dq, dkm, dkp, dkc, dvm, dvp, dvc, dsk = vjp(do_ref[...])
        dq_ref[...] = dq
        ds_ref[...] += dsk
        km_scr[...] += dkm
        vm_scr[...] += dvm
        first = (n == 0).astype(F32)
        dk_ref[0] = dkc + kp_scr[...] + first * km_scr[...]
        dv_ref[0] = dvc + vp_scr[...] + first * vm_scr[...]
        kp_scr[...] = dkp
        vp_scr[...] = dvp

    kv = jax.ShapeDtypeStruct(k.shape, F32)
    return pl.pallas_call(
        body, name="swa_bwd", grid=(SWA_KVH, nb),
        in_specs=[qsp, meta, prev, cur, meta, prev, cur, scal, qsp],
        out_specs=[qsp, cur, cur, scal],
        out_shape=[jax.ShapeDtypeStruct(q.shape, F32), kv, kv, jax.ShapeDtypeStruct(sink.shape, F32)],
        scratch_shapes=[pltpu.VMEM((BLK, SWA_D), F32)] * 4,
        compiler_params=_cp(("parallel", "arbitrary")),
    )(q, k, k, k, v, v, v, sink, do)


@jax.custom_vjp
def swa_core(q, k, v, sink):
    return _swa_fwd_call(q, k, v, sink)


def _swa_core_fwd(q, k, v, sink):
    return _swa_fwd_call(q, k, v, sink), (q, k, v, sink)


def _swa_core_bwd(res, do):
    return tuple(_swa_bwd_call(*res, do))


swa_core.defvjp(_swa_core_fwd, _swa_core_bwd)


def _tile(n, pref):
    if n <= pref:
        return n
    best = None
    for t in range(128, pref + 1, 128):
        if n % t == 0:
            best = t
    assert best is not None, (n, pref)
    return best


def _mm_call(a, b, mode, name):
    if mode == "nn":
        (m, kk), n = a.shape, b.shape[1]
    elif mode == "nt":
        (m, kk), n = a.shape, b.shape[0]
    else:
        (kk, m), n = a.shape, b.shape[1]
    tm, tn, tk = _tile(m, 1408), _tile(n, 512), _tile(kk, 1408)
    nk = kk // tk
    if mode == "nn":
        a_spec = pl.BlockSpec((tm, tk), lambda i, j, k: (i, k))
        b_spec = pl.BlockSpec((tk, tn), lambda i, j, k: (k, j))
        dims = (((1,), (0,)), ((), ()))
    elif mode == "nt":
        a_spec = pl.BlockSpec((tm, tk), lambda i, j, k: (i, k))
        b_spec = pl.BlockSpec((tn, tk), lambda i, j, k: (j, k))
        dims = (((1,), (1,)), ((), ()))
    else:
        a_spec = pl.BlockSpec((tk, tm), lambda i, j, k: (k, i))
        b_spec = pl.BlockSpec((tk, tn), lambda i, j, k: (k, j))
        dims = (((0,), (0,)), ((), ()))

    def body(a_ref, b_ref, o_ref, acc_ref):
        k = pl.program_id(2)
        part = lax.dot_general(a_ref[...].astype(BF16), b_ref[...].astype(BF16), dims, preferred_element_type=F32)

        @pl.when(k == 0)
        def _():
            acc_ref[...] = part

        @pl.when(k > 0)
        def _():
            acc_ref[...] += part

        @pl.when(k == nk - 1)
        def _():
            o_ref[...] = acc_ref[...]

    return pl.pallas_call(
        body, name=name, grid=(m // tm, n // tn, nk),
        in_specs=[a_spec, b_spec],
        out_specs=pl.BlockSpec((tm, tn), lambda i, j, k: (i, j)),
        out_shape=jax.ShapeDtypeStruct((m, n), F32),
        scratch_shapes=[pltpu.VMEM((tm, tn), F32)],
        compiler_params=_cp(("parallel", "parallel", "arbitrary")),
    )(a, b)


@jax.custom_vjp
def mm(a, b, grad_slot):
    return _mm_call(a, b, "nn", "mm_nn")


def _mm_fwd(a, b, grad_slot):
    return _mm_call(a, b, "nn", "mm_nn"), (a, b)


def _mm_bwd(res, dc):
    a, b = res
    return _mm_call(dc, b, "nt", "mm_nt"), jnp.zeros_like(b), _mm_call(a, dc, "tn", "mm_tn")


mm.defvjp(_mm_fwd, _mm_bwd)


def _row_specs(arrs, tr):
    return [pl.BlockSpec((tr, a.shape[1]), lambda i: (i, 0)) for a in arrs]


def _par_specs(arrs):
    return [pl.BlockSpec(a.shape, lambda i: (0, 0)) for a in arrs]


def _row_fwd_call(fn, rows, params, out_cols, tr, name):
    seq = rows[0].shape[0]
    nr = len(rows)

    def body(*refs):
        vals = [r[...] for r in refs[:-1]]
        refs[-1][...] = fn(*vals)

    return pl.pallas_call(
        body, name=name, grid=(seq // tr,),
        in_specs=_row_specs(rows, tr) + _par_specs(params),
        out_specs=pl.BlockSpec((tr, out_cols), lambda i: (i, 0)),
        out_shape=jax.ShapeDtypeStruct((seq, out_cols), F32),
        compiler_params=_cp(("parallel",)),
    )(*rows, *params)


def _row_bwd_call(fn, rows, params, dy, tr, name):
    seq = rows[0].shape[0]
    nr, npar = len(rows), len(params)

    def body(*refs):
        ins = refs[:nr + npar]
        dy_ref = refs[nr + npar]
        outs = refs[nr + npar + 1:]
        _, vjp = jax.vjp(fn, *[r[...] for r in ins])
        cts = vjp(dy_ref[...])
        for o_ref, ct in zip(outs[:nr], cts[:nr]):
            o_ref[...] = ct

        @pl.when(pl.program_id(0) == 0)
        def _():
            for o_ref in outs[nr:]:
                o_ref[...] = jnp.zeros_like(o_ref)

        for o_ref, ct in zip(outs[nr:], cts[nr:]):
            o_ref[...] += ct

    return pl.pallas_call(
        body, name=name, grid=(seq // tr,),
        in_specs=_row_specs(rows, tr) + _par_specs(params) + _row_specs([dy], tr),
        out_specs=_row_specs(rows, tr) + _par_specs(params),
        out_shape=[jax.ShapeDtypeStruct(a.shape, F32) for a in (*rows, *params)],
        compiler_params=_cp(("arbitrary",)),
    )(*rows, *params, dy)


def _make_rowop(fn, nrows, out_cols, tr, name):
    @jax.custom_vjp
    def op(*args):
        return _row_fwd_call(fn, args[:nrows], args[nrows:], out_cols, tr, name + "_fwd")

    def fwd(*args):
        return op(*args), args

    def bwd(args, dy):
        return tuple(_row_bwd_call(fn, args[:nrows], args[nrows:], dy, tr, name + "_bwd"))

    op.defvjp(fwd, bwd)
    return op


def _rms_fn(x, w):
    return x * lax.rsqrt(jnp.mean(x * x, axis=-1, keepdims=True) + RMS_EPS) * w


def _merge_fn(pa, pb, pc, gl):
    d = D_MODEL
    return (jax.nn.sigmoid(gl[:, :d]) * pa + jax.nn.sigmoid(gl[:, d:2 * d]) * pb
            + jax.nn.sigmoid(gl[:, 2 * d:]) * pc)


def _relu2_fn(a):
    r = jnp.maximum(a, 0.0)
    return r * r


rms_op = _make_rowop(_rms_fn, 1, D_MODEL, 384, "rms")
merge_op = _make_rowop(_merge_fn, 4, D_MODEL, 192, "merge")
relu2_op = _make_rowop(_relu2_fn, 1, D_FF, 192, "relu2")


def _conv_taps(xext, w, nrows):
    z = None
    for j in range(CONV_K):
        sh = CONV_K - 1 - j
        xs = pltpu.roll(xext, sh, 0) if sh else xext
        term = w[j:j + 1, :] * xs[8:8 + nrows, :]
        z = term if z is None else z + term
    return z


def _halo(ref, start, ok):
    return jnp.where(ok, ref[pl.ds(pl.multiple_of(start, 8), 8), :], 0.0)


def _conv_fwd_call(x, w, b):
    seq, ch = x.shape
    nb = seq // BLK

    def body(x_ref, w_ref, b_ref, o_ref):
        w = w_ref[...]
        bias = b_ref[...]

        def step(i, carry):
            r0 = pl.multiple_of(i * BLK, BLK)
            xext = jnp.concatenate([_halo(x_ref, jnp.maximum(r0 - 8, 0), i > 0), x_ref[pl.ds(r0, BLK), :]], axis=0)
            o_ref[pl.ds(r0, BLK), :] = _silu(_conv_taps(xext, w, BLK) + bias)
            return carry

        lax.fori_loop(0, nb, step, 0)

    strip = pl.BlockSpec((seq, 128), lambda c: (0, c))
    return pl.pallas_call(
        body, name="conv_fwd", grid=(ch // 128,),
        in_specs=[strip, pl.BlockSpec((CONV_K, 128), lambda c: (0, c)), pl.BlockSpec((1, 128), lambda c: (0, c))],
        out_specs=strip, out_shape=jax.ShapeDtypeStruct(x.shape, F32),
        compiler_params=_cp(("parallel",)),
    )(x, w, b)


def _conv_bwd_call(x, w, b, dy):
    seq, ch = x.shape
    nb = seq // BLK

    def body(x_ref, w_ref, b_ref, dy_ref, dx_ref, dw_ref, db_ref):
        w = w_ref[...]
        bias = b_ref[...]

        def step(i, carry):
            r0 = pl.multiple_of(i * BLK, BLK)
            last = i == nb - 1
            nxt = jnp.minimum(r0 + BLK, seq - 8)
            xext = jnp.concatenate([_halo(x_ref, jnp.maximum(r0 - 8, 0), i > 0), x_ref[pl.ds(r0, BLK), :],
                                    _halo(x_ref, nxt, jnp.logical_not(last))], axis=0)
            dyext = jnp.concatenate([dy_ref[pl.ds(r0, BLK), :], _halo(dy_ref, nxt, jnp.logical_not(last))], axis=0)
            z = _conv_taps(xext, w, BLK + 8) + bias
            sg = jax.nn.sigmoid(z)
            dz = dyext * (sg * (1.0 + z * (1.0 - sg)))
            dx = None
            for j in range(CONV_K):
                sh = CONV_K - 1 - j
                dzs = pltpu.roll(dz, BLK + 8 - sh, 0) if sh else dz
                term = w[j:j + 1, :] * dzs[:BLK, :]
                dx = term if dx is None else dx + term
            dx_ref[pl.ds(r0, BLK), :] = dx
            dzm = dz[:BLK, :]
            out = []
            for j in range(CONV_K):
                sh = CONV_K - 1 - j
                xs = pltpu.roll(xext, sh, 0) if sh else xext
                out.append(carry[j] + jnp.sum(dzm * xs[8:8 + BLK, :], axis=0, keepdims=True))
            out.append(carry[CONV_K] + jnp.sum(dzm, axis=0, keepdims=True))
            return tuple(out)

        zero = jnp.zeros((1, 128), F32)
        acc = lax.fori_loop(0, nb, step, (zero,) * (CONV_K + 1))
        dw_ref[...] = jnp.concatenate(acc[:CONV_K], axis=0)
        db_ref[...] = acc[CONV_K]

    strip = pl.BlockSpec((seq, 128), lambda c: (0, c))
    wsp = pl.BlockSpec((CONV_K, 128), lambda c: (0, c))
    bsp = pl.BlockSpec((1, 128), lambda c: (0, c))
    return pl.pallas_call(
        body, name="conv_bwd", grid=(ch // 128,),
        in_specs=[strip, wsp, bsp, strip],
        out_specs=[strip, wsp, bsp],
        out_shape=[jax.ShapeDtypeStruct(x.shape, F32), jax.ShapeDtypeStruct(w.shape, F32),
                   jax.ShapeDtypeStruct(b.shape, F32)],
        compiler_params=_cp(("parallel",)),
    )(x, w, b, dy)


@jax.custom_vjp
def conv_silu(x, w, b):
    return _conv_fwd_call(x, w, b)


def _conv_silu_fwd(x, w, b):
    return _conv_fwd_call(x, w, b), (x, w, b)


def _conv_silu_bwd(res, dy):
    return tuple(_conv_bwd_call(*res, dy))


conv_silu.defvjp(_conv_silu_fwd, _conv_silu_bwd)


def _loss_call(h, wf, target):
    seq, d = h.shape
    nb = seq // BLK

    def body(h_ref, w_ref, t_ref, loss_ref, dh_ref, dw_ref):
        i = pl.program_id(0)
        live = (i > 0).astype(F32)
        tgt = t_ref[...]

        def fn(hh, ww):
            err = _rms_fn(hh, ww) - tgt
            return 0.5 * live * jnp.sum(jnp.mean(err * err, axis=-1, keepdims=True), axis=0, keepdims=True)

        val, vjp = jax.vjp(fn, h_ref[...], w_ref[...])
        dh, dw = vjp(jnp.ones((1, 1), F32))
        dh_ref[...] = dh

        @pl.when(i == 0)
        def _():
            loss_ref[...] = jnp.zeros_like(loss_ref)
            dw_ref[...] = jnp.zeros_like(dw_ref)

        loss_ref[...] += val
        dw_ref[...] += dw

    return pl.pallas_call(
        body, name="loss_head", grid=(nb,),
        in_specs=[pl.BlockSpec((BLK, d), lambda i: (i, 0)), pl.BlockSpec((1, d), lambda i: (0, 0)),
                  pl.BlockSpec((BLK, d), lambda i: (jnp.maximum(i - 1, 0), 0))],
        out_specs=[pl.BlockSpec((1, 1), lambda i: (0, 0)), pl.BlockSpec((BLK, d), lambda i: (i, 0)),
                   pl.BlockSpec((1, d), lambda i: (0, 0))],
        out_shape=[jax.ShapeDtypeStruct((1, 1), F32), jax.ShapeDtypeStruct(h.shape, F32),
                   jax.ShapeDtypeStruct((1, d), F32)],
        compiler_params=_cp(("arbitrary",)),
    )(h, wf, target)


def _make_loss_head(target):
    @jax.custom_vjp
    def head(h, wf):
        return _loss_call(h, wf, target)[0][0, 0]

    def fwd(h, wf):
        loss, dh, dw = _loss_call(h, wf, target)
        return loss[0, 0], (dh, dw)

    def bwd(res, g):
        return g * res[0], g * res[1]

    head.defvjp(fwd, bwd)
    return head


_IN_SEGS = (("q", 0, 1024), ("k", 1024, 1024), ("v", 2048, 1024), ("gate", 3072, 1024), ("z", 4112, 1024),
            ("xbc", 5136, 2048), ("cq", 7200, 1024), ("ck", 8224, 256), ("cv", 8480, 256), ("gl", 8736, 3072),
            ("b", 4096, 8), ("a", 4104, 8), ("dt", 7184, 16))
_IN_PAD = 96
_SPLIT = (1024, 1024, 1024, 1024, 1024, 2048, 1024, 512, 3072, 128)


@jax.custom_vjp
def split_cols(u):
    offs = [sum(_SPLIT[:i]) for i in range(len(_SPLIT))]
    return tuple(u[:, o:o + s] for o, s in zip(offs, _SPLIT))


def _split_fwd(u):
    return split_cols(u), None


def _split_bwd(_, cts):
    return (jnp.concatenate(cts, axis=1),)


split_cols.defvjp(_split_fwd, _split_bwd)


def _heads(t, nh):
    return t.reshape(t.shape[0], nh, 64).transpose(1, 0, 2)


def _unheads(t):
    return t.transpose(1, 0, 2).reshape(t.shape[1], t.shape[0] * t.shape[2])


def _cols(t, chunk):
    return t.T.reshape(t.shape[1], t.shape[0] // chunk, chunk, 1)


def _layer(h, p, wb, slot):
    u = mm(rms_op(h, p["norm1_w"].reshape(1, -1)), wb["w_in"], slot["w_in"])
    q_pre, k_pre, v_pre, gate, z, xbc_pre, cq, ckv, gl, small = split_cols(u)

    gcw = p["gdn_conv_w"]
    nob = jnp.zeros((1, GDN_H * GDN_D), F32)
    qa = conv_silu(q_pre, gcw[:, :1024], nob)
    ka = conv_silu(k_pre, gcw[:, 1024:2048], nob)
    va = conv_silu(v_pre, gcw[:, 2048:], nob)
    y_gdn = gdn_core(qa, ka, va, gate, _cols(small[:, 8:16], GDN_C), _cols(small[:, 0:8], GDN_C),
                     p["gdn_a_log"].reshape(GDN_H, 1, 1), p["gdn_dt_bias"].reshape(GDN_H, 1, 1),
                     p["gdn_norm_w"].reshape(1, GDN_D))

    xbc = conv_silu(xbc_pre, p["ssd_conv_w"], p["ssd_conv_b"].reshape(1, -1))
    y_ssd = _unheads(ssd_core(_heads(xbc[:, :1024], SSD_H), _heads(z, SSD_H), xbc[:, 1024:1536], xbc[:, 1536:],
                              _cols(small[:, 16:32], BLK), p["ssd_dt_bias"].reshape(SSD_H, 1, 1),
                              p["ssd_a_log"].reshape(SSD_H, 1, 1), p["ssd_d"].reshape(SSD_H, 1, 1),
                              p["ssd_norm_w"].reshape(SSD_H, 1, SSD_P)))

    y_swa = _unheads(swa_core(_heads(cq, SWA_QH), _heads(ckv[:, :256], SWA_KVH), _heads(ckv[:, 256:], SWA_KVH),
                              p["swa_sinks"].reshape(SWA_QH, 1, 1)))

    def proj(t, name):
        return mm(t, wb[name], slot[name])

    merged = merge_op(proj(y_gdn, "w_proj_gdn"), proj(y_ssd, "w_proj_ssd"), proj(y_swa, "w_proj_swa"), gl)
    h = h + proj(merged, "w_out")
    a1 = proj(rms_op(h, p["norm2_w"].reshape(1, -1)), "w_up")
    return h + proj(relu2_op(a1), "w_down")


_MATMUL = ("w_in", "w_proj_gdn", "w_proj_ssd", "w_proj_swa", "w_out", "w_up", "w_down")
_PER_LAYER = ("norm1_w", "gdn_conv_w", "gdn_a_log", "gdn_dt_bias", "gdn_norm_w", "ssd_conv_w", "ssd_conv_b",
              "ssd_dt_bias", "ssd_a_log", "ssd_d", "ssd_norm_w", "swa_sinks", "norm2_w")


def _local_loss(x, params, slots, wb, loss_head):
    h = jnp.concatenate([jnp.zeros((NPAD, D_MODEL), F32), params["meta_tokens"], x], axis=0)
    for l in range(len(wb)):
        h = _layer(h, {n: params[n][l] for n in _PER_LAYER}, wb[l], slots[l])
    return loss_head(h, params["final_norm_w"].reshape(1, -1))


_IN_SHARD = 1476


def _in_pieces():
    out = []
    for _, s, n in _IN_SEGS:
        c = s
        while c < s + n:
            d = c // _IN_SHARD
            e = min(s + n, (d + 1) * _IN_SHARD)
            out.append((d, c - d * _IN_SHARD, e - d * _IN_SHARD))
            c = e
    return out


def _in_pieces_back():
    start, off = {}, 0
    for _, s, n in _IN_SEGS:
        start[s] = off
        off += n
    out = [[] for _ in range(N_DEV)]
    for _, s, n in sorted(_IN_SEGS, key=lambda t: t[1]):
        c = s
        while c < s + n:
            d = c // _IN_SHARD
            e = min(s + n, (d + 1) * _IN_SHARD)
            out[d].append((start[s] + c - s, start[s] + e - s))
            c = e
    return out


def _regroup_w_in(stacked):
    parts = [stacked[d, :, lo:hi] for d, lo, hi in _in_pieces()]
    return jnp.concatenate(parts + [jnp.zeros((D_MODEL, _IN_PAD), stacked.dtype)], axis=1)


def _ungroup_w_in(g):
    return jnp.stack([jnp.concatenate([g[:, lo:hi] for lo, hi in pieces], axis=1) for pieces in _in_pieces_back()])


def _position():
    return lax.axis_index("x"), lax.axis_index("y"), lax.axis_index("c")


_ANY = pl.BlockSpec(memory_space=pl.ANY)


def _chip_of(x, y, k):
    return (1 - x if k & 1 else x, 1 - y if k & 2 else y)


def _allgather_call(shards, name):
    n = len(shards)

    def body(*refs):
        x_refs, out_refs = refs[:n], refs[n:2 * n]
        send_sems, recv_sems, local_sems = refs[2 * n:]
        x, y, c = _position()
        me, sibling = (x, y, c), (x, y, 1 - c)
        chips = [_chip_of(x, y, k) for k in (1, 2, 3)]

        def slab(a, px, py, pc):
            return out_refs[a].at[4 * px + 2 * py + pc]

        def copy(a, k, block, to, src=None):
            return pltpu.make_async_remote_copy(
                src_ref=slab(a, *block) if src is None else src, dst_ref=slab(a, *block),
                send_sem=send_sems.at[7 * a + k], recv_sem=recv_sems.at[7 * a + k], device_id=to, device_id_type=MESH)

        mine = [pltpu.make_async_copy(x_refs[a], slab(a, *me), local_sems.at[a]) for a in range(n)]
        first = []
        for a in range(n):
            mine[a].start()
            first.append(copy(a, 0, me, sibling, src=x_refs[a]))
            first += [copy(a, 1 + j, me, (*chip, c), src=x_refs[a]) for j, chip in enumerate(chips)]
        for cp in first:
            cp.start()
        passed = []
        for j, chip in enumerate(chips):
            for a in range(n):
                copy(a, 1 + j, (*chip, c), me).wait_recv()
                passed.append(copy(a, 4 + j, (*chip, c), sibling))
                passed[-1].start()
        for a in range(n):
            copy(a, 0, sibling, me).wait_recv()
        for j, chip in enumerate(chips):
            for a in range(n):
                copy(a, 4 + j, (*chip, 1 - c), me).wait_recv()
        for cp in first + passed:
            cp.wait_send()
        for cp in mine:
            cp.wait()

    return pl.pallas_call(
        body, name=name,
        out_shape=[jax.ShapeDtypeStruct((N_DEV, *s.shape), s.dtype) for s in shards],
        in_specs=[_ANY] * n, out_specs=[_ANY] * n,
        scratch_shapes=[pltpu.SemaphoreType.DMA((7 * n,)), pltpu.SemaphoreType.DMA((7 * n,)),
                        pltpu.SemaphoreType.DMA((n,))],
    )(*shards)


def _sibling_exchange_call(grads, name):
    n = len(grads)

    def body(*refs):
        g_refs, out_refs = refs[:n], refs[n:2 * n]
        send_sems, recv_sems = refs[2 * n:]
        x, y, c = _position()
        copies = []
        for a in range(n):
            for k in range(4):
                px, py = _chip_of(x, y, k)
                copies.append(pltpu.make_async_remote_copy(
                    src_ref=g_refs[a].at[4 * px + 2 * py + (1 - c)], dst_ref=out_refs[a].at[k],
                    send_sem=send_sems.at[4 * a + k], recv_sem=recv_sems.at[4 * a + k],
                    device_id=(x, y, 1 - c), device_id_type=MESH))
        for cp in copies:
            cp.start()
        for cp in copies:
            cp.wait_recv()
        for cp in copies:
            cp.wait_send()

    return pl.pallas_call(
        body, name=name,
        out_shape=[jax.ShapeDtypeStruct((4, *g.shape[1:]), g.dtype) for g in grads],
        in_specs=[_ANY] * n, out_specs=[_ANY] * n,
        scratch_shapes=[pltpu.SemaphoreType.DMA((4 * n,)), pltpu.SemaphoreType.DMA((4 * n,))],
    )(*grads)


def _chip_exchange_call(partials, name):
    n = len(partials)

    def body(*refs):
        p_refs, out_refs = refs[:n], refs[n:2 * n]
        send_sems, recv_sems = refs[2 * n:]
        x, y, c = _position()
        copies = []
        for a in range(n):
            for k in (1, 2, 3):
                copies.append(pltpu.make_async_remote_copy(
                    src_ref=p_refs[a].at[k - 1], dst_ref=out_refs[a].at[k - 1],
                    send_sem=send_sems.at[3 * a + k - 1], recv_sem=recv_sems.at[3 * a + k - 1],
                    device_id=(*_chip_of(x, y, k), c), device_id_type=MESH))
        for cp in copies:
            cp.start()
        for cp in copies:
            cp.wait_recv()
        for cp in copies:
            cp.wait_send()

    return pl.pallas_call(
        body, name=name,
        out_shape=[jax.ShapeDtypeStruct(p.shape, p.dtype) for p in partials],
        in_specs=[_ANY] * n, out_specs=[_ANY] * n,
        scratch_shapes=[pltpu.SemaphoreType.DMA((3 * n,)), pltpu.SemaphoreType.DMA((3 * n,))],
    )(*partials)


def _chip_partial_call(own, sib, tr, name):
    _, r, c = own.shape

    def body(g_ref, s_ref, own_ref, out_ref):
        own_ref[...] = g_ref[0] + s_ref[0]
        for k in (1, 2, 3):
            out_ref[k - 1] = (g_ref[k] + s_ref[k]).astype(BF16)

    four = pl.BlockSpec((4, tr, c), lambda i: (0, i, 0))
    return pl.pallas_call(
        body, name=name, grid=(r // tr,),
        in_specs=[four, four],
        out_specs=[pl.BlockSpec((tr, c), lambda i: (i, 0)), pl.BlockSpec((3, tr, c), lambda i: (0, i, 0))],
        out_shape=[jax.ShapeDtypeStruct((r, c), F32), jax.ShapeDtypeStruct((3, r, c), BF16)],
        compiler_params=_cp(("parallel",)),
    )(own, sib)


def _adamw_call(parts, w, m, v, tr, name):
    r, c = w.shape
    npart = len(parts)

    def body(*refs):
        p_refs = refs[:npart]
        w_ref, m_ref, v_ref, g_ref, d_ref, nm_ref, nv_ref = refs[npart:]
        g = None
        for p_ref in p_refs:
            for s in range(p_ref.shape[0]):
                term = p_ref[s].astype(F32)
                g = term if g is None else g + term
        nm = ADAM_B1 * m_ref[...] + (1.0 - ADAM_B1) * g
        nv = ADAM_B2 * v_ref[...] + (1.0 - ADAM_B2) * (g * g)
        m_hat = nm / (1.0 - ADAM_B1 ** ADAM_STEP)
        v_hat = nv / (1.0 - ADAM_B2 ** ADAM_STEP)
        g_ref[...] = g
        d_ref[...] = -ADAM_LR * (m_hat / (jnp.sqrt(v_hat) + ADAM_EPS) + ADAM_WD * w_ref[...])
        nm_ref[...] = nm
        nv_ref[...] = nv

    flat = pl.BlockSpec((tr, c), lambda i: (i, 0))
    return pl.pallas_call(
        body, name=name, grid=(r // tr,),
        in_specs=[pl.BlockSpec((p.shape[0], tr, c), lambda i: (0, i, 0)) for p in parts] + [flat, flat, flat],
        out_specs=[flat] * 4,
        out_shape=[jax.ShapeDtypeStruct((r, c), F32)] * 4,
        compiler_params=_cp(("parallel",)),
    )(*parts, w, m, v)


_WEIGHTS = ("meta_tokens", "norm1_w", "w_in", "gdn_conv_w", "gdn_a_log", "gdn_dt_bias", "gdn_norm_w", "ssd_conv_w",
            "ssd_conv_b", "ssd_dt_bias", "ssd_a_log", "ssd_d", "ssd_norm_w", "swa_sinks", "w_proj_gdn", "w_proj_ssd",
            "w_proj_swa", "w_out", "norm2_w", "w_up", "w_down", "final_norm_w")
_SHARD_AXIS = {"meta_tokens": 1, "w_in": 2, "gdn_conv_w": 2, "ssd_conv_w": 2, "w_proj_gdn": 1, "w_proj_ssd": 1,
               "w_proj_swa": 1, "w_out": 1, "w_up": 2, "w_down": 1}
_BIG = tuple(n for n in _WEIGHTS if n in _SHARD_AXIS)
_SMALL = tuple(n for n in _WEIGHTS if n not in _SHARD_AXIS)
FLAT_C = 1024


def _pack(arrs, rows, lead=()):
    flat = jnp.concatenate([a.reshape(*lead, -1) for a in arrs], axis=-1)
    pad = rows * FLAT_C - flat.shape[-1]
    flat = jnp.pad(flat, [(0, 0)] * len(lead) + [(0, pad)])
    return flat.reshape(*lead, rows, FLAT_C)


def _unpack(flat, shapes, lead=()):
    flat = flat.reshape(*lead, -1)
    out, off = [], 0
    for s in shapes:
        n = math.prod(s)
        out.append(flat[..., off:off + n].reshape(*lead, *s))
        off += n
    return out


def _rows_for(shapes):
    n = sum(math.prod(s) for s in shapes)
    return -(-n // (FLAT_C * 8)) * 8


def _rows_tile(r, c):
    if r <= 256:
        return r
    return 128 if c > 1024 else 256


def _join(stacked, axis):
    moved = jnp.moveaxis(stacked, 0, axis)
    return moved.reshape(*moved.shape[:axis], -1, *moved.shape[axis + 2:])


def kernel(x, meta_tokens, norm1_w, w_in, gdn_conv_w, gdn_a_log, gdn_dt_bias, gdn_norm_w, ssd_conv_w, ssd_conv_b,
           ssd_dt_bias, ssd_a_log, ssd_d, ssd_norm_w, swa_sinks, w_proj_gdn, w_proj_ssd, w_proj_swa, w_out, norm2_w,
           w_up, w_down, final_norm_w, loss_target, m_meta_tokens, m_norm1_w, m_w_in, m_gdn_conv_w, m_gdn_a_log,
           m_gdn_dt_bias, m_gdn_norm_w, m_ssd_conv_w, m_ssd_conv_b, m_ssd_dt_bias, m_ssd_a_log, m_ssd_d, m_ssd_norm_w,
           m_swa_sinks, m_w_proj_gdn, m_w_proj_ssd, m_w_proj_swa, m_w_out, m_norm2_w, m_w_up, m_w_down,
           m_final_norm_w, v_meta_tokens, v_norm1_w, v_w_in, v_gdn_conv_w, v_gdn_a_log, v_gdn_dt_bias, v_gdn_norm_w,
           v_ssd_conv_w, v_ssd_conv_b, v_ssd_dt_bias, v_ssd_a_log, v_ssd_d, v_ssd_norm_w, v_swa_sinks, v_w_proj_gdn,
           v_w_proj_ssd, v_w_proj_swa, v_w_out, v_norm2_w, v_w_up, v_w_down, v_final_norm_w):
    args = (meta_tokens, norm1_w, w_in, gdn_conv_w, gdn_a_log, gdn_dt_bias, gdn_norm_w, ssd_conv_w, ssd_conv_b,
            ssd_dt_bias, ssd_a_log, ssd_d, ssd_norm_w, swa_sinks, w_proj_gdn, w_proj_ssd, w_proj_swa, w_out, norm2_w,
            w_up, w_down, final_norm_w, m_meta_tokens, m_norm1_w, m_w_in, m_gdn_conv_w, m_gdn_a_log,
            m_gdn_dt_bias, m_gdn_norm_w, m_ssd_conv_w, m_ssd_conv_b, m_ssd_dt_bias, m_ssd_a_log, m_ssd_d, m_ssd_norm_w,
            m_swa_sinks, m_w_proj_gdn, m_w_proj_ssd, m_w_proj_swa, m_w_out, m_norm2_w, m_w_up, m_w_down,
            m_final_norm_w, v_meta_tokens, v_norm1_w, v_w_in, v_gdn_conv_w, v_gdn_a_log, v_gdn_dt_bias, v_gdn_norm_w,
            v_ssd_conv_w, v_ssd_conv_b, v_ssd_dt_bias, v_ssd_a_log, v_ssd_d, v_ssd_norm_w, v_swa_sinks, v_w_proj_gdn,
            v_w_proj_ssd, v_w_proj_swa, v_w_out, v_norm2_w, v_w_up, v_w_down, v_final_norm_w)
    nw = len(_WEIGHTS)
    w = dict(zip(_WEIGHTS, args[:nw]))
    m = dict(zip(_WEIGHTS, args[nw:2 * nw]))
    v = dict(zip(_WEIGHTS, args[2 * nw:]))

    depth = w["w_in"].shape[0]
    small_shapes = [w[n].shape for n in _SMALL]
    small_rows = _rows_for(small_shapes)

    def flat2(t):
        return t.reshape(-1, t.shape[-1])

    sent = [flat2(w[n]).astype(BF16) if n in _MATMUL else flat2(w[n]) for n in _BIG]
    gathered = dict(zip(_BIG, _allgather_call(sent, "gather_weights")))

    def layer_of(name, l):
        t = gathered[name]
        return t.reshape(N_DEV, depth, t.shape[1] // depth, t.shape[2])[:, l]

    wb, slots = [], []
    for l in range(depth):
        full = {"w_in": _regroup_w_in(layer_of("w_in", l)),
                "w_up": layer_of("w_up", l).transpose(1, 0, 2).reshape(D_MODEL, D_FF),
                "w_down": layer_of("w_down", l).reshape(D_FF, D_MODEL)}
        for n in ("w_proj_gdn", "w_proj_ssd", "w_proj_swa", "w_out"):
            full[n] = layer_of(n, l).reshape(D_MODEL, D_MODEL)
        wb.append(full)
        slots.append({n: jnp.zeros(t.shape, F32) for n, t in full.items()})

    tiny = {n: gathered[n].reshape(N_DEV, *w[n].shape) for n in _BIG if n not in _MATMUL}
    small = {n: w[n] for n in _SMALL}
    loss_head = _make_loss_head(loss_target[0])

    def local_loss(x_rows, slots, tiny, small):
        joined = {n: _join(tiny[n], _SHARD_AXIS[n]) for n in tiny}
        return _local_loss(x_rows, {**joined, **small}, slots, wb, loss_head)

    loss, (gx, g_slots, g_tiny, g_small) = jax.value_and_grad(local_loss, argnums=(0, 1, 2, 3))(
        x[0], slots, tiny, small)
    loss = lax.psum(loss, ("x", "y", "c"))

    def per_device(name, g):
        if name == "w_in":
            return _ungroup_w_in(g)
        if name == "w_up":
            return g.reshape(D_MODEL, N_DEV, D_FF // N_DEV).transpose(1, 0, 2)
        return g.reshape(N_DEV, g.shape[0] // N_DEV, g.shape[1])

    contrib = []
    for n in _BIG:
        if n in _MATMUL:
            t = jnp.stack([per_device(n, g_slots[l][n]) for l in range(depth)], axis=1)
        else:
            t = g_tiny[n]
        contrib.append(t.reshape(N_DEV, -1, t.shape[-1]))

    from_sibling = _sibling_exchange_call(contrib, "grads_to_sibling")
    x_pos, y_pos, c_pos = _position()
    slabs = [4 * px + 2 * py + c_pos for px, py in (_chip_of(x_pos, y_pos, k) for k in range(4))]
    mine = [jnp.stack([lax.dynamic_index_in_dim(g, s, 0, keepdims=False) for s in slabs]) for g in contrib]
    own, outgoing = [], []
    for n, g, s in zip(_BIG, mine, from_sibling):
        o, p = _chip_partial_call(g, s, _rows_tile(g.shape[1], g.shape[2]), "chip_partial_" + n)
        own.append(o)
        outgoing.append(p)
    incoming = _chip_exchange_call(outgoing, "grads_to_chips")

    by_name = {}
    for n, o, r in zip(_BIG, own, incoming):
        res = _adamw_call([o[None], r], flat2(w[n]), flat2(m[n]), flat2(v[n]), _rows_tile(o.shape[0], o.shape[1]),
                          "adamw_" + n)
        by_name[n] = [t.reshape(w[n].shape) for t in res]

    small_parts = _allgather_call([_pack([g_small[n] for n in _SMALL], small_rows)], "gather_small_grads")
    small_out = _adamw_call(small_parts, *[_pack([d[n] for n in _SMALL], small_rows) for d in (w, m, v)], small_rows,
                            "adamw_replicated")
    for kind in range(4):
        for n, t in zip(_SMALL, _unpack(small_out[kind], small_shapes)):
            by_name.setdefault(n, [None] * 4)[kind] = t

    outs = [by_name[n][kind] for kind in range(4) for n in _WEIGHTS]
    return (loss, gx[None], *outs)
```

```python
import functools
import math

import jax
import jax.numpy as jnp
from jax import lax
from jax.experimental import pallas as pl
from jax.experimental.pallas import tpu as pltpu

F32 = jnp.float32
BF16 = jnp.bfloat16
HI = lax.Precision.HIGHEST
NEG = -1e30

D_MODEL = 1024
N_META = 16
BLK = 128
NPAD = BLK - N_META
RMS_EPS = 1e-6
L2_EPS = 1e-6
CONV_K = 4

GDN_H, GDN_D, GDN_C = 8, 128, 64
SSD_H, SSD_P, SSD_G, SSD_N = 16, 64, 4, 128
SSD_HPG = SSD_H // SSD_G
SWA_QH, SWA_KVH, SWA_D = 16, 4, 64
SWA_REP = SWA_QH // SWA_KVH
D_FF = 4 * D_MODEL

N_DEV = 8
MESH = pl.DeviceIdType.MESH

ADAM_LR, ADAM_B1, ADAM_B2, ADAM_EPS, ADAM_WD, ADAM_STEP = 0.001, 0.9, 0.999, 1e-08, 0.01, 10

VMEM_LIMIT = 56 * 1024 * 1024


def _cp(sem=None):
    return pltpu.CompilerParams(dimension_semantics=sem, vmem_limit_bytes=VMEM_LIMIT)


def _dot(a, b, ca, cb, prec=HI):
    return lax.dot_general(a, b, (((ca,), (cb,)), ((), ())), precision=prec, preferred_element_type=F32)


def _nn(a, b, prec=HI):
    return _dot(a, b, 1, 0, prec)


def _nt(a, b, prec=HI):
    return _dot(a, b, 1, 1, prec)


def _tn(a, b, prec=HI):
    return _dot(a, b, 0, 0, prec)


def _bdot(a, b, ca, cb):
    return lax.dot_general(a.astype(BF16), b.astype(BF16), (((ca,), (cb,)), ((), ())), preferred_element_type=F32)


@jax.custom_vjp
def _lo_nn(a, b):
    return _bdot(a, b, 1, 0)


_lo_nn.defvjp(lambda a, b: (_bdot(a, b, 1, 0), (a, b)),
              lambda r, d: (_bdot(d, r[1], 1, 1), _bdot(r[0], d, 0, 0)))


@jax.custom_vjp
def _lo_nt(a, b):
    return _bdot(a, b, 1, 1)


_lo_nt.defvjp(lambda a, b: (_bdot(a, b, 1, 1), (a, b)),
              lambda r, d: (_bdot(d, r[1], 1, 0), _bdot(d, r[0], 0, 0)))


@jax.custom_vjp
def _lo_tn(a, b):
    return _bdot(a, b, 0, 0)


_lo_tn.defvjp(lambda a, b: (_bdot(a, b, 0, 0), (a, b)),
              lambda r, d: (_bdot(r[1], d, 1, 1), _bdot(r[0], d, 1, 0)))


def _iota2(n, m, axis):
    return lax.broadcasted_iota(jnp.int32, (n, m), axis)


def _silu(x):
    return x * jax.nn.sigmoid(x)


def _softplus(x):
    return jnp.maximum(x, 0.0) + jnp.log(1.0 + jnp.exp(-jnp.abs(x)))


def _row_of(col):
    n = col.shape[0]
    eye = (_iota2(n, n, 0) == _iota2(n, n, 1)).astype(F32)
    return _nn(jnp.ones((n, n), F32), eye * col)


def _cumsum_col(col):
    n = col.shape[0]
    tril = (_iota2(n, n, 0) >= _iota2(n, n, 1)).astype(F32)
    return _nn(tril, col)


def _tri_inv(a):
    n = a.shape[0]
    r, c = _iota2(n, n, 0), _iota2(n, n, 1)
    eye = (r == c).astype(F32)
    blk = jnp.right_shift(r, 4) == jnp.right_shift(c, 4)
    d = jnp.where(blk, a, 0.0)
    off = a - d
    d2 = _nn(d, d)
    d4 = _nn(d2, d2)
    d8 = _nn(d4, d4)
    td = _nn(_nn(_nn(eye - d, eye + d2), eye + d4), eye + d8)
    m = _nn(td, off)
    m2 = _nn(m, m)
    return _nn(_nn(eye - m, eye + m2), td)


@jax.custom_vjp
def _tri_solve(a, a_t, rhs):
    return _nn(_tri_inv(a), rhs)


def _tri_solve_fwd(a, a_t, rhs):
    sol = _nn(_tri_inv(a), rhs)
    return sol, (a_t, sol)


def _tri_solve_bwd(res, dsol):
    a_t, sol = res
    drhs = _nn(_tri_inv(a_t), dsol)
    return -_nt(drhs, sol), jnp.zeros_like(a_t), drhs


_tri_solve.defvjp(_tri_solve_fwd, _tri_solve_bwd)


def _gdn_chunk(qa, ka, va, gate, a_raw, b_raw, s, a_log, dt_bias, norm_w, valid):
    c = qa.shape[0]
    q = qa * lax.rsqrt(jnp.sum(qa * qa, axis=-1, keepdims=True) + L2_EPS) * (GDN_D ** -0.5)
    k = ka * lax.rsqrt(jnp.sum(ka * ka, axis=-1, keepdims=True) + L2_EPS)
    beta = jax.nn.sigmoid(b_raw)
    g = -jnp.exp(a_log) * _softplus(a_raw + dt_bias) * valid
    gam = _cumsum_col(g)
    gam_row = _row_of(gam)
    r, cc = _iota2(c, c, 0), _iota2(c, c, 1)
    decay = jnp.exp(jnp.where(r >= cc, gam - gam_row, NEG))
    kb = k * beta
    a = jnp.where(r > cc, _lo_nt(kb, k) * decay, 0.0)
    a_t = lax.stop_gradient(jnp.where(cc > r, _bdot(k, kb, 1, 1) * jnp.exp(jnp.where(cc >= r, gam_row - gam, NEG)), 0.0))
    egam = jnp.exp(gam)
    sol = _tri_solve(a, a_t, jnp.concatenate([va * beta, kb * egam], axis=1))
    u = sol[:, :GDN_D]
    w = sol[:, GDN_D:]
    attn = _lo_nt(q, k) * decay
    g_last = jnp.sum(g, axis=0, keepdims=True)
    k_tail = k * jnp.exp(g_last - gam)
    v_new = u - _lo_nn(w, s)
    o = _lo_nn(q * egam, s) + _lo_nn(attn, v_new)
    s_new = s * jnp.exp(g_last) + _lo_tn(k_tail, v_new)
    y = o * lax.rsqrt(jnp.mean(o * o, axis=-1, keepdims=True) + RMS_EPS) * norm_w * _silu(gate)
    return y, s_new


def _valid_col(row0, n):
    return (row0 + _iota2(n, 1, 0) >= NPAD).astype(F32)


GDN_HB = GDN_H

SM_B, SM_A, SM_DT, SM_W = 0, 8, 16, 128


def _pick_cols(sm, first, n):
    return jnp.stack([sm[:, first + j:first + j + 1] for j in range(n)])


def _spread_cols(cols, first):
    lane = _iota2(1, SM_W, 1)
    out = None
    for j in range(cols.shape[0]):
        term = cols[j] * (lane == first + j).astype(F32)
        out = term if out is None else out + term
    return out


def _gdn_specs(nc, rev):
    ci = (lambda i: nc - 1 - i) if rev else (lambda i: i)
    hb = GDN_HB
    tile = pl.BlockSpec((GDN_C, hb * GDN_D), lambda h, i: (ci(i), h))
    col = pl.BlockSpec((GDN_C, SM_W), lambda h, i: (ci(i), 0))
    scal = pl.BlockSpec((hb, 1, 1), lambda h, i: (h, 0, 0))
    nw = pl.BlockSpec((1, GDN_D), lambda h, i: (0, 0))
    st = pl.BlockSpec((hb, 1, GDN_D, GDN_D), lambda h, i: (h, ci(i), 0, 0))
    return tile, col, scal, nw, st


def _lanes(j):
    return slice(j * GDN_D, (j + 1) * GDN_D)


def _by_head(ref):
    return jnp.stack([ref[:, _lanes(j)] for j in range(GDN_HB)])


def _gdn_fwd_call(q, k, v, gate, small, a_log, dt_bias, norm_w):
    seq = q.shape[0]
    nc = seq // GDN_C
    tile, col, scal, nw, st = _gdn_specs(nc, False)

    def body(q_ref, k_ref, v_ref, g_ref, sm_ref, al_ref, dt_ref, nw_ref, y_ref, st_ref, s_scr):
        i = pl.program_id(1)

        @pl.when(i == 0)
        def _():
            s_scr[...] = jnp.zeros_like(s_scr)

        s = s_scr[...]
        st_ref[:, 0] = s
        sm = sm_ref[...]
        fn = jax.vmap(functools.partial(_gdn_chunk, valid=_valid_col(i * GDN_C, GDN_C)))
        y, s_new = fn(_by_head(q_ref), _by_head(k_ref), _by_head(v_ref), _by_head(g_ref),
                      _pick_cols(sm, SM_A, GDN_H), _pick_cols(sm, SM_B, GDN_H), s,
                      al_ref[...], dt_ref[...], jnp.broadcast_to(nw_ref[...], (GDN_HB, 1, GDN_D)))
        for j in range(GDN_HB):
            y_ref[:, _lanes(j)] = y[j]
        s_scr[...] = s_new

    return pl.pallas_call(
        body, name="gdn_fwd", grid=(GDN_H // GDN_HB, nc),
        in_specs=[tile, tile, tile, tile, col, scal, scal, nw],
        out_specs=[tile, st],
        out_shape=[jax.ShapeDtypeStruct((seq, GDN_H * GDN_D), F32),
                   jax.ShapeDtypeStruct((GDN_H, nc, GDN_D, GDN_D), F32)],
        scratch_shapes=[pltpu.VMEM((GDN_HB, GDN_D, GDN_D), F32)],
        compiler_params=_cp(("parallel", "arbitrary")),
    )(q, k, v, gate, small, a_log, dt_bias, norm_w)


def _gdn_bwd_call(q, k, v, gate, small, a_log, dt_bias, norm_w, states, dy):
    seq = q.shape[0]
    nc = seq // GDN_C
    tile, col, scal, nw, st = _gdn_specs(nc, True)
    nwh = pl.BlockSpec((GDN_HB, 1, GDN_D), lambda h, i: (h, 0, 0))

    def body(q_ref, k_ref, v_ref, g_ref, sm_ref, al_ref, dt_ref, nw_ref, st_ref, dy_ref,
             dq_ref, dk_ref, dv_ref, dg_ref, dsm_ref, dal_ref, ddt_ref, dnw_ref, ds_scr):
        i = pl.program_id(1)

        @pl.when(i == 0)
        def _():
            ds_scr[...] = jnp.zeros_like(ds_scr)
            dal_ref[...] = jnp.zeros_like(dal_ref)
            ddt_ref[...] = jnp.zeros_like(ddt_ref)
            dnw_ref[...] = jnp.zeros_like(dnw_ref)

        sm = sm_ref[...]
        fn = jax.vmap(functools.partial(_gdn_chunk, valid=_valid_col((nc - 1 - i) * GDN_C, GDN_C)))
        _, vjp = jax.vjp(fn, _by_head(q_ref), _by_head(k_ref), _by_head(v_ref), _by_head(g_ref),
                         _pick_cols(sm, SM_A, GDN_H), _pick_cols(sm, SM_B, GDN_H), st_ref[:, 0], al_ref[...],
                         dt_ref[...], jnp.broadcast_to(nw_ref[...], (GDN_HB, 1, GDN_D)))
        dq, dk, dv, dg, da, db, ds, dal, ddt, dnw = vjp((_by_head(dy_ref), ds_scr[...]))
        for j in range(GDN_HB):
            dq_ref[:, _lanes(j)] = dq[j]
            dk_ref[:, _lanes(j)] = dk[j]
            dv_ref[:, _lanes(j)] = dv[j]
            dg_ref[:, _lanes(j)] = dg[j]
        dsm_ref[...] = _spread_cols(da, SM_A) + _spread_cols(db, SM_B)
        ds_scr[...] = ds
        dal_ref[...] += dal
        ddt_ref[...] += ddt
        dnw_ref[...] += dnw

    big = jax.ShapeDtypeStruct((seq, GDN_H * GDN_D), F32)
    return pl.pallas_call(
        body, name="gdn_bwd", grid=(GDN_H // GDN_HB, nc),
        in_specs=[tile, tile, tile, tile, col, scal, scal, nw, st, tile],
        out_specs=[tile, tile, tile, tile, col, scal, scal, nwh],
        out_shape=[big, big, big, big, jax.ShapeDtypeStruct(small.shape, F32),
                   jax.ShapeDtypeStruct((GDN_H, 1, 1), F32), jax.ShapeDtypeStruct((GDN_H, 1, 1), F32),
                   jax.ShapeDtypeStruct((GDN_H, 1, GDN_D), F32)],
        scratch_shapes=[pltpu.VMEM((GDN_HB, GDN_D, GDN_D), F32)],
        compiler_params=_cp(("parallel", "arbitrary")),
    )(q, k, v, gate, small, a_log, dt_bias, norm_w, states, dy)


@jax.custom_vjp
def gdn_core(q, k, v, gate, small, a_log, dt_bias, norm_w):
    return _gdn_fwd_call(q, k, v, gate, small, a_log, dt_bias, norm_w)[0]


def _gdn_core_fwd(q, k, v, gate, small, a_log, dt_bias, norm_w):
    y, states = _gdn_fwd_call(q, k, v, gate, small, a_log, dt_bias, norm_w)
    return y, (q, k, v, gate, small, a_log, dt_bias, norm_w, states)


def _gdn_core_bwd(res, dy):
    dq, dk, dv, dg, dsm, dal, ddt, dnw = _gdn_bwd_call(*res, dy)
    return dq, dk, dv, dg, dsm, dal, ddt, jnp.sum(dnw, axis=0)


gdn_core.defvjp(_gdn_core_fwd, _gdn_core_bwd)


def _ssd_head(x, z, dt_raw, h, dt_bias, a_log, d_skip, bm, cm, cb, valid):
    c = bm.shape[0]
    r, cc = _iota2(c, c, 0), _iota2(c, c, 1)
    dtp = _softplus(dt_raw + dt_bias)
    x = x * valid
    adt = -jnp.exp(a_log) * dtp * valid
    xdt = x * dtp
    acum = _cumsum_col(adt)
    lmat = jnp.exp(jnp.where(r >= cc, acum - _row_of(acum), NEG))
    a_last = jnp.sum(adt, axis=0, keepdims=True)
    y = _lo_nn(cb * lmat, xdt) + _lo_nt(cm * jnp.exp(acum), h) + d_skip * x
    h_new = h * jnp.exp(a_last) + _lo_tn(xdt * jnp.exp(a_last - acum), bm)
    return y * _silu(z), h_new


def _ssd_chunk(xs, z, bm, cm, dt_raw, h, dt_bias, a_log, d_skip, norm_w, valid):
    nh, c, p = xs.shape
    ng = bm.shape[0]
    hpg = nh // ng
    bm = bm * valid
    cm = cm * valid
    cb = jax.vmap(_lo_nt)(cm, bm)
    per_head = lambda t: jnp.repeat(t, hpg, axis=0)
    ys, hs = jax.vmap(functools.partial(_ssd_head, valid=valid))(
        xs, z, dt_raw, h, dt_bias, a_log, d_skip, per_head(bm), per_head(cm), per_head(cb))
    ss = jnp.sum(jnp.sum(ys * ys, axis=-1, keepdims=True).reshape(ng, hpg, c, 1), axis=1, keepdims=True)
    rstd = lax.rsqrt(ss / (hpg * p) + RMS_EPS)
    return (ys.reshape(ng, hpg, c, p) * rstd).reshape(nh, c, p) * norm_w, hs


SSD_INNER = SSD_H * SSD_P
SSD_BC = SSD_G * SSD_N


def _split_lanes(t, n, w):
    return jnp.stack([t[:, j * w:(j + 1) * w] for j in range(n)])


def _join_lanes(t):
    return jnp.concatenate([t[j] for j in range(t.shape[0])], axis=1)


def _ssd_specs(nc, rev):
    ci = (lambda i: nc - 1 - i) if rev else (lambda i: i)
    wide = pl.BlockSpec((BLK, SSD_INNER), lambda i: (ci(i), 0))
    bmat = pl.BlockSpec((BLK, SSD_BC), lambda i: (ci(i), SSD_INNER // SSD_BC))
    cmat = pl.BlockSpec((BLK, SSD_BC), lambda i: (ci(i), SSD_INNER // SSD_BC + 1))
    xbc = pl.BlockSpec((BLK, SSD_INNER + 2 * SSD_BC), lambda i: (ci(i), 0))
    col = pl.BlockSpec((BLK, SM_W), lambda i: (ci(i), 0))
    scal = pl.BlockSpec((SSD_H, 1, 1), lambda i: (0, 0, 0))
    nw = pl.BlockSpec((SSD_H, 1, SSD_P), lambda i: (0, 0, 0))
    st = pl.BlockSpec((SSD_H, 1, SSD_P, SSD_N), lambda i: (0, ci(i), 0, 0))
    return wide, bmat, cmat, xbc, col, scal, nw, st


def _ssd_fwd_call(xbc, z, small, dt_bias, a_log, d_skip, norm_w):
    seq = z.shape[0]
    nc = seq // BLK
    wide, bmat, cmat, _, col, scal, nw, st = _ssd_specs(nc, False)

    def body(x_ref, b_ref, c_ref, z_ref, sm_ref, db_ref, al_ref, ds_ref, nw_ref, y_ref, st_ref, h_scr):
        i = pl.program_id(0)

        @pl.when(i == 0)
        def _():
            h_scr[...] = jnp.zeros_like(h_scr)

        h = h_scr[...]
        st_ref[:, 0] = h
        y, h_new = _ssd_chunk(_split_lanes(x_ref[...], SSD_H, SSD_P), _split_lanes(z_ref[...], SSD_H, SSD_P),
                              _split_lanes(b_ref[...], SSD_G, SSD_N), _split_lanes(c_ref[...], SSD_G, SSD_N),
                              _pick_cols(sm_ref[...], SM_DT, SSD_H), h, db_ref[...], al_ref[...], ds_ref[...],
                              nw_ref[...], _valid_col(i * BLK, BLK))
        y_ref[...] = _join_lanes(y)
        h_scr[...] = h_new

    return pl.pallas_call(
        body, name="ssd_fwd", grid=(nc,),
        in_specs=[wide, bmat, cmat, wide, col, scal, scal, scal, nw],
        out_specs=[wide, st],
        out_shape=[jax.ShapeDtypeStruct((seq, SSD_INNER), F32),
                   jax.ShapeDtypeStruct((SSD_H, nc, SSD_P, SSD_N), F32)],
        scratch_shapes=[pltpu.VMEM((SSD_H, SSD_P, SSD_N), F32)],
        compiler_params=_cp(("arbitrary",)),
    )(xbc, xbc, xbc, z, small, dt_bias, a_log, d_skip, norm_w)


def _ssd_bwd_call(xbc, z, small, dt_bias, a_log, d_skip, norm_w, states, dy):
    seq = z.shape[0]
    nc = seq // BLK
    wide, bmat, cmat, xbc_spec, col, scal, nw, st = _ssd_specs(nc, True)

    def body(x_ref, b_ref, c_ref, z_ref, sm_ref, db_ref, al_ref, ds_ref, nw_ref, st_ref, dy_ref,
             dxbc_ref, dz_ref, dsm_ref, ddb_ref, dal_ref, dds_ref, dnw_ref, dh_scr):
        i = pl.program_id(0)

        @pl.when(i == 0)
        def _():
            dh_scr[...] = jnp.zeros_like(dh_scr)
            ddb_ref[...] = jnp.zeros_like(ddb_ref)
            dal_ref[...] = jnp.zeros_like(dal_ref)
            dds_ref[...] = jnp.zeros_like(dds_ref)
            dnw_ref[...] = jnp.zeros_like(dnw_ref)

        fn = functools.partial(_ssd_chunk, valid=_valid_col((nc - 1 - i) * BLK, BLK))
        _, vjp = jax.vjp(fn, _split_lanes(x_ref[...], SSD_H, SSD_P), _split_lanes(z_ref[...], SSD_H, SSD_P),
                         _split_lanes(b_ref[...], SSD_G, SSD_N), _split_lanes(c_ref[...], SSD_G, SSD_N),
                         _pick_cols(sm_ref[...], SM_DT, SSD_H), st_ref[:, 0], db_ref[...], al_ref[...], ds_ref[...],
                         nw_ref[...])
        dx, dz, dbm, dcm, ddt, dh, ddb, dal, dds, dnw = vjp((_split_lanes(dy_ref[...], SSD_H, SSD_P), dh_scr[...]))
        dxbc_ref[:, :SSD_INNER] = _join_lanes(dx)
        dxbc_ref[:, SSD_INNER:SSD_INNER + SSD_BC] = _join_lanes(dbm)
        dxbc_ref[:, SSD_INNER + SSD_BC:] = _join_lanes(dcm)
        dz_ref[...] = _join_lanes(dz)
        dsm_ref[...] = _spread_cols(ddt, SM_DT)
        dh_scr[...] = dh
        ddb_ref[...] += ddb
        dal_ref[...] += dal
        dds_ref[...] += dds
        dnw_ref[...] += dnw

    sshape = jax.ShapeDtypeStruct((SSD_H, 1, 1), F32)
    return pl.pallas_call(
        body, name="ssd_bwd", grid=(nc,),
        in_specs=[wide, bmat, cmat, wide, col, scal, scal, scal, nw, st, wide],
        out_specs=[xbc_spec, wide, col, scal, scal, scal, nw],
        out_shape=[jax.ShapeDtypeStruct(xbc.shape, F32), jax.ShapeDtypeStruct(z.shape, F32),
                   jax.ShapeDtypeStruct(small.shape, F32), sshape, sshape, sshape,
                   jax.ShapeDtypeStruct((SSD_H, 1, SSD_P), F32)],
        scratch_shapes=[pltpu.VMEM((SSD_H, SSD_P, SSD_N), F32)],
        compiler_params=_cp(("arbitrary",)),
    )(xbc, xbc, xbc, z, small, dt_bias, a_log, d_skip, norm_w, states, dy)


@jax.custom_vjp
def ssd_core(xbc, z, small, dt_bias, a_log, d_skip, norm_w):
    return _ssd_fwd_call(xbc, z, small, dt_bias, a_log, d_skip, norm_w)[0]


def _ssd_core_fwd(*args):
    y, states = _ssd_fwd_call(*args)
    return y, (*args, states)


def _ssd_core_bwd(res, dy):
    return tuple(_ssd_bwd_call(*res, dy))


ssd_core.defvjp(_ssd_core_fwd, _ssd_core_bwd)


def _swa_block(q, km, kp, kc, vm, vp, vc, sink, n):
    rows = SWA_REP * BLK
    qs = q.reshape(rows, SWA_D) * (SWA_D ** -0.5)
    s = _lo_nt(qs, jnp.concatenate([km, kp, kc], axis=0))
    i = jnp.bitwise_and(_iota2(rows, 3 * BLK, 0), BLK - 1)
    col = _iota2(rows, 3 * BLK, 1)
    j = jnp.bitwise_and(col, BLK - 1)
    part = jnp.right_shift(col, 7)
    ok_m = (part == 0) & (j >= NPAD) & ((n >= 1) | (j <= i))
    ok_p = (part == 1) & (n >= 2) & (j > i)
    ok_c = (part == 2) & (n >= 1) & (j <= i)
    ok = ok_m | ok_p | ok_c
    s = jnp.where(ok, s, NEG)
    snk = jnp.concatenate([jnp.broadcast_to(sink[r], (BLK, 1)) for r in range(SWA_REP)], axis=0)
    m = lax.stop_gradient(jnp.maximum(jnp.max(s, axis=-1, keepdims=True), snk))
    e = jnp.exp(s - m)
    p = e / (jnp.sum(e, axis=-1, keepdims=True) + jnp.exp(snk - m))
    o = _lo_nn(p, jnp.concatenate([vm, vp, vc], axis=0))
    return o.reshape(SWA_REP, BLK, SWA_D)


SWA_QW = SWA_QH * SWA_D
SWA_KW = SWA_KVH * SWA_D


def _swa_specs(nb, rev):
    ci = (lambda i: nb - 1 - i) if rev else (lambda i: i)
    qsp = pl.BlockSpec((BLK, SWA_QW), lambda i: (ci(i), 0))
    cur = pl.BlockSpec((BLK, 2 * SWA_KW), lambda i: (ci(i), 0))
    prev = pl.BlockSpec((BLK, 2 * SWA_KW), lambda i: (jnp.maximum(ci(i) - 1, 0), 0))
    meta = pl.BlockSpec((BLK, 2 * SWA_KW), lambda i: (0, 0))
    scal = pl.BlockSpec((SWA_QH, 1, 1), lambda i: (0, 0, 0))
    return qsp, cur, prev, meta, scal


def _swa_by_head(q, kvm, kvp, kvc, sink):
    def kv(t):
        return _split_lanes(t[:, :SWA_KW], SWA_KVH, SWA_D), _split_lanes(t[:, SWA_KW:], SWA_KVH, SWA_D)

    (km, vm), (kp, vp), (kc, vc) = kv(kvm), kv(kvp), kv(kvc)
    qh = _split_lanes(q, SWA_QH, SWA_D).reshape(SWA_KVH, SWA_REP, BLK, SWA_D)
    return qh, km, kp, kc, vm, vp, vc, sink.reshape(SWA_KVH, SWA_REP, 1, 1)


def _swa_kv_tile(dk, dv):
    return jnp.concatenate([_join_lanes(dk), _join_lanes(dv)], axis=1)


def _swa_fwd_call(q, kv, sink):
    seq = q.shape[0]
    nb = seq // BLK
    qsp, cur, prev, meta, scal = _swa_specs(nb, False)

    def body(q_ref, m_ref, p_ref, c_ref, s_ref, o_ref):
        fn = jax.vmap(functools.partial(_swa_block, n=pl.program_id(0)))
        o = fn(*_swa_by_head(q_ref[...], m_ref[...], p_ref[...], c_ref[...], s_ref[...]))
        o_ref[...] = _join_lanes(o.reshape(SWA_QH, BLK, SWA_D))

    return pl.pallas_call(
        body, name="swa_fwd", grid=(nb,),
        in_specs=[qsp, meta, prev, cur, scal],
        out_specs=qsp,
        out_shape=jax.ShapeDtypeStruct(q.shape, F32),
        compiler_params=_cp(("parallel",)),
    )(q, kv, kv, kv, sink)


def _swa_bwd_call(q, kv, sink, do):
    seq = q.shape[0]
    nb = seq // BLK
    qsp, cur, prev, meta, scal = _swa_specs(nb, True)

    def body(q_ref, m_ref, p_ref, c_ref, s_ref, do_ref, dq_ref, dkv_ref, ds_ref, prev_scr, meta_scr):
        i = pl.program_id(0)
        n = nb - 1 - i

        @pl.when(i == 0)
        def _():
            prev_scr[...] = jnp.zeros_like(prev_scr)
            meta_scr[...] = jnp.zeros_like(meta_scr)
            ds_ref[...] = jnp.zeros_like(ds_ref)

        fn = jax.vmap(functools.partial(_swa_block, n=n))
        _, vjp = jax.vjp(fn, *_swa_by_head(q_ref[...], m_ref[...], p_ref[...], c_ref[...], s_ref[...]))
        do = _split_lanes(do_ref[...], SWA_QH, SWA_D).reshape(SWA_KVH, SWA_REP, BLK, SWA_D)
        dq, dkm, dkp, dkc, dvm, dvp, dvc, dsk = vjp(do)
        dq_ref[...] = _join_lanes(dq.reshape(SWA_QH, BLK, SWA_D))
        ds_ref[...] += dsk.reshape(SWA_QH, 1, 1)
        meta_scr[...] += _swa_kv_tile(dkm, dvm)
        first = (n == 0).astype(F32)
        dkv_ref[...] = _swa_kv_tile(dkc, dvc) + prev_scr[...] + first * meta_scr[...]
        prev_scr[...] = _swa_kv_tile(dkp, dvp)

    return pl.pallas_call(
        body, name="swa_bwd", grid=(nb,),
        in_specs=[qsp, meta, prev, cur, scal, qsp],
        out_specs=[qsp, cur, scal],
        out_shape=[jax.ShapeDtypeStruct(q.shape, F32), jax.ShapeDtypeStruct(kv.shape, F32),
                   jax.ShapeDtypeStruct(sink.shape, F32)],
        scratch_shapes=[pltpu.VMEM((BLK, 2 * SWA_KW), F32)] * 2,
        compiler_params=_cp(("arbitrary",)),
    )(q, kv, kv, kv, sink, do)


@jax.custom_vjp
def swa_core(q, kv, sink):
    return _swa_fwd_call(q, kv, sink)


def _swa_core_fwd(q, kv, sink):
    return _swa_fwd_call(q, kv, sink), (q, kv, sink)


def _swa_core_bwd(res, do):
    return tuple(_swa_bwd_call(*res, do))


swa_core.defvjp(_swa_core_fwd, _swa_core_bwd)


def _tile(n, pref):
    if n <= pref:
        return n
    best = None
    for t in range(128, pref + 1, 128):
        if n % t == 0:
            best = t
    assert best is not None, (n, pref)
    return best


def _mm_tiles(m, n, kk):
    if kk > 8192:
        return _tile(m, 704), _tile(n, 512), _tile(kk, 4096)
    return _tile(m, 1408), _tile(n, 512), _tile(kk, 1408)


def _mm_call(a, b, name):
    (m, kk), n = a.shape, b.shape[1]
    tm, tn, tk = _mm_tiles(m, n, kk)
    nk = kk // tk
    a_spec = pl.BlockSpec((tm, tk), lambda i, j, k: (i, k))
    b_spec = pl.BlockSpec((tk, tn), lambda i, j, k: (k, j))

    def body(a_ref, b_ref, o_ref, acc_ref):
        k = pl.program_id(2)
        part = jnp.dot(a_ref[...].astype(BF16), b_ref[...].astype(BF16), preferred_element_type=F32)

        @pl.when(k == 0)
        def _():
            acc_ref[...] = part

        @pl.when(k > 0)
        def _():
            acc_ref[...] += part

        @pl.when(k == nk - 1)
        def _():
            o_ref[...] = acc_ref[...]

    return pl.pallas_call(
        body, name=name, grid=(m // tm, n // tn, nk),
        in_specs=[a_spec, b_spec],
        out_specs=pl.BlockSpec((tm, tn), lambda i, j, k: (i, j)),
        out_shape=jax.ShapeDtypeStruct((m, n), F32),
        scratch_shapes=[pltpu.VMEM((tm, tn), F32)],
        compiler_params=_cp(("parallel", "parallel", "arbitrary")),
    )(a, b)


@jax.custom_vjp
def mm(a, b, b_t, grad_slot):
    return _mm_call(a, b, "mm_fwd")


def _mm_fwd(a, b, b_t, grad_slot):
    return _mm_call(a, b, "mm_fwd"), (a, b, b_t)


def _mm_bwd(res, dc):
    a, b, b_t = res
    return (_mm_call(dc, b_t, "mm_dx"), jnp.zeros_like(b), jnp.zeros_like(b_t),
            _mm_call(a.astype(BF16).T, dc, "mm_dw"))


mm.defvjp(_mm_fwd, _mm_bwd)


def _row_specs(arrs, tr):
    return [pl.BlockSpec((tr, a.shape[1]), lambda i: (i, 0)) for a in arrs]


def _par_specs(arrs):
    return [pl.BlockSpec(a.shape, lambda i: (0, 0)) for a in arrs]


def _row_fwd_call(fn, rows, params, out_cols, tr, name):
    seq = rows[0].shape[0]
    nr = len(rows)

    def body(*refs):
        vals = [r[...] for r in refs[:-1]]
        refs[-1][...] = fn(*vals)

    return pl.pallas_call(
        body, name=name, grid=(seq // tr,),
        in_specs=_row_specs(rows, tr) + _par_specs(params),
        out_specs=pl.BlockSpec((tr, out_cols), lambda i: (i, 0)),
        out_shape=jax.ShapeDtypeStruct((seq, out_cols), F32),
        compiler_params=_cp(("parallel",)),
    )(*rows, *params)


def _row_bwd_call(fn, rows, params, dy, tr, name):
    seq = rows[0].shape[0]
    nr, npar = len(rows), len(params)

    def body(*refs):
        ins = refs[:nr + npar]
        dy_ref = refs[nr + npar]
        outs = refs[nr + npar + 1:]
        _, vjp = jax.vjp(fn, *[r[...] for r in ins])
        cts = vjp(dy_ref[...])
        for o_ref, ct in zip(outs[:nr], cts[:nr]):
            o_ref[...] = ct

        @pl.when(pl.program_id(0) == 0)
        def _():
            for o_ref in outs[nr:]:
                o_ref[...] = jnp.zeros_like(o_ref)

        for o_ref, ct in zip(outs[nr:], cts[nr:]):
            o_ref[...] += ct

    return pl.pallas_call(
        body, name=name, grid=(seq // tr,),
        in_specs=_row_specs(rows, tr) + _par_specs(params) + _row_specs([dy], tr),
        out_specs=_row_specs(rows, tr) + _par_specs(params),
        out_shape=[jax.ShapeDtypeStruct(a.shape, F32) for a in (*rows, *params)],
        compiler_params=_cp(("arbitrary",)),
    )(*rows, *params, dy)


def _make_rowop(fn, nrows, out_cols, tr, name):
    @jax.custom_vjp
    def op(*args):
        return _row_fwd_call(fn, args[:nrows], args[nrows:], out_cols, tr, name + "_fwd")

    def fwd(*args):
        return op(*args), args

    def bwd(args, dy):
        return tuple(_row_bwd_call(fn, args[:nrows], args[nrows:], dy, tr, name + "_bwd"))

    op.defvjp(fwd, bwd)
    return op


def _rms_fn(x, w):
    return x * lax.rsqrt(jnp.mean(x * x, axis=-1, keepdims=True) + RMS_EPS) * w


def _merge_fn(pa, pb, pc, gl):
    d = D_MODEL
    return (jax.nn.sigmoid(gl[:, :d]) * pa + jax.nn.sigmoid(gl[:, d:2 * d]) * pb
            + jax.nn.sigmoid(gl[:, 2 * d:]) * pc)


def _relu2_fn(a):
    r = jnp.maximum(a, 0.0)
    return r * r


rms_op = _make_rowop(_rms_fn, 1, D_MODEL, 384, "rms")
merge_op = _make_rowop(_merge_fn, 4, D_MODEL, 192, "merge")
relu2_op = _make_rowop(_relu2_fn, 1, D_FF, 192, "relu2")


def _conv_taps(xext, w, nrows):
    z = None
    for j in range(CONV_K):
        sh = CONV_K - 1 - j
        xs = pltpu.roll(xext, sh, 0) if sh else xext
        term = w[j:j + 1, :] * xs[8:8 + nrows, :]
        z = term if z is None else z + term
    return z


def _halo(ref, start, ok):
    return jnp.where(ok, ref[pl.ds(pl.multiple_of(start, 8), 8), :], 0.0)


def _conv_fwd_call(x, w, b):
    seq, ch = x.shape
    nb = seq // BLK

    def body(x_ref, w_ref, b_ref, o_ref):
        w = w_ref[...]
        bias = b_ref[...]

        def step(i, carry):
            r0 = pl.multiple_of(i * BLK, BLK)
            xext = jnp.concatenate([_halo(x_ref, jnp.maximum(r0 - 8, 0), i > 0), x_ref[pl.ds(r0, BLK), :]], axis=0)
            o_ref[pl.ds(r0, BLK), :] = _silu(_conv_taps(xext, w, BLK) + bias)
            return carry

        lax.fori_loop(0, nb, step, 0)

    strip = pl.BlockSpec((seq, 128), lambda c: (0, c))
    return pl.pallas_call(
        body, name="conv_fwd", grid=(ch // 128,),
        in_specs=[strip, pl.BlockSpec((CONV_K, 128), lambda c: (0, c)), pl.BlockSpec((1, 128), lambda c: (0, c))],
        out_specs=strip, out_shape=jax.ShapeDtypeStruct(x.shape, F32),
        compiler_params=_cp(("parallel",)),
    )(x, w, b)


def _conv_bwd_call(x, w, b, dy):
    seq, ch = x.shape
    nb = seq // BLK

    def body(x_ref, w_ref, b_ref, dy_ref, dx_ref, dw_ref, db_ref):
        w = w_ref[...]
        bias = b_ref[...]

        def step(i, carry):
            r0 = pl.multiple_of(i * BLK, BLK)
            last = i == nb - 1
            nxt = jnp.minimum(r0 + BLK, seq - 8)
            xext = jnp.concatenate([_halo(x_ref, jnp.maximum(r0 - 8, 0), i > 0), x_ref[pl.ds(r0, BLK), :],
                                    _halo(x_ref, nxt, jnp.logical_not(last))], axis=0)
            dyext = jnp.concatenate([dy_ref[pl.ds(r0, BLK), :], _halo(dy_ref, nxt, jnp.logical_not(last))], axis=0)
            z = _conv_taps(xext, w, BLK + 8) + bias
            sg = jax.nn.sigmoid(z)
            dz = dyext * (sg * (1.0 + z * (1.0 - sg)))
            dx = None
            for j in range(CONV_K):
                sh = CONV_K - 1 - j
                dzs = pltpu.roll(dz, BLK + 8 - sh, 0) if sh else dz
                term = w[j:j + 1, :] * dzs[:BLK, :]
                dx = term if dx is None else dx + term
            dx_ref[pl.ds(r0, BLK), :] = dx
            dzm = dz[:BLK, :]
            out = []
            for j in range(CONV_K):
                sh = CONV_K - 1 - j
                xs = pltpu.roll(xext, sh, 0) if sh else xext
                out.append(carry[j] + jnp.sum(dzm * xs[8:8 + BLK, :], axis=0, keepdims=True))
            out.append(carry[CONV_K] + jnp.sum(dzm, axis=0, keepdims=True))
            return tuple(out)

        zero = jnp.zeros((1, 128), F32)
        acc = lax.fori_loop(0, nb, step, (zero,) * (CONV_K + 1))
        dw_ref[...] = jnp.concatenate(acc[:CONV_K], axis=0)
        db_ref[...] = acc[CONV_K]

    strip = pl.BlockSpec((seq, 128), lambda c: (0, c))
    wsp = pl.BlockSpec((CONV_K, 128), lambda c: (0, c))
    bsp = pl.BlockSpec((1, 128), lambda c: (0, c))
    return pl.pallas_call(
        body, name="conv_bwd", grid=(ch // 128,),
        in_specs=[strip, wsp, bsp, strip],
        out_specs=[strip, wsp, bsp],
        out_shape=[jax.ShapeDtypeStruct(x.shape, F32), jax.ShapeDtypeStruct(w.shape, F32),
                   jax.ShapeDtypeStruct(b.shape, F32)],
        compiler_params=_cp(("parallel",)),
    )(x, w, b, dy)


@jax.custom_vjp
def conv_silu(x, w, b):
    return _conv_fwd_call(x, w, b)


def _conv_silu_fwd(x, w, b):
    return _conv_fwd_call(x, w, b), (x, w, b)


def _conv_silu_bwd(res, dy):
    return tuple(_conv_bwd_call(*res, dy))


conv_silu.defvjp(_conv_silu_fwd, _conv_silu_bwd)


def _loss_call(h, wf, target):
    seq, d = h.shape
    nb = seq // BLK

    def body(h_ref, w_ref, t_ref, loss_ref, dh_ref, dw_ref):
        i = pl.program_id(0)
        live = (i > 0).astype(F32)
        tgt = t_ref[...]

        def fn(hh, ww):
            err = _rms_fn(hh, ww) - tgt
            return 0.5 * live * jnp.sum(jnp.mean(err * err, axis=-1, keepdims=True), axis=0, keepdims=True)

        val, vjp = jax.vjp(fn, h_ref[...], w_ref[...])
        dh, dw = vjp(jnp.ones((1, 1), F32))
        dh_ref[...] = dh

        @pl.when(i == 0)
        def _():
            loss_ref[...] = jnp.zeros_like(loss_ref)
            dw_ref[...] = jnp.zeros_like(dw_ref)

        loss_ref[...] += val
        dw_ref[...] += dw

    return pl.pallas_call(
        body, name="loss_head", grid=(nb,),
        in_specs=[pl.BlockSpec((BLK, d), lambda i: (i, 0)), pl.BlockSpec((1, d), lambda i: (0, 0)),
                  pl.BlockSpec((BLK, d), lambda i: (jnp.maximum(i - 1, 0), 0))],
        out_specs=[pl.BlockSpec((1, 1), lambda i: (0, 0)), pl.BlockSpec((BLK, d), lambda i: (i, 0)),
                   pl.BlockSpec((1, d), lambda i: (0, 0))],
        out_shape=[jax.ShapeDtypeStruct((1, 1), F32), jax.ShapeDtypeStruct(h.shape, F32),
                   jax.ShapeDtypeStruct((1, d), F32)],
        compiler_params=_cp(("arbitrary",)),
    )(h, wf, target)


def _make_loss_head(target):
    @jax.custom_vjp
    def head(h, wf):
        return _loss_call(h, wf, target)[0][0, 0]

    def fwd(h, wf):
        loss, dh, dw = _loss_call(h, wf, target)
        return loss[0, 0], (dh, dw)

    def bwd(res, g):
        return g * res[0], g * res[1]

    head.defvjp(fwd, bwd)
    return head


_IN_SEGS = (("q", 0, 1024), ("k", 1024, 1024), ("v", 2048, 1024), ("gate", 3072, 1024), ("z", 4112, 1024),
            ("xbc", 5136, 2048), ("cq", 7200, 1024), ("ck", 8224, 256), ("cv", 8480, 256), ("gl", 8736, 3072),
            ("b", 4096, 8), ("a", 4104, 8), ("dt", 7184, 16))
_IN_PAD = 96
_SPLIT = (1024, 1024, 1024, 1024, 1024, 2048, 1024, 512, 3072, 128)


@jax.custom_vjp
def split_cols(u):
    offs = [sum(_SPLIT[:i]) for i in range(len(_SPLIT))]
    return tuple(u[:, o:o + s] for o, s in zip(offs, _SPLIT))


def _split_fwd(u):
    return split_cols(u), None


def _split_bwd(_, cts):
    return (jnp.concatenate(cts, axis=1),)


split_cols.defvjp(_split_fwd, _split_bwd)


def _layer(h, p, wb, slot):
    def proj(t, name):
        return mm(t, wb[name], wb[name + "_t"], slot[name])

    u = proj(rms_op(h, p["norm1_w"].reshape(1, -1)), "w_in")
    q_pre, k_pre, v_pre, gate, z, xbc_pre, cq, ckv, gl, small = split_cols(u)

    gcw = p["gdn_conv_w"]
    nob = jnp.zeros((1, GDN_H * GDN_D), F32)
    qa = conv_silu(q_pre, gcw[:, :1024], nob)
    ka = conv_silu(k_pre, gcw[:, 1024:2048], nob)
    va = conv_silu(v_pre, gcw[:, 2048:], nob)
    y_gdn = gdn_core(qa, ka, va, gate, small, p["gdn_a_log"].reshape(GDN_H, 1, 1),
                     p["gdn_dt_bias"].reshape(GDN_H, 1, 1), p["gdn_norm_w"].reshape(1, GDN_D))

    xbc = conv_silu(xbc_pre, p["ssd_conv_w"], p["ssd_conv_b"].reshape(1, -1))
    y_ssd = ssd_core(xbc, z, small, p["ssd_dt_bias"].reshape(SSD_H, 1, 1), p["ssd_a_log"].reshape(SSD_H, 1, 1),
                     p["ssd_d"].reshape(SSD_H, 1, 1), p["ssd_norm_w"].reshape(SSD_H, 1, SSD_P))

    y_swa = swa_core(cq, ckv, p["swa_sinks"].reshape(SWA_QH, 1, 1))

    merged = merge_op(proj(y_gdn, "w_proj_gdn"), proj(y_ssd, "w_proj_ssd"), proj(y_swa, "w_proj_swa"), gl)
    h = h + proj(merged, "w_out")
    a1 = proj(rms_op(h, p["norm2_w"].reshape(1, -1)), "w_up")
    return h + proj(relu2_op(a1), "w_down")


_MATMUL = ("w_in", "w_proj_gdn", "w_proj_ssd", "w_proj_swa", "w_out", "w_up", "w_down")
_PER_LAYER = ("norm1_w", "gdn_conv_w", "gdn_a_log", "gdn_dt_bias", "gdn_norm_w", "ssd_conv_w", "ssd_conv_b",
              "ssd_dt_bias", "ssd_a_log", "ssd_d", "ssd_norm_w", "swa_sinks", "norm2_w")


def _local_loss(x, params, slots, wb, loss_head):
    h = jnp.concatenate([jnp.zeros((NPAD, D_MODEL), F32), params["meta_tokens"], x], axis=0)
    for l in range(len(wb)):
        h = _layer(h, {n: params[n][l] for n in _PER_LAYER}, wb[l], slots[l])
    return loss_head(h, params["final_norm_w"].reshape(1, -1))


_IN_SHARD = 1476


def _in_pieces():
    out = []
    for _, s, n in _IN_SEGS:
        c = s
        while c < s + n:
            d = c // _IN_SHARD
            e = min(s + n, (d + 1) * _IN_SHARD)
            out.append((d, c - d * _IN_SHARD, e - d * _IN_SHARD))
            c = e
    return out


def _in_pieces_back():
    start, off = {}, 0
    for _, s, n in _IN_SEGS:
        start[s] = off
        off += n
    out = [[] for _ in range(N_DEV)]
    for _, s, n in sorted(_IN_SEGS, key=lambda t: t[1]):
        c = s
        while c < s + n:
            d = c // _IN_SHARD
            e = min(s + n, (d + 1) * _IN_SHARD)
            out[d].append((start[s] + c - s, start[s] + e - s))
            c = e
    return out


def _regroup_w_in(stacked):
    parts = [stacked[d, :, lo:hi] for d, lo, hi in _in_pieces()]
    return jnp.concatenate(parts + [jnp.zeros((D_MODEL, _IN_PAD), stacked.dtype)], axis=1)


def _ungroup_w_in(g):
    return jnp.stack([jnp.concatenate([g[:, lo:hi] for lo, hi in pieces], axis=1) for pieces in _in_pieces_back()])


def _position():
    return lax.axis_index("x"), lax.axis_index("y"), lax.axis_index("c")


_ANY = pl.BlockSpec(memory_space=pl.ANY)


def _chip_of(x, y, k):
    return (1 - x if k & 1 else x, 1 - y if k & 2 else y)


def _allgather_call(shards, name):
    n = len(shards)

    def body(*refs):
        x_refs, out_refs = refs[:n], refs[n:2 * n]
        send_sems, recv_sems, local_sems = refs[2 * n:]
        x, y, c = _position()
        me, sibling = (x, y, c), (x, y, 1 - c)
        chips = [_chip_of(x, y, k) for k in (1, 2, 3)]

        def slab(a, px, py, pc):
            return out_refs[a].at[4 * px + 2 * py + pc]

        def copy(a, k, block, to, src=None):
            return pltpu.make_async_remote_copy(
                src_ref=slab(a, *block) if src is None else src, dst_ref=slab(a, *block),
                send_sem=send_sems.at[7 * a + k], recv_sem=recv_sems.at[7 * a + k], device_id=to, device_id_type=MESH)

        mine = [pltpu.make_async_copy(x_refs[a], slab(a, *me), local_sems.at[a]) for a in range(n)]
        first = []
        for a in range(n):
            mine[a].start()
            first.append(copy(a, 0, me, sibling, src=x_refs[a]))
            first += [copy(a, 1 + j, me, (*chip, c), src=x_refs[a]) for j, chip in enumerate(chips)]
        for cp in first:
            cp.start()
        passed = []
        for j, chip in enumerate(chips):
            for a in range(n):
                copy(a, 1 + j, (*chip, c), me).wait_recv()
                passed.append(copy(a, 4 + j, (*chip, c), sibling))
                passed[-1].start()
        for a in range(n):
            copy(a, 0, sibling, me).wait_recv()
        for j, chip in enumerate(chips):
            for a in range(n):
                copy(a, 4 + j, (*chip, 1 - c), me).wait_recv()
        for cp in first + passed:
            cp.wait_send()
        for cp in mine:
            cp.wait()

    return pl.pallas_call(
        body, name=name,
        out_shape=[jax.ShapeDtypeStruct((N_DEV, *s.shape), s.dtype) for s in shards],
        in_specs=[_ANY] * n, out_specs=[_ANY] * n,
        scratch_shapes=[pltpu.SemaphoreType.DMA((7 * n,)), pltpu.SemaphoreType.DMA((7 * n,)),
                        pltpu.SemaphoreType.DMA((n,))],
    )(*shards)


def _sibling_exchange_call(grads, name):
    n = len(grads)

    def body(*refs):
        g_refs, out_refs = refs[:n], refs[n:2 * n]
        send_sems, recv_sems = refs[2 * n:]
        x, y, c = _position()
        copies = []
        for a in range(n):
            for k in range(4):
                px, py = _chip_of(x, y, k)
                copies.append(pltpu.make_async_remote_copy(
                    src_ref=g_refs[a].at[4 * px + 2 * py + (1 - c)], dst_ref=out_refs[a].at[k],
                    send_sem=send_sems.at[4 * a + k], recv_sem=recv_sems.at[4 * a + k],
                    device_id=(x, y, 1 - c), device_id_type=MESH))
        for cp in copies:
            cp.start()
        for cp in copies:
            cp.wait_recv()
        for cp in copies:
            cp.wait_send()

    return pl.pallas_call(
        body, name=name,
        out_shape=[jax.ShapeDtypeStruct((4, *g.shape[1:]), g.dtype) for g in grads],
        in_specs=[_ANY] * n, out_specs=[_ANY] * n,
        scratch_shapes=[pltpu.SemaphoreType.DMA((4 * n,)), pltpu.SemaphoreType.DMA((4 * n,))],
    )(*grads)


def _chip_exchange_call(partials, name):
    n = len(partials)

    def body(*refs):
        p_refs, out_refs = refs[:n], refs[n:2 * n]
        send_sems, recv_sems = refs[2 * n:]
        x, y, c = _position()
        copies = []
        for a in range(n):
            for k in (1, 2, 3):
                copies.append(pltpu.make_async_remote_copy(
                    src_ref=p_refs[a].at[k - 1], dst_ref=out_refs[a].at[k - 1],
                    send_sem=send_sems.at[3 * a + k - 1], recv_sem=recv_sems.at[3 * a + k - 1],
                    device_id=(*_chip_of(x, y, k), c), device_id_type=MESH))
        for cp in copies:
            cp.start()
        for cp in copies:
            cp.wait_recv()
        for cp in copies:
            cp.wait_send()

    return pl.pallas_call(
        body, name=name,
        out_shape=[jax.ShapeDtypeStruct(p.shape, p.dtype) for p in partials],
        in_specs=[_ANY] * n, out_specs=[_ANY] * n,
        scratch_shapes=[pltpu.SemaphoreType.DMA((3 * n,)), pltpu.SemaphoreType.DMA((3 * n,))],
    )(*partials)


def _chip_partial_call(own, sib, tr, name):
    _, r, c = own.shape

    def body(g_ref, s_ref, own_ref, out_ref):
        own_ref[...] = g_ref[0] + s_ref[0]
        for k in (1, 2, 3):
            out_ref[k - 1] = (g_ref[k] + s_ref[k]).astype(BF16)

    four = pl.BlockSpec((4, tr, c), lambda i: (0, i, 0))
    return pl.pallas_call(
        body, name=name, grid=(r // tr,),
        in_specs=[four, four],
        out_specs=[pl.BlockSpec((tr, c), lambda i: (i, 0)), pl.BlockSpec((3, tr, c), lambda i: (0, i, 0))],
        out_shape=[jax.ShapeDtypeStruct((r, c), F32), jax.ShapeDtypeStruct((3, r, c), BF16)],
        compiler_params=_cp(("parallel",)),
    )(own, sib)


def _adamw_call(parts, w, m, v, tr, name):
    r, c = w.shape
    npart = len(parts)

    def body(*refs):
        p_refs = refs[:npart]
        w_ref, m_ref, v_ref, g_ref, d_ref, nm_ref, nv_ref = refs[npart:]
        g = None
        for p_ref in p_refs:
            for s in range(p_ref.shape[0]):
                term = p_ref[s].astype(F32)
                g = term if g is None else g + term
        nm = ADAM_B1 * m_ref[...] + (1.0 - ADAM_B1) * g
        nv = ADAM_B2 * v_ref[...] + (1.0 - ADAM_B2) * (g * g)
        m_hat = nm / (1.0 - ADAM_B1 ** ADAM_STEP)
        v_hat = nv / (1.0 - ADAM_B2 ** ADAM_STEP)
        g_ref[...] = g
        d_ref[...] = -ADAM_LR * (m_hat / (jnp.sqrt(v_hat) + ADAM_EPS) + ADAM_WD * w_ref[...])
        nm_ref[...] = nm
        nv_ref[...] = nv

    flat = pl.BlockSpec((tr, c), lambda i: (i, 0))
    return pl.pallas_call(
        body, name=name, grid=(r // tr,),
        in_specs=[pl.BlockSpec((p.shape[0], tr, c), lambda i: (0, i, 0)) for p in parts] + [flat, flat, flat],
        out_specs=[flat] * 4,
        out_shape=[jax.ShapeDtypeStruct((r, c), F32)] * 4,
        compiler_params=_cp(("parallel",)),
    )(*parts, w, m, v)


_WEIGHTS = ("meta_tokens", "norm1_w", "w_in", "gdn_conv_w", "gdn_a_log", "gdn_dt_bias", "gdn_norm_w", "ssd_conv_w",
            "ssd_conv_b", "ssd_dt_bias", "ssd_a_log", "ssd_d", "ssd_norm_w", "swa_sinks", "w_proj_gdn", "w_proj_ssd",
            "w_proj_swa", "w_out", "norm2_w", "w_up", "w_down", "final_norm_w")
_SHARD_AXIS = {"meta_tokens": 1, "w_in": 2, "gdn_conv_w": 2, "ssd_conv_w": 2, "w_proj_gdn": 1, "w_proj_ssd": 1,
               "w_proj_swa": 1, "w_out": 1, "w_up": 2, "w_down": 1}
_BIG = tuple(n for n in _WEIGHTS if n in _SHARD_AXIS)
_SMALL = tuple(n for n in _WEIGHTS if n not in _SHARD_AXIS)
FLAT_C = 1024


def _pack(arrs, rows, lead=()):
    flat = jnp.concatenate([a.reshape(*lead, -1) for a in arrs], axis=-1)
    pad = rows * FLAT_C - flat.shape[-1]
    flat = jnp.pad(flat, [(0, 0)] * len(lead) + [(0, pad)])
    return flat.reshape(*lead, rows, FLAT_C)


def _unpack(flat, shapes, lead=()):
    flat = flat.reshape(*lead, -1)
    out, off = [], 0
    for s in shapes:
        n = math.prod(s)
        out.append(flat[..., off:off + n].reshape(*lead, *s))
        off += n
    return out


def _rows_for(shapes):
    n = sum(math.prod(s) for s in shapes)
    return -(-n // (FLAT_C * 8)) * 8


def _rows_tile(r, c):
    if r <= 256:
        return r
    return 128 if c > 1024 else 256


def _join(stacked, axis):
    moved = jnp.moveaxis(stacked, 0, axis)
    return moved.reshape(*moved.shape[:axis], -1, *moved.shape[axis + 2:])


def kernel(x, meta_tokens, norm1_w, w_in, gdn_conv_w, gdn_a_log, gdn_dt_bias, gdn_norm_w, ssd_conv_w, ssd_conv_b,
           ssd_dt_bias, ssd_a_log, ssd_d, ssd_norm_w, swa_sinks, w_proj_gdn, w_proj_ssd, w_proj_swa, w_out, norm2_w,
           w_up, w_down, final_norm_w, loss_target, m_meta_tokens, m_norm1_w, m_w_in, m_gdn_conv_w, m_gdn_a_log,
           m_gdn_dt_bias, m_gdn_norm_w, m_ssd_conv_w, m_ssd_conv_b, m_ssd_dt_bias, m_ssd_a_log, m_ssd_d, m_ssd_norm_w,
           m_swa_sinks, m_w_proj_gdn, m_w_proj_ssd, m_w_proj_swa, m_w_out, m_norm2_w, m_w_up, m_w_down,
           m_final_norm_w, v_meta_tokens, v_norm1_w, v_w_in, v_gdn_conv_w, v_gdn_a_log, v_gdn_dt_bias, v_gdn_norm_w,
           v_ssd_conv_w, v_ssd_conv_b, v_ssd_dt_bias, v_ssd_a_log, v_ssd_d, v_ssd_norm_w, v_swa_sinks, v_w_proj_gdn,
           v_w_proj_ssd, v_w_proj_swa, v_w_out, v_norm2_w, v_w_up, v_w_down, v_final_norm_w):
    args = (meta_tokens, norm1_w, w_in, gdn_conv_w, gdn_a_log, gdn_dt_bias, gdn_norm_w, ssd_conv_w, ssd_conv_b,
            ssd_dt_bias, ssd_a_log, ssd_d, ssd_norm_w, swa_sinks, w_proj_gdn, w_proj_ssd, w_proj_swa, w_out, norm2_w,
            w_up, w_down, final_norm_w, m_meta_tokens, m_norm1_w, m_w_in, m_gdn_conv_w, m_gdn_a_log,
            m_gdn_dt_bias, m_gdn_norm_w, m_ssd_conv_w, m_ssd_conv_b, m_ssd_dt_bias, m_ssd_a_log, m_ssd_d, m_ssd_norm_w,
            m_swa_sinks, m_w_proj_gdn, m_w_proj_ssd, m_w_proj_swa, m_w_out, m_norm2_w, m_w_up, m_w_down,
            m_final_norm_w, v_meta_tokens, v_norm1_w, v_w_in, v_gdn_conv_w, v_gdn_a_log, v_gdn_dt_bias, v_gdn_norm_w,
            v_ssd_conv_w, v_ssd_conv_b, v_ssd_dt_bias, v_ssd_a_log, v_ssd_d, v_ssd_norm_w, v_swa_sinks, v_w_proj_gdn,
            v_w_proj_ssd, v_w_proj_swa, v_w_out, v_norm2_w, v_w_up, v_w_down, v_final_norm_w)
    nw = len(_WEIGHTS)
    w = dict(zip(_WEIGHTS, args[:nw]))
    m = dict(zip(_WEIGHTS, args[nw:2 * nw]))
    v = dict(zip(_WEIGHTS, args[2 * nw:]))

    depth = w["w_in"].shape[0]
    small_shapes = [w[n].shape for n in _SMALL]
    small_rows = _rows_for(small_shapes)

    def flat2(t):
        return t.reshape(-1, t.shape[-1])

    sent = [flat2(w[n]).astype(BF16) if n in _MATMUL else flat2(w[n]) for n in _BIG]
    gathered = dict(zip(_BIG, _allgather_call(sent, "gather_weights")))

    def layer_of(name, l):
        t = gathered[name]
        return t.reshape(N_DEV, depth, t.shape[1] // depth, t.shape[2])[:, l]

    wb, slots = [], []
    for l in range(depth):
        full = {"w_in": _regroup_w_in(layer_of("w_in", l)),
                "w_up": layer_of("w_up", l).transpose(1, 0, 2).reshape(D_MODEL, D_FF),
                "w_down": layer_of("w_down", l).reshape(D_FF, D_MODEL)}
        for n in ("w_proj_gdn", "w_proj_ssd", "w_proj_swa", "w_out"):
            full[n] = layer_of(n, l).reshape(D_MODEL, D_MODEL)
        slots.append({n: jnp.zeros(t.shape, F32) for n, t in full.items()})
        full.update({n + "_t": t.T for n, t in list(full.items())})
        wb.append(full)

    tiny = {n: gathered[n].reshape(N_DEV, *w[n].shape) for n in _BIG if n not in _MATMUL}
    small = {n: w[n] for n in _SMALL}
    loss_head = _make_loss_head(loss_target[0])

    def local_loss(x_rows, slots, tiny, small):
        joined = {n: _join(tiny[n], _SHARD_AXIS[n]) for n in tiny}
        return _local_loss(x_rows, {**joined, **small}, slots, wb, loss_head)

    loss, (gx, g_slots, g_tiny, g_small) = jax.value_and_grad(local_loss, argnums=(0, 1, 2, 3))(
        x[0], slots, tiny, small)
    loss = lax.psum(loss, ("x", "y", "c"))

    def per_device(name, g):
        if name == "w_in":
            return _ungroup_w_in(g)
        if name == "w_up":
            return g.reshape(D_MODEL, N_DEV, D_FF // N_DEV).transpose(1, 0, 2)
        return g.reshape(N_DEV, g.shape[0] // N_DEV, g.shape[1])

    contrib = []
    for n in _BIG:
        if n in _MATMUL:
            t = jnp.stack([per_device(n, g_slots[l][n]) for l in range(depth)], axis=1)
        else:
            t = g_tiny[n]
        contrib.append(t.reshape(N_DEV, -1, t.shape[-1]))

    from_sibling = _sibling_exchange_call(contrib, "grads_to_sibling")
    x_pos, y_pos, c_pos = _position()
    slabs = [4 * px + 2 * py + c_pos for px, py in (_chip_of(x_pos, y_pos, k) for k in range(4))]
    mine = [jnp.stack([lax.dynamic_index_in_dim(g, s, 0, keepdims=False) for s in slabs]) for g in contrib]
    own, outgoing = [], []
    for n, g, s in zip(_BIG, mine, from_sibling):
        o, p = _chip_partial_call(g, s, _rows_tile(g.shape[1], g.shape[2]), "chip_partial_" + n)
        own.append(o)
        outgoing.append(p)
    incoming = _chip_exchange_call(outgoing, "grads_to_chips")

    by_name = {}
    for n, o, r in zip(_BIG, own, incoming):
        res = _adamw_call([o[None], r], flat2(w[n]), flat2(m[n]), flat2(v[n]), _rows_tile(o.shape[0], o.shape[1]),
                          "adamw_" + n)
        by_name[n] = [t.reshape(w[n].shape) for t in res]

    small_parts = _allgather_call([_pack([g_small[n] for n in _SMALL], small_rows)], "gather_small_grads")
    small_out = _adamw_call(small_parts, *[_pack([d[n] for n in _SMALL], small_rows) for d in (w, m, v)], small_rows,
                            "adamw_replicated")
    for kind in range(4):
        for n, t in zip(_SMALL, _unpack(small_out[kind], small_shapes)):
            by_name.setdefault(n, [None] * 4)[kind] = t

    outs = [by_name[n][kind] for kind in range(4) for n in _WEIGHTS]
    return (loss, gx[None], *outs)
```

```python
import functools
import math

import jax
import jax.numpy as jnp
from jax import lax
from jax.experimental import pallas as pl
from jax.experimental.pallas import tpu as pltpu

F32 = jnp.float32
BF16 = jnp.bfloat16
HI = lax.Precision.HIGH
NEG = -1e30

D_MODEL = 1024
N_META = 16
BLK = 128
NPAD = BLK - N_META
RMS_EPS = 1e-6
L2_EPS = 1e-6
CONV_K = 4

GDN_H, GDN_D, GDN_C = 8, 128, 64
SSD_H, SSD_P, SSD_G, SSD_N = 16, 64, 4, 128
SSD_HPG = SSD_H // SSD_G
SWA_QH, SWA_KVH, SWA_D = 16, 4, 64
SWA_REP = SWA_QH // SWA_KVH
D_FF = 4 * D_MODEL

N_DEV = 8
MESH = pl.DeviceIdType.MESH

ADAM_LR, ADAM_B1, ADAM_B2, ADAM_EPS, ADAM_WD, ADAM_STEP = 0.001, 0.9, 0.999, 1e-08, 0.01, 10

VMEM_LIMIT = 56 * 1024 * 1024


def _cp(sem=None):
    return pltpu.CompilerParams(dimension_semantics=sem, vmem_limit_bytes=VMEM_LIMIT)


def _dot(a, b, ca, cb, prec=HI):
    return lax.dot_general(a, b, (((ca,), (cb,)), ((), ())), precision=prec, preferred_element_type=F32)


def _nn(a, b, prec=HI):
    return _dot(a, b, 1, 0, prec)


def _nt(a, b, prec=HI):
    return _dot(a, b, 1, 1, prec)


def _tn(a, b, prec=HI):
    return _dot(a, b, 0, 0, prec)


def _bdot(a, b, ca, cb):
    return lax.dot_general(a.astype(BF16), b.astype(BF16), (((ca,), (cb,)), ((), ())), preferred_element_type=F32)


@jax.custom_vjp
def _lo_nn(a, b):
    return _bdot(a, b, 1, 0)


_lo_nn.defvjp(lambda a, b: (_bdot(a, b, 1, 0), (a, b)),
              lambda r, d: (_bdot(d, r[1], 1, 1), _bdot(r[0], d, 0, 0)))


@jax.custom_vjp
def _lo_nt(a, b):
    return _bdot(a, b, 1, 1)


_lo_nt.defvjp(lambda a, b: (_bdot(a, b, 1, 1), (a, b)),
              lambda r, d: (_bdot(d, r[1], 1, 0), _bdot(d, r[0], 0, 0)))


@jax.custom_vjp
def _lo_tn(a, b):
    return _bdot(a, b, 0, 0)


_lo_tn.defvjp(lambda a, b: (_bdot(a, b, 0, 0), (a, b)),
              lambda r, d: (_bdot(r[1], d, 1, 1), _bdot(r[0], d, 1, 0)))


def _iota2(n, m, axis):
    return lax.broadcasted_iota(jnp.int32, (n, m), axis)


def _silu(x):
    return x * jax.nn.sigmoid(x)


def _softplus(x):
    return jnp.maximum(x, 0.0) + jnp.log(1.0 + jnp.exp(-jnp.abs(x)))


def _row_of(col):
    n = col.shape[0]
    return jnp.broadcast_to(col, (n, n)).T


def _cumsum_col(col):
    n = col.shape[0]
    tril = (_iota2(n, n, 0) >= _iota2(n, n, 1)).astype(F32)
    return _nn(tril, col)


def _tri_inv(a):
    n = a.shape[0]
    r, c = _iota2(n, n, 0), _iota2(n, n, 1)
    eye = (r == c).astype(F32)
    blk = jnp.right_shift(r, 4) == jnp.right_shift(c, 4)
    d = jnp.where(blk, a, 0.0)
    off = a - d
    d2 = _nn(d, d)
    d4 = _nn(d2, d2)
    d8 = _nn(d4, d4)
    td = _nn(_nn(_nn(eye - d, eye + d2), eye + d4), eye + d8)
    m = _nn(td, off)
    m2 = _nn(m, m)
    return _nn(_nn(eye - m, eye + m2), td)


@jax.custom_vjp
def _tri_solve(a, a_t, rhs):
    return _nn(_tri_inv(a), rhs)


def _tri_solve_fwd(a, a_t, rhs):
    sol = _nn(_tri_inv(a), rhs)
    return sol, (a_t, sol)


def _tri_solve_bwd(res, dsol):
    a_t, sol = res
    drhs = _nn(_tri_inv(a_t), dsol)
    return -_nt(drhs, sol), jnp.zeros_like(a_t), drhs


_tri_solve.defvjp(_tri_solve_fwd, _tri_solve_bwd)


def _gdn_chunk(qa, ka, va, gate, a_raw, b_raw, s, a_log, dt_bias, norm_w, valid):
    c = qa.shape[0]
    q = qa * lax.rsqrt(jnp.sum(qa * qa, axis=-1, keepdims=True) + L2_EPS) * (GDN_D ** -0.5)
    k = ka * lax.rsqrt(jnp.sum(ka * ka, axis=-1, keepdims=True) + L2_EPS)
    beta = jax.nn.sigmoid(b_raw)
    g = -jnp.exp(a_log) * _softplus(a_raw + dt_bias) * valid
    gam = _cumsum_col(g)
    gam_row = _row_of(gam)
    r, cc = _iota2(c, c, 0), _iota2(c, c, 1)
    decay = jnp.exp(jnp.where(r >= cc, gam - gam_row, NEG))
    kb = k * beta
    a = jnp.where(r > cc, _lo_nt(kb, k) * decay, 0.0)
    a_t = lax.stop_gradient(jnp.where(cc > r, _bdot(k, kb, 1, 1) * jnp.exp(jnp.where(cc >= r, gam_row - gam, NEG)), 0.0))
    egam = jnp.exp(gam)
    sol = _tri_solve(a, a_t, jnp.concatenate([va * beta, kb * egam], axis=1))
    u = sol[:, :GDN_D]
    w = sol[:, GDN_D:]
    attn = _lo_nt(q, k) * decay
    g_last = jnp.sum(g, axis=0, keepdims=True)
    k_tail = k * jnp.exp(g_last - gam)
    v_new = u - _lo_nn(w, s)
    o = _lo_nn(q * egam, s) + _lo_nn(attn, v_new)
    s_new = s * jnp.exp(g_last) + _lo_tn(k_tail, v_new)
    y = o * lax.rsqrt(jnp.mean(o * o, axis=-1, keepdims=True) + RMS_EPS) * norm_w * _silu(gate)
    return y, s_new


def _valid_col(row0, n):
    return (row0 + _iota2(n, 1, 0) >= NPAD).astype(F32)


GDN_HB = GDN_H

SM_B, SM_A, SM_DT, SM_W = 0, 8, 16, 128


def _pick_cols(sm, first, n):
    return jnp.stack([sm[:, first + j:first + j + 1] for j in range(n)])


def _spread_cols(cols, first):
    lane = _iota2(1, SM_W, 1)
    out = None
    for j in range(cols.shape[0]):
        term = cols[j] * (lane == first + j).astype(F32)
        out = term if out is None else out + term
    return out


def _gdn_specs(nc, rev):
    ci = (lambda i: nc - 1 - i) if rev else (lambda i: i)
    hb = GDN_HB
    tile = pl.BlockSpec((GDN_C, hb * GDN_D), lambda h, i: (ci(i), h))
    col = pl.BlockSpec((GDN_C, SM_W), lambda h, i: (ci(i), 0))
    scal = pl.BlockSpec((hb, 1, 1), lambda h, i: (h, 0, 0))
    nw = pl.BlockSpec((1, GDN_D), lambda h, i: (0, 0))
    st = pl.BlockSpec((hb, 1, GDN_D, GDN_D), lambda h, i: (h, ci(i), 0, 0))
    return tile, col, scal, nw, st


def _lanes(j):
    return slice(j * GDN_D, (j + 1) * GDN_D)


def _by_head(ref):
    return jnp.stack([ref[:, _lanes(j)] for j in range(GDN_HB)])


def _gdn_fwd_call(q, k, v, gate, small, a_log, dt_bias, norm_w):
    seq = q.shape[0]
    nc = seq // GDN_C
    tile, col, scal, nw, st = _gdn_specs(nc, False)

    def body(q_ref, k_ref, v_ref, g_ref, sm_ref, al_ref, dt_ref, nw_ref, y_ref, st_ref, s_scr):
        i = pl.program_id(1)

        @pl.when(i == 0)
        def _():
            s_scr[...] = jnp.zeros_like(s_scr)

        s = s_scr[...]
        st_ref[:, 0] = s
        sm = sm_ref[...]
        fn = jax.vmap(functools.partial(_gdn_chunk, valid=_valid_col(i * GDN_C, GDN_C)))
        y, s_new = fn(_by_head(q_ref), _by_head(k_ref), _by_head(v_ref), _by_head(g_ref),
                      _pick_cols(sm, SM_A, GDN_H), _pick_cols(sm, SM_B, GDN_H), s,
                      al_ref[...], dt_ref[...], jnp.broadcast_to(nw_ref[...], (GDN_HB, 1, GDN_D)))
        for j in range(GDN_HB):
            y_ref[:, _lanes(j)] = y[j]
        s_scr[...] = s_new

    return pl.pallas_call(
        body, name="gdn_fwd", grid=(GDN_H // GDN_HB, nc),
        in_specs=[tile, tile, tile, tile, col, scal, scal, nw],
        out_specs=[tile, st],
        out_shape=[jax.ShapeDtypeStruct((seq, GDN_H * GDN_D), F32),
                   jax.ShapeDtypeStruct((GDN_H, nc, GDN_D, GDN_D), F32)],
        scratch_shapes=[pltpu.VMEM((GDN_HB, GDN_D, GDN_D), F32)],
        compiler_params=_cp(("parallel", "arbitrary")),
    )(q, k, v, gate, small, a_log, dt_bias, norm_w)


def _gdn_bwd_call(q, k, v, gate, small, a_log, dt_bias, norm_w, states, dy):
    seq = q.shape[0]
    nc = seq // GDN_C
    tile, col, scal, nw, st = _gdn_specs(nc, True)
    nwh = pl.BlockSpec((GDN_HB, 1, GDN_D), lambda h, i: (h, 0, 0))

    def body(q_ref, k_ref, v_ref, g_ref, sm_ref, al_ref, dt_ref, nw_ref, st_ref, dy_ref,
             dq_ref, dk_ref, dv_ref, dg_ref, dsm_ref, dal_ref, ddt_ref, dnw_ref, ds_scr):
        i = pl.program_id(1)

        @pl.when(i == 0)
        def _():
            ds_scr[...] = jnp.zeros_like(ds_scr)
            dal_ref[...] = jnp.zeros_like(dal_ref)
            ddt_ref[...] = jnp.zeros_like(ddt_ref)
            dnw_ref[...] = jnp.zeros_like(dnw_ref)

        sm = sm_ref[...]
        fn = jax.vmap(functools.partial(_gdn_chunk, valid=_valid_col((nc - 1 - i) * GDN_C, GDN_C)))
        _, vjp = jax.vjp(fn, _by_head(q_ref), _by_head(k_ref), _by_head(v_ref), _by_head(g_ref),
                         _pick_cols(sm, SM_A, GDN_H), _pick_cols(sm, SM_B, GDN_H), st_ref[:, 0], al_ref[...],
                         dt_ref[...], jnp.broadcast_to(nw_ref[...], (GDN_HB, 1, GDN_D)))
        dq, dk, dv, dg, da, db, ds, dal, ddt, dnw = vjp((_by_head(dy_ref), ds_scr[...]))
        for j in range(GDN_HB):
            dq_ref[:, _lanes(j)] = dq[j]
            dk_ref[:, _lanes(j)] = dk[j]
            dv_ref[:, _lanes(j)] = dv[j]
            dg_ref[:, _lanes(j)] = dg[j]
        dsm_ref[...] = _spread_cols(da, SM_A) + _spread_cols(db, SM_B)
        ds_scr[...] = ds
        dal_ref[...] += dal
        ddt_ref[...] += ddt
        dnw_ref[...] += dnw

    big = jax.ShapeDtypeStruct((seq, GDN_H * GDN_D), F32)
    return pl.pallas_call(
        body, name="gdn_bwd", grid=(GDN_H // GDN_HB, nc),
        in_specs=[tile, tile, tile, tile, col, scal, scal, nw, st, tile],
        out_specs=[tile, tile, tile, tile, col, scal, scal, nwh],
        out_shape=[big, big, big, big, jax.ShapeDtypeStruct(small.shape, F32),
                   jax.ShapeDtypeStruct((GDN_H, 1, 1), F32), jax.ShapeDtypeStruct((GDN_H, 1, 1), F32),
                   jax.ShapeDtypeStruct((GDN_H, 1, GDN_D), F32)],
        scratch_shapes=[pltpu.VMEM((GDN_HB, GDN_D, GDN_D), F32)],
        compiler_params=_cp(("parallel", "arbitrary")),
    )(q, k, v, gate, small, a_log, dt_bias, norm_w, states, dy)


@jax.custom_vjp
def gdn_core(q, k, v, gate, small, a_log, dt_bias, norm_w):
    return _gdn_fwd_call(q, k, v, gate, small, a_log, dt_bias, norm_w)[0]


def _gdn_core_fwd(q, k, v, gate, small, a_log, dt_bias, norm_w):
    y, states = _gdn_fwd_call(q, k, v, gate, small, a_log, dt_bias, norm_w)
    return y, (q, k, v, gate, small, a_log, dt_bias, norm_w, states)


def _gdn_core_bwd(res, dy):
    dq, dk, dv, dg, dsm, dal, ddt, dnw = _gdn_bwd_call(*res, dy)
    return dq, dk, dv, dg, dsm, dal, ddt, jnp.sum(dnw, axis=0)


gdn_core.defvjp(_gdn_core_fwd, _gdn_core_bwd)


def _ssd_head(x, z, dt_raw, h, dt_bias, a_log, d_skip, bm, cm, cb, valid):
    c = bm.shape[0]
    r, cc = _iota2(c, c, 0), _iota2(c, c, 1)
    dtp = _softplus(dt_raw + dt_bias)
    x = x * valid
    adt = -jnp.exp(a_log) * dtp * valid
    xdt = x * dtp
    acum = _cumsum_col(adt)
    lmat = jnp.exp(jnp.where(r >= cc, acum - _row_of(acum), NEG))
    a_last = jnp.sum(adt, axis=0, keepdims=True)
    y = _lo_nn(cb * lmat, xdt) + _lo_nt(cm * jnp.exp(acum), h) + d_skip * x
    h_new = h * jnp.exp(a_last) + _lo_tn(xdt * jnp.exp(a_last - acum), bm)
    return y * _silu(z), h_new


def _ssd_chunk(xs, z, bm, cm, dt_raw, h, dt_bias, a_log, d_skip, norm_w, valid):
    nh, c, p = xs.shape
    ng = bm.shape[0]
    hpg = nh // ng
    bm = bm * valid
    cm = cm * valid
    cb = jax.vmap(_lo_nt)(cm, bm)
    per_head = lambda t: jnp.repeat(t, hpg, axis=0)
    ys, hs = jax.vmap(functools.partial(_ssd_head, valid=valid))(
        xs, z, dt_raw, h, dt_bias, a_log, d_skip, per_head(bm), per_head(cm), per_head(cb))
    ss = jnp.sum(jnp.sum(ys * ys, axis=-1, keepdims=True).reshape(ng, hpg, c, 1), axis=1, keepdims=True)
    rstd = lax.rsqrt(ss / (hpg * p) + RMS_EPS)
    return (ys.reshape(ng, hpg, c, p) * rstd).reshape(nh, c, p) * norm_w, hs


SSD_INNER = SSD_H * SSD_P
SSD_BC = SSD_G * SSD_N


def _split_lanes(t, n, w):
    return jnp.stack([t[:, j * w:(j + 1) * w] for j in range(n)])


def _join_lanes(t):
    return jnp.concatenate([t[j] for j in range(t.shape[0])], axis=1)


def _ssd_specs(nc, rev):
    ci = (lambda i: nc - 1 - i) if rev else (lambda i: i)
    wide = pl.BlockSpec((BLK, SSD_INNER), lambda i: (ci(i), 0))
    bmat = pl.BlockSpec((BLK, SSD_BC), lambda i: (ci(i), SSD_INNER // SSD_BC))
    cmat = pl.BlockSpec((BLK, SSD_BC), lambda i: (ci(i), SSD_INNER // SSD_BC + 1))
    xbc = pl.BlockSpec((BLK, SSD_INNER + 2 * SSD_BC), lambda i: (ci(i), 0))
    col = pl.BlockSpec((BLK, SM_W), lambda i: (ci(i), 0))
    scal = pl.BlockSpec((SSD_H, 1, 1), lambda i: (0, 0, 0))
    nw = pl.BlockSpec((SSD_H, 1, SSD_P), lambda i: (0, 0, 0))
    st = pl.BlockSpec((SSD_H, 1, SSD_P, SSD_N), lambda i: (0, ci(i), 0, 0))
    return wide, bmat, cmat, xbc, col, scal, nw, st


def _ssd_fwd_call(xbc, z, small, dt_bias, a_log, d_skip, norm_w):
    seq = z.shape[0]
    nc = seq // BLK
    wide, bmat, cmat, _, col, scal, nw, st = _ssd_specs(nc, False)

    def body(x_ref, b_ref, c_ref, z_ref, sm_ref, db_ref, al_ref, ds_ref, nw_ref, y_ref, st_ref, h_scr):
        i = pl.program_id(0)

        @pl.when(i == 0)
        def _():
            h_scr[...] = jnp.zeros_like(h_scr)

        h = h_scr[...]
        st_ref[:, 0] = h
        y, h_new = _ssd_chunk(_split_lanes(x_ref[...], SSD_H, SSD_P), _split_lanes(z_ref[...], SSD_H, SSD_P),
                              _split_lanes(b_ref[...], SSD_G, SSD_N), _split_lanes(c_ref[...], SSD_G, SSD_N),
                              _pick_cols(sm_ref[...], SM_DT, SSD_H), h, db_ref[...], al_ref[...], ds_ref[...],
                              nw_ref[...], _valid_col(i * BLK, BLK))
        y_ref[...] = _join_lanes(y)
        h_scr[...] = h_new

    return pl.pallas_call(
        body, name="ssd_fwd", grid=(nc,),
        in_specs=[wide, bmat, cmat, wide, col, scal, scal, scal, nw],
        out_specs=[wide, st],
        out_shape=[jax.ShapeDtypeStruct((seq, SSD_INNER), F32),
                   jax.ShapeDtypeStruct((SSD_H, nc, SSD_P, SSD_N), F32)],
        scratch_shapes=[pltpu.VMEM((SSD_H, SSD_P, SSD_N), F32)],
        compiler_params=_cp(("arbitrary",)),
    )(xbc, xbc, xbc, z, small, dt_bias, a_log, d_skip, norm_w)


def _ssd_bwd_call(xbc, z, small, dt_bias, a_log, d_skip, norm_w, states, dy):
    seq = z.shape[0]
    nc = seq // BLK
    wide, bmat, cmat, xbc_spec, col, scal, nw, st = _ssd_specs(nc, True)

    def body(x_ref, b_ref, c_ref, z_ref, sm_ref, db_ref, al_ref, ds_ref, nw_ref, st_ref, dy_ref,
             dxbc_ref, dz_ref, dsm_ref, ddb_ref, dal_ref, dds_ref, dnw_ref, dh_scr):
        i = pl.program_id(0)

        @pl.when(i == 0)
        def _():
            dh_scr[...] = jnp.zeros_like(dh_scr)
            ddb_ref[...] = jnp.zeros_like(ddb_ref)
            dal_ref[...] = jnp.zeros_like(dal_ref)
            dds_ref[...] = jnp.zeros_like(dds_ref)
            dnw_ref[...] = jnp.zeros_like(dnw_ref)

        fn = functools.partial(_ssd_chunk, valid=_valid_col((nc - 1 - i) * BLK, BLK))
        _, vjp = jax.vjp(fn, _split_lanes(x_ref[...], SSD_H, SSD_P), _split_lanes(z_ref[...], SSD_H, SSD_P),
                         _split_lanes(b_ref[...], SSD_G, SSD_N), _split_lanes(c_ref[...], SSD_G, SSD_N),
                         _pick_cols(sm_ref[...], SM_DT, SSD_H), st_ref[:, 0], db_ref[...], al_ref[...], ds_ref[...],
                         nw_ref[...])
        dx, dz, dbm, dcm, ddt, dh, ddb, dal, dds, dnw = vjp((_split_lanes(dy_ref[...], SSD_H, SSD_P), dh_scr[...]))
        dxbc_ref[:, :SSD_INNER] = _join_lanes(dx)
        dxbc_ref[:, SSD_INNER:SSD_INNER + SSD_BC] = _join_lanes(dbm)
        dxbc_ref[:, SSD_INNER + SSD_BC:] = _join_lanes(dcm)
        dz_ref[...] = _join_lanes(dz)
        dsm_ref[...] = _spread_cols(ddt, SM_DT)
        dh_scr[...] = dh
        ddb_ref[...] += ddb
        dal_ref[...] += dal
        dds_ref[...] += dds
        dnw_ref[...] += dnw

    sshape = jax.ShapeDtypeStruct((SSD_H, 1, 1), F32)
    return pl.pallas_call(
        body, name="ssd_bwd", grid=(nc,),
        in_specs=[wide, bmat, cmat, wide, col, scal, scal, scal, nw, st, wide],
        out_specs=[xbc_spec, wide, col, scal, scal, scal, nw],
        out_shape=[jax.ShapeDtypeStruct(xbc.shape, F32), jax.ShapeDtypeStruct(z.shape, F32),
                   jax.ShapeDtypeStruct(small.shape, F32), sshape, sshape, sshape,
                   jax.ShapeDtypeStruct((SSD_H, 1, SSD_P), F32)],
        scratch_shapes=[pltpu.VMEM((SSD_H, SSD_P, SSD_N), F32)],
        compiler_params=_cp(("arbitrary",)),
    )(xbc, xbc, xbc, z, small, dt_bias, a_log, d_skip, norm_w, states, dy)


@jax.custom_vjp
def ssd_core(xbc, z, small, dt_bias, a_log, d_skip, norm_w):
    return _ssd_fwd_call(xbc, z, small, dt_bias, a_log, d_skip, norm_w)[0]


def _ssd_core_fwd(*args):
    y, states = _ssd_fwd_call(*args)
    return y, (*args, states)


def _ssd_core_bwd(res, dy):
    return tuple(_ssd_bwd_call(*res, dy))


ssd_core.defvjp(_ssd_core_fwd, _ssd_core_bwd)


def _swa_block(q, km, kp, kc, vm, vp, vc, sink, n):
    rows = SWA_REP * BLK
    qs = q.reshape(rows, SWA_D) * (SWA_D ** -0.5)
    s = _lo_nt(qs, jnp.concatenate([km, kp, kc], axis=0))
    i = jnp.bitwise_and(_iota2(rows, 3 * BLK, 0), BLK - 1)
    col = _iota2(rows, 3 * BLK, 1)
    j = jnp.bitwise_and(col, BLK - 1)
    part = jnp.right_shift(col, 7)
    ok_m = (part == 0) & (j >= NPAD) & ((n >= 1) | (j <= i))
    ok_p = (part == 1) & (n >= 2) & (j > i)
    ok_c = (part == 2) & (n >= 1) & (j <= i)
    ok = ok_m | ok_p | ok_c
    s = jnp.where(ok, s, NEG)
    snk = jnp.concatenate([jnp.broadcast_to(sink[r], (BLK, 1)) for r in range(SWA_REP)], axis=0)
    m = lax.stop_gradient(jnp.maximum(jnp.max(s, axis=-1, keepdims=True), snk))
    e = jnp.exp(s - m)
    p = e / (jnp.sum(e, axis=-1, keepdims=True) + jnp.exp(snk - m))
    o = _lo_nn(p, jnp.concatenate([vm, vp, vc], axis=0))
    return o.reshape(SWA_REP, BLK, SWA_D)


SWA_QW = SWA_QH * SWA_D
SWA_KW = SWA_KVH * SWA_D


def _swa_specs(nb, rev):
    ci = (lambda i: nb - 1 - i) if rev else (lambda i: i)
    qsp = pl.BlockSpec((BLK, SWA_QW), lambda i: (ci(i), 0))
    cur = pl.BlockSpec((BLK, 2 * SWA_KW), lambda i: (ci(i), 0))
    prev = pl.BlockSpec((BLK, 2 * SWA_KW), lambda i: (jnp.maximum(ci(i) - 1, 0), 0))
    meta = pl.BlockSpec((BLK, 2 * SWA_KW), lambda i: (0, 0))
    scal = pl.BlockSpec((SWA_QH, 1, 1), lambda i: (0, 0, 0))
    return qsp, cur, prev, meta, scal


def _swa_by_head(q, kvm, kvp, kvc, sink):
    def kv(t):
        return _split_lanes(t[:, :SWA_KW], SWA_KVH, SWA_D), _split_lanes(t[:, SWA_KW:], SWA_KVH, SWA_D)

    (km, vm), (kp, vp), (kc, vc) = kv(kvm), kv(kvp), kv(kvc)
    qh = _split_lanes(q, SWA_QH, SWA_D).reshape(SWA_KVH, SWA_REP, BLK, SWA_D)
    return qh, km, kp, kc, vm, vp, vc, sink.reshape(SWA_KVH, SWA_REP, 1, 1)


def _swa_kv_tile(dk, dv):
    return jnp.concatenate([_join_lanes(dk), _join_lanes(dv)], axis=1)


def _swa_fwd_call(q, kv, sink):
    seq = q.shape[0]
    nb = seq // BLK
    qsp, cur, prev, meta, scal = _swa_specs(nb, False)

    def body(q_ref, m_ref, p_ref, c_ref, s_ref, o_ref):
        fn = jax.vmap(functools.partial(_swa_block, n=pl.program_id(0)))
        o = fn(*_swa_by_head(q_ref[...], m_ref[...], p_ref[...], c_ref[...], s_ref[...]))
        o_ref[...] = _join_lanes(o.reshape(SWA_QH, BLK, SWA_D))

    return pl.pallas_call(
        body, name="swa_fwd", grid=(nb,),
        in_specs=[qsp, meta, prev, cur, scal],
        out_specs=qsp,
        out_shape=jax.ShapeDtypeStruct(q.shape, F32),
        compiler_params=_cp(("parallel",)),
    )(q, kv, kv, kv, sink)


def _swa_bwd_call(q, kv, sink, do):
    seq = q.shape[0]
    nb = seq // BLK
    qsp, cur, prev, meta, scal = _swa_specs(nb, True)

    def body(q_ref, m_ref, p_ref, c_ref, s_ref, do_ref, dq_ref, dkv_ref, ds_ref, prev_scr, meta_scr):
        i = pl.program_id(0)
        n = nb - 1 - i

        @pl.when(i == 0)
        def _():
            prev_scr[...] = jnp.zeros_like(prev_scr)
            meta_scr[...] = jnp.zeros_like(meta_scr)
            ds_ref[...] = jnp.zeros_like(ds_ref)

        fn = jax.vmap(functools.partial(_swa_block, n=n))
        _, vjp = jax.vjp(fn, *_swa_by_head(q_ref[...], m_ref[...], p_ref[...], c_ref[...], s_ref[...]))
        do = _split_lanes(do_ref[...], SWA_QH, SWA_D).reshape(SWA_KVH, SWA_REP, BLK, SWA_D)
        dq, dkm, dkp, dkc, dvm, dvp, dvc, dsk = vjp(do)
        dq_ref[...] = _join_lanes(dq.reshape(SWA_QH, BLK, SWA_D))
        ds_ref[...] += dsk.reshape(SWA_QH, 1, 1)
        meta_scr[...] += _swa_kv_tile(dkm, dvm)
        first = (n == 0).astype(F32)
        dkv_ref[...] = _swa_kv_tile(dkc, dvc) + prev_scr[...] + first * meta_scr[...]
        prev_scr[...] = _swa_kv_tile(dkp, dvp)

    return pl.pallas_call(
        body, name="swa_bwd", grid=(nb,),
        in_specs=[qsp, meta, prev, cur, scal, qsp],
        out_specs=[qsp, cur, scal],
        out_shape=[jax.ShapeDtypeStruct(q.shape, F32), jax.ShapeDtypeStruct(kv.shape, F32),
                   jax.ShapeDtypeStruct(sink.shape, F32)],
        scratch_shapes=[pltpu.VMEM((BLK, 2 * SWA_KW), F32)] * 2,
        compiler_params=_cp(("arbitrary",)),
    )(q, kv, kv, kv, sink, do)


@jax.custom_vjp
def swa_core(q, kv, sink):
    return _swa_fwd_call(q, kv, sink)


def _swa_core_fwd(q, kv, sink):
    return _swa_fwd_call(q, kv, sink), (q, kv, sink)


def _swa_core_bwd(res, do):
    return tuple(_swa_bwd_call(*res, do))


swa_core.defvjp(_swa_core_fwd, _swa_core_bwd)


def _tile(n, pref):
    if n <= pref:
        return n
    best = None
    for t in range(128, pref + 1, 128):
        if n % t == 0:
            best = t
    assert best is not None, (n, pref)
    return best


def _mm_tiles(m, n, kk):
    if kk > 8192:
        return _tile(m, 704), _tile(n, 512), _tile(kk, 4096)
    return _tile(m, 1408), _tile(n, 512), _tile(kk, 1408)


def _mm_call(a, b, name):
    (m, kk), n = a.shape, b.shape[1]
    tm, tn, tk = _mm_tiles(m, n, kk)
    nk = kk // tk
    a_spec = pl.BlockSpec((tm, tk), lambda i, j, k: (i, k))
    b_spec = pl.BlockSpec((tk, tn), lambda i, j, k: (k, j))

    def body(a_ref, b_ref, o_ref, acc_ref):
        k = pl.program_id(2)
        part = jnp.dot(a_ref[...].astype(BF16), b_ref[...].astype(BF16), preferred_element_type=F32)

        @pl.when(k == 0)
        def _():
            acc_ref[...] = part

        @pl.when(k > 0)
        def _():
            acc_ref[...] += part

        @pl.when(k == nk - 1)
        def _():
            o_ref[...] = acc_ref[...]

    return pl.pallas_call(
        body, name=name, grid=(m // tm, n // tn, nk),
        in_specs=[a_spec, b_spec],
        out_specs=pl.BlockSpec((tm, tn), lambda i, j, k: (i, j)),
        out_shape=jax.ShapeDtypeStruct((m, n), F32),
        scratch_shapes=[pltpu.VMEM((tm, tn), F32)],
        compiler_params=_cp(("parallel", "parallel", "arbitrary")),
    )(a, b)


@jax.custom_vjp
def mm(a, b, b_t, grad_slot):
    return _mm_call(a, b, "mm_fwd")


def _mm_fwd(a, b, b_t, grad_slot):
    return _mm_call(a, b, "mm_fwd"), (a, b, b_t)


def _mm_bwd(res, dc):
    a, b, b_t = res
    return (_mm_call(dc, b_t, "mm_dx"), jnp.zeros_like(b), jnp.zeros_like(b_t),
            _mm_call(a.astype(BF16).T, dc, "mm_dw"))


mm.defvjp(_mm_fwd, _mm_bwd)


def _row_specs(arrs, tr):
    return [pl.BlockSpec((tr, a.shape[1]), lambda i: (i, 0)) for a in arrs]


def _par_specs(arrs):
    return [pl.BlockSpec(a.shape, lambda i: (0, 0)) for a in arrs]


def _row_fwd_call(fn, rows, params, out_cols, tr, name):
    seq = rows[0].shape[0]
    nr = len(rows)

    def body(*refs):
        vals = [r[...] for r in refs[:-1]]
        refs[-1][...] = fn(*vals)

    return pl.pallas_call(
        body, name=name, grid=(seq // tr,),
        in_specs=_row_specs(rows, tr) + _par_specs(params),
        out_specs=pl.BlockSpec((tr, out_cols), lambda i: (i, 0)),
        out_shape=jax.ShapeDtypeStruct((seq, out_cols), F32),
        compiler_params=_cp(("parallel",)),
    )(*rows, *params)


def _row_bwd_call(fn, rows, params, dy, tr, name):
    seq = rows[0].shape[0]
    nr, npar = len(rows), len(params)

    def body(*refs):
        ins = refs[:nr + npar]
        dy_ref = refs[nr + npar]
        outs = refs[nr + npar + 1:]
        _, vjp = jax.vjp(fn, *[r[...] for r in ins])
        cts = vjp(dy_ref[...])
        for o_ref, ct in zip(outs[:nr], cts[:nr]):
            o_ref[...] = ct

        @pl.when(pl.program_id(0) == 0)
        def _():
            for o_ref in outs[nr:]:
                o_ref[...] = jnp.zeros_like(o_ref)

        for o_ref, ct in zip(outs[nr:], cts[nr:]):
            o_ref[...] += ct

    return pl.pallas_call(
        body, name=name, grid=(seq // tr,),
        in_specs=_row_specs(rows, tr) + _par_specs(params) + _row_specs([dy], tr),
        out_specs=_row_specs(rows, tr) + _par_specs(params),
        out_shape=[jax.ShapeDtypeStruct(a.shape, F32) for a in (*rows, *params)],
        compiler_params=_cp(("arbitrary",)),
    )(*rows, *params, dy)


def _make_rowop(fn, nrows, out_cols, tr, name):
    @jax.custom_vjp
    def op(*args):
        return _row_fwd_call(fn, args[:nrows], args[nrows:], out_cols, tr, name + "_fwd")

    def fwd(*args):
        return op(*args), args

    def bwd(args, dy):
        return tuple(_row_bwd_call(fn, args[:nrows], args[nrows:], dy, tr, name + "_bwd"))

    op.defvjp(fwd, bwd)
    return op


def _rms_fn(x, w):
    return x * lax.rsqrt(jnp.mean(x * x, axis=-1, keepdims=True) + RMS_EPS) * w


def _merge_fn(pa, pb, pc, gl):
    d = D_MODEL
    return (jax.nn.sigmoid(gl[:, :d]) * pa + jax.nn.sigmoid(gl[:, d:2 * d]) * pb
            + jax.nn.sigmoid(gl[:, 2 * d:]) * pc)


def _relu2_fn(a):
    r = jnp.maximum(a, 0.0)
    return r * r


rms_op = _make_rowop(_rms_fn, 1, D_MODEL, 384, "rms")
merge_op = _make_rowop(_merge_fn, 4, D_MODEL, 192, "merge")
relu2_op = _make_rowop(_relu2_fn, 1, D_FF, 192, "relu2")


def _conv_taps(xext, w, nrows):
    z = None
    for j in range(CONV_K):
        sh = CONV_K - 1 - j
        xs = pltpu.roll(xext, sh, 0) if sh else xext
        term = w[j:j + 1, :] * xs[8:8 + nrows, :]
        z = term if z is None else z + term
    return z


def _halo(ref, start, ok):
    return jnp.where(ok, ref[pl.ds(pl.multiple_of(start, 8), 8), :], 0.0)


def _conv_fwd_call(x, w, b):
    seq, ch = x.shape
    nb = seq // BLK

    def body(x_ref, w_ref, b_ref, o_ref):
        w = w_ref[...]
        bias = b_ref[...]

        def step(i, carry):
            r0 = pl.multiple_of(i * BLK, BLK)
            xext = jnp.concatenate([_halo(x_ref, jnp.maximum(r0 - 8, 0), i > 0), x_ref[pl.ds(r0, BLK), :]], axis=0)
            o_ref[pl.ds(r0, BLK), :] = _silu(_conv_taps(xext, w, BLK) + bias)
            return carry

        lax.fori_loop(0, nb, step, 0)

    strip = pl.BlockSpec((seq, 128), lambda c: (0, c))
    return pl.pallas_call(
        body, name="conv_fwd", grid=(ch // 128,),
        in_specs=[strip, pl.BlockSpec((CONV_K, 128), lambda c: (0, c)), pl.BlockSpec((1, 128), lambda c: (0, c))],
        out_specs=strip, out_shape=jax.ShapeDtypeStruct(x.shape, F32),
        compiler_params=_cp(("parallel",)),
    )(x, w, b)


def _conv_bwd_call(x, w, b, dy):
    seq, ch = x.shape
    nb = seq // BLK

    def body(x_ref, w_ref, b_ref, dy_ref, dx_ref, dw_ref, db_ref):
        w = w_ref[...]
        bias = b_ref[...]

        def step(i, carry):
            r0 = pl.multiple_of(i * BLK, BLK)
            last = i == nb - 1
            nxt = jnp.minimum(r0 + BLK, seq - 8)
            xext = jnp.concatenate([_halo(x_ref, jnp.maximum(r0 - 8, 0), i > 0), x_ref[pl.ds(r0, BLK), :],
                                    _halo(x_ref, nxt, jnp.logical_not(last))], axis=0)
            dyext = jnp.concatenate([dy_ref[pl.ds(r0, BLK), :], _halo(dy_ref, nxt, jnp.logical_not(last))], axis=0)
            z = _conv_taps(xext, w, BLK + 8) + bias
            sg = jax.nn.sigmoid(z)
            dz = dyext * (sg * (1.0 + z * (1.0 - sg)))
            dx = None
            for j in range(CONV_K):
                sh = CONV_K - 1 - j
                dzs = pltpu.roll(dz, BLK + 8 - sh, 0) if sh else dz
                term = w[j:j + 1, :] * dzs[:BLK, :]
                dx = term if dx is None else dx + term
            dx_ref[pl.ds(r0, BLK), :] = dx
            dzm = dz[:BLK, :]
            out = []
            for j in range(CONV_K):
                sh = CONV_K - 1 - j
                xs = pltpu.roll(xext, sh, 0) if sh else xext
                out.append(carry[j] + jnp.sum(dzm * xs[8:8 + BLK, :], axis=0, keepdims=True))
            out.append(carry[CONV_K] + jnp.sum(dzm, axis=0, keepdims=True))
            return tuple(out)

        zero = jnp.zeros((1, 128), F32)
        acc = lax.fori_loop(0, nb, step, (zero,) * (CONV_K + 1))
        dw_ref[...] = jnp.concatenate(acc[:CONV_K], axis=0)
        db_ref[...] = acc[CONV_K]

    strip = pl.BlockSpec((seq, 128), lambda c: (0, c))
    wsp = pl.BlockSpec((CONV_K, 128), lambda c: (0, c))
    bsp = pl.BlockSpec((1, 128), lambda c: (0, c))
    return pl.pallas_call(
        body, name="conv_bwd", grid=(ch // 128,),
        in_specs=[strip, wsp, bsp, strip],
        out_specs=[strip, wsp, bsp],
        out_shape=[jax.ShapeDtypeStruct(x.shape, F32), jax.ShapeDtypeStruct(w.shape, F32),
                   jax.ShapeDtypeStruct(b.shape, F32)],
        compiler_params=_cp(("parallel",)),
    )(x, w, b, dy)


@jax.custom_vjp
def conv_silu(x, w, b):
    return _conv_fwd_call(x, w, b)


def _conv_silu_fwd(x, w, b):
    return _conv_fwd_call(x, w, b), (x, w, b)


def _conv_silu_bwd(res, dy):
    return tuple(_conv_bwd_call(*res, dy))


conv_silu.defvjp(_conv_silu_fwd, _conv_silu_bwd)


def _loss_call(h, wf, target):
    seq, d = h.shape
    nb = seq // BLK

    def body(h_ref, w_ref, t_ref, loss_ref, dh_ref, dw_ref):
        i = pl.program_id(0)
        live = (i > 0).astype(F32)
        tgt = t_ref[...]

        def fn(hh, ww):
            err = _rms_fn(hh, ww) - tgt
            return 0.5 * live * jnp.sum(jnp.mean(err * err, axis=-1, keepdims=True), axis=0, keepdims=True)

        val, vjp = jax.vjp(fn, h_ref[...], w_ref[...])
        dh, dw = vjp(jnp.ones((1, 1), F32))
        dh_ref[...] = dh

        @pl.when(i == 0)
        def _():
            loss_ref[...] = jnp.zeros_like(loss_ref)
            dw_ref[...] = jnp.zeros_like(dw_ref)

        loss_ref[...] += val
        dw_ref[...] += dw

    return pl.pallas_call(
        body, name="loss_head", grid=(nb,),
        in_specs=[pl.BlockSpec((BLK, d), lambda i: (i, 0)), pl.BlockSpec((1, d), lambda i: (0, 0)),
                  pl.BlockSpec((BLK, d), lambda i: (jnp.maximum(i - 1, 0), 0))],
        out_specs=[pl.BlockSpec((1, 1), lambda i: (0, 0)), pl.BlockSpec((BLK, d), lambda i: (i, 0)),
                   pl.BlockSpec((1, d), lambda i: (0, 0))],
        out_shape=[jax.ShapeDtypeStruct((1, 1), F32), jax.ShapeDtypeStruct(h.shape, F32),
                   jax.ShapeDtypeStruct((1, d), F32)],
        compiler_params=_cp(("arbitrary",)),
    )(h, wf, target)


def _make_loss_head(target):
    @jax.custom_vjp
    def head(h, wf):
        return _loss_call(h, wf, target)[0][0, 0]

    def fwd(h, wf):
        loss, dh, dw = _loss_call(h, wf, target)
        return loss[0, 0], (dh, dw)

    def bwd(res, g):
        return g * res[0], g * res[1]

    head.defvjp(fwd, bwd)
    return head


_IN_SEGS = (("q", 0, 1024), ("k", 1024, 1024), ("v", 2048, 1024), ("gate", 3072, 1024), ("z", 4112, 1024),
            ("xbc", 5136, 2048), ("cq", 7200, 1024), ("ck", 8224, 256), ("cv", 8480, 256), ("gl", 8736, 3072),
            ("b", 4096, 8), ("a", 4104, 8), ("dt", 7184, 16))
_IN_PAD = 96
_SPLIT = (1024, 1024, 1024, 1024, 1024, 2048, 1024, 512, 3072, 128)


@jax.custom_vjp
def split_cols(u):
    offs = [sum(_SPLIT[:i]) for i in range(len(_SPLIT))]
    return tuple(u[:, o:o + s] for o, s in zip(offs, _SPLIT))


def _split_fwd(u):
    return split_cols(u), None


def _split_bwd(_, cts):
    return (jnp.concatenate(cts, axis=1),)


split_cols.defvjp(_split_fwd, _split_bwd)


def _layer(h, p, wb, slot):
    def proj(t, name):
        return mm(t, wb[name], wb[name + "_t"], slot[name])

    u = proj(rms_op(h, p["norm1_w"].reshape(1, -1)), "w_in")
    q_pre, k_pre, v_pre, gate, z, xbc_pre, cq, ckv, gl, small = split_cols(u)

    gcw = p["gdn_conv_w"]
    nob = jnp.zeros((1, GDN_H * GDN_D), F32)
    qa = conv_silu(q_pre, gcw[:, :1024], nob)
    ka = conv_silu(k_pre, gcw[:, 1024:2048], nob)
    va = conv_silu(v_pre, gcw[:, 2048:], nob)
    y_gdn = gdn_core(qa, ka, va, gate, small, p["gdn_a_log"].reshape(GDN_H, 1, 1),
                     p["gdn_dt_bias"].reshape(GDN_H, 1, 1), p["gdn_norm_w"].reshape(1, GDN_D))

    xbc = conv_silu(xbc_pre, p["ssd_conv_w"], p["ssd_conv_b"].reshape(1, -1))
    y_ssd = ssd_core(xbc, z, small, p["ssd_dt_bias"].reshape(SSD_H, 1, 1), p["ssd_a_log"].reshape(SSD_H, 1, 1),
                     p["ssd_d"].reshape(SSD_H, 1, 1), p["ssd_norm_w"].reshape(SSD_H, 1, SSD_P))

    y_swa = swa_core(cq, ckv, p["swa_sinks"].reshape(SWA_QH, 1, 1))

    merged = merge_op(proj(y_gdn, "w_proj_gdn"), proj(y_ssd, "w_proj_ssd"), proj(y_swa, "w_proj_swa"), gl)
    h = h + proj(merged, "w_out")
    a1 = proj(rms_op(h, p["norm2_w"].reshape(1, -1)), "w_up")
    return h + proj(relu2_op(a1), "w_down")


_MATMUL = ("w_in", "w_proj_gdn", "w_proj_ssd", "w_proj_swa", "w_out", "w_up", "w_down")
_PER_LAYER = ("norm1_w", "gdn_conv_w", "gdn_a_log", "gdn_dt_bias", "gdn_norm_w", "ssd_conv_w", "ssd_conv_b",
              "ssd_dt_bias", "ssd_a_log", "ssd_d", "ssd_norm_w", "swa_sinks", "norm2_w")


def _local_loss(x, params, slots, wb, loss_head):
    h = jnp.concatenate([jnp.zeros((NPAD, D_MODEL), F32), params["meta_tokens"], x], axis=0)
    for l in range(len(wb)):
        h = _layer(h, {n: params[n][l] for n in _PER_LAYER}, wb[l], slots[l])
    return loss_head(h, params["final_norm_w"].reshape(1, -1))


_IN_SHARD = 1476


def _in_pieces():
    out = []
    for _, s, n in _IN_SEGS:
        c = s
        while c < s + n:
            d = c // _IN_SHARD
            e = min(s + n, (d + 1) * _IN_SHARD)
            out.append((d, c - d * _IN_SHARD, e - d * _IN_SHARD))
            c = e
    return out


def _in_pieces_back():
    start, off = {}, 0
    for _, s, n in _IN_SEGS:
        start[s] = off
        off += n
    out = [[] for _ in range(N_DEV)]
    for _, s, n in sorted(_IN_SEGS, key=lambda t: t[1]):
        c = s
        while c < s + n:
            d = c // _IN_SHARD
            e = min(s + n, (d + 1) * _IN_SHARD)
            out[d].append((start[s] + c - s, start[s] + e - s))
            c = e
    return out


def _regroup_w_in(stacked):
    parts = [stacked[d, :, lo:hi] for d, lo, hi in _in_pieces()]
    return jnp.concatenate(parts + [jnp.zeros((D_MODEL, _IN_PAD), stacked.dtype)], axis=1)


def _ungroup_w_in(g):
    return [jnp.concatenate([g[:, lo:hi] for lo, hi in pieces], axis=1) for pieces in _in_pieces_back()]


def _position():
    return lax.axis_index("x"), lax.axis_index("y"), lax.axis_index("c")


_ANY = pl.BlockSpec(memory_space=pl.ANY)


def _chip_of(x, y, k):
    return (1 - x if k & 1 else x, 1 - y if k & 2 else y)


def _allgather_call(shards, name):
    n = len(shards)

    def body(*refs):
        x_refs, out_refs = refs[:n], refs[n:2 * n]
        send_sems, recv_sems, local_sems = refs[2 * n:]
        x, y, c = _position()
        me, sibling = (x, y, c), (x, y, 1 - c)
        chips = [_chip_of(x, y, k) for k in (1, 2, 3)]

        def slab(a, px, py, pc):
            return out_refs[a].at[4 * px + 2 * py + pc]

        def copy(a, k, block, to, src=None):
            return pltpu.make_async_remote_copy(
                src_ref=slab(a, *block) if src is None else src, dst_ref=slab(a, *block),
                send_sem=send_sems.at[7 * a + k], recv_sem=recv_sems.at[7 * a + k], device_id=to, device_id_type=MESH)

        mine = [pltpu.make_async_copy(x_refs[a], slab(a, *me), local_sems.at[a]) for a in range(n)]
        first = []
        for a in range(n):
            mine[a].start()
            first.append(copy(a, 0, me, sibling, src=x_refs[a]))
            first += [copy(a, 1 + j, me, (*chip, c), src=x_refs[a]) for j, chip in enumerate(chips)]
        for cp in first:
            cp.start()
        passed = []
        for j, chip in enumerate(chips):
            for a in range(n):
                copy(a, 1 + j, (*chip, c), me).wait_recv()
                passed.append(copy(a, 4 + j, (*chip, c), sibling))
                passed[-1].start()
        for a in range(n):
            copy(a, 0, sibling, me).wait_recv()
        for j, chip in enumerate(chips):
            for a in range(n):
                copy(a, 4 + j, (*chip, 1 - c), me).wait_recv()
        for cp in first + passed:
            cp.wait_send()
        for cp in mine:
            cp.wait()

    return pl.pallas_call(
        body, name=name,
        out_shape=[jax.ShapeDtypeStruct((N_DEV, *s.shape), s.dtype) for s in shards],
        in_specs=[_ANY] * n, out_specs=[_ANY] * n,
        scratch_shapes=[pltpu.SemaphoreType.DMA((7 * n,)), pltpu.SemaphoreType.DMA((7 * n,)),
                        pltpu.SemaphoreType.DMA((n,))],
    )(*shards)


def _sibling_exchange_call(for_c0, for_c1, name):
    n = len(for_c0)

    def body(*refs):
        c0_refs, c1_refs, out_refs = refs[:n], refs[n:2 * n], refs[2 * n:3 * n]
        send_sems, recv_sems = refs[3 * n:]
        x, y, c = _position()

        def copies(src_refs):
            return [pltpu.make_async_remote_copy(
                src_ref=src_refs[a].at[q], dst_ref=out_refs[a].at[q],
                send_sem=send_sems.at[4 * a + q], recv_sem=recv_sems.at[4 * a + q],
                device_id=(x, y, 1 - c), device_id_type=MESH) for a in range(n) for q in range(4)]

        @pl.when(c == 0)
        def _():
            for cp in copies(c1_refs):
                cp.start()

        @pl.when(c == 1)
        def _():
            for cp in copies(c0_refs):
                cp.start()

        waits = copies(c0_refs)
        for cp in waits:
            cp.wait_recv()
        for cp in waits:
            cp.wait_send()

    return pl.pallas_call(
        body, name=name,
        out_shape=[jax.ShapeDtypeStruct(g.shape, g.dtype) for g in for_c0],
        in_specs=[_ANY] * (2 * n), out_specs=[_ANY] * n,
        scratch_shapes=[pltpu.SemaphoreType.DMA((4 * n,)), pltpu.SemaphoreType.DMA((4 * n,))],
    )(*for_c0, *for_c1)


def _chip_exchange_call(partials, name):
    n = len(partials)

    def body(*refs):
        p_refs, out_refs = refs[:n], refs[n:2 * n]
        send_sems, recv_sems = refs[2 * n:]
        x, y, c = _position()
        copies = []
        for a in range(n):
            for k in (1, 2, 3):
                px, py = _chip_of(x, y, k)
                copies.append(pltpu.make_async_remote_copy(
                    src_ref=p_refs[a].at[2 * px + py], dst_ref=out_refs[a].at[k - 1],
                    send_sem=send_sems.at[3 * a + k - 1], recv_sem=recv_sems.at[3 * a + k - 1],
                    device_id=(px, py, c), device_id_type=MESH))
        for cp in copies:
            cp.start()
        for cp in copies:
            cp.wait_recv()
        for cp in copies:
            cp.wait_send()

    return pl.pallas_call(
        body, name=name,
        out_shape=[jax.ShapeDtypeStruct((3, *p.shape[1:]), p.dtype) for p in partials],
        in_specs=[_ANY] * n, out_specs=[_ANY] * n,
        scratch_shapes=[pltpu.SemaphoreType.DMA((3 * n,)), pltpu.SemaphoreType.DMA((3 * n,))],
    )(*partials)


def _chip_partial_call(for_c0, for_c1, sib, tr, name):
    _, r, c = sib.shape

    def body(c0_ref, c1_ref, s_ref, own_ref, out_ref):
        x, y, core = _position()
        mine = jnp.where(core == 0, c0_ref[...], c1_ref[...])
        partial = mine + s_ref[...]
        own = jnp.zeros((tr, c), F32)
        for q in range(4):
            own = jnp.where(2 * x + y == q, partial[q], own)
        own_ref[...] = own
        out_ref[...] = partial.astype(BF16)

    four = pl.BlockSpec((4, tr, c), lambda i: (0, i, 0))
    return pl.pallas_call(
        body, name=name, grid=(r // tr,),
        in_specs=[four, four, four],
        out_specs=[pl.BlockSpec((tr, c), lambda i: (i, 0)), four],
        out_shape=[jax.ShapeDtypeStruct((r, c), F32), jax.ShapeDtypeStruct((4, r, c), BF16)],
        compiler_params=_cp(("parallel",)),
    )(for_c0, for_c1, sib)


def _adamw_call(parts, w, m, v, tr, name):
    ns, r, c = w.shape
    counts = [len(p) for p in parts]
    flat_parts = [a for p in parts for a in p]

    def body(*refs):
        p_refs = refs[:len(flat_parts)]
        w_ref, m_ref, v_ref, g_ref, d_ref, nm_ref, nv_ref = refs[len(flat_parts):]
        at = 0
        for s in range(ns):
            g = None
            for p_ref in p_refs[at:at + counts[s]]:
                for j in range(p_ref.shape[0]):
                    term = p_ref[j].astype(F32)
                    g = term if g is None else g + term
            at += counts[s]
            nm = ADAM_B1 * m_ref[s] + (1.0 - ADAM_B1) * g
            nv = ADAM_B2 * v_ref[s] + (1.0 - ADAM_B2) * (g * g)
            m_hat = nm / (1.0 - ADAM_B1 ** ADAM_STEP)
            v_hat = nv / (1.0 - ADAM_B2 ** ADAM_STEP)
            g_ref[s] = g
            d_ref[s] = -ADAM_LR * (m_hat / (jnp.sqrt(v_hat) + ADAM_EPS) + ADAM_WD * w_ref[s])
            nm_ref[s] = nm
            nv_ref[s] = nv

    slabs = pl.BlockSpec((ns, tr, c), lambda i: (0, i, 0))
    return pl.pallas_call(
        body, name=name, grid=(r // tr,),
        in_specs=[pl.BlockSpec((a.shape[0], tr, c), lambda i: (0, i, 0)) for a in flat_parts] + [slabs] * 3,
        out_specs=[slabs] * 4,
        out_shape=[jax.ShapeDtypeStruct((ns, r, c), F32)] * 4,
        compiler_params=_cp(("parallel",)),
    )(*flat_parts, w, m, v)


_WEIGHTS = ("meta_tokens", "norm1_w", "w_in", "gdn_conv_w", "gdn_a_log", "gdn_dt_bias", "gdn_norm_w", "ssd_conv_w",
            "ssd_conv_b", "ssd_dt_bias", "ssd_a_log", "ssd_d", "ssd_norm_w", "swa_sinks", "w_proj_gdn", "w_proj_ssd",
            "w_proj_swa", "w_out", "norm2_w", "w_up", "w_down", "final_norm_w")
_SHARD_AXIS = {"meta_tokens": 1, "w_in": 2, "gdn_conv_w": 2, "ssd_conv_w": 2, "w_proj_gdn": 1, "w_proj_ssd": 1,
               "w_proj_swa": 1, "w_out": 1, "w_up": 2, "w_down": 1}
_BIG = tuple(n for n in _WEIGHTS if n in _SHARD_AXIS)
_SMALL = tuple(n for n in _WEIGHTS if n not in _SHARD_AXIS)
FLAT_C = 1024


def _pack(arrs, rows, lead=()):
    flat = jnp.concatenate([a.reshape(*lead, -1) for a in arrs], axis=-1)
    pad = rows * FLAT_C - flat.shape[-1]
    flat = jnp.pad(flat, [(0, 0)] * len(lead) + [(0, pad)])
    return flat.reshape(*lead, rows, FLAT_C)


def _unpack(flat, shapes, lead=()):
    flat = flat.reshape(*lead, -1)
    out, off = [], 0
    for s in shapes:
        n = math.prod(s)
        out.append(flat[..., off:off + n].reshape(*lead, *s))
        off += n
    return out


def _rows_for(shapes):
    n = sum(math.prod(s) for s in shapes)
    return -(-n // (FLAT_C * 8)) * 8


def _rows_tile(r, c):
    if r <= 256:
        return r
    return 128 if c > 1024 else 256


def _join(stacked, axis):
    moved = jnp.moveaxis(stacked, 0, axis)
    return moved.reshape(*moved.shape[:axis], -1, *moved.shape[axis + 2:])


def kernel(x, meta_tokens, norm1_w, w_in, gdn_conv_w, gdn_a_log, gdn_dt_bias, gdn_norm_w, ssd_conv_w, ssd_conv_b,
           ssd_dt_bias, ssd_a_log, ssd_d, ssd_norm_w, swa_sinks, w_proj_gdn, w_proj_ssd, w_proj_swa, w_out, norm2_w,
           w_up, w_down, final_norm_w, loss_target, m_meta_tokens, m_norm1_w, m_w_in, m_gdn_conv_w, m_gdn_a_log,
           m_gdn_dt_bias, m_gdn_norm_w, m_ssd_conv_w, m_ssd_conv_b, m_ssd_dt_bias, m_ssd_a_log, m_ssd_d, m_ssd_norm_w,
           m_swa_sinks, m_w_proj_gdn, m_w_proj_ssd, m_w_proj_swa, m_w_out, m_norm2_w, m_w_up, m_w_down,
           m_final_norm_w, v_meta_tokens, v_norm1_w, v_w_in, v_gdn_conv_w, v_gdn_a_log, v_gdn_dt_bias, v_gdn_norm_w,
           v_ssd_conv_w, v_ssd_conv_b, v_ssd_dt_bias, v_ssd_a_log, v_ssd_d, v_ssd_norm_w, v_swa_sinks, v_w_proj_gdn,
           v_w_proj_ssd, v_w_proj_swa, v_w_out, v_norm2_w, v_w_up, v_w_down, v_final_norm_w):
    args = (meta_tokens, norm1_w, w_in, gdn_conv_w, gdn_a_log, gdn_dt_bias, gdn_norm_w, ssd_conv_w, ssd_conv_b,
            ssd_dt_bias, ssd_a_log, ssd_d, ssd_norm_w, swa_sinks, w_proj_gdn, w_proj_ssd, w_proj_swa, w_out, norm2_w,
            w_up, w_down, final_norm_w, m_meta_tokens, m_norm1_w, m_w_in, m_gdn_conv_w, m_gdn_a_log,
            m_gdn_dt_bias, m_gdn_norm_w, m_ssd_conv_w, m_ssd_conv_b, m_ssd_dt_bias, m_ssd_a_log, m_ssd_d, m_ssd_norm_w,
            m_swa_sinks, m_w_proj_gdn, m_w_proj_ssd, m_w_proj_swa, m_w_out, m_norm2_w, m_w_up, m_w_down,
            m_final_norm_w, v_meta_tokens, v_norm1_w, v_w_in, v_gdn_conv_w, v_gdn_a_log, v_gdn_dt_bias, v_gdn_norm_w,
            v_ssd_conv_w, v_ssd_conv_b, v_ssd_dt_bias, v_ssd_a_log, v_ssd_d, v_ssd_norm_w, v_swa_sinks, v_w_proj_gdn,
            v_w_proj_ssd, v_w_proj_swa, v_w_out, v_norm2_w, v_w_up, v_w_down, v_final_norm_w)
    nw = len(_WEIGHTS)
    w = dict(zip(_WEIGHTS, args[:nw]))
    m = dict(zip(_WEIGHTS, args[nw:2 * nw]))
    v = dict(zip(_WEIGHTS, args[2 * nw:]))

    depth = w["w_in"].shape[0]
    small_shapes = [w[n].shape for n in _SMALL]
    small_rows = _rows_for(small_shapes)

    def flat2(t):
        return t.reshape(-1, t.shape[-1])

    tiny_names = [n for n in _BIG if n not in _MATMUL]
    units = [(n, l) for n in _MATMUL for l in range(depth)] + [(n, None) for n in tiny_names]
    sent = [w[n][l].astype(BF16) if l is not None else flat2(w[n]) for n, l in units]
    gathered = dict(zip(units, _allgather_call(sent, "gather_weights")))

    wb, slots = [], []
    for l in range(depth):
        full = {"w_in": _regroup_w_in(gathered["w_in", l]),
                "w_up": gathered["w_up", l].transpose(1, 0, 2).reshape(D_MODEL, D_FF),
                "w_down": gathered["w_down", l].reshape(D_FF, D_MODEL)}
        for n in ("w_proj_gdn", "w_proj_ssd", "w_proj_swa", "w_out"):
            full[n] = gathered[n, l].reshape(D_MODEL, D_MODEL)
        slots.append({n: jnp.zeros(t.shape, F32) for n, t in full.items()})
        full.update({n + "_t": t.T for n, t in list(full.items())})
        wb.append(full)

    tiny = {n: gathered[n, None].reshape(N_DEV, *w[n].shape) for n in tiny_names}
    small = {n: w[n] for n in _SMALL}
    loss_head = _make_loss_head(loss_target[0])

    def local_loss(x_rows, slots, tiny, small):
        joined = {n: _join(tiny[n], _SHARD_AXIS[n]) for n in tiny}
        return _local_loss(x_rows, {**joined, **small}, slots, wb, loss_head)

    loss, (gx, g_slots, g_tiny, g_small) = jax.value_and_grad(local_loss, argnums=(0, 1, 2, 3))(
        x[0], slots, tiny, small)
    loss = lax.psum(loss, ("x", "y", "c"))

    def by_core(name, g):
        if name == "w_in":
            shards = _ungroup_w_in(g)
            return jnp.stack(shards[0::2]), jnp.stack(shards[1::2])
        if name == "w_up":
            t = g.reshape(D_MODEL, 4, 2, D_FF // N_DEV)
            return t[:, :, 0].transpose(1, 0, 2), t[:, :, 1].transpose(1, 0, 2)
        t = g.reshape(4, 2, -1, g.shape[-1])
        return t[:, 0], t[:, 1]

    for_c0, for_c1 = [], []
    for n, l in units:
        a0, a1 = by_core(n, g_slots[l][n]) if l is not None else by_core(n, g_tiny[n].reshape(N_DEV, -1, w[n].shape[-1]))
        for_c0.append(a0)
        for_c1.append(a1)

    from_sibling = _sibling_exchange_call(for_c0, for_c1, "grads_to_sibling")
    own, outgoing = {}, []
    for u, a0, a1, s in zip(units, for_c0, for_c1, from_sibling):
        own[u], p = _chip_partial_call(a0, a1, s, _rows_tile(s.shape[1], s.shape[2]), "chip_partial_" + u[0])
        outgoing.append(p)
    incoming = dict(zip(units, _chip_exchange_call(outgoing, "grads_to_chips")))

    by_name = {}
    for n in _BIG:
        layers = list(range(depth)) if n in _MATMUL else [None]
        parts = [[own[n, l][None], incoming[n, l]] for l in layers]
        r, c = own[n, layers[0]].shape
        stacked = [d[n].reshape(len(layers), r, c) for d in (w, m, v)]
        res = _adamw_call(parts, *stacked, _rows_tile(r, c), "adamw_" + n)
        by_name[n] = [t.reshape(w[n].shape) for t in res]

    small_parts = _allgather_call([_pack([g_small[n] for n in _SMALL], small_rows)], "gather_small_grads")
    small_out = _adamw_call([small_parts], *[_pack([d[n] for n in _SMALL], small_rows)[None] for d in (w, m, v)],
                            small_rows, "adamw_replicated")
    for kind in range(4):
        for n, t in zip(_SMALL, _unpack(small_out[kind][0], small_shapes)):
            by_name.setdefault(n, [None] * 4)[kind] = t

    outs = [by_name[n][kind] for kind in range(4) for n in _WEIGHTS]
    return (loss, gx[None], *outs)
```

```python
import functools
import math

import jax
import jax.numpy as jnp
from jax import lax
from jax.experimental import pallas as pl
from jax.experimental.pallas import tpu as pltpu

F32 = jnp.float32
BF16 = jnp.bfloat16
HI = lax.Precision.HIGH
NEG = -1e30

D_MODEL = 1024
N_META = 16
BLK = 128
NPAD = BLK - N_META
RMS_EPS = 1e-6
L2_EPS = 1e-6
CONV_K = 4

GDN_H, GDN_D, GDN_C = 8, 128, 64
SSD_H, SSD_P, SSD_G, SSD_N = 16, 64, 4, 128
SSD_HPG = SSD_H // SSD_G
SWA_QH, SWA_KVH, SWA_D = 16, 4, 64
SWA_REP = SWA_QH // SWA_KVH
D_FF = 4 * D_MODEL

N_DEV = 8
MESH = pl.DeviceIdType.MESH

ADAM_LR, ADAM_B1, ADAM_B2, ADAM_EPS, ADAM_WD, ADAM_STEP = 0.001, 0.9, 0.999, 1e-08, 0.01, 10

VMEM_LIMIT = 56 * 1024 * 1024


def _cp(sem=None):
    return pltpu.CompilerParams(dimension_semantics=sem, vmem_limit_bytes=VMEM_LIMIT)


def _dot(a, b, ca, cb, prec=HI):
    return lax.dot_general(a, b, (((ca,), (cb,)), ((), ())), precision=prec, preferred_element_type=F32)


def _nn(a, b, prec=HI):
    return _dot(a, b, 1, 0, prec)


def _nt(a, b, prec=HI):
    return _dot(a, b, 1, 1, prec)


def _tn(a, b, prec=HI):
    return _dot(a, b, 0, 0, prec)


def _bdot(a, b, ca, cb):
    return lax.dot_general(a.astype(BF16), b.astype(BF16), (((ca,), (cb,)), ((), ())), preferred_element_type=F32)


@jax.custom_vjp
def _lo_nn(a, b):
    return _bdot(a, b, 1, 0)


_lo_nn.defvjp(lambda a, b: (_bdot(a, b, 1, 0), (a, b)),
              lambda r, d: (_bdot(d, r[1], 1, 1), _bdot(r[0], d, 0, 0)))


@jax.custom_vjp
def _lo_nt(a, b):
    return _bdot(a, b, 1, 1)


_lo_nt.defvjp(lambda a, b: (_bdot(a, b, 1, 1), (a, b)),
              lambda r, d: (_bdot(d, r[1], 1, 0), _bdot(d, r[0], 0, 0)))


@jax.custom_vjp
def _lo_tn(a, b):
    return _bdot(a, b, 0, 0)


_lo_tn.defvjp(lambda a, b: (_bdot(a, b, 0, 0), (a, b)),
              lambda r, d: (_bdot(r[1], d, 1, 1), _bdot(r[0], d, 1, 0)))


def _iota2(n, m, axis):
    return lax.broadcasted_iota(jnp.int32, (n, m), axis)


def _silu(x):
    return x * jax.nn.sigmoid(x)


def _softplus(x):
    return jnp.maximum(x, 0.0) + jnp.log(1.0 + jnp.exp(-jnp.abs(x)))


def _row_of(col):
    n = col.shape[0]
    return jnp.broadcast_to(col, (n, n)).T


def _cumsum_col(col):
    n = col.shape[0]
    tril = (_iota2(n, n, 0) >= _iota2(n, n, 1)).astype(F32)
    return _nn(tril, col)


def _tri_inv(a):
    n = a.shape[0]
    r, c = _iota2(n, n, 0), _iota2(n, n, 1)
    eye = (r == c).astype(F32)
    blk = jnp.right_shift(r, 4) == jnp.right_shift(c, 4)
    d = jnp.where(blk, a, 0.0)
    off = a - d
    d2 = _nn(d, d)
    d4 = _nn(d2, d2)
    d8 = _nn(d4, d4)
    td = _nn(_nn(_nn(eye - d, eye + d2), eye + d4), eye + d8)
    m = _nn(td, off)
    m2 = _nn(m, m)
    return _nn(_nn(eye - m, eye + m2), td)


@jax.custom_vjp
def _tri_solve(a, a_t, rhs):
    return _nn(_tri_inv(a), rhs)


def _tri_solve_fwd(a, a_t, rhs):
    sol = _nn(_tri_inv(a), rhs)
    return sol, (a_t, sol)


def _tri_solve_bwd(res, dsol):
    a_t, sol = res
    drhs = _nn(_tri_inv(a_t), dsol)
    return -_nt(drhs, sol), jnp.zeros_like(a_t), drhs


_tri_solve.defvjp(_tri_solve_fwd, _tri_solve_bwd)


def _gdn_chunk(qa, ka, va, gate, a_raw, b_raw, s, a_log, dt_bias, norm_w, valid):
    c = qa.shape[0]
    q = qa * lax.rsqrt(jnp.sum(qa * qa, axis=-1, keepdims=True) + L2_EPS) * (GDN_D ** -0.5)
    k = ka * lax.rsqrt(jnp.sum(ka * ka, axis=-1, keepdims=True) + L2_EPS)
    beta = jax.nn.sigmoid(b_raw)
    g = -jnp.exp(a_log) * _softplus(a_raw + dt_bias) * valid
    gam = _cumsum_col(g)
    gam_row = _row_of(gam)
    r, cc = _iota2(c, c, 0), _iota2(c, c, 1)
    decay = jnp.exp(jnp.where(r >= cc, gam - gam_row, NEG))
    kb = k * beta
    a = jnp.where(r > cc, _lo_nt(kb, k) * decay, 0.0)
    a_t = lax.stop_gradient(jnp.where(cc > r, _bdot(k, kb, 1, 1) * jnp.exp(jnp.where(cc >= r, gam_row - gam, NEG)), 0.0))
    egam = jnp.exp(gam)
    sol = _tri_solve(a, a_t, jnp.concatenate([va * beta, kb * egam], axis=1))
    u = sol[:, :GDN_D]
    w = sol[:, GDN_D:]
    attn = _lo_nt(q, k) * decay
    g_last = jnp.sum(g, axis=0, keepdims=True)
    k_tail = k * jnp.exp(g_last - gam)
    v_new = u - _lo_nn(w, s)
    o = _lo_nn(q * egam, s) + _lo_nn(attn, v_new)
    s_new = s * jnp.exp(g_last) + _lo_tn(k_tail, v_new)
    y = o * lax.rsqrt(jnp.mean(o * o, axis=-1, keepdims=True) + RMS_EPS) * norm_w * _silu(gate)
    return y, s_new


def _valid_col(row0, n):
    return (row0 + _iota2(n, 1, 0) >= NPAD).astype(F32)


GDN_HB = GDN_H

SM_B, SM_A, SM_DT, SM_W = 0, 8, 16, 128


def _pick_cols(sm, first, n):
    return jnp.stack([sm[:, first + j:first + j + 1] for j in range(n)])


def _spread_cols(cols, first):
    lane = _iota2(1, SM_W, 1)
    out = None
    for j in range(cols.shape[0]):
        term = cols[j] * (lane == first + j).astype(F32)
        out = term if out is None else out + term
    return out


def _gdn_specs(nc, rev):
    ci = (lambda i: nc - 1 - i) if rev else (lambda i: i)
    hb = GDN_HB
    tile = pl.BlockSpec((GDN_C, hb * GDN_D), lambda h, i: (ci(i), h))
    col = pl.BlockSpec((GDN_C, SM_W), lambda h, i: (ci(i), 0))
    scal = pl.BlockSpec((hb, 1, 1), lambda h, i: (h, 0, 0))
    nw = pl.BlockSpec((1, GDN_D), lambda h, i: (0, 0))
    st = pl.BlockSpec((hb, 1, GDN_D, GDN_D), lambda h, i: (h, ci(i), 0, 0))
    return tile, col, scal, nw, st


def _lanes(j):
    return slice(j * GDN_D, (j + 1) * GDN_D)


def _by_head(ref):
    return jnp.stack([ref[:, _lanes(j)] for j in range(GDN_HB)])


def _gdn_fwd_call(q, k, v, gate, small, a_log, dt_bias, norm_w, shards=()):
    seq = q.shape[0]
    nc = seq // GDN_C
    ns = len(shards)
    tile, col, scal, nw, st = _gdn_specs(nc, False)

    def body(*refs):
        q_ref, k_ref, v_ref, g_ref, sm_ref, al_ref, dt_ref, nw_ref = refs[:8]
        y_ref, st_ref = refs[8 + ns:10 + ns]
        s_scr = refs[10 + 2 * ns]
        i = pl.program_id(1)
        if ns:
            start, relay, finish = _gather_phases(refs[8:8 + ns], refs[10 + ns:10 + 2 * ns], *refs[11 + 2 * ns:])
            pl.when(i == 0)(start)
            pl.when(i == nc // 2)(relay)

        @pl.when(i == 0)
        def _():
            s_scr[...] = jnp.zeros_like(s_scr)

        s = s_scr[...]
        st_ref[:, 0] = s
        sm = sm_ref[...]
        fn = jax.vmap(functools.partial(_gdn_chunk, valid=_valid_col(i * GDN_C, GDN_C)))
        y, s_new = fn(_by_head(q_ref), _by_head(k_ref), _by_head(v_ref), _by_head(g_ref),
                      _pick_cols(sm, SM_A, GDN_H), _pick_cols(sm, SM_B, GDN_H), s,
                      al_ref[...], dt_ref[...], jnp.broadcast_to(nw_ref[...], (GDN_HB, 1, GDN_D)))
        for j in range(GDN_HB):
            y_ref[:, _lanes(j)] = y[j]
        s_scr[...] = s_new
        if ns:
            pl.when(i == nc - 1)(finish)

    return pl.pallas_call(
        body, name="gdn_fwd", grid=(GDN_H // GDN_HB, nc),
        in_specs=[tile, tile, tile, tile, col, scal, scal, nw] + [_ANY] * ns,
        out_specs=[tile, st] + [_ANY] * ns,
        out_shape=[jax.ShapeDtypeStruct((seq, GDN_H * GDN_D), F32),
                   jax.ShapeDtypeStruct((GDN_H, nc, GDN_D, GDN_D), F32)] + _gather_out_shapes(shards),
        scratch_shapes=[pltpu.VMEM((GDN_HB, GDN_D, GDN_D), F32)] + (_gather_sems(ns) if ns else []),
        compiler_params=_cp(("parallel", "arbitrary")),
    )(q, k, v, gate, small, a_log, dt_bias, norm_w, *shards)


def _gdn_bwd_call(q, k, v, gate, small, a_log, dt_bias, norm_w, states, dy, outgoing=()):
    seq = q.shape[0]
    nc = seq // GDN_C
    no = len(outgoing)
    tile, col, scal, nw, st = _gdn_specs(nc, True)
    nwh = pl.BlockSpec((GDN_HB, 1, GDN_D), lambda h, i: (h, 0, 0))

    def body(*refs):
        q_ref, k_ref, v_ref, g_ref, sm_ref, al_ref, dt_ref, nw_ref, st_ref, dy_ref = refs[:10]
        dq_ref, dk_ref, dv_ref, dg_ref, dsm_ref, dal_ref, ddt_ref, dnw_ref = refs[10 + no:18 + no]
        ds_scr = refs[18 + 2 * no]
        i = pl.program_id(1)
        if no:
            start, finish = _chip_exchange_phases(refs[10:10 + no], refs[18 + no:18 + 2 * no], *refs[19 + 2 * no:])
            pl.when(i == 0)(start)

        @pl.when(i == 0)
        def _():
            ds_scr[...] = jnp.zeros_like(ds_scr)
            dal_ref[...] = jnp.zeros_like(dal_ref)
            ddt_ref[...] = jnp.zeros_like(ddt_ref)
            dnw_ref[...] = jnp.zeros_like(dnw_ref)

        sm = sm_ref[...]
        fn = jax.vmap(functools.partial(_gdn_chunk, valid=_valid_col((nc - 1 - i) * GDN_C, GDN_C)))
        _, vjp = jax.vjp(fn, _by_head(q_ref), _by_head(k_ref), _by_head(v_ref), _by_head(g_ref),
                         _pick_cols(sm, SM_A, GDN_H), _pick_cols(sm, SM_B, GDN_H), st_ref[:, 0], al_ref[...],
                         dt_ref[...], jnp.broadcast_to(nw_ref[...], (GDN_HB, 1, GDN_D)))
        dq, dk, dv, dg, da, db, ds, dal, ddt, dnw = vjp((_by_head(dy_ref), ds_scr[...]))
        for j in range(GDN_HB):
            dq_ref[:, _lanes(j)] = dq[j]
            dk_ref[:, _lanes(j)] = dk[j]
            dv_ref[:, _lanes(j)] = dv[j]
            dg_ref[:, _lanes(j)] = dg[j]
        dsm_ref[...] = _spread_cols(da, SM_A) + _spread_cols(db, SM_B)
        ds_scr[...] = ds
        dal_ref[...] += dal
        ddt_ref[...] += ddt
        dnw_ref[...] += dnw
        if no:
            pl.when(i == nc - 1)(finish)

    big = jax.ShapeDtypeStruct((seq, GDN_H * GDN_D), F32)
    return pl.pallas_call(
        body, name="gdn_bwd", grid=(GDN_H // GDN_HB, nc),
        in_specs=[tile, tile, tile, tile, col, scal, scal, nw, st, tile] + [_ANY] * no,
        out_specs=[tile, tile, tile, tile, col, scal, scal, nwh] + [_ANY] * no,
        out_shape=[big, big, big, big, jax.ShapeDtypeStruct(small.shape, F32),
                   jax.ShapeDtypeStruct((GDN_H, 1, 1), F32), jax.ShapeDtypeStruct((GDN_H, 1, 1), F32),
                   jax.ShapeDtypeStruct((GDN_H, 1, GDN_D), F32)] + _chip_exchange_out_shapes(outgoing),
        scratch_shapes=[pltpu.VMEM((GDN_HB, GDN_D, GDN_D), F32)] + (_chip_exchange_sems(no) if no else []),
        compiler_params=_cp(("parallel", "arbitrary")),
    )(q, k, v, gate, small, a_log, dt_bias, norm_w, states, dy, *outgoing)


@jax.custom_vjp
def gdn_core(q, k, v, gate, small, a_log, dt_bias, norm_w, shards, slots):
    y, _, *gathered = _gdn_fwd_call(q, k, v, gate, small, a_log, dt_bias, norm_w, shards)
    return y, tuple(gathered), tuple(jnp.zeros((4, *s.shape[1:]), s.dtype) for s in slots)


def _gdn_core_fwd(q, k, v, gate, small, a_log, dt_bias, norm_w, shards, slots):
    y, states, *gathered = _gdn_fwd_call(q, k, v, gate, small, a_log, dt_bias, norm_w, shards)
    out = (y, tuple(gathered), tuple(jnp.zeros((4, *s.shape[1:]), s.dtype) for s in slots))
    return out, (q, k, v, gate, small, a_log, dt_bias, norm_w, states, shards)


def _gdn_core_bwd(res, cts):
    *args, shards = res
    dy, _, outgoing = cts
    dq, dk, dv, dg, dsm, dal, ddt, dnw, *incoming = _gdn_bwd_call(*args, dy, outgoing)
    return (dq, dk, dv, dg, dsm, dal, ddt, jnp.sum(dnw, axis=0), tuple(jnp.zeros_like(s) for s in shards),
            tuple(incoming))


gdn_core.defvjp(_gdn_core_fwd, _gdn_core_bwd)


def _ssd_head(x, z, dt_raw, h, dt_bias, a_log, d_skip, bm, cm, cb, valid):
    c = bm.shape[0]
    r, cc = _iota2(c, c, 0), _iota2(c, c, 1)
    dtp = _softplus(dt_raw + dt_bias)
    x = x * valid
    adt = -jnp.exp(a_log) * dtp * valid
    xdt = x * dtp
    acum = _cumsum_col(adt)
    lmat = jnp.exp(jnp.where(r >= cc, acum - _row_of(acum), NEG))
    a_last = jnp.sum(adt, axis=0, keepdims=True)
    y = _lo_nn(cb * lmat, xdt) + _lo_nt(cm * jnp.exp(acum), h) + d_skip * x
    h_new = h * jnp.exp(a_last) + _lo_tn(xdt * jnp.exp(a_last - acum), bm)
    return y * _silu(z), h_new


def _ssd_chunk(xs, z, bm, cm, dt_raw, h, dt_bias, a_log, d_skip, norm_w, valid):
    nh, c, p = xs.shape
    ng = bm.shape[0]
    hpg = nh // ng
    bm = bm * valid
    cm = cm * valid
    cb = jax.vmap(_lo_nt)(cm, bm)
    per_head = lambda t: jnp.repeat(t, hpg, axis=0)
    ys, hs = jax.vmap(functools.partial(_ssd_head, valid=valid))(
        xs, z, dt_raw, h, dt_bias, a_log, d_skip, per_head(bm), per_head(cm), per_head(cb))
    ss = jnp.sum(jnp.sum(ys * ys, axis=-1, keepdims=True).reshape(ng, hpg, c, 1), axis=1, keepdims=True)
    rstd = lax.rsqrt(ss / (hpg * p) + RMS_EPS)
    return (ys.reshape(ng, hpg, c, p) * rstd).reshape(nh, c, p) * norm_w, hs


SSD_INNER = SSD_H * SSD_P
SSD_BC = SSD_G * SSD_N


def _split_lanes(t, n, w):
    return jnp.stack([t[:, j * w:(j + 1) * w] for j in range(n)])


def _join_lanes(t):
    return jnp.concatenate([t[j] for j in range(t.shape[0])], axis=1)


def _ssd_specs(nc, rev):
    ci = (lambda i: nc - 1 - i) if rev else (lambda i: i)
    wide = pl.BlockSpec((BLK, SSD_INNER), lambda i: (ci(i), 0))
    bmat = pl.BlockSpec((BLK, SSD_BC), lambda i: (ci(i), SSD_INNER // SSD_BC))
    cmat = pl.BlockSpec((BLK, SSD_BC), lambda i: (ci(i), SSD_INNER // SSD_BC + 1))
    xbc = pl.BlockSpec((BLK, SSD_INNER + 2 * SSD_BC), lambda i: (ci(i), 0))
    col = pl.BlockSpec((BLK, SM_W), lambda i: (ci(i), 0))
    scal = pl.BlockSpec((SSD_H, 1, 1), lambda i: (0, 0, 0))
    nw = pl.BlockSpec((SSD_H, 1, SSD_P), lambda i: (0, 0, 0))
    st = pl.BlockSpec((SSD_H, 1, SSD_P, SSD_N), lambda i: (0, ci(i), 0, 0))
    return wide, bmat, cmat, xbc, col, scal, nw, st


def _ssd_fwd_call(xbc, z, small, dt_bias, a_log, d_skip, norm_w):
    seq = z.shape[0]
    nc = seq // BLK
    wide, bmat, cmat, _, col, scal, nw, st = _ssd_specs(nc, False)

    def body(x_ref, b_ref, c_ref, z_ref, sm_ref, db_ref, al_ref, ds_ref, nw_ref, y_ref, st_ref, h_scr):
        i = pl.program_id(0)

        @pl.when(i == 0)
        def _():
            h_scr[...] = jnp.zeros_like(h_scr)

        h = h_scr[...]
        st_ref[:, 0] = h
        y, h_new = _ssd_chunk(_split_lanes(x_ref[...], SSD_H, SSD_P), _split_lanes(z_ref[...], SSD_H, SSD_P),
                              _split_lanes(b_ref[...], SSD_G, SSD_N), _split_lanes(c_ref[...], SSD_G, SSD_N),
                              _pick_cols(sm_ref[...], SM_DT, SSD_H), h, db_ref[...], al_ref[...], ds_ref[...],
                              nw_ref[...], _valid_col(i * BLK, BLK))
        y_ref[...] = _join_lanes(y)
        h_scr[...] = h_new

    return pl.pallas_call(
        body, name="ssd_fwd", grid=(nc,),
        in_specs=[wide, bmat, cmat, wide, col, scal, scal, scal, nw],
        out_specs=[wide, st],
        out_shape=[jax.ShapeDtypeStruct((seq, SSD_INNER), F32),
                   jax.ShapeDtypeStruct((SSD_H, nc, SSD_P, SSD_N), F32)],
        scratch_shapes=[pltpu.VMEM((SSD_H, SSD_P, SSD_N), F32)],
        compiler_params=_cp(("arbitrary",)),
    )(xbc, xbc, xbc, z, small, dt_bias, a_log, d_skip, norm_w)


def _ssd_bwd_call(xbc, z, small, dt_bias, a_log, d_skip, norm_w, states, dy):
    seq = z.shape[0]
    nc = seq // BLK
    wide, bmat, cmat, xbc_spec, col, scal, nw, st = _ssd_specs(nc, True)

    def body(x_ref, b_ref, c_ref, z_ref, sm_ref, db_ref, al_ref, ds_ref, nw_ref, st_ref, dy_ref,
             dxbc_ref, dz_ref, dsm_ref, ddb_ref, dal_ref, dds_ref, dnw_ref, dh_scr):
        i = pl.program_id(0)

        @pl.when(i == 0)
        def _():
            dh_scr[...] = jnp.zeros_like(dh_scr)
            ddb_ref[...] = jnp.zeros_like(ddb_ref)
            dal_ref[...] = jnp.zeros_like(dal_ref)
            dds_ref[...] = jnp.zeros_like(dds_ref)
            dnw_ref[...] = jnp.zeros_like(dnw_ref)

        fn = functools.partial(_ssd_chunk, valid=_valid_col((nc - 1 - i) * BLK, BLK))
        _, vjp = jax.vjp(fn, _split_lanes(x_ref[...], SSD_H, SSD_P), _split_lanes(z_ref[...], SSD_H, SSD_P),
                         _split_lanes(b_ref[...], SSD_G, SSD_N), _split_lanes(c_ref[...], SSD_G, SSD_N),
                         _pick_cols(sm_ref[...], SM_DT, SSD_H), st_ref[:, 0], db_ref[...], al_ref[...], ds_ref[...],
                         nw_ref[...])
        dx, dz, dbm, dcm, ddt, dh, ddb, dal, dds, dnw = vjp((_split_lanes(dy_ref[...], SSD_H, SSD_P), dh_scr[...]))
        dxbc_ref[:, :SSD_INNER] = _join_lanes(dx)
        dxbc_ref[:, SSD_INNER:SSD_INNER + SSD_BC] = _join_lanes(dbm)
        dxbc_ref[:, SSD_INNER + SSD_BC:] = _join_lanes(dcm)
        dz_ref[...] = _join_lanes(dz)
        dsm_ref[...] = _spread_cols(ddt, SM_DT)
        dh_scr[...] = dh
        ddb_ref[...] += ddb
        dal_ref[...] += dal
        dds_ref[...] += dds
        dnw_ref[...] += dnw

    sshape = jax.ShapeDtypeStruct((SSD_H, 1, 1), F32)
    return pl.pallas_call(
        body, name="ssd_bwd", grid=(nc,),
        in_specs=[wide, bmat, cmat, wide, col, scal, scal, scal, nw, st, wide],
        out_specs=[xbc_spec, wide, col, scal, scal, scal, nw],
        out_shape=[jax.ShapeDtypeStruct(xbc.shape, F32), jax.ShapeDtypeStruct(z.shape, F32),
                   jax.ShapeDtypeStruct(small.shape, F32), sshape, sshape, sshape,
                   jax.ShapeDtypeStruct((SSD_H, 1, SSD_P), F32)],
        scratch_shapes=[pltpu.VMEM((SSD_H, SSD_P, SSD_N), F32)],
        compiler_params=_cp(("arbitrary",)),
    )(xbc, xbc, xbc, z, small, dt_bias, a_log, d_skip, norm_w, states, dy)


@jax.custom_vjp
def ssd_core(xbc, z, small, dt_bias, a_log, d_skip, norm_w):
    return _ssd_fwd_call(xbc, z, small, dt_bias, a_log, d_skip, norm_w)[0]


def _ssd_core_fwd(*args):
    y, states = _ssd_fwd_call(*args)
    return y, (*args, states)


def _ssd_core_bwd(res, dy):
    return tuple(_ssd_bwd_call(*res, dy))


ssd_core.defvjp(_ssd_core_fwd, _ssd_core_bwd)


def _swa_block(q, km, kp, kc, vm, vp, vc, sink, n):
    rows = SWA_REP * BLK
    qs = q.reshape(rows, SWA_D) * (SWA_D ** -0.5)
    s = _lo_nt(qs, jnp.concatenate([km, kp, kc], axis=0))
    i = jnp.bitwise_and(_iota2(rows, 3 * BLK, 0), BLK - 1)
    col = _iota2(rows, 3 * BLK, 1)
    j = jnp.bitwise_and(col, BLK - 1)
    part = jnp.right_shift(col, 7)
    ok_m = (part == 0) & (j >= NPAD) & ((n >= 1) | (j <= i))
    ok_p = (part == 1) & (n >= 2) & (j > i)
    ok_c = (part == 2) & (n >= 1) & (j <= i)
    ok = ok_m | ok_p | ok_c
    s = jnp.where(ok, s, NEG)
    snk = jnp.concatenate([jnp.broadcast_to(sink[r], (BLK, 1)) for r in range(SWA_REP)], axis=0)
    m = lax.stop_gradient(jnp.maximum(jnp.max(s, axis=-1, keepdims=True), snk))
    e = jnp.exp(s - m)
    p = e / (jnp.sum(e, axis=-1, keepdims=True) + jnp.exp(snk - m))
    o = _lo_nn(p, jnp.concatenate([vm, vp, vc], axis=0))
    return o.reshape(SWA_REP, BLK, SWA_D)


SWA_QW = SWA_QH * SWA_D
SWA_KW = SWA_KVH * SWA_D


def _swa_specs(nb, rev):
    ci = (lambda i: nb - 1 - i) if rev else (lambda i: i)
    qsp = pl.BlockSpec((BLK, SWA_QW), lambda i: (ci(i), 0))
    cur = pl.BlockSpec((BLK, 2 * SWA_KW), lambda i: (ci(i), 0))
    prev = pl.BlockSpec((BLK, 2 * SWA_KW), lambda i: (jnp.maximum(ci(i) - 1, 0), 0))
    meta = pl.BlockSpec((BLK, 2 * SWA_KW), lambda i: (0, 0))
    scal = pl.BlockSpec((SWA_QH, 1, 1), lambda i: (0, 0, 0))
    return qsp, cur, prev, meta, scal


def _swa_by_head(q, kvm, kvp, kvc, sink):
    def kv(t):
        return _split_lanes(t[:, :SWA_KW], SWA_KVH, SWA_D), _split_lanes(t[:, SWA_KW:], SWA_KVH, SWA_D)

    (km, vm), (kp, vp), (kc, vc) = kv(kvm), kv(kvp), kv(kvc)
    qh = _split_lanes(q, SWA_QH, SWA_D).reshape(SWA_KVH, SWA_REP, BLK, SWA_D)
    return qh, km, kp, kc, vm, vp, vc, sink.reshape(SWA_KVH, SWA_REP, 1, 1)


def _swa_kv_tile(dk, dv):
    return jnp.concatenate([_join_lanes(dk), _join_lanes(dv)], axis=1)


def _swa_fwd_call(q, kv, sink):
    seq = q.shape[0]
    nb = seq // BLK
    qsp, cur, prev, meta, scal = _swa_specs(nb, False)

    def body(q_ref, m_ref, p_ref, c_ref, s_ref, o_ref):
        fn = jax.vmap(functools.partial(_swa_block, n=pl.program_id(0)))
        o = fn(*_swa_by_head(q_ref[...], m_ref[...], p_ref[...], c_ref[...], s_ref[...]))
        o_ref[...] = _join_lanes(o.reshape(SWA_QH, BLK, SWA_D))

    return pl.pallas_call(
        body, name="swa_fwd", grid=(nb,),
        in_specs=[qsp, meta, prev, cur, scal],
        out_specs=qsp,
        out_shape=jax.ShapeDtypeStruct(q.shape, F32),
        compiler_params=_cp(("parallel",)),
    )(q, kv, kv, kv, sink)


def _swa_bwd_call(q, kv, sink, do):
    seq = q.shape[0]
    nb = seq // BLK
    qsp, cur, prev, meta, scal = _swa_specs(nb, True)

    def body(q_ref, m_ref, p_ref, c_ref, s_ref, do_ref, dq_ref, dkv_ref, ds_ref, prev_scr, meta_scr):
        i = pl.program_id(0)
        n = nb - 1 - i

        @pl.when(i == 0)
        def _():
            prev_scr[...] = jnp.zeros_like(prev_scr)
            meta_scr[...] = jnp.zeros_like(meta_scr)
            ds_ref[...] = jnp.zeros_like(ds_ref)

        fn = jax.vmap(functools.partial(_swa_block, n=n))
        _, vjp = jax.vjp(fn, *_swa_by_head(q_ref[...], m_ref[...], p_ref[...], c_ref[...], s_ref[...]))
        do = _split_lanes(do_ref[...], SWA_QH, SWA_D).reshape(SWA_KVH, SWA_REP, BLK, SWA_D)
        dq, dkm, dkp, dkc, dvm, dvp, dvc, dsk = vjp(do)
        dq_ref[...] = _join_lanes(dq.reshape(SWA_QH, BLK, SWA_D))
        ds_ref[...] += dsk.reshape(SWA_QH, 1, 1)
        meta_scr[...] += _swa_kv_tile(dkm, dvm)
        first = (n == 0).astype(F32)
        dkv_ref[...] = _swa_kv_tile(dkc, dvc) + prev_scr[...] + first * meta_scr[...]
        prev_scr[...] = _swa_kv_tile(dkp, dvp)

    return pl.pallas_call(
        body, name="swa_bwd", grid=(nb,),
        in_specs=[qsp, meta, prev, cur, scal, qsp],
        out_specs=[qsp, cur, scal],
        out_shape=[jax.ShapeDtypeStruct(q.shape, F32), jax.ShapeDtypeStruct(kv.shape, F32),
                   jax.ShapeDtypeStruct(sink.shape, F32)],
        scratch_shapes=[pltpu.VMEM((BLK, 2 * SWA_KW), F32)] * 2,
        compiler_params=_cp(("arbitrary",)),
    )(q, kv, kv, kv, sink, do)


@jax.custom_vjp
def swa_core(q, kv, sink):
    return _swa_fwd_call(q, kv, sink)


def _swa_core_fwd(q, kv, sink):
    return _swa_fwd_call(q, kv, sink), (q, kv, sink)


def _swa_core_bwd(res, do):
    return tuple(_swa_bwd_call(*res, do))


swa_core.defvjp(_swa_core_fwd, _swa_core_bwd)


def _tile(n, pref):
    if n <= pref:
        return n
    best = None
    for t in range(128, pref + 1, 128):
        if n % t == 0:
            best = t
    assert best is not None, (n, pref)
    return best


def _mm_tiles(m, n, kk):
    if kk > 8192:
        return _tile(m, 704), _tile(n, 512), _tile(kk, 4096)
    return _tile(m, 1408), _tile(n, 512), _tile(kk, 1408)


def _mm_call(a, b, name):
    (m, kk), n = a.shape, b.shape[1]
    tm, tn, tk = _mm_tiles(m, n, kk)
    nk = kk // tk
    a_spec = pl.BlockSpec((tm, tk), lambda i, j, k: (i, k))
    b_spec = pl.BlockSpec((tk, tn), lambda i, j, k: (k, j))

    def body(a_ref, b_ref, o_ref, acc_ref):
        k = pl.program_id(2)
        part = jnp.dot(a_ref[...].astype(BF16), b_ref[...].astype(BF16), preferred_element_type=F32)

        @pl.when(k == 0)
        def _():
            acc_ref[...] = part

        @pl.when(k > 0)
        def _():
            acc_ref[...] += part

        @pl.when(k == nk - 1)
        def _():
            o_ref[...] = acc_ref[...]

    return pl.pallas_call(
        body, name=name, grid=(m // tm, n // tn, nk),
        in_specs=[a_spec, b_spec],
        out_specs=pl.BlockSpec((tm, tn), lambda i, j, k: (i, j)),
        out_shape=jax.ShapeDtypeStruct((m, n), F32),
        scratch_shapes=[pltpu.VMEM((tm, tn), F32)],
        compiler_params=_cp(("parallel", "parallel", "arbitrary")),
    )(a, b)


@jax.custom_vjp
def mm(a, b, b_t, grad_slot):
    return _mm_call(a, b, "mm_fwd")


def _mm_fwd(a, b, b_t, grad_slot):
    return _mm_call(a, b, "mm_fwd"), (a, b, b_t)


def _mm_bwd(res, dc):
    a, b, b_t = res
    return (_mm_call(dc, b_t, "mm_dx"), jnp.zeros_like(b), jnp.zeros_like(b_t),
            _mm_call(a.astype(BF16).T, dc, "mm_dw"))


mm.defvjp(_mm_fwd, _mm_bwd)


def _row_specs(arrs, tr):
    return [pl.BlockSpec((tr, a.shape[1]), lambda i: (i, 0)) for a in arrs]


def _par_specs(arrs):
    return [pl.BlockSpec(a.shape, lambda i: (0, 0)) for a in arrs]


def _row_fwd_call(fn, rows, params, out_cols, tr, name):
    seq = rows[0].shape[0]
    nr = len(rows)

    def body(*refs):
        vals = [r[...] for r in refs[:-1]]
        refs[-1][...] = fn(*vals)

    return pl.pallas_call(
        body, name=name, grid=(seq // tr,),
        in_specs=_row_specs(rows, tr) + _par_specs(params),
        out_specs=pl.BlockSpec((tr, out_cols), lambda i: (i, 0)),
        out_shape=jax.ShapeDtypeStruct((seq, out_cols), F32),
        compiler_params=_cp(("parallel",)),
    )(*rows, *params)


def _row_bwd_call(fn, rows, params, dy, tr, name):
    seq = rows[0].shape[0]
    nr, npar = len(rows), len(params)

    def body(*refs):
        ins = refs[:nr + npar]
        dy_ref = refs[nr + npar]
        outs = refs[nr + npar + 1:]
        _, vjp = jax.vjp(fn, *[r[...] for r in ins])
        cts = vjp(dy_ref[...])
        for o_ref, ct in zip(outs[:nr], cts[:nr]):
            o_ref[...] = ct

        @pl.when(pl.program_id(0) == 0)
        def _():
            for o_ref in outs[nr:]:
                o_ref[...] = jnp.zeros_like(o_ref)

        for o_ref, ct in zip(outs[nr:], cts[nr:]):
            o_ref[...] += ct

    return pl.pallas_call(
        body, name=name, grid=(seq // tr,),
        in_specs=_row_specs(rows, tr) + _par_specs(params) + _row_specs([dy], tr),
        out_specs=_row_specs(rows, tr) + _par_specs(params),
        out_shape=[jax.ShapeDtypeStruct(a.shape, F32) for a in (*rows, *params)],
        compiler_params=_cp(("arbitrary",)),
    )(*rows, *params, dy)


def _make_rowop(fn, nrows, out_cols, tr, name):
    @jax.custom_vjp
    def op(*args):
        return _row_fwd_call(fn, args[:nrows], args[nrows:], out_cols, tr, name + "_fwd")

    def fwd(*args):
        return op(*args), args

    def bwd(args, dy):
        return tuple(_row_bwd_call(fn, args[:nrows], args[nrows:], dy, tr, name + "_bwd"))

    op.defvjp(fwd, bwd)
    return op


def _rms_fn(x, w):
    return x * lax.rsqrt(jnp.mean(x * x, axis=-1, keepdims=True) + RMS_EPS) * w


def _merge_fn(pa, pb, pc, gl):
    d = D_MODEL
    return (jax.nn.sigmoid(gl[:, :d]) * pa + jax.nn.sigmoid(gl[:, d:2 * d]) * pb
            + jax.nn.sigmoid(gl[:, 2 * d:]) * pc)


def _relu2_fn(a):
    r = jnp.maximum(a, 0.0)
    return r * r


rms_op = _make_rowop(_rms_fn, 1, D_MODEL, 384, "rms")
merge_op = _make_rowop(_merge_fn, 4, D_MODEL, 192, "merge")
relu2_op = _make_rowop(_relu2_fn, 1, D_FF, 192, "relu2")


def _conv_taps(xext, w, nrows):
    z = None
    for j in range(CONV_K):
        sh = CONV_K - 1 - j
        xs = pltpu.roll(xext, sh, 0) if sh else xext
        term = w[j:j + 1, :] * xs[8:8 + nrows, :]
        z = term if z is None else z + term
    return z


def _halo(ref, start, ok):
    return jnp.where(ok, ref[pl.ds(pl.multiple_of(start, 8), 8), :], 0.0)


def _conv_fwd_call(x, w, b):
    seq, ch = x.shape
    nb = seq // BLK

    def body(x_ref, w_ref, b_ref, o_ref):
        w = w_ref[...]
        bias = b_ref[...]

        def step(i, carry):
            r0 = pl.multiple_of(i * BLK, BLK)
            xext = jnp.concatenate([_halo(x_ref, jnp.maximum(r0 - 8, 0), i > 0), x_ref[pl.ds(r0, BLK), :]], axis=0)
            o_ref[pl.ds(r0, BLK), :] = _silu(_conv_taps(xext, w, BLK) + bias)
            return carry

        lax.fori_loop(0, nb, step, 0)

    strip = pl.BlockSpec((seq, 128), lambda c: (0, c))
    return pl.pallas_call(
        body, name="conv_fwd", grid=(ch // 128,),
        in_specs=[strip, pl.BlockSpec((CONV_K, 128), lambda c: (0, c)), pl.BlockSpec((1, 128), lambda c: (0, c))],
        out_specs=strip, out_shape=jax.ShapeDtypeStruct(x.shape, F32),
        compiler_params=_cp(("parallel",)),
    )(x, w, b)


def _conv_bwd_call(x, w, b, dy):
    seq, ch = x.shape
    nb = seq // BLK

    def body(x_ref, w_ref, b_ref, dy_ref, dx_ref, dw_ref, db_ref):
        w = w_ref[...]
        bias = b_ref[...]

        def step(i, carry):
            r0 = pl.multiple_of(i * BLK, BLK)
            last = i == nb - 1
            nxt = jnp.minimum(r0 + BLK, seq - 8)
            xext = jnp.concatenate([_halo(x_ref, jnp.maximum(r0 - 8, 0), i > 0), x_ref[pl.ds(r0, BLK), :],
                                    _halo(x_ref, nxt, jnp.logical_not(last))], axis=0)
            dyext = jnp.concatenate([dy_ref[pl.ds(r0, BLK), :], _halo(dy_ref, nxt, jnp.logical_not(last))], axis=0)
            z = _conv_taps(xext, w, BLK + 8) + bias
            sg = jax.nn.sigmoid(z)
            dz = dyext * (sg * (1.0 + z * (1.0 - sg)))
            dx = None
            for j in range(CONV_K):
                sh = CONV_K - 1 - j
                dzs = pltpu.roll(dz, BLK + 8 - sh, 0) if sh else dz
                term = w[j:j + 1, :] * dzs[:BLK, :]
                dx = term if dx is None else dx + term
            dx_ref[pl.ds(r0, BLK), :] = dx
            dzm = dz[:BLK, :]
            out = []
            for j in range(CONV_K):
                sh = CONV_K - 1 - j
                xs = pltpu.roll(xext, sh, 0) if sh else xext
                out.append(carry[j] + jnp.sum(dzm * xs[8:8 + BLK, :], axis=0, keepdims=True))
            out.append(carry[CONV_K] + jnp.sum(dzm, axis=0, keepdims=True))
            return tuple(out)

        zero = jnp.zeros((1, 128), F32)
        acc = lax.fori_loop(0, nb, step, (zero,) * (CONV_K + 1))
        dw_ref[...] = jnp.concatenate(acc[:CONV_K], axis=0)
        db_ref[...] = acc[CONV_K]

    strip = pl.BlockSpec((seq, 128), lambda c: (0, c))
    wsp = pl.BlockSpec((CONV_K, 128), lambda c: (0, c))
    bsp = pl.BlockSpec((1, 128), lambda c: (0, c))
    return pl.pallas_call(
        body, name="conv_bwd", grid=(ch // 128,),
        in_specs=[strip, wsp, bsp, strip],
        out_specs=[strip, wsp, bsp],
        out_shape=[jax.ShapeDtypeStruct(x.shape, F32), jax.ShapeDtypeStruct(w.shape, F32),
                   jax.ShapeDtypeStruct(b.shape, F32)],
        compiler_params=_cp(("parallel",)),
    )(x, w, b, dy)


@jax.custom_vjp
def conv_silu(x, w, b):
    return _conv_fwd_call(x, w, b)


def _conv_silu_fwd(x, w, b):
    return _conv_fwd_call(x, w, b), (x, w, b)


def _conv_silu_bwd(res, dy):
    return tuple(_conv_bwd_call(*res, dy))


conv_silu.defvjp(_conv_silu_fwd, _conv_silu_bwd)


def _loss_call(h, wf, target):
    seq, d = h.shape
    nb = seq // BLK

    def body(h_ref, w_ref, t_ref, loss_ref, dh_ref, dw_ref):
        i = pl.program_id(0)
        live = (i > 0).astype(F32)
        tgt = t_ref[...]

        def fn(hh, ww):
            err = _rms_fn(hh, ww) - tgt
            return 0.5 * live * jnp.sum(jnp.mean(err * err, axis=-1, keepdims=True), axis=0, keepdims=True)

        val, vjp = jax.vjp(fn, h_ref[...], w_ref[...])
        dh, dw = vjp(jnp.ones((1, 1), F32))
        dh_ref[...] = dh

        @pl.when(i == 0)
        def _():
            loss_ref[...] = jnp.zeros_like(loss_ref)
            dw_ref[...] = jnp.zeros_like(dw_ref)

        loss_ref[...] += val
        dw_ref[...] += dw

    return pl.pallas_call(
        body, name="loss_head", grid=(nb,),
        in_specs=[pl.BlockSpec((BLK, d), lambda i: (i, 0)), pl.BlockSpec((1, d), lambda i: (0, 0)),
                  pl.BlockSpec((BLK, d), lambda i: (jnp.maximum(i - 1, 0), 0))],
        out_specs=[pl.BlockSpec((1, 1), lambda i: (0, 0)), pl.BlockSpec((BLK, d), lambda i: (i, 0)),
                   pl.BlockSpec((1, d), lambda i: (0, 0))],
        out_shape=[jax.ShapeDtypeStruct((1, 1), F32), jax.ShapeDtypeStruct(h.shape, F32),
                   jax.ShapeDtypeStruct((1, d), F32)],
        compiler_params=_cp(("arbitrary",)),
    )(h, wf, target)


def _make_loss_head(target):
    @jax.custom_vjp
    def head(h, wf):
        return _loss_call(h, wf, target)[0][0, 0]

    def fwd(h, wf):
        loss, dh, dw = _loss_call(h, wf, target)
        return loss[0, 0], (dh, dw)

    def bwd(res, g):
        return g * res[0], g * res[1]

    head.defvjp(fwd, bwd)
    return head


_IN_SEGS = (("q", 0, 1024), ("k", 1024, 1024), ("v", 2048, 1024), ("gate", 3072, 1024), ("z", 4112, 1024),
            ("xbc", 5136, 2048), ("cq", 7200, 1024), ("ck", 8224, 256), ("cv", 8480, 256), ("gl", 8736, 3072),
            ("b", 4096, 8), ("a", 4104, 8), ("dt", 7184, 16))
_IN_PAD = 96 + 384
_SPLIT = (1024, 1024, 1024, 1024, 1024, 2048, 1024, 512, 3072, 128, 384)


@jax.custom_vjp
def split_cols(u):
    offs = [sum(_SPLIT[:i]) for i in range(len(_SPLIT))]
    return tuple(u[:, o:o + s] for o, s in zip(offs, _SPLIT))


def _split_fwd(u):
    return split_cols(u), None


def _split_bwd(_, cts):
    return (jnp.concatenate(cts, axis=1),)


split_cols.defvjp(_split_fwd, _split_bwd)


def _layer(h, p, wb, slot, shards=(), exchange_slots=()):
    def proj(t, name):
        return mm(t, wb[name], wb[name + "_t"], slot[name])

    u = proj(rms_op(h, p["norm1_w"].reshape(1, -1)), "w_in")
    q_pre, k_pre, v_pre, gate, z, xbc_pre, cq, ckv, gl, small, _ = split_cols(u)

    gcw = p["gdn_conv_w"]
    nob = jnp.zeros((1, GDN_H * GDN_D), F32)
    qa = conv_silu(q_pre, gcw[:, :1024], nob)
    ka = conv_silu(k_pre, gcw[:, 1024:2048], nob)
    va = conv_silu(v_pre, gcw[:, 2048:], nob)
    y_gdn, gathered, placeholders = gdn_core(
        qa, ka, va, gate, small, p["gdn_a_log"].reshape(GDN_H, 1, 1), p["gdn_dt_bias"].reshape(GDN_H, 1, 1),
        p["gdn_norm_w"].reshape(1, GDN_D), tuple(shards), tuple(exchange_slots))

    xbc = conv_silu(xbc_pre, p["ssd_conv_w"], p["ssd_conv_b"].reshape(1, -1))
    y_ssd = ssd_core(xbc, z, small, p["ssd_dt_bias"].reshape(SSD_H, 1, 1), p["ssd_a_log"].reshape(SSD_H, 1, 1),
                     p["ssd_d"].reshape(SSD_H, 1, 1), p["ssd_norm_w"].reshape(SSD_H, 1, SSD_P))

    y_swa = swa_core(cq, ckv, p["swa_sinks"].reshape(SWA_QH, 1, 1))

    merged = merge_op(proj(y_gdn, "w_proj_gdn"), proj(y_ssd, "w_proj_ssd"), proj(y_swa, "w_proj_swa"), gl)
    h = h + proj(merged, "w_out")
    a1 = proj(rms_op(h, p["norm2_w"].reshape(1, -1)), "w_up")
    return h + proj(relu2_op(a1), "w_down"), gathered, placeholders


_MATMUL = ("w_in", "w_proj_gdn", "w_proj_ssd", "w_proj_swa", "w_out", "w_up", "w_down")
_PER_LAYER = ("norm1_w", "gdn_conv_w", "gdn_a_log", "gdn_dt_bias", "gdn_norm_w", "ssd_conv_w", "ssd_conv_b",
              "ssd_dt_bias", "ssd_a_log", "ssd_d", "ssd_norm_w", "swa_sinks", "norm2_w")


def _embed(x, meta):
    return jnp.concatenate([jnp.zeros((NPAD, D_MODEL), F32), meta, x], axis=0)


_IN_SHARD = 1476


def _in_pieces():
    out = []
    for _, s, n in _IN_SEGS:
        c = s
        while c < s + n:
            d = c // _IN_SHARD
            e = min(s + n, (d + 1) * _IN_SHARD)
            out.append((d, c - d * _IN_SHARD, e - d * _IN_SHARD))
            c = e
    return out


def _in_pieces_back():
    start, off = {}, 0
    for _, s, n in _IN_SEGS:
        start[s] = off
        off += n
    out = [[] for _ in range(N_DEV)]
    for _, s, n in sorted(_IN_SEGS, key=lambda t: t[1]):
        c = s
        while c < s + n:
            d = c // _IN_SHARD
            e = min(s + n, (d + 1) * _IN_SHARD)
            out[d].append((start[s] + c - s, start[s] + e - s))
            c = e
    return out


def _regroup_w_in(stacked):
    parts = [stacked[d, :, lo:hi] for d, lo, hi in _in_pieces()]
    return jnp.concatenate(parts + [jnp.zeros((D_MODEL, _IN_PAD), stacked.dtype)], axis=1)


def _ungroup_w_in(g):
    return [jnp.concatenate([g[:, lo:hi] for lo, hi in pieces], axis=1) for pieces in _in_pieces_back()]


def _position():
    return lax.axis_index("x"), lax.axis_index("y"), lax.axis_index("c")


_ANY = pl.BlockSpec(memory_space=pl.ANY)


def _chip_of(x, y, k):
    return (1 - x if k & 1 else x, 1 - y if k & 2 else y)


def _allgather_call(shards, name):
    n = len(shards)

    def body(*refs):
        start, relay, finish = _gather_phases(refs[:n], refs[n:2 * n], *refs[2 * n:])
        start()
        relay()
        finish()

    return pl.pallas_call(
        body, name=name,
        out_shape=_gather_out_shapes(shards),
        in_specs=[_ANY] * n, out_specs=[_ANY] * n,
        scratch_shapes=_gather_sems(n),
    )(*shards)


def _gather_out_shapes(shards):
    return [jax.ShapeDtypeStruct((N_DEV, *s.shape), s.dtype) for s in shards]


def _gather_sems(n):
    return [pltpu.SemaphoreType.DMA((7 * n,)), pltpu.SemaphoreType.DMA((7 * n,)), pltpu.SemaphoreType.DMA((n,))]


def _gather_phases(x_refs, out_refs, send_sems, recv_sems, local_sems):
    n = len(x_refs)
    x, y, c = _position()
    me, sibling = (x, y, c), (x, y, 1 - c)
    chips = [_chip_of(x, y, k) for k in (1, 2, 3)]

    def slab(a, px, py, pc):
        return out_refs[a].at[4 * px + 2 * py + pc]

    def copy(a, k, block, to, src=None):
        return pltpu.make_async_remote_copy(
            src_ref=slab(a, *block) if src is None else src, dst_ref=slab(a, *block),
            send_sem=send_sems.at[7 * a + k], recv_sem=recv_sems.at[7 * a + k], device_id=to, device_id_type=MESH)

    def mine():
        return [pltpu.make_async_copy(x_refs[a], slab(a, *me), local_sems.at[a]) for a in range(n)]

    def first():
        out = []
        for a in range(n):
            out.append(copy(a, 0, me, sibling, src=x_refs[a]))
            out += [copy(a, 1 + j, me, (*chip, c), src=x_refs[a]) for j, chip in enumerate(chips)]
        return out

    def passed():
        return [copy(a, 4 + j, (*chip, c), sibling) for j, chip in enumerate(chips) for a in range(n)]

    def start():
        for cp in mine() + first():
            cp.start()

    def relay():
        for j, chip in enumerate(chips):
            for a in range(n):
                copy(a, 1 + j, (*chip, c), me).wait_recv()
                copy(a, 4 + j, (*chip, c), sibling).start()

    def finish():
        for a in range(n):
            copy(a, 0, sibling, me).wait_recv()
        for j, chip in enumerate(chips):
            for a in range(n):
                copy(a, 4 + j, (*chip, 1 - c), me).wait_recv()
        for cp in first() + passed():
            cp.wait_send()
        for cp in mine():
            cp.wait()

    return start, relay, finish


def _sibling_exchange_call(for_c0, for_c1, name):
    n = len(for_c0)

    def body(*refs):
        c0_refs, c1_refs, out_refs = refs[:n], refs[n:2 * n], refs[2 * n:3 * n]
        send_sems, recv_sems = refs[3 * n:]
        x, y, c = _position()

        def copies(src_refs):
            return [pltpu.make_async_remote_copy(
                src_ref=src_refs[a].at[q], dst_ref=out_refs[a].at[q],
                send_sem=send_sems.at[4 * a + q], recv_sem=recv_sems.at[4 * a + q],
                device_id=(x, y, 1 - c), device_id_type=MESH) for a in range(n) for q in range(4)]

        @pl.when(c == 0)
        def _():
            for cp in copies(c1_refs):
                cp.start()

        @pl.when(c == 1)
        def _():
            for cp in copies(c0_refs):
                cp.start()

        waits = copies(c0_refs)
        for cp in waits:
            cp.wait_recv()
        for cp in waits:
            cp.wait_send()

    return pl.pallas_call(
        body, name=name,
        out_shape=[jax.ShapeDtypeStruct(g.shape, g.dtype) for g in for_c0],
        in_specs=[_ANY] * (2 * n), out_specs=[_ANY] * n,
        scratch_shapes=[pltpu.SemaphoreType.DMA((4 * n,)), pltpu.SemaphoreType.DMA((4 * n,))],
    )(*for_c0, *for_c1)


def _chip_exchange_call(partials, name):
    n = len(partials)

    def body(*refs):
        start, finish = _chip_exchange_phases(refs[:n], refs[n:2 * n], *refs[2 * n:])
        start()
        finish()

    return pl.pallas_call(
        body, name=name,
        out_shape=_chip_exchange_out_shapes(partials),
        in_specs=[_ANY] * n, out_specs=[_ANY] * n,
        scratch_shapes=_chip_exchange_sems(n),
    )(*partials)


def _chip_exchange_out_shapes(partials):
    return [jax.ShapeDtypeStruct((3, *p.shape[1:]), p.dtype) for p in partials]


def _chip_exchange_sems(n):
    return [pltpu.SemaphoreType.DMA((3 * n,)), pltpu.SemaphoreType.DMA((3 * n,))]


def _chip_exchange_phases(p_refs, out_refs, send_sems, recv_sems):
    n = len(p_refs)
    x, y, c = _position()

    def copies():
        out = []
        for a in range(n):
            for k in (1, 2, 3):
                px, py = _chip_of(x, y, k)
                out.append(pltpu.make_async_remote_copy(
                    src_ref=p_refs[a].at[2 * px + py], dst_ref=out_refs[a].at[k - 1],
                    send_sem=send_sems.at[3 * a + k - 1], recv_sem=recv_sems.at[3 * a + k - 1],
                    device_id=(px, py, c), device_id_type=MESH))
        return out

    def start():
        for cp in copies():
            cp.start()

    def finish():
        for cp in copies():
            cp.wait_recv()
        for cp in copies():
            cp.wait_send()

    return start, finish


def _chip_partial_call(for_c0, for_c1, sib, tr, name):
    _, r, c = sib.shape

    def body(c0_ref, c1_ref, s_ref, own_ref, out_ref):
        x, y, core = _position()
        mine = jnp.where(core == 0, c0_ref[...], c1_ref[...])
        partial = mine + s_ref[...]
        own = jnp.zeros((tr, c), F32)
        for q in range(4):
            own = jnp.where(2 * x + y == q, partial[q], own)
        own_ref[...] = own
        out_ref[...] = partial.astype(BF16)

    four = pl.BlockSpec((4, tr, c), lambda i: (0, i, 0))
    return pl.pallas_call(
        body, name=name, grid=(r // tr,),
        in_specs=[four, four, four],
        out_specs=[pl.BlockSpec((tr, c), lambda i: (i, 0)), four],
        out_shape=[jax.ShapeDtypeStruct((r, c), F32), jax.ShapeDtypeStruct((4, r, c), BF16)],
        compiler_params=_cp(("parallel",)),
    )(for_c0, for_c1, sib)


def _adamw_call(parts, w, m, v, tr, name):
    ns, r, c = w.shape
    counts = [len(p) for p in parts]
    flat_parts = [a for p in parts for a in p]

    def body(*refs):
        p_refs = refs[:len(flat_parts)]
        w_ref, m_ref, v_ref, g_ref, d_ref, nm_ref, nv_ref = refs[len(flat_parts):]
        at = 0
        for s in range(ns):
            g = None
            for p_ref in p_refs[at:at + counts[s]]:
                for j in range(p_ref.shape[0]):
                    term = p_ref[j].astype(F32)
                    g = term if g is None else g + term
            at += counts[s]
            nm = ADAM_B1 * m_ref[s] + (1.0 - ADAM_B1) * g
            nv = ADAM_B2 * v_ref[s] + (1.0 - ADAM_B2) * (g * g)
            m_hat = nm / (1.0 - ADAM_B1 ** ADAM_STEP)
            v_hat = nv / (1.0 - ADAM_B2 ** ADAM_STEP)
            g_ref[s] = g
            d_ref[s] = -ADAM_LR * (m_hat / (jnp.sqrt(v_hat) + ADAM_EPS) + ADAM_WD * w_ref[s])
            nm_ref[s] = nm
            nv_ref[s] = nv

    slabs = pl.BlockSpec((ns, tr, c), lambda i: (0, i, 0))
    return pl.pallas_call(
        body, name=name, grid=(r // tr,),
        in_specs=[pl.BlockSpec((a.shape[0], tr, c), lambda i: (0, i, 0)) for a in flat_parts] + [slabs] * 3,
        out_specs=[slabs] * 4,
        out_shape=[jax.ShapeDtypeStruct((ns, r, c), F32)] * 4,
        compiler_params=_cp(("parallel",)),
    )(*flat_parts, w, m, v)


_WEIGHTS = ("meta_tokens", "norm1_w", "w_in", "gdn_conv_w", "gdn_a_log", "gdn_dt_bias", "gdn_norm_w", "ssd_conv_w",
            "ssd_conv_b", "ssd_dt_bias", "ssd_a_log", "ssd_d", "ssd_norm_w", "swa_sinks", "w_proj_gdn", "w_proj_ssd",
            "w_proj_swa", "w_out", "norm2_w", "w_up", "w_down", "final_norm_w")
_SHARD_AXIS = {"meta_tokens": 1, "w_in": 2, "gdn_conv_w": 2, "ssd_conv_w": 2, "w_proj_gdn": 1, "w_proj_ssd": 1,
               "w_proj_swa": 1, "w_out": 1, "w_up": 2, "w_down": 1}
_BIG = tuple(n for n in _WEIGHTS if n in _SHARD_AXIS)
_SMALL = tuple(n for n in _WEIGHTS if n not in _SHARD_AXIS)
FLAT_C = 1024


def _pack(arrs, rows, lead=()):
    flat = jnp.concatenate([a.reshape(*lead, -1) for a in arrs], axis=-1)
    pad = rows * FLAT_C - flat.shape[-1]
    flat = jnp.pad(flat, [(0, 0)] * len(lead) + [(0, pad)])
    return flat.reshape(*lead, rows, FLAT_C)


def _unpack(flat, shapes, lead=()):
    flat = flat.reshape(*lead, -1)
    out, off = [], 0
    for s in shapes:
        n = math.prod(s)
        out.append(flat[..., off:off + n].reshape(*lead, *s))
        off += n
    return out


def _rows_for(shapes):
    n = sum(math.prod(s) for s in shapes)
    return -(-n // (FLAT_C * 8)) * 8


def _rows_tile(r, c):
    if r <= 256:
        return r
    return 128 if c > 1024 else 256


def _join(stacked, axis):
    moved = jnp.moveaxis(stacked, 0, axis)
    return moved.reshape(*moved.shape[:axis], -1, *moved.shape[axis + 2:])


def _unjoin(full, axis):
    cut = full.reshape(*full.shape[:axis], N_DEV, full.shape[axis] // N_DEV, *full.shape[axis + 1:])
    return jnp.moveaxis(cut, axis, 0)


def kernel(x, meta_tokens, norm1_w, w_in, gdn_conv_w, gdn_a_log, gdn_dt_bias, gdn_norm_w, ssd_conv_w, ssd_conv_b,
           ssd_dt_bias, ssd_a_log, ssd_d, ssd_norm_w, swa_sinks, w_proj_gdn, w_proj_ssd, w_proj_swa, w_out, norm2_w,
           w_up, w_down, final_norm_w, loss_target, m_meta_tokens, m_norm1_w, m_w_in, m_gdn_conv_w, m_gdn_a_log,
           m_gdn_dt_bias, m_gdn_norm_w, m_ssd_conv_w, m_ssd_conv_b, m_ssd_dt_bias, m_ssd_a_log, m_ssd_d, m_ssd_norm_w,
           m_swa_sinks, m_w_proj_gdn, m_w_proj_ssd, m_w_proj_swa, m_w_out, m_norm2_w, m_w_up, m_w_down,
           m_final_norm_w, v_meta_tokens, v_norm1_w, v_w_in, v_gdn_conv_w, v_gdn_a_log, v_gdn_dt_bias, v_gdn_norm_w,
           v_ssd_conv_w, v_ssd_conv_b, v_ssd_dt_bias, v_ssd_a_log, v_ssd_d, v_ssd_norm_w, v_swa_sinks, v_w_proj_gdn,
           v_w_proj_ssd, v_w_proj_swa, v_w_out, v_norm2_w, v_w_up, v_w_down, v_final_norm_w):
    args = (meta_tokens, norm1_w, w_in, gdn_conv_w, gdn_a_log, gdn_dt_bias, gdn_norm_w, ssd_conv_w, ssd_conv_b,
            ssd_dt_bias, ssd_a_log, ssd_d, ssd_norm_w, swa_sinks, w_proj_gdn, w_proj_ssd, w_proj_swa, w_out, norm2_w,
            w_up, w_down, final_norm_w, m_meta_tokens, m_norm1_w, m_w_in, m_gdn_conv_w, m_gdn_a_log,
            m_gdn_dt_bias, m_gdn_norm_w, m_ssd_conv_w, m_ssd_conv_b, m_ssd_dt_bias, m_ssd_a_log, m_ssd_d, m_ssd_norm_w,
            m_swa_sinks, m_w_proj_gdn, m_w_proj_ssd, m_w_proj_swa, m_w_out, m_norm2_w, m_w_up, m_w_down,
            m_final_norm_w, v_meta_tokens, v_norm1_w, v_w_in, v_gdn_conv_w, v_gdn_a_log, v_gdn_dt_bias, v_gdn_norm_w,
            v_ssd_conv_w, v_ssd_conv_b, v_ssd_dt_bias, v_ssd_a_log, v_ssd_d, v_ssd_norm_w, v_swa_sinks, v_w_proj_gdn,
            v_w_proj_ssd, v_w_proj_swa, v_w_out, v_norm2_w, v_w_up, v_w_down, v_final_norm_w)
    nw = len(_WEIGHTS)
    w = dict(zip(_WEIGHTS, args[:nw]))
    m = dict(zip(_WEIGHTS, args[nw:2 * nw]))
    v = dict(zip(_WEIGHTS, args[2 * nw:]))

    depth = w["w_in"].shape[0]
    small_shapes = [w[n].shape for n in _SMALL]
    small_rows = _rows_for(small_shapes)

    def flat2(t):
        return t.reshape(-1, t.shape[-1])

    tiny_names = [n for n in _BIG if n not in _MATMUL]

    def layer_shards(l):
        return [w[n][l].astype(BF16) for n in _MATMUL]

    first = _allgather_call(layer_shards(0) + [flat2(w[n]) for n in tiny_names], "gather_weights")
    gathered = {(n, 0): t for n, t in zip(_MATMUL, first)}
    joined = {n: _join(t.reshape(N_DEV, *w[n].shape), _SHARD_AXIS[n]) for n, t in zip(tiny_names, first[len(_MATMUL):])}

    def layer_weights(l):
        full = {"w_in": _regroup_w_in(gathered["w_in", l]),
                "w_up": gathered["w_up", l].transpose(1, 0, 2).reshape(D_MODEL, D_FF),
                "w_down": gathered["w_down", l].reshape(D_FF, D_MODEL)}
        for n in ("w_proj_gdn", "w_proj_ssd", "w_proj_swa", "w_out"):
            full[n] = gathered[n, l].reshape(D_MODEL, D_MODEL)
        slot = {n: jnp.zeros(t.shape, F32) for n, t in full.items()}
        full.update({n + "_t": t.T for n, t in list(full.items())})
        return full, slot

    def layer_fn(l, wb, shards):
        if l == 0:
            def fn(x_rows, meta, p, slot, exchange_slots):
                out, g, placeholders = _layer(_embed(x_rows, meta), p, wb, slot, shards, exchange_slots)
                return (out, placeholders), g
        else:
            def fn(h_in, p, slot, exchange_slots):
                out, g, placeholders = _layer(h_in, p, wb, slot, shards, exchange_slots)
                return (out, placeholders), g
        return fn

    h, vjps = None, []
    for l in range(depth):
        wb, slot = layer_weights(l)
        p = {n: (joined[n][l] if n in joined else w[n][l]) for n in _PER_LAYER}
        more = l + 1 < depth
        shards = layer_shards(l + 1) if more else []
        exchange_slots = tuple(jnp.zeros((3, *s.shape), BF16) for s in shards)
        lead = (x[0], joined["meta_tokens"]) if l == 0 else (h,)
        (h, _), vjp, g_next = jax.vjp(layer_fn(l, wb, shards), *lead, p, slot, exchange_slots, has_aux=True)
        gathered.update({(n, l + 1): t for n, t in zip(_MATMUL, g_next)})
        vjps.append(vjp)
    loss, head_vjp = jax.vjp(_make_loss_head(loss_target[0]), h, w["final_norm_w"].reshape(1, -1))
    dh, d_final = head_vjp(jnp.ones((), F32))
    loss = lax.psum(loss, ("x", "y", "c"))

    def by_core(name, g):
        if name == "w_in":
            shards = _ungroup_w_in(g)
            return jnp.stack(shards[0::2]), jnp.stack(shards[1::2])
        if name == "w_up":
            t = g.reshape(D_MODEL, 4, 2, D_FF // N_DEV)
            return t[:, :, 0].transpose(1, 0, 2), t[:, :, 1].transpose(1, 0, 2)
        t = g.reshape(4, 2, -1, g.shape[-1])
        return t[:, 0], t[:, 1]

    own, incoming, layer_grads, outgoing = {}, {}, [None] * depth, ()
    for l in reversed(range(depth)):
        if l == 0:
            gx, d_meta, dp, dslot, arrived = vjps[0]((dh, tuple(outgoing)))
        else:
            dh, dp, dslot, arrived = vjps[l]((dh, tuple(outgoing)))
        incoming.update({(n, l + 1): t for n, t in zip(_MATMUL, arrived)})
        layer_grads[l] = dp
        todo = [((n, l), dslot[n]) for n in _MATMUL]
        if l == 0:
            full_grads = {"meta_tokens": d_meta}
            full_grads.update({n: jnp.stack([layer_grads[k][n] for k in range(depth)]) for n in tiny_names[1:]})
            todo += [((n, None), _unjoin(full_grads[n], _SHARD_AXIS[n]).reshape(N_DEV, -1, w[n].shape[-1]))
                     for n in tiny_names]
        pairs = [by_core(u[0], g) for u, g in todo]
        from_sibling = _sibling_exchange_call([a for a, _ in pairs], [b for _, b in pairs], "grads_to_sibling_%d" % l)
        outgoing = []
        for (u, _), (a0, a1), s in zip(todo, pairs, from_sibling):
            own[u], part = _chip_partial_call(a0, a1, s, _rows_tile(s.shape[1], s.shape[2]), "chip_partial_" + u[0])
            outgoing.append(part)
        if l == 0:
            incoming.update(zip([u for u, _ in todo], _chip_exchange_call(outgoing, "grads_to_chips")))

    g_small = {n: jnp.stack([layer_grads[k][n] for k in range(depth)]) for n in _SMALL if n != "final_norm_w"}
    g_small["final_norm_w"] = d_final.reshape(-1)

    by_name = {}
    for n in _BIG:
        layers = list(range(depth)) if n in _MATMUL else [None]
        parts = [[own[n, l][None], incoming[n, l]] for l in layers]
        r, c = own[n, layers[0]].shape
        stacked = [d[n].reshape(len(layers), r, c) for d in (w, m, v)]
        res = _adamw_call(parts, *stacked, _rows_tile(r, c), "adamw_" + n)
        by_name[n] = [t.reshape(w[n].shape) for t in res]

    small_parts = _allgather_call([_pack([g_small[n] for n in _SMALL], small_rows)], "gather_small_grads")
    small_out = _adamw_call([small_parts], *[_pack([d[n] for n in _SMALL], small_rows)[None] for d in (w, m, v)],
                            small_rows, "adamw_replicated")
    for kind in range(4):
        for n, t in zip(_SMALL, _unpack(small_out[kind][0], small_shapes)):
            by_name.setdefault(n, [None] * 4)[kind] = t

    outs = [by_name[n][kind] for kind in range(4) for n in _WEIGHTS]
    return (loss, gx[None], *outs)
```

```python
import functools
import math

import jax
import jax.numpy as jnp
from jax import lax
from jax.experimental import pallas as pl
from jax.experimental.pallas import tpu as pltpu

F32 = jnp.float32
BF16 = jnp.bfloat16
HI = lax.Precision.HIGH
NEG = -1e30

D_MODEL = 1024
N_META = 16
BLK = 128
NPAD = BLK - N_META
RMS_EPS = 1e-6
L2_EPS = 1e-6
CONV_K = 4

GDN_H, GDN_D, GDN_C = 8, 128, 64
SSD_H, SSD_P, SSD_G, SSD_N = 16, 64, 4, 128
SSD_HPG = SSD_H // SSD_G
SWA_QH, SWA_KVH, SWA_D = 16, 4, 64
SWA_REP = SWA_QH // SWA_KVH
D_FF = 4 * D_MODEL

N_DEV = 8
MESH = pl.DeviceIdType.MESH

ADAM_LR, ADAM_B1, ADAM_B2, ADAM_EPS, ADAM_WD, ADAM_STEP = 0.001, 0.9, 0.999, 1e-08, 0.01, 10

VMEM_LIMIT = 56 * 1024 * 1024


def _cp(sem=None):
    return pltpu.CompilerParams(dimension_semantics=sem, vmem_limit_bytes=VMEM_LIMIT)


def _dot(a, b, ca, cb, prec=HI):
    return lax.dot_general(a, b, (((ca,), (cb,)), ((), ())), precision=prec, preferred_element_type=F32)


def _nn(a, b, prec=HI):
    return _dot(a, b, 1, 0, prec)


def _nt(a, b, prec=HI):
    return _dot(a, b, 1, 1, prec)


def _tn(a, b, prec=HI):
    return _dot(a, b, 0, 0, prec)


def _bdot(a, b, ca, cb):
    return lax.dot_general(a.astype(BF16), b.astype(BF16), (((ca,), (cb,)), ((), ())), preferred_element_type=F32)


@jax.custom_vjp
def _lo_nn(a, b):
    return _bdot(a, b, 1, 0)


_lo_nn.defvjp(lambda a, b: (_bdot(a, b, 1, 0), (a, b)),
              lambda r, d: (_bdot(d, r[1], 1, 1), _bdot(r[0], d, 0, 0)))


@jax.custom_vjp
def _lo_nt(a, b):
    return _bdot(a, b, 1, 1)


_lo_nt.defvjp(lambda a, b: (_bdot(a, b, 1, 1), (a, b)),
              lambda r, d: (_bdot(d, r[1], 1, 0), _bdot(d, r[0], 0, 0)))


@jax.custom_vjp
def _lo_tn(a, b):
    return _bdot(a, b, 0, 0)


_lo_tn.defvjp(lambda a, b: (_bdot(a, b, 0, 0), (a, b)),
              lambda r, d: (_bdot(r[1], d, 1, 1), _bdot(r[0], d, 1, 0)))


def _iota2(n, m, axis):
    return lax.broadcasted_iota(jnp.int32, (n, m), axis)


def _silu(x):
    return x * jax.nn.sigmoid(x)


def _softplus(x):
    return jnp.maximum(x, 0.0) + jnp.log(1.0 + jnp.exp(-jnp.abs(x)))


def _row_of(col):
    n = col.shape[0]
    return jnp.broadcast_to(col, (n, n)).T


def _cumsum_col(col):
    n = col.shape[0]
    tril = (_iota2(n, n, 0) >= _iota2(n, n, 1)).astype(F32)
    return _nn(tril, col)


def _tri_inv(a):
    n = a.shape[0]
    r, c = _iota2(n, n, 0), _iota2(n, n, 1)
    eye = (r == c).astype(F32)
    blk = jnp.right_shift(r, 4) == jnp.right_shift(c, 4)
    d = jnp.where(blk, a, 0.0)
    off = a - d
    d2 = _nn(d, d)
    d4 = _nn(d2, d2)
    d8 = _nn(d4, d4)
    td = _nn(_nn(_nn(eye - d, eye + d2), eye + d4), eye + d8)
    m = _nn(td, off)
    m2 = _nn(m, m)
    return _nn(_nn(eye - m, eye + m2), td)


@jax.custom_vjp
def _tri_solve(a, a_t, rhs):
    return _nn(_tri_inv(a), rhs)


def _tri_solve_fwd(a, a_t, rhs):
    sol = _nn(_tri_inv(a), rhs)
    return sol, (a_t, sol)


def _tri_solve_bwd(res, dsol):
    a_t, sol = res
    drhs = _nn(_tri_inv(a_t), dsol)
    return -_nt(drhs, sol), jnp.zeros_like(a_t), drhs


_tri_solve.defvjp(_tri_solve_fwd, _tri_solve_bwd)


def _gdn_chunk(qa, ka, va, gate, a_raw, b_raw, s, a_log, dt_bias, norm_w, valid):
    c = qa.shape[0]
    q = qa * lax.rsqrt(jnp.sum(qa * qa, axis=-1, keepdims=True) + L2_EPS) * (GDN_D ** -0.5)
    k = ka * lax.rsqrt(jnp.sum(ka * ka, axis=-1, keepdims=True) + L2_EPS)
    beta = jax.nn.sigmoid(b_raw)
    g = -jnp.exp(a_log) * _softplus(a_raw + dt_bias) * valid
    gam = _cumsum_col(g)
    gam_row = _row_of(gam)
    r, cc = _iota2(c, c, 0), _iota2(c, c, 1)
    decay = jnp.exp(jnp.where(r >= cc, gam - gam_row, NEG))
    kb = k * beta
    a = jnp.where(r > cc, _lo_nt(kb, k) * decay, 0.0)
    a_t = lax.stop_gradient(jnp.where(cc > r, _bdot(k, kb, 1, 1) * jnp.exp(jnp.where(cc >= r, gam_row - gam, NEG)), 0.0))
    egam = jnp.exp(gam)
    sol = _tri_solve(a, a_t, jnp.concatenate([va * beta, kb * egam], axis=1))
    u = sol[:, :GDN_D]
    w = sol[:, GDN_D:]
    attn = _lo_nt(q, k) * decay
    g_last = jnp.sum(g, axis=0, keepdims=True)
    k_tail = k * jnp.exp(g_last - gam)
    v_new = u - _lo_nn(w, s)
    o = _lo_nn(q * egam, s) + _lo_nn(attn, v_new)
    s_new = s * jnp.exp(g_last) + _lo_tn(k_tail, v_new)
    y = o * lax.rsqrt(jnp.mean(o * o, axis=-1, keepdims=True) + RMS_EPS) * norm_w * _silu(gate)
    return y, s_new


def _valid_col(row0, n):
    return (row0 + _iota2(n, 1, 0) >= NPAD).astype(F32)


GDN_HB = GDN_H

SM_B, SM_A, SM_DT, SM_W = 0, 8, 16, 128


def _pick_cols(sm, first, n):
    return jnp.stack([sm[:, first + j:first + j + 1] for j in range(n)])


def _spread_cols(cols, first):
    lane = _iota2(1, SM_W, 1)
    out = None
    for j in range(cols.shape[0]):
        term = cols[j] * (lane == first + j).astype(F32)
        out = term if out is None else out + term
    return out


def _gdn_specs(nc, rev):
    ci = (lambda i: nc - 1 - i) if rev else (lambda i: i)
    hb = GDN_HB
    tile = pl.BlockSpec((GDN_C, hb * GDN_D), lambda h, i: (ci(i), h))
    col = pl.BlockSpec((GDN_C, SM_W), lambda h, i: (ci(i), 0))
    scal = pl.BlockSpec((hb, 1, 1), lambda h, i: (h, 0, 0))
    nw = pl.BlockSpec((1, GDN_D), lambda h, i: (0, 0))
    st = pl.BlockSpec((hb, 1, GDN_D, GDN_D), lambda h, i: (h, ci(i), 0, 0))
    return tile, col, scal, nw, st


def _lanes(j):
    return slice(j * GDN_D, (j + 1) * GDN_D)


def _by_head(ref):
    return jnp.stack([ref[:, _lanes(j)] for j in range(GDN_HB)])


def _gdn_fwd_call(q, k, v, gate, small, a_log, dt_bias, norm_w, shards=()):
    seq = q.shape[0]
    nc = seq // GDN_C
    ns = len(shards)
    tile, col, scal, nw, st = _gdn_specs(nc, False)

    def body(*refs):
        q_ref, k_ref, v_ref, g_ref, sm_ref, al_ref, dt_ref, nw_ref = refs[:8]
        y_ref, st_ref = refs[8 + ns:10 + ns]
        s_scr = refs[10 + 2 * ns]
        i = pl.program_id(1)
        if ns:
            start, relay, finish = _gather_phases(refs[8:8 + ns], refs[10 + ns:10 + 2 * ns], *refs[11 + 2 * ns:])
            pl.when(i == 0)(start)
            pl.when(i == nc - 1)(relay)

        @pl.when(i == 0)
        def _():
            s_scr[...] = jnp.zeros_like(s_scr)

        s = s_scr[...]
        st_ref[:, 0] = s
        sm = sm_ref[...]
        fn = jax.vmap(functools.partial(_gdn_chunk, valid=_valid_col(i * GDN_C, GDN_C)))
        y, s_new = fn(_by_head(q_ref), _by_head(k_ref), _by_head(v_ref), _by_head(g_ref),
                      _pick_cols(sm, SM_A, GDN_H), _pick_cols(sm, SM_B, GDN_H), s,
                      al_ref[...], dt_ref[...], jnp.broadcast_to(nw_ref[...], (GDN_HB, 1, GDN_D)))
        for j in range(GDN_HB):
            y_ref[:, _lanes(j)] = y[j]
        s_scr[...] = s_new
        if ns:
            pl.when(i == nc - 1)(finish)

    return pl.pallas_call(
        body, name="gdn_fwd", grid=(GDN_H // GDN_HB, nc),
        in_specs=[tile, tile, tile, tile, col, scal, scal, nw] + [_ANY] * ns,
        out_specs=[tile, st] + [_ANY] * ns,
        out_shape=[jax.ShapeDtypeStruct((seq, GDN_H * GDN_D), F32),
                   jax.ShapeDtypeStruct((GDN_H, nc, GDN_D, GDN_D), F32)] + _gather_out_shapes(shards),
        scratch_shapes=[pltpu.VMEM((GDN_HB, GDN_D, GDN_D), F32)] + (_gather_sems(ns) if ns else []),
        compiler_params=_cp(("parallel", "arbitrary")),
    )(q, k, v, gate, small, a_log, dt_bias, norm_w, *shards)


def _gdn_bwd_call(q, k, v, gate, small, a_log, dt_bias, norm_w, states, dy, outgoing=()):
    seq = q.shape[0]
    nc = seq // GDN_C
    no = len(outgoing)
    tile, col, scal, nw, st = _gdn_specs(nc, True)
    nwh = pl.BlockSpec((GDN_HB, 1, GDN_D), lambda h, i: (h, 0, 0))

    def body(*refs):
        q_ref, k_ref, v_ref, g_ref, sm_ref, al_ref, dt_ref, nw_ref, st_ref, dy_ref = refs[:10]
        dq_ref, dk_ref, dv_ref, dg_ref, dsm_ref, dal_ref, ddt_ref, dnw_ref = refs[10 + no:18 + no]
        ds_scr = refs[18 + 2 * no]
        i = pl.program_id(1)
        if no:
            start, finish = _chip_exchange_phases(refs[10:10 + no], refs[18 + no:18 + 2 * no], *refs[19 + 2 * no:])
            pl.when(i == 0)(start)

        @pl.when(i == 0)
        def _():
            ds_scr[...] = jnp.zeros_like(ds_scr)
            dal_ref[...] = jnp.zeros_like(dal_ref)
            ddt_ref[...] = jnp.zeros_like(ddt_ref)
            dnw_ref[...] = jnp.zeros_like(dnw_ref)

        sm = sm_ref[...]
        fn = jax.vmap(functools.partial(_gdn_chunk, valid=_valid_col((nc - 1 - i) * GDN_C, GDN_C)))
        _, vjp = jax.vjp(fn, _by_head(q_ref), _by_head(k_ref), _by_head(v_ref), _by_head(g_ref),
                         _pick_cols(sm, SM_A, GDN_H), _pick_cols(sm, SM_B, GDN_H), st_ref[:, 0], al_ref[...],
                         dt_ref[...], jnp.broadcast_to(nw_ref[...], (GDN_HB, 1, GDN_D)))
        dq, dk, dv, dg, da, db, ds, dal, ddt, dnw = vjp((_by_head(dy_ref), ds_scr[...]))
        for j in range(GDN_HB):
            dq_ref[:, _lanes(j)] = dq[j]
            dk_ref[:, _lanes(j)] = dk[j]
            dv_ref[:, _lanes(j)] = dv[j]
            dg_ref[:, _lanes(j)] = dg[j]
        dsm_ref[...] = _spread_cols(da, SM_A) + _spread_cols(db, SM_B)
        ds_scr[...] = ds
        dal_ref[...] += dal
        ddt_ref[...] += ddt
        dnw_ref[...] += dnw
        if no:
            pl.when(i == nc - 1)(finish)

    big = jax.ShapeDtypeStruct((seq, GDN_H * GDN_D), F32)
    return pl.pallas_call(
        body, name="gdn_bwd", grid=(GDN_H // GDN_HB, nc),
        in_specs=[tile, tile, tile, tile, col, scal, scal, nw, st, tile] + [_ANY] * no,
        out_specs=[tile, tile, tile, tile, col, scal, scal, nwh] + [_ANY] * no,
        out_shape=[big, big, big, big, jax.ShapeDtypeStruct(small.shape, F32),
                   jax.ShapeDtypeStruct((GDN_H, 1, 1), F32), jax.ShapeDtypeStruct((GDN_H, 1, 1), F32),
                   jax.ShapeDtypeStruct((GDN_H, 1, GDN_D), F32)] + _chip_exchange_out_shapes(outgoing),
        scratch_shapes=[pltpu.VMEM((GDN_HB, GDN_D, GDN_D), F32)] + (_chip_exchange_sems(no) if no else []),
        compiler_params=_cp(("parallel", "arbitrary")),
    )(q, k, v, gate, small, a_log, dt_bias, norm_w, states, dy, *outgoing)


@jax.custom_vjp
def gdn_core(q, k, v, gate, small, a_log, dt_bias, norm_w, shards, slots):
    y, _, *gathered = _gdn_fwd_call(q, k, v, gate, small, a_log, dt_bias, norm_w, shards)
    return y, tuple(gathered), tuple(jnp.zeros((4, *s.shape[1:]), s.dtype) for s in slots)


def _gdn_core_fwd(q, k, v, gate, small, a_log, dt_bias, norm_w, shards, slots):
    y, states, *gathered = _gdn_fwd_call(q, k, v, gate, small, a_log, dt_bias, norm_w, shards)
    out = (y, tuple(gathered), tuple(jnp.zeros((4, *s.shape[1:]), s.dtype) for s in slots))
    return out, (q, k, v, gate, small, a_log, dt_bias, norm_w, states, shards)


def _gdn_core_bwd(res, cts):
    *args, shards = res
    dy, _, outgoing = cts
    dq, dk, dv, dg, dsm, dal, ddt, dnw, *incoming = _gdn_bwd_call(*args, dy, outgoing)
    return (dq, dk, dv, dg, dsm, dal, ddt, jnp.sum(dnw, axis=0), tuple(jnp.zeros_like(s) for s in shards),
            tuple(incoming))


gdn_core.defvjp(_gdn_core_fwd, _gdn_core_bwd)


def _ssd_head(x, z, dt_raw, h, dt_bias, a_log, d_skip, bm, cm, cb, valid):
    c = bm.shape[0]
    r, cc = _iota2(c, c, 0), _iota2(c, c, 1)
    dtp = _softplus(dt_raw + dt_bias)
    x = x * valid
    adt = -jnp.exp(a_log) * dtp * valid
    xdt = x * dtp
    acum = _cumsum_col(adt)
    lmat = jnp.exp(jnp.where(r >= cc, acum - _row_of(acum), NEG))
    a_last = jnp.sum(adt, axis=0, keepdims=True)
    y = _lo_nn(cb * lmat, xdt) + _lo_nt(cm * jnp.exp(acum), h) + d_skip * x
    h_new = h * jnp.exp(a_last) + _lo_tn(xdt * jnp.exp(a_last - acum), bm)
    return y * _silu(z), h_new


def _ssd_chunk(xs, z, bm, cm, dt_raw, h, dt_bias, a_log, d_skip, norm_w, valid):
    nh, c, p = xs.shape
    ng = bm.shape[0]
    hpg = nh // ng
    bm = bm * valid
    cm = cm * valid
    cb = jax.vmap(_lo_nt)(cm, bm)
    per_head = lambda t: jnp.repeat(t, hpg, axis=0)
    ys, hs = jax.vmap(functools.partial(_ssd_head, valid=valid))(
        xs, z, dt_raw, h, dt_bias, a_log, d_skip, per_head(bm), per_head(cm), per_head(cb))
    ss = jnp.sum(jnp.sum(ys * ys, axis=-1, keepdims=True).reshape(ng, hpg, c, 1), axis=1, keepdims=True)
    rstd = lax.rsqrt(ss / (hpg * p) + RMS_EPS)
    return (ys.reshape(ng, hpg, c, p) * rstd).reshape(nh, c, p) * norm_w, hs


SSD_INNER = SSD_H * SSD_P
SSD_BC = SSD_G * SSD_N


def _split_lanes(t, n, w):
    return jnp.stack([t[:, j * w:(j + 1) * w] for j in range(n)])


def _join_lanes(t):
    return jnp.concatenate([t[j] for j in range(t.shape[0])], axis=1)


def _ssd_specs(nc, rev):
    ci = (lambda i: nc - 1 - i) if rev else (lambda i: i)
    wide = pl.BlockSpec((BLK, SSD_INNER), lambda i: (ci(i), 0))
    bmat = pl.BlockSpec((BLK, SSD_BC), lambda i: (ci(i), SSD_INNER // SSD_BC))
    cmat = pl.BlockSpec((BLK, SSD_BC), lambda i: (ci(i), SSD_INNER // SSD_BC + 1))
    xbc = pl.BlockSpec((BLK, SSD_INNER + 2 * SSD_BC), lambda i: (ci(i), 0))
    col = pl.BlockSpec((BLK, SM_W), lambda i: (ci(i), 0))
    scal = pl.BlockSpec((SSD_H, 1, 1), lambda i: (0, 0, 0))
    nw = pl.BlockSpec((SSD_H, 1, SSD_P), lambda i: (0, 0, 0))
    st = pl.BlockSpec((SSD_H, 1, SSD_P, SSD_N), lambda i: (0, ci(i), 0, 0))
    return wide, bmat, cmat, xbc, col, scal, nw, st


def _ssd_fwd_call(xbc, z, small, dt_bias, a_log, d_skip, norm_w, shards=()):
    seq = z.shape[0]
    nc = seq // BLK
    ns = len(shards)
    wide, bmat, cmat, _, col, scal, nw, st = _ssd_specs(nc, False)

    def body(*refs):
        x_ref, b_ref, c_ref, z_ref, sm_ref, db_ref, al_ref, ds_ref, nw_ref = refs[:9]
        y_ref, st_ref = refs[9 + ns:11 + ns]
        h_scr = refs[11 + 2 * ns]
        i = pl.program_id(0)
        if ns:
            start, relay, finish = _gather_phases(refs[9:9 + ns], refs[11 + ns:11 + 2 * ns], *refs[12 + 2 * ns:])
            pl.when(i == 0)(start)
            pl.when(i == nc - 1)(relay)

        @pl.when(i == 0)
        def _():
            h_scr[...] = jnp.zeros_like(h_scr)

        h = h_scr[...]
        st_ref[:, 0] = h
        y, h_new = _ssd_chunk(_split_lanes(x_ref[...], SSD_H, SSD_P), _split_lanes(z_ref[...], SSD_H, SSD_P),
                              _split_lanes(b_ref[...], SSD_G, SSD_N), _split_lanes(c_ref[...], SSD_G, SSD_N),
                              _pick_cols(sm_ref[...], SM_DT, SSD_H), h, db_ref[...], al_ref[...], ds_ref[...],
                              nw_ref[...], _valid_col(i * BLK, BLK))
        y_ref[...] = _join_lanes(y)
        h_scr[...] = h_new
        if ns:
            pl.when(i == nc - 1)(finish)

    return pl.pallas_call(
        body, name="ssd_fwd", grid=(nc,),
        in_specs=[wide, bmat, cmat, wide, col, scal, scal, scal, nw] + [_ANY] * ns,
        out_specs=[wide, st] + [_ANY] * ns,
        out_shape=[jax.ShapeDtypeStruct((seq, SSD_INNER), F32),
                   jax.ShapeDtypeStruct((SSD_H, nc, SSD_P, SSD_N), F32)] + _gather_out_shapes(shards),
        scratch_shapes=[pltpu.VMEM((SSD_H, SSD_P, SSD_N), F32)] + (_gather_sems(ns) if ns else []),
        compiler_params=_cp(("arbitrary",)),
    )(xbc, xbc, xbc, z, small, dt_bias, a_log, d_skip, norm_w, *shards)


def _ssd_bwd_call(xbc, z, small, dt_bias, a_log, d_skip, norm_w, states, dy):
    seq = z.shape[0]
    nc = seq // BLK
    wide, bmat, cmat, xbc_spec, col, scal, nw, st = _ssd_specs(nc, True)

    def body(x_ref, b_ref, c_ref, z_ref, sm_ref, db_ref, al_ref, ds_ref, nw_ref, st_ref, dy_ref,
             dxbc_ref, dz_ref, dsm_ref, ddb_ref, dal_ref, dds_ref, dnw_ref, dh_scr):
        i = pl.program_id(0)

        @pl.when(i == 0)
        def _():
            dh_scr[...] = jnp.zeros_like(dh_scr)
            ddb_ref[...] = jnp.zeros_like(ddb_ref)
            dal_ref[...] = jnp.zeros_like(dal_ref)
            dds_ref[...] = jnp.zeros_like(dds_ref)
            dnw_ref[...] = jnp.zeros_like(dnw_ref)

        fn = functools.partial(_ssd_chunk, valid=_valid_col((nc - 1 - i) * BLK, BLK))
        _, vjp = jax.vjp(fn, _split_lanes(x_ref[...], SSD_H, SSD_P), _split_lanes(z_ref[...], SSD_H, SSD_P),
                         _split_lanes(b_ref[...], SSD_G, SSD_N), _split_lanes(c_ref[...], SSD_G, SSD_N),
                         _pick_cols(sm_ref[...], SM_DT, SSD_H), st_ref[:, 0], db_ref[...], al_ref[...], ds_ref[...],
                         nw_ref[...])
        dx, dz, dbm, dcm, ddt, dh, ddb, dal, dds, dnw = vjp((_split_lanes(dy_ref[...], SSD_H, SSD_P), dh_scr[...]))
        dxbc_ref[:, :SSD_INNER] = _join_lanes(dx)
        dxbc_ref[:, SSD_INNER:SSD_INNER + SSD_BC] = _join_lanes(dbm)
        dxbc_ref[:, SSD_INNER + SSD_BC:] = _join_lanes(dcm)
        dz_ref[...] = _join_lanes(dz)
        dsm_ref[...] = _spread_cols(ddt, SM_DT)
        dh_scr[...] = dh
        ddb_ref[...] += ddb
        dal_ref[...] += dal
        dds_ref[...] += dds
        dnw_ref[...] += dnw

    sshape = jax.ShapeDtypeStruct((SSD_H, 1, 1), F32)
    return pl.pallas_call(
        body, name="ssd_bwd", grid=(nc,),
        in_specs=[wide, bmat, cmat, wide, col, scal, scal, scal, nw, st, wide],
        out_specs=[xbc_spec, wide, col, scal, scal, scal, nw],
        out_shape=[jax.ShapeDtypeStruct(xbc.shape, F32), jax.ShapeDtypeStruct(z.shape, F32),
                   jax.ShapeDtypeStruct(small.shape, F32), sshape, sshape, sshape,
                   jax.ShapeDtypeStruct((SSD_H, 1, SSD_P), F32)],
        scratch_shapes=[pltpu.VMEM((SSD_H, SSD_P, SSD_N), F32)],
        compiler_params=_cp(("arbitrary",)),
    )(xbc, xbc, xbc, z, small, dt_bias, a_log, d_skip, norm_w, states, dy)


@jax.custom_vjp
def ssd_core(xbc, z, small, dt_bias, a_log, d_skip, norm_w, shards):
    y, _, *gathered = _ssd_fwd_call(xbc, z, small, dt_bias, a_log, d_skip, norm_w, shards)
    return y, tuple(gathered)


def _ssd_core_fwd(*args):
    y, states, *gathered = _ssd_fwd_call(*args)
    return (y, tuple(gathered)), (*args[:-1], states, args[-1])


def _ssd_core_bwd(res, cts):
    *args, shards = res
    return (*_ssd_bwd_call(*args, cts[0]), tuple(jnp.zeros_like(s) for s in shards))


ssd_core.defvjp(_ssd_core_fwd, _ssd_core_bwd)


def _swa_block(q, km, kp, kc, vm, vp, vc, sink, n):
    rows = SWA_REP * BLK
    qs = q.reshape(rows, SWA_D) * (SWA_D ** -0.5)
    s = _lo_nt(qs, jnp.concatenate([km, kp, kc], axis=0))
    i = jnp.bitwise_and(_iota2(rows, 3 * BLK, 0), BLK - 1)
    col = _iota2(rows, 3 * BLK, 1)
    j = jnp.bitwise_and(col, BLK - 1)
    part = jnp.right_shift(col, 7)
    ok_m = (part == 0) & (j >= NPAD) & ((n >= 1) | (j <= i))
    ok_p = (part == 1) & (n >= 2) & (j > i)
    ok_c = (part == 2) & (n >= 1) & (j <= i)
    ok = ok_m | ok_p | ok_c
    s = jnp.where(ok, s, NEG)
    snk = jnp.concatenate([jnp.broadcast_to(sink[r], (BLK, 1)) for r in range(SWA_REP)], axis=0)
    m = lax.stop_gradient(jnp.maximum(jnp.max(s, axis=-1, keepdims=True), snk))
    e = jnp.exp(s - m)
    p = e / (jnp.sum(e, axis=-1, keepdims=True) + jnp.exp(snk - m))
    o = _lo_nn(p, jnp.concatenate([vm, vp, vc], axis=0))
    return o.reshape(SWA_REP, BLK, SWA_D)


SWA_QW = SWA_QH * SWA_D
SWA_KW = SWA_KVH * SWA_D


def _swa_specs(nb, rev):
    ci = (lambda i: nb - 1 - i) if rev else (lambda i: i)
    qsp = pl.BlockSpec((BLK, SWA_QW), lambda i: (ci(i), 0))
    cur = pl.BlockSpec((BLK, 2 * SWA_KW), lambda i: (ci(i), 0))
    prev = pl.BlockSpec((BLK, 2 * SWA_KW), lambda i: (jnp.maximum(ci(i) - 1, 0), 0))
    meta = pl.BlockSpec((BLK, 2 * SWA_KW), lambda i: (0, 0))
    scal = pl.BlockSpec((SWA_QH, 1, 1), lambda i: (0, 0, 0))
    return qsp, cur, prev, meta, scal


def _swa_by_head(q, kvm, kvp, kvc, sink):
    def kv(t):
        return _split_lanes(t[:, :SWA_KW], SWA_KVH, SWA_D), _split_lanes(t[:, SWA_KW:], SWA_KVH, SWA_D)

    (km, vm), (kp, vp), (kc, vc) = kv(kvm), kv(kvp), kv(kvc)
    qh = _split_lanes(q, SWA_QH, SWA_D).reshape(SWA_KVH, SWA_REP, BLK, SWA_D)
    return qh, km, kp, kc, vm, vp, vc, sink.reshape(SWA_KVH, SWA_REP, 1, 1)


def _swa_kv_tile(dk, dv):
    return jnp.concatenate([_join_lanes(dk), _join_lanes(dv)], axis=1)


def _swa_fwd_call(q, kv, sink):
    seq = q.shape[0]
    nb = seq // BLK
    qsp, cur, prev, meta, scal = _swa_specs(nb, False)

    def body(q_ref, m_ref, p_ref, c_ref, s_ref, o_ref):
        fn = jax.vmap(functools.partial(_swa_block, n=pl.program_id(0)))
        o = fn(*_swa_by_head(q_ref[...], m_ref[...], p_ref[...], c_ref[...], s_ref[...]))
        o_ref[...] = _join_lanes(o.reshape(SWA_QH, BLK, SWA_D))

    return pl.pallas_call(
        body, name="swa_fwd", grid=(nb,),
        in_specs=[qsp, meta, prev, cur, scal],
        out_specs=qsp,
        out_shape=jax.ShapeDtypeStruct(q.shape, F32),
        compiler_params=_cp(("parallel",)),
    )(q, kv, kv, kv, sink)


def _swa_bwd_call(q, kv, sink, do):
    seq = q.shape[0]
    nb = seq // BLK
    qsp, cur, prev, meta, scal = _swa_specs(nb, True)

    def body(q_ref, m_ref, p_ref, c_ref, s_ref, do_ref, dq_ref, dkv_ref, ds_ref, prev_scr, meta_scr):
        i = pl.program_id(0)
        n = nb - 1 - i

        @pl.when(i == 0)
        def _():
            prev_scr[...] = jnp.zeros_like(prev_scr)
            meta_scr[...] = jnp.zeros_like(meta_scr)
            ds_ref[...] = jnp.zeros_like(ds_ref)

        fn = jax.vmap(functools.partial(_swa_block, n=n))
        _, vjp = jax.vjp(fn, *_swa_by_head(q_ref[...], m_ref[...], p_ref[...], c_ref[...], s_ref[...]))
        do = _split_lanes(do_ref[...], SWA_QH, SWA_D).reshape(SWA_KVH, SWA_REP, BLK, SWA_D)
        dq, dkm, dkp, dkc, dvm, dvp, dvc, dsk = vjp(do)
        dq_ref[...] = _join_lanes(dq.reshape(SWA_QH, BLK, SWA_D))
        ds_ref[...] += dsk.reshape(SWA_QH, 1, 1)
        meta_scr[...] += _swa_kv_tile(dkm, dvm)
        first = (n == 0).astype(F32)
        dkv_ref[...] = _swa_kv_tile(dkc, dvc) + prev_scr[...] + first * meta_scr[...]
        prev_scr[...] = _swa_kv_tile(dkp, dvp)

    return pl.pallas_call(
        body, name="swa_bwd", grid=(nb,),
        in_specs=[qsp, meta, prev, cur, scal, qsp],
        out_specs=[qsp, cur, scal],
        out_shape=[jax.ShapeDtypeStruct(q.shape, F32), jax.ShapeDtypeStruct(kv.shape, F32),
                   jax.ShapeDtypeStruct(sink.shape, F32)],
        scratch_shapes=[pltpu.VMEM((BLK, 2 * SWA_KW), F32)] * 2,
        compiler_params=_cp(("arbitrary",)),
    )(q, kv, kv, kv, sink, do)


@jax.custom_vjp
def swa_core(q, kv, sink):
    return _swa_fwd_call(q, kv, sink)


def _swa_core_fwd(q, kv, sink):
    return _swa_fwd_call(q, kv, sink), (q, kv, sink)


def _swa_core_bwd(res, do):
    return tuple(_swa_bwd_call(*res, do))


swa_core.defvjp(_swa_core_fwd, _swa_core_bwd)


def _tile(n, pref):
    if n <= pref:
        return n
    best = None
    for t in range(128, pref + 1, 128):
        if n % t == 0:
            best = t
    assert best is not None, (n, pref)
    return best


def _mm_tiles(m, n, kk):
    if kk > 8192:
        return _tile(m, 704), _tile(n, 512), _tile(kk, 4096)
    return _tile(m, 1408), _tile(n, 512), _tile(kk, 1408)


def _mm_call(a, b, name):
    (m, kk), n = a.shape, b.shape[1]
    tm, tn, tk = _mm_tiles(m, n, kk)
    nk = kk // tk
    a_spec = pl.BlockSpec((tm, tk), lambda i, j, k: (i, k))
    b_spec = pl.BlockSpec((tk, tn), lambda i, j, k: (k, j))

    def body(a_ref, b_ref, o_ref, acc_ref):
        k = pl.program_id(2)
        part = jnp.dot(a_ref[...].astype(BF16), b_ref[...].astype(BF16), preferred_element_type=F32)

        @pl.when(k == 0)
        def _():
            acc_ref[...] = part

        @pl.when(k > 0)
        def _():
            acc_ref[...] += part

        @pl.when(k == nk - 1)
        def _():
            o_ref[...] = acc_ref[...]

    return pl.pallas_call(
        body, name=name, grid=(m // tm, n // tn, nk),
        in_specs=[a_spec, b_spec],
        out_specs=pl.BlockSpec((tm, tn), lambda i, j, k: (i, j)),
        out_shape=jax.ShapeDtypeStruct((m, n), F32),
        scratch_shapes=[pltpu.VMEM((tm, tn), F32)],
        compiler_params=_cp(("parallel", "parallel", "arbitrary")),
    )(a, b)


@jax.custom_vjp
def mm(a, b, b_t, grad_slot):
    return _mm_call(a, b, "mm_fwd")


def _mm_fwd(a, b, b_t, grad_slot):
    return _mm_call(a, b, "mm_fwd"), (a, b, b_t)


def _mm_bwd(res, dc):
    a, b, b_t = res
    return (_mm_call(dc, b_t, "mm_dx"), jnp.zeros_like(b), jnp.zeros_like(b_t),
            _mm_call(a.astype(BF16).T, dc, "mm_dw"))


mm.defvjp(_mm_fwd, _mm_bwd)


def _row_specs(arrs, tr):
    return [pl.BlockSpec((tr, a.shape[1]), lambda i: (i, 0)) for a in arrs]


def _par_specs(arrs):
    return [pl.BlockSpec(a.shape, lambda i: (0, 0)) for a in arrs]


def _row_fwd_call(fn, rows, params, out_cols, tr, name):
    seq = rows[0].shape[0]
    nr = len(rows)

    def body(*refs):
        vals = [r[...] for r in refs[:-1]]
        refs[-1][...] = fn(*vals)

    return pl.pallas_call(
        body, name=name, grid=(seq // tr,),
        in_specs=_row_specs(rows, tr) + _par_specs(params),
        out_specs=pl.BlockSpec((tr, out_cols), lambda i: (i, 0)),
        out_shape=jax.ShapeDtypeStruct((seq, out_cols), F32),
        compiler_params=_cp(("parallel",)),
    )(*rows, *params)


def _row_bwd_call(fn, rows, params, dy, tr, name):
    seq = rows[0].shape[0]
    nr, npar = len(rows), len(params)

    def body(*refs):
        ins = refs[:nr + npar]
        dy_ref = refs[nr + npar]
        outs = refs[nr + npar + 1:]
        _, vjp = jax.vjp(fn, *[r[...] for r in ins])
        cts = vjp(dy_ref[...])
        for o_ref, ct in zip(outs[:nr], cts[:nr]):
            o_ref[...] = ct

        @pl.when(pl.program_id(0) == 0)
        def _():
            for o_ref in outs[nr:]:
                o_ref[...] = jnp.zeros_like(o_ref)

        for o_ref, ct in zip(outs[nr:], cts[nr:]):
            o_ref[...] += ct

    return pl.pallas_call(
        body, name=name, grid=(seq // tr,),
        in_specs=_row_specs(rows, tr) + _par_specs(params) + _row_specs([dy], tr),
        out_specs=_row_specs(rows, tr) + _par_specs(params),
        out_shape=[jax.ShapeDtypeStruct(a.shape, F32) for a in (*rows, *params)],
        compiler_params=_cp(("arbitrary",)),
    )(*rows, *params, dy)


def _make_rowop(fn, nrows, out_cols, tr, name):
    @jax.custom_vjp
    def op(*args):
        return _row_fwd_call(fn, args[:nrows], args[nrows:], out_cols, tr, name + "_fwd")

    def fwd(*args):
        return op(*args), args

    def bwd(args, dy):
        return tuple(_row_bwd_call(fn, args[:nrows], args[nrows:], dy, tr, name + "_bwd"))

    op.defvjp(fwd, bwd)
    return op


def _rms_fn(x, w):
    return x * lax.rsqrt(jnp.mean(x * x, axis=-1, keepdims=True) + RMS_EPS) * w


def _merge_fn(pa, pb, pc, gl):
    d = D_MODEL
    return (jax.nn.sigmoid(gl[:, :d]) * pa + jax.nn.sigmoid(gl[:, d:2 * d]) * pb
            + jax.nn.sigmoid(gl[:, 2 * d:]) * pc)


def _relu2_fn(a):
    r = jnp.maximum(a, 0.0)
    return r * r


rms_op = _make_rowop(_rms_fn, 1, D_MODEL, 384, "rms")
merge_op = _make_rowop(_merge_fn, 4, D_MODEL, 192, "merge")
relu2_op = _make_rowop(_relu2_fn, 1, D_FF, 192, "relu2")


def _conv_taps(xext, w, nrows):
    z = None
    for j in range(CONV_K):
        sh = CONV_K - 1 - j
        xs = pltpu.roll(xext, sh, 0) if sh else xext
        term = w[j:j + 1, :] * xs[8:8 + nrows, :]
        z = term if z is None else z + term
    return z


def _halo(ref, start, ok):
    return jnp.where(ok, ref[pl.ds(pl.multiple_of(start, 8), 8), :], 0.0)


def _conv_fwd_call(x, w, b):
    seq, ch = x.shape
    nb = seq // BLK

    def body(x_ref, w_ref, b_ref, o_ref):
        w = w_ref[...]
        bias = b_ref[...]

        def step(i, carry):
            r0 = pl.multiple_of(i * BLK, BLK)
            xext = jnp.concatenate([_halo(x_ref, jnp.maximum(r0 - 8, 0), i > 0), x_ref[pl.ds(r0, BLK), :]], axis=0)
            o_ref[pl.ds(r0, BLK), :] = _silu(_conv_taps(xext, w, BLK) + bias)
            return carry

        lax.fori_loop(0, nb, step, 0)

    strip = pl.BlockSpec((seq, 128), lambda c: (0, c))
    return pl.pallas_call(
        body, name="conv_fwd", grid=(ch // 128,),
        in_specs=[strip, pl.BlockSpec((CONV_K, 128), lambda c: (0, c)), pl.BlockSpec((1, 128), lambda c: (0, c))],
        out_specs=strip, out_shape=jax.ShapeDtypeStruct(x.shape, F32),
        compiler_params=_cp(("parallel",)),
    )(x, w, b)


def _conv_bwd_call(x, w, b, dy):
    seq, ch = x.shape
    nb = seq // BLK

    def body(x_ref, w_ref, b_ref, dy_ref, dx_ref, dw_ref, db_ref):
        w = w_ref[...]
        bias = b_ref[...]

        def step(i, carry):
            r0 = pl.multiple_of(i * BLK, BLK)
            last = i == nb - 1
            nxt = jnp.minimum(r0 + BLK, seq - 8)
            xext = jnp.concatenate([_halo(x_ref, jnp.maximum(r0 - 8, 0), i > 0), x_ref[pl.ds(r0, BLK), :],
                                    _halo(x_ref, nxt, jnp.logical_not(last))], axis=0)
            dyext = jnp.concatenate([dy_ref[pl.ds(r0, BLK), :], _halo(dy_ref, nxt, jnp.logical_not(last))], axis=0)
            z = _conv_taps(xext, w, BLK + 8) + bias
            sg = jax.nn.sigmoid(z)
            dz = dyext * (sg * (1.0 + z * (1.0 - sg)))
            dx = None
            for j in range(CONV_K):
                sh = CONV_K - 1 - j
                dzs = pltpu.roll(dz, BLK + 8 - sh, 0) if sh else dz
                term = w[j:j + 1, :] * dzs[:BLK, :]
                dx = term if dx is None else dx + term
            dx_ref[pl.ds(r0, BLK), :] = dx
            dzm = dz[:BLK, :]
            out = []
            for j in range(CONV_K):
                sh = CONV_K - 1 - j
                xs = pltpu.roll(xext, sh, 0) if sh else xext
                out.append(carry[j] + jnp.sum(dzm * xs[8:8 + BLK, :], axis=0, keepdims=True))
            out.append(carry[CONV_K] + jnp.sum(dzm, axis=0, keepdims=True))
            return tuple(out)

        zero = jnp.zeros((1, 128), F32)
        acc = lax.fori_loop(0, nb, step, (zero,) * (CONV_K + 1))
        dw_ref[...] = jnp.concatenate(acc[:CONV_K], axis=0)
        db_ref[...] = acc[CONV_K]

    strip = pl.BlockSpec((seq, 128), lambda c: (0, c))
    wsp = pl.BlockSpec((CONV_K, 128), lambda c: (0, c))
    bsp = pl.BlockSpec((1, 128), lambda c: (0, c))
    return pl.pallas_call(
        body, name="conv_bwd", grid=(ch // 128,),
        in_specs=[strip, wsp, bsp, strip],
        out_specs=[strip, wsp, bsp],
        out_shape=[jax.ShapeDtypeStruct(x.shape, F32), jax.ShapeDtypeStruct(w.shape, F32),
                   jax.ShapeDtypeStruct(b.shape, F32)],
        compiler_params=_cp(("parallel",)),
    )(x, w, b, dy)


@jax.custom_vjp
def conv_silu(x, w, b):
    return _conv_fwd_call(x, w, b)


def _conv_silu_fwd(x, w, b):
    return _conv_fwd_call(x, w, b), (x, w, b)


def _conv_silu_bwd(res, dy):
    return tuple(_conv_bwd_call(*res, dy))


conv_silu.defvjp(_conv_silu_fwd, _conv_silu_bwd)


def _loss_call(h, wf, target):
    seq, d = h.shape
    nb = seq // BLK

    def body(h_ref, w_ref, t_ref, loss_ref, dh_ref, dw_ref):
        i = pl.program_id(0)
        live = (i > 0).astype(F32)
        tgt = t_ref[...]

        def fn(hh, ww):
            err = _rms_fn(hh, ww) - tgt
            return 0.5 * live * jnp.sum(jnp.mean(err * err, axis=-1, keepdims=True), axis=0, keepdims=True)

        val, vjp = jax.vjp(fn, h_ref[...], w_ref[...])
        dh, dw = vjp(jnp.ones((1, 1), F32))
        dh_ref[...] = dh

        @pl.when(i == 0)
        def _():
            loss_ref[...] = jnp.zeros_like(loss_ref)
            dw_ref[...] = jnp.zeros_like(dw_ref)

        loss_ref[...] += val
        dw_ref[...] += dw

    return pl.pallas_call(
        body, name="loss_head", grid=(nb,),
        in_specs=[pl.BlockSpec((BLK, d), lambda i: (i, 0)), pl.BlockSpec((1, d), lambda i: (0, 0)),
                  pl.BlockSpec((BLK, d), lambda i: (jnp.maximum(i - 1, 0), 0))],
        out_specs=[pl.BlockSpec((1, 1), lambda i: (0, 0)), pl.BlockSpec((BLK, d), lambda i: (i, 0)),
                   pl.BlockSpec((1, d), lambda i: (0, 0))],
        out_shape=[jax.ShapeDtypeStruct((1, 1), F32), jax.ShapeDtypeStruct(h.shape, F32),
                   jax.ShapeDtypeStruct((1, d), F32)],
        compiler_params=_cp(("arbitrary",)),
    )(h, wf, target)


def _make_loss_head(target):
    @jax.custom_vjp
    def head(h, wf):
        return _loss_call(h, wf, target)[0][0, 0]

    def fwd(h, wf):
        loss, dh, dw = _loss_call(h, wf, target)
        return loss[0, 0], (dh, dw)

    def bwd(res, g):
        return g * res[0], g * res[1]

    head.defvjp(fwd, bwd)
    return head


_IN_SEGS = (("q", 0, 1024), ("k", 1024, 1024), ("v", 2048, 1024), ("gate", 3072, 1024), ("z", 4112, 1024),
            ("xbc", 5136, 2048), ("cq", 7200, 1024), ("ck", 8224, 256), ("cv", 8480, 256), ("gl", 8736, 3072),
            ("b", 4096, 8), ("a", 4104, 8), ("dt", 7184, 16))
_IN_PAD = 96 + 384
_SPLIT = (1024, 1024, 1024, 1024, 1024, 2048, 1024, 512, 3072, 128, 384)


@jax.custom_vjp
def split_cols(u):
    offs = [sum(_SPLIT[:i]) for i in range(len(_SPLIT))]
    return tuple(u[:, o:o + s] for o, s in zip(offs, _SPLIT))


def _split_fwd(u):
    return split_cols(u), None


def _split_bwd(_, cts):
    return (jnp.concatenate(cts, axis=1),)


split_cols.defvjp(_split_fwd, _split_bwd)


_MATMUL = ("w_in", "w_proj_gdn", "w_proj_ssd", "w_proj_swa", "w_out", "w_up", "w_down")
_LATE = _MATMUL[1:]


def _late_weights(gathered):
    g = dict(zip(_LATE, gathered))
    full = {n: g[n].reshape(D_MODEL, D_MODEL) for n in _LATE[:4]}
    full["w_up"] = g["w_up"].transpose(1, 0, 2).reshape(D_MODEL, D_FF)
    full["w_down"] = g["w_down"].reshape(D_FF, D_MODEL)
    full.update({n + "_t": t.T for n, t in list(full.items())})
    return full


def _layer(h, p, w_in, w_in_t, slot, late_shards, next_shards=(), exchange_slots=()):
    wb = {"w_in": w_in, "w_in_t": w_in_t}

    def proj(t, name):
        return mm(t, wb[name], wb[name + "_t"], slot[name])

    u = proj(rms_op(h, p["norm1_w"].reshape(1, -1)), "w_in")
    q_pre, k_pre, v_pre, gate, z, xbc_pre, cq, ckv, gl, small, _ = split_cols(u)

    gcw = p["gdn_conv_w"]
    nob = jnp.zeros((1, GDN_H * GDN_D), F32)
    qa = conv_silu(q_pre, gcw[:, :1024], nob)
    ka = conv_silu(k_pre, gcw[:, 1024:2048], nob)
    va = conv_silu(v_pre, gcw[:, 2048:], nob)
    y_gdn, late, placeholders = gdn_core(
        qa, ka, va, gate, small, p["gdn_a_log"].reshape(GDN_H, 1, 1), p["gdn_dt_bias"].reshape(GDN_H, 1, 1),
        p["gdn_norm_w"].reshape(1, GDN_D), tuple(late_shards), tuple(exchange_slots))
    wb.update(_late_weights(late))

    xbc = conv_silu(xbc_pre, p["ssd_conv_w"], p["ssd_conv_b"].reshape(1, -1))
    y_ssd, gathered = ssd_core(xbc, z, small, p["ssd_dt_bias"].reshape(SSD_H, 1, 1),
                               p["ssd_a_log"].reshape(SSD_H, 1, 1), p["ssd_d"].reshape(SSD_H, 1, 1),
                               p["ssd_norm_w"].reshape(SSD_H, 1, SSD_P), tuple(next_shards))

    y_swa = swa_core(cq, ckv, p["swa_sinks"].reshape(SWA_QH, 1, 1))

    merged = merge_op(proj(y_gdn, "w_proj_gdn"), proj(y_ssd, "w_proj_ssd"), proj(y_swa, "w_proj_swa"), gl)
    h = h + proj(merged, "w_out")
    a1 = proj(rms_op(h, p["norm2_w"].reshape(1, -1)), "w_up")
    return h + proj(relu2_op(a1), "w_down"), gathered, placeholders


_PER_LAYER = ("norm1_w", "gdn_conv_w", "gdn_a_log", "gdn_dt_bias", "gdn_norm_w", "ssd_conv_w", "ssd_conv_b",
              "ssd_dt_bias", "ssd_a_log", "ssd_d", "ssd_norm_w", "swa_sinks", "norm2_w")


def _embed(x, meta):
    return jnp.concatenate([jnp.zeros((NPAD, D_MODEL), F32), meta, x], axis=0)


_IN_SHARD = 1476


def _in_pieces():
    out = []
    for _, s, n in _IN_SEGS:
        c = s
        while c < s + n:
            d = c // _IN_SHARD
            e = min(s + n, (d + 1) * _IN_SHARD)
            out.append((d, c - d * _IN_SHARD, e - d * _IN_SHARD))
            c = e
    return out


def _in_pieces_back():
    start, off = {}, 0
    for _, s, n in _IN_SEGS:
        start[s] = off
        off += n
    out = [[] for _ in range(N_DEV)]
    for _, s, n in sorted(_IN_SEGS, key=lambda t: t[1]):
        c = s
        while c < s + n:
            d = c // _IN_SHARD
            e = min(s + n, (d + 1) * _IN_SHARD)
            out[d].append((start[s] + c - s, start[s] + e - s))
            c = e
    return out


def _regroup_w_in(stacked):
    parts = [stacked[d, :, lo:hi] for d, lo, hi in _in_pieces()]
    return jnp.concatenate(parts + [jnp.zeros((D_MODEL, _IN_PAD), stacked.dtype)], axis=1)


def _ungroup_w_in(g):
    return [jnp.concatenate([g[:, lo:hi] for lo, hi in pieces], axis=1) for pieces in _in_pieces_back()]


def _position():
    return lax.axis_index("x"), lax.axis_index("y"), lax.axis_index("c")


_ANY = pl.BlockSpec(memory_space=pl.ANY)


def _chip_of(x, y, k):
    return (1 - x if k & 1 else x, 1 - y if k & 2 else y)


def _allgather_call(shards, name):
    n = len(shards)

    def body(*refs):
        start, relay, finish = _gather_phases(refs[:n], refs[n:2 * n], *refs[2 * n:])
        start()
        relay()
        finish()

    return pl.pallas_call(
        body, name=name,
        out_shape=_gather_out_shapes(shards),
        in_specs=[_ANY] * n, out_specs=[_ANY] * n,
        scratch_shapes=_gather_sems(n),
    )(*shards)


def _gather_out_shapes(shards):
    return [jax.ShapeDtypeStruct((N_DEV, *s.shape), s.dtype) for s in shards]


def _gather_sems(n):
    return [pltpu.SemaphoreType.DMA((7 * n,)), pltpu.SemaphoreType.DMA((7 * n,)), pltpu.SemaphoreType.DMA((n,))]


def _gather_phases(x_refs, out_refs, send_sems, recv_sems, local_sems):
    n = len(x_refs)
    x, y, c = _position()
    me, sibling = (x, y, c), (x, y, 1 - c)
    chips = [_chip_of(x, y, k) for k in (1, 2, 3)]

    def slab(a, px, py, pc):
        return out_refs[a].at[4 * px + 2 * py + pc]

    def copy(a, k, block, to, src=None):
        return pltpu.make_async_remote_copy(
            src_ref=slab(a, *block) if src is None else src, dst_ref=slab(a, *block),
            send_sem=send_sems.at[7 * a + k], recv_sem=recv_sems.at[7 * a + k], device_id=to, device_id_type=MESH)

    def mine():
        return [pltpu.make_async_copy(x_refs[a], slab(a, *me), local_sems.at[a]) for a in range(n)]

    def first():
        out = []
        for a in range(n):
            out.append(copy(a, 0, me, sibling, src=x_refs[a]))
            out += [copy(a, 1 + j, me, (*chip, c), src=x_refs[a]) for j, chip in enumerate(chips)]
        return out

    def passed():
        return [copy(a, 4 + j, (*chip, c), sibling) for j, chip in enumerate(chips) for a in range(n)]

    def start():
        for cp in mine() + first():
            cp.start()

    def relay():
        for j, chip in enumerate(chips):
            for a in range(n):
                copy(a, 1 + j, (*chip, c), me).wait_recv()
                copy(a, 4 + j, (*chip, c), sibling).start()

    def finish():
        for a in range(n):
            copy(a, 0, sibling, me).wait_recv()
        for j, chip in enumerate(chips):
            for a in range(n):
                copy(a, 4 + j, (*chip, 1 - c), me).wait_recv()
        for cp in first() + passed():
            cp.wait_send()
        for cp in mine():
            cp.wait()

    return start, relay, finish


def _sibling_exchange_call(for_c0, for_c1, name):
    n = len(for_c0)

    def body(*refs):
        c0_refs, c1_refs, out_refs = refs[:n], refs[n:2 * n], refs[2 * n:3 * n]
        send_sems, recv_sems = refs[3 * n:]
        x, y, c = _position()

        def copies(src_refs):
            return [pltpu.make_async_remote_copy(
                src_ref=src_refs[a].at[q], dst_ref=out_refs[a].at[q],
                send_sem=send_sems.at[4 * a + q], recv_sem=recv_sems.at[4 * a + q],
                device_id=(x, y, 1 - c), device_id_type=MESH) for a in range(n) for q in range(4)]

        @pl.when(c == 0)
        def _():
            for cp in copies(c1_refs):
                cp.start()

        @pl.when(c == 1)
        def _():
            for cp in copies(c0_refs):
                cp.start()

        waits = copies(c0_refs)
        for cp in waits:
            cp.wait_recv()
        for cp in waits:
            cp.wait_send()

    return pl.pallas_call(
        body, name=name,
        out_shape=[jax.ShapeDtypeStruct(g.shape, g.dtype) for g in for_c0],
        in_specs=[_ANY] * (2 * n), out_specs=[_ANY] * n,
        scratch_shapes=[pltpu.SemaphoreType.DMA((4 * n,)), pltpu.SemaphoreType.DMA((4 * n,))],
    )(*for_c0, *for_c1)


def _chip_exchange_call(partials, name):
    n = len(partials)

    def body(*refs):
        start, finish = _chip_exchange_phases(refs[:n], refs[n:2 * n], *refs[2 * n:])
        start()
        finish()

    return pl.pallas_call(
        body, name=name,
        out_shape=_chip_exchange_out_shapes(partials),
        in_specs=[_ANY] * n, out_specs=[_ANY] * n,
        scratch_shapes=_chip_exchange_sems(n),
    )(*partials)


def _chip_exchange_out_shapes(partials):
    return [jax.ShapeDtypeStruct((3, *p.shape[1:]), p.dtype) for p in partials]


def _chip_exchange_sems(n):
    return [pltpu.SemaphoreType.DMA((3 * n,)), pltpu.SemaphoreType.DMA((3 * n,))]


def _chip_exchange_phases(p_refs, out_refs, send_sems, recv_sems):
    n = len(p_refs)
    x, y, c = _position()

    def copies():
        out = []
        for a in range(n):
            for k in (1, 2, 3):
                px, py = _chip_of(x, y, k)
                out.append(pltpu.make_async_remote_copy(
                    src_ref=p_refs[a].at[2 * px + py], dst_ref=out_refs[a].at[k - 1],
                    send_sem=send_sems.at[3 * a + k - 1], recv_sem=recv_sems.at[3 * a + k - 1],
                    device_id=(px, py, c), device_id_type=MESH))
        return out

    def start():
        for cp in copies():
            cp.start()

    def finish():
        for cp in copies():
            cp.wait_recv()
        for cp in copies():
            cp.wait_send()

    return start, finish


def _chip_partial_call(for_c0, for_c1, sib, tr, name):
    _, r, c = sib.shape

    def body(c0_ref, c1_ref, s_ref, own_ref, out_ref):
        x, y, core = _position()
        mine = jnp.where(core == 0, c0_ref[...], c1_ref[...])
        partial = mine + s_ref[...]
        own = jnp.zeros((tr, c), F32)
        for q in range(4):
            own = jnp.where(2 * x + y == q, partial[q], own)
        own_ref[...] = own
        out_ref[...] = partial.astype(BF16)

    four = pl.BlockSpec((4, tr, c), lambda i: (0, i, 0))
    return pl.pallas_call(
        body, name=name, grid=(r // tr,),
        in_specs=[four, four, four],
        out_specs=[pl.BlockSpec((tr, c), lambda i: (i, 0)), four],
        out_shape=[jax.ShapeDtypeStruct((r, c), F32), jax.ShapeDtypeStruct((4, r, c), BF16)],
        compiler_params=_cp(("parallel",)),
    )(for_c0, for_c1, sib)


def _adamw_call(parts, w, m, v, tr, name):
    ns, r, c = w.shape
    counts = [len(p) for p in parts]
    flat_parts = [a for p in parts for a in p]

    def body(*refs):
        p_refs = refs[:len(flat_parts)]
        w_ref, m_ref, v_ref, g_ref, d_ref, nm_ref, nv_ref = refs[len(flat_parts):]
        at = 0
        for s in range(ns):
            g = None
            for p_ref in p_refs[at:at + counts[s]]:
                for j in range(p_ref.shape[0]):
                    term = p_ref[j].astype(F32)
                    g = term if g is None else g + term
            at += counts[s]
            nm = ADAM_B1 * m_ref[s] + (1.0 - ADAM_B1) * g
            nv = ADAM_B2 * v_ref[s] + (1.0 - ADAM_B2) * (g * g)
            m_hat = nm / (1.0 - ADAM_B1 ** ADAM_STEP)
            v_hat = nv / (1.0 - ADAM_B2 ** ADAM_STEP)
            g_ref[s] = g
            d_ref[s] = -ADAM_LR * (m_hat / (jnp.sqrt(v_hat) + ADAM_EPS) + ADAM_WD * w_ref[s])
            nm_ref[s] = nm
            nv_ref[s] = nv

    slabs = pl.BlockSpec((ns, tr, c), lambda i: (0, i, 0))
    return pl.pallas_call(
        body, name=name, grid=(r // tr,),
        in_specs=[pl.BlockSpec((a.shape[0], tr, c), lambda i: (0, i, 0)) for a in flat_parts] + [slabs] * 3,
        out_specs=[slabs] * 4,
        out_shape=[jax.ShapeDtypeStruct((ns, r, c), F32)] * 4,
        compiler_params=_cp(("parallel",)),
    )(*flat_parts, w, m, v)


_WEIGHTS = ("meta_tokens", "norm1_w", "w_in", "gdn_conv_w", "gdn_a_log", "gdn_dt_bias", "gdn_norm_w", "ssd_conv_w",
            "ssd_conv_b", "ssd_dt_bias", "ssd_a_log", "ssd_d", "ssd_norm_w", "swa_sinks", "w_proj_gdn", "w_proj_ssd",
            "w_proj_swa", "w_out", "norm2_w", "w_up", "w_down", "final_norm_w")
_SHARD_AXIS = {"meta_tokens": 1, "w_in": 2, "gdn_conv_w": 2, "ssd_conv_w": 2, "w_proj_gdn": 1, "w_proj_ssd": 1,
               "w_proj_swa": 1, "w_out": 1, "w_up": 2, "w_down": 1}
_BIG = tuple(n for n in _WEIGHTS if n in _SHARD_AXIS)
_SMALL = tuple(n for n in _WEIGHTS if n not in _SHARD_AXIS)
FLAT_C = 1024


def _pack(arrs, rows, lead=()):
    flat = jnp.concatenate([a.reshape(*lead, -1) for a in arrs], axis=-1)
    pad = rows * FLAT_C - flat.shape[-1]
    flat = jnp.pad(flat, [(0, 0)] * len(lead) + [(0, pad)])
    return flat.reshape(*lead, rows, FLAT_C)


def _unpack(flat, shapes, lead=()):
    flat = flat.reshape(*lead, -1)
    out, off = [], 0
    for s in shapes:
        n = math.prod(s)
        out.append(flat[..., off:off + n].reshape(*lead, *s))
        off += n
    return out


def _rows_for(shapes):
    n = sum(math.prod(s) for s in shapes)
    return -(-n // (FLAT_C * 8)) * 8


def _rows_tile(r, c):
    if r <= 256:
        return r
    return 128 if c > 1024 else 256


def _join(stacked, axis):
    moved = jnp.moveaxis(stacked, 0, axis)
    return moved.reshape(*moved.shape[:axis], -1, *moved.shape[axis + 2:])


def _unjoin(full, axis):
    cut = full.reshape(*full.shape[:axis], N_DEV, full.shape[axis] // N_DEV, *full.shape[axis + 1:])
    return jnp.moveaxis(cut, axis, 0)


def kernel(x, meta_tokens, norm1_w, w_in, gdn_conv_w, gdn_a_log, gdn_dt_bias, gdn_norm_w, ssd_conv_w, ssd_conv_b,
           ssd_dt_bias, ssd_a_log, ssd_d, ssd_norm_w, swa_sinks, w_proj_gdn, w_proj_ssd, w_proj_swa, w_out, norm2_w,
           w_up, w_down, final_norm_w, loss_target, m_meta_tokens, m_norm1_w, m_w_in, m_gdn_conv_w, m_gdn_a_log,
           m_gdn_dt_bias, m_gdn_norm_w, m_ssd_conv_w, m_ssd_conv_b, m_ssd_dt_bias, m_ssd_a_log, m_ssd_d, m_ssd_norm_w,
           m_swa_sinks, m_w_proj_gdn, m_w_proj_ssd, m_w_proj_swa, m_w_out, m_norm2_w, m_w_up, m_w_down,
           m_final_norm_w, v_meta_tokens, v_norm1_w, v_w_in, v_gdn_conv_w, v_gdn_a_log, v_gdn_dt_bias, v_gdn_norm_w,
           v_ssd_conv_w, v_ssd_conv_b, v_ssd_dt_bias, v_ssd_a_log, v_ssd_d, v_ssd_norm_w, v_swa_sinks, v_w_proj_gdn,
           v_w_proj_ssd, v_w_proj_swa, v_w_out, v_norm2_w, v_w_up, v_w_down, v_final_norm_w):
    args = (meta_tokens, norm1_w, w_in, gdn_conv_w, gdn_a_log, gdn_dt_bias, gdn_norm_w, ssd_conv_w, ssd_conv_b,
            ssd_dt_bias, ssd_a_log, ssd_d, ssd_norm_w, swa_sinks, w_proj_gdn, w_proj_ssd, w_proj_swa, w_out, norm2_w,
            w_up, w_down, final_norm_w, m_meta_tokens, m_norm1_w, m_w_in, m_gdn_conv_w, m_gdn_a_log,
            m_gdn_dt_bias, m_gdn_norm_w, m_ssd_conv_w, m_ssd_conv_b, m_ssd_dt_bias, m_ssd_a_log, m_ssd_d, m_ssd_norm_w,
            m_swa_sinks, m_w_proj_gdn, m_w_proj_ssd, m_w_proj_swa, m_w_out, m_norm2_w, m_w_up, m_w_down,
            m_final_norm_w, v_meta_tokens, v_norm1_w, v_w_in, v_gdn_conv_w, v_gdn_a_log, v_gdn_dt_bias, v_gdn_norm_w,
            v_ssd_conv_w, v_ssd_conv_b, v_ssd_dt_bias, v_ssd_a_log, v_ssd_d, v_ssd_norm_w, v_swa_sinks, v_w_proj_gdn,
            v_w_proj_ssd, v_w_proj_swa, v_w_out, v_norm2_w, v_w_up, v_w_down, v_final_norm_w)
    nw = len(_WEIGHTS)
    w = dict(zip(_WEIGHTS, args[:nw]))
    m = dict(zip(_WEIGHTS, args[nw:2 * nw]))
    v = dict(zip(_WEIGHTS, args[2 * nw:]))

    depth = w["w_in"].shape[0]
    small_shapes = [w[n].shape for n in _SMALL]
    small_rows = _rows_for(small_shapes)

    def flat2(t):
        return t.reshape(-1, t.shape[-1])

    tiny_names = [n for n in _BIG if n not in _MATMUL]

    def shard(n, l):
        return w[n][l].astype(BF16)

    first = _allgather_call([shard("w_in", 0)] + [flat2(w[n]) for n in tiny_names], "gather_weights")
    w_in_stacked = first[0]
    joined = {n: _join(t.reshape(N_DEV, *w[n].shape), _SHARD_AXIS[n]) for n, t in zip(tiny_names, first[1:])}
    slot_shapes = {"w_in": (D_MODEL, sum(_SPLIT)), "w_up": (D_MODEL, D_FF), "w_down": (D_FF, D_MODEL)}
    slot_shapes.update({n: (D_MODEL, D_MODEL) for n in _LATE[:4]})

    def layer_fn(l, w_in_full, late_shards, next_shards):
        w_in_t = w_in_full.T
        if l == 0:
            def fn(x_rows, meta, p, slot, exchange_slots):
                out, g, placeholders = _layer(_embed(x_rows, meta), p, w_in_full, w_in_t, slot, late_shards,
                                              next_shards, exchange_slots)
                return (out, placeholders), g
        else:
            def fn(h_in, p, slot, exchange_slots):
                out, g, placeholders = _layer(h_in, p, w_in_full, w_in_t, slot, late_shards, next_shards,
                                              exchange_slots)
                return (out, placeholders), g
        return fn

    h, vjps = None, []
    for l in range(depth):
        slot = {n: jnp.zeros(s, F32) for n, s in slot_shapes.items()}
        p = {n: (joined[n][l] if n in joined else w[n][l]) for n in _PER_LAYER}
        more = l + 1 < depth
        next_shards = [shard("w_in", l + 1)] if more else []
        exchange_slots = tuple(jnp.zeros((3, *w[n][l + 1].shape), BF16) for n in _MATMUL) if more else ()
        lead = (x[0], joined["meta_tokens"]) if l == 0 else (h,)
        fn = layer_fn(l, _regroup_w_in(w_in_stacked), [shard(n, l) for n in _LATE], next_shards)
        (h, _), vjp, g_next = jax.vjp(fn, *lead, p, slot, exchange_slots, has_aux=True)
        if more:
            w_in_stacked = g_next[0]
        vjps.append(vjp)
    loss, head_vjp = jax.vjp(_make_loss_head(loss_target[0]), h, w["final_norm_w"].reshape(1, -1))
    dh, d_final = head_vjp(jnp.ones((), F32))
    loss = lax.psum(loss, ("x", "y", "c"))

    def by_core(name, g):
        if name == "w_in":
            shards = _ungroup_w_in(g)
            return jnp.stack(shards[0::2]), jnp.stack(shards[1::2])
        if name == "w_up":
            t = g.reshape(D_MODEL, 4, 2, D_FF // N_DEV)
            return t[:, :, 0].transpose(1, 0, 2), t[:, :, 1].transpose(1, 0, 2)
        t = g.reshape(4, 2, -1, g.shape[-1])
        return t[:, 0], t[:, 1]

    own, incoming, layer_grads, outgoing = {}, {}, [None] * depth, ()
    for l in reversed(range(depth)):
        if l == 0:
            gx, d_meta, dp, dslot, arrived = vjps[0]((dh, tuple(outgoing)))
        else:
            dh, dp, dslot, arrived = vjps[l]((dh, tuple(outgoing)))
        incoming.update({(n, l + 1): t for n, t in zip(_MATMUL, arrived)})
        layer_grads[l] = dp
        todo = [((n, l), dslot[n]) for n in _MATMUL]
        if l == 0:
            full_grads = {"meta_tokens": d_meta}
            full_grads.update({n: jnp.stack([layer_grads[k][n] for k in range(depth)]) for n in tiny_names[1:]})
            todo += [((n, None), _unjoin(full_grads[n], _SHARD_AXIS[n]).reshape(N_DEV, -1, w[n].shape[-1]))
                     for n in tiny_names]
        pairs = [by_core(u[0], g) for u, g in todo]
        from_sibling = _sibling_exchange_call([a for a, _ in pairs], [b for _, b in pairs], "grads_to_sibling_%d" % l)
        outgoing = []
        for (u, _), (a0, a1), s in zip(todo, pairs, from_sibling):
            own[u], part = _chip_partial_call(a0, a1, s, _rows_tile(s.shape[1], s.shape[2]), "chip_partial_" + u[0])
            outgoing.append(part)
        if l == 0:
            incoming.update(zip([u for u, _ in todo], _chip_exchange_call(outgoing, "grads_to_chips")))

    g_small = {n: jnp.stack([layer_grads[k][n] for k in range(depth)]) for n in _SMALL if n != "final_norm_w"}
    g_small["final_norm_w"] = d_final.reshape(-1)

    by_name = {}
    for n in _BIG:
        layers = list(range(depth)) if n in _MATMUL else [None]
        parts = [[own[n, l][None], incoming[n, l]] for l in layers]
        r, c = own[n, layers[0]].shape
        stacked = [d[n].reshape(len(layers), r, c) for d in (w, m, v)]
        res = _adamw_call(parts, *stacked, _rows_tile(r, c), "adamw_" + n)
        by_name[n] = [t.reshape(w[n].shape) for t in res]

    small_parts = _allgather_call([_pack([g_small[n] for n in _SMALL], small_rows)], "gather_small_grads")
    small_out = _adamw_call([small_parts], *[_pack([d[n] for n in _SMALL], small_rows)[None] for d in (w, m, v)],
                            small_rows, "adamw_replicated")
    for kind in range(4):
        for n, t in zip(_SMALL, _unpack(small_out[kind][0], small_shapes)):
            by_name.setdefault(n, [None] * 4)[kind] = t

    outs = [by_name[n][kind] for kind in range(4) for n in _WEIGHTS]
    return (loss, gx[None], *outs)
```

```python
import functools
import math

import jax
import jax.numpy as jnp
from jax import lax
from jax.experimental import pallas as pl
from jax.experimental.pallas import tpu as pltpu

F32 = jnp.float32
BF16 = jnp.bfloat16
HI = lax.Precision.HIGH
NEG = -1e30

D_MODEL = 1024
N_META = 16
BLK = 128
NPAD = BLK - N_META
RMS_EPS = 1e-6
L2_EPS = 1e-6
CONV_K = 4

GDN_H, GDN_D, GDN_C = 8, 128, 64
SSD_H, SSD_P, SSD_G, SSD_N = 16, 64, 4, 128
SSD_HPG = SSD_H // SSD_G
SWA_QH, SWA_KVH, SWA_D = 16, 4, 64
SWA_REP = SWA_QH // SWA_KVH
D_FF = 4 * D_MODEL

N_DEV = 8
MESH = pl.DeviceIdType.MESH

ADAM_LR, ADAM_B1, ADAM_B2, ADAM_EPS, ADAM_WD, ADAM_STEP = 0.001, 0.9, 0.999, 1e-08, 0.01, 10

VMEM_LIMIT = 56 * 1024 * 1024


def _cp(sem=None):
    return pltpu.CompilerParams(dimension_semantics=sem, vmem_limit_bytes=VMEM_LIMIT)


def _dot(a, b, ca, cb, prec=HI):
    return lax.dot_general(a, b, (((ca,), (cb,)), ((), ())), precision=prec, preferred_element_type=F32)


def _nn(a, b, prec=HI):
    return _dot(a, b, 1, 0, prec)


def _nt(a, b, prec=HI):
    return _dot(a, b, 1, 1, prec)


def _tn(a, b, prec=HI):
    return _dot(a, b, 0, 0, prec)


def _bdot(a, b, ca, cb):
    return lax.dot_general(a.astype(BF16), b.astype(BF16), (((ca,), (cb,)), ((), ())), preferred_element_type=F32)


@jax.custom_vjp
def _lo_nn(a, b):
    return _bdot(a, b, 1, 0)


_lo_nn.defvjp(lambda a, b: (_bdot(a, b, 1, 0), (a, b)),
              lambda r, d: (_bdot(d, r[1], 1, 1), _bdot(r[0], d, 0, 0)))


@jax.custom_vjp
def _lo_nt(a, b):
    return _bdot(a, b, 1, 1)


_lo_nt.defvjp(lambda a, b: (_bdot(a, b, 1, 1), (a, b)),
              lambda r, d: (_bdot(d, r[1], 1, 0), _bdot(d, r[0], 0, 0)))


@jax.custom_vjp
def _lo_tn(a, b):
    return _bdot(a, b, 0, 0)


_lo_tn.defvjp(lambda a, b: (_bdot(a, b, 0, 0), (a, b)),
              lambda r, d: (_bdot(r[1], d, 1, 1), _bdot(r[0], d, 1, 0)))


def _iota2(n, m, axis):
    return lax.broadcasted_iota(jnp.int32, (n, m), axis)


def _silu(x):
    return x * jax.nn.sigmoid(x)


def _softplus(x):
    return jnp.maximum(x, 0.0) + jnp.log(1.0 + jnp.exp(-jnp.abs(x)))


def _row_of(col):
    n = col.shape[0]
    return jnp.broadcast_to(col, (n, n)).T


def _cumsum_col(col):
    n = col.shape[0]
    tril = (_iota2(n, n, 0) >= _iota2(n, n, 1)).astype(F32)
    return _nn(tril, col)


def _tri_inv(a):
    n = a.shape[0]
    r, c = _iota2(n, n, 0), _iota2(n, n, 1)
    eye = (r == c).astype(F32)
    blk = jnp.right_shift(r, 4) == jnp.right_shift(c, 4)
    d = jnp.where(blk, a, 0.0)
    off = a - d
    d2 = _nn(d, d)
    d4 = _nn(d2, d2)
    d8 = _nn(d4, d4)
    td = _nn(_nn(_nn(eye - d, eye + d2), eye + d4), eye + d8)
    m = _nn(td, off)
    m2 = _nn(m, m)
    return _nn(_nn(eye - m, eye + m2), td)


@jax.custom_vjp
def _tri_solve(a, a_t, rhs):
    return _nn(_tri_inv(a), rhs)


def _tri_solve_fwd(a, a_t, rhs):
    sol = _nn(_tri_inv(a), rhs)
    return sol, (a_t, sol)


def _tri_solve_bwd(res, dsol):
    a_t, sol = res
    drhs = _nn(_tri_inv(a_t), dsol)
    return -_nt(drhs, sol), jnp.zeros_like(a_t), drhs


_tri_solve.defvjp(_tri_solve_fwd, _tri_solve_bwd)


def _gdn_chunk(qa, ka, va, gate, a_raw, b_raw, s, a_log, dt_bias, norm_w, valid):
    c = qa.shape[0]
    q = qa * lax.rsqrt(jnp.sum(qa * qa, axis=-1, keepdims=True) + L2_EPS) * (GDN_D ** -0.5)
    k = ka * lax.rsqrt(jnp.sum(ka * ka, axis=-1, keepdims=True) + L2_EPS)
    beta = jax.nn.sigmoid(b_raw)
    g = -jnp.exp(a_log) * _softplus(a_raw + dt_bias) * valid
    gam = _cumsum_col(g)
    gam_row = _row_of(gam)
    r, cc = _iota2(c, c, 0), _iota2(c, c, 1)
    decay = jnp.exp(jnp.where(r >= cc, gam - gam_row, NEG))
    kb = k * beta
    a = jnp.where(r > cc, _lo_nt(kb, k) * decay, 0.0)
    a_t = lax.stop_gradient(jnp.where(cc > r, _bdot(k, kb, 1, 1) * jnp.exp(jnp.where(cc >= r, gam_row - gam, NEG)), 0.0))
    egam = jnp.exp(gam)
    sol = _tri_solve(a, a_t, jnp.concatenate([va * beta, kb * egam], axis=1))
    u = sol[:, :GDN_D]
    w = sol[:, GDN_D:]
    attn = _lo_nt(q, k) * decay
    g_last = jnp.sum(g, axis=0, keepdims=True)
    k_tail = k * jnp.exp(g_last - gam)
    v_new = u - _lo_nn(w, s)
    o = _lo_nn(q * egam, s) + _lo_nn(attn, v_new)
    s_new = s * jnp.exp(g_last) + _lo_tn(k_tail, v_new)
    y = o * lax.rsqrt(jnp.mean(o * o, axis=-1, keepdims=True) + RMS_EPS) * norm_w * _silu(gate)
    return y, s_new


def _valid_col(row0, n):
    return (row0 + _iota2(n, 1, 0) >= NPAD).astype(F32)


GDN_HB = GDN_H

SM_B, SM_A, SM_DT, SM_W = 0, 8, 16, 128


def _pick_cols(sm, first, n):
    return jnp.stack([sm[:, first + j:first + j + 1] for j in range(n)])


def _spread_cols(cols, first):
    lane = _iota2(1, SM_W, 1)
    out = None
    for j in range(cols.shape[0]):
        term = cols[j] * (lane == first + j).astype(F32)
        out = term if out is None else out + term
    return out


def _widen(t, width):
    if width == t.shape[1]:
        return t
    return jnp.concatenate([t, jnp.zeros((t.shape[0], width - t.shape[1]), t.dtype)], axis=1)


def _gdn_specs(nc, rev):
    ci = (lambda i: nc - 1 - i) if rev else (lambda i: i)
    hb = GDN_HB
    tile = pl.BlockSpec((GDN_C, hb * GDN_D), lambda h, i: (ci(i), h))
    col = pl.BlockSpec((GDN_C, SM_W), lambda h, i: (ci(i), 0))
    scal = pl.BlockSpec((hb, 1, 1), lambda h, i: (h, 0, 0))
    nw = pl.BlockSpec((1, GDN_D), lambda h, i: (0, 0))
    st = pl.BlockSpec((hb, 1, GDN_D, GDN_D), lambda h, i: (h, ci(i), 0, 0))
    return tile, col, scal, nw, st


def _lanes(j):
    return slice(j * GDN_D, (j + 1) * GDN_D)


def _by_head(ref):
    return jnp.stack([ref[:, _lanes(j)] for j in range(GDN_HB)])


def _gdn_fwd_call(q, k, v, gate, small, a_log, dt_bias, norm_w, shards=()):
    seq = q.shape[0]
    nc = seq // GDN_C
    ns = len(shards)
    tile, col, scal, nw, st = _gdn_specs(nc, False)

    def body(*refs):
        q_ref, k_ref, v_ref, g_ref, sm_ref, al_ref, dt_ref, nw_ref = refs[:8]
        y_ref, st_ref = refs[8 + ns:10 + ns]
        s_scr = refs[10 + 2 * ns]
        i = pl.program_id(1)
        if ns:
            start, relay, finish = _gather_phases(refs[8:8 + ns], refs[10 + ns:10 + 2 * ns], *refs[11 + 2 * ns:])
            pl.when(i == 0)(start)
            pl.when(i == nc - 1)(relay)

        @pl.when(i == 0)
        def _():
            s_scr[...] = jnp.zeros_like(s_scr)

        s = s_scr[...]
        st_ref[:, 0] = s
        sm = sm_ref[...]
        fn = jax.vmap(functools.partial(_gdn_chunk, valid=_valid_col(i * GDN_C, GDN_C)))
        y, s_new = fn(_by_head(q_ref), _by_head(k_ref), _by_head(v_ref), _by_head(g_ref),
                      _pick_cols(sm, SM_A, GDN_H), _pick_cols(sm, SM_B, GDN_H), s,
                      al_ref[...], dt_ref[...], jnp.broadcast_to(nw_ref[...], (GDN_HB, 1, GDN_D)))
        for j in range(GDN_HB):
            y_ref[:, _lanes(j)] = y[j]
        s_scr[...] = s_new
        if ns:
            pl.when(i == nc - 1)(finish)

    return pl.pallas_call(
        body, name="gdn_fwd", grid=(GDN_H // GDN_HB, nc),
        in_specs=[tile, tile, tile, tile, col, scal, scal, nw] + [_ANY] * ns,
        out_specs=[tile, st] + [_ANY] * ns,
        out_shape=[jax.ShapeDtypeStruct((seq, GDN_H * GDN_D), F32),
                   jax.ShapeDtypeStruct((GDN_H, nc, GDN_D, GDN_D), F32)] + _gather_out_shapes(shards),
        scratch_shapes=[pltpu.VMEM((GDN_HB, GDN_D, GDN_D), F32)] + (_gather_sems(ns) if ns else []),
        compiler_params=_cp(("parallel", "arbitrary")),
    )(q, k, v, gate, small, a_log, dt_bias, norm_w, *shards)


def _gdn_bwd_call(q, k, v, gate, small, a_log, dt_bias, norm_w, states, dy, outgoing=()):
    seq = q.shape[0]
    nc = seq // GDN_C
    no = len(outgoing)
    tile, col, scal, nw, st = _gdn_specs(nc, True)
    nwh = pl.BlockSpec((GDN_HB, 1, GDN_D), lambda h, i: (h, 0, 0))

    def body(*refs):
        q_ref, k_ref, v_ref, g_ref, sm_ref, al_ref, dt_ref, nw_ref, st_ref, dy_ref = refs[:10]
        dq_ref, dk_ref, dv_ref, dg_ref, dsm_ref, dal_ref, ddt_ref, dnw_ref = refs[10 + no:18 + no]
        ds_scr = refs[18 + 2 * no]
        i = pl.program_id(1)
        if no:
            start, finish = _chip_exchange_phases(refs[10:10 + no], refs[18 + no:18 + 2 * no], *refs[19 + 2 * no:])
            pl.when(i == 0)(start)

        @pl.when(i == 0)
        def _():
            ds_scr[...] = jnp.zeros_like(ds_scr)
            dal_ref[...] = jnp.zeros_like(dal_ref)
            ddt_ref[...] = jnp.zeros_like(ddt_ref)
            dnw_ref[...] = jnp.zeros_like(dnw_ref)

        sm = sm_ref[...]
        fn = jax.vmap(functools.partial(_gdn_chunk, valid=_valid_col((nc - 1 - i) * GDN_C, GDN_C)))
        _, vjp = jax.vjp(fn, _by_head(q_ref), _by_head(k_ref), _by_head(v_ref), _by_head(g_ref),
                         _pick_cols(sm, SM_A, GDN_H), _pick_cols(sm, SM_B, GDN_H), st_ref[:, 0], al_ref[...],
                         dt_ref[...], jnp.broadcast_to(nw_ref[...], (GDN_HB, 1, GDN_D)))
        dq, dk, dv, dg, da, db, ds, dal, ddt, dnw = vjp((_by_head(dy_ref), ds_scr[...]))
        for j in range(GDN_HB):
            dq_ref[:, _lanes(j)] = dq[j]
            dk_ref[:, _lanes(j)] = dk[j]
            dv_ref[:, _lanes(j)] = dv[j]
            dg_ref[:, _lanes(j)] = dg[j]
        dsm_ref[...] = _widen(_spread_cols(da, SM_A) + _spread_cols(db, SM_B), dsm_ref.shape[1])
        ds_scr[...] = ds
        dal_ref[...] += dal
        ddt_ref[...] += ddt
        dnw_ref[...] += dnw
        if no:
            pl.when(i == nc - 1)(finish)

    big = jax.ShapeDtypeStruct((seq, GDN_H * GDN_D), F32)
    return pl.pallas_call(
        body, name="gdn_bwd", grid=(GDN_H // GDN_HB, nc),
        in_specs=[tile, tile, tile, tile, col, scal, scal, nw, st, tile] + [_ANY] * no,
        out_specs=[tile, tile, tile, tile, pl.BlockSpec((GDN_C, small.shape[1]), lambda h, i: (nc - 1 - i, 0)),
                   scal, scal, nwh] + [_ANY] * no,
        out_shape=[big, big, big, big, jax.ShapeDtypeStruct(small.shape, F32),
                   jax.ShapeDtypeStruct((GDN_H, 1, 1), F32), jax.ShapeDtypeStruct((GDN_H, 1, 1), F32),
                   jax.ShapeDtypeStruct((GDN_H, 1, GDN_D), F32)] + _chip_exchange_out_shapes(outgoing),
        scratch_shapes=[pltpu.VMEM((GDN_HB, GDN_D, GDN_D), F32)] + (_chip_exchange_sems(no) if no else []),
        compiler_params=_cp(("parallel", "arbitrary")),
    )(q, k, v, gate, small, a_log, dt_bias, norm_w, states, dy, *outgoing)


@jax.custom_vjp
def gdn_core(q, k, v, gate, small, a_log, dt_bias, norm_w, shards, slots):
    y, _, *gathered = _gdn_fwd_call(q, k, v, gate, small, a_log, dt_bias, norm_w, shards)
    return y, tuple(gathered), tuple(jnp.zeros((4, *s.shape[1:]), s.dtype) for s in slots)


def _gdn_core_fwd(q, k, v, gate, small, a_log, dt_bias, norm_w, shards, slots):
    y, states, *gathered = _gdn_fwd_call(q, k, v, gate, small, a_log, dt_bias, norm_w, shards)
    out = (y, tuple(gathered), tuple(jnp.zeros((4, *s.shape[1:]), s.dtype) for s in slots))
    return out, (q, k, v, gate, small, a_log, dt_bias, norm_w, states, shards)


def _gdn_core_bwd(res, cts):
    *args, shards = res
    dy, _, outgoing = cts
    dq, dk, dv, dg, dsm, dal, ddt, dnw, *incoming = _gdn_bwd_call(*args, dy, outgoing)
    return (dq, dk, dv, dg, dsm, dal, ddt, jnp.sum(dnw, axis=0), tuple(jnp.zeros_like(s) for s in shards),
            tuple(incoming))


gdn_core.defvjp(_gdn_core_fwd, _gdn_core_bwd)


def _ssd_head(x, z, dt_raw, h, dt_bias, a_log, d_skip, bm, cm, cb, valid):
    c = bm.shape[0]
    r, cc = _iota2(c, c, 0), _iota2(c, c, 1)
    dtp = _softplus(dt_raw + dt_bias)
    x = x * valid
    adt = -jnp.exp(a_log) * dtp * valid
    xdt = x * dtp
    acum = _cumsum_col(adt)
    lmat = jnp.exp(jnp.where(r >= cc, acum - _row_of(acum), NEG))
    a_last = jnp.sum(adt, axis=0, keepdims=True)
    y = _lo_nn(cb * lmat, xdt) + _lo_nt(cm * jnp.exp(acum), h) + d_skip * x
    h_new = h * jnp.exp(a_last) + _lo_tn(xdt * jnp.exp(a_last - acum), bm)
    return y * _silu(z), h_new


def _ssd_chunk(xs, z, bm, cm, dt_raw, h, dt_bias, a_log, d_skip, norm_w, valid):
    nh, c, p = xs.shape
    ng = bm.shape[0]
    hpg = nh // ng
    bm = bm * valid
    cm = cm * valid
    cb = jax.vmap(_lo_nt)(cm, bm)
    per_head = lambda t: jnp.repeat(t, hpg, axis=0)
    ys, hs = jax.vmap(functools.partial(_ssd_head, valid=valid))(
        xs, z, dt_raw, h, dt_bias, a_log, d_skip, per_head(bm), per_head(cm), per_head(cb))
    ss = jnp.sum(jnp.sum(ys * ys, axis=-1, keepdims=True).reshape(ng, hpg, c, 1), axis=1, keepdims=True)
    rstd = lax.rsqrt(ss / (hpg * p) + RMS_EPS)
    return (ys.reshape(ng, hpg, c, p) * rstd).reshape(nh, c, p) * norm_w, hs


SSD_INNER = SSD_H * SSD_P
SSD_BC = SSD_G * SSD_N


def _split_lanes(t, n, w):
    return jnp.stack([t[:, j * w:(j + 1) * w] for j in range(n)])


def _join_lanes(t):
    return jnp.concatenate([t[j] for j in range(t.shape[0])], axis=1)


def _ssd_specs(nc, rev):
    ci = (lambda i: nc - 1 - i) if rev else (lambda i: i)
    wide = pl.BlockSpec((BLK, SSD_INNER), lambda i: (ci(i), 0))
    bmat = pl.BlockSpec((BLK, SSD_BC), lambda i: (ci(i), SSD_INNER // SSD_BC))
    cmat = pl.BlockSpec((BLK, SSD_BC), lambda i: (ci(i), SSD_INNER // SSD_BC + 1))
    xbc = pl.BlockSpec((BLK, SSD_INNER + 2 * SSD_BC), lambda i: (ci(i), 0))
    col = pl.BlockSpec((BLK, SM_W), lambda i: (ci(i), 0))
    scal = pl.BlockSpec((SSD_H, 1, 1), lambda i: (0, 0, 0))
    nw = pl.BlockSpec((SSD_H, 1, SSD_P), lambda i: (0, 0, 0))
    st = pl.BlockSpec((SSD_H, 1, SSD_P, SSD_N), lambda i: (0, ci(i), 0, 0))
    return wide, bmat, cmat, xbc, col, scal, nw, st


def _ssd_fwd_call(xbc, z, small, dt_bias, a_log, d_skip, norm_w, shards=()):
    seq = z.shape[0]
    nc = seq // BLK
    ns = len(shards)
    wide, bmat, cmat, _, col, scal, nw, st = _ssd_specs(nc, False)

    def body(*refs):
        x_ref, b_ref, c_ref, z_ref, sm_ref, db_ref, al_ref, ds_ref, nw_ref = refs[:9]
        y_ref, st_ref = refs[9 + ns:11 + ns]
        h_scr = refs[11 + 2 * ns]
        i = pl.program_id(0)
        if ns:
            start, relay, finish = _gather_phases(refs[9:9 + ns], refs[11 + ns:11 + 2 * ns], *refs[12 + 2 * ns:])
            pl.when(i == 0)(start)
            pl.when(i == nc - 1)(relay)

        @pl.when(i == 0)
        def _():
            h_scr[...] = jnp.zeros_like(h_scr)

        h = h_scr[...]
        st_ref[:, 0] = h
        y, h_new = _ssd_chunk(_split_lanes(x_ref[...], SSD_H, SSD_P), _split_lanes(z_ref[...], SSD_H, SSD_P),
                              _split_lanes(b_ref[...], SSD_G, SSD_N), _split_lanes(c_ref[...], SSD_G, SSD_N),
                              _pick_cols(sm_ref[...], SM_DT, SSD_H), h, db_ref[...], al_ref[...], ds_ref[...],
                              nw_ref[...], _valid_col(i * BLK, BLK))
        y_ref[...] = _join_lanes(y)
        h_scr[...] = h_new
        if ns:
            pl.when(i == nc - 1)(finish)

    return pl.pallas_call(
        body, name="ssd_fwd", grid=(nc,),
        in_specs=[wide, bmat, cmat, wide, col, scal, scal, scal, nw] + [_ANY] * ns,
        out_specs=[wide, st] + [_ANY] * ns,
        out_shape=[jax.ShapeDtypeStruct((seq, SSD_INNER), F32),
                   jax.ShapeDtypeStruct((SSD_H, nc, SSD_P, SSD_N), F32)] + _gather_out_shapes(shards),
        scratch_shapes=[pltpu.VMEM((SSD_H, SSD_P, SSD_N), F32)] + (_gather_sems(ns) if ns else []),
        compiler_params=_cp(("arbitrary",)),
    )(xbc, xbc, xbc, z, small, dt_bias, a_log, d_skip, norm_w, *shards)


def _ssd_bwd_call(xbc, z, small, dt_bias, a_log, d_skip, norm_w, states, dy):
    seq = z.shape[0]
    nc = seq // BLK
    wide, bmat, cmat, xbc_spec, col, scal, nw, st = _ssd_specs(nc, True)

    def body(x_ref, b_ref, c_ref, z_ref, sm_ref, db_ref, al_ref, ds_ref, nw_ref, st_ref, dy_ref,
             dxbc_ref, dz_ref, dsm_ref, ddb_ref, dal_ref, dds_ref, dnw_ref, dh_scr):
        i = pl.program_id(0)

        @pl.when(i == 0)
        def _():
            dh_scr[...] = jnp.zeros_like(dh_scr)
            ddb_ref[...] = jnp.zeros_like(ddb_ref)
            dal_ref[...] = jnp.zeros_like(dal_ref)
            dds_ref[...] = jnp.zeros_like(dds_ref)
            dnw_ref[...] = jnp.zeros_like(dnw_ref)

        fn = functools.partial(_ssd_chunk, valid=_valid_col((nc - 1 - i) * BLK, BLK))
        _, vjp = jax.vjp(fn, _split_lanes(x_ref[...], SSD_H, SSD_P), _split_lanes(z_ref[...], SSD_H, SSD_P),
                         _split_lanes(b_ref[...], SSD_G, SSD_N), _split_lanes(c_ref[...], SSD_G, SSD_N),
                         _pick_cols(sm_ref[...], SM_DT, SSD_H), st_ref[:, 0], db_ref[...], al_ref[...], ds_ref[...],
                         nw_ref[...])
        dx, dz, dbm, dcm, ddt, dh, ddb, dal, dds, dnw = vjp((_split_lanes(dy_ref[...], SSD_H, SSD_P), dh_scr[...]))
        dxbc_ref[:, :SSD_INNER] = _join_lanes(dx)
        dxbc_ref[:, SSD_INNER:SSD_INNER + SSD_BC] = _join_lanes(dbm)
        dxbc_ref[:, SSD_INNER + SSD_BC:] = _join_lanes(dcm)
        dz_ref[...] = _join_lanes(dz)
        dsm_ref[...] = _widen(_spread_cols(ddt, SM_DT), dsm_ref.shape[1])
        dh_scr[...] = dh
        ddb_ref[...] += ddb
        dal_ref[...] += dal
        dds_ref[...] += dds
        dnw_ref[...] += dnw

    sshape = jax.ShapeDtypeStruct((SSD_H, 1, 1), F32)
    return pl.pallas_call(
        body, name="ssd_bwd", grid=(nc,),
        in_specs=[wide, bmat, cmat, wide, col, scal, scal, scal, nw, st, wide],
        out_specs=[xbc_spec, wide, pl.BlockSpec((BLK, small.shape[1]), lambda i: (nc - 1 - i, 0)), scal, scal, scal, nw],
        out_shape=[jax.ShapeDtypeStruct(xbc.shape, F32), jax.ShapeDtypeStruct(z.shape, F32),
                   jax.ShapeDtypeStruct(small.shape, F32), sshape, sshape, sshape,
                   jax.ShapeDtypeStruct((SSD_H, 1, SSD_P), F32)],
        scratch_shapes=[pltpu.VMEM((SSD_H, SSD_P, SSD_N), F32)],
        compiler_params=_cp(("arbitrary",)),
    )(xbc, xbc, xbc, z, small, dt_bias, a_log, d_skip, norm_w, states, dy)


@jax.custom_vjp
def ssd_core(xbc, z, small, dt_bias, a_log, d_skip, norm_w, shards):
    y, _, *gathered = _ssd_fwd_call(xbc, z, small, dt_bias, a_log, d_skip, norm_w, shards)
    return y, tuple(gathered)


def _ssd_core_fwd(*args):
    y, states, *gathered = _ssd_fwd_call(*args)
    return (y, tuple(gathered)), (*args[:-1], states, args[-1])


def _ssd_core_bwd(res, cts):
    *args, shards = res
    return (*_ssd_bwd_call(*args, cts[0]), tuple(jnp.zeros_like(s) for s in shards))


ssd_core.defvjp(_ssd_core_fwd, _ssd_core_bwd)


def _swa_block(q, km, kp, kc, vm, vp, vc, sink, n):
    rows = SWA_REP * BLK
    qs = q.reshape(rows, SWA_D) * (SWA_D ** -0.5)
    s = _lo_nt(qs, jnp.concatenate([km, kp, kc], axis=0))
    i = jnp.bitwise_and(_iota2(rows, 3 * BLK, 0), BLK - 1)
    col = _iota2(rows, 3 * BLK, 1)
    j = jnp.bitwise_and(col, BLK - 1)
    part = jnp.right_shift(col, 7)
    ok_m = (part == 0) & (j >= NPAD) & ((n >= 1) | (j <= i))
    ok_p = (part == 1) & (n >= 2) & (j > i)
    ok_c = (part == 2) & (n >= 1) & (j <= i)
    ok = ok_m | ok_p | ok_c
    s = jnp.where(ok, s, NEG)
    snk = jnp.concatenate([jnp.broadcast_to(sink[r], (BLK, 1)) for r in range(SWA_REP)], axis=0)
    m = lax.stop_gradient(jnp.maximum(jnp.max(s, axis=-1, keepdims=True), snk))
    e = jnp.exp(s - m)
    p = e / (jnp.sum(e, axis=-1, keepdims=True) + jnp.exp(snk - m))
    o = _lo_nn(p, jnp.concatenate([vm, vp, vc], axis=0))
    return o.reshape(SWA_REP, BLK, SWA_D)


SWA_QW = SWA_QH * SWA_D
SWA_KW = SWA_KVH * SWA_D


def _swa_specs(nb, rev):
    ci = (lambda i: nb - 1 - i) if rev else (lambda i: i)
    qsp = pl.BlockSpec((BLK, SWA_QW), lambda i: (ci(i), 0))
    cur = pl.BlockSpec((BLK, 2 * SWA_KW), lambda i: (ci(i), 0))
    prev = pl.BlockSpec((BLK, 2 * SWA_KW), lambda i: (jnp.maximum(ci(i) - 1, 0), 0))
    meta = pl.BlockSpec((BLK, 2 * SWA_KW), lambda i: (0, 0))
    scal = pl.BlockSpec((SWA_QH, 1, 1), lambda i: (0, 0, 0))
    return qsp, cur, prev, meta, scal


def _swa_by_head(q, kvm, kvp, kvc, sink):
    def kv(t):
        return _split_lanes(t[:, :SWA_KW], SWA_KVH, SWA_D), _split_lanes(t[:, SWA_KW:], SWA_KVH, SWA_D)

    (km, vm), (kp, vp), (kc, vc) = kv(kvm), kv(kvp), kv(kvc)
    qh = _split_lanes(q, SWA_QH, SWA_D).reshape(SWA_KVH, SWA_REP, BLK, SWA_D)
    return qh, km, kp, kc, vm, vp, vc, sink.reshape(SWA_KVH, SWA_REP, 1, 1)


def _swa_kv_tile(dk, dv):
    return jnp.concatenate([_join_lanes(dk), _join_lanes(dv)], axis=1)


def _swa_fwd_call(q, kv, sink):
    seq = q.shape[0]
    nb = seq // BLK
    qsp, cur, prev, meta, scal = _swa_specs(nb, False)

    def body(q_ref, m_ref, p_ref, c_ref, s_ref, o_ref):
        fn = jax.vmap(functools.partial(_swa_block, n=pl.program_id(0)))
        o = fn(*_swa_by_head(q_ref[...], m_ref[...], p_ref[...], c_ref[...], s_ref[...]))
        o_ref[...] = _join_lanes(o.reshape(SWA_QH, BLK, SWA_D))

    return pl.pallas_call(
        body, name="swa_fwd", grid=(nb,),
        in_specs=[qsp, meta, prev, cur, scal],
        out_specs=qsp,
        out_shape=jax.ShapeDtypeStruct(q.shape, F32),
        compiler_params=_cp(("parallel",)),
    )(q, kv, kv, kv, sink)


def _swa_bwd_call(q, kv, sink, do):
    seq = q.shape[0]
    nb = seq // BLK
    qsp, cur, prev, meta, scal = _swa_specs(nb, True)

    def body(q_ref, m_ref, p_ref, c_ref, s_ref, do_ref, dq_ref, dkv_ref, ds_ref, prev_scr, meta_scr):
        i = pl.program_id(0)
        n = nb - 1 - i

        @pl.when(i == 0)
        def _():
            prev_scr[...] = jnp.zeros_like(prev_scr)
            meta_scr[...] = jnp.zeros_like(meta_scr)
            ds_ref[...] = jnp.zeros_like(ds_ref)

        fn = jax.vmap(functools.partial(_swa_block, n=n))
        _, vjp = jax.vjp(fn, *_swa_by_head(q_ref[...], m_ref[...], p_ref[...], c_ref[...], s_ref[...]))
        do = _split_lanes(do_ref[...], SWA_QH, SWA_D).reshape(SWA_KVH, SWA_REP, BLK, SWA_D)
        dq, dkm, dkp, dkc, dvm, dvp, dvc, dsk = vjp(do)
        dq_ref[...] = _join_lanes(dq.reshape(SWA_QH, BLK, SWA_D))
        ds_ref[...] += dsk.reshape(SWA_QH, 1, 1)
        meta_scr[...] += _swa_kv_tile(dkm, dvm)
        first = (n == 0).astype(F32)
        dkv_ref[...] = _swa_kv_tile(dkc, dvc) + prev_scr[...] + first * meta_scr[...]
        prev_scr[...] = _swa_kv_tile(dkp, dvp)

    return pl.pallas_call(
        body, name="swa_bwd", grid=(nb,),
        in_specs=[qsp, meta, prev, cur, scal, qsp],
        out_specs=[qsp, cur, scal],
        out_shape=[jax.ShapeDtypeStruct(q.shape, F32), jax.ShapeDtypeStruct(kv.shape, F32),
                   jax.ShapeDtypeStruct(sink.shape, F32)],
        scratch_shapes=[pltpu.VMEM((BLK, 2 * SWA_KW), F32)] * 2,
        compiler_params=_cp(("arbitrary",)),
    )(q, kv, kv, kv, sink, do)


@jax.custom_vjp
def swa_core(q, kv, sink):
    return _swa_fwd_call(q, kv, sink)


def _swa_core_fwd(q, kv, sink):
    return _swa_fwd_call(q, kv, sink), (q, kv, sink)


def _swa_core_bwd(res, do):
    return tuple(_swa_bwd_call(*res, do))


swa_core.defvjp(_swa_core_fwd, _swa_core_bwd)


def _tile(n, pref):
    if n <= pref:
        return n
    best = None
    for t in range(128, pref + 1, 128):
        if n % t == 0:
            best = t
    assert best is not None, (n, pref)
    return best


def _mm_tiles(m, n, kk):
    if kk > 8192:
        return _tile(m, 704), _tile(n, 512), _tile(kk, 4096)
    return _tile(m, 1408), _tile(n, 512), _tile(kk, 1408)


def _mm_call(a, b, name):
    (m, kk), n = a.shape, b.shape[1]
    tm, tn, tk = _mm_tiles(m, n, kk)
    nk = kk // tk
    a_spec = pl.BlockSpec((tm, tk), lambda i, j, k: (i, k))
    b_spec = pl.BlockSpec((tk, tn), lambda i, j, k: (k, j))

    def body(a_ref, b_ref, o_ref, acc_ref):
        k = pl.program_id(2)
        part = jnp.dot(a_ref[...].astype(BF16), b_ref[...].astype(BF16), preferred_element_type=F32)

        @pl.when(k == 0)
        def _():
            acc_ref[...] = part

        @pl.when(k > 0)
        def _():
            acc_ref[...] += part

        @pl.when(k == nk - 1)
        def _():
            o_ref[...] = acc_ref[...]

    return pl.pallas_call(
        body, name=name, grid=(m // tm, n // tn, nk),
        in_specs=[a_spec, b_spec],
        out_specs=pl.BlockSpec((tm, tn), lambda i, j, k: (i, j)),
        out_shape=jax.ShapeDtypeStruct((m, n), F32),
        scratch_shapes=[pltpu.VMEM((tm, tn), F32)],
        compiler_params=_cp(("parallel", "parallel", "arbitrary")),
    )(a, b)


@jax.custom_vjp
def mm(a, b, b_t, grad_slot):
    return _mm_call(a, b, "mm_fwd")


def _mm_fwd(a, b, b_t, grad_slot):
    return _mm_call(a, b, "mm_fwd"), (a, b, b_t)


def _mm_bwd(res, dc):
    a, b, b_t = res
    return (_mm_call(dc, b_t, "mm_dx"), jnp.zeros_like(b), jnp.zeros_like(b_t),
            _mm_call(a.astype(BF16).T, dc, "mm_dw"))


mm.defvjp(_mm_fwd, _mm_bwd)


def _mm_split_call(a, b, widths, name):
    (m, kk), n = a.shape, b.shape[1]
    tm, tn = _tile(m, 704), 512
    assert kk <= 1408 and all(wd % tn == 0 for wd in widths) and sum(widths) == n
    first = [sum(widths[:s]) // tn for s in range(len(widths))]
    count = [wd // tn for wd in widths]

    def body(a_ref, b_ref, *o_refs):
        j = pl.program_id(1)
        part = jnp.dot(a_ref[...].astype(BF16), b_ref[...].astype(BF16), preferred_element_type=F32)
        for s, o_ref in enumerate(o_refs):
            @pl.when((j >= first[s]) & (j < first[s] + count[s]))
            def _(o_ref=o_ref):
                o_ref[...] = part

    def out_spec(s):
        return pl.BlockSpec((tm, tn), lambda i, j: (i, jnp.clip(j - first[s], 0, count[s] - 1)))

    return pl.pallas_call(
        body, name=name, grid=(m // tm, n // tn),
        in_specs=[pl.BlockSpec((tm, kk), lambda i, j: (i, 0)), pl.BlockSpec((kk, tn), lambda i, j: (0, j))],
        out_specs=[out_spec(s) for s in range(len(widths))],
        out_shape=[jax.ShapeDtypeStruct((m, wd), F32) for wd in widths],
        compiler_params=_cp(("parallel", "arbitrary")),
    )(a, b)


_SPLIT = (1024, 1024, 1024, 1024, 1024, 2048, 1024, 512, 3072, 512)


@jax.custom_vjp
def mm_split(a, b, b_t, grad_slot):
    return tuple(_mm_split_call(a, b, _SPLIT, "mm_fwd_split"))


def _mm_split_fwd(a, b, b_t, grad_slot):
    return tuple(_mm_split_call(a, b, _SPLIT, "mm_fwd_split")), (a, b, b_t)


def _mm_split_bwd(res, cts):
    return _mm_bwd(res, jnp.concatenate(cts, axis=1))


mm_split.defvjp(_mm_split_fwd, _mm_split_bwd)


def _row_specs(arrs, tr):
    return [pl.BlockSpec((tr, a.shape[1]), lambda i: (i, 0)) for a in arrs]


def _par_specs(arrs):
    return [pl.BlockSpec(a.shape, lambda i: (0, 0)) for a in arrs]


def _row_fwd_call(fn, rows, params, out_cols, tr, name):
    seq = rows[0].shape[0]
    nr = len(rows)

    def body(*refs):
        vals = [r[...] for r in refs[:-1]]
        refs[-1][...] = fn(*vals)

    return pl.pallas_call(
        body, name=name, grid=(seq // tr,),
        in_specs=_row_specs(rows, tr) + _par_specs(params),
        out_specs=pl.BlockSpec((tr, out_cols), lambda i: (i, 0)),
        out_shape=jax.ShapeDtypeStruct((seq, out_cols), F32),
        compiler_params=_cp(("parallel",)),
    )(*rows, *params)


def _row_bwd_call(fn, rows, params, dy, tr, name):
    seq = rows[0].shape[0]
    nr, npar = len(rows), len(params)

    def body(*refs):
        ins = refs[:nr + npar]
        dy_ref = refs[nr + npar]
        outs = refs[nr + npar + 1:]
        _, vjp = jax.vjp(fn, *[r[...] for r in ins])
        cts = vjp(dy_ref[...])
        for o_ref, ct in zip(outs[:nr], cts[:nr]):
            o_ref[...] = ct

        @pl.when(pl.program_id(0) == 0)
        def _():
            for o_ref in outs[nr:]:
                o_ref[...] = jnp.zeros_like(o_ref)

        for o_ref, ct in zip(outs[nr:], cts[nr:]):
            o_ref[...] += ct

    return pl.pallas_call(
        body, name=name, grid=(seq // tr,),
        in_specs=_row_specs(rows, tr) + _par_specs(params) + _row_specs([dy], tr),
        out_specs=_row_specs(rows, tr) + _par_specs(params),
        out_shape=[jax.ShapeDtypeStruct(a.shape, F32) for a in (*rows, *params)],
        compiler_params=_cp(("arbitrary",)),
    )(*rows, *params, dy)


def _make_rowop(fn, nrows, out_cols, tr, name):
    @jax.custom_vjp
    def op(*args):
        return _row_fwd_call(fn, args[:nrows], args[nrows:], out_cols, tr, name + "_fwd")

    def fwd(*args):
        return op(*args), args

    def bwd(args, dy):
        return tuple(_row_bwd_call(fn, args[:nrows], args[nrows:], dy, tr, name + "_bwd"))

    op.defvjp(fwd, bwd)
    return op


def _rms_fn(x, w):
    return x * lax.rsqrt(jnp.mean(x * x, axis=-1, keepdims=True) + RMS_EPS) * w


def _merge_fn(pa, pb, pc, gl):
    d = D_MODEL
    return (jax.nn.sigmoid(gl[:, :d]) * pa + jax.nn.sigmoid(gl[:, d:2 * d]) * pb
            + jax.nn.sigmoid(gl[:, 2 * d:]) * pc)


def _relu2_fn(a):
    r = jnp.maximum(a, 0.0)
    return r * r


rms_op = _make_rowop(_rms_fn, 1, D_MODEL, 384, "rms")
merge_op = _make_rowop(_merge_fn, 4, D_MODEL, 192, "merge")
relu2_op = _make_rowop(_relu2_fn, 1, D_FF, 192, "relu2")


def _conv_taps(xext, w, nrows):
    z = None
    for j in range(CONV_K):
        sh = CONV_K - 1 - j
        xs = pltpu.roll(xext, sh, 0) if sh else xext
        term = w[j:j + 1, :] * xs[8:8 + nrows, :]
        z = term if z is None else z + term
    return z


def _halo(ref, start, ok):
    return jnp.where(ok, ref[pl.ds(pl.multiple_of(start, 8), 8), :], 0.0)


def _conv_fwd_call(x, w, b):
    seq, ch = x.shape
    nb = seq // BLK

    def body(x_ref, w_ref, b_ref, o_ref):
        w = w_ref[...]
        bias = b_ref[...]

        def step(i, carry):
            r0 = pl.multiple_of(i * BLK, BLK)
            xext = jnp.concatenate([_halo(x_ref, jnp.maximum(r0 - 8, 0), i > 0), x_ref[pl.ds(r0, BLK), :]], axis=0)
            o_ref[pl.ds(r0, BLK), :] = _silu(_conv_taps(xext, w, BLK) + bias)
            return carry

        lax.fori_loop(0, nb, step, 0)

    strip = pl.BlockSpec((seq, 128), lambda c: (0, c))
    return pl.pallas_call(
        body, name="conv_fwd", grid=(ch // 128,),
        in_specs=[strip, pl.BlockSpec((CONV_K, 128), lambda c: (0, c)), pl.BlockSpec((1, 128), lambda c: (0, c))],
        out_specs=strip, out_shape=jax.ShapeDtypeStruct(x.shape, F32),
        compiler_params=_cp(("parallel",)),
    )(x, w, b)


def _conv_bwd_call(x, w, b, dy):
    seq, ch = x.shape
    nb = seq // BLK

    def body(x_ref, w_ref, b_ref, dy_ref, dx_ref, dw_ref, db_ref):
        w = w_ref[...]
        bias = b_ref[...]

        def step(i, carry):
            r0 = pl.multiple_of(i * BLK, BLK)
            last = i == nb - 1
            nxt = jnp.minimum(r0 + BLK, seq - 8)
            xext = jnp.concatenate([_halo(x_ref, jnp.maximum(r0 - 8, 0), i > 0), x_ref[pl.ds(r0, BLK), :],
                                    _halo(x_ref, nxt, jnp.logical_not(last))], axis=0)
            dyext = jnp.concatenate([dy_ref[pl.ds(r0, BLK), :], _halo(dy_ref, nxt, jnp.logical_not(last))], axis=0)
            z = _conv_taps(xext, w, BLK + 8) + bias
            sg = jax.nn.sigmoid(z)
            dz = dyext * (sg * (1.0 + z * (1.0 - sg)))
            dx = None
            for j in range(CONV_K):
                sh = CONV_K - 1 - j
                dzs = pltpu.roll(dz, BLK + 8 - sh, 0) if sh else dz
                term = w[j:j + 1, :] * dzs[:BLK, :]
                dx = term if dx is None else dx + term
            dx_ref[pl.ds(r0, BLK), :] = dx
            dzm = dz[:BLK, :]
            out = []
            for j in range(CONV_K):
                sh = CONV_K - 1 - j
                xs = pltpu.roll(xext, sh, 0) if sh else xext
                out.append(carry[j] + jnp.sum(dzm * xs[8:8 + BLK, :], axis=0, keepdims=True))
            out.append(carry[CONV_K] + jnp.sum(dzm, axis=0, keepdims=True))
            return tuple(out)

        zero = jnp.zeros((1, 128), F32)
        acc = lax.fori_loop(0, nb, step, (zero,) * (CONV_K + 1))
        dw_ref[...] = jnp.concatenate(acc[:CONV_K], axis=0)
        db_ref[...] = acc[CONV_K]

    strip = pl.BlockSpec((seq, 128), lambda c: (0, c))
    wsp = pl.BlockSpec((CONV_K, 128), lambda c: (0, c))
    bsp = pl.BlockSpec((1, 128), lambda c: (0, c))
    return pl.pallas_call(
        body, name="conv_bwd", grid=(ch // 128,),
        in_specs=[strip, wsp, bsp, strip],
        out_specs=[strip, wsp, bsp],
        out_shape=[jax.ShapeDtypeStruct(x.shape, F32), jax.ShapeDtypeStruct(w.shape, F32),
                   jax.ShapeDtypeStruct(b.shape, F32)],
        compiler_params=_cp(("parallel",)),
    )(x, w, b, dy)


@jax.custom_vjp
def conv_silu(x, w, b):
    return _conv_fwd_call(x, w, b)


def _conv_silu_fwd(x, w, b):
    return _conv_fwd_call(x, w, b), (x, w, b)


def _conv_silu_bwd(res, dy):
    return tuple(_conv_bwd_call(*res, dy))


conv_silu.defvjp(_conv_silu_fwd, _conv_silu_bwd)


def _loss_call(h, wf, target):
    seq, d = h.shape
    nb = seq // BLK

    def body(h_ref, w_ref, t_ref, loss_ref, dh_ref, dw_ref):
        i = pl.program_id(0)
        live = (i > 0).astype(F32)
        tgt = t_ref[...]

        def fn(hh, ww):
            err = _rms_fn(hh, ww) - tgt
            return 0.5 * live * jnp.sum(jnp.mean(err * err, axis=-1, keepdims=True), axis=0, keepdims=True)

        val, vjp = jax.vjp(fn, h_ref[...], w_ref[...])
        dh, dw = vjp(jnp.ones((1, 1), F32))
        dh_ref[...] = dh

        @pl.when(i == 0)
        def _():
            loss_ref[...] = jnp.zeros_like(loss_ref)
            dw_ref[...] = jnp.zeros_like(dw_ref)

        loss_ref[...] += val
        dw_ref[...] += dw

    return pl.pallas_call(
        body, name="loss_head", grid=(nb,),
        in_specs=[pl.BlockSpec((BLK, d), lambda i: (i, 0)), pl.BlockSpec((1, d), lambda i: (0, 0)),
                  pl.BlockSpec((BLK, d), lambda i: (jnp.maximum(i - 1, 0), 0))],
        out_specs=[pl.BlockSpec((1, 1), lambda i: (0, 0)), pl.BlockSpec((BLK, d), lambda i: (i, 0)),
                   pl.BlockSpec((1, d), lambda i: (0, 0))],
        out_shape=[jax.ShapeDtypeStruct((1, 1), F32), jax.ShapeDtypeStruct(h.shape, F32),
                   jax.ShapeDtypeStruct((1, d), F32)],
        compiler_params=_cp(("arbitrary",)),
    )(h, wf, target)


def _make_loss_head(target):
    @jax.custom_vjp
    def head(h, wf):
        return _loss_call(h, wf, target)[0][0, 0]

    def fwd(h, wf):
        loss, dh, dw = _loss_call(h, wf, target)
        return loss[0, 0], (dh, dw)

    def bwd(res, g):
        return g * res[0], g * res[1]

    head.defvjp(fwd, bwd)
    return head


_IN_SEGS = (("q", 0, 1024), ("k", 1024, 1024), ("v", 2048, 1024), ("gate", 3072, 1024), ("z", 4112, 1024),
            ("xbc", 5136, 2048), ("cq", 7200, 1024), ("ck", 8224, 256), ("cv", 8480, 256), ("gl", 8736, 3072),
            ("b", 4096, 8), ("a", 4104, 8), ("dt", 7184, 16))
_IN_PAD = sum(_SPLIT) - sum(n for _, _, n in _IN_SEGS)


_MATMUL = ("w_in", "w_proj_gdn", "w_proj_ssd", "w_proj_swa", "w_out", "w_up", "w_down")
_LATE = _MATMUL[1:]


def _late_weights(gathered):
    g = dict(zip(_LATE, gathered))
    full = {n: g[n].reshape(D_MODEL, D_MODEL) for n in _LATE[:4]}
    full["w_up"] = g["w_up"].transpose(1, 0, 2).reshape(D_MODEL, D_FF)
    full["w_down"] = g["w_down"].reshape(D_FF, D_MODEL)
    full.update({n + "_t": t.T for n, t in list(full.items())})
    return full


def _layer(h, p, w_in, w_in_t, slot, late_shards, next_shards=(), exchange_slots=()):
    wb = {"w_in": w_in, "w_in_t": w_in_t}

    def proj(t, name):
        return mm(t, wb[name], wb[name + "_t"], slot[name])

    q_pre, k_pre, v_pre, gate, z, xbc_pre, cq, ckv, gl, small = mm_split(
        rms_op(h, p["norm1_w"].reshape(1, -1)), w_in, w_in_t, slot["w_in"])

    gcw = p["gdn_conv_w"]
    nob = jnp.zeros((1, GDN_H * GDN_D), F32)
    qa = conv_silu(q_pre, gcw[:, :1024], nob)
    ka = conv_silu(k_pre, gcw[:, 1024:2048], nob)
    va = conv_silu(v_pre, gcw[:, 2048:], nob)
    y_gdn, late, placeholders = gdn_core(
        qa, ka, va, gate, small, p["gdn_a_log"].reshape(GDN_H, 1, 1), p["gdn_dt_bias"].reshape(GDN_H, 1, 1),
        p["gdn_norm_w"].reshape(1, GDN_D), tuple(late_shards), tuple(exchange_slots))
    wb.update(_late_weights(late))

    xbc = conv_silu(xbc_pre, p["ssd_conv_w"], p["ssd_conv_b"].reshape(1, -1))
    y_ssd, gathered = ssd_core(xbc, z, small, p["ssd_dt_bias"].reshape(SSD_H, 1, 1),
                               p["ssd_a_log"].reshape(SSD_H, 1, 1), p["ssd_d"].reshape(SSD_H, 1, 1),
                               p["ssd_norm_w"].reshape(SSD_H, 1, SSD_P), tuple(next_shards))

    y_swa = swa_core(cq, ckv, p["swa_sinks"].reshape(SWA_QH, 1, 1))

    merged = merge_op(proj(y_gdn, "w_proj_gdn"), proj(y_ssd, "w_proj_ssd"), proj(y_swa, "w_proj_swa"), gl)
    h = h + proj(merged, "w_out")
    a1 = proj(rms_op(h, p["norm2_w"].reshape(1, -1)), "w_up")
    return h + proj(relu2_op(a1), "w_down"), gathered, placeholders


_PER_LAYER = ("norm1_w", "gdn_conv_w", "gdn_a_log", "gdn_dt_bias", "gdn_norm_w", "ssd_conv_w", "ssd_conv_b",
              "ssd_dt_bias", "ssd_a_log", "ssd_d", "ssd_norm_w", "swa_sinks", "norm2_w")


def _embed(x, meta):
    return jnp.concatenate([jnp.zeros((NPAD, D_MODEL), F32), meta, x], axis=0)


_IN_SHARD = 1476


def _in_pieces():
    out = []
    for _, s, n in _IN_SEGS:
        c = s
        while c < s + n:
            d = c // _IN_SHARD
            e = min(s + n, (d + 1) * _IN_SHARD)
            out.append((d, c - d * _IN_SHARD, e - d * _IN_SHARD))
            c = e
    return out


def _in_pieces_back():
    start, off = {}, 0
    for _, s, n in _IN_SEGS:
        start[s] = off
        off += n
    out = [[] for _ in range(N_DEV)]
    for _, s, n in sorted(_IN_SEGS, key=lambda t: t[1]):
        c = s
        while c < s + n:
            d = c // _IN_SHARD
            e = min(s + n, (d + 1) * _IN_SHARD)
            out[d].append((start[s] + c - s, start[s] + e - s))
            c = e
    return out


def _regroup_w_in(stacked):
    parts = [stacked[d, :, lo:hi] for d, lo, hi in _in_pieces()]
    return jnp.concatenate(parts + [jnp.zeros((D_MODEL, _IN_PAD), stacked.dtype)], axis=1)


def _ungroup_w_in(g):
    return [jnp.concatenate([g[:, lo:hi] for lo, hi in pieces], axis=1) for pieces in _in_pieces_back()]


def _position():
    return lax.axis_index("x"), lax.axis_index("y"), lax.axis_index("c")


_ANY = pl.BlockSpec(memory_space=pl.ANY)


def _chip_of(x, y, k):
    return (1 - x if k & 1 else x, 1 - y if k & 2 else y)


def _allgather_call(shards, name):
    n = len(shards)

    def body(*refs):
        start, relay, finish = _gather_phases(refs[:n], refs[n:2 * n], *refs[2 * n:])
        start()
        relay()
        finish()

    return pl.pallas_call(
        body, name=name,
        out_shape=_gather_out_shapes(shards),
        in_specs=[_ANY] * n, out_specs=[_ANY] * n,
        scratch_shapes=_gather_sems(n),
    )(*shards)


def _gather_out_shapes(shards):
    return [jax.ShapeDtypeStruct((N_DEV, *s.shape), s.dtype) for s in shards]


def _gather_sems(n):
    return [pltpu.SemaphoreType.DMA((7 * n,)), pltpu.SemaphoreType.DMA((7 * n,)), pltpu.SemaphoreType.DMA((n,))]


def _gather_phases(x_refs, out_refs, send_sems, recv_sems, local_sems):
    n = len(x_refs)
    x, y, c = _position()
    me, sibling = (x, y, c), (x, y, 1 - c)
    chips = [_chip_of(x, y, k) for k in (1, 2, 3)]

    def slab(a, px, py, pc):
        return out_refs[a].at[4 * px + 2 * py + pc]

    def copy(a, k, block, to, src=None):
        return pltpu.make_async_remote_copy(
            src_ref=slab(a, *block) if src is None else src, dst_ref=slab(a, *block),
            send_sem=send_sems.at[7 * a + k], recv_sem=recv_sems.at[7 * a + k], device_id=to, device_id_type=MESH)

    def mine():
        return [pltpu.make_async_copy(x_refs[a], slab(a, *me), local_sems.at[a]) for a in range(n)]

    def first():
        out = []
        for a in range(n):
            out.append(copy(a, 0, me, sibling, src=x_refs[a]))
            out += [copy(a, 1 + j, me, (*chip, c), src=x_refs[a]) for j, chip in enumerate(chips)]
        return out

    def passed():
        return [copy(a, 4 + j, (*chip, c), sibling) for j, chip in enumerate(chips) for a in range(n)]

    def start():
        for cp in mine() + first():
            cp.start()

    def relay():
        for j, chip in enumerate(chips):
            for a in range(n):
                copy(a, 1 + j, (*chip, c), me).wait_recv()
                copy(a, 4 + j, (*chip, c), sibling).start()

    def finish():
        for a in range(n):
            copy(a, 0, sibling, me).wait_recv()
        for j, chip in enumerate(chips):
            for a in range(n):
                copy(a, 4 + j, (*chip, 1 - c), me).wait_recv()
        for cp in first() + passed():
            cp.wait_send()
        for cp in mine():
            cp.wait()

    return start, relay, finish


def _sibling_exchange_call(for_c0, for_c1, name):
    n = len(for_c0)

    def body(*refs):
        c0_refs, c1_refs, out_refs = refs[:n], refs[n:2 * n], refs[2 * n:3 * n]
        send_sems, recv_sems = refs[3 * n:]
        x, y, c = _position()

        def copies(src_refs):
            return [pltpu.make_async_remote_copy(
                src_ref=src_refs[a].at[q], dst_ref=out_refs[a].at[q],
                send_sem=send_sems.at[4 * a + q], recv_sem=recv_sems.at[4 * a + q],
                device_id=(x, y, 1 - c), device_id_type=MESH) for a in range(n) for q in range(4)]

        @pl.when(c == 0)
        def _():
            for cp in copies(c1_refs):
                cp.start()

        @pl.when(c == 1)
        def _():
            for cp in copies(c0_refs):
                cp.start()

        waits = copies(c0_refs)
        for cp in waits:
            cp.wait_recv()
        for cp in waits:
            cp.wait_send()

    return pl.pallas_call(
        body, name=name,
        out_shape=[jax.ShapeDtypeStruct(g.shape, g.dtype) for g in for_c0],
        in_specs=[_ANY] * (2 * n), out_specs=[_ANY] * n,
        scratch_shapes=[pltpu.SemaphoreType.DMA((4 * n,)), pltpu.SemaphoreType.DMA((4 * n,))],
    )(*for_c0, *for_c1)


def _chip_exchange_call(partials, name):
    n = len(partials)

    def body(*refs):
        start, finish = _chip_exchange_phases(refs[:n], refs[n:2 * n], *refs[2 * n:])
        start()
        finish()

    return pl.pallas_call(
        body, name=name,
        out_shape=_chip_exchange_out_shapes(partials),
        in_specs=[_ANY] * n, out_specs=[_ANY] * n,
        scratch_shapes=_chip_exchange_sems(n),
    )(*partials)


def _chip_exchange_out_shapes(partials):
    return [jax.ShapeDtypeStruct((3, *p.shape[1:]), p.dtype) for p in partials]


def _chip_exchange_sems(n):
    return [pltpu.SemaphoreType.DMA((3 * n,)), pltpu.SemaphoreType.DMA((3 * n,))]


def _chip_exchange_phases(p_refs, out_refs, send_sems, recv_sems):
    n = len(p_refs)
    x, y, c = _position()

    def copies():
        out = []
        for a in range(n):
            for k in (1, 2, 3):
                px, py = _chip_of(x, y, k)
                out.append(pltpu.make_async_remote_copy(
                    src_ref=p_refs[a].at[2 * px + py], dst_ref=out_refs[a].at[k - 1],
                    send_sem=send_sems.at[3 * a + k - 1], recv_sem=recv_sems.at[3 * a + k - 1],
                    device_id=(px, py, c), device_id_type=MESH))
        return out

    def start():
        for cp in copies():
            cp.start()

    def finish():
        for cp in copies():
            cp.wait_recv()
        for cp in copies():
            cp.wait_send()

    return start, finish


def _chip_partial_call(for_c0, for_c1, sib, tr, name):
    _, r, c = sib.shape

    def body(c0_ref, c1_ref, s_ref, own_ref, out_ref):
        x, y, core = _position()
        mine = jnp.where(core == 0, c0_ref[...], c1_ref[...])
        partial = mine + s_ref[...]
        own = jnp.zeros((tr, c), F32)
        for q in range(4):
            own = jnp.where(2 * x + y == q, partial[q], own)
        own_ref[...] = own
        out_ref[...] = partial.astype(BF16)

    four = pl.BlockSpec((4, tr, c), lambda i: (0, i, 0))
    return pl.pallas_call(
        body, name=name, grid=(r // tr,),
        in_specs=[four, four, four],
        out_specs=[pl.BlockSpec((tr, c), lambda i: (i, 0)), four],
        out_shape=[jax.ShapeDtypeStruct((r, c), F32), jax.ShapeDtypeStruct((4, r, c), BF16)],
        compiler_params=_cp(("parallel",)),
    )(for_c0, for_c1, sib)


def _adamw_call(parts, w, m, v, tr, name):
    ns, r, c = w.shape
    counts = [len(p) for p in parts]
    flat_parts = [a for p in parts for a in p]

    def body(*refs):
        p_refs = refs[:len(flat_parts)]
        w_ref, m_ref, v_ref, g_ref, d_ref, nm_ref, nv_ref = refs[len(flat_parts):]
        at = 0
        for s in range(ns):
            g = None
            for p_ref in p_refs[at:at + counts[s]]:
                for j in range(p_ref.shape[0]):
                    term = p_ref[j].astype(F32)
                    g = term if g is None else g + term
            at += counts[s]
            nm = ADAM_B1 * m_ref[s] + (1.0 - ADAM_B1) * g
            nv = ADAM_B2 * v_ref[s] + (1.0 - ADAM_B2) * (g * g)
            m_hat = nm / (1.0 - ADAM_B1 ** ADAM_STEP)
            v_hat = nv / (1.0 - ADAM_B2 ** ADAM_STEP)
            g_ref[s] = g
            d_ref[s] = -ADAM_LR * (m_hat / (jnp.sqrt(v_hat) + ADAM_EPS) + ADAM_WD * w_ref[s])
            nm_ref[s] = nm
            nv_ref[s] = nv

    slabs = pl.BlockSpec((ns, tr, c), lambda i: (0, i, 0))
    return pl.pallas_call(
        body, name=name, grid=(r // tr,),
        in_specs=[pl.BlockSpec((a.shape[0], tr, c), lambda i: (0, i, 0)) for a in flat_parts] + [slabs] * 3,
        out_specs=[slabs] * 4,
        out_shape=[jax.ShapeDtypeStruct((ns, r, c), F32)] * 4,
        compiler_params=_cp(("parallel",)),
    )(*flat_parts, w, m, v)


_WEIGHTS = ("meta_tokens", "norm1_w", "w_in", "gdn_conv_w", "gdn_a_log", "gdn_dt_bias", "gdn_norm_w", "ssd_conv_w",
            "ssd_conv_b", "ssd_dt_bias", "ssd_a_log", "ssd_d", "ssd_norm_w", "swa_sinks", "w_proj_gdn", "w_proj_ssd",
            "w_proj_swa", "w_out", "norm2_w", "w_up", "w_down", "final_norm_w")
_SHARD_AXIS = {"meta_tokens": 1, "w_in": 2, "gdn_conv_w": 2, "ssd_conv_w": 2, "w_proj_gdn": 1, "w_proj_ssd": 1,
               "w_proj_swa": 1, "w_out": 1, "w_up": 2, "w_down": 1}
_BIG = tuple(n for n in _WEIGHTS if n in _SHARD_AXIS)
_SMALL = tuple(n for n in _WEIGHTS if n not in _SHARD_AXIS)
FLAT_C = 1024


def _pack(arrs, rows, lead=()):
    flat = jnp.concatenate([a.reshape(*lead, -1) for a in arrs], axis=-1)
    pad = rows * FLAT_C - flat.shape[-1]
    flat = jnp.pad(flat, [(0, 0)] * len(lead) + [(0, pad)])
    return flat.reshape(*lead, rows, FLAT_C)


def _unpack(flat, shapes, lead=()):
    flat = flat.reshape(*lead, -1)
    out, off = [], 0
    for s in shapes:
        n = math.prod(s)
        out.append(flat[..., off:off + n].reshape(*lead, *s))
        off += n
    return out


def _rows_for(shapes):
    n = sum(math.prod(s) for s in shapes)
    return -(-n // (FLAT_C * 8)) * 8


def _rows_tile(r, c):
    if r <= 256:
        return r
    return 128 if c > 1024 else 256


def _join(stacked, axis):
    moved = jnp.moveaxis(stacked, 0, axis)
    return moved.reshape(*moved.shape[:axis], -1, *moved.shape[axis + 2:])


def _unjoin(full, axis):
    cut = full.reshape(*full.shape[:axis], N_DEV, full.shape[axis] // N_DEV, *full.shape[axis + 1:])
    return jnp.moveaxis(cut, axis, 0)


def kernel(x, meta_tokens, norm1_w, w_in, gdn_conv_w, gdn_a_log, gdn_dt_bias, gdn_norm_w, ssd_conv_w, ssd_conv_b,
           ssd_dt_bias, ssd_a_log, ssd_d, ssd_norm_w, swa_sinks, w_proj_gdn, w_proj_ssd, w_proj_swa, w_out, norm2_w,
           w_up, w_down, final_norm_w, loss_target, m_meta_tokens, m_norm1_w, m_w_in, m_gdn_conv_w, m_gdn_a_log,
           m_gdn_dt_bias, m_gdn_norm_w, m_ssd_conv_w, m_ssd_conv_b, m_ssd_dt_bias, m_ssd_a_log, m_ssd_d, m_ssd_norm_w,
           m_swa_sinks, m_w_proj_gdn, m_w_proj_ssd, m_w_proj_swa, m_w_out, m_norm2_w, m_w_up, m_w_down,
           m_final_norm_w, v_meta_tokens, v_norm1_w, v_w_in, v_gdn_conv_w, v_gdn_a_log, v_gdn_dt_bias, v_gdn_norm_w,
           v_ssd_conv_w, v_ssd_conv_b, v_ssd_dt_bias, v_ssd_a_log, v_ssd_d, v_ssd_norm_w, v_swa_sinks, v_w_proj_gdn,
           v_w_proj_ssd, v_w_proj_swa, v_w_out, v_norm2_w, v_w_up, v_w_down, v_final_norm_w):
    args = (meta_tokens, norm1_w, w_in, gdn_conv_w, gdn_a_log, gdn_dt_bias, gdn_norm_w, ssd_conv_w, ssd_conv_b,
            ssd_dt_bias, ssd_a_log, ssd_d, ssd_norm_w, swa_sinks, w_proj_gdn, w_proj_ssd, w_proj_swa, w_out, norm2_w,
            w_up, w_down, final_norm_w, m_meta_tokens, m_norm1_w, m_w_in, m_gdn_conv_w, m_gdn_a_log,
            m_gdn_dt_bias, m_gdn_norm_w, m_ssd_conv_w, m_ssd_conv_b, m_ssd_dt_bias, m_ssd_a_log, m_ssd_d, m_ssd_norm_w,
            m_swa_sinks, m_w_proj_gdn, m_w_proj_ssd, m_w_proj_swa, m_w_out, m_norm2_w, m_w_up, m_w_down,
            m_final_norm_w, v_meta_tokens, v_norm1_w, v_w_in, v_gdn_conv_w, v_gdn_a_log, v_gdn_dt_bias, v_gdn_norm_w,
            v_ssd_conv_w, v_ssd_conv_b, v_ssd_dt_bias, v_ssd_a_log, v_ssd_d, v_ssd_norm_w, v_swa_sinks, v_w_proj_gdn,
            v_w_proj_ssd, v_w_proj_swa, v_w_out, v_norm2_w, v_w_up, v_w_down, v_final_norm_w)
    nw = len(_WEIGHTS)
    w = dict(zip(_WEIGHTS, args[:nw]))
    m = dict(zip(_WEIGHTS, args[nw:2 * nw]))
    v = dict(zip(_WEIGHTS, args[2 * nw:]))

    depth = w["w_in"].shape[0]
    small_shapes = [w[n].shape for n in _SMALL]
    small_rows = _rows_for(small_shapes)

    def flat2(t):
        return t.reshape(-1, t.shape[-1])

    tiny_names = [n for n in _BIG if n not in _MATMUL]

    def shard(n, l):
        return w[n][l].astype(BF16)

    first = _allgather_call([shard("w_in", 0)] + [flat2(w[n]) for n in tiny_names], "gather_weights")
    w_in_stacked = first[0]
    joined = {n: _join(t.reshape(N_DEV, *w[n].shape), _SHARD_AXIS[n]) for n, t in zip(tiny_names, first[1:])}
    slot_shapes = {"w_in": (D_MODEL, sum(_SPLIT)), "w_up": (D_MODEL, D_FF), "w_down": (D_FF, D_MODEL)}
    slot_shapes.update({n: (D_MODEL, D_MODEL) for n in _LATE[:4]})

    def layer_fn(l, w_in_full, late_shards, next_shards):
        w_in_t = w_in_full.T
        if l == 0:
            def fn(x_rows, meta, p, slot, exchange_slots):
                out, g, placeholders = _layer(_embed(x_rows, meta), p, w_in_full, w_in_t, slot, late_shards,
                                              next_shards, exchange_slots)
                return (out, placeholders), g
        else:
            def fn(h_in, p, slot, exchange_slots):
                out, g, placeholders = _layer(h_in, p, w_in_full, w_in_t, slot, late_shards, next_shards,
                                              exchange_slots)
                return (out, placeholders), g
        return fn

    h, vjps = None, []
    for l in range(depth):
        slot = {n: jnp.zeros(s, F32) for n, s in slot_shapes.items()}
        p = {n: (joined[n][l] if n in joined else w[n][l]) for n in _PER_LAYER}
        more = l + 1 < depth
        next_shards = [shard("w_in", l + 1)] if more else []
        exchange_slots = tuple(jnp.zeros((3, *w[n][l + 1].shape), BF16) for n in _MATMUL) if more else ()
        lead = (x[0], joined["meta_tokens"]) if l == 0 else (h,)
        fn = layer_fn(l, _regroup_w_in(w_in_stacked), [shard(n, l) for n in _LATE], next_shards)
        (h, _), vjp, g_next = jax.vjp(fn, *lead, p, slot, exchange_slots, has_aux=True)
        if more:
            w_in_stacked = g_next[0]
        vjps.append(vjp)
    loss, head_vjp = jax.vjp(_make_loss_head(loss_target[0]), h, w["final_norm_w"].reshape(1, -1))
    dh, d_final = head_vjp(jnp.ones((), F32))
    loss = lax.psum(loss, ("x", "y", "c"))

    def by_core(name, g):
        if name == "w_in":
            shards = _ungroup_w_in(g)
            return jnp.stack(shards[0::2]), jnp.stack(shards[1::2])
        if name == "w_up":
            t = g.reshape(D_MODEL, 4, 2, D_FF // N_DEV)
            return t[:, :, 0].transpose(1, 0, 2), t[:, :, 1].transpose(1, 0, 2)
        t = g.reshape(4, 2, -1, g.shape[-1])
        return t[:, 0], t[:, 1]

    own, incoming, layer_grads, outgoing = {}, {}, [None] * depth, ()
    for l in reversed(range(depth)):
        if l == 0:
            gx, d_meta, dp, dslot, arrived = vjps[0]((dh, tuple(outgoing)))
        else:
            dh, dp, dslot, arrived = vjps[l]((dh, tuple(outgoing)))
        incoming.update({(n, l + 1): t for n, t in zip(_MATMUL, arrived)})
        layer_grads[l] = dp
        todo = [((n, l), dslot[n]) for n in _MATMUL]
        if l == 0:
            full_grads = {"meta_tokens": d_meta}
            full_grads.update({n: jnp.stack([layer_grads[k][n] for k in range(depth)]) for n in tiny_names[1:]})
            todo += [((n, None), _unjoin(full_grads[n], _SHARD_AXIS[n]).reshape(N_DEV, -1, w[n].shape[-1]))
                     for n in tiny_names]
        pairs = [by_core(u[0], g) for u, g in todo]
        from_sibling = _sibling_exchange_call([a for a, _ in pairs], [b for _, b in pairs], "grads_to_sibling_%d" % l)
        outgoing = []
        for (u, _), (a0, a1), s in zip(todo, pairs, from_sibling):
            own[u], part = _chip_partial_call(a0, a1, s, _rows_tile(s.shape[1], s.shape[2]), "chip_partial_" + u[0])
            outgoing.append(part)
        if l == 0:
            incoming.update(zip([u for u, _ in todo], _chip_exchange_call(outgoing, "grads_to_chips")))

    g_small = {n: jnp.stack([layer_grads[k][n] for k in range(depth)]) for n in _SMALL if n != "final_norm_w"}
    g_small["final_norm_w"] = d_final.reshape(-1)

    by_name = {}
    for n in _BIG:
        layers = list(range(depth)) if n in _MATMUL else [None]
        parts = [[own[n, l][None], incoming[n, l]] for l in layers]
        r, c = own[n, layers[0]].shape
        stacked = [d[n].reshape(len(layers), r, c) for d in (w, m, v)]
        res = _adamw_call(parts, *stacked, _rows_tile(r, c), "adamw_" + n)
        by_name[n] = [t.reshape(w[n].shape) for t in res]

    small_parts = _allgather_call([_pack([g_small[n] for n in _SMALL], small_rows)], "gather_small_grads")
    small_out = _adamw_call([small_parts], *[_pack([d[n] for n in _SMALL], small_rows)[None] for d in (w, m, v)],
                            small_rows, "adamw_replicated")
    for kind in range(4):
        for n, t in zip(_SMALL, _unpack(small_out[kind][0], small_shapes)):
            by_name.setdefault(n, [None] * 4)[kind] = t

    outs = [by_name[n][kind] for kind in range(4) for n in _WEIGHTS]
    return (loss, gx[None], *outs)
```

```python
import functools
import math

import jax
import jax.numpy as jnp
from jax import lax
from jax.experimental import pallas as pl
from jax.experimental.pallas import tpu as pltpu

F32 = jnp.float32
BF16 = jnp.bfloat16
HI = lax.Precision.HIGH
NEG = -1e30

D_MODEL = 1024
N_META = 16
BLK = 128
NPAD = BLK - N_META
RMS_EPS = 1e-6
L2_EPS = 1e-6
CONV_K = 4

GDN_H, GDN_D, GDN_C = 8, 128, 64
SSD_H, SSD_P, SSD_G, SSD_N = 16, 64, 4, 128
SSD_HPG = SSD_H // SSD_G
SWA_QH, SWA_KVH, SWA_D = 16, 4, 64
SWA_REP = SWA_QH // SWA_KVH
D_FF = 4 * D_MODEL

N_DEV = 8
MESH = pl.DeviceIdType.MESH

ADAM_LR, ADAM_B1, ADAM_B2, ADAM_EPS, ADAM_WD, ADAM_STEP = 0.001, 0.9, 0.999, 1e-08, 0.01, 10

VMEM_LIMIT = 56 * 1024 * 1024


def _cp(sem=None):
    return pltpu.CompilerParams(dimension_semantics=sem, vmem_limit_bytes=VMEM_LIMIT)


def _dot(a, b, ca, cb, prec=HI):
    return lax.dot_general(a, b, (((ca,), (cb,)), ((), ())), precision=prec, preferred_element_type=F32)


def _nn(a, b, prec=HI):
    return _dot(a, b, 1, 0, prec)


def _nt(a, b, prec=HI):
    return _dot(a, b, 1, 1, prec)


def _tn(a, b, prec=HI):
    return _dot(a, b, 0, 0, prec)


def _bdot(a, b, ca, cb):
    return lax.dot_general(a.astype(BF16), b.astype(BF16), (((ca,), (cb,)), ((), ())), preferred_element_type=F32)


@jax.custom_vjp
def _lo_nn(a, b):
    return _bdot(a, b, 1, 0)


_lo_nn.defvjp(lambda a, b: (_bdot(a, b, 1, 0), (a, b)),
              lambda r, d: (_bdot(d, r[1], 1, 1), _bdot(r[0], d, 0, 0)))


@jax.custom_vjp
def _lo_nt(a, b):
    return _bdot(a, b, 1, 1)


_lo_nt.defvjp(lambda a, b: (_bdot(a, b, 1, 1), (a, b)),
              lambda r, d: (_bdot(d, r[1], 1, 0), _bdot(d, r[0], 0, 0)))


@jax.custom_vjp
def _lo_tn(a, b):
    return _bdot(a, b, 0, 0)


_lo_tn.defvjp(lambda a, b: (_bdot(a, b, 0, 0), (a, b)),
              lambda r, d: (_bdot(r[1], d, 1, 1), _bdot(r[0], d, 1, 0)))


def _iota2(n, m, axis):
    return lax.broadcasted_iota(jnp.int32, (n, m), axis)


def _silu(x):
    return x * jax.nn.sigmoid(x)


def _softplus(x):
    return jnp.maximum(x, 0.0) + jnp.log(1.0 + jnp.exp(-jnp.abs(x)))


def _row_of(col):
    n = col.shape[0]
    return jnp.broadcast_to(col, (n, n)).T


def _cumsum_col(col):
    n = col.shape[0]
    tril = (_iota2(n, n, 0) >= _iota2(n, n, 1)).astype(F32)
    return _nn(tril, col)


def _tri_inv(a):
    n = a.shape[0]
    r, c = _iota2(n, n, 0), _iota2(n, n, 1)
    eye = (r == c).astype(F32)
    blk = jnp.right_shift(r, 4) == jnp.right_shift(c, 4)
    d = jnp.where(blk, a, 0.0)
    off = a - d
    d2 = _nn(d, d)
    d4 = _nn(d2, d2)
    d8 = _nn(d4, d4)
    td = _nn(_nn(_nn(eye - d, eye + d2), eye + d4), eye + d8)
    m = _nn(td, off)
    m2 = _nn(m, m)
    return _nn(_nn(eye - m, eye + m2), td)


@jax.custom_vjp
def _tri_solve(a, rhs):
    return _nn(_tri_inv(a), rhs)


def _tri_solve_fwd(a, rhs):
    inv = _tri_inv(a)
    sol = _nn(inv, rhs)
    return sol, (inv, sol)


def _tri_solve_bwd(res, dsol):
    inv, sol = res
    drhs = _nn(inv.T, dsol)
    return -_nt(drhs, sol), drhs


_tri_solve.defvjp(_tri_solve_fwd, _tri_solve_bwd)


def _gdn_chunk(qa, ka, va, gate, a_raw, b_raw, s, a_log, dt_bias, norm_w, valid):
    c = qa.shape[0]
    q = qa * lax.rsqrt(jnp.sum(qa * qa, axis=-1, keepdims=True) + L2_EPS) * (GDN_D ** -0.5)
    k = ka * lax.rsqrt(jnp.sum(ka * ka, axis=-1, keepdims=True) + L2_EPS)
    beta = jax.nn.sigmoid(b_raw)
    g = -jnp.exp(a_log) * _softplus(a_raw + dt_bias) * valid
    gam = _cumsum_col(g)
    gam_row = _row_of(gam)
    r, cc = _iota2(c, c, 0), _iota2(c, c, 1)
    decay = jnp.exp(jnp.where(r >= cc, gam - gam_row, NEG))
    kb = k * beta
    a = jnp.where(r > cc, _lo_nt(kb, k) * decay, 0.0)
    egam = jnp.exp(gam)
    sol = _tri_solve(a, jnp.concatenate([va * beta, kb * egam], axis=1))
    u = sol[:, :GDN_D]
    w = sol[:, GDN_D:]
    attn = _lo_nt(q, k) * decay
    g_last = jnp.sum(g, axis=0, keepdims=True)
    k_tail = k * jnp.exp(g_last - gam)
    v_new = u - _lo_nn(w, s)
    o = _lo_nn(q * egam, s) + _lo_nn(attn, v_new)
    s_new = s * jnp.exp(g_last) + _lo_tn(k_tail, v_new)
    y = o * lax.rsqrt(jnp.mean(o * o, axis=-1, keepdims=True) + RMS_EPS) * norm_w * _silu(gate)
    return y, s_new


def _valid_col(row0, n):
    return (row0 + _iota2(n, 1, 0) >= NPAD).astype(F32)


GDN_HB = GDN_H

SM_B, SM_A, SM_DT, SM_W = 0, 8, 16, 128


def _pick_cols(sm, first, n):
    return jnp.stack([sm[:, first + j:first + j + 1] for j in range(n)])


def _spread_cols(cols, first):
    lane = _iota2(1, SM_W, 1)
    out = None
    for j in range(cols.shape[0]):
        term = cols[j] * (lane == first + j).astype(F32)
        out = term if out is None else out + term
    return out


def _widen(t, width):
    if width == t.shape[1]:
        return t
    return jnp.concatenate([t, jnp.zeros((t.shape[0], width - t.shape[1]), t.dtype)], axis=1)


def _gdn_specs(nc, rev):
    ci = (lambda i: nc - 1 - i) if rev else (lambda i: i)
    hb = GDN_HB
    tile = pl.BlockSpec((GDN_C, hb * GDN_D), lambda h, i: (ci(i), h))
    col = pl.BlockSpec((GDN_C, SM_W), lambda h, i: (ci(i), 0))
    scal = pl.BlockSpec((hb, 1, 1), lambda h, i: (h, 0, 0))
    nw = pl.BlockSpec((1, GDN_D), lambda h, i: (0, 0))
    st = pl.BlockSpec((hb, 1, GDN_D, GDN_D), lambda h, i: (h, ci(i), 0, 0))
    return tile, col, scal, nw, st


def _lanes(j):
    return slice(j * GDN_D, (j + 1) * GDN_D)


def _by_head(ref):
    return jnp.stack([ref[:, _lanes(j)] for j in range(GDN_HB)])


def _gdn_fwd_call(q, k, v, gate, small, a_log, dt_bias, norm_w, shards=()):
    seq = q.shape[0]
    nc = seq // GDN_C
    ns = len(shards)
    tile, col, scal, nw, st = _gdn_specs(nc, False)

    def body(*refs):
        q_ref, k_ref, v_ref, g_ref, sm_ref, al_ref, dt_ref, nw_ref = refs[:8]
        y_ref, st_ref = refs[8 + ns:10 + ns]
        s_scr = refs[10 + 2 * ns]
        i = pl.program_id(1)
        if ns:
            start, relay, finish = _gather_phases(refs[8:8 + ns], refs[10 + ns:10 + 2 * ns], *refs[11 + 2 * ns:])
            pl.when(i == 0)(start)
            pl.when(i == nc - 1)(relay)

        @pl.when(i == 0)
        def _():
            s_scr[...] = jnp.zeros_like(s_scr)

        s = s_scr[...]
        st_ref[:, 0] = s
        sm = sm_ref[...]
        fn = jax.vmap(functools.partial(_gdn_chunk, valid=_valid_col(i * GDN_C, GDN_C)))
        y, s_new = fn(_by_head(q_ref), _by_head(k_ref), _by_head(v_ref), _by_head(g_ref),
                      _pick_cols(sm, SM_A, GDN_H), _pick_cols(sm, SM_B, GDN_H), s,
                      al_ref[...], dt_ref[...], jnp.broadcast_to(nw_ref[...], (GDN_HB, 1, GDN_D)))
        for j in range(GDN_HB):
            y_ref[:, _lanes(j)] = y[j]
        s_scr[...] = s_new
        if ns:
            pl.when(i == nc - 1)(finish)

    return pl.pallas_call(
        body, name="gdn_fwd", grid=(GDN_H // GDN_HB, nc),
        in_specs=[tile, tile, tile, tile, col, scal, scal, nw] + [_ANY] * ns,
        out_specs=[tile, st] + [_ANY] * ns,
        out_shape=[jax.ShapeDtypeStruct((seq, GDN_H * GDN_D), F32),
                   jax.ShapeDtypeStruct((GDN_H, nc, GDN_D, GDN_D), F32)] + _gather_out_shapes(shards),
        scratch_shapes=[pltpu.VMEM((GDN_HB, GDN_D, GDN_D), F32)] + (_gather_sems(ns) if ns else []),
        compiler_params=_cp(("parallel", "arbitrary")),
    )(q, k, v, gate, small, a_log, dt_bias, norm_w, *shards)


def _gdn_bwd_call(q, k, v, gate, small, a_log, dt_bias, norm_w, states, dy, outgoing=()):
    seq = q.shape[0]
    nc = seq // GDN_C
    no = len(outgoing)
    tile, col, scal, nw, st = _gdn_specs(nc, True)
    nwh = pl.BlockSpec((GDN_HB, 1, GDN_D), lambda h, i: (h, 0, 0))

    def body(*refs):
        q_ref, k_ref, v_ref, g_ref, sm_ref, al_ref, dt_ref, nw_ref, st_ref, dy_ref = refs[:10]
        dq_ref, dk_ref, dv_ref, dg_ref, dsm_ref, dal_ref, ddt_ref, dnw_ref = refs[10 + no:18 + no]
        ds_scr = refs[18 + 2 * no]
        i = pl.program_id(1)
        if no:
            start, finish = _chip_exchange_phases(refs[10:10 + no], refs[18 + no:18 + 2 * no], *refs[19 + 2 * no:])
            pl.when(i == 0)(start)

        @pl.when(i == 0)
        def _():
            ds_scr[...] = jnp.zeros_like(ds_scr)
            dal_ref[...] = jnp.zeros_like(dal_ref)
            ddt_ref[...] = jnp.zeros_like(ddt_ref)
            dnw_ref[...] = jnp.zeros_like(dnw_ref)

        sm = sm_ref[...]
        fn = jax.vmap(functools.partial(_gdn_chunk, valid=_valid_col((nc - 1 - i) * GDN_C, GDN_C)))
        _, vjp = jax.vjp(fn, _by_head(q_ref), _by_head(k_ref), _by_head(v_ref), _by_head(g_ref),
                         _pick_cols(sm, SM_A, GDN_H), _pick_cols(sm, SM_B, GDN_H), st_ref[:, 0], al_ref[...],
                         dt_ref[...], jnp.broadcast_to(nw_ref[...], (GDN_HB, 1, GDN_D)))
        dq, dk, dv, dg, da, db, ds, dal, ddt, dnw = vjp((_by_head(dy_ref), ds_scr[...]))
        for j in range(GDN_HB):
            dq_ref[:, _lanes(j)] = dq[j]
            dk_ref[:, _lanes(j)] = dk[j]
            dv_ref[:, _lanes(j)] = dv[j]
            dg_ref[:, _lanes(j)] = dg[j]
        dsm_ref[...] = _widen(_spread_cols(da, SM_A) + _spread_cols(db, SM_B), dsm_ref.shape[1])
        ds_scr[...] = ds
        dal_ref[...] += dal
        ddt_ref[...] += ddt
        dnw_ref[...] += dnw
        if no:
            pl.when(i == nc - 1)(finish)

    big = jax.ShapeDtypeStruct((seq, GDN_H * GDN_D), F32)
    return pl.pallas_call(
        body, name="gdn_bwd", grid=(GDN_H // GDN_HB, nc),
        in_specs=[tile, tile, tile, tile, col, scal, scal, nw, st, tile] + [_ANY] * no,
        out_specs=[tile, tile, tile, tile, pl.BlockSpec((GDN_C, small.shape[1]), lambda h, i: (nc - 1 - i, 0)),
                   scal, scal, nwh] + [_ANY] * no,
        out_shape=[big, big, big, big, jax.ShapeDtypeStruct(small.shape, F32),
                   jax.ShapeDtypeStruct((GDN_H, 1, 1), F32), jax.ShapeDtypeStruct((GDN_H, 1, 1), F32),
                   jax.ShapeDtypeStruct((GDN_H, 1, GDN_D), F32)] + _chip_exchange_out_shapes(outgoing),
        scratch_shapes=[pltpu.VMEM((GDN_HB, GDN_D, GDN_D), F32)] + (_chip_exchange_sems(no) if no else []),
        compiler_params=_cp(("parallel", "arbitrary")),
    )(q, k, v, gate, small, a_log, dt_bias, norm_w, states, dy, *outgoing)


@jax.custom_vjp
def gdn_core(q, k, v, gate, small, a_log, dt_bias, norm_w, shards, slots):
    y, _, *gathered = _gdn_fwd_call(q, k, v, gate, small, a_log, dt_bias, norm_w, shards)
    return y, tuple(gathered), tuple(jnp.zeros((4, *s.shape[1:]), s.dtype) for s in slots)


def _gdn_core_fwd(q, k, v, gate, small, a_log, dt_bias, norm_w, shards, slots):
    y, states, *gathered = _gdn_fwd_call(q, k, v, gate, small, a_log, dt_bias, norm_w, shards)
    out = (y, tuple(gathered), tuple(jnp.zeros((4, *s.shape[1:]), s.dtype) for s in slots))
    return out, (q, k, v, gate, small, a_log, dt_bias, norm_w, states, shards)


def _gdn_core_bwd(res, cts):
    *args, shards = res
    dy, _, outgoing = cts
    dq, dk, dv, dg, dsm, dal, ddt, dnw, *incoming = _gdn_bwd_call(*args, dy, outgoing)
    return (dq, dk, dv, dg, dsm, dal, ddt, jnp.sum(dnw, axis=0), tuple(jnp.zeros_like(s) for s in shards),
            tuple(incoming))


gdn_core.defvjp(_gdn_core_fwd, _gdn_core_bwd)


def _ssd_head(x, z, dt_raw, h, dt_bias, a_log, d_skip, bm, cm, cb, valid):
    c = bm.shape[0]
    r, cc = _iota2(c, c, 0), _iota2(c, c, 1)
    dtp = _softplus(dt_raw + dt_bias)
    x = x * valid
    adt = -jnp.exp(a_log) * dtp * valid
    xdt = x * dtp
    acum = _cumsum_col(adt)
    lmat = jnp.exp(jnp.where(r >= cc, acum - _row_of(acum), NEG))
    a_last = jnp.sum(adt, axis=0, keepdims=True)
    y = _lo_nn(cb * lmat, xdt) + _lo_nt(cm * jnp.exp(acum), h) + d_skip * x
    h_new = h * jnp.exp(a_last) + _lo_tn(xdt * jnp.exp(a_last - acum), bm)
    return y * _silu(z), h_new


def _ssd_chunk(xs, z, bm, cm, dt_raw, h, dt_bias, a_log, d_skip, norm_w, valid):
    nh, c, p = xs.shape
    ng = bm.shape[0]
    hpg = nh // ng
    bm = bm * valid
    cm = cm * valid
    cb = jax.vmap(_lo_nt)(cm, bm)
    per_head = lambda t: jnp.repeat(t, hpg, axis=0)
    ys, hs = jax.vmap(functools.partial(_ssd_head, valid=valid))(
        xs, z, dt_raw, h, dt_bias, a_log, d_skip, per_head(bm), per_head(cm), per_head(cb))
    ss = jnp.sum(jnp.sum(ys * ys, axis=-1, keepdims=True).reshape(ng, hpg, c, 1), axis=1, keepdims=True)
    rstd = lax.rsqrt(ss / (hpg * p) + RMS_EPS)
    return (ys.reshape(ng, hpg, c, p) * rstd).reshape(nh, c, p) * norm_w, hs


SSD_INNER = SSD_H * SSD_P
SSD_BC = SSD_G * SSD_N


def _split_lanes(t, n, w):
    return jnp.stack([t[:, j * w:(j + 1) * w] for j in range(n)])


def _join_lanes(t):
    return jnp.concatenate([t[j] for j in range(t.shape[0])], axis=1)


def _ssd_specs(nc, rev):
    ci = (lambda i: nc - 1 - i) if rev else (lambda i: i)
    wide = pl.BlockSpec((BLK, SSD_INNER), lambda i: (ci(i), 0))
    bmat = pl.BlockSpec((BLK, SSD_BC), lambda i: (ci(i), SSD_INNER // SSD_BC))
    cmat = pl.BlockSpec((BLK, SSD_BC), lambda i: (ci(i), SSD_INNER // SSD_BC + 1))
    xbc = pl.BlockSpec((BLK, SSD_INNER + 2 * SSD_BC), lambda i: (ci(i), 0))
    col = pl.BlockSpec((BLK, SM_W), lambda i: (ci(i), 0))
    scal = pl.BlockSpec((SSD_H, 1, 1), lambda i: (0, 0, 0))
    nw = pl.BlockSpec((SSD_H, 1, SSD_P), lambda i: (0, 0, 0))
    st = pl.BlockSpec((SSD_H, 1, SSD_P, SSD_N), lambda i: (0, ci(i), 0, 0))
    return wide, bmat, cmat, xbc, col, scal, nw, st


def _ssd_fwd_call(xbc, z, small, dt_bias, a_log, d_skip, norm_w, shards=()):
    seq = z.shape[0]
    nc = seq // BLK
    ns = len(shards)
    wide, bmat, cmat, _, col, scal, nw, st = _ssd_specs(nc, False)

    def body(*refs):
        x_ref, b_ref, c_ref, z_ref, sm_ref, db_ref, al_ref, ds_ref, nw_ref = refs[:9]
        y_ref, st_ref = refs[9 + ns:11 + ns]
        h_scr = refs[11 + 2 * ns]
        i = pl.program_id(0)
        if ns:
            start, relay, finish = _gather_phases(refs[9:9 + ns], refs[11 + ns:11 + 2 * ns], *refs[12 + 2 * ns:])
            pl.when(i == 0)(start)
            pl.when(i == nc - 1)(relay)

        @pl.when(i == 0)
        def _():
            h_scr[...] = jnp.zeros_like(h_scr)

        h = h_scr[...]
        st_ref[:, 0] = h
        y, h_new = _ssd_chunk(_split_lanes(x_ref[...], SSD_H, SSD_P), _split_lanes(z_ref[...], SSD_H, SSD_P),
                              _split_lanes(b_ref[...], SSD_G, SSD_N), _split_lanes(c_ref[...], SSD_G, SSD_N),
                              _pick_cols(sm_ref[...], SM_DT, SSD_H), h, db_ref[...], al_ref[...], ds_ref[...],
                              nw_ref[...], _valid_col(i * BLK, BLK))
        y_ref[...] = _join_lanes(y)
        h_scr[...] = h_new
        if ns:
            pl.when(i == nc - 1)(finish)

    return pl.pallas_call(
        body, name="ssd_fwd", grid=(nc,),
        in_specs=[wide, bmat, cmat, wide, col, scal, scal, scal, nw] + [_ANY] * ns,
        out_specs=[wide, st] + [_ANY] * ns,
        out_shape=[jax.ShapeDtypeStruct((seq, SSD_INNER), F32),
                   jax.ShapeDtypeStruct((SSD_H, nc, SSD_P, SSD_N), F32)] + _gather_out_shapes(shards),
        scratch_shapes=[pltpu.VMEM((SSD_H, SSD_P, SSD_N), F32)] + (_gather_sems(ns) if ns else []),
        compiler_params=_cp(("arbitrary",)),
    )(xbc, xbc, xbc, z, small, dt_bias, a_log, d_skip, norm_w, *shards)


def _ssd_bwd_call(xbc, z, small, dt_bias, a_log, d_skip, norm_w, states, dy):
    seq = z.shape[0]
    nc = seq // BLK
    wide, bmat, cmat, xbc_spec, col, scal, nw, st = _ssd_specs(nc, True)

    def body(x_ref, b_ref, c_ref, z_ref, sm_ref, db_ref, al_ref, ds_ref, nw_ref, st_ref, dy_ref,
             dxbc_ref, dz_ref, dsm_ref, ddb_ref, dal_ref, dds_ref, dnw_ref, dh_scr):
        i = pl.program_id(0)

        @pl.when(i == 0)
        def _():
            dh_scr[...] = jnp.zeros_like(dh_scr)
            ddb_ref[...] = jnp.zeros_like(ddb_ref)
            dal_ref[...] = jnp.zeros_like(dal_ref)
            dds_ref[...] = jnp.zeros_like(dds_ref)
            dnw_ref[...] = jnp.zeros_like(dnw_ref)

        fn = functools.partial(_ssd_chunk, valid=_valid_col((nc - 1 - i) * BLK, BLK))
        _, vjp = jax.vjp(fn, _split_lanes(x_ref[...], SSD_H, SSD_P), _split_lanes(z_ref[...], SSD_H, SSD_P),
                         _split_lanes(b_ref[...], SSD_G, SSD_N), _split_lanes(c_ref[...], SSD_G, SSD_N),
                         _pick_cols(sm_ref[...], SM_DT, SSD_H), st_ref[:, 0], db_ref[...], al_ref[...], ds_ref[...],
                         nw_ref[...])
        dx, dz, dbm, dcm, ddt, dh, ddb, dal, dds, dnw = vjp((_split_lanes(dy_ref[...], SSD_H, SSD_P), dh_scr[...]))
        dxbc_ref[:, :SSD_INNER] = _join_lanes(dx)
        dxbc_ref[:, SSD_INNER:SSD_INNER + SSD_BC] = _join_lanes(dbm)
        dxbc_ref[:, SSD_INNER + SSD_BC:] = _join_lanes(dcm)
        dz_ref[...] = _join_lanes(dz)
        dsm_ref[...] = _widen(_spread_cols(ddt, SM_DT), dsm_ref.shape[1])
        dh_scr[...] = dh
        ddb_ref[...] += ddb
        dal_ref[...] += dal
        dds_ref[...] += dds
        dnw_ref[...] += dnw

    sshape = jax.ShapeDtypeStruct((SSD_H, 1, 1), F32)
    return pl.pallas_call(
        body, name="ssd_bwd", grid=(nc,),
        in_specs=[wide, bmat, cmat, wide, col, scal, scal, scal, nw, st, wide],
        out_specs=[xbc_spec, wide, pl.BlockSpec((BLK, small.shape[1]), lambda i: (nc - 1 - i, 0)), scal, scal, scal, nw],
        out_shape=[jax.ShapeDtypeStruct(xbc.shape, F32), jax.ShapeDtypeStruct(z.shape, F32),
                   jax.ShapeDtypeStruct(small.shape, F32), sshape, sshape, sshape,
                   jax.ShapeDtypeStruct((SSD_H, 1, SSD_P), F32)],
        scratch_shapes=[pltpu.VMEM((SSD_H, SSD_P, SSD_N), F32)],
        compiler_params=_cp(("arbitrary",)),
    )(xbc, xbc, xbc, z, small, dt_bias, a_log, d_skip, norm_w, states, dy)


@jax.custom_vjp
def ssd_core(xbc, z, small, dt_bias, a_log, d_skip, norm_w, shards):
    y, _, *gathered = _ssd_fwd_call(xbc, z, small, dt_bias, a_log, d_skip, norm_w, shards)
    return y, tuple(gathered)


def _ssd_core_fwd(*args):
    y, states, *gathered = _ssd_fwd_call(*args)
    return (y, tuple(gathered)), (*args[:-1], states, args[-1])


def _ssd_core_bwd(res, cts):
    *args, shards = res
    return (*_ssd_bwd_call(*args, cts[0]), tuple(jnp.zeros_like(s) for s in shards))


ssd_core.defvjp(_ssd_core_fwd, _ssd_core_bwd)


def _swa_block(q, km, kp, kc, vm, vp, vc, sink, n):
    rows = SWA_REP * BLK
    qs = q.reshape(rows, SWA_D) * (SWA_D ** -0.5)
    s = _lo_nt(qs, jnp.concatenate([km, kp, kc], axis=0))
    i = jnp.bitwise_and(_iota2(rows, 3 * BLK, 0), BLK - 1)
    col = _iota2(rows, 3 * BLK, 1)
    j = jnp.bitwise_and(col, BLK - 1)
    part = jnp.right_shift(col, 7)
    ok_m = (part == 0) & (j >= NPAD) & ((n >= 1) | (j <= i))
    ok_p = (part == 1) & (n >= 2) & (j > i)
    ok_c = (part == 2) & (n >= 1) & (j <= i)
    ok = ok_m | ok_p | ok_c
    s = jnp.where(ok, s, NEG)
    snk = jnp.concatenate([jnp.broadcast_to(sink[r], (BLK, 1)) for r in range(SWA_REP)], axis=0)
    m = lax.stop_gradient(jnp.maximum(jnp.max(s, axis=-1, keepdims=True), snk))
    e = jnp.exp(s - m)
    p = e / (jnp.sum(e, axis=-1, keepdims=True) + jnp.exp(snk - m))
    o = _lo_nn(p, jnp.concatenate([vm, vp, vc], axis=0))
    return o.reshape(SWA_REP, BLK, SWA_D)


SWA_QW = SWA_QH * SWA_D
SWA_KW = SWA_KVH * SWA_D


def _swa_specs(nb, rev):
    ci = (lambda i: nb - 1 - i) if rev else (lambda i: i)
    qsp = pl.BlockSpec((BLK, SWA_QW), lambda i: (ci(i), 0))
    cur = pl.BlockSpec((BLK, 2 * SWA_KW), lambda i: (ci(i), 0))
    prev = pl.BlockSpec((BLK, 2 * SWA_KW), lambda i: (jnp.maximum(ci(i) - 1, 0), 0))
    meta = pl.BlockSpec((BLK, 2 * SWA_KW), lambda i: (0, 0))
    scal = pl.BlockSpec((SWA_QH, 1, 1), lambda i: (0, 0, 0))
    return qsp, cur, prev, meta, scal


def _swa_by_head(q, kvm, kvp, kvc, sink):
    def kv(t):
        return _split_lanes(t[:, :SWA_KW], SWA_KVH, SWA_D), _split_lanes(t[:, SWA_KW:], SWA_KVH, SWA_D)

    (km, vm), (kp, vp), (kc, vc) = kv(kvm), kv(kvp), kv(kvc)
    qh = _split_lanes(q, SWA_QH, SWA_D).reshape(SWA_KVH, SWA_REP, BLK, SWA_D)
    return qh, km, kp, kc, vm, vp, vc, sink.reshape(SWA_KVH, SWA_REP, 1, 1)


def _swa_kv_tile(dk, dv):
    return jnp.concatenate([_join_lanes(dk), _join_lanes(dv)], axis=1)


def _swa_fwd_call(q, kv, sink):
    seq = q.shape[0]
    nb = seq // BLK
    qsp, cur, prev, meta, scal = _swa_specs(nb, False)

    def body(q_ref, m_ref, p_ref, c_ref, s_ref, o_ref):
        fn = jax.vmap(functools.partial(_swa_block, n=pl.program_id(0)))
        o = fn(*_swa_by_head(q_ref[...], m_ref[...], p_ref[...], c_ref[...], s_ref[...]))
        o_ref[...] = _join_lanes(o.reshape(SWA_QH, BLK, SWA_D))

    return pl.pallas_call(
        body, name="swa_fwd", grid=(nb,),
        in_specs=[qsp, meta, prev, cur, scal],
        out_specs=qsp,
        out_shape=jax.ShapeDtypeStruct(q.shape, F32),
        compiler_params=_cp(("parallel",)),
    )(q, kv, kv, kv, sink)


def _swa_bwd_call(q, kv, sink, do):
    seq = q.shape[0]
    nb = seq // BLK
    qsp, cur, prev, meta, scal = _swa_specs(nb, True)

    def body(q_ref, m_ref, p_ref, c_ref, s_ref, do_ref, dq_ref, dkv_ref, ds_ref, prev_scr, meta_scr):
        i = pl.program_id(0)
        n = nb - 1 - i

        @pl.when(i == 0)
        def _():
            prev_scr[...] = jnp.zeros_like(prev_scr)
            meta_scr[...] = jnp.zeros_like(meta_scr)
            ds_ref[...] = jnp.zeros_like(ds_ref)

        fn = jax.vmap(functools.partial(_swa_block, n=n))
        _, vjp = jax.vjp(fn, *_swa_by_head(q_ref[...], m_ref[...], p_ref[...], c_ref[...], s_ref[...]))
        do = _split_lanes(do_ref[...], SWA_QH, SWA_D).reshape(SWA_KVH, SWA_REP, BLK, SWA_D)
        dq, dkm, dkp, dkc, dvm, dvp, dvc, dsk = vjp(do)
        dq_ref[...] = _join_lanes(dq.reshape(SWA_QH, BLK, SWA_D))
        ds_ref[...] += dsk.reshape(SWA_QH, 1, 1)
        meta_scr[...] += _swa_kv_tile(dkm, dvm)
        first = (n == 0).astype(F32)
        dkv_ref[...] = _swa_kv_tile(dkc, dvc) + prev_scr[...] + first * meta_scr[...]
        prev_scr[...] = _swa_kv_tile(dkp, dvp)

    return pl.pallas_call(
        body, name="swa_bwd", grid=(nb,),
        in_specs=[qsp, meta, prev, cur, scal, qsp],
        out_specs=[qsp, cur, scal],
        out_shape=[jax.ShapeDtypeStruct(q.shape, F32), jax.ShapeDtypeStruct(kv.shape, F32),
                   jax.ShapeDtypeStruct(sink.shape, F32)],
        scratch_shapes=[pltpu.VMEM((BLK, 2 * SWA_KW), F32)] * 2,
        compiler_params=_cp(("arbitrary",)),
    )(q, kv, kv, kv, sink, do)


@jax.custom_vjp
def swa_core(q, kv, sink):
    return _swa_fwd_call(q, kv, sink)


def _swa_core_fwd(q, kv, sink):
    return _swa_fwd_call(q, kv, sink), (q, kv, sink)


def _swa_core_bwd(res, do):
    return tuple(_swa_bwd_call(*res, do))


swa_core.defvjp(_swa_core_fwd, _swa_core_bwd)


def _tile(n, pref):
    if n <= pref:
        return n
    best = None
    for t in range(128, pref + 1, 128):
        if n % t == 0:
            best = t
    assert best is not None, (n, pref)
    return best


def _mm_tiles(m, n, kk):
    if kk > 8192:
        return _tile(m, 704), _tile(n, 512), _tile(kk, 4096)
    return _tile(m, 1408), _tile(n, 512), _tile(kk, 1408)


def _mm_call(a, b, name):
    (m, kk), n = a.shape, b.shape[1]
    tm, tn, tk = _mm_tiles(m, n, kk)
    nk = kk // tk
    a_spec = pl.BlockSpec((tm, tk), lambda i, j, k: (i, k))
    b_spec = pl.BlockSpec((tk, tn), lambda i, j, k: (k, j))

    def body(a_ref, b_ref, o_ref, acc_ref):
        k = pl.program_id(2)
        part = jnp.dot(a_ref[...].astype(BF16), b_ref[...].astype(BF16), preferred_element_type=F32)

        @pl.when(k == 0)
        def _():
            acc_ref[...] = part

        @pl.when(k > 0)
        def _():
            acc_ref[...] += part

        @pl.when(k == nk - 1)
        def _():
            o_ref[...] = acc_ref[...]

    return pl.pallas_call(
        body, name=name, grid=(m // tm, n // tn, nk),
        in_specs=[a_spec, b_spec],
        out_specs=pl.BlockSpec((tm, tn), lambda i, j, k: (i, j)),
        out_shape=jax.ShapeDtypeStruct((m, n), F32),
        scratch_shapes=[pltpu.VMEM((tm, tn), F32)],
        compiler_params=_cp(("parallel", "parallel", "arbitrary")),
    )(a, b)


@jax.custom_vjp
def mm(a, b, b_t, grad_slot):
    return _mm_call(a, b, "mm_fwd")


def _mm_fwd(a, b, b_t, grad_slot):
    return _mm_call(a, b, "mm_fwd"), (a, b, b_t)


def _mm_bwd(res, dc):
    a, b, b_t = res
    return (_mm_call(dc, b_t, "mm_dx"), jnp.zeros_like(b), jnp.zeros_like(b_t),
            _mm_call(a.astype(BF16).T, dc, "mm_dw"))


mm.defvjp(_mm_fwd, _mm_bwd)


_SPLIT = (1024, 1024, 1024, 1024, 1024, 2048, 1024, 512, 3072, 512)


@jax.custom_vjp
def split_cols(u):
    offs = [sum(_SPLIT[:i]) for i in range(len(_SPLIT))]
    return tuple(u[:, o:o + s] for o, s in zip(offs, _SPLIT))


def _split_fwd(u):
    return split_cols(u), None


def _split_bwd(_, cts):
    return (jnp.concatenate(cts, axis=1),)


split_cols.defvjp(_split_fwd, _split_bwd)


def _row_specs(arrs, tr):
    return [pl.BlockSpec((tr, a.shape[1]), lambda i: (i, 0)) for a in arrs]


def _par_specs(arrs):
    return [pl.BlockSpec(a.shape, lambda i: (0, 0)) for a in arrs]


def _row_fwd_call(fn, rows, params, out_cols, tr, name):
    seq = rows[0].shape[0]
    nr = len(rows)

    def body(*refs):
        vals = [r[...] for r in refs[:-1]]
        refs[-1][...] = fn(*vals)

    return pl.pallas_call(
        body, name=name, grid=(seq // tr,),
        in_specs=_row_specs(rows, tr) + _par_specs(params),
        out_specs=pl.BlockSpec((tr, out_cols), lambda i: (i, 0)),
        out_shape=jax.ShapeDtypeStruct((seq, out_cols), F32),
        compiler_params=_cp(("parallel",)),
    )(*rows, *params)


def _row_bwd_call(fn, rows, params, dy, tr, name):
    seq = rows[0].shape[0]
    nr, npar = len(rows), len(params)

    def body(*refs):
        ins = refs[:nr + npar]
        dy_ref = refs[nr + npar]
        outs = refs[nr + npar + 1:]
        _, vjp = jax.vjp(fn, *[r[...] for r in ins])
        cts = vjp(dy_ref[...])
        for o_ref, ct in zip(outs[:nr], cts[:nr]):
            o_ref[...] = ct

        @pl.when(pl.program_id(0) == 0)
        def _():
            for o_ref in outs[nr:]:
                o_ref[...] = jnp.zeros_like(o_ref)

        for o_ref, ct in zip(outs[nr:], cts[nr:]):
            o_ref[...] += ct

    return pl.pallas_call(
        body, name=name, grid=(seq // tr,),
        in_specs=_row_specs(rows, tr) + _par_specs(params) + _row_specs([dy], tr),
        out_specs=_row_specs(rows, tr) + _par_specs(params),
        out_shape=[jax.ShapeDtypeStruct(a.shape, F32) for a in (*rows, *params)],
        compiler_params=_cp(("arbitrary",)),
    )(*rows, *params, dy)


def _make_rowop(fn, nrows, out_cols, tr, name):
    @jax.custom_vjp
    def op(*args):
        return _row_fwd_call(fn, args[:nrows], args[nrows:], out_cols, tr, name + "_fwd")

    def fwd(*args):
        return op(*args), args

    def bwd(args, dy):
        return tuple(_row_bwd_call(fn, args[:nrows], args[nrows:], dy, tr, name + "_bwd"))

    op.defvjp(fwd, bwd)
    return op


def _rms_fn(x, w):
    return x * lax.rsqrt(jnp.mean(x * x, axis=-1, keepdims=True) + RMS_EPS) * w


def _merge_fn(pa, pb, pc, gl):
    d = D_MODEL
    return (jax.nn.sigmoid(gl[:, :d]) * pa + jax.nn.sigmoid(gl[:, d:2 * d]) * pb
            + jax.nn.sigmoid(gl[:, 2 * d:]) * pc)


def _relu2_fn(a):
    r = jnp.maximum(a, 0.0)
    return r * r


rms_op = _make_rowop(_rms_fn, 1, D_MODEL, 384, "rms")
merge_op = _make_rowop(_merge_fn, 4, D_MODEL, 192, "merge")
relu2_op = _make_rowop(_relu2_fn, 1, D_FF, 192, "relu2")


def _conv_taps(xext, w, nrows):
    z = None
    for j in range(CONV_K):
        sh = CONV_K - 1 - j
        xs = pltpu.roll(xext, sh, 0) if sh else xext
        term = w[j:j + 1, :] * xs[8:8 + nrows, :]
        z = term if z is None else z + term
    return z


def _halo(ref, start, ok):
    return jnp.where(ok, ref[pl.ds(pl.multiple_of(start, 8), 8), :], 0.0)


def _conv_fwd_call(x, w, b):
    seq, ch = x.shape
    nb = seq // BLK

    def body(x_ref, w_ref, b_ref, o_ref):
        w = w_ref[...]
        bias = b_ref[...]

        def step(i, carry):
            r0 = pl.multiple_of(i * BLK, BLK)
            xext = jnp.concatenate([_halo(x_ref, jnp.maximum(r0 - 8, 0), i > 0), x_ref[pl.ds(r0, BLK), :]], axis=0)
            o_ref[pl.ds(r0, BLK), :] = _silu(_conv_taps(xext, w, BLK) + bias)
            return carry

        lax.fori_loop(0, nb, step, 0)

    strip = pl.BlockSpec((seq, 128), lambda c: (0, c))
    return pl.pallas_call(
        body, name="conv_fwd", grid=(ch // 128,),
        in_specs=[strip, pl.BlockSpec((CONV_K, 128), lambda c: (0, c)), pl.BlockSpec((1, 128), lambda c: (0, c))],
        out_specs=strip, out_shape=jax.ShapeDtypeStruct(x.shape, F32),
        compiler_params=_cp(("parallel",)),
    )(x, w, b)


def _conv_bwd_call(x, w, b, dy):
    seq, ch = x.shape
    nb = seq // BLK

    def body(x_ref, w_ref, b_ref, dy_ref, dx_ref, dw_ref, db_ref):
        w = w_ref[...]
        bias = b_ref[...]

        def step(i, carry):
            r0 = pl.multiple_of(i * BLK, BLK)
            last = i == nb - 1
            nxt = jnp.minimum(r0 + BLK, seq - 8)
            xext = jnp.concatenate([_halo(x_ref, jnp.maximum(r0 - 8, 0), i > 0), x_ref[pl.ds(r0, BLK), :],
                                    _halo(x_ref, nxt, jnp.logical_not(last))], axis=0)
            dyext = jnp.concatenate([dy_ref[pl.ds(r0, BLK), :], _halo(dy_ref, nxt, jnp.logical_not(last))], axis=0)
            z = _conv_taps(xext, w, BLK + 8) + bias
            sg = jax.nn.sigmoid(z)
            dz = dyext * (sg * (1.0 + z * (1.0 - sg)))
            dx = None
            for j in range(CONV_K):
                sh = CONV_K - 1 - j
                dzs = pltpu.roll(dz, BLK + 8 - sh, 0) if sh else dz
                term = w[j:j + 1, :] * dzs[:BLK, :]
                dx = term if dx is None else dx + term
            dx_ref[pl.ds(r0, BLK), :] = dx
            dzm = dz[:BLK, :]
            out = []
            for j in range(CONV_K):
                sh = CONV_K - 1 - j
                xs = pltpu.roll(xext, sh, 0) if sh else xext
                out.append(carry[j] + jnp.sum(dzm * xs[8:8 + BLK, :], axis=0, keepdims=True))
            out.append(carry[CONV_K] + jnp.sum(dzm, axis=0, keepdims=True))
            return tuple(out)

        zero = jnp.zeros((1, 128), F32)
        acc = lax.fori_loop(0, nb, step, (zero,) * (CONV_K + 1))
        dw_ref[...] = jnp.concatenate(acc[:CONV_K], axis=0)
        db_ref[...] = acc[CONV_K]

    strip = pl.BlockSpec((seq, 128), lambda c: (0, c))
    wsp = pl.BlockSpec((CONV_K, 128), lambda c: (0, c))
    bsp = pl.BlockSpec((1, 128), lambda c: (0, c))
    return pl.pallas_call(
        body, name="conv_bwd", grid=(ch // 128,),
        in_specs=[strip, wsp, bsp, strip],
        out_specs=[strip, wsp, bsp],
        out_shape=[jax.ShapeDtypeStruct(x.shape, F32), jax.ShapeDtypeStruct(w.shape, F32),
                   jax.ShapeDtypeStruct(b.shape, F32)],
        compiler_params=_cp(("parallel",)),
    )(x, w, b, dy)


@jax.custom_vjp
def conv_silu(x, w, b):
    return _conv_fwd_call(x, w, b)


def _conv_silu_fwd(x, w, b):
    return _conv_fwd_call(x, w, b), (x, w, b)


def _conv_silu_bwd(res, dy):
    return tuple(_conv_bwd_call(*res, dy))


conv_silu.defvjp(_conv_silu_fwd, _conv_silu_bwd)


def _loss_call(h, wf, target):
    seq, d = h.shape
    nb = seq // BLK

    def body(h_ref, w_ref, t_ref, loss_ref, dh_ref, dw_ref):
        i = pl.program_id(0)
        live = (i > 0).astype(F32)
        tgt = t_ref[...]

        def fn(hh, ww):
            err = _rms_fn(hh, ww) - tgt
            return 0.5 * live * jnp.sum(jnp.mean(err * err, axis=-1, keepdims=True), axis=0, keepdims=True)

        val, vjp = jax.vjp(fn, h_ref[...], w_ref[...])
        dh, dw = vjp(jnp.ones((1, 1), F32))
        dh_ref[...] = dh

        @pl.when(i == 0)
        def _():
            loss_ref[...] = jnp.zeros_like(loss_ref)
            dw_ref[...] = jnp.zeros_like(dw_ref)

        loss_ref[...] += val
        dw_ref[...] += dw

    return pl.pallas_call(
        body, name="loss_head", grid=(nb,),
        in_specs=[pl.BlockSpec((BLK, d), lambda i: (i, 0)), pl.BlockSpec((1, d), lambda i: (0, 0)),
                  pl.BlockSpec((BLK, d), lambda i: (jnp.maximum(i - 1, 0), 0))],
        out_specs=[pl.BlockSpec((1, 1), lambda i: (0, 0)), pl.BlockSpec((BLK, d), lambda i: (i, 0)),
                   pl.BlockSpec((1, d), lambda i: (0, 0))],
        out_shape=[jax.ShapeDtypeStruct((1, 1), F32), jax.ShapeDtypeStruct(h.shape, F32),
                   jax.ShapeDtypeStruct((1, d), F32)],
        compiler_params=_cp(("arbitrary",)),
    )(h, wf, target)


def _make_loss_head(target):
    @jax.custom_vjp
    def head(h, wf):
        return _loss_call(h, wf, target)[0][0, 0]

    def fwd(h, wf):
        loss, dh, dw = _loss_call(h, wf, target)
        return loss[0, 0], (dh, dw)

    def bwd(res, g):
        return g * res[0], g * res[1]

    head.defvjp(fwd, bwd)
    return head


_IN_SEGS = (("q", 0, 1024), ("k", 1024, 1024), ("v", 2048, 1024), ("gate", 3072, 1024), ("z", 4112, 1024),
            ("xbc", 5136, 2048), ("cq", 7200, 1024), ("ck", 8224, 256), ("cv", 8480, 256), ("gl", 8736, 3072),
            ("b", 4096, 8), ("a", 4104, 8), ("dt", 7184, 16))
_IN_PAD = sum(_SPLIT) - sum(n for _, _, n in _IN_SEGS)


_MATMUL = ("w_in", "w_proj_gdn", "w_proj_ssd", "w_proj_swa", "w_out", "w_up", "w_down")
_LATE = _MATMUL[1:]


def _late_weights(gathered):
    g = dict(zip(_LATE, gathered))
    full = {n: g[n].reshape(D_MODEL, D_MODEL) for n in _LATE[:4]}
    full["w_up"] = g["w_up"].transpose(1, 0, 2).reshape(D_MODEL, D_FF)
    full["w_down"] = g["w_down"].reshape(D_FF, D_MODEL)
    full.update({n + "_t": t.T for n, t in list(full.items())})
    return full


def _layer(h, p, w_in, w_in_t, slot, late_shards, next_shards=(), exchange_slots=()):
    wb = {"w_in": w_in, "w_in_t": w_in_t}

    def proj(t, name):
        return mm(t, wb[name], wb[name + "_t"], slot[name])

    q_pre, k_pre, v_pre, gate, z, xbc_pre, cq, ckv, gl, small = split_cols(
        proj(rms_op(h, p["norm1_w"].reshape(1, -1)), "w_in"))

    gcw = p["gdn_conv_w"]
    nob = jnp.zeros((1, GDN_H * GDN_D), F32)
    qa = conv_silu(q_pre, gcw[:, :1024], nob)
    ka = conv_silu(k_pre, gcw[:, 1024:2048], nob)
    va = conv_silu(v_pre, gcw[:, 2048:], nob)
    y_gdn, late, placeholders = gdn_core(
        qa, ka, va, gate, small, p["gdn_a_log"].reshape(GDN_H, 1, 1), p["gdn_dt_bias"].reshape(GDN_H, 1, 1),
        p["gdn_norm_w"].reshape(1, GDN_D), tuple(late_shards), tuple(exchange_slots))
    wb.update(_late_weights(late))

    xbc = conv_silu(xbc_pre, p["ssd_conv_w"], p["ssd_conv_b"].reshape(1, -1))
    y_ssd, gathered = ssd_core(xbc, z, small, p["ssd_dt_bias"].reshape(SSD_H, 1, 1),
                               p["ssd_a_log"].reshape(SSD_H, 1, 1), p["ssd_d"].reshape(SSD_H, 1, 1),
                               p["ssd_norm_w"].reshape(SSD_H, 1, SSD_P), tuple(next_shards))

    y_swa = swa_core(cq, ckv, p["swa_sinks"].reshape(SWA_QH, 1, 1))

    merged = merge_op(proj(y_gdn, "w_proj_gdn"), proj(y_ssd, "w_proj_ssd"), proj(y_swa, "w_proj_swa"), gl)
    h = h + proj(merged, "w_out")
    a1 = proj(rms_op(h, p["norm2_w"].reshape(1, -1)), "w_up")
    return h + proj(relu2_op(a1), "w_down"), gathered, placeholders


_PER_LAYER = ("norm1_w", "gdn_conv_w", "gdn_a_log", "gdn_dt_bias", "gdn_norm_w", "ssd_conv_w", "ssd_conv_b",
              "ssd_dt_bias", "ssd_a_log", "ssd_d", "ssd_norm_w", "swa_sinks", "norm2_w")


def _embed(x, meta):
    return jnp.concatenate([jnp.zeros((NPAD, D_MODEL), F32), meta, x], axis=0)


_IN_SHARD = 1476


def _in_pieces():
    out = []
    for _, s, n in _IN_SEGS:
        c = s
        while c < s + n:
            d = c // _IN_SHARD
            e = min(s + n, (d + 1) * _IN_SHARD)
            out.append((d, c - d * _IN_SHARD, e - d * _IN_SHARD))
            c = e
    return out


def _in_pieces_back():
    start, off = {}, 0
    for _, s, n in _IN_SEGS:
        start[s] = off
        off += n
    out = [[] for _ in range(N_DEV)]
    for _, s, n in sorted(_IN_SEGS, key=lambda t: t[1]):
        c = s
        while c < s + n:
            d = c // _IN_SHARD
            e = min(s + n, (d + 1) * _IN_SHARD)
            out[d].append((start[s] + c - s, start[s] + e - s))
            c = e
    return out


def _regroup_w_in(stacked):
    parts = [stacked[d, :, lo:hi] for d, lo, hi in _in_pieces()]
    return jnp.concatenate(parts + [jnp.zeros((D_MODEL, _IN_PAD), stacked.dtype)], axis=1)


def _ungroup_w_in(g):
    return [jnp.concatenate([g[:, lo:hi] for lo, hi in pieces], axis=1) for pieces in _in_pieces_back()]


def _position():
    return lax.axis_index("x"), lax.axis_index("y"), lax.axis_index("c")


_ANY = pl.BlockSpec(memory_space=pl.ANY)


def _chip_of(x, y, k):
    return (1 - x if k & 1 else x, 1 - y if k & 2 else y)


def _allgather_call(shards, name):
    n = len(shards)

    def body(*refs):
        start, relay, finish = _gather_phases(refs[:n], refs[n:2 * n], *refs[2 * n:])
        start()
        relay()
        finish()

    return pl.pallas_call(
        body, name=name,
        out_shape=_gather_out_shapes(shards),
        in_specs=[_ANY] * n, out_specs=[_ANY] * n,
        scratch_shapes=_gather_sems(n),
    )(*shards)


def _gather_out_shapes(shards):
    return [jax.ShapeDtypeStruct((N_DEV, *s.shape), s.dtype) for s in shards]


def _gather_sems(n):
    return [pltpu.SemaphoreType.DMA((7 * n,)), pltpu.SemaphoreType.DMA((7 * n,)), pltpu.SemaphoreType.DMA((n,))]


def _gather_phases(x_refs, out_refs, send_sems, recv_sems, local_sems):
    n = len(x_refs)
    x, y, c = _position()
    me, sibling = (x, y, c), (x, y, 1 - c)
    chips = [_chip_of(x, y, k) for k in (1, 2, 3)]

    def slab(a, px, py, pc):
        return out_refs[a].at[4 * px + 2 * py + pc]

    def copy(a, k, block, to, src=None):
        return pltpu.make_async_remote_copy(
            src_ref=slab(a, *block) if src is None else src, dst_ref=slab(a, *block),
            send_sem=send_sems.at[7 * a + k], recv_sem=recv_sems.at[7 * a + k], device_id=to, device_id_type=MESH)

    def mine():
        return [pltpu.make_async_copy(x_refs[a], slab(a, *me), local_sems.at[a]) for a in range(n)]

    def first():
        out = []
        for a in range(n):
            out.append(copy(a, 0, me, sibling, src=x_refs[a]))
            out += [copy(a, 1 + j, me, (*chip, c), src=x_refs[a]) for j, chip in enumerate(chips)]
        return out

    def passed():
        return [copy(a, 4 + j, (*chip, c), sibling) for j, chip in enumerate(chips) for a in range(n)]

    def start():
        for cp in mine() + first():
            cp.start()

    def relay():
        for j, chip in enumerate(chips):
            for a in range(n):
                copy(a, 1 + j, (*chip, c), me).wait_recv()
                copy(a, 4 + j, (*chip, c), sibling).start()

    def finish():
        for a in range(n):
            copy(a, 0, sibling, me).wait_recv()
        for j, chip in enumerate(chips):
            for a in range(n):
                copy(a, 4 + j, (*chip, 1 - c), me).wait_recv()
        for cp in first() + passed():
            cp.wait_send()
        for cp in mine():
            cp.wait()

    return start, relay, finish


def _sibling_exchange_call(for_c0, for_c1, name):
    n = len(for_c0)

    def body(*refs):
        c0_refs, c1_refs, out_refs = refs[:n], refs[n:2 * n], refs[2 * n:3 * n]
        send_sems, recv_sems = refs[3 * n:]
        x, y, c = _position()

        def copies(src_refs):
            return [pltpu.make_async_remote_copy(
                src_ref=src_refs[a].at[q], dst_ref=out_refs[a].at[q],
                send_sem=send_sems.at[4 * a + q], recv_sem=recv_sems.at[4 * a + q],
                device_id=(x, y, 1 - c), device_id_type=MESH) for a in range(n) for q in range(4)]

        @pl.when(c == 0)
        def _():
            for cp in copies(c1_refs):
                cp.start()

        @pl.when(c == 1)
        def _():
            for cp in copies(c0_refs):
                cp.start()

        waits = copies(c0_refs)
        for cp in waits:
            cp.wait_recv()
        for cp in waits:
            cp.wait_send()

    return pl.pallas_call(
        body, name=name,
        out_shape=[jax.ShapeDtypeStruct(g.shape, g.dtype) for g in for_c0],
        in_specs=[_ANY] * (2 * n), out_specs=[_ANY] * n,
        scratch_shapes=[pltpu.SemaphoreType.DMA((4 * n,)), pltpu.SemaphoreType.DMA((4 * n,))],
    )(*for_c0, *for_c1)


def _chip_exchange_call(partials, name):
    n = len(partials)

    def body(*refs):
        start, finish = _chip_exchange_phases(refs[:n], refs[n:2 * n], *refs[2 * n:])
        start()
        finish()

    return pl.pallas_call(
        body, name=name,
        out_shape=_chip_exchange_out_shapes(partials),
        in_specs=[_ANY] * n, out_specs=[_ANY] * n,
        scratch_shapes=_chip_exchange_sems(n),
    )(*partials)


def _chip_exchange_out_shapes(partials):
    return [jax.ShapeDtypeStruct((3, *p.shape[1:]), p.dtype) for p in partials]


def _chip_exchange_sems(n):
    return [pltpu.SemaphoreType.DMA((3 * n,)), pltpu.SemaphoreType.DMA((3 * n,))]


def _chip_exchange_phases(p_refs, out_refs, send_sems, recv_sems):
    n = len(p_refs)
    x, y, c = _position()

    def copies():
        out = []
        for a in range(n):
            for k in (1, 2, 3):
                px, py = _chip_of(x, y, k)
                out.append(pltpu.make_async_remote_copy(
                    src_ref=p_refs[a].at[2 * px + py], dst_ref=out_refs[a].at[k - 1],
                    send_sem=send_sems.at[3 * a + k - 1], recv_sem=recv_sems.at[3 * a + k - 1],
                    device_id=(px, py, c), device_id_type=MESH))
        return out

    def start():
        for cp in copies():
            cp.start()

    def finish():
        for cp in copies():
            cp.wait_recv()
        for cp in copies():
            cp.wait_send()

    return start, finish


def _chip_partial_call(for_c0, for_c1, sib, tr, name):
    _, r, c = sib.shape

    def body(c0_ref, c1_ref, s_ref, own_ref, out_ref):
        x, y, core = _position()
        mine = jnp.where(core == 0, c0_ref[...], c1_ref[...])
        partial = mine + s_ref[...]
        own = jnp.zeros((tr, c), F32)
        for q in range(4):
            own = jnp.where(2 * x + y == q, partial[q], own)
        own_ref[...] = own
        out_ref[...] = partial.astype(BF16)

    four = pl.BlockSpec((4, tr, c), lambda i: (0, i, 0))
    return pl.pallas_call(
        body, name=name, grid=(r // tr,),
        in_specs=[four, four, four],
        out_specs=[pl.BlockSpec((tr, c), lambda i: (i, 0)), four],
        out_shape=[jax.ShapeDtypeStruct((r, c), F32), jax.ShapeDtypeStruct((4, r, c), BF16)],
        compiler_params=_cp(("parallel",)),
    )(for_c0, for_c1, sib)


def _adamw_call(parts, w, m, v, tr, name):
    ns, r, c = w.shape
    counts = [len(p) for p in parts]
    flat_parts = [a for p in parts for a in p]

    def body(*refs):
        p_refs = refs[:len(flat_parts)]
        w_ref, m_ref, v_ref, g_ref, d_ref, nm_ref, nv_ref = refs[len(flat_parts):]
        at = 0
        for s in range(ns):
            g = None
            for p_ref in p_refs[at:at + counts[s]]:
                for j in range(p_ref.shape[0]):
                    term = p_ref[j].astype(F32)
                    g = term if g is None else g + term
            at += counts[s]
            nm = ADAM_B1 * m_ref[s] + (1.0 - ADAM_B1) * g
            nv = ADAM_B2 * v_ref[s] + (1.0 - ADAM_B2) * (g * g)
            m_hat = nm / (1.0 - ADAM_B1 ** ADAM_STEP)
            v_hat = nv / (1.0 - ADAM_B2 ** ADAM_STEP)
            g_ref[s] = g
            d_ref[s] = -ADAM_LR * (m_hat / (jnp.sqrt(v_hat) + ADAM_EPS) + ADAM_WD * w_ref[s])
            nm_ref[s] = nm
            nv_ref[s] = nv

    slabs = pl.BlockSpec((ns, tr, c), lambda i: (0, i, 0))
    return pl.pallas_call(
        body, name=name, grid=(r // tr,),
        in_specs=[pl.BlockSpec((a.shape[0], tr, c), lambda i: (0, i, 0)) for a in flat_parts] + [slabs] * 3,
        out_specs=[slabs] * 4,
        out_shape=[jax.ShapeDtypeStruct((ns, r, c), F32)] * 4,
        compiler_params=_cp(("parallel",)),
    )(*flat_parts, w, m, v)


_WEIGHTS = ("meta_tokens", "norm1_w", "w_in", "gdn_conv_w", "gdn_a_log", "gdn_dt_bias", "gdn_norm_w", "ssd_conv_w",
            "ssd_conv_b", "ssd_dt_bias", "ssd_a_log", "ssd_d", "ssd_norm_w", "swa_sinks", "w_proj_gdn", "w_proj_ssd",
            "w_proj_swa", "w_out", "norm2_w", "w_up", "w_down", "final_norm_w")
_SHARD_AXIS = {"meta_tokens": 1, "w_in": 2, "gdn_conv_w": 2, "ssd_conv_w": 2, "w_proj_gdn": 1, "w_proj_ssd": 1,
               "w_proj_swa": 1, "w_out": 1, "w_up": 2, "w_down": 1}
_BIG = tuple(n for n in _WEIGHTS if n in _SHARD_AXIS)
_SMALL = tuple(n for n in _WEIGHTS if n not in _SHARD_AXIS)
FLAT_C = 1024


def _pack(arrs, rows, lead=()):
    flat = jnp.concatenate([a.reshape(*lead, -1) for a in arrs], axis=-1)
    pad = rows * FLAT_C - flat.shape[-1]
    flat = jnp.pad(flat, [(0, 0)] * len(lead) + [(0, pad)])
    return flat.reshape(*lead, rows, FLAT_C)


def _unpack(flat, shapes, lead=()):
    flat = flat.reshape(*lead, -1)
    out, off = [], 0
    for s in shapes:
        n = math.prod(s)
        out.append(flat[..., off:off + n].reshape(*lead, *s))
        off += n
    return out


def _rows_for(shapes):
    n = sum(math.prod(s) for s in shapes)
    return -(-n // (FLAT_C * 8)) * 8


def _rows_tile(r, c):
    if r <= 256:
        return r
    return 128 if c > 1024 else 256


def _join(stacked, axis):
    moved = jnp.moveaxis(stacked, 0, axis)
    return moved.reshape(*moved.shape[:axis], -1, *moved.shape[axis + 2:])


def _unjoin(full, axis):
    cut = full.reshape(*full.shape[:axis], N_DEV, full.shape[axis] // N_DEV, *full.shape[axis + 1:])
    return jnp.moveaxis(cut, axis, 0)


def kernel(x, meta_tokens, norm1_w, w_in, gdn_conv_w, gdn_a_log, gdn_dt_bias, gdn_norm_w, ssd_conv_w, ssd_conv_b,
           ssd_dt_bias, ssd_a_log, ssd_d, ssd_norm_w, swa_sinks, w_proj_gdn, w_proj_ssd, w_proj_swa, w_out, norm2_w,
           w_up, w_down, final_norm_w, loss_target, m_meta_tokens, m_norm1_w, m_w_in, m_gdn_conv_w, m_gdn_a_log,
           m_gdn_dt_bias, m_gdn_norm_w, m_ssd_conv_w, m_ssd_conv_b, m_ssd_dt_bias, m_ssd_a_log, m_ssd_d, m_ssd_norm_w,
           m_swa_sinks, m_w_proj_gdn, m_w_proj_ssd, m_w_proj_swa, m_w_out, m_norm2_w, m_w_up, m_w_down,
           m_final_norm_w, v_meta_tokens, v_norm1_w, v_w_in, v_gdn_conv_w, v_gdn_a_log, v_gdn_dt_bias, v_gdn_norm_w,
           v_ssd_conv_w, v_ssd_conv_b, v_ssd_dt_bias, v_ssd_a_log, v_ssd_d, v_ssd_norm_w, v_swa_sinks, v_w_proj_gdn,
           v_w_proj_ssd, v_w_proj_swa, v_w_out, v_norm2_w, v_w_up, v_w_down, v_final_norm_w):
    args = (meta_tokens, norm1_w, w_in, gdn_conv_w, gdn_a_log, gdn_dt_bias, gdn_norm_w, ssd_conv_w, ssd_conv_b,
            ssd_dt_bias, ssd_a_log, ssd_d, ssd_norm_w, swa_sinks, w_proj_gdn, w_proj_ssd, w_proj_swa, w_out, norm2_w,
            w_up, w_down, final_norm_w, m_meta_tokens, m_norm1_w, m_w_in, m_gdn_conv_w, m_gdn_a_log,
            m_gdn_dt_bias, m_gdn_norm_w, m_ssd_conv_w, m_ssd_conv_b, m_ssd_dt_bias, m_ssd_a_log, m_ssd_d, m_ssd_norm_w,
            m_swa_sinks, m_w_proj_gdn, m_w_proj_ssd, m_w_proj_swa, m_w_out, m_norm2_w, m_w_up, m_w_down,
            m_final_norm_w, v_meta_tokens, v_norm1_w, v_w_in, v_gdn_conv_w, v_gdn_a_log, v_gdn_dt_bias, v_gdn_norm_w,
            v_ssd_conv_w, v_ssd_conv_b, v_ssd_dt_bias, v_ssd_a_log, v_ssd_d, v_ssd_norm_w, v_swa_sinks, v_w_proj_gdn,
            v_w_proj_ssd, v_w_proj_swa, v_w_out, v_norm2_w, v_w_up, v_w_down, v_final_norm_w)
    nw = len(_WEIGHTS)
    w = dict(zip(_WEIGHTS, args[:nw]))
    m = dict(zip(_WEIGHTS, args[nw:2 * nw]))
    v = dict(zip(_WEIGHTS, args[2 * nw:]))

    depth = w["w_in"].shape[0]
    small_shapes = [w[n].shape for n in _SMALL]
    small_rows = _rows_for(small_shapes)

    def flat2(t):
        return t.reshape(-1, t.shape[-1])

    tiny_names = [n for n in _BIG if n not in _MATMUL]

    def shard(n, l):
        return w[n][l].astype(BF16)

    first = _allgather_call([shard("w_in", 0)] + [flat2(w[n]) for n in tiny_names], "gather_weights")
    w_in_stacked = first[0]
    joined = {n: _join(t.reshape(N_DEV, *w[n].shape), _SHARD_AXIS[n]) for n, t in zip(tiny_names, first[1:])}
    slot_shapes = {"w_in": (D_MODEL, sum(_SPLIT)), "w_up": (D_MODEL, D_FF), "w_down": (D_FF, D_MODEL)}
    slot_shapes.update({n: (D_MODEL, D_MODEL) for n in _LATE[:4]})

    def layer_fn(l, w_in_full, late_shards, next_shards):
        w_in_t = w_in_full.T
        if l == 0:
            def fn(x_rows, meta, p, slot, exchange_slots):
                out, g, placeholders = _layer(_embed(x_rows, meta), p, w_in_full, w_in_t, slot, late_shards,
                                              next_shards, exchange_slots)
                return (out, placeholders), g
        else:
            def fn(h_in, p, slot, exchange_slots):
                out, g, placeholders = _layer(h_in, p, w_in_full, w_in_t, slot, late_shards, next_shards,
                                              exchange_slots)
                return (out, placeholders), g
        return fn

    h, vjps = None, []
    for l in range(depth):
        slot = {n: jnp.zeros(s, F32) for n, s in slot_shapes.items()}
        p = {n: (joined[n][l] if n in joined else w[n][l]) for n in _PER_LAYER}
        more = l + 1 < depth
        next_shards = [shard("w_in", l + 1)] if more else []
        exchange_slots = tuple(jnp.zeros((3, *w[n][l + 1].shape), BF16) for n in _MATMUL) if more else ()
        lead = (x[0], joined["meta_tokens"]) if l == 0 else (h,)
        fn = layer_fn(l, _regroup_w_in(w_in_stacked), [shard(n, l) for n in _LATE], next_shards)
        (h, _), vjp, g_next = jax.vjp(fn, *lead, p, slot, exchange_slots, has_aux=True)
        if more:
            w_in_stacked = g_next[0]
        vjps.append(vjp)
    loss, head_vjp = jax.vjp(_make_loss_head(loss_target[0]), h, w["final_norm_w"].reshape(1, -1))
    dh, d_final = head_vjp(jnp.ones((), F32))
    loss = lax.psum(loss, ("x", "y", "c"))

    def by_core(name, g):
        if name == "w_in":
            shards = _ungroup_w_in(g)
            return jnp.stack(shards[0::2]), jnp.stack(shards[1::2])
        if name == "w_up":
            t = g.reshape(D_MODEL, 4, 2, D_FF // N_DEV)
            return t[:, :, 0].transpose(1, 0, 2), t[:, :, 1].transpose(1, 0, 2)
        t = g.reshape(4, 2, -1, g.shape[-1])
        return t[:, 0], t[:, 1]

    own, incoming, layer_grads, outgoing = {}, {}, [None] * depth, ()
    for l in reversed(range(depth)):
        if l == 0:
            gx, d_meta, dp, dslot, arrived = vjps[0]((dh, tuple(outgoing)))
        else:
            dh, dp, dslot, arrived = vjps[l]((dh, tuple(outgoing)))
        incoming.update({(n, l + 1): t for n, t in zip(_MATMUL, arrived)})
        layer_grads[l] = dp
        todo = [((n, l), dslot[n]) for n in _MATMUL]
        if l == 0:
            full_grads = {"meta_tokens": d_meta}
            full_grads.update({n: jnp.stack([layer_grads[k][n] for k in range(depth)]) for n in tiny_names[1:]})
            todo += [((n, None), _unjoin(full_grads[n], _SHARD_AXIS[n]).reshape(N_DEV, -1, w[n].shape[-1]))
                     for n in tiny_names]
        pairs = [by_core(u[0], g) for u, g in todo]
        from_sibling = _sibling_exchange_call([a for a, _ in pairs], [b for _, b in pairs], "grads_to_sibling_%d" % l)
        outgoing = []
        for (u, _), (a0, a1), s in zip(todo, pairs, from_sibling):
            own[u], part = _chip_partial_call(a0, a1, s, _rows_tile(s.shape[1], s.shape[2]), "chip_partial_" + u[0])
            outgoing.append(part)
        if l == 0:
            incoming.update(zip([u for u, _ in todo], _chip_exchange_call(outgoing, "grads_to_chips")))

    g_small = {n: jnp.stack([layer_grads[k][n] for k in range(depth)]) for n in _SMALL if n != "final_norm_w"}
    g_small["final_norm_w"] = d_final.reshape(-1)

    by_name = {}
    for n in _BIG:
        layers = list(range(depth)) if n in _MATMUL else [None]
        parts = [[own[n, l][None], incoming[n, l]] for l in layers]
        r, c = own[n, layers[0]].shape
        stacked = [d[n].reshape(len(layers), r, c) for d in (w, m, v)]
        res = _adamw_call(parts, *stacked, _rows_tile(r, c), "adamw_" + n)
        by_name[n] = [t.reshape(w[n].shape) for t in res]

    small_parts = _allgather_call([_pack([g_small[n] for n in _SMALL], small_rows)], "gather_small_grads")
    small_out = _adamw_call([small_parts], *[_pack([d[n] for n in _SMALL], small_rows)[None] for d in (w, m, v)],
                            small_rows, "adamw_replicated")
    for kind in range(4):
        for n, t in zip(_SMALL, _unpack(small_out[kind][0], small_shapes)):
            by_name.setdefault(n, [None] * 4)[kind] = t

    outs = [by_name[n][kind] for kind in range(4) for n in _WEIGHTS]
    return (loss, gx[None], *outs)
```

```python
import functools
import math

import jax
import jax.numpy as jnp
from jax import lax
from jax.experimental import pallas as pl
from jax.experimental.pallas import tpu as pltpu

F32 = jnp.float32
BF16 = jnp.bfloat16
HI = lax.Precision.HIGH
NEG = -1e30

D_MODEL = 1024
N_META = 16
BLK = 128
NPAD = BLK - N_META
RMS_EPS = 1e-6
L2_EPS = 1e-6
CONV_K = 4

GDN_H, GDN_D, GDN_C = 8, 128, 64
SSD_H, SSD_P, SSD_G, SSD_N = 16, 64, 4, 128
SSD_HPG = SSD_H // SSD_G
SWA_QH, SWA_KVH, SWA_D = 16, 4, 64
SWA_REP = SWA_QH // SWA_KVH
D_FF = 4 * D_MODEL

N_DEV = 8
MESH = pl.DeviceIdType.MESH

ADAM_LR, ADAM_B1, ADAM_B2, ADAM_EPS, ADAM_WD, ADAM_STEP = 0.001, 0.9, 0.999, 1e-08, 0.01, 10

VMEM_LIMIT = 56 * 1024 * 1024


def _cp(sem=None):
    return pltpu.CompilerParams(dimension_semantics=sem, vmem_limit_bytes=VMEM_LIMIT)


def _dot(a, b, ca, cb, prec=HI):
    return lax.dot_general(a, b, (((ca,), (cb,)), ((), ())), precision=prec, preferred_element_type=F32)


def _nn(a, b, prec=HI):
    return _dot(a, b, 1, 0, prec)


def _nt(a, b, prec=HI):
    return _dot(a, b, 1, 1, prec)


def _tn(a, b, prec=HI):
    return _dot(a, b, 0, 0, prec)


def _bdot(a, b, ca, cb):
    return lax.dot_general(a.astype(BF16), b.astype(BF16), (((ca,), (cb,)), ((), ())), preferred_element_type=F32)


@jax.custom_vjp
def _lo_nn(a, b):
    return _bdot(a, b, 1, 0)


_lo_nn.defvjp(lambda a, b: (_bdot(a, b, 1, 0), (a, b)),
              lambda r, d: (_bdot(d, r[1], 1, 1), _bdot(r[0], d, 0, 0)))


@jax.custom_vjp
def _lo_nt(a, b):
    return _bdot(a, b, 1, 1)


_lo_nt.defvjp(lambda a, b: (_bdot(a, b, 1, 1), (a, b)),
              lambda r, d: (_bdot(d, r[1], 1, 0), _bdot(d, r[0], 0, 0)))


@jax.custom_vjp
def _lo_tn(a, b):
    return _bdot(a, b, 0, 0)


_lo_tn.defvjp(lambda a, b: (_bdot(a, b, 0, 0), (a, b)),
              lambda r, d: (_bdot(r[1], d, 1, 1), _bdot(r[0], d, 1, 0)))


def _iota2(n, m, axis):
    return lax.broadcasted_iota(jnp.int32, (n, m), axis)


def _silu(x):
    return x * jax.nn.sigmoid(x)


def _softplus(x):
    return jnp.maximum(x, 0.0) + jnp.log(1.0 + jnp.exp(-jnp.abs(x)))


def _row_of(col):
    n = col.shape[0]
    return jnp.broadcast_to(col, (n, n)).T


def _cumsum_col(col):
    n = col.shape[0]
    tril = (_iota2(n, n, 0) >= _iota2(n, n, 1)).astype(F32)
    return _nn(tril, col)


def _tri_inv(a):
    n = a.shape[0]
    r, c = _iota2(n, n, 0), _iota2(n, n, 1)
    eye = (r == c).astype(F32)
    blk = jnp.right_shift(r, 4) == jnp.right_shift(c, 4)
    d = jnp.where(blk, a, 0.0)
    off = a - d
    d2 = _nn(d, d)
    d4 = _nn(d2, d2)
    d8 = _nn(d4, d4)
    td = _nn(_nn(_nn(eye - d, eye + d2), eye + d4), eye + d8)
    m = _nn(td, off)
    m2 = _nn(m, m)
    return _nn(_nn(eye - m, eye + m2), td)


@jax.custom_vjp
def _tri_solve(a, inv, rhs):
    return _nn(inv, rhs)


def _tri_solve_fwd(a, inv, rhs):
    sol = _nn(inv, rhs)
    return sol, (inv, sol)


def _tri_solve_bwd(res, dsol):
    inv, sol = res
    drhs = _nn(inv.T, dsol)
    return -_nt(drhs, sol), jnp.zeros_like(inv), drhs


_tri_solve.defvjp(_tri_solve_fwd, _tri_solve_bwd)


def _gdn_chunk(qa, ka, va, gate, a_raw, b_raw, s, a_log, dt_bias, norm_w, valid, inv=None, want_inv=False):
    c = qa.shape[0]
    q = qa * lax.rsqrt(jnp.sum(qa * qa, axis=-1, keepdims=True) + L2_EPS) * (GDN_D ** -0.5)
    k = ka * lax.rsqrt(jnp.sum(ka * ka, axis=-1, keepdims=True) + L2_EPS)
    beta = jax.nn.sigmoid(b_raw)
    g = -jnp.exp(a_log) * _softplus(a_raw + dt_bias) * valid
    gam = _cumsum_col(g)
    gam_row = _row_of(gam)
    r, cc = _iota2(c, c, 0), _iota2(c, c, 1)
    decay = jnp.exp(jnp.where(r >= cc, gam - gam_row, NEG))
    kb = k * beta
    a = jnp.where(r > cc, _lo_nt(kb, k) * decay, 0.0)
    egam = jnp.exp(gam)
    if inv is None:
        inv = _tri_inv(lax.stop_gradient(a))
    sol = _tri_solve(a, inv, jnp.concatenate([va * beta, kb * egam], axis=1))
    u = sol[:, :GDN_D]
    w = sol[:, GDN_D:]
    attn = _lo_nt(q, k) * decay
    g_last = jnp.sum(g, axis=0, keepdims=True)
    k_tail = k * jnp.exp(g_last - gam)
    v_new = u - _lo_nn(w, s)
    o = _lo_nn(q * egam, s) + _lo_nn(attn, v_new)
    s_new = s * jnp.exp(g_last) + _lo_tn(k_tail, v_new)
    y = o * lax.rsqrt(jnp.mean(o * o, axis=-1, keepdims=True) + RMS_EPS) * norm_w * _silu(gate)
    return (y, s_new, inv) if want_inv else (y, s_new)


def _valid_col(row0, n):
    return (row0 + _iota2(n, 1, 0) >= NPAD).astype(F32)


GDN_HB = GDN_H

SM_B, SM_A, SM_DT, SM_W = 0, 8, 16, 128


def _pick_cols(sm, first, n):
    return jnp.stack([sm[:, first + j:first + j + 1] for j in range(n)])


def _spread_cols(cols, first):
    lane = _iota2(1, SM_W, 1)
    out = None
    for j in range(cols.shape[0]):
        term = cols[j] * (lane == first + j).astype(F32)
        out = term if out is None else out + term
    return out


def _widen(t, width):
    if width == t.shape[1]:
        return t
    return jnp.concatenate([t, jnp.zeros((t.shape[0], width - t.shape[1]), t.dtype)], axis=1)


def _gdn_specs(nc, rev):
    ci = (lambda i: nc - 1 - i) if rev else (lambda i: i)
    hb = GDN_HB
    tile = pl.BlockSpec((GDN_C, hb * GDN_D), lambda h, i: (ci(i), h))
    col = pl.BlockSpec((GDN_C, SM_W), lambda h, i: (ci(i), 0))
    scal = pl.BlockSpec((hb, 1, 1), lambda h, i: (h, 0, 0))
    nw = pl.BlockSpec((1, GDN_D), lambda h, i: (0, 0))
    st = pl.BlockSpec((hb, 1, GDN_D, GDN_D), lambda h, i: (h, ci(i), 0, 0))
    return tile, col, scal, nw, st


def _lanes(j):
    return slice(j * GDN_D, (j + 1) * GDN_D)


def _by_head(ref):
    return jnp.stack([ref[:, _lanes(j)] for j in range(GDN_HB)])


def _gdn_fwd_call(q, k, v, gate, small, a_log, dt_bias, norm_w, shards=()):
    seq = q.shape[0]
    nc = seq // GDN_C
    ns = len(shards)
    tile, col, scal, nw, st = _gdn_specs(nc, False)

    def body(*refs):
        q_ref, k_ref, v_ref, g_ref, sm_ref, al_ref, dt_ref, nw_ref = refs[:8]
        y_ref, st_ref, inv_ref = refs[8 + ns:11 + ns]
        s_scr = refs[11 + 2 * ns]
        i = pl.program_id(1)
        if ns:
            start, relay, finish = _gather_phases(refs[8:8 + ns], refs[11 + ns:11 + 2 * ns], *refs[12 + 2 * ns:])
            pl.when(i == 0)(start)
            pl.when(i == nc - 1)(relay)

        @pl.when(i == 0)
        def _():
            s_scr[...] = jnp.zeros_like(s_scr)

        s = s_scr[...]
        st_ref[:, 0] = s
        sm = sm_ref[...]
        fn = jax.vmap(functools.partial(_gdn_chunk, valid=_valid_col(i * GDN_C, GDN_C), want_inv=True))
        y, s_new, inv = fn(_by_head(q_ref), _by_head(k_ref), _by_head(v_ref), _by_head(g_ref),
                           _pick_cols(sm, SM_A, GDN_H), _pick_cols(sm, SM_B, GDN_H), s,
                           al_ref[...], dt_ref[...], jnp.broadcast_to(nw_ref[...], (GDN_HB, 1, GDN_D)))
        for j in range(GDN_HB):
            y_ref[:, _lanes(j)] = y[j]
        inv_ref[:, 0] = inv
        s_scr[...] = s_new
        if ns:
            pl.when(i == nc - 1)(finish)

    return pl.pallas_call(
        body, name="gdn_fwd", grid=(GDN_H // GDN_HB, nc),
        in_specs=[tile, tile, tile, tile, col, scal, scal, nw] + [_ANY] * ns,
        out_specs=[tile, st, pl.BlockSpec((GDN_HB, 1, GDN_C, GDN_C), lambda h, i: (h, i, 0, 0))] + [_ANY] * ns,
        out_shape=[jax.ShapeDtypeStruct((seq, GDN_H * GDN_D), F32),
                   jax.ShapeDtypeStruct((GDN_H, nc, GDN_D, GDN_D), F32),
                   jax.ShapeDtypeStruct((GDN_H, nc, GDN_C, GDN_C), F32)] + _gather_out_shapes(shards),
        scratch_shapes=[pltpu.VMEM((GDN_HB, GDN_D, GDN_D), F32)] + (_gather_sems(ns) if ns else []),
        compiler_params=_cp(("parallel", "arbitrary")),
    )(q, k, v, gate, small, a_log, dt_bias, norm_w, *shards)


def _gdn_bwd_call(q, k, v, gate, small, a_log, dt_bias, norm_w, states, invs, dy, outgoing=()):
    seq = q.shape[0]
    nc = seq // GDN_C
    no = len(outgoing)
    tile, col, scal, nw, st = _gdn_specs(nc, True)
    nwh = pl.BlockSpec((GDN_HB, 1, GDN_D), lambda h, i: (h, 0, 0))
    inv_spec = pl.BlockSpec((GDN_HB, 1, GDN_C, GDN_C), lambda h, i: (h, nc - 1 - i, 0, 0))

    def body(*refs):
        q_ref, k_ref, v_ref, g_ref, sm_ref, al_ref, dt_ref, nw_ref, st_ref, inv_ref, dy_ref = refs[:11]
        dq_ref, dk_ref, dv_ref, dg_ref, dsm_ref, dal_ref, ddt_ref, dnw_ref = refs[11 + no:19 + no]
        ds_scr = refs[19 + 2 * no]
        i = pl.program_id(1)
        if no:
            start, finish = _chip_exchange_phases(refs[11:11 + no], refs[19 + no:19 + 2 * no], *refs[20 + 2 * no:])
            pl.when(i == 0)(start)

        @pl.when(i == 0)
        def _():
            ds_scr[...] = jnp.zeros_like(ds_scr)
            dal_ref[...] = jnp.zeros_like(dal_ref)
            ddt_ref[...] = jnp.zeros_like(ddt_ref)
            dnw_ref[...] = jnp.zeros_like(dnw_ref)

        sm = sm_ref[...]
        valid = _valid_col((nc - 1 - i) * GDN_C, GDN_C)
        kept = inv_ref[:, 0]

        def fn(*heads):
            return jax.vmap(lambda *t: _gdn_chunk(*t[:-1], valid=valid, inv=t[-1]))(*heads, kept)

        _, vjp = jax.vjp(fn, _by_head(q_ref), _by_head(k_ref), _by_head(v_ref), _by_head(g_ref),
                         _pick_cols(sm, SM_A, GDN_H), _pick_cols(sm, SM_B, GDN_H), st_ref[:, 0], al_ref[...],
                         dt_ref[...], jnp.broadcast_to(nw_ref[...], (GDN_HB, 1, GDN_D)))
        dq, dk, dv, dg, da, db, ds, dal, ddt, dnw = vjp((_by_head(dy_ref), ds_scr[...]))
        for j in range(GDN_HB):
            dq_ref[:, _lanes(j)] = dq[j]
            dk_ref[:, _lanes(j)] = dk[j]
            dv_ref[:, _lanes(j)] = dv[j]
            dg_ref[:, _lanes(j)] = dg[j]
        dsm_ref[...] = _widen(_spread_cols(da, SM_A) + _spread_cols(db, SM_B), dsm_ref.shape[1])
        ds_scr[...] = ds
        dal_ref[...] += dal
        ddt_ref[...] += ddt
        dnw_ref[...] += dnw
        if no:
            pl.when(i == nc - 1)(finish)

    big = jax.ShapeDtypeStruct((seq, GDN_H * GDN_D), F32)
    return pl.pallas_call(
        body, name="gdn_bwd", grid=(GDN_H // GDN_HB, nc),
        in_specs=[tile, tile, tile, tile, col, scal, scal, nw, st, inv_spec, tile] + [_ANY] * no,
        out_specs=[tile, tile, tile, tile, pl.BlockSpec((GDN_C, small.shape[1]), lambda h, i: (nc - 1 - i, 0)),
                   scal, scal, nwh] + [_ANY] * no,
        out_shape=[big, big, big, big, jax.ShapeDtypeStruct(small.shape, F32),
                   jax.ShapeDtypeStruct((GDN_H, 1, 1), F32), jax.ShapeDtypeStruct((GDN_H, 1, 1), F32),
                   jax.ShapeDtypeStruct((GDN_H, 1, GDN_D), F32)] + _chip_exchange_out_shapes(outgoing),
        scratch_shapes=[pltpu.VMEM((GDN_HB, GDN_D, GDN_D), F32)] + (_chip_exchange_sems(no) if no else []),
        compiler_params=_cp(("parallel", "arbitrary")),
    )(q, k, v, gate, small, a_log, dt_bias, norm_w, states, invs, dy, *outgoing)


@jax.custom_vjp
def gdn_core(q, k, v, gate, small, a_log, dt_bias, norm_w, shards, slots):
    y, _, _, *gathered = _gdn_fwd_call(q, k, v, gate, small, a_log, dt_bias, norm_w, shards)
    return y, tuple(gathered), tuple(jnp.zeros((4, *s.shape[1:]), s.dtype) for s in slots)


def _gdn_core_fwd(q, k, v, gate, small, a_log, dt_bias, norm_w, shards, slots):
    y, states, invs, *gathered = _gdn_fwd_call(q, k, v, gate, small, a_log, dt_bias, norm_w, shards)
    out = (y, tuple(gathered), tuple(jnp.zeros((4, *s.shape[1:]), s.dtype) for s in slots))
    return out, (q, k, v, gate, small, a_log, dt_bias, norm_w, states, invs, shards)


def _gdn_core_bwd(res, cts):
    *args, shards = res
    dy, _, outgoing = cts
    dq, dk, dv, dg, dsm, dal, ddt, dnw, *incoming = _gdn_bwd_call(*args, dy, outgoing)
    return (dq, dk, dv, dg, dsm, dal, ddt, jnp.sum(dnw, axis=0), tuple(jnp.zeros_like(s) for s in shards),
            tuple(incoming))


gdn_core.defvjp(_gdn_core_fwd, _gdn_core_bwd)


def _ssd_head(x, z, dt_raw, h, dt_bias, a_log, d_skip, bm, cm, cb, valid):
    c = bm.shape[0]
    r, cc = _iota2(c, c, 0), _iota2(c, c, 1)
    dtp = _softplus(dt_raw + dt_bias)
    x = x * valid
    adt = -jnp.exp(a_log) * dtp * valid
    xdt = x * dtp
    acum = _cumsum_col(adt)
    lmat = jnp.exp(jnp.where(r >= cc, acum - _row_of(acum), NEG))
    a_last = jnp.sum(adt, axis=0, keepdims=True)
    y = _lo_nn(cb * lmat, xdt) + _lo_nt(cm * jnp.exp(acum), h) + d_skip * x
    h_new = h * jnp.exp(a_last) + _lo_tn(xdt * jnp.exp(a_last - acum), bm)
    return y * _silu(z), h_new


SSD_SIDE = SSD_H


def _ssd_chunk(xs, z, bm, cm, dt_raw, h, dt_bias, a_log, d_skip, norm_w, valid):
    nh, c, p = xs.shape
    ng = bm.shape[0]
    hpg = nh // ng
    bm = bm * valid
    cm = cm * valid
    cb = jax.vmap(_lo_nt)(cm, bm)
    per_head = lambda t: jnp.repeat(t, hpg, axis=0)
    args = (xs, z, dt_raw, h, dt_bias, a_log, d_skip, per_head(bm), per_head(cm), per_head(cb))
    outs = [jax.vmap(functools.partial(_ssd_head, valid=valid))(*[t[s:s + SSD_SIDE] for t in args])
            for s in range(0, nh, SSD_SIDE)]
    ys = jnp.concatenate([o[0] for o in outs], axis=0)
    hs = jnp.concatenate([o[1] for o in outs], axis=0)
    ss = jnp.sum(jnp.sum(ys * ys, axis=-1, keepdims=True).reshape(ng, hpg, c, 1), axis=1, keepdims=True)
    rstd = lax.rsqrt(ss / (hpg * p) + RMS_EPS)
    return (ys.reshape(ng, hpg, c, p) * rstd).reshape(nh, c, p) * norm_w, hs


SSD_INNER = SSD_H * SSD_P
SSD_BC = SSD_G * SSD_N


def _split_lanes(t, n, w):
    return jnp.stack([t[:, j * w:(j + 1) * w] for j in range(n)])


def _join_lanes(t):
    return jnp.concatenate([t[j] for j in range(t.shape[0])], axis=1)


def _ssd_specs(nc, rev):
    ci = (lambda i: nc - 1 - i) if rev else (lambda i: i)
    wide = pl.BlockSpec((BLK, SSD_INNER), lambda i: (ci(i), 0))
    bmat = pl.BlockSpec((BLK, SSD_BC), lambda i: (ci(i), SSD_INNER // SSD_BC))
    cmat = pl.BlockSpec((BLK, SSD_BC), lambda i: (ci(i), SSD_INNER // SSD_BC + 1))
    xbc = pl.BlockSpec((BLK, SSD_INNER + 2 * SSD_BC), lambda i: (ci(i), 0))
    col = pl.BlockSpec((BLK, SM_W), lambda i: (ci(i), 0))
    scal = pl.BlockSpec((SSD_H, 1, 1), lambda i: (0, 0, 0))
    nw = pl.BlockSpec((SSD_H, 1, SSD_P), lambda i: (0, 0, 0))
    st = pl.BlockSpec((SSD_H, 1, SSD_P, SSD_N), lambda i: (0, ci(i), 0, 0))
    return wide, bmat, cmat, xbc, col, scal, nw, st


def _ssd_fwd_call(xbc, z, small, dt_bias, a_log, d_skip, norm_w, shards=()):
    seq = z.shape[0]
    nc = seq // BLK
    ns = len(shards)
    wide, bmat, cmat, _, col, scal, nw, st = _ssd_specs(nc, False)

    def body(*refs):
        x_ref, b_ref, c_ref, z_ref, sm_ref, db_ref, al_ref, ds_ref, nw_ref = refs[:9]
        y_ref, st_ref = refs[9 + ns:11 + ns]
        h_scr = refs[11 + 2 * ns]
        i = pl.program_id(0)
        if ns:
            start, relay, finish = _gather_phases(refs[9:9 + ns], refs[11 + ns:11 + 2 * ns], *refs[12 + 2 * ns:])
            pl.when(i == 0)(start)
            pl.when(i == nc - 1)(relay)

        @pl.when(i == 0)
        def _():
            h_scr[...] = jnp.zeros_like(h_scr)

        h = h_scr[...]
        st_ref[:, 0] = h
        y, h_new = _ssd_chunk(_split_lanes(x_ref[...], SSD_H, SSD_P), _split_lanes(z_ref[...], SSD_H, SSD_P),
                              _split_lanes(b_ref[...], SSD_G, SSD_N), _split_lanes(c_ref[...], SSD_G, SSD_N),
                              _pick_cols(sm_ref[...], SM_DT, SSD_H), h, db_ref[...], al_ref[...], ds_ref[...],
                              nw_ref[...], _valid_col(i * BLK, BLK))
        y_ref[...] = _join_lanes(y)
        h_scr[...] = h_new
        if ns:
            pl.when(i == nc - 1)(finish)

    return pl.pallas_call(
        body, name="ssd_fwd", grid=(nc,),
        in_specs=[wide, bmat, cmat, wide, col, scal, scal, scal, nw] + [_ANY] * ns,
        out_specs=[wide, st] + [_ANY] * ns,
        out_shape=[jax.ShapeDtypeStruct((seq, SSD_INNER), F32),
                   jax.ShapeDtypeStruct((SSD_H, nc, SSD_P, SSD_N), F32)] + _gather_out_shapes(shards),
        scratch_shapes=[pltpu.VMEM((SSD_H, SSD_P, SSD_N), F32)] + (_gather_sems(ns) if ns else []),
        compiler_params=_cp(("arbitrary",)),
    )(xbc, xbc, xbc, z, small, dt_bias, a_log, d_skip, norm_w, *shards)


def _ssd_bwd_call(xbc, z, small, dt_bias, a_log, d_skip, norm_w, states, dy):
    seq = z.shape[0]
    nc = seq // BLK
    wide, bmat, cmat, xbc_spec, col, scal, nw, st = _ssd_specs(nc, True)

    def body(x_ref, b_ref, c_ref, z_ref, sm_ref, db_ref, al_ref, ds_ref, nw_ref, st_ref, dy_ref,
             dxbc_ref, dz_ref, dsm_ref, ddb_ref, dal_ref, dds_ref, dnw_ref, dh_scr):
        i = pl.program_id(0)

        @pl.when(i == 0)
        def _():
            dh_scr[...] = jnp.zeros_like(dh_scr)
            ddb_ref[...] = jnp.zeros_like(ddb_ref)
            dal_ref[...] = jnp.zeros_like(dal_ref)
            dds_ref[...] = jnp.zeros_like(dds_ref)
            dnw_ref[...] = jnp.zeros_like(dnw_ref)

        fn = functools.partial(_ssd_chunk, valid=_valid_col((nc - 1 - i) * BLK, BLK))
        _, vjp = jax.vjp(fn, _split_lanes(x_ref[...], SSD_H, SSD_P), _split_lanes(z_ref[...], SSD_H, SSD_P),
                         _split_lanes(b_ref[...], SSD_G, SSD_N), _split_lanes(c_ref[...], SSD_G, SSD_N),
                         _pick_cols(sm_ref[...], SM_DT, SSD_H), st_ref[:, 0], db_ref[...], al_ref[...], ds_ref[...],
                         nw_ref[...])
        dx, dz, dbm, dcm, ddt, dh, ddb, dal, dds, dnw = vjp((_split_lanes(dy_ref[...], SSD_H, SSD_P), dh_scr[...]))
        dxbc_ref[:, :SSD_INNER] = _join_lanes(dx)
        dxbc_ref[:, SSD_INNER:SSD_INNER + SSD_BC] = _join_lanes(dbm)
        dxbc_ref[:, SSD_INNER + SSD_BC:] = _join_lanes(dcm)
        dz_ref[...] = _join_lanes(dz)
        dsm_ref[...] = _widen(_spread_cols(ddt, SM_DT), dsm_ref.shape[1])
        dh_scr[...] = dh
        ddb_ref[...] += ddb
        dal_ref[...] += dal
        dds_ref[...] += dds
        dnw_ref[...] += dnw

    sshape = jax.ShapeDtypeStruct((SSD_H, 1, 1), F32)
    return pl.pallas_call(
        body, name="ssd_bwd", grid=(nc,),
        in_specs=[wide, bmat, cmat, wide, col, scal, scal, scal, nw, st, wide],
        out_specs=[xbc_spec, wide, pl.BlockSpec((BLK, small.shape[1]), lambda i: (nc - 1 - i, 0)), scal, scal, scal, nw],
        out_shape=[jax.ShapeDtypeStruct(xbc.shape, F32), jax.ShapeDtypeStruct(z.shape, F32),
                   jax.ShapeDtypeStruct(small.shape, F32), sshape, sshape, sshape,
                   jax.ShapeDtypeStruct((SSD_H, 1, SSD_P), F32)],
        scratch_shapes=[pltpu.VMEM((SSD_H, SSD_P, SSD_N), F32)],
        compiler_params=_cp(("arbitrary",)),
    )(xbc, xbc, xbc, z, small, dt_bias, a_log, d_skip, norm_w, states, dy)


@jax.custom_vjp
def ssd_core(xbc, z, small, dt_bias, a_log, d_skip, norm_w, shards):
    y, _, *gathered = _ssd_fwd_call(xbc, z, small, dt_bias, a_log, d_skip, norm_w, shards)
    return y, tuple(gathered)


def _ssd_core_fwd(*args):
    y, states, *gathered = _ssd_fwd_call(*args)
    return (y, tuple(gathered)), (*args[:-1], states, args[-1])


def _ssd_core_bwd(res, cts):
    *args, shards = res
    return (*_ssd_bwd_call(*args, cts[0]), tuple(jnp.zeros_like(s) for s in shards))


ssd_core.defvjp(_ssd_core_fwd, _ssd_core_bwd)


def _swa_block(q, km, kp, kc, vm, vp, vc, sink, n):
    rows = SWA_REP * BLK
    qs = q.reshape(rows, SWA_D) * (SWA_D ** -0.5)
    s = _lo_nt(qs, jnp.concatenate([km, kp, kc], axis=0))
    i = jnp.bitwise_and(_iota2(rows, 3 * BLK, 0), BLK - 1)
    col = _iota2(rows, 3 * BLK, 1)
    j = jnp.bitwise_and(col, BLK - 1)
    part = jnp.right_shift(col, 7)
    ok_m = (part == 0) & (j >= NPAD) & ((n >= 1) | (j <= i))
    ok_p = (part == 1) & (n >= 2) & (j > i)
    ok_c = (part == 2) & (n >= 1) & (j <= i)
    ok = ok_m | ok_p | ok_c
    s = jnp.where(ok, s, NEG)
    snk = jnp.concatenate([jnp.broadcast_to(sink[r], (BLK, 1)) for r in range(SWA_REP)], axis=0)
    m = lax.stop_gradient(jnp.maximum(jnp.max(s, axis=-1, keepdims=True), snk))
    e = jnp.exp(s - m)
    p = e / (jnp.sum(e, axis=-1, keepdims=True) + jnp.exp(snk - m))
    o = _lo_nn(p, jnp.concatenate([vm, vp, vc], axis=0))
    return o.reshape(SWA_REP, BLK, SWA_D)


SWA_QW = SWA_QH * SWA_D
SWA_KW = SWA_KVH * SWA_D


def _swa_specs(nb, rev):
    ci = (lambda i: nb - 1 - i) if rev else (lambda i: i)
    qsp = pl.BlockSpec((BLK, SWA_QW), lambda i: (ci(i), 0))
    cur = pl.BlockSpec((BLK, 2 * SWA_KW), lambda i: (ci(i), 0))
    prev = pl.BlockSpec((BLK, 2 * SWA_KW), lambda i: (jnp.maximum(ci(i) - 1, 0), 0))
    meta = pl.BlockSpec((BLK, 2 * SWA_KW), lambda i: (0, 0))
    scal = pl.BlockSpec((SWA_QH, 1, 1), lambda i: (0, 0, 0))
    return qsp, cur, prev, meta, scal


def _swa_by_head(q, kvm, kvp, kvc, sink):
    def kv(t):
        return _split_lanes(t[:, :SWA_KW], SWA_KVH, SWA_D), _split_lanes(t[:, SWA_KW:], SWA_KVH, SWA_D)

    (km, vm), (kp, vp), (kc, vc) = kv(kvm), kv(kvp), kv(kvc)
    qh = _split_lanes(q, SWA_QH, SWA_D).reshape(SWA_KVH, SWA_REP, BLK, SWA_D)
    return qh, km, kp, kc, vm, vp, vc, sink.reshape(SWA_KVH, SWA_REP, 1, 1)


def _swa_kv_tile(dk, dv):
    return jnp.concatenate([_join_lanes(dk), _join_lanes(dv)], axis=1)


def _swa_fwd_call(q, kv, sink):
    seq = q.shape[0]
    nb = seq // BLK
    qsp, cur, prev, meta, scal = _swa_specs(nb, False)

    def body(q_ref, m_ref, p_ref, c_ref, s_ref, o_ref):
        fn = jax.vmap(functools.partial(_swa_block, n=pl.program_id(0)))
        o = fn(*_swa_by_head(q_ref[...], m_ref[...], p_ref[...], c_ref[...], s_ref[...]))
        o_ref[...] = _join_lanes(o.reshape(SWA_QH, BLK, SWA_D))

    return pl.pallas_call(
        body, name="swa_fwd", grid=(nb,),
        in_specs=[qsp, meta, prev, cur, scal],
        out_specs=qsp,
        out_shape=jax.ShapeDtypeStruct(q.shape, F32),
        compiler_params=_cp(("parallel",)),
    )(q, kv, kv, kv, sink)


def _swa_bwd_call(q, kv, sink, do):
    seq = q.shape[0]
    nb = seq // BLK
    qsp, cur, prev, meta, scal = _swa_specs(nb, True)

    def body(q_ref, m_ref, p_ref, c_ref, s_ref, do_ref, dq_ref, dkv_ref, ds_ref, prev_scr, meta_scr):
        i = pl.program_id(0)
        n = nb - 1 - i

        @pl.when(i == 0)
        def _():
            prev_scr[...] = jnp.zeros_like(prev_scr)
            meta_scr[...] = jnp.zeros_like(meta_scr)
            ds_ref[...] = jnp.zeros_like(ds_ref)

        fn = jax.vmap(functools.partial(_swa_block, n=n))
        _, vjp = jax.vjp(fn, *_swa_by_head(q_ref[...], m_ref[...], p_ref[...], c_ref[...], s_ref[...]))
        do = _split_lanes(do_ref[...], SWA_QH, SWA_D).reshape(SWA_KVH, SWA_REP, BLK, SWA_D)
        dq, dkm, dkp, dkc, dvm, dvp, dvc, dsk = vjp(do)
        dq_ref[...] = _join_lanes(dq.reshape(SWA_QH, BLK, SWA_D))
        ds_ref[...] += dsk.reshape(SWA_QH, 1, 1)
        meta_scr[...] += _swa_kv_tile(dkm, dvm)
        first = (n == 0).astype(F32)
        dkv_ref[...] = _swa_kv_tile(dkc, dvc) + prev_scr[...] + first * meta_scr[...]
        prev_scr[...] = _swa_kv_tile(dkp, dvp)

    return pl.pallas_call(
        body, name="swa_bwd", grid=(nb,),
        in_specs=[qsp, meta, prev, cur, scal, qsp],
        out_specs=[qsp, cur, scal],
        out_shape=[jax.ShapeDtypeStruct(q.shape, F32), jax.ShapeDtypeStruct(kv.shape, F32),
                   jax.ShapeDtypeStruct(sink.shape, F32)],
        scratch_shapes=[pltpu.VMEM((BLK, 2 * SWA_KW), F32)] * 2,
        compiler_params=_cp(("arbitrary",)),
    )(q, kv, kv, kv, sink, do)


@jax.custom_vjp
def swa_core(q, kv, sink):
    return _swa_fwd_call(q, kv, sink)


def _swa_core_fwd(q, kv, sink):
    return _swa_fwd_call(q, kv, sink), (q, kv, sink)


def _swa_core_bwd(res, do):
    return tuple(_swa_bwd_call(*res, do))


swa_core.defvjp(_swa_core_fwd, _swa_core_bwd)


def _tile(n, pref):
    if n <= pref:
        return n
    best = None
    for t in range(128, pref + 1, 128):
        if n % t == 0:
            best = t
    assert best is not None, (n, pref)
    return best


def _mm_tiles(m, n, kk):
    if kk > 8192:
        return _tile(m, 704), _tile(n, 512), _tile(kk, 4096)
    return _tile(m, 1408), _tile(n, 512), _tile(kk, 1408)


def _mm_call(a, b, name):
    (m, kk), n = a.shape, b.shape[1]
    tm, tn, tk = _mm_tiles(m, n, kk)
    nk = kk // tk
    a_spec = pl.BlockSpec((tm, tk), lambda i, j, k: (i, k))
    b_spec = pl.BlockSpec((tk, tn), lambda i, j, k: (k, j))

    def body(a_ref, b_ref, o_ref, acc_ref):
        k = pl.program_id(2)
        part = jnp.dot(a_ref[...].astype(BF16), b_ref[...].astype(BF16), preferred_element_type=F32)

        @pl.when(k == 0)
        def _():
            acc_ref[...] = part

        @pl.when(k > 0)
        def _():
            acc_ref[...] += part

        @pl.when(k == nk - 1)
        def _():
            o_ref[...] = acc_ref[...]

    return pl.pallas_call(
        body, name=name, grid=(m // tm, n // tn, nk),
        in_specs=[a_spec, b_spec],
        out_specs=pl.BlockSpec((tm, tn), lambda i, j, k: (i, j)),
        out_shape=jax.ShapeDtypeStruct((m, n), F32),
        scratch_shapes=[pltpu.VMEM((tm, tn), F32)],
        compiler_params=_cp(("parallel", "parallel", "arbitrary")),
    )(a, b)


@jax.custom_vjp
def mm(a, b, b_t, grad_slot):
    return _mm_call(a, b, "mm_fwd")


def _mm_fwd(a, b, b_t, grad_slot):
    return _mm_call(a, b, "mm_fwd"), (a, b, b_t)


def _mm_bwd(res, dc):
    a, b, b_t = res
    return (_mm_call(dc, b_t, "mm_dx"), jnp.zeros_like(b), jnp.zeros_like(b_t),
            _mm_call(a.astype(BF16).T, dc, "mm_dw"))


mm.defvjp(_mm_fwd, _mm_bwd)


_SPLIT = (1024, 1024, 1024, 1024, 1024, 2048, 1024, 512, 3072, 512)


@jax.custom_vjp
def split_cols(u):
    offs = [sum(_SPLIT[:i]) for i in range(len(_SPLIT))]
    return tuple(u[:, o:o + s] for o, s in zip(offs, _SPLIT))


def _split_fwd(u):
    return split_cols(u), None


def _split_bwd(_, cts):
    return (jnp.concatenate(cts, axis=1),)


split_cols.defvjp(_split_fwd, _split_bwd)


def _row_specs(arrs, tr):
    return [pl.BlockSpec((tr, a.shape[1]), lambda i: (i, 0)) for a in arrs]


def _par_specs(arrs):
    return [pl.BlockSpec(a.shape, lambda i: (0, 0)) for a in arrs]


def _row_fwd_call(fn, rows, params, out_cols, tr, name):
    seq = rows[0].shape[0]
    nr = len(rows)

    def body(*refs):
        vals = [r[...] for r in refs[:-1]]
        refs[-1][...] = fn(*vals)

    return pl.pallas_call(
        body, name=name, grid=(seq // tr,),
        in_specs=_row_specs(rows, tr) + _par_specs(params),
        out_specs=pl.BlockSpec((tr, out_cols), lambda i: (i, 0)),
        out_shape=jax.ShapeDtypeStruct((seq, out_cols), F32),
        compiler_params=_cp(("parallel",)),
    )(*rows, *params)


def _row_bwd_call(fn, rows, params, dy, tr, name):
    seq = rows[0].shape[0]
    nr, npar = len(rows), len(params)

    def body(*refs):
        ins = refs[:nr + npar]
        dy_ref = refs[nr + npar]
        outs = refs[nr + npar + 1:]
        _, vjp = jax.vjp(fn, *[r[...] for r in ins])
        cts = vjp(dy_ref[...])
        for o_ref, ct in zip(outs[:nr], cts[:nr]):
            o_ref[...] = ct

        @pl.when(pl.program_id(0) == 0)
        def _():
            for o_ref in outs[nr:]:
                o_ref[...] = jnp.zeros_like(o_ref)

        for o_ref, ct in zip(outs[nr:], cts[nr:]):
            o_ref[...] += ct

    return pl.pallas_call(
        body, name=name, grid=(seq // tr,),
        in_specs=_row_specs(rows, tr) + _par_specs(params) + _row_specs([dy], tr),
        out_specs=_row_specs(rows, tr) + _par_specs(params),
        out_shape=[jax.ShapeDtypeStruct(a.shape, F32) for a in (*rows, *params)],
        compiler_params=_cp(("arbitrary",)),
    )(*rows, *params, dy)


def _make_rowop(fn, nrows, out_cols, tr, name):
    @jax.custom_vjp
    def op(*args):
        return _row_fwd_call(fn, args[:nrows], args[nrows:], out_cols, tr, name + "_fwd")

    def fwd(*args):
        return op(*args), args

    def bwd(args, dy):
        return tuple(_row_bwd_call(fn, args[:nrows], args[nrows:], dy, tr, name + "_bwd"))

    op.defvjp(fwd, bwd)
    return op


def _rms_fn(x, w):
    return x * lax.rsqrt(jnp.mean(x * x, axis=-1, keepdims=True) + RMS_EPS) * w


def _merge_fn(pa, pb, pc, gl):
    d = D_MODEL
    return (jax.nn.sigmoid(gl[:, :d]) * pa + jax.nn.sigmoid(gl[:, d:2 * d]) * pb
            + jax.nn.sigmoid(gl[:, 2 * d:]) * pc)


def _relu2_fn(a):
    r = jnp.maximum(a, 0.0)
    return r * r


rms_op = _make_rowop(_rms_fn, 1, D_MODEL, 384, "rms")
merge_op = _make_rowop(_merge_fn, 4, D_MODEL, 192, "merge")
relu2_op = _make_rowop(_relu2_fn, 1, D_FF, 192, "relu2")


def _conv_taps(xext, w, nrows):
    z = None
    for j in range(CONV_K):
        sh = CONV_K - 1 - j
        xs = pltpu.roll(xext, sh, 0) if sh else xext
        term = w[j:j + 1, :] * xs[8:8 + nrows, :]
        z = term if z is None else z + term
    return z


def _halo(ref, start, ok):
    return jnp.where(ok, ref[pl.ds(pl.multiple_of(start, 8), 8), :], 0.0)


def _conv_fwd_call(x, w, b):
    seq, ch = x.shape
    nb = seq // BLK

    def body(x_ref, w_ref, b_ref, o_ref):
        w = w_ref[...]
        bias = b_ref[...]

        def step(i, carry):
            r0 = pl.multiple_of(i * BLK, BLK)
            xext = jnp.concatenate([_halo(x_ref, jnp.maximum(r0 - 8, 0), i > 0), x_ref[pl.ds(r0, BLK), :]], axis=0)
            o_ref[pl.ds(r0, BLK), :] = _silu(_conv_taps(xext, w, BLK) + bias)
            return carry

        lax.fori_loop(0, nb, step, 0)

    strip = pl.BlockSpec((seq, 128), lambda c: (0, c))
    return pl.pallas_call(
        body, name="conv_fwd", grid=(ch // 128,),
        in_specs=[strip, pl.BlockSpec((CONV_K, 128), lambda c: (0, c)), pl.BlockSpec((1, 128), lambda c: (0, c))],
        out_specs=strip, out_shape=jax.ShapeDtypeStruct(x.shape, F32),
        compiler_params=_cp(("parallel",)),
    )(x, w, b)


def _conv_bwd_call(x, w, b, dy):
    seq, ch = x.shape
    nb = seq // BLK

    def body(x_ref, w_ref, b_ref, dy_ref, dx_ref, dw_ref, db_ref):
        w = w_ref[...]
        bias = b_ref[...]

        def step(i, carry):
            r0 = pl.multiple_of(i * BLK, BLK)
            last = i == nb - 1
            nxt = jnp.minimum(r0 + BLK, seq - 8)
            xext = jnp.concatenate([_halo(x_ref, jnp.maximum(r0 - 8, 0), i > 0), x_ref[pl.ds(r0, BLK), :],
                                    _halo(x_ref, nxt, jnp.logical_not(last))], axis=0)
            dyext = jnp.concatenate([dy_ref[pl.ds(r0, BLK), :], _halo(dy_ref, nxt, jnp.logical_not(last))], axis=0)
            z = _conv_taps(xext, w, BLK + 8) + bias
            sg = jax.nn.sigmoid(z)
            dz = dyext * (sg * (1.0 + z * (1.0 - sg)))
            dx = None
            for j in range(CONV_K):
                sh = CONV_K - 1 - j
                dzs = pltpu.roll(dz, BLK + 8 - sh, 0) if sh else dz
                term = w[j:j + 1, :] * dzs[:BLK, :]
                dx = term if dx is None else dx + term
            dx_ref[pl.ds(r0, BLK), :] = dx
            dzm = dz[:BLK, :]
            out = []
            for j in range(CONV_K):
                sh = CONV_K - 1 - j
                xs = pltpu.roll(xext, sh, 0) if sh else xext
                out.append(carry[j] + jnp.sum(dzm * xs[8:8 + BLK, :], axis=0, keepdims=True))
            out.append(carry[CONV_K] + jnp.sum(dzm, axis=0, keepdims=True))
            return tuple(out)

        zero = jnp.zeros((1, 128), F32)
        acc = lax.fori_loop(0, nb, step, (zero,) * (CONV_K + 1))
        dw_ref[...] = jnp.concatenate(acc[:CONV_K], axis=0)
        db_ref[...] = acc[CONV_K]

    strip = pl.BlockSpec((seq, 128), lambda c: (0, c))
    wsp = pl.BlockSpec((CONV_K, 128), lambda c: (0, c))
    bsp = pl.BlockSpec((1, 128), lambda c: (0, c))
    return pl.pallas_call(
        body, name="conv_bwd", grid=(ch // 128,),
        in_specs=[strip, wsp, bsp, strip],
        out_specs=[strip, wsp, bsp],
        out_shape=[jax.ShapeDtypeStruct(x.shape, F32), jax.ShapeDtypeStruct(w.shape, F32),
                   jax.ShapeDtypeStruct(b.shape, F32)],
        compiler_params=_cp(("parallel",)),
    )(x, w, b, dy)


@jax.custom_vjp
def conv_silu(x, w, b):
    return _conv_fwd_call(x, w, b)


def _conv_silu_fwd(x, w, b):
    return _conv_fwd_call(x, w, b), (x, w, b)


def _conv_silu_bwd(res, dy):
    return tuple(_conv_bwd_call(*res, dy))


conv_silu.defvjp(_conv_silu_fwd, _conv_silu_bwd)


def _loss_call(h, wf, target):
    seq, d = h.shape
    nb = seq // BLK

    def body(h_ref, w_ref, t_ref, loss_ref, dh_ref, dw_ref):
        i = pl.program_id(0)
        live = (i > 0).astype(F32)
        tgt = t_ref[...]

        def fn(hh, ww):
            err = _rms_fn(hh, ww) - tgt
            return 0.5 * live * jnp.sum(jnp.mean(err * err, axis=-1, keepdims=True), axis=0, keepdims=True)

        val, vjp = jax.vjp(fn, h_ref[...], w_ref[...])
        dh, dw = vjp(jnp.ones((1, 1), F32))
        dh_ref[...] = dh

        @pl.when(i == 0)
        def _():
            loss_ref[...] = jnp.zeros_like(loss_ref)
            dw_ref[...] = jnp.zeros_like(dw_ref)

        loss_ref[...] += val
        dw_ref[...] += dw

    return pl.pallas_call(
        body, name="loss_head", grid=(nb,),
        in_specs=[pl.BlockSpec((BLK, d), lambda i: (i, 0)), pl.BlockSpec((1, d), lambda i: (0, 0)),
                  pl.BlockSpec((BLK, d), lambda i: (jnp.maximum(i - 1, 0), 0))],
        out_specs=[pl.BlockSpec((1, 1), lambda i: (0, 0)), pl.BlockSpec((BLK, d), lambda i: (i, 0)),
                   pl.BlockSpec((1, d), lambda i: (0, 0))],
        out_shape=[jax.ShapeDtypeStruct((1, 1), F32), jax.ShapeDtypeStruct(h.shape, F32),
                   jax.ShapeDtypeStruct((1, d), F32)],
        compiler_params=_cp(("arbitrary",)),
    )(h, wf, target)


def _make_loss_head(target):
    @jax.custom_vjp
    def head(h, wf):
        return _loss_call(h, wf, target)[0][0, 0]

    def fwd(h, wf):
        loss, dh, dw = _loss_call(h, wf, target)
        return loss[0, 0], (dh, dw)

    def bwd(res, g):
        return g * res[0], g * res[1]

    head.defvjp(fwd, bwd)
    return head


_IN_SEGS = (("q", 0, 1024), ("k", 1024, 1024), ("v", 2048, 1024), ("gate", 3072, 1024), ("z", 4112, 1024),
            ("xbc", 5136, 2048), ("cq", 7200, 1024), ("ck", 8224, 256), ("cv", 8480, 256), ("gl", 8736, 3072),
            ("b", 4096, 8), ("a", 4104, 8), ("dt", 7184, 16))
_IN_PAD = sum(_SPLIT) - sum(n for _, _, n in _IN_SEGS)


_MATMUL = ("w_in", "w_proj_gdn", "w_proj_ssd", "w_proj_swa", "w_out", "w_up", "w_down")
_LATE = _MATMUL[1:]


def _late_weights(gathered):
    g = dict(zip(_LATE, gathered))
    full = {n: g[n].reshape(D_MODEL, D_MODEL) for n in _LATE[:4]}
    full["w_up"] = g["w_up"].transpose(1, 0, 2).reshape(D_MODEL, D_FF)
    full["w_down"] = g["w_down"].reshape(D_FF, D_MODEL)
    full.update({n + "_t": t.T for n, t in list(full.items())})
    return full


def _layer(h, p, w_in, w_in_t, slot, late_shards, next_shards=(), exchange_slots=()):
    wb = {"w_in": w_in, "w_in_t": w_in_t}

    def proj(t, name):
        return mm(t, wb[name], wb[name + "_t"], slot[name])

    q_pre, k_pre, v_pre, gate, z, xbc_pre, cq, ckv, gl, small = split_cols(
        proj(rms_op(h, p["norm1_w"].reshape(1, -1)), "w_in"))

    gcw = p["gdn_conv_w"]
    nob = jnp.zeros((1, GDN_H * GDN_D), F32)
    qa = conv_silu(q_pre, gcw[:, :1024], nob)
    ka = conv_silu(k_pre, gcw[:, 1024:2048], nob)
    va = conv_silu(v_pre, gcw[:, 2048:], nob)
    y_gdn, late, placeholders = gdn_core(
        qa, ka, va, gate, small, p["gdn_a_log"].reshape(GDN_H, 1, 1), p["gdn_dt_bias"].reshape(GDN_H, 1, 1),
        p["gdn_norm_w"].reshape(1, GDN_D), tuple(late_shards), tuple(exchange_slots))
    wb.update(_late_weights(late))

    xbc = conv_silu(xbc_pre, p["ssd_conv_w"], p["ssd_conv_b"].reshape(1, -1))
    y_ssd, gathered = ssd_core(xbc, z, small, p["ssd_dt_bias"].reshape(SSD_H, 1, 1),
                               p["ssd_a_log"].reshape(SSD_H, 1, 1), p["ssd_d"].reshape(SSD_H, 1, 1),
                               p["ssd_norm_w"].reshape(SSD_H, 1, SSD_P), tuple(next_shards))

    y_swa = swa_core(cq, ckv, p["swa_sinks"].reshape(SWA_QH, 1, 1))

    merged = merge_op(proj(y_gdn, "w_proj_gdn"), proj(y_ssd, "w_proj_ssd"), proj(y_swa, "w_proj_swa"), gl)
    h = h + proj(merged, "w_out")
    a1 = proj(rms_op(h, p["norm2_w"].reshape(1, -1)), "w_up")
    return h + proj(relu2_op(a1), "w_down"), gathered, placeholders


_PER_LAYER = ("norm1_w", "gdn_conv_w", "gdn_a_log", "gdn_dt_bias", "gdn_norm_w", "ssd_conv_w", "ssd_conv_b",
              "ssd_dt_bias", "ssd_a_log", "ssd_d", "ssd_norm_w", "swa_sinks", "norm2_w")


def _embed(x, meta):
    return jnp.concatenate([jnp.zeros((NPAD, D_MODEL), F32), meta, x], axis=0)


_IN_SHARD = 1476


def _in_pieces():
    out = []
    for _, s, n in _IN_SEGS:
        c = s
        while c < s + n:
            d = c // _IN_SHARD
            e = min(s + n, (d + 1) * _IN_SHARD)
            out.append((d, c - d * _IN_SHARD, e - d * _IN_SHARD))
            c = e
    return out


def _in_pieces_back():
    start, off = {}, 0
    for _, s, n in _IN_SEGS:
        start[s] = off
        off += n
    out = [[] for _ in range(N_DEV)]
    for _, s, n in sorted(_IN_SEGS, key=lambda t: t[1]):
        c = s
        while c < s + n:
            d = c // _IN_SHARD
            e = min(s + n, (d + 1) * _IN_SHARD)
            out[d].append((start[s] + c - s, start[s] + e - s))
            c = e
    return out


def _regroup_w_in(stacked):
    parts = [stacked[d, :, lo:hi] for d, lo, hi in _in_pieces()]
    return jnp.concatenate(parts + [jnp.zeros((D_MODEL, _IN_PAD), stacked.dtype)], axis=1)


def _ungroup_w_in(g):
    return [jnp.concatenate([g[:, lo:hi] for lo, hi in pieces], axis=1) for pieces in _in_pieces_back()]


def _position():
    return lax.axis_index("x"), lax.axis_index("y"), lax.axis_index("c")


_ANY = pl.BlockSpec(memory_space=pl.ANY)


def _chip_of(x, y, k):
    return (1 - x if k & 1 else x, 1 - y if k & 2 else y)


def _allgather_call(shards, name):
    n = len(shards)

    def body(*refs):
        start, relay, finish = _gather_phases(refs[:n], refs[n:2 * n], *refs[2 * n:])
        start()
        relay()
        finish()

    return pl.pallas_call(
        body, name=name,
        out_shape=_gather_out_shapes(shards),
        in_specs=[_ANY] * n, out_specs=[_ANY] * n,
        scratch_shapes=_gather_sems(n),
    )(*shards)


def _gather_out_shapes(shards):
    return [jax.ShapeDtypeStruct((N_DEV, *s.shape), s.dtype) for s in shards]


def _gather_sems(n):
    return [pltpu.SemaphoreType.DMA((7 * n,)), pltpu.SemaphoreType.DMA((7 * n,)), pltpu.SemaphoreType.DMA((n,))]


def _gather_phases(x_refs, out_refs, send_sems, recv_sems, local_sems):
    n = len(x_refs)
    x, y, c = _position()
    me, sibling = (x, y, c), (x, y, 1 - c)
    chips = [_chip_of(x, y, k) for k in (1, 2, 3)]

    def slab(a, px, py, pc):
        return out_refs[a].at[4 * px + 2 * py + pc]

    def copy(a, k, block, to, src=None):
        return pltpu.make_async_remote_copy(
            src_ref=slab(a, *block) if src is None else src, dst_ref=slab(a, *block),
            send_sem=send_sems.at[7 * a + k], recv_sem=recv_sems.at[7 * a + k], device_id=to, device_id_type=MESH)

    def mine():
        return [pltpu.make_async_copy(x_refs[a], slab(a, *me), local_sems.at[a]) for a in range(n)]

    def first():
        out = []
        for a in range(n):
            out.append(copy(a, 0, me, sibling, src=x_refs[a]))
            out += [copy(a, 1 + j, me, (*chip, c), src=x_refs[a]) for j, chip in enumerate(chips)]
        return out

    def passed():
        return [copy(a, 4 + j, (*chip, c), sibling) for j, chip in enumerate(chips) for a in range(n)]

    def start():
        for cp in mine() + first():
            cp.start()

    def relay():
        for j, chip in enumerate(chips):
            for a in range(n):
                copy(a, 1 + j, (*chip, c), me).wait_recv()
                copy(a, 4 + j, (*chip, c), sibling).start()

    def finish():
        for a in range(n):
            copy(a, 0, sibling, me).wait_recv()
        for j, chip in enumerate(chips):
            for a in range(n):
                copy(a, 4 + j, (*chip, 1 - c), me).wait_recv()
        for cp in first() + passed():
            cp.wait_send()
        for cp in mine():
            cp.wait()

    return start, relay, finish


def _sibling_exchange_call(for_c0, for_c1, name):
    n = len(for_c0)

    def body(*refs):
        c0_refs, c1_refs, out_refs = refs[:n], refs[n:2 * n], refs[2 * n:3 * n]
        send_sems, recv_sems = refs[3 * n:]
        x, y, c = _position()

        def copies(src_refs):
            return [pltpu.make_async_remote_copy(
                src_ref=src_refs[a].at[q], dst_ref=out_refs[a].at[q],
                send_sem=send_sems.at[4 * a + q], recv_sem=recv_sems.at[4 * a + q],
                device_id=(x, y, 1 - c), device_id_type=MESH) for a in range(n) for q in range(4)]

        @pl.when(c == 0)
        def _():
            for cp in copies(c1_refs):
                cp.start()

        @pl.when(c == 1)
        def _():
            for cp in copies(c0_refs):
                cp.start()

        waits = copies(c0_refs)
        for cp in waits:
            cp.wait_recv()
        for cp in waits:
            cp.wait_send()

    return pl.pallas_call(
        body, name=name,
        out_shape=[jax.ShapeDtypeStruct(g.shape, g.dtype) for g in for_c0],
        in_specs=[_ANY] * (2 * n), out_specs=[_ANY] * n,
        scratch_shapes=[pltpu.SemaphoreType.DMA((4 * n,)), pltpu.SemaphoreType.DMA((4 * n,))],
    )(*for_c0, *for_c1)


def _chip_exchange_call(partials, name):
    n = len(partials)

    def body(*refs):
        start, finish = _chip_exchange_phases(refs[:n], refs[n:2 * n], *refs[2 * n:])
        start()
        finish()

    return pl.pallas_call(
        body, name=name,
        out_shape=_chip_exchange_out_shapes(partials),
        in_specs=[_ANY] * n, out_specs=[_ANY] * n,
        scratch_shapes=_chip_exchange_sems(n),
    )(*partials)


def _chip_exchange_out_shapes(partials):
    return [jax.ShapeDtypeStruct((3, *p.shape[1:]), p.dtype) for p in partials]


def _chip_exchange_sems(n):
    return [pltpu.SemaphoreType.DMA((3 * n,)), pltpu.SemaphoreType.DMA((3 * n,))]


def _chip_exchange_phases(p_refs, out_refs, send_sems, recv_sems):
    n = len(p_refs)
    x, y, c = _position()

    def copies():
        out = []
        for a in range(n):
            for k in (1, 2, 3):
                px, py = _chip_of(x, y, k)
                out.append(pltpu.make_async_remote_copy(
                    src_ref=p_refs[a].at[2 * px + py], dst_ref=out_refs[a].at[k - 1],
                    send_sem=send_sems.at[3 * a + k - 1], recv_sem=recv_sems.at[3 * a + k - 1],
                    device_id=(px, py, c), device_id_type=MESH))
        return out

    def start():
        for cp in copies():
            cp.start()

    def finish():
        for cp in copies():
            cp.wait_recv()
        for cp in copies():
            cp.wait_send()

    return start, finish


def _chip_partial_call(for_c0, for_c1, sib, tr, name):
    _, r, c = sib.shape

    def body(c0_ref, c1_ref, s_ref, own_ref, out_ref):
        x, y, core = _position()
        mine = jnp.where(core == 0, c0_ref[...], c1_ref[...])
        partial = mine + s_ref[...]
        own = jnp.zeros((tr, c), F32)
        for q in range(4):
            own = jnp.where(2 * x + y == q, partial[q], own)
        own_ref[...] = own
        out_ref[...] = partial.astype(BF16)

    four = pl.BlockSpec((4, tr, c), lambda i: (0, i, 0))
    return pl.pallas_call(
        body, name=name, grid=(r // tr,),
        in_specs=[four, four, four],
        out_specs=[pl.BlockSpec((tr, c), lambda i: (i, 0)), four],
        out_shape=[jax.ShapeDtypeStruct((r, c), F32), jax.ShapeDtypeStruct((4, r, c), BF16)],
        compiler_params=_cp(("parallel",)),
    )(for_c0, for_c1, sib)


def _adamw_call(parts, w, m, v, tr, name):
    ns, r, c = w.shape
    counts = [len(p) for p in parts]
    flat_parts = [a for p in parts for a in p]

    def body(*refs):
        p_refs = refs[:len(flat_parts)]
        w_ref, m_ref, v_ref, g_ref, d_ref, nm_ref, nv_ref = refs[len(flat_parts):]
        at = 0
        for s in range(ns):
            g = None
            for p_ref in p_refs[at:at + counts[s]]:
                for j in range(p_ref.shape[0]):
                    term = p_ref[j].astype(F32)
                    g = term if g is None else g + term
            at += counts[s]
            nm = ADAM_B1 * m_ref[s] + (1.0 - ADAM_B1) * g
            nv = ADAM_B2 * v_ref[s] + (1.0 - ADAM_B2) * (g * g)
            m_hat = nm / (1.0 - ADAM_B1 ** ADAM_STEP)
            v_hat = nv / (1.0 - ADAM_B2 ** ADAM_STEP)
            g_ref[s] = g
            d_ref[s] = -ADAM_LR * (m_hat / (jnp.sqrt(v_hat) + ADAM_EPS) + ADAM_WD * w_ref[s])
            nm_ref[s] = nm
            nv_ref[s] = nv

    slabs = pl.BlockSpec((ns, tr, c), lambda i: (0, i, 0))
    return pl.pallas_call(
        body, name=name, grid=(r // tr,),
        in_specs=[pl.BlockSpec((a.shape[0], tr, c), lambda i: (0, i, 0)) for a in flat_parts] + [slabs] * 3,
        out_specs=[slabs] * 4,
        out_shape=[jax.ShapeDtypeStruct((ns, r, c), F32)] * 4,
        compiler_params=_cp(("parallel",)),
    )(*flat_parts, w, m, v)


_WEIGHTS = ("meta_tokens", "norm1_w", "w_in", "gdn_conv_w", "gdn_a_log", "gdn_dt_bias", "gdn_norm_w", "ssd_conv_w",
            "ssd_conv_b", "ssd_dt_bias", "ssd_a_log", "ssd_d", "ssd_norm_w", "swa_sinks", "w_proj_gdn", "w_proj_ssd",
            "w_proj_swa", "w_out", "norm2_w", "w_up", "w_down", "final_norm_w")
_SHARD_AXIS = {"meta_tokens": 1, "w_in": 2, "gdn_conv_w": 2, "ssd_conv_w": 2, "w_proj_gdn": 1, "w_proj_ssd": 1,
               "w_proj_swa": 1, "w_out": 1, "w_up": 2, "w_down": 1}
_BIG = tuple(n for n in _WEIGHTS if n in _SHARD_AXIS)
_SMALL = tuple(n for n in _WEIGHTS if n not in _SHARD_AXIS)
FLAT_C = 1024


def _pack(arrs, rows, lead=()):
    flat = jnp.concatenate([a.reshape(*lead, -1) for a in arrs], axis=-1)
    pad = rows * FLAT_C - flat.shape[-1]
    flat = jnp.pad(flat, [(0, 0)] * len(lead) + [(0, pad)])
    return flat.reshape(*lead, rows, FLAT_C)


def _unpack(flat, shapes, lead=()):
    flat = flat.reshape(*lead, -1)
    out, off = [], 0
    for s in shapes:
        n = math.prod(s)
        out.append(flat[..., off:off + n].reshape(*lead, *s))
        off += n
    return out


def _rows_for(shapes):
    n = sum(math.prod(s) for s in shapes)
    return -(-n // (FLAT_C * 8)) * 8


def _rows_tile(r, c):
    if r <= 256:
        return r
    return 128 if c > 1024 else 256


def _join(stacked, axis):
    moved = jnp.moveaxis(stacked, 0, axis)
    return moved.reshape(*moved.shape[:axis], -1, *moved.shape[axis + 2:])


def _unjoin(full, axis):
    cut = full.reshape(*full.shape[:axis], N_DEV, full.shape[axis] // N_DEV, *full.shape[axis + 1:])
    return jnp.moveaxis(cut, axis, 0)


def kernel(x, meta_tokens, norm1_w, w_in, gdn_conv_w, gdn_a_log, gdn_dt_bias, gdn_norm_w, ssd_conv_w, ssd_conv_b,
           ssd_dt_bias, ssd_a_log, ssd_d, ssd_norm_w, swa_sinks, w_proj_gdn, w_proj_ssd, w_proj_swa, w_out, norm2_w,
           w_up, w_down, final_norm_w, loss_target, m_meta_tokens, m_norm1_w, m_w_in, m_gdn_conv_w, m_gdn_a_log,
           m_gdn_dt_bias, m_gdn_norm_w, m_ssd_conv_w, m_ssd_conv_b, m_ssd_dt_bias, m_ssd_a_log, m_ssd_d, m_ssd_norm_w,
           m_swa_sinks, m_w_proj_gdn, m_w_proj_ssd, m_w_proj_swa, m_w_out, m_norm2_w, m_w_up, m_w_down,
           m_final_norm_w, v_meta_tokens, v_norm1_w, v_w_in, v_gdn_conv_w, v_gdn_a_log, v_gdn_dt_bias, v_gdn_norm_w,
           v_ssd_conv_w, v_ssd_conv_b, v_ssd_dt_bias, v_ssd_a_log, v_ssd_d, v_ssd_norm_w, v_swa_sinks, v_w_proj_gdn,
           v_w_proj_ssd, v_w_proj_swa, v_w_out, v_norm2_w, v_w_up, v_w_down, v_final_norm_w):
    args = (meta_tokens, norm1_w, w_in, gdn_conv_w, gdn_a_log, gdn_dt_bias, gdn_norm_w, ssd_conv_w, ssd_conv_b,
            ssd_dt_bias, ssd_a_log, ssd_d, ssd_norm_w, swa_sinks, w_proj_gdn, w_proj_ssd, w_proj_swa, w_out, norm2_w,
            w_up, w_down, final_norm_w, m_meta_tokens, m_norm1_w, m_w_in, m_gdn_conv_w, m_gdn_a_log,
            m_gdn_dt_bias, m_gdn_norm_w, m_ssd_conv_w, m_ssd_conv_b, m_ssd_dt_bias, m_ssd_a_log, m_ssd_d, m_ssd_norm_w,
            m_swa_sinks, m_w_proj_gdn, m_w_proj_ssd, m_w_proj_swa, m_w_out, m_norm2_w, m_w_up, m_w_down,
            m_final_norm_w, v_meta_tokens, v_norm1_w, v_w_in, v_gdn_conv_w, v_gdn_a_log, v_gdn_dt_bias, v_gdn_norm_w,
            v_ssd_conv_w, v_ssd_conv_b, v_ssd_dt_bias, v_ssd_a_log, v_ssd_d, v_ssd_norm_w, v_swa_sinks, v_w_proj_gdn,
            v_w_proj_ssd, v_w_proj_swa, v_w_out, v_norm2_w, v_w_up, v_w_down, v_final_norm_w)
    nw = len(_WEIGHTS)
    w = dict(zip(_WEIGHTS, args[:nw]))
    m = dict(zip(_WEIGHTS, args[nw:2 * nw]))
    v = dict(zip(_WEIGHTS, args[2 * nw:]))

    depth = w["w_in"].shape[0]
    small_shapes = [w[n].shape for n in _SMALL]
    small_rows = _rows_for(small_shapes)

    def flat2(t):
        return t.reshape(-1, t.shape[-1])

    tiny_names = [n for n in _BIG if n not in _MATMUL]

    def shard(n, l):
        return w[n][l].astype(BF16)

    first = _allgather_call([shard("w_in", 0)] + [flat2(w[n]) for n in tiny_names], "gather_weights")
    w_in_stacked = first[0]
    joined = {n: _join(t.reshape(N_DEV, *w[n].shape), _SHARD_AXIS[n]) for n, t in zip(tiny_names, first[1:])}
    slot_shapes = {"w_in": (D_MODEL, sum(_SPLIT)), "w_up": (D_MODEL, D_FF), "w_down": (D_FF, D_MODEL)}
    slot_shapes.update({n: (D_MODEL, D_MODEL) for n in _LATE[:4]})

    def layer_fn(l, w_in_full, late_shards, next_shards):
        w_in_t = w_in_full.T
        if l == 0:
            def fn(x_rows, meta, p, slot, exchange_slots):
                out, g, placeholders = _layer(_embed(x_rows, meta), p, w_in_full, w_in_t, slot, late_shards,
                                              next_shards, exchange_slots)
                return (out, placeholders), g
        else:
            def fn(h_in, p, slot, exchange_slots):
                out, g, placeholders = _layer(h_in, p, w_in_full, w_in_t, slot, late_shards, next_shards,
                                              exchange_slots)
                return (out, placeholders), g
        return fn

    h, vjps = None, []
    for l in range(depth):
        slot = {n: jnp.zeros(s, F32) for n, s in slot_shapes.items()}
        p = {n: (joined[n][l] if n in joined else w[n][l]) for n in _PER_LAYER}
        more = l + 1 < depth
        next_shards = [shard("w_in", l + 1)] if more else []
        exchange_slots = tuple(jnp.zeros((3, *w[n][l + 1].shape), BF16) for n in _MATMUL) if more else ()
        lead = (x[0], joined["meta_tokens"]) if l == 0 else (h,)
        fn = layer_fn(l, _regroup_w_in(w_in_stacked), [shard(n, l) for n in _LATE], next_shards)
        (h, _), vjp, g_next = jax.vjp(fn, *lead, p, slot, exchange_slots, has_aux=True)
        if more:
            w_in_stacked = g_next[0]
        vjps.append(vjp)
    loss, head_vjp = jax.vjp(_make_loss_head(loss_target[0]), h, w["final_norm_w"].reshape(1, -1))
    dh, d_final = head_vjp(jnp.ones((), F32))
    loss = lax.psum(loss, ("x", "y", "c"))

    def by_core(name, g):
        if name == "w_in":
            shards = _ungroup_w_in(g)
            return jnp.stack(shards[0::2]), jnp.stack(shards[1::2])
        if name == "w_up":
            t = g.reshape(D_MODEL, 4, 2, D_FF // N_DEV)
            return t[:, :, 0].transpose(1, 0, 2), t[:, :, 1].transpose(1, 0, 2)
        t = g.reshape(4, 2, -1, g.shape[-1])
        return t[:, 0], t[:, 1]

    own, incoming, layer_grads, outgoing = {}, {}, [None] * depth, ()
    for l in reversed(range(depth)):
        if l == 0:
            gx, d_meta, dp, dslot, arrived = vjps[0]((dh, tuple(outgoing)))
        else:
            dh, dp, dslot, arrived = vjps[l]((dh, tuple(outgoing)))
        incoming.update({(n, l + 1): t for n, t in zip(_MATMUL, arrived)})
        layer_grads[l] = dp
        todo = [((n, l), dslot[n]) for n in _MATMUL]
        if l == 0:
            full_grads = {"meta_tokens": d_meta}
            full_grads.update({n: jnp.stack([layer_grads[k][n] for k in range(depth)]) for n in tiny_names[1:]})
            todo += [((n, None), _unjoin(full_grads[n], _SHARD_AXIS[n]).reshape(N_DEV, -1, w[n].shape[-1]))
                     for n in tiny_names]
        pairs = [by_core(u[0], g) for u, g in todo]
        from_sibling = _sibling_exchange_call([a for a, _ in pairs], [b for _, b in pairs], "grads_to_sibling_%d" % l)
        outgoing = []
        for (u, _), (a0, a1), s in zip(todo, pairs, from_sibling):
            own[u], part = _chip_partial_call(a0, a1, s, _rows_tile(s.shape[1], s.shape[2]), "chip_partial_" + u[0])
            outgoing.append(part)
        if l == 0:
            incoming.update(zip([u for u, _ in todo], _chip_exchange_call(outgoing, "grads_to_chips")))

    g_small = {n: jnp.stack([layer_grads[k][n] for k in range(depth)]) for n in _SMALL if n != "final_norm_w"}
    g_small["final_norm_w"] = d_final.reshape(-1)

    by_name = {}
    for n in _BIG:
        layers = list(range(depth)) if n in _MATMUL else [None]
        parts = [[own[n, l][None], incoming[n, l]] for l in layers]
        r, c = own[n, layers[0]].shape
        stacked = [d[n].reshape(len(layers), r, c) for d in (w, m, v)]
        res = _adamw_call(parts, *stacked, _rows_tile(r, c), "adamw_" + n)
        by_name[n] = [t.reshape(w[n].shape) for t in res]

    small_parts = _allgather_call([_pack([g_small[n] for n in _SMALL], small_rows)], "gather_small_grads")
    small_out = _adamw_call([small_parts], *[_pack([d[n] for n in _SMALL], small_rows)[None] for d in (w, m, v)],
                            small_rows, "adamw_replicated")
    for kind in range(4):
        for n, t in zip(_SMALL, _unpack(small_out[kind][0], small_shapes)):
            by_name.setdefault(n, [None] * 4)[kind] = t

    outs = [by_name[n][kind] for kind in range(4) for n in _WEIGHTS]
    return (loss, gx[None], *outs)
```

```python
import functools
import math

import jax
import jax.numpy as jnp
from jax import lax
from jax.experimental import pallas as pl
from jax.experimental.pallas import tpu as pltpu

F32 = jnp.float32
BF16 = jnp.bfloat16
HI = lax.Precision.HIGH
NEG = -1e30

D_MODEL = 1024
N_META = 16
BLK = 128
NPAD = BLK - N_META
RMS_EPS = 1e-6
L2_EPS = 1e-6
CONV_K = 4

GDN_H, GDN_D, GDN_C = 8, 128, 64
SSD_H, SSD_P, SSD_G, SSD_N = 16, 64, 4, 128
SSD_HPG = SSD_H // SSD_G
SWA_QH, SWA_KVH, SWA_D = 16, 4, 64
SWA_REP = SWA_QH // SWA_KVH
D_FF = 4 * D_MODEL

N_DEV = 8
MESH = pl.DeviceIdType.MESH

ADAM_LR, ADAM_B1, ADAM_B2, ADAM_EPS, ADAM_WD, ADAM_STEP = 0.001, 0.9, 0.999, 1e-08, 0.01, 10

VMEM_LIMIT = 56 * 1024 * 1024


def _cp(sem=None):
    return pltpu.CompilerParams(dimension_semantics=sem, vmem_limit_bytes=VMEM_LIMIT)


def _dot(a, b, ca, cb, prec=HI):
    return lax.dot_general(a, b, (((ca,), (cb,)), ((), ())), precision=prec, preferred_element_type=F32)


def _nn(a, b, prec=HI):
    return _dot(a, b, 1, 0, prec)


def _nt(a, b, prec=HI):
    return _dot(a, b, 1, 1, prec)


def _tn(a, b, prec=HI):
    return _dot(a, b, 0, 0, prec)


def _bdot(a, b, ca, cb):
    return lax.dot_general(a.astype(BF16), b.astype(BF16), (((ca,), (cb,)), ((), ())), preferred_element_type=F32)


@jax.custom_vjp
def _lo_nn(a, b):
    return _bdot(a, b, 1, 0)


_lo_nn.defvjp(lambda a, b: (_bdot(a, b, 1, 0), (a, b)),
              lambda r, d: (_bdot(d, r[1], 1, 1), _bdot(r[0], d, 0, 0)))


@jax.custom_vjp
def _lo_nt(a, b):
    return _bdot(a, b, 1, 1)


_lo_nt.defvjp(lambda a, b: (_bdot(a, b, 1, 1), (a, b)),
              lambda r, d: (_bdot(d, r[1], 1, 0), _bdot(d, r[0], 0, 0)))


@jax.custom_vjp
def _lo_tn(a, b):
    return _bdot(a, b, 0, 0)


_lo_tn.defvjp(lambda a, b: (_bdot(a, b, 0, 0), (a, b)),
              lambda r, d: (_bdot(r[1], d, 1, 1), _bdot(r[0], d, 1, 0)))


def _iota2(n, m, axis):
    return lax.broadcasted_iota(jnp.int32, (n, m), axis)


def _silu(x):
    return x * jax.nn.sigmoid(x)


def _softplus(x):
    return jnp.maximum(x, 0.0) + jnp.log(1.0 + jnp.exp(-jnp.abs(x)))


def _row_of(col):
    n = col.shape[0]
    return jnp.broadcast_to(col, (n, n)).T


def _cumsum_col(col):
    n = col.shape[0]
    tril = (_iota2(n, n, 0) >= _iota2(n, n, 1)).astype(F32)
    return _nn(tril, col)


def _tri_inv(a):
    n = a.shape[0]
    r, c = _iota2(n, n, 0), _iota2(n, n, 1)
    eye = (r == c).astype(F32)
    blk = jnp.right_shift(r, 4) == jnp.right_shift(c, 4)
    d = jnp.where(blk, a, 0.0)
    off = a - d
    d2 = _nn(d, d)
    d4 = _nn(d2, d2)
    d8 = _nn(d4, d4)
    td = _nn(_nn(_nn(eye - d, eye + d2), eye + d4), eye + d8)
    m = _nn(td, off)
    m2 = _nn(m, m)
    return _nn(_nn(eye - m, eye + m2), td)


@jax.custom_vjp
def _tri_solve(a, inv, rhs):
    return _nn(inv, rhs)


def _tri_solve_fwd(a, inv, rhs):
    sol = _nn(inv, rhs)
    return sol, (inv, sol)


def _tri_solve_bwd(res, dsol):
    inv, sol = res
    drhs = _nn(inv.T, dsol)
    return -_nt(drhs, sol), jnp.zeros_like(inv), drhs


_tri_solve.defvjp(_tri_solve_fwd, _tri_solve_bwd)


def _gdn_chunk(qa, ka, va, gate, a_raw, b_raw, s, a_log, dt_bias, norm_w, valid, inv=None, want_inv=False):
    c = qa.shape[0]
    q = qa * lax.rsqrt(jnp.sum(qa * qa, axis=-1, keepdims=True) + L2_EPS) * (GDN_D ** -0.5)
    k = ka * lax.rsqrt(jnp.sum(ka * ka, axis=-1, keepdims=True) + L2_EPS)
    beta = jax.nn.sigmoid(b_raw)
    g = -jnp.exp(a_log) * _softplus(a_raw + dt_bias) * valid
    gam = _cumsum_col(g)
    gam_row = _row_of(gam)
    r, cc = _iota2(c, c, 0), _iota2(c, c, 1)
    decay = jnp.exp(jnp.where(r >= cc, gam - gam_row, NEG))
    kb = k * beta
    a = jnp.where(r > cc, _lo_nt(kb, k) * decay, 0.0)
    egam = jnp.exp(gam)
    if inv is None:
        inv = _tri_inv(lax.stop_gradient(a))
    sol = _tri_solve(a, inv, jnp.concatenate([va * beta, kb * egam], axis=1))
    u = sol[:, :GDN_D]
    w = sol[:, GDN_D:]
    attn = _lo_nt(q, k) * decay
    g_last = jnp.sum(g, axis=0, keepdims=True)
    k_tail = k * jnp.exp(g_last - gam)
    v_new = u - _lo_nn(w, s)
    o = _lo_nn(q * egam, s) + _lo_nn(attn, v_new)
    s_new = s * jnp.exp(g_last) + _lo_tn(k_tail, v_new)
    y = o * lax.rsqrt(jnp.mean(o * o, axis=-1, keepdims=True) + RMS_EPS) * norm_w * _silu(gate)
    return (y, s_new, inv) if want_inv else (y, s_new)


def _valid_col(row0, n):
    return (row0 + _iota2(n, 1, 0) >= NPAD).astype(F32)


GDN_HB = GDN_H

SM_B, SM_A, SM_DT, SM_W = 0, 8, 16, 128


def _pick_cols(sm, first, n):
    return jnp.stack([sm[:, first + j:first + j + 1] for j in range(n)])


def _spread_cols(cols, first):
    lane = _iota2(1, SM_W, 1)
    out = None
    for j in range(cols.shape[0]):
        term = cols[j] * (lane == first + j).astype(F32)
        out = term if out is None else out + term
    return out


def _widen(t, width):
    if width == t.shape[1]:
        return t
    return jnp.concatenate([t, jnp.zeros((t.shape[0], width - t.shape[1]), t.dtype)], axis=1)


def _gdn_specs(nc, rev):
    ci = (lambda i: nc - 1 - i) if rev else (lambda i: i)
    hb = GDN_HB
    tile = pl.BlockSpec((GDN_C, hb * GDN_D), lambda h, i: (ci(i), h))
    col = pl.BlockSpec((GDN_C, SM_W), lambda h, i: (ci(i), 0))
    scal = pl.BlockSpec((hb, 1, 1), lambda h, i: (h, 0, 0))
    nw = pl.BlockSpec((1, GDN_D), lambda h, i: (0, 0))
    st = pl.BlockSpec((hb, 1, GDN_D, GDN_D), lambda h, i: (h, ci(i), 0, 0))
    return tile, col, scal, nw, st


def _lanes(j):
    return slice(j * GDN_D, (j + 1) * GDN_D)


def _by_head(ref):
    return jnp.stack([ref[:, _lanes(j)] for j in range(GDN_HB)])


def _gdn_fwd_call(q, k, v, gate, small, a_log, dt_bias, norm_w, shards=()):
    seq = q.shape[0]
    nc = seq // GDN_C
    ns = len(shards)
    tile, col, scal, nw, st = _gdn_specs(nc, False)

    def body(*refs):
        q_ref, k_ref, v_ref, g_ref, sm_ref, al_ref, dt_ref, nw_ref = refs[:8]
        y_ref, st_ref, inv_ref = refs[8 + ns:11 + ns]
        s_scr = refs[11 + 2 * ns]
        i = pl.program_id(1)
        if ns:
            start, relay, finish = _gather_phases(refs[8:8 + ns], refs[11 + ns:11 + 2 * ns], *refs[12 + 2 * ns:])
            pl.when(i == 0)(start)
            pl.when(i == nc - 1)(relay)

        @pl.when(i == 0)
        def _():
            s_scr[...] = jnp.zeros_like(s_scr)

        s = s_scr[...]
        st_ref[:, 0] = s
        sm = sm_ref[...]
        fn = jax.vmap(functools.partial(_gdn_chunk, valid=_valid_col(i * GDN_C, GDN_C), want_inv=True))
        y, s_new, inv = fn(_by_head(q_ref), _by_head(k_ref), _by_head(v_ref), _by_head(g_ref),
                           _pick_cols(sm, SM_A, GDN_H), _pick_cols(sm, SM_B, GDN_H), s,
                           al_ref[...], dt_ref[...], jnp.broadcast_to(nw_ref[...], (GDN_HB, 1, GDN_D)))
        for j in range(GDN_HB):
            y_ref[:, _lanes(j)] = y[j]
        inv_ref[:, 0] = inv
        s_scr[...] = s_new
        if ns:
            pl.when(i == nc - 1)(finish)

    return pl.pallas_call(
        body, name="gdn_fwd", grid=(GDN_H // GDN_HB, nc),
        in_specs=[tile, tile, tile, tile, col, scal, scal, nw] + [_ANY] * ns,
        out_specs=[tile, st, pl.BlockSpec((GDN_HB, 1, GDN_C, GDN_C), lambda h, i: (h, i, 0, 0))] + [_ANY] * ns,
        out_shape=[jax.ShapeDtypeStruct((seq, GDN_H * GDN_D), F32),
                   jax.ShapeDtypeStruct((GDN_H, nc, GDN_D, GDN_D), F32),
                   jax.ShapeDtypeStruct((GDN_H, nc, GDN_C, GDN_C), F32)] + _gather_out_shapes(shards),
        scratch_shapes=[pltpu.VMEM((GDN_HB, GDN_D, GDN_D), F32)] + (_gather_sems(ns) if ns else []),
        compiler_params=_cp(("parallel", "arbitrary")),
    )(q, k, v, gate, small, a_log, dt_bias, norm_w, *shards)


def _gdn_bwd_call(q, k, v, gate, small, a_log, dt_bias, norm_w, states, invs, dy, outgoing=()):
    seq = q.shape[0]
    nc = seq // GDN_C
    no = len(outgoing)
    tile, col, scal, nw, st = _gdn_specs(nc, True)
    nwh = pl.BlockSpec((GDN_HB, 1, GDN_D), lambda h, i: (h, 0, 0))
    inv_spec = pl.BlockSpec((GDN_HB, 1, GDN_C, GDN_C), lambda h, i: (h, nc - 1 - i, 0, 0))

    def body(*refs):
        q_ref, k_ref, v_ref, g_ref, sm_ref, al_ref, dt_ref, nw_ref, st_ref, inv_ref, dy_ref = refs[:11]
        dq_ref, dk_ref, dv_ref, dg_ref, dsm_ref, dal_ref, ddt_ref, dnw_ref = refs[11 + no:19 + no]
        ds_scr = refs[19 + 2 * no]
        i = pl.program_id(1)
        if no:
            start, finish = _chip_exchange_phases(refs[11:11 + no], refs[19 + no:19 + 2 * no], *refs[20 + 2 * no:])
            pl.when(i == 0)(start)

        @pl.when(i == 0)
        def _():
            ds_scr[...] = jnp.zeros_like(ds_scr)
            dal_ref[...] = jnp.zeros_like(dal_ref)
            ddt_ref[...] = jnp.zeros_like(ddt_ref)
            dnw_ref[...] = jnp.zeros_like(dnw_ref)

        sm = sm_ref[...]
        valid = _valid_col((nc - 1 - i) * GDN_C, GDN_C)
        kept = inv_ref[:, 0]

        def fn(*heads):
            return jax.vmap(lambda *t: _gdn_chunk(*t[:-1], valid=valid, inv=t[-1]))(*heads, kept)

        _, vjp = jax.vjp(fn, _by_head(q_ref), _by_head(k_ref), _by_head(v_ref), _by_head(g_ref),
                         _pick_cols(sm, SM_A, GDN_H), _pick_cols(sm, SM_B, GDN_H), st_ref[:, 0], al_ref[...],
                         dt_ref[...], jnp.broadcast_to(nw_ref[...], (GDN_HB, 1, GDN_D)))
        dq, dk, dv, dg, da, db, ds, dal, ddt, dnw = vjp((_by_head(dy_ref), ds_scr[...]))
        for j in range(GDN_HB):
            dq_ref[:, _lanes(j)] = dq[j]
            dk_ref[:, _lanes(j)] = dk[j]
            dv_ref[:, _lanes(j)] = dv[j]
            dg_ref[:, _lanes(j)] = dg[j]
        dsm_ref[...] = _widen(_spread_cols(da, SM_A) + _spread_cols(db, SM_B), dsm_ref.shape[1])
        ds_scr[...] = ds
        dal_ref[...] += dal
        ddt_ref[...] += ddt
        dnw_ref[...] += dnw
        if no:
            pl.when(i == nc - 1)(finish)

    big = jax.ShapeDtypeStruct((seq, GDN_H * GDN_D), F32)
    return pl.pallas_call(
        body, name="gdn_bwd", grid=(GDN_H // GDN_HB, nc),
        in_specs=[tile, tile, tile, tile, col, scal, scal, nw, st, inv_spec, tile] + [_ANY] * no,
        out_specs=[tile, tile, tile, tile, pl.BlockSpec((GDN_C, small.shape[1]), lambda h, i: (nc - 1 - i, 0)),
                   scal, scal, nwh] + [_ANY] * no,
        out_shape=[big, big, big, big, jax.ShapeDtypeStruct(small.shape, F32),
                   jax.ShapeDtypeStruct((GDN_H, 1, 1), F32), jax.ShapeDtypeStruct((GDN_H, 1, 1), F32),
                   jax.ShapeDtypeStruct((GDN_H, 1, GDN_D), F32)] + _chip_exchange_out_shapes(outgoing),
        scratch_shapes=[pltpu.VMEM((GDN_HB, GDN_D, GDN_D), F32)] + (_chip_exchange_sems(no) if no else []),
        compiler_params=_cp(("parallel", "arbitrary")),
    )(q, k, v, gate, small, a_log, dt_bias, norm_w, states, invs, dy, *outgoing)


@jax.custom_vjp
def gdn_core(q, k, v, gate, small, a_log, dt_bias, norm_w, shards, slots):
    y, _, _, *gathered = _gdn_fwd_call(q, k, v, gate, small, a_log, dt_bias, norm_w, shards)
    return y, tuple(gathered), tuple(jnp.zeros((4, *s.shape[1:]), s.dtype) for s in slots)


def _gdn_core_fwd(q, k, v, gate, small, a_log, dt_bias, norm_w, shards, slots):
    y, states, invs, *gathered = _gdn_fwd_call(q, k, v, gate, small, a_log, dt_bias, norm_w, shards)
    out = (y, tuple(gathered), tuple(jnp.zeros((4, *s.shape[1:]), s.dtype) for s in slots))
    return out, (q, k, v, gate, small, a_log, dt_bias, norm_w, states, invs, shards)


def _gdn_core_bwd(res, cts):
    *args, shards = res
    dy, _, outgoing = cts
    dq, dk, dv, dg, dsm, dal, ddt, dnw, *incoming = _gdn_bwd_call(*args, dy, outgoing)
    return (dq, dk, dv, dg, dsm, dal, ddt, jnp.sum(dnw, axis=0), tuple(jnp.zeros_like(s) for s in shards),
            tuple(incoming))


gdn_core.defvjp(_gdn_core_fwd, _gdn_core_bwd)


def _ssd_head(x, z, dt_raw, h, dt_bias, a_log, d_skip, bm, cm, cb, valid):
    c = bm.shape[0]
    r, cc = _iota2(c, c, 0), _iota2(c, c, 1)
    dtp = _softplus(dt_raw + dt_bias)
    x = x * valid
    adt = -jnp.exp(a_log) * dtp * valid
    xdt = x * dtp
    acum = _cumsum_col(adt)
    lmat = jnp.exp(jnp.where(r >= cc, acum - _row_of(acum), NEG))
    a_last = jnp.sum(adt, axis=0, keepdims=True)
    y = _lo_nn(cb * lmat, xdt) + _lo_nt(cm * jnp.exp(acum), h) + d_skip * x
    h_new = h * jnp.exp(a_last) + _lo_tn(xdt * jnp.exp(a_last - acum), bm)
    return y * _silu(z), h_new


SSD_SIDE = SSD_H


def _ssd_chunk(xs, z, bm, cm, dt_raw, h, dt_bias, a_log, d_skip, norm_w, valid):
    nh, c, p = xs.shape
    ng = bm.shape[0]
    hpg = nh // ng
    bm = bm * valid
    cm = cm * valid
    cb = jax.vmap(_lo_nt)(cm, bm)
    per_head = lambda t: jnp.repeat(t, hpg, axis=0)
    args = (xs, z, dt_raw, h, dt_bias, a_log, d_skip, per_head(bm), per_head(cm), per_head(cb))
    outs = [jax.vmap(functools.partial(_ssd_head, valid=valid))(*[t[s:s + SSD_SIDE] for t in args])
            for s in range(0, nh, SSD_SIDE)]
    ys = jnp.concatenate([o[0] for o in outs], axis=0)
    hs = jnp.concatenate([o[1] for o in outs], axis=0)
    ss = jnp.sum(jnp.sum(ys * ys, axis=-1, keepdims=True).reshape(ng, hpg, c, 1), axis=1, keepdims=True)
    rstd = lax.rsqrt(ss / (hpg * p) + RMS_EPS)
    return (ys.reshape(ng, hpg, c, p) * rstd).reshape(nh, c, p) * norm_w, hs


SSD_INNER = SSD_H * SSD_P
SSD_BC = SSD_G * SSD_N


def _split_lanes(t, n, w):
    return jnp.stack([t[:, j * w:(j + 1) * w] for j in range(n)])


def _join_lanes(t):
    return jnp.concatenate([t[j] for j in range(t.shape[0])], axis=1)


def _ssd_specs(nc, rev):
    ci = (lambda i: nc - 1 - i) if rev else (lambda i: i)
    wide = pl.BlockSpec((BLK, SSD_INNER), lambda i: (ci(i), 0))
    bmat = pl.BlockSpec((BLK, SSD_BC), lambda i: (ci(i), SSD_INNER // SSD_BC))
    cmat = pl.BlockSpec((BLK, SSD_BC), lambda i: (ci(i), SSD_INNER // SSD_BC + 1))
    xbc = pl.BlockSpec((BLK, SSD_INNER + 2 * SSD_BC), lambda i: (ci(i), 0))
    col = pl.BlockSpec((BLK, SM_W), lambda i: (ci(i), 0))
    scal = pl.BlockSpec((SSD_H, 1, 1), lambda i: (0, 0, 0))
    nw = pl.BlockSpec((SSD_H, 1, SSD_P), lambda i: (0, 0, 0))
    st = pl.BlockSpec((SSD_H, 1, SSD_P, SSD_N), lambda i: (0, ci(i), 0, 0))
    return wide, bmat, cmat, xbc, col, scal, nw, st


def _ssd_fwd_call(xbc, z, small, dt_bias, a_log, d_skip, norm_w, shards=()):
    seq = z.shape[0]
    nc = seq // BLK
    ns = len(shards)
    wide, bmat, cmat, _, col, scal, nw, st = _ssd_specs(nc, False)

    def body(*refs):
        x_ref, b_ref, c_ref, z_ref, sm_ref, db_ref, al_ref, ds_ref, nw_ref = refs[:9]
        y_ref, st_ref = refs[9 + ns:11 + ns]
        h_scr = refs[11 + 2 * ns]
        i = pl.program_id(0)
        if ns:
            start, relay, finish = _gather_phases(refs[9:9 + ns], refs[11 + ns:11 + 2 * ns], *refs[12 + 2 * ns:])
            pl.when(i == 0)(start)
            pl.when(i == nc - 1)(relay)

        @pl.when(i == 0)
        def _():
            h_scr[...] = jnp.zeros_like(h_scr)

        h = h_scr[...]
        st_ref[:, 0] = h
        y, h_new = _ssd_chunk(_split_lanes(x_ref[...], SSD_H, SSD_P), _split_lanes(z_ref[...], SSD_H, SSD_P),
                              _split_lanes(b_ref[...], SSD_G, SSD_N), _split_lanes(c_ref[...], SSD_G, SSD_N),
                              _pick_cols(sm_ref[...], SM_DT, SSD_H), h, db_ref[...], al_ref[...], ds_ref[...],
                              nw_ref[...], _valid_col(i * BLK, BLK))
        y_ref[...] = _join_lanes(y)
        h_scr[...] = h_new
        if ns:
            pl.when(i == nc - 1)(finish)

    return pl.pallas_call(
        body, name="ssd_fwd", grid=(nc,),
        in_specs=[wide, bmat, cmat, wide, col, scal, scal, scal, nw] + [_ANY] * ns,
        out_specs=[wide, st] + [_ANY] * ns,
        out_shape=[jax.ShapeDtypeStruct((seq, SSD_INNER), F32),
                   jax.ShapeDtypeStruct((SSD_H, nc, SSD_P, SSD_N), F32)] + _gather_out_shapes(shards),
        scratch_shapes=[pltpu.VMEM((SSD_H, SSD_P, SSD_N), F32)] + (_gather_sems(ns) if ns else []),
        compiler_params=_cp(("arbitrary",)),
    )(xbc, xbc, xbc, z, small, dt_bias, a_log, d_skip, norm_w, *shards)


def _ssd_bwd_call(xbc, z, small, dt_bias, a_log, d_skip, norm_w, states, dy):
    seq = z.shape[0]
    nc = seq // BLK
    wide, bmat, cmat, xbc_spec, col, scal, nw, st = _ssd_specs(nc, True)

    def body(x_ref, b_ref, c_ref, z_ref, sm_ref, db_ref, al_ref, ds_ref, nw_ref, st_ref, dy_ref,
             dxbc_ref, dz_ref, dsm_ref, ddb_ref, dal_ref, dds_ref, dnw_ref, dh_scr):
        i = pl.program_id(0)

        @pl.when(i == 0)
        def _():
            dh_scr[...] = jnp.zeros_like(dh_scr)
            ddb_ref[...] = jnp.zeros_like(ddb_ref)
            dal_ref[...] = jnp.zeros_like(dal_ref)
            dds_ref[...] = jnp.zeros_like(dds_ref)
            dnw_ref[...] = jnp.zeros_like(dnw_ref)

        fn = functools.partial(_ssd_chunk, valid=_valid_col((nc - 1 - i) * BLK, BLK))
        _, vjp = jax.vjp(fn, _split_lanes(x_ref[...], SSD_H, SSD_P), _split_lanes(z_ref[...], SSD_H, SSD_P),
                         _split_lanes(b_ref[...], SSD_G, SSD_N), _split_lanes(c_ref[...], SSD_G, SSD_N),
                         _pick_cols(sm_ref[...], SM_DT, SSD_H), st_ref[:, 0], db_ref[...], al_ref[...], ds_ref[...],
                         nw_ref[...])
        dx, dz, dbm, dcm, ddt, dh, ddb, dal, dds, dnw = vjp((_split_lanes(dy_ref[...], SSD_H, SSD_P), dh_scr[...]))
        dxbc_ref[:, :SSD_INNER] = _join_lanes(dx)
        dxbc_ref[:, SSD_INNER:SSD_INNER + SSD_BC] = _join_lanes(dbm)
        dxbc_ref[:, SSD_INNER + SSD_BC:] = _join_lanes(dcm)
        dz_ref[...] = _join_lanes(dz)
        dsm_ref[...] = _widen(_spread_cols(ddt, SM_DT), dsm_ref.shape[1])
        dh_scr[...] = dh
        ddb_ref[...] += ddb
        dal_ref[...] += dal
        dds_ref[...] += dds
        dnw_ref[...] += dnw

    sshape = jax.ShapeDtypeStruct((SSD_H, 1, 1), F32)
    return pl.pallas_call(
        body, name="ssd_bwd", grid=(nc,),
        in_specs=[wide, bmat, cmat, wide, col, scal, scal, scal, nw, st, wide],
        out_specs=[xbc_spec, wide, pl.BlockSpec((BLK, small.shape[1]), lambda i: (nc - 1 - i, 0)), scal, scal, scal, nw],
        out_shape=[jax.ShapeDtypeStruct(xbc.shape, F32), jax.ShapeDtypeStruct(z.shape, F32),
                   jax.ShapeDtypeStruct(small.shape, F32), sshape, sshape, sshape,
                   jax.ShapeDtypeStruct((SSD_H, 1, SSD_P), F32)],
        scratch_shapes=[pltpu.VMEM((SSD_H, SSD_P, SSD_N), F32)],
        compiler_params=_cp(("arbitrary",)),
    )(xbc, xbc, xbc, z, small, dt_bias, a_log, d_skip, norm_w, states, dy)


@jax.custom_vjp
def ssd_core(xbc, z, small, dt_bias, a_log, d_skip, norm_w, shards):
    y, _, *gathered = _ssd_fwd_call(xbc, z, small, dt_bias, a_log, d_skip, norm_w, shards)
    return y, tuple(gathered)


def _ssd_core_fwd(*args):
    y, states, *gathered = _ssd_fwd_call(*args)
    return (y, tuple(gathered)), (*args[:-1], states, args[-1])


def _ssd_core_bwd(res, cts):
    *args, shards = res
    return (*_ssd_bwd_call(*args, cts[0]), tuple(jnp.zeros_like(s) for s in shards))


ssd_core.defvjp(_ssd_core_fwd, _ssd_core_bwd)


def _swa_block(q, km, kp, kc, vm, vp, vc, sink, n):
    rows = SWA_REP * BLK
    qs = q.reshape(rows, SWA_D) * (SWA_D ** -0.5)
    s = _lo_nt(qs, jnp.concatenate([km, kp, kc], axis=0))
    i = jnp.bitwise_and(_iota2(rows, 3 * BLK, 0), BLK - 1)
    col = _iota2(rows, 3 * BLK, 1)
    j = jnp.bitwise_and(col, BLK - 1)
    part = jnp.right_shift(col, 7)
    ok_m = (part == 0) & (j >= NPAD) & ((n >= 1) | (j <= i))
    ok_p = (part == 1) & (n >= 2) & (j > i)
    ok_c = (part == 2) & (n >= 1) & (j <= i)
    ok = ok_m | ok_p | ok_c
    s = jnp.where(ok, s, NEG)
    snk = jnp.concatenate([jnp.broadcast_to(sink[r], (BLK, 1)) for r in range(SWA_REP)], axis=0)
    m = lax.stop_gradient(jnp.maximum(jnp.max(s, axis=-1, keepdims=True), snk))
    e = jnp.exp(s - m)
    p = e / (jnp.sum(e, axis=-1, keepdims=True) + jnp.exp(snk - m))
    o = _lo_nn(p, jnp.concatenate([vm, vp, vc], axis=0))
    return o.reshape(SWA_REP, BLK, SWA_D)


SWA_QW = SWA_QH * SWA_D
SWA_KW = SWA_KVH * SWA_D


def _swa_specs(nb, rev):
    ci = (lambda i: nb - 1 - i) if rev else (lambda i: i)
    qsp = pl.BlockSpec((BLK, SWA_QW), lambda i: (ci(i), 0))
    cur = pl.BlockSpec((BLK, 2 * SWA_KW), lambda i: (ci(i), 0))
    prev = pl.BlockSpec((BLK, 2 * SWA_KW), lambda i: (jnp.maximum(ci(i) - 1, 0), 0))
    meta = pl.BlockSpec((BLK, 2 * SWA_KW), lambda i: (0, 0))
    scal = pl.BlockSpec((SWA_QH, 1, 1), lambda i: (0, 0, 0))
    return qsp, cur, prev, meta, scal


def _swa_by_head(q, kvm, kvp, kvc, sink):
    def kv(t):
        return _split_lanes(t[:, :SWA_KW], SWA_KVH, SWA_D), _split_lanes(t[:, SWA_KW:], SWA_KVH, SWA_D)

    (km, vm), (kp, vp), (kc, vc) = kv(kvm), kv(kvp), kv(kvc)
    qh = _split_lanes(q, SWA_QH, SWA_D).reshape(SWA_KVH, SWA_REP, BLK, SWA_D)
    return qh, km, kp, kc, vm, vp, vc, sink.reshape(SWA_KVH, SWA_REP, 1, 1)


def _swa_kv_tile(dk, dv):
    return jnp.concatenate([_join_lanes(dk), _join_lanes(dv)], axis=1)


def _swa_fwd_call(q, kv, sink):
    seq = q.shape[0]
    nb = seq // BLK
    qsp, cur, prev, meta, scal = _swa_specs(nb, False)

    def body(q_ref, m_ref, p_ref, c_ref, s_ref, o_ref):
        fn = jax.vmap(functools.partial(_swa_block, n=pl.program_id(0)))
        o = fn(*_swa_by_head(q_ref[...], m_ref[...], p_ref[...], c_ref[...], s_ref[...]))
        o_ref[...] = _join_lanes(o.reshape(SWA_QH, BLK, SWA_D))

    return pl.pallas_call(
        body, name="swa_fwd", grid=(nb,),
        in_specs=[qsp, meta, prev, cur, scal],
        out_specs=qsp,
        out_shape=jax.ShapeDtypeStruct(q.shape, F32),
        compiler_params=_cp(("parallel",)),
    )(q, kv, kv, kv, sink)


def _swa_bwd_call(q, kv, sink, do):
    seq = q.shape[0]
    nb = seq // BLK
    qsp, cur, prev, meta, scal = _swa_specs(nb, True)

    def body(q_ref, m_ref, p_ref, c_ref, s_ref, do_ref, dq_ref, dkv_ref, ds_ref, prev_scr, meta_scr):
        i = pl.program_id(0)
        n = nb - 1 - i

        @pl.when(i == 0)
        def _():
            prev_scr[...] = jnp.zeros_like(prev_scr)
            meta_scr[...] = jnp.zeros_like(meta_scr)
            ds_ref[...] = jnp.zeros_like(ds_ref)

        fn = jax.vmap(functools.partial(_swa_block, n=n))
        _, vjp = jax.vjp(fn, *_swa_by_head(q_ref[...], m_ref[...], p_ref[...], c_ref[...], s_ref[...]))
        do = _split_lanes(do_ref[...], SWA_QH, SWA_D).reshape(SWA_KVH, SWA_REP, BLK, SWA_D)
        dq, dkm, dkp, dkc, dvm, dvp, dvc, dsk = vjp(do)
        dq_ref[...] = _join_lanes(dq.reshape(SWA_QH, BLK, SWA_D))
        ds_ref[...] += dsk.reshape(SWA_QH, 1, 1)
        meta_scr[...] += _swa_kv_tile(dkm, dvm)
        first = (n == 0).astype(F32)
        dkv_ref[...] = _swa_kv_tile(dkc, dvc) + prev_scr[...] + first * meta_scr[...]
        prev_scr[...] = _swa_kv_tile(dkp, dvp)

    return pl.pallas_call(
        body, name="swa_bwd", grid=(nb,),
        in_specs=[qsp, meta, prev, cur, scal, qsp],
        out_specs=[qsp, cur, scal],
        out_shape=[jax.ShapeDtypeStruct(q.shape, F32), jax.ShapeDtypeStruct(kv.shape, F32),
                   jax.ShapeDtypeStruct(sink.shape, F32)],
        scratch_shapes=[pltpu.VMEM((BLK, 2 * SWA_KW), F32)] * 2,
        compiler_params=_cp(("arbitrary",)),
    )(q, kv, kv, kv, sink, do)


@jax.custom_vjp
def swa_core(q, kv, sink):
    return _swa_fwd_call(q, kv, sink)


def _swa_core_fwd(q, kv, sink):
    return _swa_fwd_call(q, kv, sink), (q, kv, sink)


def _swa_core_bwd(res, do):
    return tuple(_swa_bwd_call(*res, do))


swa_core.defvjp(_swa_core_fwd, _swa_core_bwd)


def _tile(n, pref):
    if n <= pref:
        return n
    best = None
    for t in range(128, pref + 1, 128):
        if n % t == 0:
            best = t
    assert best is not None, (n, pref)
    return best


MM_TILE_BYTES = 9 * 1024 * 1024


def _mm_tiles(m, n, kk, a_bytes):
    if kk > 8192:
        return _tile(m, 2816 // a_bytes), _tile(n, 512), _tile(kk, 4096)
    if kk > 1408 and _tile(m, 1024) * kk * a_bytes <= MM_TILE_BYTES:
        return _tile(m, 1024), _tile(n, 512), kk
    return _tile(m, 1408), _tile(n, 512), _tile(kk, 1408)


def _mm_call(a, b, name):
    (m, kk), n = a.shape, b.shape[1]
    tm, tn, tk = _mm_tiles(m, n, kk, a.dtype.itemsize)
    nk = kk // tk
    a_spec = pl.BlockSpec((tm, tk), lambda i, j, k: (i, k))
    b_spec = pl.BlockSpec((tk, tn), lambda i, j, k: (k, j))

    def body(a_ref, b_ref, o_ref, acc_ref):
        k = pl.program_id(2)
        part = jnp.dot(a_ref[...].astype(BF16), b_ref[...].astype(BF16), preferred_element_type=F32)

        @pl.when(k == 0)
        def _():
            acc_ref[...] = part

        @pl.when(k > 0)
        def _():
            acc_ref[...] += part

        @pl.when(k == nk - 1)
        def _():
            o_ref[...] = acc_ref[...]

    return pl.pallas_call(
        body, name=name, grid=(m // tm, n // tn, nk),
        in_specs=[a_spec, b_spec],
        out_specs=pl.BlockSpec((tm, tn), lambda i, j, k: (i, j)),
        out_shape=jax.ShapeDtypeStruct((m, n), F32),
        scratch_shapes=[pltpu.VMEM((tm, tn), F32)],
        compiler_params=_cp(("parallel", "parallel", "arbitrary")),
    )(a, b)


@jax.custom_vjp
def mm(a, b, b_t, grad_slot):
    return _mm_call(a, b, "mm_fwd")


def _mm_fwd(a, b, b_t, grad_slot):
    return _mm_call(a, b, "mm_fwd"), (a, b, b_t)


def _mm_bwd(res, dc):
    a, b, b_t = res
    return (_mm_call(dc, b_t, "mm_dx"), jnp.zeros_like(b), jnp.zeros_like(b_t),
            _mm_call(a.astype(BF16).T, dc, "mm_dw"))


mm.defvjp(_mm_fwd, _mm_bwd)


_SPLIT = (1024, 1024, 1024, 1024, 1024, 2048, 1024, 512, 3072, 512)


def _split_cols(u):
    offs = [sum(_SPLIT[:i]) for i in range(len(_SPLIT))]
    return tuple(u[:, o:o + s] for o, s in zip(offs, _SPLIT))


@jax.custom_vjp
def mm_split(a, b, b_t, grad_slot):
    return _split_cols(_mm_call(a, b, "mm_fwd"))


def _mm_split_fwd(a, b, b_t, grad_slot):
    return _split_cols(_mm_call(a, b, "mm_fwd")), (a, b, b_t)


def _mm_split_bwd(res, cts):
    return _mm_bwd(res, jnp.concatenate([c.astype(BF16) for c in cts], axis=1))


mm_split.defvjp(_mm_split_fwd, _mm_split_bwd)


def _row_specs(arrs, tr):
    return [pl.BlockSpec((tr, a.shape[1]), lambda i: (i, 0)) for a in arrs]


def _par_specs(arrs):
    return [pl.BlockSpec(a.shape, lambda i: (0, 0)) for a in arrs]


def _row_fwd_call(fn, rows, params, out_cols, tr, name):
    seq = rows[0].shape[0]
    nr = len(rows)

    def body(*refs):
        vals = [r[...] for r in refs[:-1]]
        refs[-1][...] = fn(*vals)

    return pl.pallas_call(
        body, name=name, grid=(seq // tr,),
        in_specs=_row_specs(rows, tr) + _par_specs(params),
        out_specs=pl.BlockSpec((tr, out_cols), lambda i: (i, 0)),
        out_shape=jax.ShapeDtypeStruct((seq, out_cols), F32),
        compiler_params=_cp(("parallel",)),
    )(*rows, *params)


def _row_bwd_call(fn, rows, params, dy, tr, name):
    seq = rows[0].shape[0]
    nr, npar = len(rows), len(params)

    def body(*refs):
        ins = refs[:nr + npar]
        dy_ref = refs[nr + npar]
        outs = refs[nr + npar + 1:]
        _, vjp = jax.vjp(fn, *[r[...] for r in ins])
        cts = vjp(dy_ref[...])
        for o_ref, ct in zip(outs[:nr], cts[:nr]):
            o_ref[...] = ct

        @pl.when(pl.program_id(0) == 0)
        def _():
            for o_ref in outs[nr:]:
                o_ref[...] = jnp.zeros_like(o_ref)

        for o_ref, ct in zip(outs[nr:], cts[nr:]):
            o_ref[...] += ct

    return pl.pallas_call(
        body, name=name, grid=(seq // tr,),
        in_specs=_row_specs(rows, tr) + _par_specs(params) + _row_specs([dy], tr),
        out_specs=_row_specs(rows, tr) + _par_specs(params),
        out_shape=[jax.ShapeDtypeStruct(a.shape, F32) for a in (*rows, *params)],
        compiler_params=_cp(("arbitrary",)),
    )(*rows, *params, dy)


def _make_rowop(fn, nrows, out_cols, tr, name):
    @jax.custom_vjp
    def op(*args):
        return _row_fwd_call(fn, args[:nrows], args[nrows:], out_cols, tr, name + "_fwd")

    def fwd(*args):
        return op(*args), args

    def bwd(args, dy):
        return tuple(_row_bwd_call(fn, args[:nrows], args[nrows:], dy, tr, name + "_bwd"))

    op.defvjp(fwd, bwd)
    return op


def _rms_fn(x, w):
    return x * lax.rsqrt(jnp.mean(x * x, axis=-1, keepdims=True) + RMS_EPS) * w


def _merge_fn(pa, pb, pc, gl):
    d = D_MODEL
    return (jax.nn.sigmoid(gl[:, :d]) * pa + jax.nn.sigmoid(gl[:, d:2 * d]) * pb
            + jax.nn.sigmoid(gl[:, 2 * d:]) * pc)


def _relu2_fn(a):
    r = jnp.maximum(a, 0.0)
    return r * r


rms_op = _make_rowop(_rms_fn, 1, D_MODEL, 384, "rms")
merge_op = _make_rowop(_merge_fn, 4, D_MODEL, 192, "merge")
relu2_op = _make_rowop(_relu2_fn, 1, D_FF, 192, "relu2")


def _conv_taps(xext, w, nrows):
    z = None
    for j in range(CONV_K):
        sh = CONV_K - 1 - j
        xs = pltpu.roll(xext, sh, 0) if sh else xext
        term = w[j:j + 1, :] * xs[8:8 + nrows, :]
        z = term if z is None else z + term
    return z


def _halo(ref, start, ok):
    return jnp.where(ok, ref[pl.ds(pl.multiple_of(start, 8), 8), :], 0.0)


def _conv_fwd_call(x, w, b):
    seq, ch = x.shape
    nb = seq // BLK

    def body(x_ref, w_ref, b_ref, o_ref):
        w = w_ref[...]
        bias = b_ref[...]

        def step(i, carry):
            r0 = pl.multiple_of(i * BLK, BLK)
            xext = jnp.concatenate([_halo(x_ref, jnp.maximum(r0 - 8, 0), i > 0), x_ref[pl.ds(r0, BLK), :]], axis=0)
            o_ref[pl.ds(r0, BLK), :] = _silu(_conv_taps(xext, w, BLK) + bias)
            return carry

        lax.fori_loop(0, nb, step, 0)

    strip = pl.BlockSpec((seq, 128), lambda c: (0, c))
    return pl.pallas_call(
        body, name="conv_fwd", grid=(ch // 128,),
        in_specs=[strip, pl.BlockSpec((CONV_K, 128), lambda c: (0, c)), pl.BlockSpec((1, 128), lambda c: (0, c))],
        out_specs=strip, out_shape=jax.ShapeDtypeStruct(x.shape, F32),
        compiler_params=_cp(("parallel",)),
    )(x, w, b)


def _conv_bwd_call(x, w, b, dy):
    seq, ch = x.shape
    nb = seq // BLK

    def body(x_ref, w_ref, b_ref, dy_ref, dx_ref, dw_ref, db_ref):
        w = w_ref[...]
        bias = b_ref[...]

        def step(i, carry):
            r0 = pl.multiple_of(i * BLK, BLK)
            last = i == nb - 1
            nxt = jnp.minimum(r0 + BLK, seq - 8)
            xext = jnp.concatenate([_halo(x_ref, jnp.maximum(r0 - 8, 0), i > 0), x_ref[pl.ds(r0, BLK), :],
                                    _halo(x_ref, nxt, jnp.logical_not(last))], axis=0)
            dyext = jnp.concatenate([dy_ref[pl.ds(r0, BLK), :], _halo(dy_ref, nxt, jnp.logical_not(last))], axis=0)
            z = _conv_taps(xext, w, BLK + 8) + bias
            sg = jax.nn.sigmoid(z)
            dz = dyext * (sg * (1.0 + z * (1.0 - sg)))
            dx = None
            for j in range(CONV_K):
                sh = CONV_K - 1 - j
                dzs = pltpu.roll(dz, BLK + 8 - sh, 0) if sh else dz
                term = w[j:j + 1, :] * dzs[:BLK, :]
                dx = term if dx is None else dx + term
            dx_ref[pl.ds(r0, BLK), :] = dx
            dzm = dz[:BLK, :]
            out = []
            for j in range(CONV_K):
                sh = CONV_K - 1 - j
                xs = pltpu.roll(xext, sh, 0) if sh else xext
                out.append(carry[j] + jnp.sum(dzm * xs[8:8 + BLK, :], axis=0, keepdims=True))
            out.append(carry[CONV_K] + jnp.sum(dzm, axis=0, keepdims=True))
            return tuple(out)

        zero = jnp.zeros((1, 128), F32)
        acc = lax.fori_loop(0, nb, step, (zero,) * (CONV_K + 1))
        dw_ref[...] = jnp.concatenate(acc[:CONV_K], axis=0)
        db_ref[...] = acc[CONV_K]

    strip = pl.BlockSpec((seq, 128), lambda c: (0, c))
    wsp = pl.BlockSpec((CONV_K, 128), lambda c: (0, c))
    bsp = pl.BlockSpec((1, 128), lambda c: (0, c))
    return pl.pallas_call(
        body, name="conv_bwd", grid=(ch // 128,),
        in_specs=[strip, wsp, bsp, strip],
        out_specs=[strip, wsp, bsp],
        out_shape=[jax.ShapeDtypeStruct(x.shape, F32), jax.ShapeDtypeStruct(w.shape, F32),
                   jax.ShapeDtypeStruct(b.shape, F32)],
        compiler_params=_cp(("parallel",)),
    )(x, w, b, dy)


@jax.custom_vjp
def conv_silu(x, w, b):
    return _conv_fwd_call(x, w, b)


def _conv_silu_fwd(x, w, b):
    return _conv_fwd_call(x, w, b), (x, w, b)


def _conv_silu_bwd(res, dy):
    return tuple(_conv_bwd_call(*res, dy))


conv_silu.defvjp(_conv_silu_fwd, _conv_silu_bwd)


def _loss_call(h, wf, target):
    seq, d = h.shape
    nb = seq // BLK

    def body(h_ref, w_ref, t_ref, loss_ref, dh_ref, dw_ref):
        i = pl.program_id(0)
        live = (i > 0).astype(F32)
        tgt = t_ref[...]

        def fn(hh, ww):
            err = _rms_fn(hh, ww) - tgt
            return 0.5 * live * jnp.sum(jnp.mean(err * err, axis=-1, keepdims=True), axis=0, keepdims=True)

        val, vjp = jax.vjp(fn, h_ref[...], w_ref[...])
        dh, dw = vjp(jnp.ones((1, 1), F32))
        dh_ref[...] = dh

        @pl.when(i == 0)
        def _():
            loss_ref[...] = jnp.zeros_like(loss_ref)
            dw_ref[...] = jnp.zeros_like(dw_ref)

        loss_ref[...] += val
        dw_ref[...] += dw

    return pl.pallas_call(
        body, name="loss_head", grid=(nb,),
        in_specs=[pl.BlockSpec((BLK, d), lambda i: (i, 0)), pl.BlockSpec((1, d), lambda i: (0, 0)),
                  pl.BlockSpec((BLK, d), lambda i: (jnp.maximum(i - 1, 0), 0))],
        out_specs=[pl.BlockSpec((1, 1), lambda i: (0, 0)), pl.BlockSpec((BLK, d), lambda i: (i, 0)),
                   pl.BlockSpec((1, d), lambda i: (0, 0))],
        out_shape=[jax.ShapeDtypeStruct((1, 1), F32), jax.ShapeDtypeStruct(h.shape, F32),
                   jax.ShapeDtypeStruct((1, d), F32)],
        compiler_params=_cp(("arbitrary",)),
    )(h, wf, target)


def _make_loss_head(target):
    @jax.custom_vjp
    def head(h, wf):
        return _loss_call(h, wf, target)[0][0, 0]

    def fwd(h, wf):
        loss, dh, dw = _loss_call(h, wf, target)
        return loss[0, 0], (dh, dw)

    def bwd(res, g):
        return g * res[0], g * res[1]

    head.defvjp(fwd, bwd)
    return head


_IN_SEGS = (("q", 0, 1024), ("k", 1024, 1024), ("v", 2048, 1024), ("gate", 3072, 1024), ("z", 4112, 1024),
            ("xbc", 5136, 2048), ("cq", 7200, 1024), ("ck", 8224, 256), ("cv", 8480, 256), ("gl", 8736, 3072),
            ("b", 4096, 8), ("a", 4104, 8), ("dt", 7184, 16))
_IN_PAD = sum(_SPLIT) - sum(n for _, _, n in _IN_SEGS)


_MATMUL = ("w_in", "w_proj_gdn", "w_proj_ssd", "w_proj_swa", "w_out", "w_up", "w_down")
_LATE = _MATMUL[1:]


def _late_weights(gathered):
    g = dict(zip(_LATE, gathered))
    full = {n: g[n].reshape(D_MODEL, D_MODEL) for n in _LATE[:4]}
    full["w_up"] = g["w_up"].transpose(1, 0, 2).reshape(D_MODEL, D_FF)
    full["w_down"] = g["w_down"].reshape(D_FF, D_MODEL)
    full.update({n + "_t": t.T for n, t in list(full.items())})
    return full


def _layer(h, p, w_in, w_in_t, slot, late_shards, next_shards=(), exchange_slots=()):
    wb = {"w_in": w_in, "w_in_t": w_in_t}

    def proj(t, name):
        return mm(t, wb[name], wb[name + "_t"], slot[name])

    q_pre, k_pre, v_pre, gate, z, xbc_pre, cq, ckv, gl, small = mm_split(
        rms_op(h, p["norm1_w"].reshape(1, -1)), w_in, w_in_t, slot["w_in"])

    gcw = p["gdn_conv_w"]
    nob = jnp.zeros((1, GDN_H * GDN_D), F32)
    qa = conv_silu(q_pre, gcw[:, :1024], nob)
    ka = conv_silu(k_pre, gcw[:, 1024:2048], nob)
    va = conv_silu(v_pre, gcw[:, 2048:], nob)
    y_gdn, late, placeholders = gdn_core(
        qa, ka, va, gate, small, p["gdn_a_log"].reshape(GDN_H, 1, 1), p["gdn_dt_bias"].reshape(GDN_H, 1, 1),
        p["gdn_norm_w"].reshape(1, GDN_D), tuple(late_shards), tuple(exchange_slots))
    wb.update(_late_weights(late))

    xbc = conv_silu(xbc_pre, p["ssd_conv_w"], p["ssd_conv_b"].reshape(1, -1))
    y_ssd, gathered = ssd_core(xbc, z, small, p["ssd_dt_bias"].reshape(SSD_H, 1, 1),
                               p["ssd_a_log"].reshape(SSD_H, 1, 1), p["ssd_d"].reshape(SSD_H, 1, 1),
                               p["ssd_norm_w"].reshape(SSD_H, 1, SSD_P), tuple(next_shards))

    y_swa = swa_core(cq, ckv, p["swa_sinks"].reshape(SWA_QH, 1, 1))

    merged = merge_op(proj(y_gdn, "w_proj_gdn"), proj(y_ssd, "w_proj_ssd"), proj(y_swa, "w_proj_swa"), gl)
    h = h + proj(merged, "w_out")
    a1 = proj(rms_op(h, p["norm2_w"].reshape(1, -1)), "w_up")
    return h + proj(relu2_op(a1), "w_down"), gathered, placeholders


_PER_LAYER = ("norm1_w", "gdn_conv_w", "gdn_a_log", "gdn_dt_bias", "gdn_norm_w", "ssd_conv_w", "ssd_conv_b",
              "ssd_dt_bias", "ssd_a_log", "ssd_d", "ssd_norm_w", "swa_sinks", "norm2_w")


def _embed(x, meta):
    return jnp.concatenate([jnp.zeros((NPAD, D_MODEL), F32), meta, x], axis=0)


_IN_SHARD = 1476


def _in_pieces():
    out = []
    for _, s, n in _IN_SEGS:
        c = s
        while c < s + n:
            d = c // _IN_SHARD
            e = min(s + n, (d + 1) * _IN_SHARD)
            out.append((d, c - d * _IN_SHARD, e - d * _IN_SHARD))
            c = e
    return out


def _in_pieces_back():
    start, off = {}, 0
    for _, s, n in _IN_SEGS:
        start[s] = off
        off += n
    out = [[] for _ in range(N_DEV)]
    for _, s, n in sorted(_IN_SEGS, key=lambda t: t[1]):
        c = s
        while c < s + n:
            d = c // _IN_SHARD
            e = min(s + n, (d + 1) * _IN_SHARD)
            out[d].append((start[s] + c - s, start[s] + e - s))
            c = e
    return out


def _regroup_w_in(stacked):
    parts = [stacked[d, :, lo:hi] for d, lo, hi in _in_pieces()]
    return jnp.concatenate(parts + [jnp.zeros((D_MODEL, _IN_PAD), stacked.dtype)], axis=1)


def _ungroup_w_in(g):
    return [jnp.concatenate([g[:, lo:hi] for lo, hi in pieces], axis=1) for pieces in _in_pieces_back()]


def _position():
    return lax.axis_index("x"), lax.axis_index("y"), lax.axis_index("c")


_ANY = pl.BlockSpec(memory_space=pl.ANY)


def _chip_of(x, y, k):
    return (1 - x if k & 1 else x, 1 - y if k & 2 else y)


def _allgather_call(shards, name):
    n = len(shards)

    def body(*refs):
        start, relay, finish = _gather_phases(refs[:n], refs[n:2 * n], *refs[2 * n:])
        start()
        relay()
        finish()

    return pl.pallas_call(
        body, name=name,
        out_shape=_gather_out_shapes(shards),
        in_specs=[_ANY] * n, out_specs=[_ANY] * n,
        scratch_shapes=_gather_sems(n),
    )(*shards)


def _gather_out_shapes(shards):
    return [jax.ShapeDtypeStruct((N_DEV, *s.shape), s.dtype) for s in shards]


def _gather_sems(n):
    return [pltpu.SemaphoreType.DMA((7 * n,)), pltpu.SemaphoreType.DMA((7 * n,)), pltpu.SemaphoreType.DMA((n,))]


def _gather_phases(x_refs, out_refs, send_sems, recv_sems, local_sems):
    n = len(x_refs)
    x, y, c = _position()
    me, sibling = (x, y, c), (x, y, 1 - c)
    chips = [_chip_of(x, y, k) for k in (1, 2, 3)]

    def slab(a, px, py, pc):
        return out_refs[a].at[4 * px + 2 * py + pc]

    def copy(a, k, block, to, src=None):
        return pltpu.make_async_remote_copy(
            src_ref=slab(a, *block) if src is None else src, dst_ref=slab(a, *block),
            send_sem=send_sems.at[7 * a + k], recv_sem=recv_sems.at[7 * a + k], device_id=to, device_id_type=MESH)

    def mine():
        return [pltpu.make_async_copy(x_refs[a], slab(a, *me), local_sems.at[a]) for a in range(n)]

    def first():
        out = []
        for a in range(n):
            out.append(copy(a, 0, me, sibling, src=x_refs[a]))
            out += [copy(a, 1 + j, me, (*chip, c), src=x_refs[a]) for j, chip in enumerate(chips)]
        return out

    def passed():
        return [copy(a, 4 + j, (*chip, c), sibling) for j, chip in enumerate(chips) for a in range(n)]

    def start():
        for cp in mine() + first():
            cp.start()

    def relay():
        for j, chip in enumerate(chips):
            for a in range(n):
                copy(a, 1 + j, (*chip, c), me).wait_recv()
                copy(a, 4 + j, (*chip, c), sibling).start()

    def finish():
        for a in range(n):
            copy(a, 0, sibling, me).wait_recv()
        for j, chip in enumerate(chips):
            for a in range(n):
                copy(a, 4 + j, (*chip, 1 - c), me).wait_recv()
        for cp in first() + passed():
            cp.wait_send()
        for cp in mine():
            cp.wait()

    return start, relay, finish


def _sibling_exchange_call(for_c0, for_c1, name):
    n = len(for_c0)

    def body(*refs):
        c0_refs, c1_refs, out_refs = refs[:n], refs[n:2 * n], refs[2 * n:3 * n]
        send_sems, recv_sems = refs[3 * n:]
        x, y, c = _position()

        def copies(src_refs):
            return [pltpu.make_async_remote_copy(
                src_ref=src_refs[a].at[q], dst_ref=out_refs[a].at[q],
                send_sem=send_sems.at[4 * a + q], recv_sem=recv_sems.at[4 * a + q],
                device_id=(x, y, 1 - c), device_id_type=MESH) for a in range(n) for q in range(4)]

        @pl.when(c == 0)
        def _():
            for cp in copies(c1_refs):
                cp.start()

        @pl.when(c == 1)
        def _():
            for cp in copies(c0_refs):
                cp.start()

        waits = copies(c0_refs)
        for cp in waits:
            cp.wait_recv()
        for cp in waits:
            cp.wait_send()

    return pl.pallas_call(
        body, name=name,
        out_shape=[jax.ShapeDtypeStruct(g.shape, g.dtype) for g in for_c0],
        in_specs=[_ANY] * (2 * n), out_specs=[_ANY] * n,
        scratch_shapes=[pltpu.SemaphoreType.DMA((4 * n,)), pltpu.SemaphoreType.DMA((4 * n,))],
    )(*for_c0, *for_c1)


def _chip_exchange_call(partials, name):
    n = len(partials)

    def body(*refs):
        start, finish = _chip_exchange_phases(refs[:n], refs[n:2 * n], *refs[2 * n:])
        start()
        finish()

    return pl.pallas_call(
        body, name=name,
        out_shape=_chip_exchange_out_shapes(partials),
        in_specs=[_ANY] * n, out_specs=[_ANY] * n,
        scratch_shapes=_chip_exchange_sems(n),
    )(*partials)


def _chip_exchange_out_shapes(partials):
    return [jax.ShapeDtypeStruct((3, *p.shape[1:]), p.dtype) for p in partials]


def _chip_exchange_sems(n):
    return [pltpu.SemaphoreType.DMA((3 * n,)), pltpu.SemaphoreType.DMA((3 * n,))]


def _chip_exchange_phases(p_refs, out_refs, send_sems, recv_sems):
    n = len(p_refs)
    x, y, c = _position()

    def copies():
        out = []
        for a in range(n):
            for k in (1, 2, 3):
                px, py = _chip_of(x, y, k)
                out.append(pltpu.make_async_remote_copy(
                    src_ref=p_refs[a].at[2 * px + py], dst_ref=out_refs[a].at[k - 1],
                    send_sem=send_sems.at[3 * a + k - 1], recv_sem=recv_sems.at[3 * a + k - 1],
                    device_id=(px, py, c), device_id_type=MESH))
        return out

    def start():
        for cp in copies():
            cp.start()

    def finish():
        for cp in copies():
            cp.wait_recv()
        for cp in copies():
            cp.wait_send()

    return start, finish


def _chip_partial_call(for_c0, for_c1, sib, tr, name):
    _, r, c = sib.shape

    def body(c0_ref, c1_ref, s_ref, own_ref, out_ref):
        x, y, core = _position()
        mine = jnp.where(core == 0, c0_ref[...], c1_ref[...])
        partial = mine + s_ref[...]
        own = jnp.zeros((tr, c), F32)
        for q in range(4):
            own = jnp.where(2 * x + y == q, partial[q], own)
        own_ref[...] = own
        out_ref[...] = partial.astype(BF16)

    four = pl.BlockSpec((4, tr, c), lambda i: (0, i, 0))
    return pl.pallas_call(
        body, name=name, grid=(r // tr,),
        in_specs=[four, four, four],
        out_specs=[pl.BlockSpec((tr, c), lambda i: (i, 0)), four],
        out_shape=[jax.ShapeDtypeStruct((r, c), F32), jax.ShapeDtypeStruct((4, r, c), BF16)],
        compiler_params=_cp(("parallel",)),
    )(for_c0, for_c1, sib)


def _adamw_call(parts, w, m, v, tr, name):
    ns, r, c = w.shape
    counts = [len(p) for p in parts]
    flat_parts = [a for p in parts for a in p]

    def body(*refs):
        p_refs = refs[:len(flat_parts)]
        w_ref, m_ref, v_ref, g_ref, d_ref, nm_ref, nv_ref = refs[len(flat_parts):]
        at = 0
        for s in range(ns):
            g = None
            for p_ref in p_refs[at:at + counts[s]]:
                for j in range(p_ref.shape[0]):
                    term = p_ref[j].astype(F32)
                    g = term if g is None else g + term
            at += counts[s]
            nm = ADAM_B1 * m_ref[s] + (1.0 - ADAM_B1) * g
            nv = ADAM_B2 * v_ref[s] + (1.0 - ADAM_B2) * (g * g)
            m_hat = nm / (1.0 - ADAM_B1 ** ADAM_STEP)
            v_hat = nv / (1.0 - ADAM_B2 ** ADAM_STEP)
            g_ref[s] = g
            d_ref[s] = -ADAM_LR * (m_hat / (jnp.sqrt(v_hat) + ADAM_EPS) + ADAM_WD * w_ref[s])
            nm_ref[s] = nm
            nv_ref[s] = nv

    slabs = pl.BlockSpec((ns, tr, c), lambda i: (0, i, 0))
    return pl.pallas_call(
        body, name=name, grid=(r // tr,),
        in_specs=[pl.BlockSpec((a.shape[0], tr, c), lambda i: (0, i, 0)) for a in flat_parts] + [slabs] * 3,
        out_specs=[slabs] * 4,
        out_shape=[jax.ShapeDtypeStruct((ns, r, c), F32)] * 4,
        compiler_params=_cp(("parallel",)),
    )(*flat_parts, w, m, v)


_WEIGHTS = ("meta_tokens", "norm1_w", "w_in", "gdn_conv_w", "gdn_a_log", "gdn_dt_bias", "gdn_norm_w", "ssd_conv_w",
            "ssd_conv_b", "ssd_dt_bias", "ssd_a_log", "ssd_d", "ssd_norm_w", "swa_sinks", "w_proj_gdn", "w_proj_ssd",
            "w_proj_swa", "w_out", "norm2_w", "w_up", "w_down", "final_norm_w")
_SHARD_AXIS = {"meta_tokens": 1, "w_in": 2, "gdn_conv_w": 2, "ssd_conv_w": 2, "w_proj_gdn": 1, "w_proj_ssd": 1,
               "w_proj_swa": 1, "w_out": 1, "w_up": 2, "w_down": 1}
_BIG = tuple(n for n in _WEIGHTS if n in _SHARD_AXIS)
_SMALL = tuple(n for n in _WEIGHTS if n not in _SHARD_AXIS)
FLAT_C = 1024


def _pack(arrs, rows, lead=()):
    flat = jnp.concatenate([a.reshape(*lead, -1) for a in arrs], axis=-1)
    pad = rows * FLAT_C - flat.shape[-1]
    flat = jnp.pad(flat, [(0, 0)] * len(lead) + [(0, pad)])
    return flat.reshape(*lead, rows, FLAT_C)


def _unpack(flat, shapes, lead=()):
    flat = flat.reshape(*lead, -1)
    out, off = [], 0
    for s in shapes:
        n = math.prod(s)
        out.append(flat[..., off:off + n].reshape(*lead, *s))
        off += n
    return out


def _rows_for(shapes):
    n = sum(math.prod(s) for s in shapes)
    return -(-n // (FLAT_C * 8)) * 8


def _rows_tile(r, c):
    if r <= 256:
        return r
    return 128 if c > 1024 else 256


def _join(stacked, axis):
    moved = jnp.moveaxis(stacked, 0, axis)
    return moved.reshape(*moved.shape[:axis], -1, *moved.shape[axis + 2:])


def _unjoin(full, axis):
    cut = full.reshape(*full.shape[:axis], N_DEV, full.shape[axis] // N_DEV, *full.shape[axis + 1:])
    return jnp.moveaxis(cut, axis, 0)


def kernel(x, meta_tokens, norm1_w, w_in, gdn_conv_w, gdn_a_log, gdn_dt_bias, gdn_norm_w, ssd_conv_w, ssd_conv_b,
           ssd_dt_bias, ssd_a_log, ssd_d, ssd_norm_w, swa_sinks, w_proj_gdn, w_proj_ssd, w_proj_swa, w_out, norm2_w,
           w_up, w_down, final_norm_w, loss_target, m_meta_tokens, m_norm1_w, m_w_in, m_gdn_conv_w, m_gdn_a_log,
           m_gdn_dt_bias, m_gdn_norm_w, m_ssd_conv_w, m_ssd_conv_b, m_ssd_dt_bias, m_ssd_a_log, m_ssd_d, m_ssd_norm_w,
           m_swa_sinks, m_w_proj_gdn, m_w_proj_ssd, m_w_proj_swa, m_w_out, m_norm2_w, m_w_up, m_w_down,
           m_final_norm_w, v_meta_tokens, v_norm1_w, v_w_in, v_gdn_conv_w, v_gdn_a_log, v_gdn_dt_bias, v_gdn_norm_w,
           v_ssd_conv_w, v_ssd_conv_b, v_ssd_dt_bias, v_ssd_a_log, v_ssd_d, v_ssd_norm_w, v_swa_sinks, v_w_proj_gdn,
           v_w_proj_ssd, v_w_proj_swa, v_w_out, v_norm2_w, v_w_up, v_w_down, v_final_norm_w):
    args = (meta_tokens, norm1_w, w_in, gdn_conv_w, gdn_a_log, gdn_dt_bias, gdn_norm_w, ssd_conv_w, ssd_conv_b,
            ssd_dt_bias, ssd_a_log, ssd_d, ssd_norm_w, swa_sinks, w_proj_gdn, w_proj_ssd, w_proj_swa, w_out, norm2_w,
            w_up, w_down, final_norm_w, m_meta_tokens, m_norm1_w, m_w_in, m_gdn_conv_w, m_gdn_a_log,
            m_gdn_dt_bias, m_gdn_norm_w, m_ssd_conv_w, m_ssd_conv_b, m_ssd_dt_bias, m_ssd_a_log, m_ssd_d, m_ssd_norm_w,
            m_swa_sinks, m_w_proj_gdn, m_w_proj_ssd, m_w_proj_swa, m_w_out, m_norm2_w, m_w_up, m_w_down,
            m_final_norm_w, v_meta_tokens, v_norm1_w, v_w_in, v_gdn_conv_w, v_gdn_a_log, v_gdn_dt_bias, v_gdn_norm_w,
            v_ssd_conv_w, v_ssd_conv_b, v_ssd_dt_bias, v_ssd_a_log, v_ssd_d, v_ssd_norm_w, v_swa_sinks, v_w_proj_gdn,
            v_w_proj_ssd, v_w_proj_swa, v_w_out, v_norm2_w, v_w_up, v_w_down, v_final_norm_w)
    nw = len(_WEIGHTS)
    w = dict(zip(_WEIGHTS, args[:nw]))
    m = dict(zip(_WEIGHTS, args[nw:2 * nw]))
    v = dict(zip(_WEIGHTS, args[2 * nw:]))

    depth = w["w_in"].shape[0]
    small_shapes = [w[n].shape for n in _SMALL]
    small_rows = _rows_for(small_shapes)

    def flat2(t):
        return t.reshape(-1, t.shape[-1])

    tiny_names = [n for n in _BIG if n not in _MATMUL]

    def shard(n, l):
        return w[n][l].astype(BF16)

    first = _allgather_call([shard("w_in", 0)] + [flat2(w[n]) for n in tiny_names], "gather_weights")
    w_in_stacked = first[0]
    joined = {n: _join(t.reshape(N_DEV, *w[n].shape), _SHARD_AXIS[n]) for n, t in zip(tiny_names, first[1:])}
    slot_shapes = {"w_in": (D_MODEL, sum(_SPLIT)), "w_up": (D_MODEL, D_FF), "w_down": (D_FF, D_MODEL)}
    slot_shapes.update({n: (D_MODEL, D_MODEL) for n in _LATE[:4]})

    def layer_fn(l, w_in_full, late_shards, next_shards):
        w_in_t = w_in_full.T
        if l == 0:
            def fn(x_rows, meta, p, slot, exchange_slots):
                out, g, placeholders = _layer(_embed(x_rows, meta), p, w_in_full, w_in_t, slot, late_shards,
                                              next_shards, exchange_slots)
                return (out, placeholders), g
        else:
            def fn(h_in, p, slot, exchange_slots):
                out, g, placeholders = _layer(h_in, p, w_in_full, w_in_t, slot, late_shards, next_shards,
                                              exchange_slots)
                return (out, placeholders), g
        return fn

    h, vjps = None, []
    for l in range(depth):
        slot = {n: jnp.zeros(s, F32) for n, s in slot_shapes.items()}
        p = {n: (joined[n][l] if n in joined else w[n][l]) for n in _PER_LAYER}
        more = l + 1 < depth
        next_shards = [shard("w_in", l + 1)] if more else []
        exchange_slots = tuple(jnp.zeros((3, *w[n][l + 1].shape), BF16) for n in _MATMUL) if more else ()
        lead = (x[0], joined["meta_tokens"]) if l == 0 else (h,)
        fn = layer_fn(l, _regroup_w_in(w_in_stacked), [shard(n, l) for n in _LATE], next_shards)
        (h, _), vjp, g_next = jax.vjp(fn, *lead, p, slot, exchange_slots, has_aux=True)
        if more:
            w_in_stacked = g_next[0]
        vjps.append(vjp)
    loss, head_vjp = jax.vjp(_make_loss_head(loss_target[0]), h, w["final_norm_w"].reshape(1, -1))
    dh, d_final = head_vjp(jnp.ones((), F32))
    loss = lax.psum(loss, ("x", "y", "c"))

    def by_core(name, g):
        if name == "w_in":
            shards = _ungroup_w_in(g)
            return jnp.stack(shards[0::2]), jnp.stack(shards[1::2])
        if name == "w_up":
            t = g.reshape(D_MODEL, 4, 2, D_FF // N_DEV)
            return t[:, :, 0].transpose(1, 0, 2), t[:, :, 1].transpose(1, 0, 2)
        t = g.reshape(4, 2, -1, g.shape[-1])
        return t[:, 0], t[:, 1]

    own, incoming, layer_grads, outgoing = {}, {}, [None] * depth, ()
    for l in reversed(range(depth)):
        if l == 0:
            gx, d_meta, dp, dslot, arrived = vjps[0]((dh, tuple(outgoing)))
        else:
            dh, dp, dslot, arrived = vjps[l]((dh, tuple(outgoing)))
        incoming.update({(n, l + 1): t for n, t in zip(_MATMUL, arrived)})
        layer_grads[l] = dp
        todo = [((n, l), dslot[n]) for n in _MATMUL]
        if l == 0:
            full_grads = {"meta_tokens": d_meta}
            full_grads.update({n: jnp.stack([layer_grads[k][n] for k in range(depth)]) for n in tiny_names[1:]})
            todo += [((n, None), _unjoin(full_grads[n], _SHARD_AXIS[n]).reshape(N_DEV, -1, w[n].shape[-1]))
                     for n in tiny_names]
        pairs = [by_core(u[0], g) for u, g in todo]
        from_sibling = _sibling_exchange_call([a for a, _ in pairs], [b for _, b in pairs], "grads_to_sibling_%d" % l)
        outgoing = []
        for (u, _), (a0, a1), s in zip(todo, pairs, from_sibling):
            own[u], part = _chip_partial_call(a0, a1, s, _rows_tile(s.shape[1], s.shape[2]), "chip_partial_" + u[0])
            outgoing.append(part)
        if l == 0:
            incoming.update(zip([u for u, _ in todo], _chip_exchange_call(outgoing, "grads_to_chips")))

    g_small = {n: jnp.stack([layer_grads[k][n] for k in range(depth)]) for n in _SMALL if n != "final_norm_w"}
    g_small["final_norm_w"] = d_final.reshape(-1)

    by_name = {}
    for n in _BIG:
        layers = list(range(depth)) if n in _MATMUL else [None]
        parts = [[own[n, l][None], incoming[n, l]] for l in layers]
        r, c = own[n, layers[0]].shape
        stacked = [d[n].reshape(len(layers), r, c) for d in (w, m, v)]
        res = _adamw_call(parts, *stacked, _rows_tile(r, c), "adamw_" + n)
        by_name[n] = [t.reshape(w[n].shape) for t in res]

    small_parts = _allgather_call([_pack([g_small[n] for n in _SMALL], small_rows)], "gather_small_grads")
    small_out = _adamw_call([small_parts], *[_pack([d[n] for n in _SMALL], small_rows)[None] for d in (w, m, v)],
                            small_rows, "adamw_replicated")
    for kind in range(4):
        for n, t in zip(_SMALL, _unpack(small_out[kind][0], small_shapes)):
            by_name.setdefault(n, [None] * 4)[kind] = t

    outs = [by_name[n][kind] for kind in range(4) for n in _WEIGHTS]
    return (loss, gx[None], *outs)
```

```python
import functools
import math

import jax
import jax.numpy as jnp
from jax import lax
from jax.experimental import pallas as pl
from jax.experimental.pallas import tpu as pltpu

F32 = jnp.float32
BF16 = jnp.bfloat16
HI = lax.Precision.HIGH
NEG = -1e30

D_MODEL = 1024
N_META = 16
BLK = 128
NPAD = BLK - N_META
RMS_EPS = 1e-6
L2_EPS = 1e-6
CONV_K = 4

GDN_H, GDN_D, GDN_C = 8, 128, 64
SSD_H, SSD_P, SSD_G, SSD_N = 16, 64, 4, 128
SSD_HPG = SSD_H // SSD_G
SWA_QH, SWA_KVH, SWA_D = 16, 4, 64
SWA_REP = SWA_QH // SWA_KVH
D_FF = 4 * D_MODEL

N_DEV = 8
MESH = pl.DeviceIdType.MESH

ADAM_LR, ADAM_B1, ADAM_B2, ADAM_EPS, ADAM_WD, ADAM_STEP = 0.001, 0.9, 0.999, 1e-08, 0.01, 10

VMEM_LIMIT = 56 * 1024 * 1024


def _cp(sem=None):
    return pltpu.CompilerParams(dimension_semantics=sem, vmem_limit_bytes=VMEM_LIMIT)


def _dot(a, b, ca, cb, prec=HI):
    return lax.dot_general(a, b, (((ca,), (cb,)), ((), ())), precision=prec, preferred_element_type=F32)


def _nn(a, b, prec=HI):
    return _dot(a, b, 1, 0, prec)


def _nt(a, b, prec=HI):
    return _dot(a, b, 1, 1, prec)


def _tn(a, b, prec=HI):
    return _dot(a, b, 0, 0, prec)


def _bdot(a, b, ca, cb):
    return lax.dot_general(a.astype(BF16), b.astype(BF16), (((ca,), (cb,)), ((), ())), preferred_element_type=F32)


@jax.custom_vjp
def _lo_nn(a, b):
    return _bdot(a, b, 1, 0)


_lo_nn.defvjp(lambda a, b: (_bdot(a, b, 1, 0), (a, b)),
              lambda r, d: (_bdot(d, r[1], 1, 1), _bdot(r[0], d, 0, 0)))


@jax.custom_vjp
def _lo_nt(a, b):
    return _bdot(a, b, 1, 1)


_lo_nt.defvjp(lambda a, b: (_bdot(a, b, 1, 1), (a, b)),
              lambda r, d: (_bdot(d, r[1], 1, 0), _bdot(d, r[0], 0, 0)))


@jax.custom_vjp
def _lo_tn(a, b):
    return _bdot(a, b, 0, 0)


_lo_tn.defvjp(lambda a, b: (_bdot(a, b, 0, 0), (a, b)),
              lambda r, d: (_bdot(r[1], d, 1, 1), _bdot(r[0], d, 1, 0)))


def _iota2(n, m, axis):
    return lax.broadcasted_iota(jnp.int32, (n, m), axis)


def _silu(x):
    return x * jax.nn.sigmoid(x)


def _softplus(x):
    return jnp.maximum(x, 0.0) + jnp.log(1.0 + jnp.exp(-jnp.abs(x)))


def _row_of(col):
    n = col.shape[0]
    return jnp.broadcast_to(col, (n, n)).T


def _cumsum_col(col):
    n = col.shape[0]
    tril = (_iota2(n, n, 0) >= _iota2(n, n, 1)).astype(F32)
    return _nn(tril, col)


def _tri_inv(a):
    n = a.shape[0]
    r, c = _iota2(n, n, 0), _iota2(n, n, 1)
    eye = (r == c).astype(F32)
    blk = jnp.right_shift(r, 4) == jnp.right_shift(c, 4)
    d = jnp.where(blk, a, 0.0)
    off = a - d
    d2 = _nn(d, d)
    d4 = _nn(d2, d2)
    d8 = _nn(d4, d4)
    td = _nn(_nn(_nn(eye - d, eye + d2), eye + d4), eye + d8)
    m = _nn(td, off)
    m2 = _nn(m, m)
    return _nn(_nn(eye - m, eye + m2), td)


@jax.custom_vjp
def _tri_solve(a, inv, rhs):
    return _nn(inv, rhs)


def _tri_solve_fwd(a, inv, rhs):
    sol = _nn(inv, rhs)
    return sol, (inv, sol)


def _tri_solve_bwd(res, dsol):
    inv, sol = res
    drhs = _nn(inv.T, dsol)
    return -_nt(drhs, sol), jnp.zeros_like(inv), drhs


_tri_solve.defvjp(_tri_solve_fwd, _tri_solve_bwd)


def _gdn_chunk(qa, ka, va, gate, a_raw, b_raw, s, a_log, dt_bias, norm_w, valid, inv=None, want_inv=False):
    c = qa.shape[0]
    q = qa * lax.rsqrt(jnp.sum(qa * qa, axis=-1, keepdims=True) + L2_EPS) * (GDN_D ** -0.5)
    k = ka * lax.rsqrt(jnp.sum(ka * ka, axis=-1, keepdims=True) + L2_EPS)
    beta = jax.nn.sigmoid(b_raw)
    g = -jnp.exp(a_log) * _softplus(a_raw + dt_bias) * valid
    gam = _cumsum_col(g)
    gam_row = _row_of(gam)
    r, cc = _iota2(c, c, 0), _iota2(c, c, 1)
    decay = jnp.exp(jnp.where(r >= cc, gam - gam_row, NEG))
    kb = k * beta
    a = jnp.where(r > cc, _lo_nt(kb, k) * decay, 0.0)
    egam = jnp.exp(gam)
    if inv is None:
        inv = _tri_inv(lax.stop_gradient(a))
    sol = _tri_solve(a, inv, jnp.concatenate([va * beta, kb * egam], axis=1))
    u = sol[:, :GDN_D]
    w = sol[:, GDN_D:]
    attn = _lo_nt(q, k) * decay
    g_last = jnp.sum(g, axis=0, keepdims=True)
    k_tail = k * jnp.exp(g_last - gam)
    v_new = u - _lo_nn(w, s)
    o = _lo_nn(q * egam, s) + _lo_nn(attn, v_new)
    s_new = s * jnp.exp(g_last) + _lo_tn(k_tail, v_new)
    y = o * lax.rsqrt(jnp.mean(o * o, axis=-1, keepdims=True) + RMS_EPS) * norm_w * _silu(gate)
    return (y, s_new, inv) if want_inv else (y, s_new)


def _valid_col(row0, n):
    return (row0 + _iota2(n, 1, 0) >= NPAD).astype(F32)


GDN_HB = GDN_H

SM_B, SM_A, SM_DT, SM_W = 0, 8, 16, 128


def _pick_cols(sm, first, n):
    return jnp.stack([sm[:, first + j:first + j + 1] for j in range(n)])


def _spread_cols(cols, first):
    lane = _iota2(1, SM_W, 1)
    out = None
    for j in range(cols.shape[0]):
        term = cols[j] * (lane == first + j).astype(F32)
        out = term if out is None else out + term
    return out


def _widen(t, width):
    if width == t.shape[1]:
        return t
    return jnp.concatenate([t, jnp.zeros((t.shape[0], width - t.shape[1]), t.dtype)], axis=1)


def _gdn_specs(nc, rev):
    ci = (lambda i: nc - 1 - i) if rev else (lambda i: i)
    hb = GDN_HB
    tile = pl.BlockSpec((GDN_C, hb * GDN_D), lambda h, i: (ci(i), h))
    col = pl.BlockSpec((GDN_C, SM_W), lambda h, i: (ci(i), 0))
    scal = pl.BlockSpec((hb, 1, 1), lambda h, i: (h, 0, 0))
    nw = pl.BlockSpec((1, GDN_D), lambda h, i: (0, 0))
    st = pl.BlockSpec((hb, 1, GDN_D, GDN_D), lambda h, i: (h, ci(i), 0, 0))
    return tile, col, scal, nw, st


def _lanes(j):
    return slice(j * GDN_D, (j + 1) * GDN_D)


def _by_head(ref):
    return jnp.stack([ref[:, _lanes(j)] for j in range(GDN_HB)])


def _gdn_fwd_call(q, k, v, gate, small, a_log, dt_bias, norm_w, shards=()):
    seq = q.shape[0]
    nc = seq // GDN_C
    ns = len(shards)
    tile, col, scal, nw, st = _gdn_specs(nc, False)

    def body(*refs):
        q_ref, k_ref, v_ref, g_ref, sm_ref, al_ref, dt_ref, nw_ref = refs[:8]
        y_ref, st_ref, inv_ref = refs[8 + ns:11 + ns]
        s_scr = refs[11 + 2 * ns]
        i = pl.program_id(1)
        if ns:
            start, relay, finish = _gather_phases(refs[8:8 + ns], refs[11 + ns:11 + 2 * ns], *refs[12 + 2 * ns:])
            pl.when(i == 0)(start)
            pl.when(i == nc - 1)(relay)

        @pl.when(i == 0)
        def _():
            s_scr[...] = jnp.zeros_like(s_scr)

        s = s_scr[...]
        st_ref[:, 0] = s
        sm = sm_ref[...]
        fn = jax.vmap(functools.partial(_gdn_chunk, valid=_valid_col(i * GDN_C, GDN_C), want_inv=True))
        y, s_new, inv = fn(_by_head(q_ref), _by_head(k_ref), _by_head(v_ref), _by_head(g_ref),
                           _pick_cols(sm, SM_A, GDN_H), _pick_cols(sm, SM_B, GDN_H), s,
                           al_ref[...], dt_ref[...], jnp.broadcast_to(nw_ref[...], (GDN_HB, 1, GDN_D)))
        for j in range(GDN_HB):
            y_ref[:, _lanes(j)] = y[j]
        inv_ref[:, 0] = inv
        s_scr[...] = s_new
        if ns:
            pl.when(i == nc - 1)(finish)

    return pl.pallas_call(
        body, name="gdn_fwd", grid=(GDN_H // GDN_HB, nc),
        in_specs=[tile, tile, tile, tile, col, scal, scal, nw] + [_ANY] * ns,
        out_specs=[tile, st, pl.BlockSpec((GDN_HB, 1, GDN_C, GDN_C), lambda h, i: (h, i, 0, 0))] + [_ANY] * ns,
        out_shape=[jax.ShapeDtypeStruct((seq, GDN_H * GDN_D), F32),
                   jax.ShapeDtypeStruct((GDN_H, nc, GDN_D, GDN_D), F32),
                   jax.ShapeDtypeStruct((GDN_H, nc, GDN_C, GDN_C), F32)] + _gather_out_shapes(shards),
        scratch_shapes=[pltpu.VMEM((GDN_HB, GDN_D, GDN_D), F32)] + (_gather_sems(ns) if ns else []),
        compiler_params=_cp(("parallel", "arbitrary")),
    )(q, k, v, gate, small, a_log, dt_bias, norm_w, *shards)


def _gdn_bwd_call(q, k, v, gate, small, a_log, dt_bias, norm_w, states, invs, dy, outgoing=()):
    seq = q.shape[0]
    nc = seq // GDN_C
    no = len(outgoing)
    tile, col, scal, nw, st = _gdn_specs(nc, True)
    nwh = pl.BlockSpec((GDN_HB, 1, GDN_D), lambda h, i: (h, 0, 0))
    inv_spec = pl.BlockSpec((GDN_HB, 1, GDN_C, GDN_C), lambda h, i: (h, nc - 1 - i, 0, 0))

    def body(*refs):
        q_ref, k_ref, v_ref, g_ref, sm_ref, al_ref, dt_ref, nw_ref, st_ref, inv_ref, dy_ref = refs[:11]
        dq_ref, dk_ref, dv_ref, dg_ref, dsm_ref, dal_ref, ddt_ref, dnw_ref = refs[11 + no:19 + no]
        ds_scr = refs[19 + 2 * no]
        i = pl.program_id(1)
        if no:
            start, finish = _chip_exchange_phases(refs[11:11 + no], refs[19 + no:19 + 2 * no], *refs[20 + 2 * no:])
            pl.when(i == 0)(start)

        @pl.when(i == 0)
        def _():
            ds_scr[...] = jnp.zeros_like(ds_scr)
            dal_ref[...] = jnp.zeros_like(dal_ref)
            ddt_ref[...] = jnp.zeros_like(ddt_ref)
            dnw_ref[...] = jnp.zeros_like(dnw_ref)

        sm = sm_ref[...]
        valid = _valid_col((nc - 1 - i) * GDN_C, GDN_C)
        kept = inv_ref[:, 0]

        def fn(*heads):
            return jax.vmap(lambda *t: _gdn_chunk(*t[:-1], valid=valid, inv=t[-1]))(*heads, kept)

        _, vjp = jax.vjp(fn, _by_head(q_ref), _by_head(k_ref), _by_head(v_ref), _by_head(g_ref),
                         _pick_cols(sm, SM_A, GDN_H), _pick_cols(sm, SM_B, GDN_H), st_ref[:, 0], al_ref[...],
                         dt_ref[...], jnp.broadcast_to(nw_ref[...], (GDN_HB, 1, GDN_D)))
        dq, dk, dv, dg, da, db, ds, dal, ddt, dnw = vjp((_by_head(dy_ref), ds_scr[...]))
        for j in range(GDN_HB):
            dq_ref[:, _lanes(j)] = dq[j]
            dk_ref[:, _lanes(j)] = dk[j]
            dv_ref[:, _lanes(j)] = dv[j]
            dg_ref[:, _lanes(j)] = dg[j]
        dsm_ref[...] = _widen(_spread_cols(da, SM_A) + _spread_cols(db, SM_B), dsm_ref.shape[1])
        ds_scr[...] = ds
        dal_ref[...] += dal
        ddt_ref[...] += ddt
        dnw_ref[...] += dnw
        if no:
            pl.when(i == nc - 1)(finish)

    big = jax.ShapeDtypeStruct((seq, GDN_H * GDN_D), F32)
    return pl.pallas_call(
        body, name="gdn_bwd", grid=(GDN_H // GDN_HB, nc),
        in_specs=[tile, tile, tile, tile, col, scal, scal, nw, st, inv_spec, tile] + [_ANY] * no,
        out_specs=[tile, tile, tile, tile, pl.BlockSpec((GDN_C, small.shape[1]), lambda h, i: (nc - 1 - i, 0)),
                   scal, scal, nwh] + [_ANY] * no,
        out_shape=[big, big, big, big, jax.ShapeDtypeStruct(small.shape, F32),
                   jax.ShapeDtypeStruct((GDN_H, 1, 1), F32), jax.ShapeDtypeStruct((GDN_H, 1, 1), F32),
                   jax.ShapeDtypeStruct((GDN_H, 1, GDN_D), F32)] + _chip_exchange_out_shapes(outgoing),
        scratch_shapes=[pltpu.VMEM((GDN_HB, GDN_D, GDN_D), F32)] + (_chip_exchange_sems(no) if no else []),
        compiler_params=_cp(("parallel", "arbitrary")),
    )(q, k, v, gate, small, a_log, dt_bias, norm_w, states, invs, dy, *outgoing)


@jax.custom_vjp
def gdn_core(q, k, v, gate, small, a_log, dt_bias, norm_w, shards, slots):
    y, _, _, *gathered = _gdn_fwd_call(q, k, v, gate, small, a_log, dt_bias, norm_w, shards)
    return y, tuple(gathered), tuple(jnp.zeros((4, *s.shape[1:]), s.dtype) for s in slots)


def _gdn_core_fwd(q, k, v, gate, small, a_log, dt_bias, norm_w, shards, slots):
    y, states, invs, *gathered = _gdn_fwd_call(q, k, v, gate, small, a_log, dt_bias, norm_w, shards)
    out = (y, tuple(gathered), tuple(jnp.zeros((4, *s.shape[1:]), s.dtype) for s in slots))
    return out, (q, k, v, gate, small, a_log, dt_bias, norm_w, states, invs, shards)


def _gdn_core_bwd(res, cts):
    *args, shards = res
    dy, _, outgoing = cts
    dq, dk, dv, dg, dsm, dal, ddt, dnw, *incoming = _gdn_bwd_call(*args, dy, outgoing)
    return (dq, dk, dv, dg, dsm, dal, ddt, jnp.sum(dnw, axis=0), tuple(jnp.zeros_like(s) for s in shards),
            tuple(incoming))


gdn_core.defvjp(_gdn_core_fwd, _gdn_core_bwd)


def _ssd_head(x, z, dt_raw, h, dt_bias, a_log, d_skip, bm, cm, cb, valid):
    c = bm.shape[0]
    r, cc = _iota2(c, c, 0), _iota2(c, c, 1)
    dtp = _softplus(dt_raw + dt_bias)
    x = x * valid
    adt = -jnp.exp(a_log) * dtp * valid
    xdt = x * dtp
    acum = _cumsum_col(adt)
    lmat = jnp.exp(jnp.where(r >= cc, acum - _row_of(acum), NEG))
    a_last = jnp.sum(adt, axis=0, keepdims=True)
    y = _lo_nn(cb * lmat, xdt) + _lo_nt(cm * jnp.exp(acum), h) + d_skip * x
    h_new = h * jnp.exp(a_last) + _lo_tn(xdt * jnp.exp(a_last - acum), bm)
    return y * _silu(z), h_new


SSD_SIDE = SSD_H


def _ssd_chunk(xs, z, bm, cm, dt_raw, h, dt_bias, a_log, d_skip, norm_w, valid):
    nh, c, p = xs.shape
    ng = bm.shape[0]
    hpg = nh // ng
    bm = bm * valid
    cm = cm * valid
    cb = jax.vmap(_lo_nt)(cm, bm)
    per_head = lambda t: jnp.repeat(t, hpg, axis=0)
    args = (xs, z, dt_raw, h, dt_bias, a_log, d_skip, per_head(bm), per_head(cm), per_head(cb))
    outs = [jax.vmap(functools.partial(_ssd_head, valid=valid))(*[t[s:s + SSD_SIDE] for t in args])
            for s in range(0, nh, SSD_SIDE)]
    ys = jnp.concatenate([o[0] for o in outs], axis=0)
    hs = jnp.concatenate([o[1] for o in outs], axis=0)
    ss = jnp.sum(jnp.sum(ys * ys, axis=-1, keepdims=True).reshape(ng, hpg, c, 1), axis=1, keepdims=True)
    rstd = lax.rsqrt(ss / (hpg * p) + RMS_EPS)
    return (ys.reshape(ng, hpg, c, p) * rstd).reshape(nh, c, p) * norm_w, hs


SSD_INNER = SSD_H * SSD_P
SSD_BC = SSD_G * SSD_N


def _split_lanes(t, n, w):
    return jnp.stack([t[:, j * w:(j + 1) * w] for j in range(n)])


def _join_lanes(t):
    return jnp.concatenate([t[j] for j in range(t.shape[0])], axis=1)


def _ssd_specs(nc, rev):
    ci = (lambda i: nc - 1 - i) if rev else (lambda i: i)
    wide = pl.BlockSpec((BLK, SSD_INNER), lambda i: (ci(i), 0))
    bmat = pl.BlockSpec((BLK, SSD_BC), lambda i: (ci(i), SSD_INNER // SSD_BC))
    cmat = pl.BlockSpec((BLK, SSD_BC), lambda i: (ci(i), SSD_INNER // SSD_BC + 1))
    xbc = pl.BlockSpec((BLK, SSD_INNER + 2 * SSD_BC), lambda i: (ci(i), 0))
    col = pl.BlockSpec((BLK, SM_W), lambda i: (ci(i), 0))
    scal = pl.BlockSpec((SSD_H, 1, 1), lambda i: (0, 0, 0))
    nw = pl.BlockSpec((SSD_H, 1, SSD_P), lambda i: (0, 0, 0))
    st = pl.BlockSpec((SSD_H, 1, SSD_P, SSD_N), lambda i: (0, ci(i), 0, 0))
    return wide, bmat, cmat, xbc, col, scal, nw, st


def _ssd_fwd_call(xbc, z, small, dt_bias, a_log, d_skip, norm_w, shards=()):
    seq = z.shape[0]
    nc = seq // BLK
    ns = len(shards)
    wide, bmat, cmat, _, col, scal, nw, st = _ssd_specs(nc, False)

    def body(*refs):
        x_ref, b_ref, c_ref, z_ref, sm_ref, db_ref, al_ref, ds_ref, nw_ref = refs[:9]
        y_ref, st_ref = refs[9 + ns:11 + ns]
        h_scr = refs[11 + 2 * ns]
        i = pl.program_id(0)
        if ns:
            start, relay, finish = _gather_phases(refs[9:9 + ns], refs[11 + ns:11 + 2 * ns], *refs[12 + 2 * ns:])
            pl.when(i == 0)(start)
            pl.when(i == nc - 1)(relay)

        @pl.when(i == 0)
        def _():
            h_scr[...] = jnp.zeros_like(h_scr)

        h = h_scr[...]
        st_ref[:, 0] = h
        y, h_new = _ssd_chunk(_split_lanes(x_ref[...], SSD_H, SSD_P), _split_lanes(z_ref[...], SSD_H, SSD_P),
                              _split_lanes(b_ref[...], SSD_G, SSD_N), _split_lanes(c_ref[...], SSD_G, SSD_N),
                              _pick_cols(sm_ref[...], SM_DT, SSD_H), h, db_ref[...], al_ref[...], ds_ref[...],
                              nw_ref[...], _valid_col(i * BLK, BLK))
        y_ref[...] = _join_lanes(y)
        h_scr[...] = h_new
        if ns:
            pl.when(i == nc - 1)(finish)

    return pl.pallas_call(
        body, name="ssd_fwd", grid=(nc,),
        in_specs=[wide, bmat, cmat, wide, col, scal, scal, scal, nw] + [_ANY] * ns,
        out_specs=[wide, st] + [_ANY] * ns,
        out_shape=[jax.ShapeDtypeStruct((seq, SSD_INNER), F32),
                   jax.ShapeDtypeStruct((SSD_H, nc, SSD_P, SSD_N), F32)] + _gather_out_shapes(shards),
        scratch_shapes=[pltpu.VMEM((SSD_H, SSD_P, SSD_N), F32)] + (_gather_sems(ns) if ns else []),
        compiler_params=_cp(("arbitrary",)),
    )(xbc, xbc, xbc, z, small, dt_bias, a_log, d_skip, norm_w, *shards)


def _ssd_bwd_call(xbc, z, small, dt_bias, a_log, d_skip, norm_w, states, dy):
    seq = z.shape[0]
    nc = seq // BLK
    wide, bmat, cmat, xbc_spec, col, scal, nw, st = _ssd_specs(nc, True)

    def body(x_ref, b_ref, c_ref, z_ref, sm_ref, db_ref, al_ref, ds_ref, nw_ref, st_ref, dy_ref,
             dxbc_ref, dz_ref, dsm_ref, ddb_ref, dal_ref, dds_ref, dnw_ref, dh_scr):
        i = pl.program_id(0)

        @pl.when(i == 0)
        def _():
            dh_scr[...] = jnp.zeros_like(dh_scr)
            ddb_ref[...] = jnp.zeros_like(ddb_ref)
            dal_ref[...] = jnp.zeros_like(dal_ref)
            dds_ref[...] = jnp.zeros_like(dds_ref)
            dnw_ref[...] = jnp.zeros_like(dnw_ref)

        fn = functools.partial(_ssd_chunk, valid=_valid_col((nc - 1 - i) * BLK, BLK))
        _, vjp = jax.vjp(fn, _split_lanes(x_ref[...], SSD_H, SSD_P), _split_lanes(z_ref[...], SSD_H, SSD_P),
                         _split_lanes(b_ref[...], SSD_G, SSD_N), _split_lanes(c_ref[...], SSD_G, SSD_N),
                         _pick_cols(sm_ref[...], SM_DT, SSD_H), st_ref[:, 0], db_ref[...], al_ref[...], ds_ref[...],
                         nw_ref[...])
        dx, dz, dbm, dcm, ddt, dh, ddb, dal, dds, dnw = vjp((_split_lanes(dy_ref[...], SSD_H, SSD_P), dh_scr[...]))
        dxbc_ref[:, :SSD_INNER] = _join_lanes(dx)
        dxbc_ref[:, SSD_INNER:SSD_INNER + SSD_BC] = _join_lanes(dbm)
        dxbc_ref[:, SSD_INNER + SSD_BC:] = _join_lanes(dcm)
        dz_ref[...] = _join_lanes(dz)
        dsm_ref[...] = _widen(_spread_cols(ddt, SM_DT), dsm_ref.shape[1])
        dh_scr[...] = dh
        ddb_ref[...] += ddb
        dal_ref[...] += dal
        dds_ref[...] += dds
        dnw_ref[...] += dnw

    sshape = jax.ShapeDtypeStruct((SSD_H, 1, 1), F32)
    return pl.pallas_call(
        body, name="ssd_bwd", grid=(nc,),
        in_specs=[wide, bmat, cmat, wide, col, scal, scal, scal, nw, st, wide],
        out_specs=[xbc_spec, wide, pl.BlockSpec((BLK, small.shape[1]), lambda i: (nc - 1 - i, 0)), scal, scal, scal, nw],
        out_shape=[jax.ShapeDtypeStruct(xbc.shape, F32), jax.ShapeDtypeStruct(z.shape, F32),
                   jax.ShapeDtypeStruct(small.shape, F32), sshape, sshape, sshape,
                   jax.ShapeDtypeStruct((SSD_H, 1, SSD_P), F32)],
        scratch_shapes=[pltpu.VMEM((SSD_H, SSD_P, SSD_N), F32)],
        compiler_params=_cp(("arbitrary",)),
    )(xbc, xbc, xbc, z, small, dt_bias, a_log, d_skip, norm_w, states, dy)


@jax.custom_vjp
def ssd_core(xbc, z, small, dt_bias, a_log, d_skip, norm_w, shards):
    y, _, *gathered = _ssd_fwd_call(xbc, z, small, dt_bias, a_log, d_skip, norm_w, shards)
    return y, tuple(gathered)


def _ssd_core_fwd(*args):
    y, states, *gathered = _ssd_fwd_call(*args)
    return (y, tuple(gathered)), (*args[:-1], states, args[-1])


def _ssd_core_bwd(res, cts):
    *args, shards = res
    return (*_ssd_bwd_call(*args, cts[0]), tuple(jnp.zeros_like(s) for s in shards))


ssd_core.defvjp(_ssd_core_fwd, _ssd_core_bwd)


def _swa_block(q, km, kp, kc, vm, vp, vc, sink, n):
    rows = SWA_REP * BLK
    qs = q.reshape(rows, SWA_D) * (SWA_D ** -0.5)
    s = _lo_nt(qs, jnp.concatenate([km, kp, kc], axis=0))
    i = jnp.bitwise_and(_iota2(rows, 3 * BLK, 0), BLK - 1)
    col = _iota2(rows, 3 * BLK, 1)
    j = jnp.bitwise_and(col, BLK - 1)
    part = jnp.right_shift(col, 7)
    ok_m = (part == 0) & (j >= NPAD) & ((n >= 1) | (j <= i))
    ok_p = (part == 1) & (n >= 2) & (j > i)
    ok_c = (part == 2) & (n >= 1) & (j <= i)
    ok = ok_m | ok_p | ok_c
    s = jnp.where(ok, s, NEG)
    snk = jnp.concatenate([jnp.broadcast_to(sink[r], (BLK, 1)) for r in range(SWA_REP)], axis=0)
    m = lax.stop_gradient(jnp.maximum(jnp.max(s, axis=-1, keepdims=True), snk))
    e = jnp.exp(s - m)
    p = e / (jnp.sum(e, axis=-1, keepdims=True) + jnp.exp(snk - m))
    o = _lo_nn(p, jnp.concatenate([vm, vp, vc], axis=0))
    return o.reshape(SWA_REP, BLK, SWA_D)


SWA_QW = SWA_QH * SWA_D
SWA_KW = SWA_KVH * SWA_D


def _swa_specs(nb, rev):
    ci = (lambda i: nb - 1 - i) if rev else (lambda i: i)
    qsp = pl.BlockSpec((BLK, SWA_QW), lambda i: (ci(i), 0))
    cur = pl.BlockSpec((BLK, 2 * SWA_KW), lambda i: (ci(i), 0))
    prev = pl.BlockSpec((BLK, 2 * SWA_KW), lambda i: (jnp.maximum(ci(i) - 1, 0), 0))
    meta = pl.BlockSpec((BLK, 2 * SWA_KW), lambda i: (0, 0))
    scal = pl.BlockSpec((SWA_QH, 1, 1), lambda i: (0, 0, 0))
    return qsp, cur, prev, meta, scal


def _swa_by_head(q, kvm, kvp, kvc, sink):
    def kv(t):
        return _split_lanes(t[:, :SWA_KW], SWA_KVH, SWA_D), _split_lanes(t[:, SWA_KW:], SWA_KVH, SWA_D)

    (km, vm), (kp, vp), (kc, vc) = kv(kvm), kv(kvp), kv(kvc)
    qh = _split_lanes(q, SWA_QH, SWA_D).reshape(SWA_KVH, SWA_REP, BLK, SWA_D)
    return qh, km, kp, kc, vm, vp, vc, sink.reshape(SWA_KVH, SWA_REP, 1, 1)


def _swa_kv_tile(dk, dv):
    return jnp.concatenate([_join_lanes(dk), _join_lanes(dv)], axis=1)


def _swa_fwd_call(q, kv, sink):
    seq = q.shape[0]
    nb = seq // BLK
    qsp, cur, prev, meta, scal = _swa_specs(nb, False)

    def body(q_ref, m_ref, p_ref, c_ref, s_ref, o_ref):
        fn = jax.vmap(functools.partial(_swa_block, n=pl.program_id(0)))
        o = fn(*_swa_by_head(q_ref[...], m_ref[...], p_ref[...], c_ref[...], s_ref[...]))
        o_ref[...] = _join_lanes(o.reshape(SWA_QH, BLK, SWA_D))

    return pl.pallas_call(
        body, name="swa_fwd", grid=(nb,),
        in_specs=[qsp, meta, prev, cur, scal],
        out_specs=qsp,
        out_shape=jax.ShapeDtypeStruct(q.shape, F32),
        compiler_params=_cp(("parallel",)),
    )(q, kv, kv, kv, sink)


def _swa_bwd_call(q, kv, sink, do):
    seq = q.shape[0]
    nb = seq // BLK
    qsp, cur, prev, meta, scal = _swa_specs(nb, True)

    def body(q_ref, m_ref, p_ref, c_ref, s_ref, do_ref, dq_ref, dkv_ref, ds_ref, prev_scr, meta_scr):
        i = pl.program_id(0)
        n = nb - 1 - i

        @pl.when(i == 0)
        def _():
            prev_scr[...] = jnp.zeros_like(prev_scr)
            meta_scr[...] = jnp.zeros_like(meta_scr)
            ds_ref[...] = jnp.zeros_like(ds_ref)

        fn = jax.vmap(functools.partial(_swa_block, n=n))
        _, vjp = jax.vjp(fn, *_swa_by_head(q_ref[...], m_ref[...], p_ref[...], c_ref[...], s_ref[...]))
        do = _split_lanes(do_ref[...], SWA_QH, SWA_D).reshape(SWA_KVH, SWA_REP, BLK, SWA_D)
        dq, dkm, dkp, dkc, dvm, dvp, dvc, dsk = vjp(do)
        dq_ref[...] = _join_lanes(dq.reshape(SWA_QH, BLK, SWA_D))
        ds_ref[...] += dsk.reshape(SWA_QH, 1, 1)
        meta_scr[...] += _swa_kv_tile(dkm, dvm)
        first = (n == 0).astype(F32)
        dkv_ref[...] = _swa_kv_tile(dkc, dvc) + prev_scr[...] + first * meta_scr[...]
        prev_scr[...] = _swa_kv_tile(dkp, dvp)

    return pl.pallas_call(
        body, name="swa_bwd", grid=(nb,),
        in_specs=[qsp, meta, prev, cur, scal, qsp],
        out_specs=[qsp, cur, scal],
        out_shape=[jax.ShapeDtypeStruct(q.shape, F32), jax.ShapeDtypeStruct(kv.shape, F32),
                   jax.ShapeDtypeStruct(sink.shape, F32)],
        scratch_shapes=[pltpu.VMEM((BLK, 2 * SWA_KW), F32)] * 2,
        compiler_params=_cp(("arbitrary",)),
    )(q, kv, kv, kv, sink, do)


@jax.custom_vjp
def swa_core(q, kv, sink):
    return _swa_fwd_call(q, kv, sink)


def _swa_core_fwd(q, kv, sink):
    return _swa_fwd_call(q, kv, sink), (q, kv, sink)


def _swa_core_bwd(res, do):
    return tuple(_swa_bwd_call(*res, do))


swa_core.defvjp(_swa_core_fwd, _swa_core_bwd)


def _tile(n, pref):
    if n <= pref:
        return n
    best = None
    for t in range(128, pref + 1, 128):
        if n % t == 0:
            best = t
    assert best is not None, (n, pref)
    return best


MM_TILE_BYTES = 9 * 1024 * 1024


def _mm_tiles(m, n, kk, a_bytes):
    if kk > 8192:
        return _tile(m, 2816 // a_bytes), _tile(n, 512), _tile(kk, 4096)
    if kk > 1408 and _tile(m, 1024) * kk * a_bytes <= MM_TILE_BYTES:
        return _tile(m, 1024), _tile(n, 512), kk
    return _tile(m, 1408), _tile(n, 1024 if kk <= 1408 else 512), _tile(kk, 1408)


def _mm_call(a, b, name):
    (m, kk), n = a.shape, b.shape[1]
    tm, tn, tk = _mm_tiles(m, n, kk, a.dtype.itemsize)
    nk = kk // tk
    a_spec = pl.BlockSpec((tm, tk), lambda i, j, k: (i, k))
    b_spec = pl.BlockSpec((tk, tn), lambda i, j, k: (k, j))

    def body(a_ref, b_ref, o_ref, acc_ref):
        k = pl.program_id(2)
        part = jnp.dot(a_ref[...].astype(BF16), b_ref[...].astype(BF16), preferred_element_type=F32)

        @pl.when(k == 0)
        def _():
            acc_ref[...] = part

        @pl.when(k > 0)
        def _():
            acc_ref[...] += part

        @pl.when(k == nk - 1)
        def _():
            o_ref[...] = acc_ref[...]

    return pl.pallas_call(
        body, name=name, grid=(m // tm, n // tn, nk),
        in_specs=[a_spec, b_spec],
        out_specs=pl.BlockSpec((tm, tn), lambda i, j, k: (i, j)),
        out_shape=jax.ShapeDtypeStruct((m, n), F32),
        scratch_shapes=[pltpu.VMEM((tm, tn), F32)],
        compiler_params=_cp(("parallel", "parallel", "arbitrary")),
    )(a, b)


@jax.custom_vjp
def mm(a, b, b_t, grad_slot):
    return _mm_call(a, b, "mm_fwd")


def _mm_fwd(a, b, b_t, grad_slot):
    return _mm_call(a, b, "mm_fwd"), (a, b, b_t)


def _mm_bwd(res, dc):
    a, b, b_t = res
    return (_mm_call(dc, b_t, "mm_dx"), jnp.zeros_like(b), jnp.zeros_like(b_t),
            _mm_call(a.astype(BF16).T, dc, "mm_dw"))


mm.defvjp(_mm_fwd, _mm_bwd)


_SPLIT = (1024, 1024, 1024, 1024, 1024, 2048, 1024, 512, 3072, 512)


def _split_cols(u):
    offs = [sum(_SPLIT[:i]) for i in range(len(_SPLIT))]
    return tuple(u[:, o:o + s] for o, s in zip(offs, _SPLIT))


@jax.custom_vjp
def mm_split(a, b, b_t, grad_slot):
    return _split_cols(_mm_call(a, b, "mm_fwd"))


def _mm_split_fwd(a, b, b_t, grad_slot):
    return _split_cols(_mm_call(a, b, "mm_fwd")), (a, b, b_t)


def _mm_split_bwd(res, cts):
    return _mm_bwd(res, jnp.concatenate([c.astype(BF16) for c in cts], axis=1))


mm_split.defvjp(_mm_split_fwd, _mm_split_bwd)


def _row_specs(arrs, tr):
    return [pl.BlockSpec((tr, a.shape[1]), lambda i: (i, 0)) for a in arrs]


def _par_specs(arrs):
    return [pl.BlockSpec(a.shape, lambda i: (0, 0)) for a in arrs]


def _row_fwd_call(fn, rows, params, out_cols, tr, name):
    seq = rows[0].shape[0]
    nr = len(rows)

    def body(*refs):
        vals = [r[...] for r in refs[:-1]]
        refs[-1][...] = fn(*vals)

    return pl.pallas_call(
        body, name=name, grid=(seq // tr,),
        in_specs=_row_specs(rows, tr) + _par_specs(params),
        out_specs=pl.BlockSpec((tr, out_cols), lambda i: (i, 0)),
        out_shape=jax.ShapeDtypeStruct((seq, out_cols), F32),
        compiler_params=_cp(("parallel",)),
    )(*rows, *params)


def _row_bwd_call(fn, rows, params, dy, tr, name):
    seq = rows[0].shape[0]
    nr, npar = len(rows), len(params)

    def body(*refs):
        ins = refs[:nr + npar]
        dy_ref = refs[nr + npar]
        outs = refs[nr + npar + 1:]
        _, vjp = jax.vjp(fn, *[r[...] for r in ins])
        cts = vjp(dy_ref[...])
        for o_ref, ct in zip(outs[:nr], cts[:nr]):
            o_ref[...] = ct

        @pl.when(pl.program_id(0) == 0)
        def _():
            for o_ref in outs[nr:]:
                o_ref[...] = jnp.zeros_like(o_ref)

        for o_ref, ct in zip(outs[nr:], cts[nr:]):
            o_ref[...] += ct

    return pl.pallas_call(
        body, name=name, grid=(seq // tr,),
        in_specs=_row_specs(rows, tr) + _par_specs(params) + _row_specs([dy], tr),
        out_specs=_row_specs(rows, tr) + _par_specs(params),
        out_shape=[jax.ShapeDtypeStruct(a.shape, F32) for a in (*rows, *params)],
        compiler_params=_cp(("arbitrary",)),
    )(*rows, *params, dy)


def _make_rowop(fn, nrows, out_cols, tr, name):
    @jax.custom_vjp
    def op(*args):
        return _row_fwd_call(fn, args[:nrows], args[nrows:], out_cols, tr, name + "_fwd")

    def fwd(*args):
        return op(*args), args

    def bwd(args, dy):
        return tuple(_row_bwd_call(fn, args[:nrows], args[nrows:], dy, tr, name + "_bwd"))

    op.defvjp(fwd, bwd)
    return op


def _rms_fn(x, w):
    return x * lax.rsqrt(jnp.mean(x * x, axis=-1, keepdims=True) + RMS_EPS) * w


def _merge_fn(pa, pb, pc, gl):
    d = D_MODEL
    return (jax.nn.sigmoid(gl[:, :d]) * pa + jax.nn.sigmoid(gl[:, d:2 * d]) * pb
            + jax.nn.sigmoid(gl[:, 2 * d:]) * pc)


def _relu2_fn(a):
    r = jnp.maximum(a, 0.0)
    return r * r


rms_op = _make_rowop(_rms_fn, 1, D_MODEL, 384, "rms")
merge_op = _make_rowop(_merge_fn, 4, D_MODEL, 192, "merge")
relu2_op = _make_rowop(_relu2_fn, 1, D_FF, 192, "relu2")


def _conv_taps(xext, w, nrows):
    z = None
    for j in range(CONV_K):
        sh = CONV_K - 1 - j
        xs = pltpu.roll(xext, sh, 0) if sh else xext
        term = w[j:j + 1, :] * xs[8:8 + nrows, :]
        z = term if z is None else z + term
    return z


def _halo(ref, start, ok):
    return jnp.where(ok, ref[pl.ds(pl.multiple_of(start, 8), 8), :], 0.0)


def _conv_fwd_call(x, w, b):
    seq, ch = x.shape
    nb = seq // BLK

    def body(x_ref, w_ref, b_ref, o_ref):
        w = w_ref[...]
        bias = b_ref[...]

        def step(i, carry):
            r0 = pl.multiple_of(i * BLK, BLK)
            xext = jnp.concatenate([_halo(x_ref, jnp.maximum(r0 - 8, 0), i > 0), x_ref[pl.ds(r0, BLK), :]], axis=0)
            o_ref[pl.ds(r0, BLK), :] = _silu(_conv_taps(xext, w, BLK) + bias)
            return carry

        lax.fori_loop(0, nb, step, 0)

    strip = pl.BlockSpec((seq, 128), lambda c: (0, c))
    return pl.pallas_call(
        body, name="conv_fwd", grid=(ch // 128,),
        in_specs=[strip, pl.BlockSpec((CONV_K, 128), lambda c: (0, c)), pl.BlockSpec((1, 128), lambda c: (0, c))],
        out_specs=strip, out_shape=jax.ShapeDtypeStruct(x.shape, F32),
        compiler_params=_cp(("parallel",)),
    )(x, w, b)


def _conv_bwd_call(x, w, b, dy):
    seq, ch = x.shape
    nb = seq // BLK

    def body(x_ref, w_ref, b_ref, dy_ref, dx_ref, dw_ref, db_ref):
        w = w_ref[...]
        bias = b_ref[...]

        def step(i, carry):
            r0 = pl.multiple_of(i * BLK, BLK)
            last = i == nb - 1
            nxt = jnp.minimum(r0 + BLK, seq - 8)
            xext = jnp.concatenate([_halo(x_ref, jnp.maximum(r0 - 8, 0), i > 0), x_ref[pl.ds(r0, BLK), :],
                                    _halo(x_ref, nxt, jnp.logical_not(last))], axis=0)
            dyext = jnp.concatenate([dy_ref[pl.ds(r0, BLK), :], _halo(dy_ref, nxt, jnp.logical_not(last))], axis=0)
            z = _conv_taps(xext, w, BLK + 8) + bias
            sg = jax.nn.sigmoid(z)
            dz = dyext * (sg * (1.0 + z * (1.0 - sg)))
            dx = None
            for j in range(CONV_K):
                sh = CONV_K - 1 - j
                dzs = pltpu.roll(dz, BLK + 8 - sh, 0) if sh else dz
                term = w[j:j + 1, :] * dzs[:BLK, :]
                dx = term if dx is None else dx + term
            dx_ref[pl.ds(r0, BLK), :] = dx
            dzm = dz[:BLK, :]
            out = []
            for j in range(CONV_K):
                sh = CONV_K - 1 - j
                xs = pltpu.roll(xext, sh, 0) if sh else xext
                out.append(carry[j] + jnp.sum(dzm * xs[8:8 + BLK, :], axis=0, keepdims=True))
            out.append(carry[CONV_K] + jnp.sum(dzm, axis=0, keepdims=True))
            return tuple(out)

        zero = jnp.zeros((1, 128), F32)
        acc = lax.fori_loop(0, nb, step, (zero,) * (CONV_K + 1))
        dw_ref[...] = jnp.concatenate(acc[:CONV_K], axis=0)
        db_ref[...] = acc[CONV_K]

    strip = pl.BlockSpec((seq, 128), lambda c: (0, c))
    wsp = pl.BlockSpec((CONV_K, 128), lambda c: (0, c))
    bsp = pl.BlockSpec((1, 128), lambda c: (0, c))
    return pl.pallas_call(
        body, name="conv_bwd", grid=(ch // 128,),
        in_specs=[strip, wsp, bsp, strip],
        out_specs=[strip, wsp, bsp],
        out_shape=[jax.ShapeDtypeStruct(x.shape, F32), jax.ShapeDtypeStruct(w.shape, F32),
                   jax.ShapeDtypeStruct(b.shape, F32)],
        compiler_params=_cp(("parallel",)),
    )(x, w, b, dy)


@jax.custom_vjp
def conv_silu(x, w, b):
    return _conv_fwd_call(x, w, b)


def _conv_silu_fwd(x, w, b):
    return _conv_fwd_call(x, w, b), (x, w, b)


def _conv_silu_bwd(res, dy):
    return tuple(_conv_bwd_call(*res, dy))


conv_silu.defvjp(_conv_silu_fwd, _conv_silu_bwd)


def _loss_call(h, wf, target):
    seq, d = h.shape
    nb = seq // BLK

    def body(h_ref, w_ref, t_ref, loss_ref, dh_ref, dw_ref):
        i = pl.program_id(0)
        live = (i > 0).astype(F32)
        tgt = t_ref[...]

        def fn(hh, ww):
            err = _rms_fn(hh, ww) - tgt
            return 0.5 * live * jnp.sum(jnp.mean(err * err, axis=-1, keepdims=True), axis=0, keepdims=True)

        val, vjp = jax.vjp(fn, h_ref[...], w_ref[...])
        dh, dw = vjp(jnp.ones((1, 1), F32))
        dh_ref[...] = dh

        @pl.when(i == 0)
        def _():
            loss_ref[...] = jnp.zeros_like(loss_ref)
            dw_ref[...] = jnp.zeros_like(dw_ref)

        loss_ref[...] += val
        dw_ref[...] += dw

    return pl.pallas_call(
        body, name="loss_head", grid=(nb,),
        in_specs=[pl.BlockSpec((BLK, d), lambda i: (i, 0)), pl.BlockSpec((1, d), lambda i: (0, 0)),
                  pl.BlockSpec((BLK, d), lambda i: (jnp.maximum(i - 1, 0), 0))],
        out_specs=[pl.BlockSpec((1, 1), lambda i: (0, 0)), pl.BlockSpec((BLK, d), lambda i: (i, 0)),
                   pl.BlockSpec((1, d), lambda i: (0, 0))],
        out_shape=[jax.ShapeDtypeStruct((1, 1), F32), jax.ShapeDtypeStruct(h.shape, F32),
                   jax.ShapeDtypeStruct((1, d), F32)],
        compiler_params=_cp(("arbitrary",)),
    )(h, wf, target)


def _make_loss_head(target):
    @jax.custom_vjp
    def head(h, wf):
        return _loss_call(h, wf, target)[0][0, 0]

    def fwd(h, wf):
        loss, dh, dw = _loss_call(h, wf, target)
        return loss[0, 0], (dh, dw)

    def bwd(res, g):
        return g * res[0], g * res[1]

    head.defvjp(fwd, bwd)
    return head


_IN_SEGS = (("q", 0, 1024), ("k", 1024, 1024), ("v", 2048, 1024), ("gate", 3072, 1024), ("z", 4112, 1024),
            ("xbc", 5136, 2048), ("cq", 7200, 1024), ("ck", 8224, 256), ("cv", 8480, 256), ("gl", 8736, 3072),
            ("b", 4096, 8), ("a", 4104, 8), ("dt", 7184, 16))
_IN_PAD = sum(_SPLIT) - sum(n for _, _, n in _IN_SEGS)


_MATMUL = ("w_in", "w_proj_gdn", "w_proj_ssd", "w_proj_swa", "w_out", "w_up", "w_down")
_LATE = _MATMUL[1:]


def _late_weights(gathered):
    g = dict(zip(_LATE, gathered))
    full = {n: g[n].reshape(D_MODEL, D_MODEL) for n in _LATE[:4]}
    full["w_up"] = g["w_up"].transpose(1, 0, 2).reshape(D_MODEL, D_FF)
    full["w_down"] = g["w_down"].reshape(D_FF, D_MODEL)
    full.update({n + "_t": t.T for n, t in list(full.items())})
    return full


def _layer(h, p, w_in, w_in_t, slot, late_shards, next_shards=(), exchange_slots=()):
    wb = {"w_in": w_in, "w_in_t": w_in_t}

    def proj(t, name):
        return mm(t, wb[name], wb[name + "_t"], slot[name])

    q_pre, k_pre, v_pre, gate, z, xbc_pre, cq, ckv, gl, small = mm_split(
        rms_op(h, p["norm1_w"].reshape(1, -1)), w_in, w_in_t, slot["w_in"])

    gcw = p["gdn_conv_w"]
    nob = jnp.zeros((1, GDN_H * GDN_D), F32)
    qa = conv_silu(q_pre, gcw[:, :1024], nob)
    ka = conv_silu(k_pre, gcw[:, 1024:2048], nob)
    va = conv_silu(v_pre, gcw[:, 2048:], nob)
    y_gdn, late, placeholders = gdn_core(
        qa, ka, va, gate, small, p["gdn_a_log"].reshape(GDN_H, 1, 1), p["gdn_dt_bias"].reshape(GDN_H, 1, 1),
        p["gdn_norm_w"].reshape(1, GDN_D), tuple(late_shards), tuple(exchange_slots))
    wb.update(_late_weights(late))

    xbc = conv_silu(xbc_pre, p["ssd_conv_w"], p["ssd_conv_b"].reshape(1, -1))
    y_ssd, gathered = ssd_core(xbc, z, small, p["ssd_dt_bias"].reshape(SSD_H, 1, 1),
                               p["ssd_a_log"].reshape(SSD_H, 1, 1), p["ssd_d"].reshape(SSD_H, 1, 1),
                               p["ssd_norm_w"].reshape(SSD_H, 1, SSD_P), tuple(next_shards))

    y_swa = swa_core(cq, ckv, p["swa_sinks"].reshape(SWA_QH, 1, 1))

    merged = merge_op(proj(y_gdn, "w_proj_gdn"), proj(y_ssd, "w_proj_ssd"), proj(y_swa, "w_proj_swa"), gl)
    h = h + proj(merged, "w_out")
    a1 = proj(rms_op(h, p["norm2_w"].reshape(1, -1)), "w_up")
    return h + proj(relu2_op(a1), "w_down"), gathered, placeholders


_PER_LAYER = ("norm1_w", "gdn_conv_w", "gdn_a_log", "gdn_dt_bias", "gdn_norm_w", "ssd_conv_w", "ssd_conv_b",
              "ssd_dt_bias", "ssd_a_log", "ssd_d", "ssd_norm_w", "swa_sinks", "norm2_w")


def _embed(x, meta):
    return jnp.concatenate([jnp.zeros((NPAD, D_MODEL), F32), meta, x], axis=0)


_IN_SHARD = 1476


def _in_pieces():
    out = []
    for _, s, n in _IN_SEGS:
        c = s
        while c < s + n:
            d = c // _IN_SHARD
            e = min(s + n, (d + 1) * _IN_SHARD)
            out.append((d, c - d * _IN_SHARD, e - d * _IN_SHARD))
            c = e
    return out


def _in_pieces_back():
    start, off = {}, 0
    for _, s, n in _IN_SEGS:
        start[s] = off
        off += n
    out = [[] for _ in range(N_DEV)]
    for _, s, n in sorted(_IN_SEGS, key=lambda t: t[1]):
        c = s
        while c < s + n:
            d = c // _IN_SHARD
            e = min(s + n, (d + 1) * _IN_SHARD)
            out[d].append((start[s] + c - s, start[s] + e - s))
            c = e
    return out


def _regroup_w_in(stacked):
    parts = [stacked[d, :, lo:hi] for d, lo, hi in _in_pieces()]
    return jnp.concatenate(parts + [jnp.zeros((D_MODEL, _IN_PAD), stacked.dtype)], axis=1)


def _ungroup_w_in(g):
    return [jnp.concatenate([g[:, lo:hi] for lo, hi in pieces], axis=1) for pieces in _in_pieces_back()]


def _position():
    return lax.axis_index("x"), lax.axis_index("y"), lax.axis_index("c")


_ANY = pl.BlockSpec(memory_space=pl.ANY)


def _chip_of(x, y, k):
    return (1 - x if k & 1 else x, 1 - y if k & 2 else y)


def _allgather_call(shards, name):
    n = len(shards)

    def body(*refs):
        start, relay, finish = _gather_phases(refs[:n], refs[n:2 * n], *refs[2 * n:])
        start()
        relay()
        finish()

    return pl.pallas_call(
        body, name=name,
        out_shape=_gather_out_shapes(shards),
        in_specs=[_ANY] * n, out_specs=[_ANY] * n,
        scratch_shapes=_gather_sems(n),
    )(*shards)


def _gather_out_shapes(shards):
    return [jax.ShapeDtypeStruct((N_DEV, *s.shape), s.dtype) for s in shards]


def _gather_sems(n):
    return [pltpu.SemaphoreType.DMA((7 * n,)), pltpu.SemaphoreType.DMA((7 * n,)), pltpu.SemaphoreType.DMA((n,))]


def _gather_phases(x_refs, out_refs, send_sems, recv_sems, local_sems):
    n = len(x_refs)
    x, y, c = _position()
    me, sibling = (x, y, c), (x, y, 1 - c)
    chips = [_chip_of(x, y, k) for k in (1, 2, 3)]

    def slab(a, px, py, pc):
        return out_refs[a].at[4 * px + 2 * py + pc]

    def copy(a, k, block, to, src=None):
        return pltpu.make_async_remote_copy(
            src_ref=slab(a, *block) if src is None else src, dst_ref=slab(a, *block),
            send_sem=send_sems.at[7 * a + k], recv_sem=recv_sems.at[7 * a + k], device_id=to, device_id_type=MESH)

    def mine():
        return [pltpu.make_async_copy(x_refs[a], slab(a, *me), local_sems.at[a]) for a in range(n)]

    def first():
        out = []
        for a in range(n):
            out.append(copy(a, 0, me, sibling, src=x_refs[a]))
            out += [copy(a, 1 + j, me, (*chip, c), src=x_refs[a]) for j, chip in enumerate(chips)]
        return out

    def passed():
        return [copy(a, 4 + j, (*chip, c), sibling) for j, chip in enumerate(chips) for a in range(n)]

    def start():
        for cp in mine() + first():
            cp.start()

    def relay():
        for j, chip in enumerate(chips):
            for a in range(n):
                copy(a, 1 + j, (*chip, c), me).wait_recv()
                copy(a, 4 + j, (*chip, c), sibling).start()

    def finish():
        for a in range(n):
            copy(a, 0, sibling, me).wait_recv()
        for j, chip in enumerate(chips):
            for a in range(n):
                copy(a, 4 + j, (*chip, 1 - c), me).wait_recv()
        for cp in first() + passed():
            cp.wait_send()
        for cp in mine():
            cp.wait()

    return start, relay, finish


def _sibling_exchange_call(for_c0, for_c1, name):
    n = len(for_c0)

    def body(*refs):
        c0_refs, c1_refs, out_refs = refs[:n], refs[n:2 * n], refs[2 * n:3 * n]
        send_sems, recv_sems = refs[3 * n:]
        x, y, c = _position()

        def copies(src_refs):
            return [pltpu.make_async_remote_copy(
                src_ref=src_refs[a].at[q], dst_ref=out_refs[a].at[q],
                send_sem=send_sems.at[4 * a + q], recv_sem=recv_sems.at[4 * a + q],
                device_id=(x, y, 1 - c), device_id_type=MESH) for a in range(n) for q in range(4)]

        @pl.when(c == 0)
        def _():
            for cp in copies(c1_refs):
                cp.start()

        @pl.when(c == 1)
        def _():
            for cp in copies(c0_refs):
                cp.start()

        waits = copies(c0_refs)
        for cp in waits:
            cp.wait_recv()
        for cp in waits:
            cp.wait_send()

    return pl.pallas_call(
        body, name=name,
        out_shape=[jax.ShapeDtypeStruct(g.shape, g.dtype) for g in for_c0],
        in_specs=[_ANY] * (2 * n), out_specs=[_ANY] * n,
        scratch_shapes=[pltpu.SemaphoreType.DMA((4 * n,)), pltpu.SemaphoreType.DMA((4 * n,))],
    )(*for_c0, *for_c1)


def _chip_exchange_call(partials, name):
    n = len(partials)

    def body(*refs):
        start, finish = _chip_exchange_phases(refs[:n], refs[n:2 * n], *refs[2 * n:])
        start()
        finish()

    return pl.pallas_call(
        body, name=name,
        out_shape=_chip_exchange_out_shapes(partials),
        in_specs=[_ANY] * n, out_specs=[_ANY] * n,
        scratch_shapes=_chip_exchange_sems(n),
    )(*partials)


def _chip_exchange_out_shapes(partials):
    return [jax.ShapeDtypeStruct((3, *p.shape[1:]), p.dtype) for p in partials]


def _chip_exchange_sems(n):
    return [pltpu.SemaphoreType.DMA((3 * n,)), pltpu.SemaphoreType.DMA((3 * n,))]


def _chip_exchange_phases(p_refs, out_refs, send_sems, recv_sems):
    n = len(p_refs)
    x, y, c = _position()

    def copies():
        out = []
        for a in range(n):
            for k in (1, 2, 3):
                px, py = _chip_of(x, y, k)
                out.append(pltpu.make_async_remote_copy(
                    src_ref=p_refs[a].at[2 * px + py], dst_ref=out_refs[a].at[k - 1],
                    send_sem=send_sems.at[3 * a + k - 1], recv_sem=recv_sems.at[3 * a + k - 1],
                    device_id=(px, py, c), device_id_type=MESH))
        return out

    def start():
        for cp in copies():
            cp.start()

    def finish():
        for cp in copies():
            cp.wait_recv()
        for cp in copies():
            cp.wait_send()

    return start, finish


def _chip_partial_call(for_c0, for_c1, sib, tr, name):
    _, r, c = sib.shape

    def body(c0_ref, c1_ref, s_ref, own_ref, out_ref):
        x, y, core = _position()
        mine = jnp.where(core == 0, c0_ref[...], c1_ref[...])
        partial = mine + s_ref[...]
        own = jnp.zeros((tr, c), F32)
        for q in range(4):
            own = jnp.where(2 * x + y == q, partial[q], own)
        own_ref[...] = own
        out_ref[...] = partial.astype(BF16)

    four = pl.BlockSpec((4, tr, c), lambda i: (0, i, 0))
    return pl.pallas_call(
        body, name=name, grid=(r // tr,),
        in_specs=[four, four, four],
        out_specs=[pl.BlockSpec((tr, c), lambda i: (i, 0)), four],
        out_shape=[jax.ShapeDtypeStruct((r, c), F32), jax.ShapeDtypeStruct((4, r, c), BF16)],
        compiler_params=_cp(("parallel",)),
    )(for_c0, for_c1, sib)


def _adamw_call(parts, w, m, v, tr, name):
    ns, r, c = w.shape
    counts = [len(p) for p in parts]
    flat_parts = [a for p in parts for a in p]

    def body(*refs):
        p_refs = refs[:len(flat_parts)]
        w_ref, m_ref, v_ref, g_ref, d_ref, nm_ref, nv_ref = refs[len(flat_parts):]
        at = 0
        for s in range(ns):
            g = None
            for p_ref in p_refs[at:at + counts[s]]:
                for j in range(p_ref.shape[0]):
                    term = p_ref[j].astype(F32)
                    g = term if g is None else g + term
            at += counts[s]
            nm = ADAM_B1 * m_ref[s] + (1.0 - ADAM_B1) * g
            nv = ADAM_B2 * v_ref[s] + (1.0 - ADAM_B2) * (g * g)
            m_hat = nm / (1.0 - ADAM_B1 ** ADAM_STEP)
            v_hat = nv / (1.0 - ADAM_B2 ** ADAM_STEP)
            g_ref[s] = g
            d_ref[s] = -ADAM_LR * (m_hat / (jnp.sqrt(v_hat) + ADAM_EPS) + ADAM_WD * w_ref[s])
            nm_ref[s] = nm
            nv_ref[s] = nv

    slabs = pl.BlockSpec((ns, tr, c), lambda i: (0, i, 0))
    return pl.pallas_call(
        body, name=name, grid=(r // tr,),
        in_specs=[pl.BlockSpec((a.shape[0], tr, c), lambda i: (0, i, 0)) for a in flat_parts] + [slabs] * 3,
        out_specs=[slabs] * 4,
        out_shape=[jax.ShapeDtypeStruct((ns, r, c), F32)] * 4,
        compiler_params=_cp(("parallel",)),
    )(*flat_parts, w, m, v)


_WEIGHTS = ("meta_tokens", "norm1_w", "w_in", "gdn_conv_w", "gdn_a_log", "gdn_dt_bias", "gdn_norm_w", "ssd_conv_w",
            "ssd_conv_b", "ssd_dt_bias", "ssd_a_log", "ssd_d", "ssd_norm_w", "swa_sinks", "w_proj_gdn", "w_proj_ssd",
            "w_proj_swa", "w_out", "norm2_w", "w_up", "w_down", "final_norm_w")
_SHARD_AXIS = {"meta_tokens": 1, "w_in": 2, "gdn_conv_w": 2, "ssd_conv_w": 2, "w_proj_gdn": 1, "w_proj_ssd": 1,
               "w_proj_swa": 1, "w_out": 1, "w_up": 2, "w_down": 1}
_BIG = tuple(n for n in _WEIGHTS if n in _SHARD_AXIS)
_SMALL = tuple(n for n in _WEIGHTS if n not in _SHARD_AXIS)
FLAT_C = 1024


def _pack(arrs, rows, lead=()):
    flat = jnp.concatenate([a.reshape(*lead, -1) for a in arrs], axis=-1)
    pad = rows * FLAT_C - flat.shape[-1]
    flat = jnp.pad(flat, [(0, 0)] * len(lead) + [(0, pad)])
    return flat.reshape(*lead, rows, FLAT_C)


def _unpack(flat, shapes, lead=()):
    flat = flat.reshape(*lead, -1)
    out, off = [], 0
    for s in shapes:
        n = math.prod(s)
        out.append(flat[..., off:off + n].reshape(*lead, *s))
        off += n
    return out


def _rows_for(shapes):
    n = sum(math.prod(s) for s in shapes)
    return -(-n // (FLAT_C * 8)) * 8


def _rows_tile(r, c):
    if r <= 256:
        return r
    return 128 if c > 1024 else 256


def _join(stacked, axis):
    moved = jnp.moveaxis(stacked, 0, axis)
    return moved.reshape(*moved.shape[:axis], -1, *moved.shape[axis + 2:])


def _unjoin(full, axis):
    cut = full.reshape(*full.shape[:axis], N_DEV, full.shape[axis] // N_DEV, *full.shape[axis + 1:])
    return jnp.moveaxis(cut, axis, 0)


def kernel(x, meta_tokens, norm1_w, w_in, gdn_conv_w, gdn_a_log, gdn_dt_bias, gdn_norm_w, ssd_conv_w, ssd_conv_b,
           ssd_dt_bias, ssd_a_log, ssd_d, ssd_norm_w, swa_sinks, w_proj_gdn, w_proj_ssd, w_proj_swa, w_out, norm2_w,
           w_up, w_down, final_norm_w, loss_target, m_meta_tokens, m_norm1_w, m_w_in, m_gdn_conv_w, m_gdn_a_log,
           m_gdn_dt_bias, m_gdn_norm_w, m_ssd_conv_w, m_ssd_conv_b, m_ssd_dt_bias, m_ssd_a_log, m_ssd_d, m_ssd_norm_w,
           m_swa_sinks, m_w_proj_gdn, m_w_proj_ssd, m_w_proj_swa, m_w_out, m_norm2_w, m_w_up, m_w_down,
           m_final_norm_w, v_meta_tokens, v_norm1_w, v_w_in, v_gdn_conv_w, v_gdn_a_log, v_gdn_dt_bias, v_gdn_norm_w,
           v_ssd_conv_w, v_ssd_conv_b, v_ssd_dt_bias, v_ssd_a_log, v_ssd_d, v_ssd_norm_w, v_swa_sinks, v_w_proj_gdn,
           v_w_proj_ssd, v_w_proj_swa, v_w_out, v_norm2_w, v_w_up, v_w_down, v_final_norm_w):
    args = (meta_tokens, norm1_w, w_in, gdn_conv_w, gdn_a_log, gdn_dt_bias, gdn_norm_w, ssd_conv_w, ssd_conv_b,
            ssd_dt_bias, ssd_a_log, ssd_d, ssd_norm_w, swa_sinks, w_proj_gdn, w_proj_ssd, w_proj_swa, w_out, norm2_w,
            w_up, w_down, final_norm_w, m_meta_tokens, m_norm1_w, m_w_in, m_gdn_conv_w, m_gdn_a_log,
            m_gdn_dt_bias, m_gdn_norm_w, m_ssd_conv_w, m_ssd_conv_b, m_ssd_dt_bias, m_ssd_a_log, m_ssd_d, m_ssd_norm_w,
            m_swa_sinks, m_w_proj_gdn, m_w_proj_ssd, m_w_proj_swa, m_w_out, m_norm2_w, m_w_up, m_w_down,
            m_final_norm_w, v_meta_tokens, v_norm1_w, v_w_in, v_gdn_conv_w, v_gdn_a_log, v_gdn_dt_bias, v_gdn_norm_w,
            v_ssd_conv_w, v_ssd_conv_b, v_ssd_dt_bias, v_ssd_a_log, v_ssd_d, v_ssd_norm_w, v_swa_sinks, v_w_proj_gdn,
            v_w_proj_ssd, v_w_proj_swa, v_w_out, v_norm2_w, v_w_up, v_w_down, v_final_norm_w)
    nw = len(_WEIGHTS)
    w = dict(zip(_WEIGHTS, args[:nw]))
    m = dict(zip(_WEIGHTS, args[nw:2 * nw]))
    v = dict(zip(_WEIGHTS, args[2 * nw:]))

    depth = w["w_in"].shape[0]
    small_shapes = [w[n].shape for n in _SMALL]
    small_rows = _rows_for(small_shapes)

    def flat2(t):
        return t.reshape(-1, t.shape[-1])

    tiny_names = [n for n in _BIG if n not in _MATMUL]

    def shard(n, l):
        return w[n][l].astype(BF16)

    first = _allgather_call([shard("w_in", 0)] + [flat2(w[n]) for n in tiny_names], "gather_weights")
    w_in_stacked = first[0]
    joined = {n: _join(t.reshape(N_DEV, *w[n].shape), _SHARD_AXIS[n]) for n, t in zip(tiny_names, first[1:])}
    slot_shapes = {"w_in": (D_MODEL, sum(_SPLIT)), "w_up": (D_MODEL, D_FF), "w_down": (D_FF, D_MODEL)}
    slot_shapes.update({n: (D_MODEL, D_MODEL) for n in _LATE[:4]})

    def layer_fn(l, w_in_full, late_shards, next_shards):
        w_in_t = w_in_full.T
        if l == 0:
            def fn(x_rows, meta, p, slot, exchange_slots):
                out, g, placeholders = _layer(_embed(x_rows, meta), p, w_in_full, w_in_t, slot, late_shards,
                                              next_shards, exchange_slots)
                return (out, placeholders), g
        else:
            def fn(h_in, p, slot, exchange_slots):
                out, g, placeholders = _layer(h_in, p, w_in_full, w_in_t, slot, late_shards, next_shards,
                                              exchange_slots)
                return (out, placeholders), g
        return fn

    h, vjps = None, []
    for l in range(depth):
        slot = {n: jnp.zeros(s, F32) for n, s in slot_shapes.items()}
        p = {n: (joined[n][l] if n in joined else w[n][l]) for n in _PER_LAYER}
        more = l + 1 < depth
        next_shards = [shard("w_in", l + 1)] if more else []
        exchange_slots = tuple(jnp.zeros((3, *w[n][l + 1].shape), BF16) for n in _MATMUL) if more else ()
        lead = (x[0], joined["meta_tokens"]) if l == 0 else (h,)
        fn = layer_fn(l, _regroup_w_in(w_in_stacked), [shard(n, l) for n in _LATE], next_shards)
        (h, _), vjp, g_next = jax.vjp(fn, *lead, p, slot, exchange_slots, has_aux=True)
        if more:
            w_in_stacked = g_next[0]
        vjps.append(vjp)
    loss, head_vjp = jax.vjp(_make_loss_head(loss_target[0]), h, w["final_norm_w"].reshape(1, -1))
    dh, d_final = head_vjp(jnp.ones((), F32))
    loss = lax.psum(loss, ("x", "y", "c"))

    def by_core(name, g):
        if name == "w_in":
            shards = _ungroup_w_in(g)
            return jnp.stack(shards[0::2]), jnp.stack(shards[1::2])
        if name == "w_up":
            t = g.reshape(D_MODEL, 4, 2, D_FF // N_DEV)
            return t[:, :, 0].transpose(1, 0, 2), t[:, :, 1].transpose(1, 0, 2)
        t = g.reshape(4, 2, -1, g.shape[-1])
        return t[:, 0], t[:, 1]

    own, incoming, layer_grads, outgoing = {}, {}, [None] * depth, ()
    for l in reversed(range(depth)):
        if l == 0:
            gx, d_meta, dp, dslot, arrived = vjps[0]((dh, tuple(outgoing)))
        else:
            dh, dp, dslot, arrived = vjps[l]((dh, tuple(outgoing)))
        incoming.update({(n, l + 1): t for n, t in zip(_MATMUL, arrived)})
        layer_grads[l] = dp
        todo = [((n, l), dslot[n]) for n in _MATMUL]
        if l == 0:
            full_grads = {"meta_tokens": d_meta}
            full_grads.update({n: jnp.stack([layer_grads[k][n] for k in range(depth)]) for n in tiny_names[1:]})
            todo += [((n, None), _unjoin(full_grads[n], _SHARD_AXIS[n]).reshape(N_DEV, -1, w[n].shape[-1]))
                     for n in tiny_names]
        pairs = [by_core(u[0], g) for u, g in todo]
        from_sibling = _sibling_exchange_call([a for a, _ in pairs], [b for _, b in pairs], "grads_to_sibling_%d" % l)
        outgoing = []
        for (u, _), (a0, a1), s in zip(todo, pairs, from_sibling):
            own[u], part = _chip_partial_call(a0, a1, s, _rows_tile(s.shape[1], s.shape[2]), "chip_partial_" + u[0])
            outgoing.append(part)
        if l == 0:
            incoming.update(zip([u for u, _ in todo], _chip_exchange_call(outgoing, "grads_to_chips")))

    g_small = {n: jnp.stack([layer_grads[k][n] for k in range(depth)]) for n in _SMALL if n != "final_norm_w"}
    g_small["final_norm_w"] = d_final.reshape(-1)

    by_name = {}
    for n in _BIG:
        layers = list(range(depth)) if n in _MATMUL else [None]
        parts = [[own[n, l][None], incoming[n, l]] for l in layers]
        r, c = own[n, layers[0]].shape
        stacked = [d[n].reshape(len(layers), r, c) for d in (w, m, v)]
        res = _adamw_call(parts, *stacked, _rows_tile(r, c), "adamw_" + n)
        by_name[n] = [t.reshape(w[n].shape) for t in res]

    small_parts = _allgather_call([_pack([g_small[n] for n in _SMALL], small_rows)], "gather_small_grads")
    small_out = _adamw_call([small_parts], *[_pack([d[n] for n in _SMALL], small_rows)[None] for d in (w, m, v)],
                            small_rows, "adamw_replicated")
    for kind in range(4):
        for n, t in zip(_SMALL, _unpack(small_out[kind][0], small_shapes)):
            by_name.setdefault(n, [None] * 4)[kind] = t

    outs = [by_name[n][kind] for kind in range(4) for n in _WEIGHTS]
    return (loss, gx[None], *outs)
```

```python
import functools
import math

import jax
import jax.numpy as jnp
from jax import lax
from jax.experimental import pallas as pl
from jax.experimental.pallas import tpu as pltpu

F32 = jnp.float32
BF16 = jnp.bfloat16
HI = lax.Precision.HIGH
NEG = -1e30

D_MODEL = 1024
N_META = 16
BLK = 128
NPAD = BLK - N_META
RMS_EPS = 1e-6
L2_EPS = 1e-6
CONV_K = 4

GDN_H, GDN_D, GDN_C = 8, 128, 64
SSD_H, SSD_P, SSD_G, SSD_N = 16, 64, 4, 128
SSD_HPG = SSD_H // SSD_G
SWA_QH, SWA_KVH, SWA_D = 16, 4, 64
SWA_REP = SWA_QH // SWA_KVH
D_FF = 4 * D_MODEL

N_DEV = 8
MESH = pl.DeviceIdType.MESH

ADAM_LR, ADAM_B1, ADAM_B2, ADAM_EPS, ADAM_WD, ADAM_STEP = 0.001, 0.9, 0.999, 1e-08, 0.01, 10

VMEM_LIMIT = 56 * 1024 * 1024


def _cp(sem=None):
    return pltpu.CompilerParams(dimension_semantics=sem, vmem_limit_bytes=VMEM_LIMIT)


def _dot(a, b, ca, cb, prec=HI):
    return lax.dot_general(a, b, (((ca,), (cb,)), ((), ())), precision=prec, preferred_element_type=F32)


def _nn(a, b, prec=HI):
    return _dot(a, b, 1, 0, prec)


def _nt(a, b, prec=HI):
    return _dot(a, b, 1, 1, prec)


def _tn(a, b, prec=HI):
    return _dot(a, b, 0, 0, prec)


def _bdot(a, b, ca, cb):
    return lax.dot_general(a.astype(BF16), b.astype(BF16), (((ca,), (cb,)), ((), ())), preferred_element_type=F32)


@jax.custom_vjp
def _lo_nn(a, b):
    return _bdot(a, b, 1, 0)


_lo_nn.defvjp(lambda a, b: (_bdot(a, b, 1, 0), (a, b)),
              lambda r, d: (_bdot(d, r[1], 1, 1), _bdot(r[0], d, 0, 0)))


@jax.custom_vjp
def _lo_nt(a, b):
    return _bdot(a, b, 1, 1)


_lo_nt.defvjp(lambda a, b: (_bdot(a, b, 1, 1), (a, b)),
              lambda r, d: (_bdot(d, r[1], 1, 0), _bdot(d, r[0], 0, 0)))


@jax.custom_vjp
def _lo_tn(a, b):
    return _bdot(a, b, 0, 0)


_lo_tn.defvjp(lambda a, b: (_bdot(a, b, 0, 0), (a, b)),
              lambda r, d: (_bdot(r[1], d, 1, 1), _bdot(r[0], d, 1, 0)))


def _iota2(n, m, axis):
    return lax.broadcasted_iota(jnp.int32, (n, m), axis)


def _silu(x):
    return x * jax.nn.sigmoid(x)


def _softplus(x):
    return jnp.maximum(x, 0.0) + jnp.log(1.0 + jnp.exp(-jnp.abs(x)))


def _row_of(col):
    n = col.shape[0]
    return jnp.broadcast_to(col, (n, n)).T


def _cumsum_col(col):
    n = col.shape[0]
    tril = (_iota2(n, n, 0) >= _iota2(n, n, 1)).astype(F32)
    return _nn(tril, col)


def _tri_inv(a):
    n = a.shape[0]
    r, c = _iota2(n, n, 0), _iota2(n, n, 1)
    eye = (r == c).astype(F32)
    blk = jnp.right_shift(r, 4) == jnp.right_shift(c, 4)
    d = jnp.where(blk, a, 0.0)
    off = a - d
    d2 = _nn(d, d)
    d4 = _nn(d2, d2)
    d8 = _nn(d4, d4)
    td = _nn(_nn(_nn(eye - d, eye + d2), eye + d4), eye + d8)
    m = _nn(td, off)
    m2 = _nn(m, m)
    return _nn(_nn(eye - m, eye + m2), td)


@jax.custom_vjp
def _tri_solve(a, inv, rhs):
    return _nn(inv, rhs)


def _tri_solve_fwd(a, inv, rhs):
    sol = _nn(inv, rhs)
    return sol, (inv, sol)


def _tri_solve_bwd(res, dsol):
    inv, sol = res
    drhs = _nn(inv.T, dsol)
    return -_nt(drhs, sol), jnp.zeros_like(inv), drhs


_tri_solve.defvjp(_tri_solve_fwd, _tri_solve_bwd)


def _gdn_chunk(qa, ka, va, gate, a_raw, b_raw, s, a_log, dt_bias, norm_w, valid, inv=None, want_inv=False):
    c = qa.shape[0]
    q = qa * lax.rsqrt(jnp.sum(qa * qa, axis=-1, keepdims=True) + L2_EPS) * (GDN_D ** -0.5)
    k = ka * lax.rsqrt(jnp.sum(ka * ka, axis=-1, keepdims=True) + L2_EPS)
    beta = jax.nn.sigmoid(b_raw)
    g = -jnp.exp(a_log) * _softplus(a_raw + dt_bias) * valid
    gam = _cumsum_col(g)
    gam_row = _row_of(gam)
    r, cc = _iota2(c, c, 0), _iota2(c, c, 1)
    decay = jnp.exp(jnp.where(r >= cc, gam - gam_row, NEG))
    kb = k * beta
    a = jnp.where(r > cc, _lo_nt(kb, k) * decay, 0.0)
    egam = jnp.exp(gam)
    if inv is None:
        inv = _tri_inv(lax.stop_gradient(a))
    sol = _tri_solve(a, inv, jnp.concatenate([va * beta, kb * egam], axis=1))
    u = sol[:, :GDN_D]
    w = sol[:, GDN_D:]
    attn = _lo_nt(q, k) * decay
    g_last = jnp.sum(g, axis=0, keepdims=True)
    k_tail = k * jnp.exp(g_last - gam)
    v_new = u - _lo_nn(w, s)
    o = _lo_nn(q * egam, s) + _lo_nn(attn, v_new)
    s_new = s * jnp.exp(g_last) + _lo_tn(k_tail, v_new)
    y = o * lax.rsqrt(jnp.mean(o * o, axis=-1, keepdims=True) + RMS_EPS) * norm_w * _silu(gate)
    return (y, s_new, inv) if want_inv else (y, s_new)


def _valid_col(row0, n):
    return (row0 + _iota2(n, 1, 0) >= NPAD).astype(F32)


GDN_HB = GDN_H

SM_B, SM_A, SM_DT, SM_W = 0, 8, 16, 128


def _pick_cols(sm, first, n):
    return jnp.stack([sm[:, first + j:first + j + 1] for j in range(n)])


def _spread_cols(cols, first):
    lane = _iota2(1, SM_W, 1)
    out = None
    for j in range(cols.shape[0]):
        term = cols[j] * (lane == first + j).astype(F32)
        out = term if out is None else out + term
    return out


def _widen(t, width):
    if width == t.shape[1]:
        return t
    return jnp.concatenate([t, jnp.zeros((t.shape[0], width - t.shape[1]), t.dtype)], axis=1)


def _gdn_specs(nc, rev):
    ci = (lambda i: nc - 1 - i) if rev else (lambda i: i)
    hb = GDN_HB
    tile = pl.BlockSpec((GDN_C, hb * GDN_D), lambda h, i: (ci(i), h))
    col = pl.BlockSpec((GDN_C, SM_W), lambda h, i: (ci(i), 0))
    scal = pl.BlockSpec((hb, 1, 1), lambda h, i: (h, 0, 0))
    nw = pl.BlockSpec((1, GDN_D), lambda h, i: (0, 0))
    st = pl.BlockSpec((hb, 1, GDN_D, GDN_D), lambda h, i: (h, ci(i), 0, 0))
    return tile, col, scal, nw, st


def _lanes(j):
    return slice(j * GDN_D, (j + 1) * GDN_D)


def _by_head(ref):
    return jnp.stack([ref[:, _lanes(j)] for j in range(GDN_HB)])


def _gdn_fwd_call(q, k, v, gate, small, a_log, dt_bias, norm_w, shards=()):
    seq = q.shape[0]
    nc = seq // GDN_C
    ns = len(shards)
    tile, col, scal, nw, st = _gdn_specs(nc, False)

    def body(*refs):
        q_ref, k_ref, v_ref, g_ref, sm_ref, al_ref, dt_ref, nw_ref = refs[:8]
        y_ref, st_ref, inv_ref = refs[8 + ns:11 + ns]
        s_scr = refs[11 + 2 * ns]
        i = pl.program_id(1)
        if ns:
            start, relay, finish = _gather_phases(refs[8:8 + ns], refs[11 + ns:11 + 2 * ns], *refs[12 + 2 * ns:])
            pl.when(i == 0)(start)
            pl.when(i == nc - 1)(relay)

        @pl.when(i == 0)
        def _():
            s_scr[...] = jnp.zeros_like(s_scr)

        s = s_scr[...]
        st_ref[:, 0] = s
        sm = sm_ref[...]
        fn = jax.vmap(functools.partial(_gdn_chunk, valid=_valid_col(i * GDN_C, GDN_C), want_inv=True))
        y, s_new, inv = fn(_by_head(q_ref), _by_head(k_ref), _by_head(v_ref), _by_head(g_ref),
                           _pick_cols(sm, SM_A, GDN_H), _pick_cols(sm, SM_B, GDN_H), s,
                           al_ref[...], dt_ref[...], jnp.broadcast_to(nw_ref[...], (GDN_HB, 1, GDN_D)))
        for j in range(GDN_HB):
            y_ref[:, _lanes(j)] = y[j]
        inv_ref[:, 0] = inv
        s_scr[...] = s_new
        if ns:
            pl.when(i == nc - 1)(finish)

    return pl.pallas_call(
        body, name="gdn_fwd", grid=(GDN_H // GDN_HB, nc),
        in_specs=[tile, tile, tile, tile, col, scal, scal, nw] + [_ANY] * ns,
        out_specs=[tile, st, pl.BlockSpec((GDN_HB, 1, GDN_C, GDN_C), lambda h, i: (h, i, 0, 0))] + [_ANY] * ns,
        out_shape=[jax.ShapeDtypeStruct((seq, GDN_H * GDN_D), F32),
                   jax.ShapeDtypeStruct((GDN_H, nc, GDN_D, GDN_D), F32),
                   jax.ShapeDtypeStruct((GDN_H, nc, GDN_C, GDN_C), F32)] + _gather_out_shapes(shards),
        scratch_shapes=[pltpu.VMEM((GDN_HB, GDN_D, GDN_D), F32)] + (_gather_sems(ns) if ns else []),
        compiler_params=_cp(("parallel", "arbitrary")),
    )(q, k, v, gate, small, a_log, dt_bias, norm_w, *shards)


def _gdn_bwd_call(q, k, v, gate, small, a_log, dt_bias, norm_w, states, invs, dy, outgoing=()):
    seq = q.shape[0]
    nc = seq // GDN_C
    no = len(outgoing)
    tile, col, scal, nw, st = _gdn_specs(nc, True)
    nwh = pl.BlockSpec((GDN_HB, 1, GDN_D), lambda h, i: (h, 0, 0))
    inv_spec = pl.BlockSpec((GDN_HB, 1, GDN_C, GDN_C), lambda h, i: (h, nc - 1 - i, 0, 0))

    def body(*refs):
        q_ref, k_ref, v_ref, g_ref, sm_ref, al_ref, dt_ref, nw_ref, st_ref, inv_ref, dy_ref = refs[:11]
        dq_ref, dk_ref, dv_ref, dg_ref, dsm_ref, dal_ref, ddt_ref, dnw_ref = refs[11 + no:19 + no]
        ds_scr = refs[19 + 2 * no]
        i = pl.program_id(1)
        if no:
            start, finish = _chip_exchange_phases(refs[11:11 + no], refs[19 + no:19 + 2 * no], *refs[20 + 2 * no:])
            pl.when(i == 0)(start)

        @pl.when(i == 0)
        def _():
            ds_scr[...] = jnp.zeros_like(ds_scr)
            dal_ref[...] = jnp.zeros_like(dal_ref)
            ddt_ref[...] = jnp.zeros_like(ddt_ref)
            dnw_ref[...] = jnp.zeros_like(dnw_ref)

        sm = sm_ref[...]
        valid = _valid_col((nc - 1 - i) * GDN_C, GDN_C)
        kept = inv_ref[:, 0]

        def fn(*heads):
            return jax.vmap(lambda *t: _gdn_chunk(*t[:-1], valid=valid, inv=t[-1]))(*heads, kept)

        _, vjp = jax.vjp(fn, _by_head(q_ref), _by_head(k_ref), _by_head(v_ref), _by_head(g_ref),
                         _pick_cols(sm, SM_A, GDN_H), _pick_cols(sm, SM_B, GDN_H), st_ref[:, 0], al_ref[...],
                         dt_ref[...], jnp.broadcast_to(nw_ref[...], (GDN_HB, 1, GDN_D)))
        dq, dk, dv, dg, da, db, ds, dal, ddt, dnw = vjp((_by_head(dy_ref), ds_scr[...]))
        for j in range(GDN_HB):
            dq_ref[:, _lanes(j)] = dq[j]
            dk_ref[:, _lanes(j)] = dk[j]
            dv_ref[:, _lanes(j)] = dv[j]
            dg_ref[:, _lanes(j)] = dg[j]
        dsm_ref[...] = _widen(_spread_cols(da, SM_A) + _spread_cols(db, SM_B), dsm_ref.shape[1])
        ds_scr[...] = ds
        dal_ref[...] += dal
        ddt_ref[...] += ddt
        dnw_ref[...] += dnw
        if no:
            pl.when(i == nc - 1)(finish)

    big = jax.ShapeDtypeStruct((seq, GDN_H * GDN_D), F32)
    return pl.pallas_call(
        body, name="gdn_bwd", grid=(GDN_H // GDN_HB, nc),
        in_specs=[tile, tile, tile, tile, col, scal, scal, nw, st, inv_spec, tile] + [_ANY] * no,
        out_specs=[tile, tile, tile, tile, pl.BlockSpec((GDN_C, small.shape[1]), lambda h, i: (nc - 1 - i, 0)),
                   scal, scal, nwh] + [_ANY] * no,
        out_shape=[big, big, big, big, jax.ShapeDtypeStruct(small.shape, F32),
                   jax.ShapeDtypeStruct((GDN_H, 1, 1), F32), jax.ShapeDtypeStruct((GDN_H, 1, 1), F32),
                   jax.ShapeDtypeStruct((GDN_H, 1, GDN_D), F32)] + _chip_exchange_out_shapes(outgoing),
        scratch_shapes=[pltpu.VMEM((GDN_HB, GDN_D, GDN_D), F32)] + (_chip_exchange_sems(no) if no else []),
        compiler_params=_cp(("parallel", "arbitrary")),
    )(q, k, v, gate, small, a_log, dt_bias, norm_w, states, invs, dy, *outgoing)


@jax.custom_vjp
def gdn_core(q, k, v, gate, small, a_log, dt_bias, norm_w, shards, slots):
    y, _, _, *gathered = _gdn_fwd_call(q, k, v, gate, small, a_log, dt_bias, norm_w, shards)
    return y, tuple(gathered), tuple(jnp.zeros((4, *s.shape[1:]), s.dtype) for s in slots)


def _gdn_core_fwd(q, k, v, gate, small, a_log, dt_bias, norm_w, shards, slots):
    y, states, invs, *gathered = _gdn_fwd_call(q, k, v, gate, small, a_log, dt_bias, norm_w, shards)
    out = (y, tuple(gathered), tuple(jnp.zeros((4, *s.shape[1:]), s.dtype) for s in slots))
    return out, (q, k, v, gate, small, a_log, dt_bias, norm_w, states, invs, shards)


def _gdn_core_bwd(res, cts):
    *args, shards = res
    dy, _, outgoing = cts
    dq, dk, dv, dg, dsm, dal, ddt, dnw, *incoming = _gdn_bwd_call(*args, dy, outgoing)
    return (dq, dk, dv, dg, dsm, dal, ddt, jnp.sum(dnw, axis=0), tuple(jnp.zeros_like(s) for s in shards),
            tuple(incoming))


gdn_core.defvjp(_gdn_core_fwd, _gdn_core_bwd)


def _ssd_head(x, z, dt_raw, h, dt_bias, a_log, d_skip, bm, cm, cb, valid):
    c = bm.shape[0]
    r, cc = _iota2(c, c, 0), _iota2(c, c, 1)
    dtp = _softplus(dt_raw + dt_bias)
    x = x * valid
    adt = -jnp.exp(a_log) * dtp * valid
    xdt = x * dtp
    acum = _cumsum_col(adt)
    lmat = jnp.exp(jnp.where(r >= cc, acum - _row_of(acum), NEG))
    a_last = jnp.sum(adt, axis=0, keepdims=True)
    y = _lo_nn(cb * lmat, xdt) + _lo_nt(cm * jnp.exp(acum), h) + d_skip * x
    h_new = h * jnp.exp(a_last) + _lo_tn(xdt * jnp.exp(a_last - acum), bm)
    return y * _silu(z), h_new


SSD_SIDE = SSD_H


def _ssd_chunk(xs, z, bm, cm, dt_raw, h, dt_bias, a_log, d_skip, norm_w, valid):
    nh, c, p = xs.shape
    ng = bm.shape[0]
    hpg = nh // ng
    bm = bm * valid
    cm = cm * valid
    cb = jax.vmap(_lo_nt)(cm, bm)
    per_head = lambda t: jnp.repeat(t, hpg, axis=0)
    args = (xs, z, dt_raw, h, dt_bias, a_log, d_skip, per_head(bm), per_head(cm), per_head(cb))
    outs = [jax.vmap(functools.partial(_ssd_head, valid=valid))(*[t[s:s + SSD_SIDE] for t in args])
            for s in range(0, nh, SSD_SIDE)]
    ys = jnp.concatenate([o[0] for o in outs], axis=0)
    hs = jnp.concatenate([o[1] for o in outs], axis=0)
    ss = jnp.sum(jnp.sum(ys * ys, axis=-1, keepdims=True).reshape(ng, hpg, c, 1), axis=1, keepdims=True)
    rstd = lax.rsqrt(ss / (hpg * p) + RMS_EPS)
    return (ys.reshape(ng, hpg, c, p) * rstd).reshape(nh, c, p) * norm_w, hs


SSD_INNER = SSD_H * SSD_P
SSD_BC = SSD_G * SSD_N


def _split_lanes(t, n, w):
    return jnp.stack([t[:, j * w:(j + 1) * w] for j in range(n)])


def _join_lanes(t):
    return jnp.concatenate([t[j] for j in range(t.shape[0])], axis=1)


def _ssd_specs(nc, rev):
    ci = (lambda i: nc - 1 - i) if rev else (lambda i: i)
    wide = pl.BlockSpec((BLK, SSD_INNER), lambda i: (ci(i), 0))
    bmat = pl.BlockSpec((BLK, SSD_BC), lambda i: (ci(i), SSD_INNER // SSD_BC))
    cmat = pl.BlockSpec((BLK, SSD_BC), lambda i: (ci(i), SSD_INNER // SSD_BC + 1))
    xbc = pl.BlockSpec((BLK, SSD_INNER + 2 * SSD_BC), lambda i: (ci(i), 0))
    col = pl.BlockSpec((BLK, SM_W), lambda i: (ci(i), 0))
    scal = pl.BlockSpec((SSD_H, 1, 1), lambda i: (0, 0, 0))
    nw = pl.BlockSpec((SSD_H, 1, SSD_P), lambda i: (0, 0, 0))
    st = pl.BlockSpec((SSD_H, 1, SSD_P, SSD_N), lambda i: (0, ci(i), 0, 0))
    return wide, bmat, cmat, xbc, col, scal, nw, st


def _ssd_fwd_call(xbc, z, small, dt_bias, a_log, d_skip, norm_w, shards=()):
    seq = z.shape[0]
    nc = seq // BLK
    ns = len(shards)
    wide, bmat, cmat, _, col, scal, nw, st = _ssd_specs(nc, False)

    def body(*refs):
        x_ref, b_ref, c_ref, z_ref, sm_ref, db_ref, al_ref, ds_ref, nw_ref = refs[:9]
        y_ref, st_ref = refs[9 + ns:11 + ns]
        h_scr = refs[11 + 2 * ns]
        i = pl.program_id(0)
        if ns:
            start, relay, finish = _gather_phases(refs[9:9 + ns], refs[11 + ns:11 + 2 * ns], *refs[12 + 2 * ns:])
            pl.when(i == 0)(start)
            pl.when(i == nc - 1)(relay)

        @pl.when(i == 0)
        def _():
            h_scr[...] = jnp.zeros_like(h_scr)

        h = h_scr[...]
        st_ref[:, 0] = h
        y, h_new = _ssd_chunk(_split_lanes(x_ref[...], SSD_H, SSD_P), _split_lanes(z_ref[...], SSD_H, SSD_P),
                              _split_lanes(b_ref[...], SSD_G, SSD_N), _split_lanes(c_ref[...], SSD_G, SSD_N),
                              _pick_cols(sm_ref[...], SM_DT, SSD_H), h, db_ref[...], al_ref[...], ds_ref[...],
                              nw_ref[...], _valid_col(i * BLK, BLK))
        y_ref[...] = _join_lanes(y)
        h_scr[...] = h_new
        if ns:
            pl.when(i == nc - 1)(finish)

    return pl.pallas_call(
        body, name="ssd_fwd", grid=(nc,),
        in_specs=[wide, bmat, cmat, wide, col, scal, scal, scal, nw] + [_ANY] * ns,
        out_specs=[wide, st] + [_ANY] * ns,
        out_shape=[jax.ShapeDtypeStruct((seq, SSD_INNER), F32),
                   jax.ShapeDtypeStruct((SSD_H, nc, SSD_P, SSD_N), F32)] + _gather_out_shapes(shards),
        scratch_shapes=[pltpu.VMEM((SSD_H, SSD_P, SSD_N), F32)] + (_gather_sems(ns) if ns else []),
        compiler_params=_cp(("arbitrary",)),
    )(xbc, xbc, xbc, z, small, dt_bias, a_log, d_skip, norm_w, *shards)


def _ssd_bwd_call(xbc, z, small, dt_bias, a_log, d_skip, norm_w, states, dy):
    seq = z.shape[0]
    nc = seq // BLK
    wide, bmat, cmat, xbc_spec, col, scal, nw, st = _ssd_specs(nc, True)

    def body(x_ref, b_ref, c_ref, z_ref, sm_ref, db_ref, al_ref, ds_ref, nw_ref, st_ref, dy_ref,
             dxbc_ref, dz_ref, dsm_ref, ddb_ref, dal_ref, dds_ref, dnw_ref, dh_scr):
        i = pl.program_id(0)

        @pl.when(i == 0)
        def _():
            dh_scr[...] = jnp.zeros_like(dh_scr)
            ddb_ref[...] = jnp.zeros_like(ddb_ref)
            dal_ref[...] = jnp.zeros_like(dal_ref)
            dds_ref[...] = jnp.zeros_like(dds_ref)
            dnw_ref[...] = jnp.zeros_like(dnw_ref)

        fn = functools.partial(_ssd_chunk, valid=_valid_col((nc - 1 - i) * BLK, BLK))
        _, vjp = jax.vjp(fn, _split_lanes(x_ref[...], SSD_H, SSD_P), _split_lanes(z_ref[...], SSD_H, SSD_P),
                         _split_lanes(b_ref[...], SSD_G, SSD_N), _split_lanes(c_ref[...], SSD_G, SSD_N),
                         _pick_cols(sm_ref[...], SM_DT, SSD_H), st_ref[:, 0], db_ref[...], al_ref[...], ds_ref[...],
                         nw_ref[...])
        dx, dz, dbm, dcm, ddt, dh, ddb, dal, dds, dnw = vjp((_split_lanes(dy_ref[...], SSD_H, SSD_P), dh_scr[...]))
        dxbc_ref[:, :SSD_INNER] = _join_lanes(dx)
        dxbc_ref[:, SSD_INNER:SSD_INNER + SSD_BC] = _join_lanes(dbm)
        dxbc_ref[:, SSD_INNER + SSD_BC:] = _join_lanes(dcm)
        dz_ref[...] = _join_lanes(dz)
        dsm_ref[...] = _widen(_spread_cols(ddt, SM_DT), dsm_ref.shape[1])
        dh_scr[...] = dh
        ddb_ref[...] += ddb
        dal_ref[...] += dal
        dds_ref[...] += dds
        dnw_ref[...] += dnw

    sshape = jax.ShapeDtypeStruct((SSD_H, 1, 1), F32)
    return pl.pallas_call(
        body, name="ssd_bwd", grid=(nc,),
        in_specs=[wide, bmat, cmat, wide, col, scal, scal, scal, nw, st, wide],
        out_specs=[xbc_spec, wide, pl.BlockSpec((BLK, small.shape[1]), lambda i: (nc - 1 - i, 0)), scal, scal, scal, nw],
        out_shape=[jax.ShapeDtypeStruct(xbc.shape, F32), jax.ShapeDtypeStruct(z.shape, F32),
                   jax.ShapeDtypeStruct(small.shape, F32), sshape, sshape, sshape,
                   jax.ShapeDtypeStruct((SSD_H, 1, SSD_P), F32)],
        scratch_shapes=[pltpu.VMEM((SSD_H, SSD_P, SSD_N), F32)],
        compiler_params=_cp(("arbitrary",)),
    )(xbc, xbc, xbc, z, small, dt_bias, a_log, d_skip, norm_w, states, dy)


@jax.custom_vjp
def ssd_core(xbc, z, small, dt_bias, a_log, d_skip, norm_w, shards):
    y, _, *gathered = _ssd_fwd_call(xbc, z, small, dt_bias, a_log, d_skip, norm_w, shards)
    return y, tuple(gathered)


def _ssd_core_fwd(*args):
    y, states, *gathered = _ssd_fwd_call(*args)
    return (y, tuple(gathered)), (*args[:-1], states, args[-1])


def _ssd_core_bwd(res, cts):
    *args, shards = res
    return (*_ssd_bwd_call(*args, cts[0]), tuple(jnp.zeros_like(s) for s in shards))


ssd_core.defvjp(_ssd_core_fwd, _ssd_core_bwd)


def _swa_block(q, km, kp, kc, vm, vp, vc, sink, n):
    rows = SWA_REP * BLK
    qs = q.reshape(rows, SWA_D) * (SWA_D ** -0.5)
    s = _lo_nt(qs, jnp.concatenate([km, kp, kc], axis=0))
    i = jnp.bitwise_and(_iota2(rows, 3 * BLK, 0), BLK - 1)
    col = _iota2(rows, 3 * BLK, 1)
    j = jnp.bitwise_and(col, BLK - 1)
    part = jnp.right_shift(col, 7)
    ok_m = (part == 0) & (j >= NPAD) & ((n >= 1) | (j <= i))
    ok_p = (part == 1) & (n >= 2) & (j > i)
    ok_c = (part == 2) & (n >= 1) & (j <= i)
    ok = ok_m | ok_p | ok_c
    s = jnp.where(ok, s, NEG)
    snk = jnp.concatenate([jnp.broadcast_to(sink[r], (BLK, 1)) for r in range(SWA_REP)], axis=0)
    m = lax.stop_gradient(jnp.maximum(jnp.max(s, axis=-1, keepdims=True), snk))
    e = jnp.exp(s - m)
    p = e / (jnp.sum(e, axis=-1, keepdims=True) + jnp.exp(snk - m))
    o = _lo_nn(p, jnp.concatenate([vm, vp, vc], axis=0))
    return o.reshape(SWA_REP, BLK, SWA_D)


SWA_QW = SWA_QH * SWA_D
SWA_KW = SWA_KVH * SWA_D


def _swa_specs(nb, rev):
    ci = (lambda i: nb - 1 - i) if rev else (lambda i: i)
    qsp = pl.BlockSpec((BLK, SWA_QW), lambda i: (ci(i), 0))
    cur = pl.BlockSpec((BLK, 2 * SWA_KW), lambda i: (ci(i), 0))
    prev = pl.BlockSpec((BLK, 2 * SWA_KW), lambda i: (jnp.maximum(ci(i) - 1, 0), 0))
    meta = pl.BlockSpec((BLK, 2 * SWA_KW), lambda i: (0, 0))
    scal = pl.BlockSpec((SWA_QH, 1, 1), lambda i: (0, 0, 0))
    return qsp, cur, prev, meta, scal


def _swa_by_head(q, kvm, kvp, kvc, sink):
    def kv(t):
        return _split_lanes(t[:, :SWA_KW], SWA_KVH, SWA_D), _split_lanes(t[:, SWA_KW:], SWA_KVH, SWA_D)

    (km, vm), (kp, vp), (kc, vc) = kv(kvm), kv(kvp), kv(kvc)
    qh = _split_lanes(q, SWA_QH, SWA_D).reshape(SWA_KVH, SWA_REP, BLK, SWA_D)
    return qh, km, kp, kc, vm, vp, vc, sink.reshape(SWA_KVH, SWA_REP, 1, 1)


def _swa_kv_tile(dk, dv):
    return jnp.concatenate([_join_lanes(dk), _join_lanes(dv)], axis=1)


def _swa_fwd_call(q, kv, sink):
    seq = q.shape[0]
    nb = seq // BLK
    qsp, cur, prev, meta, scal = _swa_specs(nb, False)

    def body(q_ref, m_ref, p_ref, c_ref, s_ref, o_ref):
        fn = jax.vmap(functools.partial(_swa_block, n=pl.program_id(0)))
        o = fn(*_swa_by_head(q_ref[...], m_ref[...], p_ref[...], c_ref[...], s_ref[...]))
        o_ref[...] = _join_lanes(o.reshape(SWA_QH, BLK, SWA_D))

    return pl.pallas_call(
        body, name="swa_fwd", grid=(nb,),
        in_specs=[qsp, meta, prev, cur, scal],
        out_specs=qsp,
        out_shape=jax.ShapeDtypeStruct(q.shape, F32),
        compiler_params=_cp(("parallel",)),
    )(q, kv, kv, kv, sink)


def _swa_bwd_call(q, kv, sink, do):
    seq = q.shape[0]
    nb = seq // BLK
    qsp, cur, prev, meta, scal = _swa_specs(nb, True)

    def body(q_ref, m_ref, p_ref, c_ref, s_ref, do_ref, dq_ref, dkv_ref, ds_ref, prev_scr, meta_scr):
        i = pl.program_id(0)
        n = nb - 1 - i

        @pl.when(i == 0)
        def _():
            prev_scr[...] = jnp.zeros_like(prev_scr)
            meta_scr[...] = jnp.zeros_like(meta_scr)
            ds_ref[...] = jnp.zeros_like(ds_ref)

        fn = jax.vmap(functools.partial(_swa_block, n=n))
        _, vjp = jax.vjp(fn, *_swa_by_head(q_ref[...], m_ref[...], p_ref[...], c_ref[...], s_ref[...]))
        do = _split_lanes(do_ref[...], SWA_QH, SWA_D).reshape(SWA_KVH, SWA_REP, BLK, SWA_D)
        dq, dkm, dkp, dkc, dvm, dvp, dvc, dsk = vjp(do)
        dq_ref[...] = _join_lanes(dq.reshape(SWA_QH, BLK, SWA_D))
        ds_ref[...] += dsk.reshape(SWA_QH, 1, 1)
        meta_scr[...] += _swa_kv_tile(dkm, dvm)
        first = (n == 0).astype(F32)
        dkv_ref[...] = _swa_kv_tile(dkc, dvc) + prev_scr[...] + first * meta_scr[...]
        prev_scr[...] = _swa_kv_tile(dkp, dvp)

    return pl.pallas_call(
        body, name="swa_bwd", grid=(nb,),
        in_specs=[qsp, meta, prev, cur, scal, qsp],
        out_specs=[qsp, cur, scal],
        out_shape=[jax.ShapeDtypeStruct(q.shape, F32), jax.ShapeDtypeStruct(kv.shape, F32),
                   jax.ShapeDtypeStruct(sink.shape, F32)],
        scratch_shapes=[pltpu.VMEM((BLK, 2 * SWA_KW), F32)] * 2,
        compiler_params=_cp(("arbitrary",)),
    )(q, kv, kv, kv, sink, do)


@jax.custom_vjp
def swa_core(q, kv, sink):
    return _swa_fwd_call(q, kv, sink)


def _swa_core_fwd(q, kv, sink):
    return _swa_fwd_call(q, kv, sink), (q, kv, sink)


def _swa_core_bwd(res, do):
    return tuple(_swa_bwd_call(*res, do))


swa_core.defvjp(_swa_core_fwd, _swa_core_bwd)


def _tile(n, pref):
    if n <= pref:
        return n
    best = None
    for t in range(128, pref + 1, 128):
        if n % t == 0:
            best = t
    assert best is not None, (n, pref)
    return best


MM_TILE_BYTES = 9 * 1024 * 1024


def _mm_tiles(m, n, kk, a_bytes, b_bytes):
    if kk > 8192:
        return _tile(m, 2816 // a_bytes), _tile(n, 512), _tile(kk, 4096)
    if kk > 1408 and _tile(m, 1024) * kk * a_bytes <= MM_TILE_BYTES:
        return _tile(m, 1024), _tile(n, 1024 if 1024 * kk * b_bytes <= MM_TILE_BYTES else 512), kk
    return _tile(m, 1408), _tile(n, 1024 if kk <= 1408 else 512), _tile(kk, 1408)


def _mm_call(a, b, name):
    (m, kk), n = a.shape, b.shape[1]
    tm, tn, tk = _mm_tiles(m, n, kk, a.dtype.itemsize, b.dtype.itemsize)
    nk = kk // tk
    a_spec = pl.BlockSpec((tm, tk), lambda i, j, k: (i, k))
    b_spec = pl.BlockSpec((tk, tn), lambda i, j, k: (k, j))

    def body(a_ref, b_ref, o_ref, acc_ref):
        k = pl.program_id(2)
        part = jnp.dot(a_ref[...].astype(BF16), b_ref[...].astype(BF16), preferred_element_type=F32)

        @pl.when(k == 0)
        def _():
            acc_ref[...] = part

        @pl.when(k > 0)
        def _():
            acc_ref[...] += part

        @pl.when(k == nk - 1)
        def _():
            o_ref[...] = acc_ref[...]

    return pl.pallas_call(
        body, name=name, grid=(m // tm, n // tn, nk),
        in_specs=[a_spec, b_spec],
        out_specs=pl.BlockSpec((tm, tn), lambda i, j, k: (i, j)),
        out_shape=jax.ShapeDtypeStruct((m, n), F32),
        scratch_shapes=[pltpu.VMEM((tm, tn), F32)],
        compiler_params=_cp(("parallel", "parallel", "arbitrary")),
    )(a, b)


@jax.custom_vjp
def mm(a, b, b_t, grad_slot):
    return _mm_call(a, b, "mm_fwd")


def _mm_fwd(a, b, b_t, grad_slot):
    return _mm_call(a, b, "mm_fwd"), (a, b, b_t)


def _mm_bwd(res, dc):
    a, b, b_t = res
    return (_mm_call(dc, b_t, "mm_dx"), jnp.zeros_like(b), jnp.zeros_like(b_t),
            _mm_call(a.astype(BF16).T, dc, "mm_dw"))


mm.defvjp(_mm_fwd, _mm_bwd)


_SPLIT = (1024, 1024, 1024, 1024, 1024, 2048, 1024, 512, 3072, 512)


def _split_cols(u):
    offs = [sum(_SPLIT[:i]) for i in range(len(_SPLIT))]
    return tuple(u[:, o:o + s] for o, s in zip(offs, _SPLIT))


@jax.custom_vjp
def mm_split(a, b, b_t, grad_slot):
    return _split_cols(_mm_call(a, b, "mm_fwd"))


def _mm_split_fwd(a, b, b_t, grad_slot):
    return _split_cols(_mm_call(a, b, "mm_fwd")), (a, b, b_t)


def _mm_split_bwd(res, cts):
    return _mm_bwd(res, jnp.concatenate([c.astype(BF16) for c in cts], axis=1))


mm_split.defvjp(_mm_split_fwd, _mm_split_bwd)


def _row_specs(arrs, tr):
    return [pl.BlockSpec((tr, a.shape[1]), lambda i: (i, 0)) for a in arrs]


def _par_specs(arrs):
    return [pl.BlockSpec(a.shape, lambda i: (0, 0)) for a in arrs]


def _row_fwd_call(fn, rows, params, out_cols, tr, name):
    seq = rows[0].shape[0]
    nr = len(rows)

    def body(*refs):
        vals = [r[...] for r in refs[:-1]]
        refs[-1][...] = fn(*vals)

    return pl.pallas_call(
        body, name=name, grid=(seq // tr,),
        in_specs=_row_specs(rows, tr) + _par_specs(params),
        out_specs=pl.BlockSpec((tr, out_cols), lambda i: (i, 0)),
        out_shape=jax.ShapeDtypeStruct((seq, out_cols), F32),
        compiler_params=_cp(("parallel",)),
    )(*rows, *params)


def _row_bwd_call(fn, rows, params, dy, tr, name):
    seq = rows[0].shape[0]
    nr, npar = len(rows), len(params)

    def body(*refs):
        ins = refs[:nr + npar]
        dy_ref = refs[nr + npar]
        outs = refs[nr + npar + 1:]
        _, vjp = jax.vjp(fn, *[r[...] for r in ins])
        cts = vjp(dy_ref[...])
        for o_ref, ct in zip(outs[:nr], cts[:nr]):
            o_ref[...] = ct

        @pl.when(pl.program_id(0) == 0)
        def _():
            for o_ref in outs[nr:]:
                o_ref[...] = jnp.zeros_like(o_ref)

        for o_ref, ct in zip(outs[nr:], cts[nr:]):
            o_ref[...] += ct

    return pl.pallas_call(
        body, name=name, grid=(seq // tr,),
        in_specs=_row_specs(rows, tr) + _par_specs(params) + _row_specs([dy], tr),
        out_specs=_row_specs(rows, tr) + _par_specs(params),
        out_shape=[jax.ShapeDtypeStruct(a.shape, F32) for a in (*rows, *params)],
        compiler_params=_cp(("arbitrary",)),
    )(*rows, *params, dy)


def _make_rowop(fn, nrows, out_cols, tr, name):
    @jax.custom_vjp
    def op(*args):
        return _row_fwd_call(fn, args[:nrows], args[nrows:], out_cols, tr, name + "_fwd")

    def fwd(*args):
        return op(*args), args

    def bwd(args, dy):
        return tuple(_row_bwd_call(fn, args[:nrows], args[nrows:], dy, tr, name + "_bwd"))

    op.defvjp(fwd, bwd)
    return op


def _rms_fn(x, w):
    return x * lax.rsqrt(jnp.mean(x * x, axis=-1, keepdims=True) + RMS_EPS) * w


def _merge_fn(pa, pb, pc, gl):
    d = D_MODEL
    return (jax.nn.sigmoid(gl[:, :d]) * pa + jax.nn.sigmoid(gl[:, d:2 * d]) * pb
            + jax.nn.sigmoid(gl[:, 2 * d:]) * pc)


def _relu2_fn(a):
    r = jnp.maximum(a, 0.0)
    return r * r


rms_op = _make_rowop(_rms_fn, 1, D_MODEL, 384, "rms")
merge_op = _make_rowop(_merge_fn, 4, D_MODEL, 192, "merge")
relu2_op = _make_rowop(_relu2_fn, 1, D_FF, 192, "relu2")


def _conv_taps(xext, w, nrows):
    z = None
    for j in range(CONV_K):
        sh = CONV_K - 1 - j
        xs = pltpu.roll(xext, sh, 0) if sh else xext
        term = w[j:j + 1, :] * xs[8:8 + nrows, :]
        z = term if z is None else z + term
    return z


def _halo(ref, start, ok):
    return jnp.where(ok, ref[pl.ds(pl.multiple_of(start, 8), 8), :], 0.0)


def _conv_fwd_call(x, w, b):
    seq, ch = x.shape
    nb = seq // BLK

    def body(x_ref, w_ref, b_ref, o_ref):
        w = w_ref[...]
        bias = b_ref[...]

        def step(i, carry):
            r0 = pl.multiple_of(i * BLK, BLK)
            xext = jnp.concatenate([_halo(x_ref, jnp.maximum(r0 - 8, 0), i > 0), x_ref[pl.ds(r0, BLK), :]], axis=0)
            o_ref[pl.ds(r0, BLK), :] = _silu(_conv_taps(xext, w, BLK) + bias)
            return carry

        lax.fori_loop(0, nb, step, 0)

    strip = pl.BlockSpec((seq, 128), lambda c: (0, c))
    return pl.pallas_call(
        body, name="conv_fwd", grid=(ch // 128,),
        in_specs=[strip, pl.BlockSpec((CONV_K, 128), lambda c: (0, c)), pl.BlockSpec((1, 128), lambda c: (0, c))],
        out_specs=strip, out_shape=jax.ShapeDtypeStruct(x.shape, F32),
        compiler_params=_cp(("parallel",)),
    )(x, w, b)


def _conv_bwd_call(x, w, b, dy):
    seq, ch = x.shape
    nb = seq // BLK

    def body(x_ref, w_ref, b_ref, dy_ref, dx_ref, dw_ref, db_ref):
        w = w_ref[...]
        bias = b_ref[...]

        def step(i, carry):
            r0 = pl.multiple_of(i * BLK, BLK)
            last = i == nb - 1
            nxt = jnp.minimum(r0 + BLK, seq - 8)
            xext = jnp.concatenate([_halo(x_ref, jnp.maximum(r0 - 8, 0), i > 0), x_ref[pl.ds(r0, BLK), :],
                                    _halo(x_ref, nxt, jnp.logical_not(last))], axis=0)
            dyext = jnp.concatenate([dy_ref[pl.ds(r0, BLK), :], _halo(dy_ref, nxt, jnp.logical_not(last))], axis=0)
            z = _conv_taps(xext, w, BLK + 8) + bias
            sg = jax.nn.sigmoid(z)
            dz = dyext * (sg * (1.0 + z * (1.0 - sg)))
            dx = None
            for j in range(CONV_K):
                sh = CONV_K - 1 - j
                dzs = pltpu.roll(dz, BLK + 8 - sh, 0) if sh else dz
                term = w[j:j + 1, :] * dzs[:BLK, :]
                dx = term if dx is None else dx + term
            dx_ref[pl.ds(r0, BLK), :] = dx
            dzm = dz[:BLK, :]
            out = []
            for j in range(CONV_K):
                sh = CONV_K - 1 - j
                xs = pltpu.roll(xext, sh, 0) if sh else xext
                out.append(carry[j] + jnp.sum(dzm * xs[8:8 + BLK, :], axis=0, keepdims=True))
            out.append(carry[CONV_K] + jnp.sum(dzm, axis=0, keepdims=True))
            return tuple(out)

        zero = jnp.zeros((1, 128), F32)
        acc = lax.fori_loop(0, nb, step, (zero,) * (CONV_K + 1))
        dw_ref[...] = jnp.concatenate(acc[:CONV_K], axis=0)
        db_ref[...] = acc[CONV_K]

    strip = pl.BlockSpec((seq, 128), lambda c: (0, c))
    wsp = pl.BlockSpec((CONV_K, 128), lambda c: (0, c))
    bsp = pl.BlockSpec((1, 128), lambda c: (0, c))
    return pl.pallas_call(
        body, name="conv_bwd", grid=(ch // 128,),
        in_specs=[strip, wsp, bsp, strip],
        out_specs=[strip, wsp, bsp],
        out_shape=[jax.ShapeDtypeStruct(x.shape, F32), jax.ShapeDtypeStruct(w.shape, F32),
                   jax.ShapeDtypeStruct(b.shape, F32)],
        compiler_params=_cp(("parallel",)),
    )(x, w, b, dy)


@jax.custom_vjp
def conv_silu(x, w, b):
    return _conv_fwd_call(x, w, b)


def _conv_silu_fwd(x, w, b):
    return _conv_fwd_call(x, w, b), (x, w, b)


def _conv_silu_bwd(res, dy):
    return tuple(_conv_bwd_call(*res, dy))


conv_silu.defvjp(_conv_silu_fwd, _conv_silu_bwd)


def _loss_call(h, wf, target):
    seq, d = h.shape
    nb = seq // BLK

    def body(h_ref, w_ref, t_ref, loss_ref, dh_ref, dw_ref):
        i = pl.program_id(0)
        live = (i > 0).astype(F32)
        tgt = t_ref[...]

        def fn(hh, ww):
            err = _rms_fn(hh, ww) - tgt
            return 0.5 * live * jnp.sum(jnp.mean(err * err, axis=-1, keepdims=True), axis=0, keepdims=True)

        val, vjp = jax.vjp(fn, h_ref[...], w_ref[...])
        dh, dw = vjp(jnp.ones((1, 1), F32))
        dh_ref[...] = dh

        @pl.when(i == 0)
        def _():
            loss_ref[...] = jnp.zeros_like(loss_ref)
            dw_ref[...] = jnp.zeros_like(dw_ref)

        loss_ref[...] += val
        dw_ref[...] += dw

    return pl.pallas_call(
        body, name="loss_head", grid=(nb,),
        in_specs=[pl.BlockSpec((BLK, d), lambda i: (i, 0)), pl.BlockSpec((1, d), lambda i: (0, 0)),
                  pl.BlockSpec((BLK, d), lambda i: (jnp.maximum(i - 1, 0), 0))],
        out_specs=[pl.BlockSpec((1, 1), lambda i: (0, 0)), pl.BlockSpec((BLK, d), lambda i: (i, 0)),
                   pl.BlockSpec((1, d), lambda i: (0, 0))],
        out_shape=[jax.ShapeDtypeStruct((1, 1), F32), jax.ShapeDtypeStruct(h.shape, F32),
                   jax.ShapeDtypeStruct((1, d), F32)],
        compiler_params=_cp(("arbitrary",)),
    )(h, wf, target)


def _make_loss_head(target):
    @jax.custom_vjp
    def head(h, wf):
        return _loss_call(h, wf, target)[0][0, 0]

    def fwd(h, wf):
        loss, dh, dw = _loss_call(h, wf, target)
        return loss[0, 0], (dh, dw)

    def bwd(res, g):
        return g * res[0], g * res[1]

    head.defvjp(fwd, bwd)
    return head


_IN_SEGS = (("q", 0, 1024), ("k", 1024, 1024), ("v", 2048, 1024), ("gate", 3072, 1024), ("z", 4112, 1024),
            ("xbc", 5136, 2048), ("cq", 7200, 1024), ("ck", 8224, 256), ("cv", 8480, 256), ("gl", 8736, 3072),
            ("b", 4096, 8), ("a", 4104, 8), ("dt", 7184, 16))
_IN_PAD = sum(_SPLIT) - sum(n for _, _, n in _IN_SEGS)


_MATMUL = ("w_in", "w_proj_gdn", "w_proj_ssd", "w_proj_swa", "w_out", "w_up", "w_down")
_LATE = _MATMUL[1:]


def _late_weights(gathered):
    g = dict(zip(_LATE, gathered))
    full = {n: g[n].reshape(D_MODEL, D_MODEL) for n in _LATE[:4]}
    full["w_up"] = g["w_up"].transpose(1, 0, 2).reshape(D_MODEL, D_FF)
    full["w_down"] = g["w_down"].reshape(D_FF, D_MODEL)
    full.update({n + "_t": t.T for n, t in list(full.items())})
    return full


def _layer(h, p, w_in, w_in_t, slot, late_shards, next_shards=(), exchange_slots=()):
    wb = {"w_in": w_in, "w_in_t": w_in_t}

    def proj(t, name):
        return mm(t, wb[name], wb[name + "_t"], slot[name])

    q_pre, k_pre, v_pre, gate, z, xbc_pre, cq, ckv, gl, small = mm_split(
        rms_op(h, p["norm1_w"].reshape(1, -1)), w_in, w_in_t, slot["w_in"])

    gcw = p["gdn_conv_w"]
    nob = jnp.zeros((1, GDN_H * GDN_D), F32)
    qa = conv_silu(q_pre, gcw[:, :1024], nob)
    ka = conv_silu(k_pre, gcw[:, 1024:2048], nob)
    va = conv_silu(v_pre, gcw[:, 2048:], nob)
    y_gdn, late, placeholders = gdn_core(
        qa, ka, va, gate, small, p["gdn_a_log"].reshape(GDN_H, 1, 1), p["gdn_dt_bias"].reshape(GDN_H, 1, 1),
        p["gdn_norm_w"].reshape(1, GDN_D), tuple(late_shards), tuple(exchange_slots))
    wb.update(_late_weights(late))

    xbc = conv_silu(xbc_pre, p["ssd_conv_w"], p["ssd_conv_b"].reshape(1, -1))
    y_ssd, gathered = ssd_core(xbc, z, small, p["ssd_dt_bias"].reshape(SSD_H, 1, 1),
                               p["ssd_a_log"].reshape(SSD_H, 1, 1), p["ssd_d"].reshape(SSD_H, 1, 1),
                               p["ssd_norm_w"].reshape(SSD_H, 1, SSD_P), tuple(next_shards))

    y_swa = swa_core(cq, ckv, p["swa_sinks"].reshape(SWA_QH, 1, 1))

    merged = merge_op(proj(y_gdn, "w_proj_gdn"), proj(y_ssd, "w_proj_ssd"), proj(y_swa, "w_proj_swa"), gl)
    h = h + proj(merged, "w_out")
    a1 = proj(rms_op(h, p["norm2_w"].reshape(1, -1)), "w_up")
    return h + proj(relu2_op(a1), "w_down"), gathered, placeholders


_PER_LAYER = ("norm1_w", "gdn_conv_w", "gdn_a_log", "gdn_dt_bias", "gdn_norm_w", "ssd_conv_w", "ssd_conv_b",
              "ssd_dt_bias", "ssd_a_log", "ssd_d", "ssd_norm_w", "swa_sinks", "norm2_w")


def _embed(x, meta):
    return jnp.concatenate([jnp.zeros((NPAD, D_MODEL), F32), meta, x], axis=0)


_IN_SHARD = 1476


def _in_pieces():
    out = []
    for _, s, n in _IN_SEGS:
        c = s
        while c < s + n:
            d = c // _IN_SHARD
            e = min(s + n, (d + 1) * _IN_SHARD)
            out.append((d, c - d * _IN_SHARD, e - d * _IN_SHARD))
            c = e
    return out


def _in_pieces_back():
    start, off = {}, 0
    for _, s, n in _IN_SEGS:
        start[s] = off
        off += n
    out = [[] for _ in range(N_DEV)]
    for _, s, n in sorted(_IN_SEGS, key=lambda t: t[1]):
        c = s
        while c < s + n:
            d = c // _IN_SHARD
            e = min(s + n, (d + 1) * _IN_SHARD)
            out[d].append((start[s] + c - s, start[s] + e - s))
            c = e
    return out


def _regroup_w_in(stacked):
    parts = [stacked[d, :, lo:hi] for d, lo, hi in _in_pieces()]
    return jnp.concatenate(parts + [jnp.zeros((D_MODEL, _IN_PAD), stacked.dtype)], axis=1)


def _ungroup_w_in(g):
    return [jnp.concatenate([g[:, lo:hi] for lo, hi in pieces], axis=1) for pieces in _in_pieces_back()]


def _position():
    return lax.axis_index("x"), lax.axis_index("y"), lax.axis_index("c")


_ANY = pl.BlockSpec(memory_space=pl.ANY)


def _chip_of(x, y, k):
    return (1 - x if k & 1 else x, 1 - y if k & 2 else y)


def _allgather_call(shards, name):
    n = len(shards)

    def body(*refs):
        start, relay, finish = _gather_phases(refs[:n], refs[n:2 * n], *refs[2 * n:])
        start()
        relay()
        finish()

    return pl.pallas_call(
        body, name=name,
        out_shape=_gather_out_shapes(shards),
        in_specs=[_ANY] * n, out_specs=[_ANY] * n,
        scratch_shapes=_gather_sems(n),
    )(*shards)


def _gather_out_shapes(shards):
    return [jax.ShapeDtypeStruct((N_DEV, *s.shape), s.dtype) for s in shards]


def _gather_sems(n):
    return [pltpu.SemaphoreType.DMA((7 * n,)), pltpu.SemaphoreType.DMA((7 * n,)), pltpu.SemaphoreType.DMA((n,))]


def _gather_phases(x_refs, out_refs, send_sems, recv_sems, local_sems):
    n = len(x_refs)
    x, y, c = _position()
    me, sibling = (x, y, c), (x, y, 1 - c)
    chips = [_chip_of(x, y, k) for k in (1, 2, 3)]

    def slab(a, px, py, pc):
        return out_refs[a].at[4 * px + 2 * py + pc]

    def copy(a, k, block, to, src=None):
        return pltpu.make_async_remote_copy(
            src_ref=slab(a, *block) if src is None else src, dst_ref=slab(a, *block),
            send_sem=send_sems.at[7 * a + k], recv_sem=recv_sems.at[7 * a + k], device_id=to, device_id_type=MESH)

    def mine():
        return [pltpu.make_async_copy(x_refs[a], slab(a, *me), local_sems.at[a]) for a in range(n)]

    def first():
        out = []
        for a in range(n):
            out.append(copy(a, 0, me, sibling, src=x_refs[a]))
            out += [copy(a, 1 + j, me, (*chip, c), src=x_refs[a]) for j, chip in enumerate(chips)]
        return out

    def passed():
        return [copy(a, 4 + j, (*chip, c), sibling) for j, chip in enumerate(chips) for a in range(n)]

    def start():
        for cp in mine() + first():
            cp.start()

    def relay():
        for j, chip in enumerate(chips):
            for a in range(n):
                copy(a, 1 + j, (*chip, c), me).wait_recv()
                copy(a, 4 + j, (*chip, c), sibling).start()

    def finish():
        for a in range(n):
            copy(a, 0, sibling, me).wait_recv()
        for j, chip in enumerate(chips):
            for a in range(n):
                copy(a, 4 + j, (*chip, 1 - c), me).wait_recv()
        for cp in first() + passed():
            cp.wait_send()
        for cp in mine():
            cp.wait()

    return start, relay, finish


def _sibling_exchange_call(for_c0, for_c1, name):
    n = len(for_c0)

    def body(*refs):
        c0_refs, c1_refs, out_refs = refs[:n], refs[n:2 * n], refs[2 * n:3 * n]
        send_sems, recv_sems = refs[3 * n:]
        x, y, c = _position()

        def copies(src_refs):
            return [pltpu.make_async_remote_copy(
                src_ref=src_refs[a].at[q], dst_ref=out_refs[a].at[q],
                send_sem=send_sems.at[4 * a + q], recv_sem=recv_sems.at[4 * a + q],
                device_id=(x, y, 1 - c), device_id_type=MESH) for a in range(n) for q in range(4)]

        @pl.when(c == 0)
        def _():
            for cp in copies(c1_refs):
                cp.start()

        @pl.when(c == 1)
        def _():
            for cp in copies(c0_refs):
                cp.start()

        waits = copies(c0_refs)
        for cp in waits:
            cp.wait_recv()
        for cp in waits:
            cp.wait_send()

    return pl.pallas_call(
        body, name=name,
        out_shape=[jax.ShapeDtypeStruct(g.shape, g.dtype) for g in for_c0],
        in_specs=[_ANY] * (2 * n), out_specs=[_ANY] * n,
        scratch_shapes=[pltpu.SemaphoreType.DMA((4 * n,)), pltpu.SemaphoreType.DMA((4 * n,))],
    )(*for_c0, *for_c1)


def _chip_exchange_call(partials, name):
    n = len(partials)

    def body(*refs):
        start, finish = _chip_exchange_phases(refs[:n], refs[n:2 * n], *refs[2 * n:])
        start()
        finish()

    return pl.pallas_call(
        body, name=name,
        out_shape=_chip_exchange_out_shapes(partials),
        in_specs=[_ANY] * n, out_specs=[_ANY] * n,
        scratch_shapes=_chip_exchange_sems(n),
    )(*partials)


def _chip_exchange_out_shapes(partials):
    return [jax.ShapeDtypeStruct((3, *p.shape[1:]), p.dtype) for p in partials]


def _chip_exchange_sems(n):
    return [pltpu.SemaphoreType.DMA((3 * n,)), pltpu.SemaphoreType.DMA((3 * n,))]


def _chip_exchange_phases(p_refs, out_refs, send_sems, recv_sems):
    n = len(p_refs)
    x, y, c = _position()

    def copies():
        out = []
        for a in range(n):
            for k in (1, 2, 3):
                px, py = _chip_of(x, y, k)
                out.append(pltpu.make_async_remote_copy(
                    src_ref=p_refs[a].at[2 * px + py], dst_ref=out_refs[a].at[k - 1],
                    send_sem=send_sems.at[3 * a + k - 1], recv_sem=recv_sems.at[3 * a + k - 1],
                    device_id=(px, py, c), device_id_type=MESH))
        return out

    def start():
        for cp in copies():
            cp.start()

    def finish():
        for cp in copies():
            cp.wait_recv()
        for cp in copies():
            cp.wait_send()

    return start, finish


def _chip_partial_call(for_c0, for_c1, sib, tr, name):
    _, r, c = sib.shape

    def body(c0_ref, c1_ref, s_ref, own_ref, out_ref):
        x, y, core = _position()
        mine = jnp.where(core == 0, c0_ref[...], c1_ref[...])
        partial = mine + s_ref[...]
        own = jnp.zeros((tr, c), F32)
        for q in range(4):
            own = jnp.where(2 * x + y == q, partial[q], own)
        own_ref[...] = own
        out_ref[...] = partial.astype(BF16)

    four = pl.BlockSpec((4, tr, c), lambda i: (0, i, 0))
    return pl.pallas_call(
        body, name=name, grid=(r // tr,),
        in_specs=[four, four, four],
        out_specs=[pl.BlockSpec((tr, c), lambda i: (i, 0)), four],
        out_shape=[jax.ShapeDtypeStruct((r, c), F32), jax.ShapeDtypeStruct((4, r, c), BF16)],
        compiler_params=_cp(("parallel",)),
    )(for_c0, for_c1, sib)


def _adamw_call(parts, w, m, v, tr, name):
    ns, r, c = w.shape
    counts = [len(p) for p in parts]
    flat_parts = [a for p in parts for a in p]

    def body(*refs):
        p_refs = refs[:len(flat_parts)]
        w_ref, m_ref, v_ref, g_ref, d_ref, nm_ref, nv_ref = refs[len(flat_parts):]
        at = 0
        for s in range(ns):
            g = None
            for p_ref in p_refs[at:at + counts[s]]:
                for j in range(p_ref.shape[0]):
                    term = p_ref[j].astype(F32)
                    g = term if g is None else g + term
            at += counts[s]
            nm = ADAM_B1 * m_ref[s] + (1.0 - ADAM_B1) * g
            nv = ADAM_B2 * v_ref[s] + (1.0 - ADAM_B2) * (g * g)
            m_hat = nm / (1.0 - ADAM_B1 ** ADAM_STEP)
            v_hat = nv / (1.0 - ADAM_B2 ** ADAM_STEP)
            g_ref[s] = g
            d_ref[s] = -ADAM_LR * (m_hat / (jnp.sqrt(v_hat) + ADAM_EPS) + ADAM_WD * w_ref[s])
            nm_ref[s] = nm
            nv_ref[s] = nv

    slabs = pl.BlockSpec((ns, tr, c), lambda i: (0, i, 0))
    return pl.pallas_call(
        body, name=name, grid=(r // tr,),
        in_specs=[pl.BlockSpec((a.shape[0], tr, c), lambda i: (0, i, 0)) for a in flat_parts] + [slabs] * 3,
        out_specs=[slabs] * 4,
        out_shape=[jax.ShapeDtypeStruct((ns, r, c), F32)] * 4,
        compiler_params=_cp(("parallel",)),
    )(*flat_parts, w, m, v)


_WEIGHTS = ("meta_tokens", "norm1_w", "w_in", "gdn_conv_w", "gdn_a_log", "gdn_dt_bias", "gdn_norm_w", "ssd_conv_w",
            "ssd_conv_b", "ssd_dt_bias", "ssd_a_log", "ssd_d", "ssd_norm_w", "swa_sinks", "w_proj_gdn", "w_proj_ssd",
            "w_proj_swa", "w_out", "norm2_w", "w_up", "w_down", "final_norm_w")
_SHARD_AXIS = {"meta_tokens": 1, "w_in": 2, "gdn_conv_w": 2, "ssd_conv_w": 2, "w_proj_gdn": 1, "w_proj_ssd": 1,
               "w_proj_swa": 1, "w_out": 1, "w_up": 2, "w_down": 1}
_BIG = tuple(n for n in _WEIGHTS if n in _SHARD_AXIS)
_SMALL = tuple(n for n in _WEIGHTS if n not in _SHARD_AXIS)
FLAT_C = 1024


def _pack(arrs, rows, lead=()):
    flat = jnp.concatenate([a.reshape(*lead, -1) for a in arrs], axis=-1)
    pad = rows * FLAT_C - flat.shape[-1]
    flat = jnp.pad(flat, [(0, 0)] * len(lead) + [(0, pad)])
    return flat.reshape(*lead, rows, FLAT_C)


def _unpack(flat, shapes, lead=()):
    flat = flat.reshape(*lead, -1)
    out, off = [], 0
    for s in shapes:
        n = math.prod(s)
        out.append(flat[..., off:off + n].reshape(*lead, *s))
        off += n
    return out


def _rows_for(shapes):
    n = sum(math.prod(s) for s in shapes)
    return -(-n // (FLAT_C * 8)) * 8


def _rows_tile(r, c):
    if r <= 256:
        return r
    return 128 if c > 1024 else 256


def _join(stacked, axis):
    moved = jnp.moveaxis(stacked, 0, axis)
    return moved.reshape(*moved.shape[:axis], -1, *moved.shape[axis + 2:])


def _unjoin(full, axis):
    cut = full.reshape(*full.shape[:axis], N_DEV, full.shape[axis] // N_DEV, *full.shape[axis + 1:])
    return jnp.moveaxis(cut, axis, 0)


def kernel(x, meta_tokens, norm1_w, w_in, gdn_conv_w, gdn_a_log, gdn_dt_bias, gdn_norm_w, ssd_conv_w, ssd_conv_b,
           ssd_dt_bias, ssd_a_log, ssd_d, ssd_norm_w, swa_sinks, w_proj_gdn, w_proj_ssd, w_proj_swa, w_out, norm2_w,
           w_up, w_down, final_norm_w, loss_target, m_meta_tokens, m_norm1_w, m_w_in, m_gdn_conv_w, m_gdn_a_log,
           m_gdn_dt_bias, m_gdn_norm_w, m_ssd_conv_w, m_ssd_conv_b, m_ssd_dt_bias, m_ssd_a_log, m_ssd_d, m_ssd_norm_w,
           m_swa_sinks, m_w_proj_gdn, m_w_proj_ssd, m_w_proj_swa, m_w_out, m_norm2_w, m_w_up, m_w_down,
           m_final_norm_w, v_meta_tokens, v_norm1_w, v_w_in, v_gdn_conv_w, v_gdn_a_log, v_gdn_dt_bias, v_gdn_norm_w,
           v_ssd_conv_w, v_ssd_conv_b, v_ssd_dt_bias, v_ssd_a_log, v_ssd_d, v_ssd_norm_w, v_swa_sinks, v_w_proj_gdn,
           v_w_proj_ssd, v_w_proj_swa, v_w_out, v_norm2_w, v_w_up, v_w_down, v_final_norm_w):
    args = (meta_tokens, norm1_w, w_in, gdn_conv_w, gdn_a_log, gdn_dt_bias, gdn_norm_w, ssd_conv_w, ssd_conv_b,
            ssd_dt_bias, ssd_a_log, ssd_d, ssd_norm_w, swa_sinks, w_proj_gdn, w_proj_ssd, w_proj_swa, w_out, norm2_w,
            w_up, w_down, final_norm_w, m_meta_tokens, m_norm1_w, m_w_in, m_gdn_conv_w, m_gdn_a_log,
            m_gdn_dt_bias, m_gdn_norm_w, m_ssd_conv_w, m_ssd_conv_b, m_ssd_dt_bias, m_ssd_a_log, m_ssd_d, m_ssd_norm_w,
            m_swa_sinks, m_w_proj_gdn, m_w_proj_ssd, m_w_proj_swa, m_w_out, m_norm2_w, m_w_up, m_w_down,
            m_final_norm_w, v_meta_tokens, v_norm1_w, v_w_in, v_gdn_conv_w, v_gdn_a_log, v_gdn_dt_bias, v_gdn_norm_w,
            v_ssd_conv_w, v_ssd_conv_b, v_ssd_dt_bias, v_ssd_a_log, v_ssd_d, v_ssd_norm_w, v_swa_sinks, v_w_proj_gdn,
            v_w_proj_ssd, v_w_proj_swa, v_w_out, v_norm2_w, v_w_up, v_w_down, v_final_norm_w)
    nw = len(_WEIGHTS)
    w = dict(zip(_WEIGHTS, args[:nw]))
    m = dict(zip(_WEIGHTS, args[nw:2 * nw]))
    v = dict(zip(_WEIGHTS, args[2 * nw:]))

    depth = w["w_in"].shape[0]
    small_shapes = [w[n].shape for n in _SMALL]
    small_rows = _rows_for(small_shapes)

    def flat2(t):
        return t.reshape(-1, t.shape[-1])

    tiny_names = [n for n in _BIG if n not in _MATMUL]

    def shard(n, l):
        return w[n][l].astype(BF16)

    first = _allgather_call([shard("w_in", 0)] + [flat2(w[n]) for n in tiny_names], "gather_weights")
    w_in_stacked = first[0]
    joined = {n: _join(t.reshape(N_DEV, *w[n].shape), _SHARD_AXIS[n]) for n, t in zip(tiny_names, first[1:])}
    slot_shapes = {"w_in": (D_MODEL, sum(_SPLIT)), "w_up": (D_MODEL, D_FF), "w_down": (D_FF, D_MODEL)}
    slot_shapes.update({n: (D_MODEL, D_MODEL) for n in _LATE[:4]})

    def layer_fn(l, w_in_full, late_shards, next_shards):
        w_in_t = w_in_full.T
        if l == 0:
            def fn(x_rows, meta, p, slot, exchange_slots):
                out, g, placeholders = _layer(_embed(x_rows, meta), p, w_in_full, w_in_t, slot, late_shards,
                                              next_shards, exchange_slots)
                return (out, placeholders), g
        else:
            def fn(h_in, p, slot, exchange_slots):
                out, g, placeholders = _layer(h_in, p, w_in_full, w_in_t, slot, late_shards, next_shards,
                                              exchange_slots)
                return (out, placeholders), g
        return fn

    h, vjps = None, []
    for l in range(depth):
        slot = {n: jnp.zeros(s, F32) for n, s in slot_shapes.items()}
        p = {n: (joined[n][l] if n in joined else w[n][l]) for n in _PER_LAYER}
        more = l + 1 < depth
        next_shards = [shard("w_in", l + 1)] if more else []
        exchange_slots = tuple(jnp.zeros((3, *w[n][l + 1].shape), BF16) for n in _MATMUL) if more else ()
        lead = (x[0], joined["meta_tokens"]) if l == 0 else (h,)
        fn = layer_fn(l, _regroup_w_in(w_in_stacked), [shard(n, l) for n in _LATE], next_shards)
        (h, _), vjp, g_next = jax.vjp(fn, *lead, p, slot, exchange_slots, has_aux=True)
        if more:
            w_in_stacked = g_next[0]
        vjps.append(vjp)
    loss, head_vjp = jax.vjp(_make_loss_head(loss_target[0]), h, w["final_norm_w"].reshape(1, -1))
    dh, d_final = head_vjp(jnp.ones((), F32))
    loss = lax.psum(loss, ("x", "y", "c"))

    def by_core(name, g):
        if name == "w_in":
            shards = _ungroup_w_in(g)
            return jnp.stack(shards[0::2]), jnp.stack(shards[1::2])
        if name == "w_up":
            t = g.reshape(D_MODEL, 4, 2, D_FF // N_DEV)
            return t[:, :, 0].transpose(1, 0, 2), t[:, :, 1].transpose(1, 0, 2)
        t = g.reshape(4, 2, -1, g.shape[-1])
        return t[:, 0], t[:, 1]

    own, incoming, layer_grads, outgoing = {}, {}, [None] * depth, ()
    for l in reversed(range(depth)):
        if l == 0:
            gx, d_meta, dp, dslot, arrived = vjps[0]((dh, tuple(outgoing)))
        else:
            dh, dp, dslot, arrived = vjps[l]((dh, tuple(outgoing)))
        incoming.update({(n, l + 1): t for n, t in zip(_MATMUL, arrived)})
        layer_grads[l] = dp
        todo = [((n, l), dslot[n]) for n in _MATMUL]
        if l == 0:
            full_grads = {"meta_tokens": d_meta}
            full_grads.update({n: jnp.stack([layer_grads[k][n] for k in range(depth)]) for n in tiny_names[1:]})
            todo += [((n, None), _unjoin(full_grads[n], _SHARD_AXIS[n]).reshape(N_DEV, -1, w[n].shape[-1]))
                     for n in tiny_names]
        pairs = [by_core(u[0], g) for u, g in todo]
        from_sibling = _sibling_exchange_call([a for a, _ in pairs], [b for _, b in pairs], "grads_to_sibling_%d" % l)
        outgoing = []
        for (u, _), (a0, a1), s in zip(todo, pairs, from_sibling):
            own[u], part = _chip_partial_call(a0, a1, s, _rows_tile(s.shape[1], s.shape[2]), "chip_partial_" + u[0])
            outgoing.append(part)
        if l == 0:
            incoming.update(zip([u for u, _ in todo], _chip_exchange_call(outgoing, "grads_to_chips")))

    g_small = {n: jnp.stack([layer_grads[k][n] for k in range(depth)]) for n in _SMALL if n != "final_norm_w"}
    g_small["final_norm_w"] = d_final.reshape(-1)

    by_name = {}
    for n in _BIG:
        layers = list(range(depth)) if n in _MATMUL else [None]
        parts = [[own[n, l][None], incoming[n, l]] for l in layers]
        r, c = own[n, layers[0]].shape
        stacked = [d[n].reshape(len(layers), r, c) for d in (w, m, v)]
        res = _adamw_call(parts, *stacked, _rows_tile(r, c), "adamw_" + n)
        by_name[n] = [t.reshape(w[n].shape) for t in res]

    small_parts = _allgather_call([_pack([g_small[n] for n in _SMALL], small_rows)], "gather_small_grads")
    small_out = _adamw_call([small_parts], *[_pack([d[n] for n in _SMALL], small_rows)[None] for d in (w, m, v)],
                            small_rows, "adamw_replicated")
    for kind in range(4):
        for n, t in zip(_SMALL, _unpack(small_out[kind][0], small_shapes)):
            by_name.setdefault(n, [None] * 4)[kind] = t

    outs = [by_name[n][kind] for kind in range(4) for n in _WEIGHTS]
    return (loss, gx[None], *outs)
```

```python
import functools
import math

import jax
import jax.numpy as jnp
from jax import lax
from jax.experimental import pallas as pl
from jax.experimental.pallas import tpu as pltpu

F32 = jnp.float32
BF16 = jnp.bfloat16
HI = lax.Precision.HIGH
NEG = -1e30

D_MODEL = 1024
N_META = 16
BLK = 128
NPAD = BLK - N_META
RMS_EPS = 1e-6
L2_EPS = 1e-6
CONV_K = 4

GDN_H, GDN_D, GDN_C = 8, 128, 64
SSD_H, SSD_P, SSD_G, SSD_N = 16, 64, 4, 128
SSD_HPG = SSD_H // SSD_G
SWA_QH, SWA_KVH, SWA_D = 16, 4, 64
SWA_REP = SWA_QH // SWA_KVH
D_FF = 4 * D_MODEL

N_DEV = 8
MESH = pl.DeviceIdType.MESH

ADAM_LR, ADAM_B1, ADAM_B2, ADAM_EPS, ADAM_WD, ADAM_STEP = 0.001, 0.9, 0.999, 1e-08, 0.01, 10

VMEM_LIMIT = 56 * 1024 * 1024


def _cp(sem=None):
    return pltpu.CompilerParams(dimension_semantics=sem, vmem_limit_bytes=VMEM_LIMIT)


def _dot(a, b, ca, cb, prec=HI):
    return lax.dot_general(a, b, (((ca,), (cb,)), ((), ())), precision=prec, preferred_element_type=F32)


def _nn(a, b, prec=HI):
    return _dot(a, b, 1, 0, prec)


def _nt(a, b, prec=HI):
    return _dot(a, b, 1, 1, prec)


def _tn(a, b, prec=HI):
    return _dot(a, b, 0, 0, prec)


def _bdot(a, b, ca, cb):
    return lax.dot_general(a.astype(BF16), b.astype(BF16), (((ca,), (cb,)), ((), ())), preferred_element_type=F32)


@jax.custom_vjp
def _lo_nn(a, b):
    return _bdot(a, b, 1, 0)


_lo_nn.defvjp(lambda a, b: (_bdot(a, b, 1, 0), (a, b)),
              lambda r, d: (_bdot(d, r[1], 1, 1), _bdot(r[0], d, 0, 0)))


@jax.custom_vjp
def _lo_nt(a, b):
    return _bdot(a, b, 1, 1)


_lo_nt.defvjp(lambda a, b: (_bdot(a, b, 1, 1), (a, b)),
              lambda r, d: (_bdot(d, r[1], 1, 0), _bdot(d, r[0], 0, 0)))


@jax.custom_vjp
def _lo_tn(a, b):
    return _bdot(a, b, 0, 0)


_lo_tn.defvjp(lambda a, b: (_bdot(a, b, 0, 0), (a, b)),
              lambda r, d: (_bdot(r[1], d, 1, 1), _bdot(r[0], d, 1, 0)))


def _iota2(n, m, axis):
    return lax.broadcasted_iota(jnp.int32, (n, m), axis)


def _silu(x):
    return x * jax.nn.sigmoid(x)


def _softplus(x):
    return jnp.maximum(x, 0.0) + jnp.log(1.0 + jnp.exp(-jnp.abs(x)))


def _row_of(col):
    n = col.shape[0]
    return jnp.broadcast_to(col, (n, n)).T


def _cumsum_col(col):
    n = col.shape[0]
    tril = (_iota2(n, n, 0) >= _iota2(n, n, 1)).astype(F32)
    return _nn(tril, col)


def _tri_inv(a):
    n = a.shape[0]
    r, c = _iota2(n, n, 0), _iota2(n, n, 1)
    eye = (r == c).astype(F32)
    blk = jnp.right_shift(r, 4) == jnp.right_shift(c, 4)
    d = jnp.where(blk, a, 0.0)
    off = a - d
    d2 = _nn(d, d)
    d4 = _nn(d2, d2)
    d8 = _nn(d4, d4)
    td = _nn(_nn(_nn(eye - d, eye + d2), eye + d4), eye + d8)
    m = _nn(td, off)
    m2 = _nn(m, m)
    return _nn(_nn(eye - m, eye + m2), td)


@jax.custom_vjp
def _tri_solve(a, inv, rhs):
    return _nn(inv, rhs)


def _tri_solve_fwd(a, inv, rhs):
    sol = _nn(inv, rhs)
    return sol, (inv, sol)


def _tri_solve_bwd(res, dsol):
    inv, sol = res
    drhs = _nn(inv.T, dsol)
    return -_nt(drhs, sol), jnp.zeros_like(inv), drhs


_tri_solve.defvjp(_tri_solve_fwd, _tri_solve_bwd)


def _gdn_chunk(qa, ka, va, gate, a_raw, b_raw, s, a_log, dt_bias, norm_w, valid, inv=None, want_inv=False):
    c = qa.shape[0]
    q = qa * lax.rsqrt(jnp.sum(qa * qa, axis=-1, keepdims=True) + L2_EPS) * (GDN_D ** -0.5)
    k = ka * lax.rsqrt(jnp.sum(ka * ka, axis=-1, keepdims=True) + L2_EPS)
    beta = jax.nn.sigmoid(b_raw)
    g = -jnp.exp(a_log) * _softplus(a_raw + dt_bias) * valid
    gam = _cumsum_col(g)
    gam_row = _row_of(gam)
    r, cc = _iota2(c, c, 0), _iota2(c, c, 1)
    decay = jnp.exp(jnp.where(r >= cc, gam - gam_row, NEG))
    kb = k * beta
    a = jnp.where(r > cc, _lo_nt(kb, k) * decay, 0.0)
    egam = jnp.exp(gam)
    if inv is None:
        inv = _tri_inv(lax.stop_gradient(a))
    sol = _tri_solve(a, inv, jnp.concatenate([va * beta, kb * egam], axis=1))
    u = sol[:, :GDN_D]
    w = sol[:, GDN_D:]
    attn = _lo_nt(q, k) * decay
    g_last = jnp.sum(g, axis=0, keepdims=True)
    k_tail = k * jnp.exp(g_last - gam)
    v_new = u - _lo_nn(w, s)
    o = _lo_nn(q * egam, s) + _lo_nn(attn, v_new)
    s_new = s * jnp.exp(g_last) + _lo_tn(k_tail, v_new)
    y = o * lax.rsqrt(jnp.mean(o * o, axis=-1, keepdims=True) + RMS_EPS) * norm_w * _silu(gate)
    return (y, s_new, inv) if want_inv else (y, s_new)


def _valid_col(row0, n):
    return (row0 + _iota2(n, 1, 0) >= NPAD).astype(F32)


GDN_HB = GDN_H

SM_B, SM_A, SM_DT, SM_W = 0, 8, 16, 128


def _pick_cols(sm, first, n):
    return jnp.stack([sm[:, first + j:first + j + 1] for j in range(n)])


def _spread_cols(cols, first):
    lane = _iota2(1, SM_W, 1)
    out = None
    for j in range(cols.shape[0]):
        term = cols[j] * (lane == first + j).astype(F32)
        out = term if out is None else out + term
    return out


def _widen(t, width):
    if width == t.shape[1]:
        return t
    return jnp.concatenate([t, jnp.zeros((t.shape[0], width - t.shape[1]), t.dtype)], axis=1)


def _gdn_specs(nc, rev):
    ci = (lambda i: nc - 1 - i) if rev else (lambda i: i)
    hb = GDN_HB
    tile = pl.BlockSpec((GDN_C, hb * GDN_D), lambda h, i: (ci(i), h))
    col = pl.BlockSpec((GDN_C, SM_W), lambda h, i: (ci(i), 0))
    scal = pl.BlockSpec((hb, 1, 1), lambda h, i: (h, 0, 0))
    nw = pl.BlockSpec((1, GDN_D), lambda h, i: (0, 0))
    st = pl.BlockSpec((hb, 1, GDN_D, GDN_D), lambda h, i: (h, ci(i), 0, 0))
    return tile, col, scal, nw, st


def _lanes(j):
    return slice(j * GDN_D, (j + 1) * GDN_D)


def _by_head(ref):
    return jnp.stack([ref[:, _lanes(j)] for j in range(GDN_HB)])


def _gdn_fwd_call(q, k, v, gate, small, a_log, dt_bias, norm_w, shards=()):
    seq = q.shape[0]
    nc = seq // GDN_C
    ns = len(shards)
    tile, col, scal, nw, st = _gdn_specs(nc, False)

    def body(*refs):
        q_ref, k_ref, v_ref, g_ref, sm_ref, al_ref, dt_ref, nw_ref = refs[:8]
        y_ref, st_ref, inv_ref = refs[8 + ns:11 + ns]
        s_scr = refs[11 + 2 * ns]
        i = pl.program_id(1)
        if ns:
            start, relay, finish = _gather_phases(refs[8:8 + ns], refs[11 + ns:11 + 2 * ns], *refs[12 + 2 * ns:])
            pl.when(i == 0)(start)
            pl.when(i == nc - 1)(relay)

        @pl.when(i == 0)
        def _():
            s_scr[...] = jnp.zeros_like(s_scr)

        s = s_scr[...]
        st_ref[:, 0] = s
        sm = sm_ref[...]
        fn = jax.vmap(functools.partial(_gdn_chunk, valid=_valid_col(i * GDN_C, GDN_C), want_inv=True))
        y, s_new, inv = fn(_by_head(q_ref), _by_head(k_ref), _by_head(v_ref), _by_head(g_ref),
                           _pick_cols(sm, SM_A, GDN_H), _pick_cols(sm, SM_B, GDN_H), s,
                           al_ref[...], dt_ref[...], jnp.broadcast_to(nw_ref[...], (GDN_HB, 1, GDN_D)))
        for j in range(GDN_HB):
            y_ref[:, _lanes(j)] = y[j]
        inv_ref[:, 0] = inv
        s_scr[...] = s_new
        if ns:
            pl.when(i == nc - 1)(finish)

    return pl.pallas_call(
        body, name="gdn_fwd", grid=(GDN_H // GDN_HB, nc),
        in_specs=[tile, tile, tile, tile, col, scal, scal, nw] + [_ANY] * ns,
        out_specs=[tile, st, pl.BlockSpec((GDN_HB, 1, GDN_C, GDN_C), lambda h, i: (h, i, 0, 0))] + [_ANY] * ns,
        out_shape=[jax.ShapeDtypeStruct((seq, GDN_H * GDN_D), F32),
                   jax.ShapeDtypeStruct((GDN_H, nc, GDN_D, GDN_D), F32),
                   jax.ShapeDtypeStruct((GDN_H, nc, GDN_C, GDN_C), F32)] + _gather_out_shapes(shards),
        scratch_shapes=[pltpu.VMEM((GDN_HB, GDN_D, GDN_D), F32)] + (_gather_sems(ns) if ns else []),
        compiler_params=_cp(("parallel", "arbitrary")),
    )(q, k, v, gate, small, a_log, dt_bias, norm_w, *shards)


def _gdn_bwd_call(q, k, v, gate, small, a_log, dt_bias, norm_w, states, invs, dy, outgoing=()):
    seq = q.shape[0]
    nc = seq // GDN_C
    no = len(outgoing)
    tile, col, scal, nw, st = _gdn_specs(nc, True)
    nwh = pl.BlockSpec((GDN_HB, 1, GDN_D), lambda h, i: (h, 0, 0))
    inv_spec = pl.BlockSpec((GDN_HB, 1, GDN_C, GDN_C), lambda h, i: (h, nc - 1 - i, 0, 0))

    def body(*refs):
        q_ref, k_ref, v_ref, g_ref, sm_ref, al_ref, dt_ref, nw_ref, st_ref, inv_ref, dy_ref = refs[:11]
        dq_ref, dk_ref, dv_ref, dg_ref, dsm_ref, dal_ref, ddt_ref, dnw_ref = refs[11 + no:19 + no]
        ds_scr = refs[19 + 2 * no]
        i = pl.program_id(1)
        if no:
            start, finish = _chip_exchange_phases(refs[11:11 + no], refs[19 + no:19 + 2 * no], *refs[20 + 2 * no:])
            pl.when(i == 0)(start)

        @pl.when(i == 0)
        def _():
            ds_scr[...] = jnp.zeros_like(ds_scr)
            dal_ref[...] = jnp.zeros_like(dal_ref)
            ddt_ref[...] = jnp.zeros_like(ddt_ref)
            dnw_ref[...] = jnp.zeros_like(dnw_ref)

        sm = sm_ref[...]
        valid = _valid_col((nc - 1 - i) * GDN_C, GDN_C)
        kept = inv_ref[:, 0]

        def fn(*heads):
            return jax.vmap(lambda *t: _gdn_chunk(*t[:-1], valid=valid, inv=t[-1]))(*heads, kept)

        _, vjp = jax.vjp(fn, _by_head(q_ref), _by_head(k_ref), _by_head(v_ref), _by_head(g_ref),
                         _pick_cols(sm, SM_A, GDN_H), _pick_cols(sm, SM_B, GDN_H), st_ref[:, 0], al_ref[...],
                         dt_ref[...], jnp.broadcast_to(nw_ref[...], (GDN_HB, 1, GDN_D)))
        dq, dk, dv, dg, da, db, ds, dal, ddt, dnw = vjp((_by_head(dy_ref), ds_scr[...]))
        for j in range(GDN_HB):
            dq_ref[:, _lanes(j)] = dq[j]
            dk_ref[:, _lanes(j)] = dk[j]
            dv_ref[:, _lanes(j)] = dv[j]
            dg_ref[:, _lanes(j)] = dg[j]
        dsm_ref[...] = _widen(_spread_cols(da, SM_A) + _spread_cols(db, SM_B), dsm_ref.shape[1])
        ds_scr[...] = ds
        dal_ref[...] += dal
        ddt_ref[...] += ddt
        dnw_ref[...] += dnw
        if no:
            pl.when(i == nc - 1)(finish)

    big = jax.ShapeDtypeStruct((seq, GDN_H * GDN_D), F32)
    return pl.pallas_call(
        body, name="gdn_bwd", grid=(GDN_H // GDN_HB, nc),
        in_specs=[tile, tile, tile, tile, col, scal, scal, nw, st, inv_spec, tile] + [_ANY] * no,
        out_specs=[tile, tile, tile, tile, pl.BlockSpec((GDN_C, small.shape[1]), lambda h, i: (nc - 1 - i, 0)),
                   scal, scal, nwh] + [_ANY] * no,
        out_shape=[big, big, big, big, jax.ShapeDtypeStruct(small.shape, F32),
                   jax.ShapeDtypeStruct((GDN_H, 1, 1), F32), jax.ShapeDtypeStruct((GDN_H, 1, 1), F32),
                   jax.ShapeDtypeStruct((GDN_H, 1, GDN_D), F32)] + _chip_exchange_out_shapes(outgoing),
        scratch_shapes=[pltpu.VMEM((GDN_HB, GDN_D, GDN_D), F32)] + (_chip_exchange_sems(no) if no else []),
        compiler_params=_cp(("parallel", "arbitrary")),
    )(q, k, v, gate, small, a_log, dt_bias, norm_w, states, invs, dy, *outgoing)


@jax.custom_vjp
def gdn_core(q, k, v, gate, small, a_log, dt_bias, norm_w, shards, slots):
    y, _, _, *gathered = _gdn_fwd_call(q, k, v, gate, small, a_log, dt_bias, norm_w, shards)
    return y, tuple(gathered), tuple(jnp.zeros((4, *s.shape[1:]), s.dtype) for s in slots)


def _gdn_core_fwd(q, k, v, gate, small, a_log, dt_bias, norm_w, shards, slots):
    y, states, invs, *gathered = _gdn_fwd_call(q, k, v, gate, small, a_log, dt_bias, norm_w, shards)
    out = (y, tuple(gathered), tuple(jnp.zeros((4, *s.shape[1:]), s.dtype) for s in slots))
    return out, (q, k, v, gate, small, a_log, dt_bias, norm_w, states, invs, shards)


def _gdn_core_bwd(res, cts):
    *args, shards = res
    dy, _, outgoing = cts
    dq, dk, dv, dg, dsm, dal, ddt, dnw, *incoming = _gdn_bwd_call(*args, dy, outgoing)
    return (dq, dk, dv, dg, dsm, dal, ddt, jnp.sum(dnw, axis=0), tuple(jnp.zeros_like(s) for s in shards),
            tuple(incoming))


gdn_core.defvjp(_gdn_core_fwd, _gdn_core_bwd)


def _ssd_head(x, z, dt_raw, h, dt_bias, a_log, d_skip, bm, cm, cb, valid):
    c = bm.shape[0]
    r, cc = _iota2(c, c, 0), _iota2(c, c, 1)
    dtp = _softplus(dt_raw + dt_bias)
    x = x * valid
    adt = -jnp.exp(a_log) * dtp * valid
    xdt = x * dtp
    acum = _cumsum_col(adt)
    lmat = jnp.exp(jnp.where(r >= cc, acum - _row_of(acum), NEG))
    a_last = jnp.sum(adt, axis=0, keepdims=True)
    y = _lo_nn(cb * lmat, xdt) + _lo_nt(cm * jnp.exp(acum), h) + d_skip * x
    h_new = h * jnp.exp(a_last) + _lo_tn(xdt * jnp.exp(a_last - acum), bm)
    return y * _silu(z), h_new


SSD_SIDE = SSD_H


def _ssd_chunk(xs, z, bm, cm, dt_raw, h, dt_bias, a_log, d_skip, norm_w, valid):
    nh, c, p = xs.shape
    ng = bm.shape[0]
    hpg = nh // ng
    bm = bm * valid
    cm = cm * valid
    cb = jax.vmap(_lo_nt)(cm, bm)
    per_head = lambda t: jnp.repeat(t, hpg, axis=0)
    args = (xs, z, dt_raw, h, dt_bias, a_log, d_skip, per_head(bm), per_head(cm), per_head(cb))
    outs = [jax.vmap(functools.partial(_ssd_head, valid=valid))(*[t[s:s + SSD_SIDE] for t in args])
            for s in range(0, nh, SSD_SIDE)]
    ys = jnp.concatenate([o[0] for o in outs], axis=0)
    hs = jnp.concatenate([o[1] for o in outs], axis=0)
    ss = jnp.sum(jnp.sum(ys * ys, axis=-1, keepdims=True).reshape(ng, hpg, c, 1), axis=1, keepdims=True)
    rstd = lax.rsqrt(ss / (hpg * p) + RMS_EPS)
    return (ys.reshape(ng, hpg, c, p) * rstd).reshape(nh, c, p) * norm_w, hs


SSD_INNER = SSD_H * SSD_P
SSD_BC = SSD_G * SSD_N


def _split_lanes(t, n, w):
    return jnp.stack([t[:, j * w:(j + 1) * w] for j in range(n)])


def _join_lanes(t):
    return jnp.concatenate([t[j] for j in range(t.shape[0])], axis=1)


def _ssd_specs(nc, rev):
    ci = (lambda i: nc - 1 - i) if rev else (lambda i: i)
    wide = pl.BlockSpec((BLK, SSD_INNER), lambda i: (ci(i), 0))
    bmat = pl.BlockSpec((BLK, SSD_BC), lambda i: (ci(i), SSD_INNER // SSD_BC))
    cmat = pl.BlockSpec((BLK, SSD_BC), lambda i: (ci(i), SSD_INNER // SSD_BC + 1))
    xbc = pl.BlockSpec((BLK, SSD_INNER + 2 * SSD_BC), lambda i: (ci(i), 0))
    col = pl.BlockSpec((BLK, SM_W), lambda i: (ci(i), 0))
    scal = pl.BlockSpec((SSD_H, 1, 1), lambda i: (0, 0, 0))
    nw = pl.BlockSpec((SSD_H, 1, SSD_P), lambda i: (0, 0, 0))
    st = pl.BlockSpec((SSD_H, 1, SSD_P, SSD_N), lambda i: (0, ci(i), 0, 0))
    return wide, bmat, cmat, xbc, col, scal, nw, st


def _ssd_fwd_call(xbc, z, small, dt_bias, a_log, d_skip, norm_w, shards=()):
    seq = z.shape[0]
    nc = seq // BLK
    ns = len(shards)
    wide, bmat, cmat, _, col, scal, nw, st = _ssd_specs(nc, False)

    def body(*refs):
        x_ref, b_ref, c_ref, z_ref, sm_ref, db_ref, al_ref, ds_ref, nw_ref = refs[:9]
        y_ref, st_ref = refs[9 + ns:11 + ns]
        h_scr = refs[11 + 2 * ns]
        i = pl.program_id(0)
        if ns:
            start, relay, finish = _gather_phases(refs[9:9 + ns], refs[11 + ns:11 + 2 * ns], *refs[12 + 2 * ns:])
            pl.when(i == 0)(start)
            pl.when(i == nc - 1)(relay)

        @pl.when(i == 0)
        def _():
            h_scr[...] = jnp.zeros_like(h_scr)

        h = h_scr[...]
        st_ref[:, 0] = h
        y, h_new = _ssd_chunk(_split_lanes(x_ref[...], SSD_H, SSD_P), _split_lanes(z_ref[...], SSD_H, SSD_P),
                              _split_lanes(b_ref[...], SSD_G, SSD_N), _split_lanes(c_ref[...], SSD_G, SSD_N),
                              _pick_cols(sm_ref[...], SM_DT, SSD_H), h, db_ref[...], al_ref[...], ds_ref[...],
                              nw_ref[...], _valid_col(i * BLK, BLK))
        y_ref[...] = _join_lanes(y)
        h_scr[...] = h_new
        if ns:
            pl.when(i == nc - 1)(finish)

    return pl.pallas_call(
        body, name="ssd_fwd", grid=(nc,),
        in_specs=[wide, bmat, cmat, wide, col, scal, scal, scal, nw] + [_ANY] * ns,
        out_specs=[wide, st] + [_ANY] * ns,
        out_shape=[jax.ShapeDtypeStruct((seq, SSD_INNER), F32),
                   jax.ShapeDtypeStruct((SSD_H, nc, SSD_P, SSD_N), F32)] + _gather_out_shapes(shards),
        scratch_shapes=[pltpu.VMEM((SSD_H, SSD_P, SSD_N), F32)] + (_gather_sems(ns) if ns else []),
        compiler_params=_cp(("arbitrary",)),
    )(xbc, xbc, xbc, z, small, dt_bias, a_log, d_skip, norm_w, *shards)


def _ssd_bwd_call(xbc, z, small, dt_bias, a_log, d_skip, norm_w, states, dy):
    seq = z.shape[0]
    nc = seq // BLK
    wide, bmat, cmat, xbc_spec, col, scal, nw, st = _ssd_specs(nc, True)

    def body(x_ref, b_ref, c_ref, z_ref, sm_ref, db_ref, al_ref, ds_ref, nw_ref, st_ref, dy_ref,
             dxbc_ref, dz_ref, dsm_ref, ddb_ref, dal_ref, dds_ref, dnw_ref, dh_scr):
        i = pl.program_id(0)

        @pl.when(i == 0)
        def _():
            dh_scr[...] = jnp.zeros_like(dh_scr)
            ddb_ref[...] = jnp.zeros_like(ddb_ref)
            dal_ref[...] = jnp.zeros_like(dal_ref)
            dds_ref[...] = jnp.zeros_like(dds_ref)
            dnw_ref[...] = jnp.zeros_like(dnw_ref)

        fn = functools.partial(_ssd_chunk, valid=_valid_col((nc - 1 - i) * BLK, BLK))
        _, vjp = jax.vjp(fn, _split_lanes(x_ref[...], SSD_H, SSD_P), _split_lanes(z_ref[...], SSD_H, SSD_P),
                         _split_lanes(b_ref[...], SSD_G, SSD_N), _split_lanes(c_ref[...], SSD_G, SSD_N),
                         _pick_cols(sm_ref[...], SM_DT, SSD_H), st_ref[:, 0], db_ref[...], al_ref[...], ds_ref[...],
                         nw_ref[...])
        dx, dz, dbm, dcm, ddt, dh, ddb, dal, dds, dnw = vjp((_split_lanes(dy_ref[...], SSD_H, SSD_P), dh_scr[...]))
        dxbc_ref[:, :SSD_INNER] = _join_lanes(dx)
        dxbc_ref[:, SSD_INNER:SSD_INNER + SSD_BC] = _join_lanes(dbm)
        dxbc_ref[:, SSD_INNER + SSD_BC:] = _join_lanes(dcm)
        dz_ref[...] = _join_lanes(dz)
        dsm_ref[...] = _widen(_spread_cols(ddt, SM_DT), dsm_ref.shape[1])
        dh_scr[...] = dh
        ddb_ref[...] += ddb
        dal_ref[...] += dal
        dds_ref[...] += dds
        dnw_ref[...] += dnw

    sshape = jax.ShapeDtypeStruct((SSD_H, 1, 1), F32)
    return pl.pallas_call(
        body, name="ssd_bwd", grid=(nc,),
        in_specs=[wide, bmat, cmat, wide, col, scal, scal, scal, nw, st, wide],
        out_specs=[xbc_spec, wide, pl.BlockSpec((BLK, small.shape[1]), lambda i: (nc - 1 - i, 0)), scal, scal, scal, nw],
        out_shape=[jax.ShapeDtypeStruct(xbc.shape, F32), jax.ShapeDtypeStruct(z.shape, F32),
                   jax.ShapeDtypeStruct(small.shape, F32), sshape, sshape, sshape,
                   jax.ShapeDtypeStruct((SSD_H, 1, SSD_P), F32)],
        scratch_shapes=[pltpu.VMEM((SSD_H, SSD_P, SSD_N), F32)],
        compiler_params=_cp(("arbitrary",)),
    )(xbc, xbc, xbc, z, small, dt_bias, a_log, d_skip, norm_w, states, dy)


@jax.custom_vjp
def ssd_core(xbc, z, small, dt_bias, a_log, d_skip, norm_w, shards):
    y, _, *gathered = _ssd_fwd_call(xbc, z, small, dt_bias, a_log, d_skip, norm_w, shards)
    return y, tuple(gathered)


def _ssd_core_fwd(*args):
    y, states, *gathered = _ssd_fwd_call(*args)
    return (y, tuple(gathered)), (*args[:-1], states, args[-1])


def _ssd_core_bwd(res, cts):
    *args, shards = res
    return (*_ssd_bwd_call(*args, cts[0]), tuple(jnp.zeros_like(s) for s in shards))


ssd_core.defvjp(_ssd_core_fwd, _ssd_core_bwd)


def _swa_block(q, km, kp, kc, vm, vp, vc, sink, n):
    rows = SWA_REP * BLK
    qs = q.reshape(rows, SWA_D) * (SWA_D ** -0.5)
    s = _lo_nt(qs, jnp.concatenate([km, kp, kc], axis=0))
    i = jnp.bitwise_and(_iota2(rows, 3 * BLK, 0), BLK - 1)
    col = _iota2(rows, 3 * BLK, 1)
    j = jnp.bitwise_and(col, BLK - 1)
    part = jnp.right_shift(col, 7)
    ok_m = (part == 0) & (j >= NPAD) & ((n >= 1) | (j <= i))
    ok_p = (part == 1) & (n >= 2) & (j > i)
    ok_c = (part == 2) & (n >= 1) & (j <= i)
    ok = ok_m | ok_p | ok_c
    s = jnp.where(ok, s, NEG)
    snk = jnp.concatenate([jnp.broadcast_to(sink[r], (BLK, 1)) for r in range(SWA_REP)], axis=0)
    m = lax.stop_gradient(jnp.maximum(jnp.max(s, axis=-1, keepdims=True), snk))
    e = jnp.exp(s - m)
    p = e / (jnp.sum(e, axis=-1, keepdims=True) + jnp.exp(snk - m))
    o = _lo_nn(p, jnp.concatenate([vm, vp, vc], axis=0))
    return o.reshape(SWA_REP, BLK, SWA_D)


SWA_QW = SWA_QH * SWA_D
SWA_KW = SWA_KVH * SWA_D


def _swa_specs(nb, rev):
    ci = (lambda i: nb - 1 - i) if rev else (lambda i: i)
    qsp = pl.BlockSpec((BLK, SWA_QW), lambda i: (ci(i), 0))
    cur = pl.BlockSpec((BLK, 2 * SWA_KW), lambda i: (ci(i), 0))
    prev = pl.BlockSpec((BLK, 2 * SWA_KW), lambda i: (jnp.maximum(ci(i) - 1, 0), 0))
    meta = pl.BlockSpec((BLK, 2 * SWA_KW), lambda i: (0, 0))
    scal = pl.BlockSpec((SWA_QH, 1, 1), lambda i: (0, 0, 0))
    return qsp, cur, prev, meta, scal


def _swa_by_head(q, kvm, kvp, kvc, sink):
    def kv(t):
        return _split_lanes(t[:, :SWA_KW], SWA_KVH, SWA_D), _split_lanes(t[:, SWA_KW:], SWA_KVH, SWA_D)

    (km, vm), (kp, vp), (kc, vc) = kv(kvm), kv(kvp), kv(kvc)
    qh = _split_lanes(q, SWA_QH, SWA_D).reshape(SWA_KVH, SWA_REP, BLK, SWA_D)
    return qh, km, kp, kc, vm, vp, vc, sink.reshape(SWA_KVH, SWA_REP, 1, 1)


def _swa_kv_tile(dk, dv):
    return jnp.concatenate([_join_lanes(dk), _join_lanes(dv)], axis=1)


def _swa_fwd_call(q, kv, sink):
    seq = q.shape[0]
    nb = seq // BLK
    qsp, cur, prev, meta, scal = _swa_specs(nb, False)

    def body(q_ref, m_ref, p_ref, c_ref, s_ref, o_ref):
        fn = jax.vmap(functools.partial(_swa_block, n=pl.program_id(0)))
        o = fn(*_swa_by_head(q_ref[...], m_ref[...], p_ref[...], c_ref[...], s_ref[...]))
        o_ref[...] = _join_lanes(o.reshape(SWA_QH, BLK, SWA_D))

    return pl.pallas_call(
        body, name="swa_fwd", grid=(nb,),
        in_specs=[qsp, meta, prev, cur, scal],
        out_specs=qsp,
        out_shape=jax.ShapeDtypeStruct(q.shape, F32),
        compiler_params=_cp(("parallel",)),
    )(q, kv, kv, kv, sink)


def _swa_bwd_call(q, kv, sink, do):
    seq = q.shape[0]
    nb = seq // BLK
    qsp, cur, prev, meta, scal = _swa_specs(nb, True)

    def body(q_ref, m_ref, p_ref, c_ref, s_ref, do_ref, dq_ref, dkv_ref, ds_ref, prev_scr, meta_scr):
        i = pl.program_id(0)
        n = nb - 1 - i

        @pl.when(i == 0)
        def _():
            prev_scr[...] = jnp.zeros_like(prev_scr)
            meta_scr[...] = jnp.zeros_like(meta_scr)
            ds_ref[...] = jnp.zeros_like(ds_ref)

        fn = jax.vmap(functools.partial(_swa_block, n=n))
        _, vjp = jax.vjp(fn, *_swa_by_head(q_ref[...], m_ref[...], p_ref[...], c_ref[...], s_ref[...]))
        do = _split_lanes(do_ref[...], SWA_QH, SWA_D).reshape(SWA_KVH, SWA_REP, BLK, SWA_D)
        dq, dkm, dkp, dkc, dvm, dvp, dvc, dsk = vjp(do)
        dq_ref[...] = _join_lanes(dq.reshape(SWA_QH, BLK, SWA_D))
        ds_ref[...] += dsk.reshape(SWA_QH, 1, 1)
        meta_scr[...] += _swa_kv_tile(dkm, dvm)
        first = (n == 0).astype(F32)
        dkv_ref[...] = _swa_kv_tile(dkc, dvc) + prev_scr[...] + first * meta_scr[...]
        prev_scr[...] = _swa_kv_tile(dkp, dvp)

    return pl.pallas_call(
        body, name="swa_bwd", grid=(nb,),
        in_specs=[qsp, meta, prev, cur, scal, qsp],
        out_specs=[qsp, cur, scal],
        out_shape=[jax.ShapeDtypeStruct(q.shape, F32), jax.ShapeDtypeStruct(kv.shape, F32),
                   jax.ShapeDtypeStruct(sink.shape, F32)],
        scratch_shapes=[pltpu.VMEM((BLK, 2 * SWA_KW), F32)] * 2,
        compiler_params=_cp(("arbitrary",)),
    )(q, kv, kv, kv, sink, do)


@jax.custom_vjp
def swa_core(q, kv, sink):
    return _swa_fwd_call(q, kv, sink)


def _swa_core_fwd(q, kv, sink):
    return _swa_fwd_call(q, kv, sink), (q, kv, sink)


def _swa_core_bwd(res, do):
    return tuple(_swa_bwd_call(*res, do))


swa_core.defvjp(_swa_core_fwd, _swa_core_bwd)


def _tile(n, pref):
    if n <= pref:
        return n
    best = None
    for t in range(128, pref + 1, 128):
        if n % t == 0:
            best = t
    assert best is not None, (n, pref)
    return best


MM_TILE_BYTES = 9 * 1024 * 1024


def _mm_tiles(m, n, kk, a_bytes, b_bytes):
    if kk > 8192:
        return _tile(m, 2816 // a_bytes), _tile(n, 512), _tile(kk, 4096)
    if kk > 1408 and _tile(m, 1024) * kk * a_bytes <= MM_TILE_BYTES:
        return _tile(m, 1024), _tile(n, 1024 if 1024 * kk * b_bytes <= MM_TILE_BYTES else 512), kk
    return _tile(m, 1408), _tile(n, 1024 if kk <= 1408 else 512), _tile(kk, 1408)


def _mm_call(a, b, name):
    (m, kk), n = a.shape, b.shape[1]
    tm, tn, tk = _mm_tiles(m, n, kk, a.dtype.itemsize, b.dtype.itemsize)
    nk = kk // tk
    a_spec = pl.BlockSpec((tm, tk), lambda i, j, k: (i, k))
    b_spec = pl.BlockSpec((tk, tn), lambda i, j, k: (k, j))

    def body(a_ref, b_ref, o_ref, acc_ref):
        k = pl.program_id(2)
        part = jnp.dot(a_ref[...].astype(BF16), b_ref[...].astype(BF16), preferred_element_type=F32)

        @pl.when(k == 0)
        def _():
            acc_ref[...] = part

        @pl.when(k > 0)
        def _():
            acc_ref[...] += part

        @pl.when(k == nk - 1)
        def _():
            o_ref[...] = acc_ref[...]

    return pl.pallas_call(
        body, name=name, grid=(m // tm, n // tn, nk),
        in_specs=[a_spec, b_spec],
        out_specs=pl.BlockSpec((tm, tn), lambda i, j, k: (i, j)),
        out_shape=jax.ShapeDtypeStruct((m, n), F32),
        scratch_shapes=[pltpu.VMEM((tm, tn), F32)],
        compiler_params=_cp(("parallel", "parallel", "arbitrary")),
    )(a, b)


@jax.custom_vjp
def mm(a, b, b_t, grad_slot):
    return _mm_call(a, b, "mm_fwd")


def _mm_fwd(a, b, b_t, grad_slot):
    return _mm_call(a, b, "mm_fwd"), (a, b, b_t)


def _mm_bwd(res, dc):
    a, b, b_t = res
    return (_mm_call(dc, b_t, "mm_dx"), jnp.zeros_like(b), jnp.zeros_like(b_t),
            _mm_call(a.astype(BF16).T, dc, "mm_dw"))


mm.defvjp(_mm_fwd, _mm_bwd)


_SPLIT = (1024, 1024, 1024, 1024, 1024, 2048, 1024, 512, 3072, 512)


def _split_cols(u):
    offs = [sum(_SPLIT[:i]) for i in range(len(_SPLIT))]
    return tuple(u[:, o:o + s] for o, s in zip(offs, _SPLIT))


@jax.custom_vjp
def mm_split(a, b, b_t, grad_slot):
    return _split_cols(_mm_call(a, b, "mm_fwd"))


def _mm_split_fwd(a, b, b_t, grad_slot):
    return _split_cols(_mm_call(a, b, "mm_fwd")), (a, b, b_t)


def _mm_split_bwd(res, cts):
    return _mm_bwd(res, jnp.concatenate([c.astype(BF16) for c in cts], axis=1))


mm_split.defvjp(_mm_split_fwd, _mm_split_bwd)


def _row_specs(arrs, tr):
    return [pl.BlockSpec((tr, a.shape[1]), lambda i: (i, 0)) for a in arrs]


def _par_specs(arrs):
    return [pl.BlockSpec(a.shape, lambda i: (0, 0)) for a in arrs]


def _row_fwd_call(fn, rows, params, out_cols, tr, name):
    seq = rows[0].shape[0]
    nr = len(rows)

    def body(*refs):
        vals = [r[...] for r in refs[:-1]]
        refs[-1][...] = fn(*vals)

    return pl.pallas_call(
        body, name=name, grid=(seq // tr,),
        in_specs=_row_specs(rows, tr) + _par_specs(params),
        out_specs=pl.BlockSpec((tr, out_cols), lambda i: (i, 0)),
        out_shape=jax.ShapeDtypeStruct((seq, out_cols), F32),
        compiler_params=_cp(("parallel",)),
    )(*rows, *params)


def _row_bwd_call(fn, rows, params, dy, tr, name):
    seq = rows[0].shape[0]
    nr, npar = len(rows), len(params)

    def body(*refs):
        ins = refs[:nr + npar]
        dy_ref = refs[nr + npar]
        outs = refs[nr + npar + 1:]
        _, vjp = jax.vjp(fn, *[r[...] for r in ins])
        cts = vjp(dy_ref[...])
        for o_ref, ct in zip(outs[:nr], cts[:nr]):
            o_ref[...] = ct

        @pl.when(pl.program_id(0) == 0)
        def _():
            for o_ref in outs[nr:]:
                o_ref[...] = jnp.zeros_like(o_ref)

        for o_ref, ct in zip(outs[nr:], cts[nr:]):
            o_ref[...] += ct

    return pl.pallas_call(
        body, name=name, grid=(seq // tr,),
        in_specs=_row_specs(rows, tr) + _par_specs(params) + _row_specs([dy], tr),
        out_specs=_row_specs(rows, tr) + _par_specs(params),
        out_shape=[jax.ShapeDtypeStruct(a.shape, F32) for a in (*rows, *params)],
        compiler_params=_cp(("arbitrary",)),
    )(*rows, *params, dy)


def _make_rowop(fn, nrows, out_cols, tr, name):
    @jax.custom_vjp
    def op(*args):
        return _row_fwd_call(fn, args[:nrows], args[nrows:], out_cols, tr, name + "_fwd")

    def fwd(*args):
        return op(*args), args

    def bwd(args, dy):
        return tuple(_row_bwd_call(fn, args[:nrows], args[nrows:], dy, tr, name + "_bwd"))

    op.defvjp(fwd, bwd)
    return op


def _rms_fn(x, w):
    return x * lax.rsqrt(jnp.mean(x * x, axis=-1, keepdims=True) + RMS_EPS) * w


def _merge_fn(pa, pb, pc, gl):
    d = D_MODEL
    return (jax.nn.sigmoid(gl[:, :d]) * pa + jax.nn.sigmoid(gl[:, d:2 * d]) * pb
            + jax.nn.sigmoid(gl[:, 2 * d:]) * pc)


def _relu2_fn(a):
    r = jnp.maximum(a, 0.0)
    return r * r


rms_op = _make_rowop(_rms_fn, 1, D_MODEL, 384, "rms")
merge_op = _make_rowop(_merge_fn, 4, D_MODEL, 192, "merge")
relu2_op = _make_rowop(_relu2_fn, 1, D_FF, 192, "relu2")


def _conv_taps(xext, w, nrows):
    z = None
    for j in range(CONV_K):
        sh = CONV_K - 1 - j
        xs = pltpu.roll(xext, sh, 0) if sh else xext
        term = w[j:j + 1, :] * xs[8:8 + nrows, :]
        z = term if z is None else z + term
    return z


def _halo(ref, start, ok):
    return jnp.where(ok, ref[pl.ds(pl.multiple_of(start, 8), 8), :], 0.0)


def _conv_fwd_call(x, w, b):
    seq, ch = x.shape
    nb = seq // BLK

    def body(x_ref, w_ref, b_ref, o_ref):
        w = w_ref[...]
        bias = b_ref[...]

        def step(i, carry):
            r0 = pl.multiple_of(i * BLK, BLK)
            xext = jnp.concatenate([_halo(x_ref, jnp.maximum(r0 - 8, 0), i > 0), x_ref[pl.ds(r0, BLK), :]], axis=0)
            o_ref[pl.ds(r0, BLK), :] = _silu(_conv_taps(xext, w, BLK) + bias)
            return carry

        lax.fori_loop(0, nb, step, 0)

    strip = pl.BlockSpec((seq, 128), lambda c: (0, c))
    return pl.pallas_call(
        body, name="conv_fwd", grid=(ch // 128,),
        in_specs=[strip, pl.BlockSpec((CONV_K, 128), lambda c: (0, c)), pl.BlockSpec((1, 128), lambda c: (0, c))],
        out_specs=strip, out_shape=jax.ShapeDtypeStruct(x.shape, F32),
        compiler_params=_cp(("parallel",)),
    )(x, w, b)


def _conv_bwd_call(x, w, b, dy):
    seq, ch = x.shape
    nb = seq // BLK

    def body(x_ref, w_ref, b_ref, dy_ref, dx_ref, dw_ref, db_ref):
        w = w_ref[...]
        bias = b_ref[...]

        def step(i, carry):
            r0 = pl.multiple_of(i * BLK, BLK)
            last = i == nb - 1
            nxt = jnp.minimum(r0 + BLK, seq - 8)
            xext = jnp.concatenate([_halo(x_ref, jnp.maximum(r0 - 8, 0), i > 0), x_ref[pl.ds(r0, BLK), :],
                                    _halo(x_ref, nxt, jnp.logical_not(last))], axis=0)
            dyext = jnp.concatenate([dy_ref[pl.ds(r0, BLK), :], _halo(dy_ref, nxt, jnp.logical_not(last))], axis=0)
            z = _conv_taps(xext, w, BLK + 8) + bias
            sg = jax.nn.sigmoid(z)
            dz = dyext * (sg * (1.0 + z * (1.0 - sg)))
            dx = None
            for j in range(CONV_K):
                sh = CONV_K - 1 - j
                dzs = pltpu.roll(dz, BLK + 8 - sh, 0) if sh else dz
                term = w[j:j + 1, :] * dzs[:BLK, :]
                dx = term if dx is None else dx + term
            dx_ref[pl.ds(r0, BLK), :] = dx
            dzm = dz[:BLK, :]
            out = []
            for j in range(CONV_K):
                sh = CONV_K - 1 - j
                xs = pltpu.roll(xext, sh, 0) if sh else xext
                out.append(carry[j] + jnp.sum(dzm * xs[8:8 + BLK, :], axis=0, keepdims=True))
            out.append(carry[CONV_K] + jnp.sum(dzm, axis=0, keepdims=True))
            return tuple(out)

        zero = jnp.zeros((1, 128), F32)
        acc = lax.fori_loop(0, nb, step, (zero,) * (CONV_K + 1))
        dw_ref[...] = jnp.concatenate(acc[:CONV_K], axis=0)
        db_ref[...] = acc[CONV_K]

    strip = pl.BlockSpec((seq, 128), lambda c: (0, c))
    wsp = pl.BlockSpec((CONV_K, 128), lambda c: (0, c))
    bsp = pl.BlockSpec((1, 128), lambda c: (0, c))
    return pl.pallas_call(
        body, name="conv_bwd", grid=(ch // 128,),
        in_specs=[strip, wsp, bsp, strip],
        out_specs=[strip, wsp, bsp],
        out_shape=[jax.ShapeDtypeStruct(x.shape, F32), jax.ShapeDtypeStruct(w.shape, F32),
                   jax.ShapeDtypeStruct(b.shape, F32)],
        compiler_params=_cp(("parallel",)),
    )(x, w, b, dy)


@jax.custom_vjp
def conv_silu(x, w, b):
    return _conv_fwd_call(x, w, b)


def _conv_silu_fwd(x, w, b):
    return _conv_fwd_call(x, w, b), (x, w, b)


def _conv_silu_bwd(res, dy):
    return tuple(_conv_bwd_call(*res, dy))


conv_silu.defvjp(_conv_silu_fwd, _conv_silu_bwd)


def _loss_call(h, wf, target):
    seq, d = h.shape
    nb = seq // BLK

    def body(h_ref, w_ref, t_ref, loss_ref, dh_ref, dw_ref):
        i = pl.program_id(0)
        live = (i > 0).astype(F32)
        tgt = t_ref[...]

        def fn(hh, ww):
            err = _rms_fn(hh, ww) - tgt
            return 0.5 * live * jnp.sum(jnp.mean(err * err, axis=-1, keepdims=True), axis=0, keepdims=True)

        val, vjp = jax.vjp(fn, h_ref[...], w_ref[...])
        dh, dw = vjp(jnp.ones((1, 1), F32))
        dh_ref[...] = dh

        @pl.when(i == 0)
        def _():
            loss_ref[...] = jnp.zeros_like(loss_ref)
            dw_ref[...] = jnp.zeros_like(dw_ref)

        loss_ref[...] += val
        dw_ref[...] += dw

    return pl.pallas_call(
        body, name="loss_head", grid=(nb,),
        in_specs=[pl.BlockSpec((BLK, d), lambda i: (i, 0)), pl.BlockSpec((1, d), lambda i: (0, 0)),
                  pl.BlockSpec((BLK, d), lambda i: (jnp.maximum(i - 1, 0), 0))],
        out_specs=[pl.BlockSpec((1, 1), lambda i: (0, 0)), pl.BlockSpec((BLK, d), lambda i: (i, 0)),
                   pl.BlockSpec((1, d), lambda i: (0, 0))],
        out_shape=[jax.ShapeDtypeStruct((1, 1), F32), jax.ShapeDtypeStruct(h.shape, F32),
                   jax.ShapeDtypeStruct((1, d), F32)],
        compiler_params=_cp(("arbitrary",)),
    )(h, wf, target)


def _make_loss_head(target):
    @jax.custom_vjp
    def head(h, wf):
        return _loss_call(h, wf, target)[0][0, 0]

    def fwd(h, wf):
        loss, dh, dw = _loss_call(h, wf, target)
        return loss[0, 0], (dh, dw)

    def bwd(res, g):
        return g * res[0], g * res[1]

    head.defvjp(fwd, bwd)
    return head


_IN_SEGS = (("q", 0, 1024), ("k", 1024, 1024), ("v", 2048, 1024), ("gate", 3072, 1024), ("z", 4112, 1024),
            ("xbc", 5136, 2048), ("cq", 7200, 1024), ("ck", 8224, 256), ("cv", 8480, 256), ("gl", 8736, 3072),
            ("b", 4096, 8), ("a", 4104, 8), ("dt", 7184, 16))
_IN_PAD = sum(_SPLIT) - sum(n for _, _, n in _IN_SEGS)


_MATMUL = ("w_in", "w_proj_gdn", "w_proj_ssd", "w_proj_swa", "w_out", "w_up", "w_down")
_LATE = _MATMUL[1:]


def _late_weights(gathered):
    g = dict(zip(_LATE, gathered))
    full = {n: g[n].reshape(D_MODEL, D_MODEL) for n in _LATE[:4]}
    full["w_up"] = g["w_up"].transpose(1, 0, 2).reshape(D_MODEL, D_FF)
    full["w_down"] = g["w_down"].reshape(D_FF, D_MODEL)
    full.update({n + "_t": t.T for n, t in list(full.items())})
    return full


def _layer(h, p, w_in, w_in_t, slot, late_shards, next_shards=(), exchange_slots=()):
    wb = {"w_in": w_in, "w_in_t": w_in_t}

    def proj(t, name):
        return mm(t, wb[name], wb[name + "_t"], slot[name])

    q_pre, k_pre, v_pre, gate, z, xbc_pre, cq, ckv, gl, small = mm_split(
        rms_op(h, p["norm1_w"].reshape(1, -1)), w_in, w_in_t, slot["w_in"])

    gcw = p["gdn_conv_w"]
    nob = jnp.zeros((1, GDN_H * GDN_D), F32)
    qa = conv_silu(q_pre, gcw[:, :1024], nob)
    ka = conv_silu(k_pre, gcw[:, 1024:2048], nob)
    va = conv_silu(v_pre, gcw[:, 2048:], nob)
    y_gdn, late, placeholders = gdn_core(
        qa, ka, va, gate, small, p["gdn_a_log"].reshape(GDN_H, 1, 1), p["gdn_dt_bias"].reshape(GDN_H, 1, 1),
        p["gdn_norm_w"].reshape(1, GDN_D), tuple(late_shards), tuple(exchange_slots))
    wb.update(_late_weights(late))

    xbc = conv_silu(xbc_pre, p["ssd_conv_w"], p["ssd_conv_b"].reshape(1, -1))
    y_ssd, gathered = ssd_core(xbc, z, small, p["ssd_dt_bias"].reshape(SSD_H, 1, 1),
                               p["ssd_a_log"].reshape(SSD_H, 1, 1), p["ssd_d"].reshape(SSD_H, 1, 1),
                               p["ssd_norm_w"].reshape(SSD_H, 1, SSD_P), tuple(next_shards))

    y_swa = swa_core(cq, ckv, p["swa_sinks"].reshape(SWA_QH, 1, 1))

    merged = merge_op(proj(y_gdn, "w_proj_gdn"), proj(y_ssd, "w_proj_ssd"), proj(y_swa, "w_proj_swa"), gl)
    h = h + proj(merged, "w_out")
    a1 = proj(rms_op(h, p["norm2_w"].reshape(1, -1)), "w_up")
    return h + proj(relu2_op(a1), "w_down"), gathered, placeholders


_PER_LAYER = ("norm1_w", "gdn_conv_w", "gdn_a_log", "gdn_dt_bias", "gdn_norm_w", "ssd_conv_w", "ssd_conv_b",
              "ssd_dt_bias", "ssd_a_log", "ssd_d", "ssd_norm_w", "swa_sinks", "norm2_w")


def _embed(x, meta):
    return jnp.concatenate([jnp.zeros((NPAD, D_MODEL), F32), meta, x], axis=0)


_IN_SHARD = 1476


def _in_pieces():
    out = []
    for _, s, n in _IN_SEGS:
        c = s
        while c < s + n:
            d = c // _IN_SHARD
            e = min(s + n, (d + 1) * _IN_SHARD)
            out.append((d, c - d * _IN_SHARD, e - d * _IN_SHARD))
            c = e
    return out


def _in_pieces_back():
    start, off = {}, 0
    for _, s, n in _IN_SEGS:
        start[s] = off
        off += n
    out = [[] for _ in range(N_DEV)]
    for _, s, n in sorted(_IN_SEGS, key=lambda t: t[1]):
        c = s
        while c < s + n:
            d = c // _IN_SHARD
            e = min(s + n, (d + 1) * _IN_SHARD)
            out[d].append((start[s] + c - s, start[s] + e - s))
            c = e
    return out


def _regroup_w_in(stacked):
    parts = [stacked[d, :, lo:hi] for d, lo, hi in _in_pieces()]
    return jnp.concatenate(parts + [jnp.zeros((D_MODEL, _IN_PAD), stacked.dtype)], axis=1)


def _ungroup_w_in(g):
    return [jnp.concatenate([g[:, lo:hi] for lo, hi in pieces], axis=1) for pieces in _in_pieces_back()]


def _position():
    return lax.axis_index("x"), lax.axis_index("y"), lax.axis_index("c")


_ANY = pl.BlockSpec(memory_space=pl.ANY)


def _chip_of(x, y, k):
    return (1 - x if k & 1 else x, 1 - y if k & 2 else y)


def _allgather_call(shards, name):
    n = len(shards)

    def body(*refs):
        start, relay, finish = _gather_phases(refs[:n], refs[n:2 * n], *refs[2 * n:])
        start()
        relay()
        finish()

    return pl.pallas_call(
        body, name=name,
        out_shape=_gather_out_shapes(shards),
        in_specs=[_ANY] * n, out_specs=[_ANY] * n,
        scratch_shapes=_gather_sems(n),
    )(*shards)


def _gather_out_shapes(shards):
    return [jax.ShapeDtypeStruct((N_DEV, *s.shape), s.dtype) for s in shards]


def _gather_sems(n):
    return [pltpu.SemaphoreType.DMA((7 * n,)), pltpu.SemaphoreType.DMA((7 * n,)), pltpu.SemaphoreType.DMA((n,))]


def _gather_phases(x_refs, out_refs, send_sems, recv_sems, local_sems):
    n = len(x_refs)
    x, y, c = _position()
    me, sibling = (x, y, c), (x, y, 1 - c)
    chips = [_chip_of(x, y, k) for k in (1, 2, 3)]

    def slab(a, px, py, pc):
        return out_refs[a].at[4 * px + 2 * py + pc]

    def copy(a, k, block, to, src=None):
        return pltpu.make_async_remote_copy(
            src_ref=slab(a, *block) if src is None else src, dst_ref=slab(a, *block),
            send_sem=send_sems.at[7 * a + k], recv_sem=recv_sems.at[7 * a + k], device_id=to, device_id_type=MESH)

    def mine():
        return [pltpu.make_async_copy(x_refs[a], slab(a, *me), local_sems.at[a]) for a in range(n)]

    def first():
        out = []
        for a in range(n):
            out.append(copy(a, 0, me, sibling, src=x_refs[a]))
            out += [copy(a, 1 + j, me, (*chip, c), src=x_refs[a]) for j, chip in enumerate(chips)]
        return out

    def passed():
        return [copy(a, 4 + j, (*chip, c), sibling) for j, chip in enumerate(chips) for a in range(n)]

    def start():
        for cp in mine() + first():
            cp.start()

    def relay():
        for j, chip in enumerate(chips):
            for a in range(n):
                copy(a, 1 + j, (*chip, c), me).wait_recv()
                copy(a, 4 + j, (*chip, c), sibling).start()

    def finish():
        for a in range(n):
            copy(a, 0, sibling, me).wait_recv()
        for j, chip in enumerate(chips):
            for a in range(n):
                copy(a, 4 + j, (*chip, 1 - c), me).wait_recv()
        for cp in first() + passed():
            cp.wait_send()
        for cp in mine():
            cp.wait()

    return start, relay, finish


def _sibling_exchange_call(for_c0, for_c1, name):
    n = len(for_c0)

    def body(*refs):
        c0_refs, c1_refs, out_refs = refs[:4 * n], refs[4 * n:8 * n], refs[8 * n:9 * n]
        send_sems, recv_sems = refs[9 * n:]
        x, y, c = _position()

        def copies(src_refs):
            return [pltpu.make_async_remote_copy(
                src_ref=src_refs[4 * a + q], dst_ref=out_refs[a].at[q],
                send_sem=send_sems.at[4 * a + q], recv_sem=recv_sems.at[4 * a + q],
                device_id=(x, y, 1 - c), device_id_type=MESH) for a in range(n) for q in range(4)]

        @pl.when(c == 0)
        def _():
            for cp in copies(c1_refs):
                cp.start()

        @pl.when(c == 1)
        def _():
            for cp in copies(c0_refs):
                cp.start()

        waits = copies(c0_refs)
        for cp in waits:
            cp.wait_recv()
        for cp in waits:
            cp.wait_send()

    return pl.pallas_call(
        body, name=name,
        out_shape=[jax.ShapeDtypeStruct((4, *g[0].shape), g[0].dtype) for g in for_c0],
        in_specs=[_ANY] * (8 * n), out_specs=[_ANY] * n,
        scratch_shapes=[pltpu.SemaphoreType.DMA((4 * n,)), pltpu.SemaphoreType.DMA((4 * n,))],
    )(*[t for g in for_c0 for t in g], *[t for g in for_c1 for t in g])


def _chip_exchange_call(partials, name):
    n = len(partials)

    def body(*refs):
        start, finish = _chip_exchange_phases(refs[:n], refs[n:2 * n], *refs[2 * n:])
        start()
        finish()

    return pl.pallas_call(
        body, name=name,
        out_shape=_chip_exchange_out_shapes(partials),
        in_specs=[_ANY] * n, out_specs=[_ANY] * n,
        scratch_shapes=_chip_exchange_sems(n),
    )(*partials)


def _chip_exchange_out_shapes(partials):
    return [jax.ShapeDtypeStruct((3, *p.shape[1:]), p.dtype) for p in partials]


def _chip_exchange_sems(n):
    return [pltpu.SemaphoreType.DMA((3 * n,)), pltpu.SemaphoreType.DMA((3 * n,))]


def _chip_exchange_phases(p_refs, out_refs, send_sems, recv_sems):
    n = len(p_refs)
    x, y, c = _position()

    def copies():
        out = []
        for a in range(n):
            for k in (1, 2, 3):
                px, py = _chip_of(x, y, k)
                out.append(pltpu.make_async_remote_copy(
                    src_ref=p_refs[a].at[2 * px + py], dst_ref=out_refs[a].at[k - 1],
                    send_sem=send_sems.at[3 * a + k - 1], recv_sem=recv_sems.at[3 * a + k - 1],
                    device_id=(px, py, c), device_id_type=MESH))
        return out

    def start():
        for cp in copies():
            cp.start()

    def finish():
        for cp in copies():
            cp.wait_recv()
        for cp in copies():
            cp.wait_send()

    return start, finish


def _chip_partial_call(for_c0, for_c1, sib, tr, name):
    _, r, c = sib.shape

    def body(*refs):
        c0_refs, c1_refs, (s_ref, own_ref, out_ref) = refs[:4], refs[4:8], refs[8:]
        x, y, core = _position()
        own = jnp.zeros((tr, c), F32)
        for q in range(4):
            partial = jnp.where(core == 0, c0_refs[q][...], c1_refs[q][...]) + s_ref[q]
            own = jnp.where(2 * x + y == q, partial, own)
            out_ref[q] = partial.astype(BF16)
        own_ref[...] = own

    one = pl.BlockSpec((tr, c), lambda i: (i, 0))
    four = pl.BlockSpec((4, tr, c), lambda i: (0, i, 0))
    return pl.pallas_call(
        body, name=name, grid=(r // tr,),
        in_specs=[one] * 8 + [four],
        out_specs=[one, four],
        out_shape=[jax.ShapeDtypeStruct((r, c), F32), jax.ShapeDtypeStruct((4, r, c), BF16)],
        compiler_params=_cp(("parallel",)),
    )(*for_c0, *for_c1, sib)


def _adamw_call(parts, w, m, v, tr, name):
    ns, r, c = w.shape
    counts = [len(p) for p in parts]
    flat_parts = [a for p in parts for a in p]

    def body(*refs):
        p_refs = refs[:len(flat_parts)]
        w_ref, m_ref, v_ref, g_ref, d_ref, nm_ref, nv_ref = refs[len(flat_parts):]
        at = 0
        for s in range(ns):
            g = None
            for p_ref in p_refs[at:at + counts[s]]:
                for j in range(p_ref.shape[0]):
                    term = p_ref[j].astype(F32)
                    g = term if g is None else g + term
            at += counts[s]
            nm = ADAM_B1 * m_ref[s] + (1.0 - ADAM_B1) * g
            nv = ADAM_B2 * v_ref[s] + (1.0 - ADAM_B2) * (g * g)
            m_hat = nm / (1.0 - ADAM_B1 ** ADAM_STEP)
            v_hat = nv / (1.0 - ADAM_B2 ** ADAM_STEP)
            g_ref[s] = g
            d_ref[s] = -ADAM_LR * (m_hat / (jnp.sqrt(v_hat) + ADAM_EPS) + ADAM_WD * w_ref[s])
            nm_ref[s] = nm
            nv_ref[s] = nv

    slabs = pl.BlockSpec((ns, tr, c), lambda i: (0, i, 0))
    return pl.pallas_call(
        body, name=name, grid=(r // tr,),
        in_specs=[pl.BlockSpec((a.shape[0], tr, c), lambda i: (0, i, 0)) for a in flat_parts] + [slabs] * 3,
        out_specs=[slabs] * 4,
        out_shape=[jax.ShapeDtypeStruct((ns, r, c), F32)] * 4,
        compiler_params=_cp(("parallel",)),
    )(*flat_parts, w, m, v)


_WEIGHTS = ("meta_tokens", "norm1_w", "w_in", "gdn_conv_w", "gdn_a_log", "gdn_dt_bias", "gdn_norm_w", "ssd_conv_w",
            "ssd_conv_b", "ssd_dt_bias", "ssd_a_log", "ssd_d", "ssd_norm_w", "swa_sinks", "w_proj_gdn", "w_proj_ssd",
            "w_proj_swa", "w_out", "norm2_w", "w_up", "w_down", "final_norm_w")
_SHARD_AXIS = {"meta_tokens": 1, "w_in": 2, "gdn_conv_w": 2, "ssd_conv_w": 2, "w_proj_gdn": 1, "w_proj_ssd": 1,
               "w_proj_swa": 1, "w_out": 1, "w_up": 2, "w_down": 1}
_BIG = tuple(n for n in _WEIGHTS if n in _SHARD_AXIS)
_SMALL = tuple(n for n in _WEIGHTS if n not in _SHARD_AXIS)
FLAT_C = 1024


def _pack(arrs, rows, lead=()):
    flat = jnp.concatenate([a.reshape(*lead, -1) for a in arrs], axis=-1)
    pad = rows * FLAT_C - flat.shape[-1]
    flat = jnp.pad(flat, [(0, 0)] * len(lead) + [(0, pad)])
    return flat.reshape(*lead, rows, FLAT_C)


def _unpack(flat, shapes, lead=()):
    flat = flat.reshape(*lead, -1)
    out, off = [], 0
    for s in shapes:
        n = math.prod(s)
        out.append(flat[..., off:off + n].reshape(*lead, *s))
        off += n
    return out


def _rows_for(shapes):
    n = sum(math.prod(s) for s in shapes)
    return -(-n // (FLAT_C * 8)) * 8


def _rows_tile(r, c):
    if r <= 256:
        return r
    return 128 if c > 1024 else 256


def _join(stacked, axis):
    moved = jnp.moveaxis(stacked, 0, axis)
    return moved.reshape(*moved.shape[:axis], -1, *moved.shape[axis + 2:])


def _unjoin(full, axis):
    cut = full.reshape(*full.shape[:axis], N_DEV, full.shape[axis] // N_DEV, *full.shape[axis + 1:])
    return jnp.moveaxis(cut, axis, 0)


def kernel(x, meta_tokens, norm1_w, w_in, gdn_conv_w, gdn_a_log, gdn_dt_bias, gdn_norm_w, ssd_conv_w, ssd_conv_b,
           ssd_dt_bias, ssd_a_log, ssd_d, ssd_norm_w, swa_sinks, w_proj_gdn, w_proj_ssd, w_proj_swa, w_out, norm2_w,
           w_up, w_down, final_norm_w, loss_target, m_meta_tokens, m_norm1_w, m_w_in, m_gdn_conv_w, m_gdn_a_log,
           m_gdn_dt_bias, m_gdn_norm_w, m_ssd_conv_w, m_ssd_conv_b, m_ssd_dt_bias, m_ssd_a_log, m_ssd_d, m_ssd_norm_w,
           m_swa_sinks, m_w_proj_gdn, m_w_proj_ssd, m_w_proj_swa, m_w_out, m_norm2_w, m_w_up, m_w_down,
           m_final_norm_w, v_meta_tokens, v_norm1_w, v_w_in, v_gdn_conv_w, v_gdn_a_log, v_gdn_dt_bias, v_gdn_norm_w,
           v_ssd_conv_w, v_ssd_conv_b, v_ssd_dt_bias, v_ssd_a_log, v_ssd_d, v_ssd_norm_w, v_swa_sinks, v_w_proj_gdn,
           v_w_proj_ssd, v_w_proj_swa, v_w_out, v_norm2_w, v_w_up, v_w_down, v_final_norm_w):
    args = (meta_tokens, norm1_w, w_in, gdn_conv_w, gdn_a_log, gdn_dt_bias, gdn_norm_w, ssd_conv_w, ssd_conv_b,
            ssd_dt_bias, ssd_a_log, ssd_d, ssd_norm_w, swa_sinks, w_proj_gdn, w_proj_ssd, w_proj_swa, w_out, norm2_w,
            w_up, w_down, final_norm_w, m_meta_tokens, m_norm1_w, m_w_in, m_gdn_conv_w, m_gdn_a_log,
            m_gdn_dt_bias, m_gdn_norm_w, m_ssd_conv_w, m_ssd_conv_b, m_ssd_dt_bias, m_ssd_a_log, m_ssd_d, m_ssd_norm_w,
            m_swa_sinks, m_w_proj_gdn, m_w_proj_ssd, m_w_proj_swa, m_w_out, m_norm2_w, m_w_up, m_w_down,
            m_final_norm_w, v_meta_tokens, v_norm1_w, v_w_in, v_gdn_conv_w, v_gdn_a_log, v_gdn_dt_bias, v_gdn_norm_w,
            v_ssd_conv_w, v_ssd_conv_b, v_ssd_dt_bias, v_ssd_a_log, v_ssd_d, v_ssd_norm_w, v_swa_sinks, v_w_proj_gdn,
            v_w_proj_ssd, v_w_proj_swa, v_w_out, v_norm2_w, v_w_up, v_w_down, v_final_norm_w)
    nw = len(_WEIGHTS)
    w = dict(zip(_WEIGHTS, args[:nw]))
    m = dict(zip(_WEIGHTS, args[nw:2 * nw]))
    v = dict(zip(_WEIGHTS, args[2 * nw:]))

    depth = w["w_in"].shape[0]
    small_shapes = [w[n].shape for n in _SMALL]
    small_rows = _rows_for(small_shapes)

    def flat2(t):
        return t.reshape(-1, t.shape[-1])

    tiny_names = [n for n in _BIG if n not in _MATMUL]

    def shard(n, l):
        return w[n][l].astype(BF16)

    first = _allgather_call([shard("w_in", 0)] + [flat2(w[n]) for n in tiny_names], "gather_weights")
    w_in_stacked = first[0]
    joined = {n: _join(t.reshape(N_DEV, *w[n].shape), _SHARD_AXIS[n]) for n, t in zip(tiny_names, first[1:])}
    slot_shapes = {"w_in": (D_MODEL, sum(_SPLIT)), "w_up": (D_MODEL, D_FF), "w_down": (D_FF, D_MODEL)}
    slot_shapes.update({n: (D_MODEL, D_MODEL) for n in _LATE[:4]})

    def layer_fn(l, w_in_full, late_shards, next_shards):
        w_in_t = w_in_full.T
        if l == 0:
            def fn(x_rows, meta, p, slot, exchange_slots):
                out, g, placeholders = _layer(_embed(x_rows, meta), p, w_in_full, w_in_t, slot, late_shards,
                                              next_shards, exchange_slots)
                return (out, placeholders), g
        else:
            def fn(h_in, p, slot, exchange_slots):
                out, g, placeholders = _layer(h_in, p, w_in_full, w_in_t, slot, late_shards, next_shards,
                                              exchange_slots)
                return (out, placeholders), g
        return fn

    h, vjps = None, []
    for l in range(depth):
        slot = {n: jnp.zeros(s, F32) for n, s in slot_shapes.items()}
        p = {n: (joined[n][l] if n in joined else w[n][l]) for n in _PER_LAYER}
        more = l + 1 < depth
        next_shards = [shard("w_in", l + 1)] if more else []
        exchange_slots = tuple(jnp.zeros((3, *w[n][l + 1].shape), BF16) for n in _MATMUL) if more else ()
        lead = (x[0], joined["meta_tokens"]) if l == 0 else (h,)
        fn = layer_fn(l, _regroup_w_in(w_in_stacked), [shard(n, l) for n in _LATE], next_shards)
        (h, _), vjp, g_next = jax.vjp(fn, *lead, p, slot, exchange_slots, has_aux=True)
        if more:
            w_in_stacked = g_next[0]
        vjps.append(vjp)
    loss, head_vjp = jax.vjp(_make_loss_head(loss_target[0]), h, w["final_norm_w"].reshape(1, -1))
    dh, d_final = head_vjp(jnp.ones((), F32))
    loss = lax.psum(loss, ("x", "y", "c"))

    def by_core(name, g):
        if name == "w_in":
            shards = _ungroup_w_in(g)
            return shards[0::2], shards[1::2]
        if name == "w_up":
            t = g.reshape(D_MODEL, 4, 2, D_FF // N_DEV)
            return [t[:, q, 0] for q in range(4)], [t[:, q, 1] for q in range(4)]
        t = g.reshape(4, 2, -1, g.shape[-1])
        return [t[q, 0] for q in range(4)], [t[q, 1] for q in range(4)]

    own, incoming, layer_grads, outgoing = {}, {}, [None] * depth, ()
    for l in reversed(range(depth)):
        if l == 0:
            gx, d_meta, dp, dslot, arrived = vjps[0]((dh, tuple(outgoing)))
        else:
            dh, dp, dslot, arrived = vjps[l]((dh, tuple(outgoing)))
        incoming.update({(n, l + 1): t for n, t in zip(_MATMUL, arrived)})
        layer_grads[l] = dp
        todo = [((n, l), dslot[n]) for n in _MATMUL]
        if l == 0:
            full_grads = {"meta_tokens": d_meta}
            full_grads.update({n: jnp.stack([layer_grads[k][n] for k in range(depth)]) for n in tiny_names[1:]})
            todo += [((n, None), _unjoin(full_grads[n], _SHARD_AXIS[n]).reshape(N_DEV, -1, w[n].shape[-1]))
                     for n in tiny_names]
        pairs = [by_core(u[0], g) for u, g in todo]
        from_sibling = _sibling_exchange_call([a for a, _ in pairs], [b for _, b in pairs], "grads_to_sibling_%d" % l)
        outgoing = []
        for (u, _), (a0, a1), s in zip(todo, pairs, from_sibling):
            own[u], part = _chip_partial_call(a0, a1, s, _rows_tile(s.shape[1], s.shape[2]), "chip_partial_" + u[0])
            outgoing.append(part)
        if l == 0:
            incoming.update(zip([u for u, _ in todo], _chip_exchange_call(outgoing, "grads_to_chips")))

    g_small = {n: jnp.stack([layer_grads[k][n] for k in range(depth)]) for n in _SMALL if n != "final_norm_w"}
    g_small["final_norm_w"] = d_final.reshape(-1)

    by_name = {}
    for n in _BIG:
        layers = list(range(depth)) if n in _MATMUL else [None]
        parts = [[own[n, l][None], incoming[n, l]] for l in layers]
        r, c = own[n, layers[0]].shape
        stacked = [d[n].reshape(len(layers), r, c) for d in (w, m, v)]
        res = _adamw_call(parts, *stacked, _rows_tile(r, c), "adamw_" + n)
        by_name[n] = [t.reshape(w[n].shape) for t in res]

    small_parts = _allgather_call([_pack([g_small[n] for n in _SMALL], small_rows)], "gather_small_grads")
    small_out = _adamw_call([small_parts], *[_pack([d[n] for n in _SMALL], small_rows)[None] for d in (w, m, v)],
                            small_rows, "adamw_replicated")
    for kind in range(4):
        for n, t in zip(_SMALL, _unpack(small_out[kind][0], small_shapes)):
            by_name.setdefault(n, [None] * 4)[kind] = t

    outs = [by_name[n][kind] for kind in range(4) for n in _WEIGHTS]
    return (loss, gx[None], *outs)
```

```python
import functools
import math

import jax
import jax.numpy as jnp
from jax import lax
from jax.experimental import pallas as pl
from jax.experimental.pallas import tpu as pltpu

F32 = jnp.float32
BF16 = jnp.bfloat16
HI = lax.Precision.HIGH
NEG = -1e30

D_MODEL = 1024
N_META = 16
BLK = 128
NPAD = BLK - N_META
RMS_EPS = 1e-6
L2_EPS = 1e-6
CONV_K = 4
CONV_W = 256

GDN_H, GDN_D, GDN_C = 8, 128, 64
SSD_H, SSD_P, SSD_G, SSD_N = 16, 64, 4, 128
SSD_HPG = SSD_H // SSD_G
SWA_QH, SWA_KVH, SWA_D = 16, 4, 64
SWA_REP = SWA_QH // SWA_KVH
D_FF = 4 * D_MODEL

N_DEV = 8
MESH = pl.DeviceIdType.MESH

ADAM_LR, ADAM_B1, ADAM_B2, ADAM_EPS, ADAM_WD, ADAM_STEP = 0.001, 0.9, 0.999, 1e-08, 0.01, 10

VMEM_LIMIT = 56 * 1024 * 1024


def _cp(sem=None):
    return pltpu.CompilerParams(dimension_semantics=sem, vmem_limit_bytes=VMEM_LIMIT)


def _dot(a, b, ca, cb, prec=HI):
    return lax.dot_general(a, b, (((ca,), (cb,)), ((), ())), precision=prec, preferred_element_type=F32)


def _nn(a, b, prec=HI):
    return _dot(a, b, 1, 0, prec)


def _nt(a, b, prec=HI):
    return _dot(a, b, 1, 1, prec)


def _tn(a, b, prec=HI):
    return _dot(a, b, 0, 0, prec)


def _bdot(a, b, ca, cb):
    return lax.dot_general(a.astype(BF16), b.astype(BF16), (((ca,), (cb,)), ((), ())), preferred_element_type=F32)


@jax.custom_vjp
def _lo_nn(a, b):
    return _bdot(a, b, 1, 0)


_lo_nn.defvjp(lambda a, b: (_bdot(a, b, 1, 0), (a, b)),
              lambda r, d: (_bdot(d, r[1], 1, 1), _bdot(r[0], d, 0, 0)))


@jax.custom_vjp
def _lo_nt(a, b):
    return _bdot(a, b, 1, 1)


_lo_nt.defvjp(lambda a, b: (_bdot(a, b, 1, 1), (a, b)),
              lambda r, d: (_bdot(d, r[1], 1, 0), _bdot(d, r[0], 0, 0)))


@jax.custom_vjp
def _lo_tn(a, b):
    return _bdot(a, b, 0, 0)


_lo_tn.defvjp(lambda a, b: (_bdot(a, b, 0, 0), (a, b)),
              lambda r, d: (_bdot(r[1], d, 1, 1), _bdot(r[0], d, 1, 0)))


def _iota2(n, m, axis):
    return lax.broadcasted_iota(jnp.int32, (n, m), axis)


def _silu(x):
    return x * jax.nn.sigmoid(x)


def _softplus(x):
    return jnp.maximum(x, 0.0) + jnp.log(1.0 + jnp.exp(-jnp.abs(x)))


def _row_of(col):
    n = col.shape[0]
    return jnp.broadcast_to(col, (n, n)).T


def _cumsum_col(col):
    n = col.shape[0]
    tril = (_iota2(n, n, 0) >= _iota2(n, n, 1)).astype(F32)
    return _nn(tril, col)


def _tri_inv(a):
    n = a.shape[0]
    r, c = _iota2(n, n, 0), _iota2(n, n, 1)
    eye = (r == c).astype(F32)
    blk = jnp.right_shift(r, 4) == jnp.right_shift(c, 4)
    d = jnp.where(blk, a, 0.0)
    off = a - d
    d2 = _nn(d, d)
    d4 = _nn(d2, d2)
    d8 = _nn(d4, d4)
    td = _nn(_nn(_nn(eye - d, eye + d2), eye + d4), eye + d8)
    m = _nn(td, off)
    m2 = _nn(m, m)
    return _nn(_nn(eye - m, eye + m2), td)


@jax.custom_vjp
def _tri_solve(a, inv, rhs):
    return _nn(inv, rhs)


def _tri_solve_fwd(a, inv, rhs):
    sol = _nn(inv, rhs)
    return sol, (inv, sol)


def _tri_solve_bwd(res, dsol):
    inv, sol = res
    drhs = _nn(inv.T, dsol)
    return -_nt(drhs, sol), jnp.zeros_like(inv), drhs


_tri_solve.defvjp(_tri_solve_fwd, _tri_solve_bwd)


def _gdn_chunk(qa, ka, va, gate, a_raw, b_raw, s, a_log, dt_bias, norm_w, valid, inv=None, want_inv=False):
    c = qa.shape[0]
    q = qa * lax.rsqrt(jnp.sum(qa * qa, axis=-1, keepdims=True) + L2_EPS) * (GDN_D ** -0.5)
    k = ka * lax.rsqrt(jnp.sum(ka * ka, axis=-1, keepdims=True) + L2_EPS)
    beta = jax.nn.sigmoid(b_raw)
    g = -jnp.exp(a_log) * _softplus(a_raw + dt_bias) * valid
    gam = _cumsum_col(g)
    gam_row = _row_of(gam)
    r, cc = _iota2(c, c, 0), _iota2(c, c, 1)
    decay = jnp.exp(jnp.where(r >= cc, gam - gam_row, NEG))
    kb = k * beta
    a = jnp.where(r > cc, _lo_nt(kb, k) * decay, 0.0)
    egam = jnp.exp(gam)
    if inv is None:
        inv = _tri_inv(lax.stop_gradient(a))
    sol = _tri_solve(a, inv, jnp.concatenate([va * beta, kb * egam], axis=1))
    u = sol[:, :GDN_D]
    w = sol[:, GDN_D:]
    attn = _lo_nt(q, k) * decay
    g_last = jnp.sum(g, axis=0, keepdims=True)
    k_tail = k * jnp.exp(g_last - gam)
    v_new = u - _lo_nn(w, s)
    o = _lo_nn(q * egam, s) + _lo_nn(attn, v_new)
    s_new = s * jnp.exp(g_last) + _lo_tn(k_tail, v_new)
    y = o * lax.rsqrt(jnp.mean(o * o, axis=-1, keepdims=True) + RMS_EPS) * norm_w * _silu(gate)
    return (y, s_new, inv) if want_inv else (y, s_new)


def _valid_col(row0, n):
    return (row0 + _iota2(n, 1, 0) >= NPAD).astype(F32)


GDN_HB = GDN_H

SM_B, SM_A, SM_DT, SM_W = 0, 8, 16, 128


def _pick_cols(sm, first, n):
    return jnp.stack([sm[:, first + j:first + j + 1] for j in range(n)])


def _spread_cols(cols, first):
    lane = _iota2(1, SM_W, 1)
    out = None
    for j in range(cols.shape[0]):
        term = cols[j] * (lane == first + j).astype(F32)
        out = term if out is None else out + term
    return out


def _widen(t, width):
    if width == t.shape[1]:
        return t
    return jnp.concatenate([t, jnp.zeros((t.shape[0], width - t.shape[1]), t.dtype)], axis=1)


def _gdn_specs(nc, rev):
    ci = (lambda i: nc - 1 - i) if rev else (lambda i: i)
    hb = GDN_HB
    tile = pl.BlockSpec((GDN_C, hb * GDN_D), lambda h, i: (ci(i), h))
    col = pl.BlockSpec((GDN_C, SM_W), lambda h, i: (ci(i), 0))
    scal = pl.BlockSpec((hb, 1, 1), lambda h, i: (h, 0, 0))
    nw = pl.BlockSpec((1, GDN_D), lambda h, i: (0, 0))
    st = pl.BlockSpec((hb, 1, GDN_D, GDN_D), lambda h, i: (h, ci(i), 0, 0))
    return tile, col, scal, nw, st


def _lanes(j):
    return slice(j * GDN_D, (j + 1) * GDN_D)


def _by_head(ref):
    return jnp.stack([ref[:, _lanes(j)] for j in range(GDN_HB)])


def _gdn_fwd_call(q, k, v, gate, small, a_log, dt_bias, norm_w, shards=()):
    seq = q.shape[0]
    nc = seq // GDN_C
    ns = len(shards)
    tile, col, scal, nw, st = _gdn_specs(nc, False)

    def body(*refs):
        q_ref, k_ref, v_ref, g_ref, sm_ref, al_ref, dt_ref, nw_ref = refs[:8]
        y_ref, st_ref, inv_ref = refs[8 + ns:11 + ns]
        s_scr = refs[11 + 2 * ns]
        i = pl.program_id(1)
        if ns:
            start, relay, finish = _gather_phases(refs[8:8 + ns], refs[11 + ns:11 + 2 * ns], *refs[12 + 2 * ns:])
            pl.when(i == 0)(start)
            pl.when(i == nc - 1)(relay)

        @pl.when(i == 0)
        def _():
            s_scr[...] = jnp.zeros_like(s_scr)

        s = s_scr[...]
        st_ref[:, 0] = s
        sm = sm_ref[...]
        fn = jax.vmap(functools.partial(_gdn_chunk, valid=_valid_col(i * GDN_C, GDN_C), want_inv=True))
        y, s_new, inv = fn(_by_head(q_ref), _by_head(k_ref), _by_head(v_ref), _by_head(g_ref),
                           _pick_cols(sm, SM_A, GDN_H), _pick_cols(sm, SM_B, GDN_H), s,
                           al_ref[...], dt_ref[...], jnp.broadcast_to(nw_ref[...], (GDN_HB, 1, GDN_D)))
        for j in range(GDN_HB):
            y_ref[:, _lanes(j)] = y[j]
        inv_ref[:, 0] = inv
        s_scr[...] = s_new
        if ns:
            pl.when(i == nc - 1)(finish)

    return pl.pallas_call(
        body, name="gdn_fwd", grid=(GDN_H // GDN_HB, nc),
        in_specs=[tile, tile, tile, tile, col, scal, scal, nw] + [_ANY] * ns,
        out_specs=[tile, st, pl.BlockSpec((GDN_HB, 1, GDN_C, GDN_C), lambda h, i: (h, i, 0, 0))] + [_ANY] * ns,
        out_shape=[jax.ShapeDtypeStruct((seq, GDN_H * GDN_D), F32),
                   jax.ShapeDtypeStruct((GDN_H, nc, GDN_D, GDN_D), F32),
                   jax.ShapeDtypeStruct((GDN_H, nc, GDN_C, GDN_C), F32)] + _gather_out_shapes(shards),
        scratch_shapes=[pltpu.VMEM((GDN_HB, GDN_D, GDN_D), F32)] + (_gather_sems(ns) if ns else []),
        compiler_params=_cp(("parallel", "arbitrary")),
    )(q, k, v, gate, small, a_log, dt_bias, norm_w, *shards)


def _gdn_bwd_call(q, k, v, gate, small, a_log, dt_bias, norm_w, states, invs, dy, outgoing=()):
    seq = q.shape[0]
    nc = seq // GDN_C
    no = len(outgoing)
    tile, col, scal, nw, st = _gdn_specs(nc, True)
    nwh = pl.BlockSpec((GDN_HB, 1, GDN_D), lambda h, i: (h, 0, 0))
    inv_spec = pl.BlockSpec((GDN_HB, 1, GDN_C, GDN_C), lambda h, i: (h, nc - 1 - i, 0, 0))

    def body(*refs):
        q_ref, k_ref, v_ref, g_ref, sm_ref, al_ref, dt_ref, nw_ref, st_ref, inv_ref, dy_ref = refs[:11]
        dq_ref, dk_ref, dv_ref, dg_ref, dsm_ref, dal_ref, ddt_ref, dnw_ref = refs[11 + no:19 + no]
        ds_scr = refs[19 + 2 * no]
        i = pl.program_id(1)
        if no:
            start, finish = _chip_exchange_phases(refs[11:11 + no], refs[19 + no:19 + 2 * no], *refs[20 + 2 * no:])
            pl.when(i == 0)(start)

        @pl.when(i == 0)
        def _():
            ds_scr[...] = jnp.zeros_like(ds_scr)
            dal_ref[...] = jnp.zeros_like(dal_ref)
            ddt_ref[...] = jnp.zeros_like(ddt_ref)
            dnw_ref[...] = jnp.zeros_like(dnw_ref)

        sm = sm_ref[...]
        valid = _valid_col((nc - 1 - i) * GDN_C, GDN_C)
        kept = inv_ref[:, 0]

        def fn(*heads):
            return jax.vmap(lambda *t: _gdn_chunk(*t[:-1], valid=valid, inv=t[-1]))(*heads, kept)

        _, vjp = jax.vjp(fn, _by_head(q_ref), _by_head(k_ref), _by_head(v_ref), _by_head(g_ref),
                         _pick_cols(sm, SM_A, GDN_H), _pick_cols(sm, SM_B, GDN_H), st_ref[:, 0], al_ref[...],
                         dt_ref[...], jnp.broadcast_to(nw_ref[...], (GDN_HB, 1, GDN_D)))
        dq, dk, dv, dg, da, db, ds, dal, ddt, dnw = vjp((_by_head(dy_ref), ds_scr[...]))
        for j in range(GDN_HB):
            dq_ref[:, _lanes(j)] = dq[j]
            dk_ref[:, _lanes(j)] = dk[j]
            dv_ref[:, _lanes(j)] = dv[j]
            dg_ref[:, _lanes(j)] = dg[j]
        dsm_ref[...] = _widen(_spread_cols(da, SM_A) + _spread_cols(db, SM_B), dsm_ref.shape[1])
        ds_scr[...] = ds
        dal_ref[...] += dal
        ddt_ref[...] += ddt
        dnw_ref[...] += dnw
        if no:
            pl.when(i == nc - 1)(finish)

    big = jax.ShapeDtypeStruct((seq, GDN_H * GDN_D), F32)
    return pl.pallas_call(
        body, name="gdn_bwd", grid=(GDN_H // GDN_HB, nc),
        in_specs=[tile, tile, tile, tile, col, scal, scal, nw, st, inv_spec, tile] + [_ANY] * no,
        out_specs=[tile, tile, tile, tile, pl.BlockSpec((GDN_C, small.shape[1]), lambda h, i: (nc - 1 - i, 0)),
                   scal, scal, nwh] + [_ANY] * no,
        out_shape=[big, big, big, big, jax.ShapeDtypeStruct(small.shape, F32),
                   jax.ShapeDtypeStruct((GDN_H, 1, 1), F32), jax.ShapeDtypeStruct((GDN_H, 1, 1), F32),
                   jax.ShapeDtypeStruct((GDN_H, 1, GDN_D), F32)] + _chip_exchange_out_shapes(outgoing),
        scratch_shapes=[pltpu.VMEM((GDN_HB, GDN_D, GDN_D), F32)] + (_chip_exchange_sems(no) if no else []),
        compiler_params=_cp(("parallel", "arbitrary")),
    )(q, k, v, gate, small, a_log, dt_bias, norm_w, states, invs, dy, *outgoing)


@jax.custom_vjp
def gdn_core(q, k, v, gate, small, a_log, dt_bias, norm_w, shards, slots):
    y, _, _, *gathered = _gdn_fwd_call(q, k, v, gate, small, a_log, dt_bias, norm_w, shards)
    return y, tuple(gathered), tuple(jnp.zeros((4, *s.shape[1:]), s.dtype) for s in slots)


def _gdn_core_fwd(q, k, v, gate, small, a_log, dt_bias, norm_w, shards, slots):
    y, states, invs, *gathered = _gdn_fwd_call(q, k, v, gate, small, a_log, dt_bias, norm_w, shards)
    out = (y, tuple(gathered), tuple(jnp.zeros((4, *s.shape[1:]), s.dtype) for s in slots))
    return out, (q, k, v, gate, small, a_log, dt_bias, norm_w, states, invs, shards)


def _gdn_core_bwd(res, cts):
    *args, shards = res
    dy, _, outgoing = cts
    dq, dk, dv, dg, dsm, dal, ddt, dnw, *incoming = _gdn_bwd_call(*args, dy, outgoing)
    return (dq, dk, dv, dg, dsm, dal, ddt, jnp.sum(dnw, axis=0), tuple(jnp.zeros_like(s) for s in shards),
            tuple(incoming))


gdn_core.defvjp(_gdn_core_fwd, _gdn_core_bwd)


def _ssd_head(x, z, dt_raw, h, dt_bias, a_log, d_skip, bm, cm, cb, valid):
    c = bm.shape[0]
    r, cc = _iota2(c, c, 0), _iota2(c, c, 1)
    dtp = _softplus(dt_raw + dt_bias)
    x = x * valid
    adt = -jnp.exp(a_log) * dtp * valid
    xdt = x * dtp
    acum = _cumsum_col(adt)
    lmat = jnp.exp(jnp.where(r >= cc, acum - _row_of(acum), NEG))
    a_last = jnp.sum(adt, axis=0, keepdims=True)
    y = _lo_nn(cb * lmat, xdt) + _lo_nt(cm * jnp.exp(acum), h) + d_skip * x
    h_new = h * jnp.exp(a_last) + _lo_tn(xdt * jnp.exp(a_last - acum), bm)
    return y * _silu(z), h_new


SSD_SIDE = SSD_H


def _ssd_chunk(xs, z, bm, cm, dt_raw, h, dt_bias, a_log, d_skip, norm_w, valid):
    nh, c, p = xs.shape
    ng = bm.shape[0]
    hpg = nh // ng
    bm = bm * valid
    cm = cm * valid
    cb = jax.vmap(_lo_nt)(cm, bm)
    per_head = lambda t: jnp.repeat(t, hpg, axis=0)
    args = (xs, z, dt_raw, h, dt_bias, a_log, d_skip, per_head(bm), per_head(cm), per_head(cb))
    outs = [jax.vmap(functools.partial(_ssd_head, valid=valid))(*[t[s:s + SSD_SIDE] for t in args])
            for s in range(0, nh, SSD_SIDE)]
    ys = jnp.concatenate([o[0] for o in outs], axis=0)
    hs = jnp.concatenate([o[1] for o in outs], axis=0)
    ss = jnp.sum(jnp.sum(ys * ys, axis=-1, keepdims=True).reshape(ng, hpg, c, 1), axis=1, keepdims=True)
    rstd = lax.rsqrt(ss / (hpg * p) + RMS_EPS)
    return (ys.reshape(ng, hpg, c, p) * rstd).reshape(nh, c, p) * norm_w, hs


SSD_INNER = SSD_H * SSD_P
SSD_BC = SSD_G * SSD_N


def _split_lanes(t, n, w):
    return jnp.stack([t[:, j * w:(j + 1) * w] for j in range(n)])


def _join_lanes(t):
    return jnp.concatenate([t[j] for j in range(t.shape[0])], axis=1)


def _ssd_specs(nc, rev):
    ci = (lambda i: nc - 1 - i) if rev else (lambda i: i)
    wide = pl.BlockSpec((BLK, SSD_INNER), lambda i: (ci(i), 0))
    bmat = pl.BlockSpec((BLK, SSD_BC), lambda i: (ci(i), SSD_INNER // SSD_BC))
    cmat = pl.BlockSpec((BLK, SSD_BC), lambda i: (ci(i), SSD_INNER // SSD_BC + 1))
    xbc = pl.BlockSpec((BLK, SSD_INNER + 2 * SSD_BC), lambda i: (ci(i), 0))
    col = pl.BlockSpec((BLK, SM_W), lambda i: (ci(i), 0))
    scal = pl.BlockSpec((SSD_H, 1, 1), lambda i: (0, 0, 0))
    nw = pl.BlockSpec((SSD_H, 1, SSD_P), lambda i: (0, 0, 0))
    st = pl.BlockSpec((SSD_H, 1, SSD_P, SSD_N), lambda i: (0, ci(i), 0, 0))
    return wide, bmat, cmat, xbc, col, scal, nw, st


def _ssd_fwd_call(xbc, z, small, dt_bias, a_log, d_skip, norm_w, shards=()):
    seq = z.shape[0]
    nc = seq // BLK
    ns = len(shards)
    wide, bmat, cmat, _, col, scal, nw, st = _ssd_specs(nc, False)

    def body(*refs):
        x_ref, b_ref, c_ref, z_ref, sm_ref, db_ref, al_ref, ds_ref, nw_ref = refs[:9]
        y_ref, st_ref = refs[9 + ns:11 + ns]
        h_scr = refs[11 + 2 * ns]
        i = pl.program_id(0)
        if ns:
            start, relay, finish = _gather_phases(refs[9:9 + ns], refs[11 + ns:11 + 2 * ns], *refs[12 + 2 * ns:])
            pl.when(i == 0)(start)
            pl.when(i == nc - 1)(relay)

        @pl.when(i == 0)
        def _():
            h_scr[...] = jnp.zeros_like(h_scr)

        h = h_scr[...]
        st_ref[:, 0] = h
        y, h_new = _ssd_chunk(_split_lanes(x_ref[...], SSD_H, SSD_P), _split_lanes(z_ref[...], SSD_H, SSD_P),
                              _split_lanes(b_ref[...], SSD_G, SSD_N), _split_lanes(c_ref[...], SSD_G, SSD_N),
                              _pick_cols(sm_ref[...], SM_DT, SSD_H), h, db_ref[...], al_ref[...], ds_ref[...],
                              nw_ref[...], _valid_col(i * BLK, BLK))
        y_ref[...] = _join_lanes(y)
        h_scr[...] = h_new
        if ns:
            pl.when(i == nc - 1)(finish)

    return pl.pallas_call(
        body, name="ssd_fwd", grid=(nc,),
        in_specs=[wide, bmat, cmat, wide, col, scal, scal, scal, nw] + [_ANY] * ns,
        out_specs=[wide, st] + [_ANY] * ns,
        out_shape=[jax.ShapeDtypeStruct((seq, SSD_INNER), F32),
                   jax.ShapeDtypeStruct((SSD_H, nc, SSD_P, SSD_N), F32)] + _gather_out_shapes(shards),
        scratch_shapes=[pltpu.VMEM((SSD_H, SSD_P, SSD_N), F32)] + (_gather_sems(ns) if ns else []),
        compiler_params=_cp(("arbitrary",)),
    )(xbc, xbc, xbc, z, small, dt_bias, a_log, d_skip, norm_w, *shards)


def _ssd_bwd_call(xbc, z, small, dt_bias, a_log, d_skip, norm_w, states, dy):
    seq = z.shape[0]
    nc = seq // BLK
    wide, bmat, cmat, xbc_spec, col, scal, nw, st = _ssd_specs(nc, True)

    def body(x_ref, b_ref, c_ref, z_ref, sm_ref, db_ref, al_ref, ds_ref, nw_ref, st_ref, dy_ref,
             dxbc_ref, dz_ref, dsm_ref, ddb_ref, dal_ref, dds_ref, dnw_ref, dh_scr):
        i = pl.program_id(0)

        @pl.when(i == 0)
        def _():
            dh_scr[...] = jnp.zeros_like(dh_scr)
            ddb_ref[...] = jnp.zeros_like(ddb_ref)
            dal_ref[...] = jnp.zeros_like(dal_ref)
            dds_ref[...] = jnp.zeros_like(dds_ref)
            dnw_ref[...] = jnp.zeros_like(dnw_ref)

        fn = functools.partial(_ssd_chunk, valid=_valid_col((nc - 1 - i) * BLK, BLK))
        _, vjp = jax.vjp(fn, _split_lanes(x_ref[...], SSD_H, SSD_P), _split_lanes(z_ref[...], SSD_H, SSD_P),
                         _split_lanes(b_ref[...], SSD_G, SSD_N), _split_lanes(c_ref[...], SSD_G, SSD_N),
                         _pick_cols(sm_ref[...], SM_DT, SSD_H), st_ref[:, 0], db_ref[...], al_ref[...], ds_ref[...],
                         nw_ref[...])
        dx, dz, dbm, dcm, ddt, dh, ddb, dal, dds, dnw = vjp((_split_lanes(dy_ref[...], SSD_H, SSD_P), dh_scr[...]))
        dxbc_ref[:, :SSD_INNER] = _join_lanes(dx)
        dxbc_ref[:, SSD_INNER:SSD_INNER + SSD_BC] = _join_lanes(dbm)
        dxbc_ref[:, SSD_INNER + SSD_BC:] = _join_lanes(dcm)
        dz_ref[...] = _join_lanes(dz)
        dsm_ref[...] = _widen(_spread_cols(ddt, SM_DT), dsm_ref.shape[1])
        dh_scr[...] = dh
        ddb_ref[...] += ddb
        dal_ref[...] += dal
        dds_ref[...] += dds
        dnw_ref[...] += dnw

    sshape = jax.ShapeDtypeStruct((SSD_H, 1, 1), F32)
    return pl.pallas_call(
        body, name="ssd_bwd", grid=(nc,),
        in_specs=[wide, bmat, cmat, wide, col, scal, scal, scal, nw, st, wide],
        out_specs=[xbc_spec, wide, pl.BlockSpec((BLK, small.shape[1]), lambda i: (nc - 1 - i, 0)), scal, scal, scal, nw],
        out_shape=[jax.ShapeDtypeStruct(xbc.shape, F32), jax.ShapeDtypeStruct(z.shape, F32),
                   jax.ShapeDtypeStruct(small.shape, F32), sshape, sshape, sshape,
                   jax.ShapeDtypeStruct((SSD_H, 1, SSD_P), F32)],
        scratch_shapes=[pltpu.VMEM((SSD_H, SSD_P, SSD_N), F32)],
        compiler_params=_cp(("arbitrary",)),
    )(xbc, xbc, xbc, z, small, dt_bias, a_log, d_skip, norm_w, states, dy)


@jax.custom_vjp
def ssd_core(xbc, z, small, dt_bias, a_log, d_skip, norm_w, shards):
    y, _, *gathered = _ssd_fwd_call(xbc, z, small, dt_bias, a_log, d_skip, norm_w, shards)
    return y, tuple(gathered)


def _ssd_core_fwd(*args):
    y, states, *gathered = _ssd_fwd_call(*args)
    return (y, tuple(gathered)), (*args[:-1], states, args[-1])


def _ssd_core_bwd(res, cts):
    *args, shards = res
    return (*_ssd_bwd_call(*args, cts[0]), tuple(jnp.zeros_like(s) for s in shards))


ssd_core.defvjp(_ssd_core_fwd, _ssd_core_bwd)


def _swa_block(q, km, kp, kc, vm, vp, vc, sink, n):
    rows = SWA_REP * BLK
    qs = q.reshape(rows, SWA_D) * (SWA_D ** -0.5)
    s = _lo_nt(qs, jnp.concatenate([km, kp, kc], axis=0))
    i = jnp.bitwise_and(_iota2(rows, 3 * BLK, 0), BLK - 1)
    col = _iota2(rows, 3 * BLK, 1)
    j = jnp.bitwise_and(col, BLK - 1)
    part = jnp.right_shift(col, 7)
    ok_m = (part == 0) & (j >= NPAD) & ((n >= 1) | (j <= i))
    ok_p = (part == 1) & (n >= 2) & (j > i)
    ok_c = (part == 2) & (n >= 1) & (j <= i)
    ok = ok_m | ok_p | ok_c
    s = jnp.where(ok, s, NEG)
    snk = jnp.concatenate([jnp.broadcast_to(sink[r], (BLK, 1)) for r in range(SWA_REP)], axis=0)
    m = lax.stop_gradient(jnp.maximum(jnp.max(s, axis=-1, keepdims=True), snk))
    e = jnp.exp(s - m)
    p = e / (jnp.sum(e, axis=-1, keepdims=True) + jnp.exp(snk - m))
    o = _lo_nn(p, jnp.concatenate([vm, vp, vc], axis=0))
    return o.reshape(SWA_REP, BLK, SWA_D)


SWA_QW = SWA_QH * SWA_D
SWA_KW = SWA_KVH * SWA_D


def _swa_specs(nb, rev):
    ci = (lambda i: nb - 1 - i) if rev else (lambda i: i)
    qsp = pl.BlockSpec((BLK, SWA_QW), lambda i: (ci(i), 0))
    cur = pl.BlockSpec((BLK, 2 * SWA_KW), lambda i: (ci(i), 0))
    prev = pl.BlockSpec((BLK, 2 * SWA_KW), lambda i: (jnp.maximum(ci(i) - 1, 0), 0))
    meta = pl.BlockSpec((BLK, 2 * SWA_KW), lambda i: (0, 0))
    scal = pl.BlockSpec((SWA_QH, 1, 1), lambda i: (0, 0, 0))
    return qsp, cur, prev, meta, scal


def _swa_by_head(q, kvm, kvp, kvc, sink):
    def kv(t):
        return _split_lanes(t[:, :SWA_KW], SWA_KVH, SWA_D), _split_lanes(t[:, SWA_KW:], SWA_KVH, SWA_D)

    (km, vm), (kp, vp), (kc, vc) = kv(kvm), kv(kvp), kv(kvc)
    qh = _split_lanes(q, SWA_QH, SWA_D).reshape(SWA_KVH, SWA_REP, BLK, SWA_D)
    return qh, km, kp, kc, vm, vp, vc, sink.reshape(SWA_KVH, SWA_REP, 1, 1)


def _swa_kv_tile(dk, dv):
    return jnp.concatenate([_join_lanes(dk), _join_lanes(dv)], axis=1)


def _swa_fwd_call(q, kv, sink):
    seq = q.shape[0]
    nb = seq // BLK
    qsp, cur, prev, meta, scal = _swa_specs(nb, False)

    def body(q_ref, m_ref, p_ref, c_ref, s_ref, o_ref):
        fn = jax.vmap(functools.partial(_swa_block, n=pl.program_id(0)))
        o = fn(*_swa_by_head(q_ref[...], m_ref[...], p_ref[...], c_ref[...], s_ref[...]))
        o_ref[...] = _join_lanes(o.reshape(SWA_QH, BLK, SWA_D))

    return pl.pallas_call(
        body, name="swa_fwd", grid=(nb,),
        in_specs=[qsp, meta, prev, cur, scal],
        out_specs=qsp,
        out_shape=jax.ShapeDtypeStruct(q.shape, F32),
        compiler_params=_cp(("parallel",)),
    )(q, kv, kv, kv, sink)


def _swa_bwd_call(q, kv, sink, do):
    seq = q.shape[0]
    nb = seq // BLK
    qsp, cur, prev, meta, scal = _swa_specs(nb, True)

    def body(q_ref, m_ref, p_ref, c_ref, s_ref, do_ref, dq_ref, dkv_ref, ds_ref, prev_scr, meta_scr):
        i = pl.program_id(0)
        n = nb - 1 - i

        @pl.when(i == 0)
        def _():
            prev_scr[...] = jnp.zeros_like(prev_scr)
            meta_scr[...] = jnp.zeros_like(meta_scr)
            ds_ref[...] = jnp.zeros_like(ds_ref)

        fn = jax.vmap(functools.partial(_swa_block, n=n))
        _, vjp = jax.vjp(fn, *_swa_by_head(q_ref[...], m_ref[...], p_ref[...], c_ref[...], s_ref[...]))
        do = _split_lanes(do_ref[...], SWA_QH, SWA_D).reshape(SWA_KVH, SWA_REP, BLK, SWA_D)
        dq, dkm, dkp, dkc, dvm, dvp, dvc, dsk = vjp(do)
        dq_ref[...] = _join_lanes(dq.reshape(SWA_QH, BLK, SWA_D))
        ds_ref[...] += dsk.reshape(SWA_QH, 1, 1)
        meta_scr[...] += _swa_kv_tile(dkm, dvm)
        first = (n == 0).astype(F32)
        dkv_ref[...] = _swa_kv_tile(dkc, dvc) + prev_scr[...] + first * meta_scr[...]
        prev_scr[...] = _swa_kv_tile(dkp, dvp)

    return pl.pallas_call(
        body, name="swa_bwd", grid=(nb,),
        in_specs=[qsp, meta, prev, cur, scal, qsp],
        out_specs=[qsp, cur, scal],
        out_shape=[jax.ShapeDtypeStruct(q.shape, F32), jax.ShapeDtypeStruct(kv.shape, F32),
                   jax.ShapeDtypeStruct(sink.shape, F32)],
        scratch_shapes=[pltpu.VMEM((BLK, 2 * SWA_KW), F32)] * 2,
        compiler_params=_cp(("arbitrary",)),
    )(q, kv, kv, kv, sink, do)


@jax.custom_vjp
def swa_core(q, kv, sink):
    return _swa_fwd_call(q, kv, sink)


def _swa_core_fwd(q, kv, sink):
    return _swa_fwd_call(q, kv, sink), (q, kv, sink)


def _swa_core_bwd(res, do):
    return tuple(_swa_bwd_call(*res, do))


swa_core.defvjp(_swa_core_fwd, _swa_core_bwd)


def _tile(n, pref):
    if n <= pref:
        return n
    best = None
    for t in range(128, pref + 1, 128):
        if n % t == 0:
            best = t
    assert best is not None, (n, pref)
    return best


MM_TILE_BYTES = 9 * 1024 * 1024


def _mm_tiles(m, n, kk, a_bytes, b_bytes):
    if kk > 8192:
        return _tile(m, 2816 // a_bytes), _tile(n, 512), _tile(kk, 4096)
    if kk > 1408 and _tile(m, 1024) * kk * a_bytes <= MM_TILE_BYTES:
        return _tile(m, 1024), _tile(n, 1024 if 1024 * kk * b_bytes <= MM_TILE_BYTES else 512), kk
    return _tile(m, 1408), _tile(n, 1024 if kk <= 1408 else 512), _tile(kk, 1408)


def _mm_call(a, b, name):
    (m, kk), n = a.shape, b.shape[1]
    tm, tn, tk = _mm_tiles(m, n, kk, a.dtype.itemsize, b.dtype.itemsize)
    nk = kk // tk
    a_spec = pl.BlockSpec((tm, tk), lambda i, j, k: (i, k))
    b_spec = pl.BlockSpec((tk, tn), lambda i, j, k: (k, j))

    def body(a_ref, b_ref, o_ref, acc_ref):
        k = pl.program_id(2)
        part = jnp.dot(a_ref[...].astype(BF16), b_ref[...].astype(BF16), preferred_element_type=F32)

        @pl.when(k == 0)
        def _():
            acc_ref[...] = part

        @pl.when(k > 0)
        def _():
            acc_ref[...] += part

        @pl.when(k == nk - 1)
        def _():
            o_ref[...] = acc_ref[...]

    return pl.pallas_call(
        body, name=name, grid=(m // tm, n // tn, nk),
        in_specs=[a_spec, b_spec],
        out_specs=pl.BlockSpec((tm, tn), lambda i, j, k: (i, j)),
        out_shape=jax.ShapeDtypeStruct((m, n), F32),
        scratch_shapes=[pltpu.VMEM((tm, tn), F32)],
        compiler_params=_cp(("parallel", "parallel", "arbitrary")),
    )(a, b)


@jax.custom_vjp
def mm(a, b, b_t, grad_slot):
    return _mm_call(a, b, "mm_fwd")


def _mm_fwd(a, b, b_t, grad_slot):
    return _mm_call(a, b, "mm_fwd"), (a, b, b_t)


def _mm_bwd(res, dc):
    a, b, b_t = res
    return (_mm_call(dc, b_t, "mm_dx"), jnp.zeros_like(b), jnp.zeros_like(b_t),
            _mm_call(a.astype(BF16).T, dc, "mm_dw"))


mm.defvjp(_mm_fwd, _mm_bwd)


_SPLIT = (1024, 1024, 1024, 1024, 1024, 2048, 1024, 512, 3072, 512)


def _split_cols(u):
    offs = [sum(_SPLIT[:i]) for i in range(len(_SPLIT))]
    return tuple(u[:, o:o + s] for o, s in zip(offs, _SPLIT))


@jax.custom_vjp
def mm_split(a, b, b_t, grad_slot):
    return _split_cols(_mm_call(a, b, "mm_fwd"))


def _mm_split_fwd(a, b, b_t, grad_slot):
    return _split_cols(_mm_call(a, b, "mm_fwd")), (a, b, b_t)


def _mm_split_bwd(res, cts):
    return _mm_bwd(res, jnp.concatenate([c.astype(BF16) for c in cts], axis=1))


mm_split.defvjp(_mm_split_fwd, _mm_split_bwd)


def _row_specs(arrs, tr):
    return [pl.BlockSpec((tr, a.shape[1]), lambda i: (i, 0)) for a in arrs]


def _par_specs(arrs):
    return [pl.BlockSpec(a.shape, lambda i: (0, 0)) for a in arrs]


def _row_fwd_call(fn, rows, params, out_cols, tr, name):
    seq = rows[0].shape[0]
    nr = len(rows)

    def body(*refs):
        vals = [r[...] for r in refs[:-1]]
        refs[-1][...] = fn(*vals)

    return pl.pallas_call(
        body, name=name, grid=(seq // tr,),
        in_specs=_row_specs(rows, tr) + _par_specs(params),
        out_specs=pl.BlockSpec((tr, out_cols), lambda i: (i, 0)),
        out_shape=jax.ShapeDtypeStruct((seq, out_cols), F32),
        compiler_params=_cp(("parallel",)),
    )(*rows, *params)


def _row_bwd_call(fn, rows, params, dy, tr, name):
    seq = rows[0].shape[0]
    nr, npar = len(rows), len(params)

    def body(*refs):
        ins = refs[:nr + npar]
        dy_ref = refs[nr + npar]
        outs = refs[nr + npar + 1:]
        _, vjp = jax.vjp(fn, *[r[...] for r in ins])
        cts = vjp(dy_ref[...])
        for o_ref, ct in zip(outs[:nr], cts[:nr]):
            o_ref[...] = ct

        @pl.when(pl.program_id(0) == 0)
        def _():
            for o_ref in outs[nr:]:
                o_ref[...] = jnp.zeros_like(o_ref)

        for o_ref, ct in zip(outs[nr:], cts[nr:]):
            o_ref[...] += ct

    return pl.pallas_call(
        body, name=name, grid=(seq // tr,),
        in_specs=_row_specs(rows, tr) + _par_specs(params) + _row_specs([dy], tr),
        out_specs=_row_specs(rows, tr) + _par_specs(params),
        out_shape=[jax.ShapeDtypeStruct(a.shape, F32) for a in (*rows, *params)],
        compiler_params=_cp(("arbitrary",)),
    )(*rows, *params, dy)


def _make_rowop(fn, nrows, out_cols, tr, name):
    @jax.custom_vjp
    def op(*args):
        return _row_fwd_call(fn, args[:nrows], args[nrows:], out_cols, tr, name + "_fwd")

    def fwd(*args):
        return op(*args), args

    def bwd(args, dy):
        return tuple(_row_bwd_call(fn, args[:nrows], args[nrows:], dy, tr, name + "_bwd"))

    op.defvjp(fwd, bwd)
    return op


def _rms_fn(x, w):
    return x * lax.rsqrt(jnp.mean(x * x, axis=-1, keepdims=True) + RMS_EPS) * w


def _merge_fn(pa, pb, pc, gl):
    d = D_MODEL
    return (jax.nn.sigmoid(gl[:, :d]) * pa + jax.nn.sigmoid(gl[:, d:2 * d]) * pb
            + jax.nn.sigmoid(gl[:, 2 * d:]) * pc)


def _relu2_fn(a):
    r = jnp.maximum(a, 0.0)
    return r * r


rms_op = _make_rowop(_rms_fn, 1, D_MODEL, 384, "rms")
merge_op = _make_rowop(_merge_fn, 4, D_MODEL, 192, "merge")
relu2_op = _make_rowop(_relu2_fn, 1, D_FF, 192, "relu2")


def _conv_taps(xext, w, nrows):
    z = None
    for j in range(CONV_K):
        sh = CONV_K - 1 - j
        xs = pltpu.roll(xext, sh, 0) if sh else xext
        term = w[j:j + 1, :] * xs[8:8 + nrows, :]
        z = term if z is None else z + term
    return z


def _halo(ref, start, ok):
    return jnp.where(ok, ref[pl.ds(pl.multiple_of(start, 8), 8), :], 0.0)


def _conv_fwd_call(x, w, b):
    seq, ch = x.shape
    nb = seq // BLK

    def body(x_ref, w_ref, b_ref, o_ref):
        w = w_ref[...]
        bias = b_ref[...]

        def step(i, carry):
            r0 = pl.multiple_of(i * BLK, BLK)
            xext = jnp.concatenate([_halo(x_ref, jnp.maximum(r0 - 8, 0), i > 0), x_ref[pl.ds(r0, BLK), :]], axis=0)
            o_ref[pl.ds(r0, BLK), :] = _silu(_conv_taps(xext, w, BLK) + bias)
            return carry

        lax.fori_loop(0, nb, step, 0)

    strip = pl.BlockSpec((seq, CONV_W), lambda c: (0, c))
    return pl.pallas_call(
        body, name="conv_fwd", grid=(ch // CONV_W,),
        in_specs=[strip, pl.BlockSpec((CONV_K, CONV_W), lambda c: (0, c)), pl.BlockSpec((1, CONV_W), lambda c: (0, c))],
        out_specs=strip, out_shape=jax.ShapeDtypeStruct(x.shape, F32),
        compiler_params=_cp(("parallel",)),
    )(x, w, b)


def _conv_bwd_call(x, w, b, dy):
    seq, ch = x.shape
    nb = seq // BLK

    def body(x_ref, w_ref, b_ref, dy_ref, dx_ref, dw_ref, db_ref):
        w = w_ref[...]
        bias = b_ref[...]

        def step(i, carry):
            r0 = pl.multiple_of(i * BLK, BLK)
            last = i == nb - 1
            nxt = jnp.minimum(r0 + BLK, seq - 8)
            xext = jnp.concatenate([_halo(x_ref, jnp.maximum(r0 - 8, 0), i > 0), x_ref[pl.ds(r0, BLK), :],
                                    _halo(x_ref, nxt, jnp.logical_not(last))], axis=0)
            dyext = jnp.concatenate([dy_ref[pl.ds(r0, BLK), :], _halo(dy_ref, nxt, jnp.logical_not(last))], axis=0)
            z = _conv_taps(xext, w, BLK + 8) + bias
            sg = jax.nn.sigmoid(z)
            dz = dyext * (sg * (1.0 + z * (1.0 - sg)))
            dx = None
            for j in range(CONV_K):
                sh = CONV_K - 1 - j
                dzs = pltpu.roll(dz, BLK + 8 - sh, 0) if sh else dz
                term = w[j:j + 1, :] * dzs[:BLK, :]
                dx = term if dx is None else dx + term
            dx_ref[pl.ds(r0, BLK), :] = dx
            dzm = dz[:BLK, :]
            out = []
            for j in range(CONV_K):
                sh = CONV_K - 1 - j
                xs = pltpu.roll(xext, sh, 0) if sh else xext
                out.append(carry[j] + jnp.sum(dzm * xs[8:8 + BLK, :], axis=0, keepdims=True))
            out.append(carry[CONV_K] + jnp.sum(dzm, axis=0, keepdims=True))
            return tuple(out)

        zero = jnp.zeros((1, CONV_W), F32)
        acc = lax.fori_loop(0, nb, step, (zero,) * (CONV_K + 1))
        dw_ref[...] = jnp.concatenate(acc[:CONV_K], axis=0)
        db_ref[...] = acc[CONV_K]

    strip = pl.BlockSpec((seq, CONV_W), lambda c: (0, c))
    wsp = pl.BlockSpec((CONV_K, CONV_W), lambda c: (0, c))
    bsp = pl.BlockSpec((1, CONV_W), lambda c: (0, c))
    return pl.pallas_call(
        body, name="conv_bwd", grid=(ch // CONV_W,),
        in_specs=[strip, wsp, bsp, strip],
        out_specs=[strip, wsp, bsp],
        out_shape=[jax.ShapeDtypeStruct(x.shape, F32), jax.ShapeDtypeStruct(w.shape, F32),
                   jax.ShapeDtypeStruct(b.shape, F32)],
        compiler_params=_cp(("parallel",)),
    )(x, w, b, dy)


@jax.custom_vjp
def conv_silu(x, w, b):
    return _conv_fwd_call(x, w, b)


def _conv_silu_fwd(x, w, b):
    return _conv_fwd_call(x, w, b), (x, w, b)


def _conv_silu_bwd(res, dy):
    return tuple(_conv_bwd_call(*res, dy))


conv_silu.defvjp(_conv_silu_fwd, _conv_silu_bwd)


def _loss_call(h, wf, target):
    seq, d = h.shape
    nb = seq // BLK

    def body(h_ref, w_ref, t_ref, loss_ref, dh_ref, dw_ref):
        i = pl.program_id(0)
        live = (i > 0).astype(F32)
        tgt = t_ref[...]

        def fn(hh, ww):
            err = _rms_fn(hh, ww) - tgt
            return 0.5 * live * jnp.sum(jnp.mean(err * err, axis=-1, keepdims=True), axis=0, keepdims=True)

        val, vjp = jax.vjp(fn, h_ref[...], w_ref[...])
        dh, dw = vjp(jnp.ones((1, 1), F32))
        dh_ref[...] = dh

        @pl.when(i == 0)
        def _():
            loss_ref[...] = jnp.zeros_like(loss_ref)
            dw_ref[...] = jnp.zeros_like(dw_ref)

        loss_ref[...] += val
        dw_ref[...] += dw

    return pl.pallas_call(
        body, name="loss_head", grid=(nb,),
        in_specs=[pl.BlockSpec((BLK, d), lambda i: (i, 0)), pl.BlockSpec((1, d), lambda i: (0, 0)),
                  pl.BlockSpec((BLK, d), lambda i: (jnp.maximum(i - 1, 0), 0))],
        out_specs=[pl.BlockSpec((1, 1), lambda i: (0, 0)), pl.BlockSpec((BLK, d), lambda i: (i, 0)),
                   pl.BlockSpec((1, d), lambda i: (0, 0))],
        out_shape=[jax.ShapeDtypeStruct((1, 1), F32), jax.ShapeDtypeStruct(h.shape, F32),
                   jax.ShapeDtypeStruct((1, d), F32)],
        compiler_params=_cp(("arbitrary",)),
    )(h, wf, target)


def _make_loss_head(target):
    @jax.custom_vjp
    def head(h, wf):
        return _loss_call(h, wf, target)[0][0, 0]

    def fwd(h, wf):
        loss, dh, dw = _loss_call(h, wf, target)
        return loss[0, 0], (dh, dw)

    def bwd(res, g):
        return g * res[0], g * res[1]

    head.defvjp(fwd, bwd)
    return head


_IN_SEGS = (("q", 0, 1024), ("k", 1024, 1024), ("v", 2048, 1024), ("gate", 3072, 1024), ("z", 4112, 1024),
            ("xbc", 5136, 2048), ("cq", 7200, 1024), ("ck", 8224, 256), ("cv", 8480, 256), ("gl", 8736, 3072),
            ("b", 4096, 8), ("a", 4104, 8), ("dt", 7184, 16))
_IN_PAD = sum(_SPLIT) - sum(n for _, _, n in _IN_SEGS)


_MATMUL = ("w_in", "w_proj_gdn", "w_proj_ssd", "w_proj_swa", "w_out", "w_up", "w_down")
_LATE = _MATMUL[1:]


def _late_weights(gathered):
    g = dict(zip(_LATE, gathered))
    full = {n: g[n].reshape(D_MODEL, D_MODEL) for n in _LATE[:4]}
    full["w_up"] = g["w_up"].transpose(1, 0, 2).reshape(D_MODEL, D_FF)
    full["w_down"] = g["w_down"].reshape(D_FF, D_MODEL)
    full.update({n + "_t": t.T for n, t in list(full.items())})
    return full


def _layer(h, p, w_in, w_in_t, slot, late_shards, next_shards=(), exchange_slots=()):
    wb = {"w_in": w_in, "w_in_t": w_in_t}

    def proj(t, name):
        return mm(t, wb[name], wb[name + "_t"], slot[name])

    q_pre, k_pre, v_pre, gate, z, xbc_pre, cq, ckv, gl, small = mm_split(
        rms_op(h, p["norm1_w"].reshape(1, -1)), w_in, w_in_t, slot["w_in"])

    gcw = p["gdn_conv_w"]
    nob = jnp.zeros((1, GDN_H * GDN_D), F32)
    qa = conv_silu(q_pre, gcw[:, :1024], nob)
    ka = conv_silu(k_pre, gcw[:, 1024:2048], nob)
    va = conv_silu(v_pre, gcw[:, 2048:], nob)
    y_gdn, late, placeholders = gdn_core(
        qa, ka, va, gate, small, p["gdn_a_log"].reshape(GDN_H, 1, 1), p["gdn_dt_bias"].reshape(GDN_H, 1, 1),
        p["gdn_norm_w"].reshape(1, GDN_D), tuple(late_shards), tuple(exchange_slots))
    wb.update(_late_weights(late))

    xbc = conv_silu(xbc_pre, p["ssd_conv_w"], p["ssd_conv_b"].reshape(1, -1))
    y_ssd, gathered = ssd_core(xbc, z, small, p["ssd_dt_bias"].reshape(SSD_H, 1, 1),
                               p["ssd_a_log"].reshape(SSD_H, 1, 1), p["ssd_d"].reshape(SSD_H, 1, 1),
                               p["ssd_norm_w"].reshape(SSD_H, 1, SSD_P), tuple(next_shards))

    y_swa = swa_core(cq, ckv, p["swa_sinks"].reshape(SWA_QH, 1, 1))

    merged = merge_op(proj(y_gdn, "w_proj_gdn"), proj(y_ssd, "w_proj_ssd"), proj(y_swa, "w_proj_swa"), gl)
    h = h + proj(merged, "w_out")
    a1 = proj(rms_op(h, p["norm2_w"].reshape(1, -1)), "w_up")
    return h + proj(relu2_op(a1), "w_down"), gathered, placeholders


_PER_LAYER = ("norm1_w", "gdn_conv_w", "gdn_a_log", "gdn_dt_bias", "gdn_norm_w", "ssd_conv_w", "ssd_conv_b",
              "ssd_dt_bias", "ssd_a_log", "ssd_d", "ssd_norm_w", "swa_sinks", "norm2_w")


def _embed(x, meta):
    return jnp.concatenate([jnp.zeros((NPAD, D_MODEL), F32), meta, x], axis=0)


_IN_SHARD = 1476


def _in_pieces():
    out = []
    for _, s, n in _IN_SEGS:
        c = s
        while c < s + n:
            d = c // _IN_SHARD
            e = min(s + n, (d + 1) * _IN_SHARD)
            out.append((d, c - d * _IN_SHARD, e - d * _IN_SHARD))
            c = e
    return out


def _in_pieces_back():
    start, off = {}, 0
    for _, s, n in _IN_SEGS:
        start[s] = off
        off += n
    out = [[] for _ in range(N_DEV)]
    for _, s, n in sorted(_IN_SEGS, key=lambda t: t[1]):
        c = s
        while c < s + n:
            d = c // _IN_SHARD
            e = min(s + n, (d + 1) * _IN_SHARD)
            out[d].append((start[s] + c - s, start[s] + e - s))
            c = e
    return out


def _regroup_w_in(stacked):
    parts = [stacked[d, :, lo:hi] for d, lo, hi in _in_pieces()]
    return jnp.concatenate(parts + [jnp.zeros((D_MODEL, _IN_PAD), stacked.dtype)], axis=1)


def _ungroup_w_in(g):
    return [jnp.concatenate([g[:, lo:hi] for lo, hi in pieces], axis=1) for pieces in _in_pieces_back()]


def _position():
    return lax.axis_index("x"), lax.axis_index("y"), lax.axis_index("c")


_ANY = pl.BlockSpec(memory_space=pl.ANY)


def _chip_of(x, y, k):
    return (1 - x if k & 1 else x, 1 - y if k & 2 else y)


def _allgather_call(shards, name):
    n = len(shards)

    def body(*refs):
        start, relay, finish = _gather_phases(refs[:n], refs[n:2 * n], *refs[2 * n:])
        start()
        relay()
        finish()

    return pl.pallas_call(
        body, name=name,
        out_shape=_gather_out_shapes(shards),
        in_specs=[_ANY] * n, out_specs=[_ANY] * n,
        scratch_shapes=_gather_sems(n),
    )(*shards)


def _gather_out_shapes(shards):
    return [jax.ShapeDtypeStruct((N_DEV, *s.shape), s.dtype) for s in shards]


def _gather_sems(n):
    return [pltpu.SemaphoreType.DMA((7 * n,)), pltpu.SemaphoreType.DMA((7 * n,)), pltpu.SemaphoreType.DMA((n,))]


def _gather_phases(x_refs, out_refs, send_sems, recv_sems, local_sems):
    n = len(x_refs)
    x, y, c = _position()
    me, sibling = (x, y, c), (x, y, 1 - c)
    chips = [_chip_of(x, y, k) for k in (1, 2, 3)]

    def slab(a, px, py, pc):
        return out_refs[a].at[4 * px + 2 * py + pc]

    def copy(a, k, block, to, src=None):
        return pltpu.make_async_remote_copy(
            src_ref=slab(a, *block) if src is None else src, dst_ref=slab(a, *block),
            send_sem=send_sems.at[7 * a + k], recv_sem=recv_sems.at[7 * a + k], device_id=to, device_id_type=MESH)

    def mine():
        return [pltpu.make_async_copy(x_refs[a], slab(a, *me), local_sems.at[a]) for a in range(n)]

    def first():
        out = []
        for a in range(n):
            out.append(copy(a, 0, me, sibling, src=x_refs[a]))
            out += [copy(a, 1 + j, me, (*chip, c), src=x_refs[a]) for j, chip in enumerate(chips)]
        return out

    def passed():
        return [copy(a, 4 + j, (*chip, c), sibling) for j, chip in enumerate(chips) for a in range(n)]

    def start():
        for cp in mine() + first():
            cp.start()

    def relay():
        for j, chip in enumerate(chips):
            for a in range(n):
                copy(a, 1 + j, (*chip, c), me).wait_recv()
                copy(a, 4 + j, (*chip, c), sibling).start()

    def finish():
        for a in range(n):
            copy(a, 0, sibling, me).wait_recv()
        for j, chip in enumerate(chips):
            for a in range(n):
                copy(a, 4 + j, (*chip, 1 - c), me).wait_recv()
        for cp in first() + passed():
            cp.wait_send()
        for cp in mine():
            cp.wait()

    return start, relay, finish


def _sibling_exchange_call(for_c0, for_c1, name):
    n = len(for_c0)

    def body(*refs):
        c0_refs, c1_refs, out_refs = refs[:4 * n], refs[4 * n:8 * n], refs[8 * n:9 * n]
        send_sems, recv_sems = refs[9 * n:]
        x, y, c = _position()

        def copies(src_refs):
            return [pltpu.make_async_remote_copy(
                src_ref=src_refs[4 * a + q], dst_ref=out_refs[a].at[q],
                send_sem=send_sems.at[4 * a + q], recv_sem=recv_sems.at[4 * a + q],
                device_id=(x, y, 1 - c), device_id_type=MESH) for a in range(n) for q in range(4)]

        @pl.when(c == 0)
        def _():
            for cp in copies(c1_refs):
                cp.start()

        @pl.when(c == 1)
        def _():
            for cp in copies(c0_refs):
                cp.start()

        waits = copies(c0_refs)
        for cp in waits:
            cp.wait_recv()
        for cp in waits:
            cp.wait_send()

    return pl.pallas_call(
        body, name=name,
        out_shape=[jax.ShapeDtypeStruct((4, *g[0].shape), g[0].dtype) for g in for_c0],
        in_specs=[_ANY] * (8 * n), out_specs=[_ANY] * n,
        scratch_shapes=[pltpu.SemaphoreType.DMA((4 * n,)), pltpu.SemaphoreType.DMA((4 * n,))],
    )(*[t for g in for_c0 for t in g], *[t for g in for_c1 for t in g])


def _chip_exchange_call(partials, name):
    n = len(partials)

    def body(*refs):
        start, finish = _chip_exchange_phases(refs[:n], refs[n:2 * n], *refs[2 * n:])
        start()
        finish()

    return pl.pallas_call(
        body, name=name,
        out_shape=_chip_exchange_out_shapes(partials),
        in_specs=[_ANY] * n, out_specs=[_ANY] * n,
        scratch_shapes=_chip_exchange_sems(n),
    )(*partials)


def _chip_exchange_out_shapes(partials):
    return [jax.ShapeDtypeStruct((3, *p.shape[1:]), p.dtype) for p in partials]


def _chip_exchange_sems(n):
    return [pltpu.SemaphoreType.DMA((3 * n,)), pltpu.SemaphoreType.DMA((3 * n,))]


def _chip_exchange_phases(p_refs, out_refs, send_sems, recv_sems):
    n = len(p_refs)
    x, y, c = _position()

    def copies():
        out = []
        for a in range(n):
            for k in (1, 2, 3):
                px, py = _chip_of(x, y, k)
                out.append(pltpu.make_async_remote_copy(
                    src_ref=p_refs[a].at[2 * px + py], dst_ref=out_refs[a].at[k - 1],
                    send_sem=send_sems.at[3 * a + k - 1], recv_sem=recv_sems.at[3 * a + k - 1],
                    device_id=(px, py, c), device_id_type=MESH))
        return out

    def start():
        for cp in copies():
            cp.start()

    def finish():
        for cp in copies():
            cp.wait_recv()
        for cp in copies():
            cp.wait_send()

    return start, finish


def _chip_partial_call(for_c0, for_c1, sib, tr, name):
    _, r, c = sib.shape

    def body(*refs):
        c0_refs, c1_refs, (s_ref, own_ref, out_ref) = refs[:4], refs[4:8], refs[8:]
        x, y, core = _position()
        own = jnp.zeros((tr, c), F32)
        for q in range(4):
            partial = jnp.where(core == 0, c0_refs[q][...], c1_refs[q][...]) + s_ref[q]
            own = jnp.where(2 * x + y == q, partial, own)
            out_ref[q] = partial.astype(BF16)
        own_ref[...] = own

    one = pl.BlockSpec((tr, c), lambda i: (i, 0))
    four = pl.BlockSpec((4, tr, c), lambda i: (0, i, 0))
    return pl.pallas_call(
        body, name=name, grid=(r // tr,),
        in_specs=[one] * 8 + [four],
        out_specs=[one, four],
        out_shape=[jax.ShapeDtypeStruct((r, c), F32), jax.ShapeDtypeStruct((4, r, c), BF16)],
        compiler_params=_cp(("parallel",)),
    )(*for_c0, *for_c1, sib)


def _adamw_call(parts, w, m, v, tr, name):
    ns, r, c = w.shape
    counts = [len(p) for p in parts]
    flat_parts = [a for p in parts for a in p]

    def body(*refs):
        p_refs = refs[:len(flat_parts)]
        w_ref, m_ref, v_ref, g_ref, d_ref, nm_ref, nv_ref = refs[len(flat_parts):]
        at = 0
        for s in range(ns):
            g = None
            for p_ref in p_refs[at:at + counts[s]]:
                for j in range(p_ref.shape[0]):
                    term = p_ref[j].astype(F32)
                    g = term if g is None else g + term
            at += counts[s]
            nm = ADAM_B1 * m_ref[s] + (1.0 - ADAM_B1) * g
            nv = ADAM_B2 * v_ref[s] + (1.0 - ADAM_B2) * (g * g)
            m_hat = nm / (1.0 - ADAM_B1 ** ADAM_STEP)
            v_hat = nv / (1.0 - ADAM_B2 ** ADAM_STEP)
            g_ref[s] = g
            d_ref[s] = -ADAM_LR * (m_hat / (jnp.sqrt(v_hat) + ADAM_EPS) + ADAM_WD * w_ref[s])
            nm_ref[s] = nm
            nv_ref[s] = nv

    slabs = pl.BlockSpec((ns, tr, c), lambda i: (0, i, 0))
    return pl.pallas_call(
        body, name=name, grid=(r // tr,),
        in_specs=[pl.BlockSpec((a.shape[0], tr, c), lambda i: (0, i, 0)) for a in flat_parts] + [slabs] * 3,
        out_specs=[slabs] * 4,
        out_shape=[jax.ShapeDtypeStruct((ns, r, c), F32)] * 4,
        compiler_params=_cp(("parallel",)),
    )(*flat_parts, w, m, v)


_WEIGHTS = ("meta_tokens", "norm1_w", "w_in", "gdn_conv_w", "gdn_a_log", "gdn_dt_bias", "gdn_norm_w", "ssd_conv_w",
            "ssd_conv_b", "ssd_dt_bias", "ssd_a_log", "ssd_d", "ssd_norm_w", "swa_sinks", "w_proj_gdn", "w_proj_ssd",
            "w_proj_swa", "w_out", "norm2_w", "w_up", "w_down", "final_norm_w")
_SHARD_AXIS = {"meta_tokens": 1, "w_in": 2, "gdn_conv_w": 2, "ssd_conv_w": 2, "w_proj_gdn": 1, "w_proj_ssd": 1,
               "w_proj_swa": 1, "w_out": 1, "w_up": 2, "w_down": 1}
_BIG = tuple(n for n in _WEIGHTS if n in _SHARD_AXIS)
_SMALL = tuple(n for n in _WEIGHTS if n not in _SHARD_AXIS)
FLAT_C = 1024


def _pack(arrs, rows, lead=()):
    flat = jnp.concatenate([a.reshape(*lead, -1) for a in arrs], axis=-1)
    pad = rows * FLAT_C - flat.shape[-1]
    flat = jnp.pad(flat, [(0, 0)] * len(lead) + [(0, pad)])
    return flat.reshape(*lead, rows, FLAT_C)


def _unpack(flat, shapes, lead=()):
    flat = flat.reshape(*lead, -1)
    out, off = [], 0
    for s in shapes:
        n = math.prod(s)
        out.append(flat[..., off:off + n].reshape(*lead, *s))
        off += n
    return out


def _rows_for(shapes):
    n = sum(math.prod(s) for s in shapes)
    return -(-n // (FLAT_C * 8)) * 8


def _rows_tile(r, c):
    if r <= 256:
        return r
    return 128 if c > 1024 else 256


def _join(stacked, axis):
    moved = jnp.moveaxis(stacked, 0, axis)
    return moved.reshape(*moved.shape[:axis], -1, *moved.shape[axis + 2:])


def _unjoin(full, axis):
    cut = full.reshape(*full.shape[:axis], N_DEV, full.shape[axis] // N_DEV, *full.shape[axis + 1:])
    return jnp.moveaxis(cut, axis, 0)


def kernel(x, meta_tokens, norm1_w, w_in, gdn_conv_w, gdn_a_log, gdn_dt_bias, gdn_norm_w, ssd_conv_w, ssd_conv_b,
           ssd_dt_bias, ssd_a_log, ssd_d, ssd_norm_w, swa_sinks, w_proj_gdn, w_proj_ssd, w_proj_swa, w_out, norm2_w,
           w_up, w_down, final_norm_w, loss_target, m_meta_tokens, m_norm1_w, m_w_in, m_gdn_conv_w, m_gdn_a_log,
           m_gdn_dt_bias, m_gdn_norm_w, m_ssd_conv_w, m_ssd_conv_b, m_ssd_dt_bias, m_ssd_a_log, m_ssd_d, m_ssd_norm_w,
           m_swa_sinks, m_w_proj_gdn, m_w_proj_ssd, m_w_proj_swa, m_w_out, m_norm2_w, m_w_up, m_w_down,
           m_final_norm_w, v_meta_tokens, v_norm1_w, v_w_in, v_gdn_conv_w, v_gdn_a_log, v_gdn_dt_bias, v_gdn_norm_w,
           v_ssd_conv_w, v_ssd_conv_b, v_ssd_dt_bias, v_ssd_a_log, v_ssd_d, v_ssd_norm_w, v_swa_sinks, v_w_proj_gdn,
           v_w_proj_ssd, v_w_proj_swa, v_w_out, v_norm2_w, v_w_up, v_w_down, v_final_norm_w):
    args = (meta_tokens, norm1_w, w_in, gdn_conv_w, gdn_a_log, gdn_dt_bias, gdn_norm_w, ssd_conv_w, ssd_conv_b,
            ssd_dt_bias, ssd_a_log, ssd_d, ssd_norm_w, swa_sinks, w_proj_gdn, w_proj_ssd, w_proj_swa, w_out, norm2_w,
            w_up, w_down, final_norm_w, m_meta_tokens, m_norm1_w, m_w_in, m_gdn_conv_w, m_gdn_a_log,
            m_gdn_dt_bias, m_gdn_norm_w, m_ssd_conv_w, m_ssd_conv_b, m_ssd_dt_bias, m_ssd_a_log, m_ssd_d, m_ssd_norm_w,
            m_swa_sinks, m_w_proj_gdn, m_w_proj_ssd, m_w_proj_swa, m_w_out, m_norm2_w, m_w_up, m_w_down,
            m_final_norm_w, v_meta_tokens, v_norm1_w, v_w_in, v_gdn_conv_w, v_gdn_a_log, v_gdn_dt_bias, v_gdn_norm_w,
            v_ssd_conv_w, v_ssd_conv_b, v_ssd_dt_bias, v_ssd_a_log, v_ssd_d, v_ssd_norm_w, v_swa_sinks, v_w_proj_gdn,
            v_w_proj_ssd, v_w_proj_swa, v_w_out, v_norm2_w, v_w_up, v_w_down, v_final_norm_w)
    nw = len(_WEIGHTS)
    w = dict(zip(_WEIGHTS, args[:nw]))
    m = dict(zip(_WEIGHTS, args[nw:2 * nw]))
    v = dict(zip(_WEIGHTS, args[2 * nw:]))

    depth = w["w_in"].shape[0]
    small_shapes = [w[n].shape for n in _SMALL]
    small_rows = _rows_for(small_shapes)

    def flat2(t):
        return t.reshape(-1, t.shape[-1])

    tiny_names = [n for n in _BIG if n not in _MATMUL]

    def shard(n, l):
        return w[n][l].astype(BF16)

    first = _allgather_call([shard("w_in", 0)] + [flat2(w[n]) for n in tiny_names], "gather_weights")
    w_in_stacked = first[0]
    joined = {n: _join(t.reshape(N_DEV, *w[n].shape), _SHARD_AXIS[n]) for n, t in zip(tiny_names, first[1:])}
    slot_shapes = {"w_in": (D_MODEL, sum(_SPLIT)), "w_up": (D_MODEL, D_FF), "w_down": (D_FF, D_MODEL)}
    slot_shapes.update({n: (D_MODEL, D_MODEL) for n in _LATE[:4]})

    def layer_fn(l, w_in_full, late_shards, next_shards):
        w_in_t = w_in_full.T
        if l == 0:
            def fn(x_rows, meta, p, slot, exchange_slots):
                out, g, placeholders = _layer(_embed(x_rows, meta), p, w_in_full, w_in_t, slot, late_shards,
                                              next_shards, exchange_slots)
                return (out, placeholders), g
        else:
            def fn(h_in, p, slot, exchange_slots):
                out, g, placeholders = _layer(h_in, p, w_in_full, w_in_t, slot, late_shards, next_shards,
                                              exchange_slots)
                return (out, placeholders), g
        return fn

    h, vjps = None, []
    for l in range(depth):
        slot = {n: jnp.zeros(s, F32) for n, s in slot_shapes.items()}
        p = {n: (joined[n][l] if n in joined else w[n][l]) for n in _PER_LAYER}
        more = l + 1 < depth
        next_shards = [shard("w_in", l + 1)] if more else []
        exchange_slots = tuple(jnp.zeros((3, *w[n][l + 1].shape), BF16) for n in _MATMUL) if more else ()
        lead = (x[0], joined["meta_tokens"]) if l == 0 else (h,)
        fn = layer_fn(l, _regroup_w_in(w_in_stacked), [shard(n, l) for n in _LATE], next_shards)
        (h, _), vjp, g_next = jax.vjp(fn, *lead, p, slot, exchange_slots, has_aux=True)
        if more:
            w_in_stacked = g_next[0]
        vjps.append(vjp)
    loss, head_vjp = jax.vjp(_make_loss_head(loss_target[0]), h, w["final_norm_w"].reshape(1, -1))
    dh, d_final = head_vjp(jnp.ones((), F32))
    loss = lax.psum(loss, ("x", "y", "c"))

    def by_core(name, g):
        if name == "w_in":
            shards = _ungroup_w_in(g)
            return shards[0::2], shards[1::2]
        if name == "w_up":
            t = g.reshape(D_MODEL, 4, 2, D_FF // N_DEV)
            return [t[:, q, 0] for q in range(4)], [t[:, q, 1] for q in range(4)]
        t = g.reshape(4, 2, -1, g.shape[-1])
        return [t[q, 0] for q in range(4)], [t[q, 1] for q in range(4)]

    own, incoming, layer_grads, outgoing = {}, {}, [None] * depth, ()
    for l in reversed(range(depth)):
        if l == 0:
            gx, d_meta, dp, dslot, arrived = vjps[0]((dh, tuple(outgoing)))
        else:
            dh, dp, dslot, arrived = vjps[l]((dh, tuple(outgoing)))
        incoming.update({(n, l + 1): t for n, t in zip(_MATMUL, arrived)})
        layer_grads[l] = dp
        todo = [((n, l), dslot[n]) for n in _MATMUL]
        if l == 0:
            full_grads = {"meta_tokens": d_meta}
            full_grads.update({n: jnp.stack([layer_grads[k][n] for k in range(depth)]) for n in tiny_names[1:]})
            todo += [((n, None), _unjoin(full_grads[n], _SHARD_AXIS[n]).reshape(N_DEV, -1, w[n].shape[-1]))
                     for n in tiny_names]
        pairs = [by_core(u[0], g) for u, g in todo]
        from_sibling = _sibling_exchange_call([a for a, _ in pairs], [b for _, b in pairs], "grads_to_sibling_%d" % l)
        outgoing = []
        for (u, _), (a0, a1), s in zip(todo, pairs, from_sibling):
            own[u], part = _chip_partial_call(a0, a1, s, _rows_tile(s.shape[1], s.shape[2]), "chip_partial_" + u[0])
            outgoing.append(part)
        if l == 0:
            incoming.update(zip([u for u, _ in todo], _chip_exchange_call(outgoing, "grads_to_chips")))

    g_small = {n: jnp.stack([layer_grads[k][n] for k in range(depth)]) for n in _SMALL if n != "final_norm_w"}
    g_small["final_norm_w"] = d_final.reshape(-1)

    by_name = {}
    for n in _BIG:
        layers = list(range(depth)) if n in _MATMUL else [None]
        parts = [[own[n, l][None], incoming[n, l]] for l in layers]
        r, c = own[n, layers[0]].shape
        stacked = [d[n].reshape(len(layers), r, c) for d in (w, m, v)]
        res = _adamw_call(parts, *stacked, _rows_tile(r, c), "adamw_" + n)
        by_name[n] = [t.reshape(w[n].shape) for t in res]

    small_parts = _allgather_call([_pack([g_small[n] for n in _SMALL], small_rows)], "gather_small_grads")
    small_out = _adamw_call([small_parts], *[_pack([d[n] for n in _SMALL], small_rows)[None] for d in (w, m, v)],
                            small_rows, "adamw_replicated")
    for kind in range(4):
        for n, t in zip(_SMALL, _unpack(small_out[kind][0], small_shapes)):
            by_name.setdefault(n, [None] * 4)[kind] = t

    outs = [by_name[n][kind] for kind in range(4) for n in _WEIGHTS]
    return (loss, gx[None], *outs)
```

```python
import functools
import math

import jax
import jax.numpy as jnp
from jax import lax
from jax.experimental import pallas as pl
from jax.experimental.pallas import tpu as pltpu

F32 = jnp.float32
BF16 = jnp.bfloat16
HI = lax.Precision.HIGH
NEG = -1e30

D_MODEL = 1024
N_META = 16
BLK = 128
NPAD = BLK - N_META
RMS_EPS = 1e-6
L2_EPS = 1e-6
CONV_K = 4
CONV_W = 256

GDN_H, GDN_D, GDN_C = 8, 128, 64
SSD_H, SSD_P, SSD_G, SSD_N = 16, 64, 4, 128
SSD_HPG = SSD_H // SSD_G
SWA_QH, SWA_KVH, SWA_D = 16, 4, 64
SWA_REP = SWA_QH // SWA_KVH
D_FF = 4 * D_MODEL

N_DEV = 8
MESH = pl.DeviceIdType.MESH

ADAM_LR, ADAM_B1, ADAM_B2, ADAM_EPS, ADAM_WD, ADAM_STEP = 0.001, 0.9, 0.999, 1e-08, 0.01, 10

VMEM_LIMIT = 56 * 1024 * 1024


def _cp(sem=None):
    return pltpu.CompilerParams(dimension_semantics=sem, vmem_limit_bytes=VMEM_LIMIT)


def _dot(a, b, ca, cb, prec=HI):
    return lax.dot_general(a, b, (((ca,), (cb,)), ((), ())), precision=prec, preferred_element_type=F32)


def _nn(a, b, prec=HI):
    return _dot(a, b, 1, 0, prec)


def _nt(a, b, prec=HI):
    return _dot(a, b, 1, 1, prec)


def _tn(a, b, prec=HI):
    return _dot(a, b, 0, 0, prec)


def _bdot(a, b, ca, cb):
    return lax.dot_general(a.astype(BF16), b.astype(BF16), (((ca,), (cb,)), ((), ())), preferred_element_type=F32)


@jax.custom_vjp
def _lo_nn(a, b):
    return _bdot(a, b, 1, 0)


_lo_nn.defvjp(lambda a, b: (_bdot(a, b, 1, 0), (a, b)),
              lambda r, d: (_bdot(d, r[1], 1, 1), _bdot(r[0], d, 0, 0)))


@jax.custom_vjp
def _lo_nt(a, b):
    return _bdot(a, b, 1, 1)


_lo_nt.defvjp(lambda a, b: (_bdot(a, b, 1, 1), (a, b)),
              lambda r, d: (_bdot(d, r[1], 1, 0), _bdot(d, r[0], 0, 0)))


@jax.custom_vjp
def _lo_tn(a, b):
    return _bdot(a, b, 0, 0)


_lo_tn.defvjp(lambda a, b: (_bdot(a, b, 0, 0), (a, b)),
              lambda r, d: (_bdot(r[1], d, 1, 1), _bdot(r[0], d, 1, 0)))


def _iota2(n, m, axis):
    return lax.broadcasted_iota(jnp.int32, (n, m), axis)


def _silu(x):
    return x * jax.nn.sigmoid(x)


def _softplus(x):
    return jnp.maximum(x, 0.0) + jnp.log(1.0 + jnp.exp(-jnp.abs(x)))


def _row_of(col):
    n = col.shape[0]
    return jnp.broadcast_to(col, (n, n)).T


def _cumsum_col(col):
    n = col.shape[0]
    tril = (_iota2(n, n, 0) >= _iota2(n, n, 1)).astype(F32)
    return _nn(tril, col)


def _tri_inv(a):
    n = a.shape[0]
    r, c = _iota2(n, n, 0), _iota2(n, n, 1)
    eye = (r == c).astype(F32)
    blk = jnp.right_shift(r, 4) == jnp.right_shift(c, 4)
    d = jnp.where(blk, a, 0.0)
    off = a - d
    d2 = _nn(d, d)
    d4 = _nn(d2, d2)
    d8 = _nn(d4, d4)
    td = _nn(_nn(_nn(eye - d, eye + d2), eye + d4), eye + d8)
    m = _nn(td, off)
    m2 = _nn(m, m)
    return _nn(_nn(eye - m, eye + m2), td)


@jax.custom_vjp
def _tri_solve(a, inv, rhs):
    return _nn(inv, rhs)


def _tri_solve_fwd(a, inv, rhs):
    sol = _nn(inv, rhs)
    return sol, (inv, sol)


def _tri_solve_bwd(res, dsol):
    inv, sol = res
    drhs = _nn(inv.T, dsol)
    return -_nt(drhs, sol), jnp.zeros_like(inv), drhs


_tri_solve.defvjp(_tri_solve_fwd, _tri_solve_bwd)


def _gdn_chunk(qa, ka, va, gate, a_raw, b_raw, s, a_log, dt_bias, norm_w, valid, inv=None, want_inv=False):
    c = qa.shape[0]
    q = qa * lax.rsqrt(jnp.sum(qa * qa, axis=-1, keepdims=True) + L2_EPS) * (GDN_D ** -0.5)
    k = ka * lax.rsqrt(jnp.sum(ka * ka, axis=-1, keepdims=True) + L2_EPS)
    beta = jax.nn.sigmoid(b_raw)
    g = -jnp.exp(a_log) * _softplus(a_raw + dt_bias) * valid
    gam = _cumsum_col(g)
    gam_row = _row_of(gam)
    r, cc = _iota2(c, c, 0), _iota2(c, c, 1)
    decay = jnp.exp(jnp.where(r >= cc, gam - gam_row, NEG))
    kb = k * beta
    a = jnp.where(r > cc, _lo_nt(kb, k) * decay, 0.0)
    egam = jnp.exp(gam)
    if inv is None:
        inv = _tri_inv(lax.stop_gradient(a))
    sol = _tri_solve(a, inv, jnp.concatenate([va * beta, kb * egam], axis=1))
    u = sol[:, :GDN_D]
    w = sol[:, GDN_D:]
    attn = _lo_nt(q, k) * decay
    g_last = jnp.sum(g, axis=0, keepdims=True)
    k_tail = k * jnp.exp(g_last - gam)
    v_new = u - _lo_nn(w, s)
    o = _lo_nn(q * egam, s) + _lo_nn(attn, v_new)
    s_new = s * jnp.exp(g_last) + _lo_tn(k_tail, v_new)
    y = o * lax.rsqrt(jnp.mean(o * o, axis=-1, keepdims=True) + RMS_EPS) * norm_w * _silu(gate)
    return (y, s_new, inv) if want_inv else (y, s_new)


def _valid_col(row0, n):
    return (row0 + _iota2(n, 1, 0) >= NPAD).astype(F32)


GDN_HB = GDN_H

SM_B, SM_A, SM_DT, SM_W = 0, 8, 16, 128


def _pick_cols(sm, first, n):
    return jnp.stack([sm[:, first + j:first + j + 1] for j in range(n)])


def _spread_cols(cols, first):
    lane = _iota2(1, SM_W, 1)
    out = None
    for j in range(cols.shape[0]):
        term = cols[j] * (lane == first + j).astype(F32)
        out = term if out is None else out + term
    return out


def _widen(t, width):
    if width == t.shape[1]:
        return t
    return jnp.concatenate([t, jnp.zeros((t.shape[0], width - t.shape[1]), t.dtype)], axis=1)


def _gdn_specs(nc, rev):
    ci = (lambda i: nc - 1 - i) if rev else (lambda i: i)
    hb = GDN_HB
    tile = pl.BlockSpec((GDN_C, hb * GDN_D), lambda h, i: (ci(i), h))
    col = pl.BlockSpec((GDN_C, SM_W), lambda h, i: (ci(i), 0))
    scal = pl.BlockSpec((hb, 1, 1), lambda h, i: (h, 0, 0))
    nw = pl.BlockSpec((1, GDN_D), lambda h, i: (0, 0))
    st = pl.BlockSpec((hb, 1, GDN_D, GDN_D), lambda h, i: (h, ci(i), 0, 0))
    return tile, col, scal, nw, st


def _lanes(j):
    return slice(j * GDN_D, (j + 1) * GDN_D)


def _by_head(ref):
    return jnp.stack([ref[:, _lanes(j)] for j in range(GDN_HB)])


def _gdn_fwd_call(q, k, v, gate, small, a_log, dt_bias, norm_w, shards=()):
    seq = q.shape[0]
    nc = seq // GDN_C
    ns = len(shards)
    tile, col, scal, nw, st = _gdn_specs(nc, False)

    def body(*refs):
        q_ref, k_ref, v_ref, g_ref, sm_ref, al_ref, dt_ref, nw_ref = refs[:8]
        y_ref, st_ref, inv_ref = refs[8 + ns:11 + ns]
        s_scr = refs[11 + 2 * ns]
        i = pl.program_id(1)
        if ns:
            start, relay, finish = _gather_phases(refs[8:8 + ns], refs[11 + ns:11 + 2 * ns], *refs[12 + 2 * ns:])
            pl.when(i == 0)(start)
            pl.when(i == nc - 1)(relay)

        @pl.when(i == 0)
        def _():
            s_scr[...] = jnp.zeros_like(s_scr)

        s = s_scr[...]
        st_ref[:, 0] = s
        sm = sm_ref[...]
        fn = jax.vmap(functools.partial(_gdn_chunk, valid=_valid_col(i * GDN_C, GDN_C), want_inv=True))
        y, s_new, inv = fn(_by_head(q_ref), _by_head(k_ref), _by_head(v_ref), _by_head(g_ref),
                           _pick_cols(sm, SM_A, GDN_H), _pick_cols(sm, SM_B, GDN_H), s,
                           al_ref[...], dt_ref[...], jnp.broadcast_to(nw_ref[...], (GDN_HB, 1, GDN_D)))
        for j in range(GDN_HB):
            y_ref[:, _lanes(j)] = y[j]
        inv_ref[:, 0] = inv
        s_scr[...] = s_new
        if ns:
            pl.when(i == nc - 1)(finish)

    return pl.pallas_call(
        body, name="gdn_fwd", grid=(GDN_H // GDN_HB, nc),
        in_specs=[tile, tile, tile, tile, col, scal, scal, nw] + [_ANY] * ns,
        out_specs=[tile, st, pl.BlockSpec((GDN_HB, 1, GDN_C, GDN_C), lambda h, i: (h, i, 0, 0))] + [_ANY] * ns,
        out_shape=[jax.ShapeDtypeStruct((seq, GDN_H * GDN_D), F32),
                   jax.ShapeDtypeStruct((GDN_H, nc, GDN_D, GDN_D), F32),
                   jax.ShapeDtypeStruct((GDN_H, nc, GDN_C, GDN_C), F32)] + _gather_out_shapes(shards),
        scratch_shapes=[pltpu.VMEM((GDN_HB, GDN_D, GDN_D), F32)] + (_gather_sems(ns) if ns else []),
        compiler_params=_cp(("parallel", "arbitrary")),
    )(q, k, v, gate, small, a_log, dt_bias, norm_w, *shards)


def _gdn_bwd_call(q, k, v, gate, small, a_log, dt_bias, norm_w, states, invs, dy, outgoing=()):
    seq = q.shape[0]
    nc = seq // GDN_C
    no = len(outgoing)
    tile, col, scal, nw, st = _gdn_specs(nc, True)
    nwh = pl.BlockSpec((GDN_HB, 1, GDN_D), lambda h, i: (h, 0, 0))
    inv_spec = pl.BlockSpec((GDN_HB, 1, GDN_C, GDN_C), lambda h, i: (h, nc - 1 - i, 0, 0))

    def body(*refs):
        q_ref, k_ref, v_ref, g_ref, sm_ref, al_ref, dt_ref, nw_ref, st_ref, inv_ref, dy_ref = refs[:11]
        dq_ref, dk_ref, dv_ref, dg_ref, dsm_ref, dal_ref, ddt_ref, dnw_ref = refs[11 + no:19 + no]
        ds_scr = refs[19 + 2 * no]
        i = pl.program_id(1)
        if no:
            start, finish = _chip_exchange_phases(refs[11:11 + no], refs[19 + no:19 + 2 * no], *refs[20 + 2 * no:])
            pl.when(i == 0)(start)

        @pl.when(i == 0)
        def _():
            ds_scr[...] = jnp.zeros_like(ds_scr)
            dal_ref[...] = jnp.zeros_like(dal_ref)
            ddt_ref[...] = jnp.zeros_like(ddt_ref)
            dnw_ref[...] = jnp.zeros_like(dnw_ref)

        sm = sm_ref[...]
        valid = _valid_col((nc - 1 - i) * GDN_C, GDN_C)
        kept = inv_ref[:, 0]

        def fn(*heads):
            return jax.vmap(lambda *t: _gdn_chunk(*t[:-1], valid=valid, inv=t[-1]))(*heads, kept)

        _, vjp = jax.vjp(fn, _by_head(q_ref), _by_head(k_ref), _by_head(v_ref), _by_head(g_ref),
                         _pick_cols(sm, SM_A, GDN_H), _pick_cols(sm, SM_B, GDN_H), st_ref[:, 0], al_ref[...],
                         dt_ref[...], jnp.broadcast_to(nw_ref[...], (GDN_HB, 1, GDN_D)))
        dq, dk, dv, dg, da, db, ds, dal, ddt, dnw = vjp((_by_head(dy_ref), ds_scr[...]))
        for j in range(GDN_HB):
            dq_ref[:, _lanes(j)] = dq[j]
            dk_ref[:, _lanes(j)] = dk[j]
            dv_ref[:, _lanes(j)] = dv[j]
            dg_ref[:, _lanes(j)] = dg[j]
        dsm_ref[...] = _widen(_spread_cols(da, SM_A) + _spread_cols(db, SM_B), dsm_ref.shape[1])
        ds_scr[...] = ds
        dal_ref[...] += dal
        ddt_ref[...] += ddt
        dnw_ref[...] += dnw
        if no:
            pl.when(i == nc - 1)(finish)

    big = jax.ShapeDtypeStruct((seq, GDN_H * GDN_D), F32)
    return pl.pallas_call(
        body, name="gdn_bwd", grid=(GDN_H // GDN_HB, nc),
        in_specs=[tile, tile, tile, tile, col, scal, scal, nw, st, inv_spec, tile] + [_ANY] * no,
        out_specs=[tile, tile, tile, tile, pl.BlockSpec((GDN_C, small.shape[1]), lambda h, i: (nc - 1 - i, 0)),
                   scal, scal, nwh] + [_ANY] * no,
        out_shape=[big, big, big, big, jax.ShapeDtypeStruct(small.shape, F32),
                   jax.ShapeDtypeStruct((GDN_H, 1, 1), F32), jax.ShapeDtypeStruct((GDN_H, 1, 1), F32),
                   jax.ShapeDtypeStruct((GDN_H, 1, GDN_D), F32)] + _chip_exchange_out_shapes(outgoing),
        scratch_shapes=[pltpu.VMEM((GDN_HB, GDN_D, GDN_D), F32)] + (_chip_exchange_sems(no) if no else []),
        compiler_params=_cp(("parallel", "arbitrary")),
    )(q, k, v, gate, small, a_log, dt_bias, norm_w, states, invs, dy, *outgoing)


@jax.custom_vjp
def gdn_core(q, k, v, gate, small, a_log, dt_bias, norm_w, shards, slots):
    y, _, _, *gathered = _gdn_fwd_call(q, k, v, gate, small, a_log, dt_bias, norm_w, shards)
    return y, tuple(gathered), tuple(jnp.zeros((4, *s.shape[1:]), s.dtype) for s in slots)


def _gdn_core_fwd(q, k, v, gate, small, a_log, dt_bias, norm_w, shards, slots):
    y, states, invs, *gathered = _gdn_fwd_call(q, k, v, gate, small, a_log, dt_bias, norm_w, shards)
    out = (y, tuple(gathered), tuple(jnp.zeros((4, *s.shape[1:]), s.dtype) for s in slots))
    return out, (q, k, v, gate, small, a_log, dt_bias, norm_w, states, invs, shards)


def _gdn_core_bwd(res, cts):
    *args, shards = res
    dy, _, outgoing = cts
    dq, dk, dv, dg, dsm, dal, ddt, dnw, *incoming = _gdn_bwd_call(*args, dy, outgoing)
    return (dq, dk, dv, dg, dsm, dal, ddt, jnp.sum(dnw, axis=0), tuple(jnp.zeros_like(s) for s in shards),
            tuple(incoming))


gdn_core.defvjp(_gdn_core_fwd, _gdn_core_bwd)


def _ssd_head(x, z, dt_raw, h, dt_bias, a_log, d_skip, bm, cm, cb, valid):
    c = bm.shape[0]
    r, cc = _iota2(c, c, 0), _iota2(c, c, 1)
    dtp = _softplus(dt_raw + dt_bias)
    x = x * valid
    adt = -jnp.exp(a_log) * dtp * valid
    xdt = x * dtp
    acum = _cumsum_col(adt)
    lmat = jnp.exp(jnp.where(r >= cc, acum - _row_of(acum), NEG))
    a_last = jnp.sum(adt, axis=0, keepdims=True)
    y = _lo_nn(cb * lmat, xdt) + _lo_nt(cm * jnp.exp(acum), h) + d_skip * x
    h_new = h * jnp.exp(a_last) + _lo_tn(xdt * jnp.exp(a_last - acum), bm)
    return y * _silu(z), h_new


SSD_SIDE = SSD_H


def _ssd_chunk(xs, z, bm, cm, dt_raw, h, dt_bias, a_log, d_skip, norm_w, valid):
    nh, c, p = xs.shape
    ng = bm.shape[0]
    hpg = nh // ng
    bm = bm * valid
    cm = cm * valid
    cb = jax.vmap(_lo_nt)(cm, bm)
    per_head = lambda t: jnp.repeat(t, hpg, axis=0)
    args = (xs, z, dt_raw, h, dt_bias, a_log, d_skip, per_head(bm), per_head(cm), per_head(cb))
    outs = [jax.vmap(functools.partial(_ssd_head, valid=valid))(*[t[s:s + SSD_SIDE] for t in args])
            for s in range(0, nh, SSD_SIDE)]
    ys = jnp.concatenate([o[0] for o in outs], axis=0)
    hs = jnp.concatenate([o[1] for o in outs], axis=0)
    ss = jnp.sum(jnp.sum(ys * ys, axis=-1, keepdims=True).reshape(ng, hpg, c, 1), axis=1, keepdims=True)
    rstd = lax.rsqrt(ss / (hpg * p) + RMS_EPS)
    return (ys.reshape(ng, hpg, c, p) * rstd).reshape(nh, c, p) * norm_w, hs


SSD_INNER = SSD_H * SSD_P
SSD_BC = SSD_G * SSD_N


def _split_lanes(t, n, w):
    return jnp.stack([t[:, j * w:(j + 1) * w] for j in range(n)])


def _join_lanes(t):
    return jnp.concatenate([t[j] for j in range(t.shape[0])], axis=1)


def _ssd_specs(nc, rev):
    ci = (lambda i: nc - 1 - i) if rev else (lambda i: i)
    wide = pl.BlockSpec((BLK, SSD_INNER), lambda i: (ci(i), 0))
    bmat = pl.BlockSpec((BLK, SSD_BC), lambda i: (ci(i), SSD_INNER // SSD_BC))
    cmat = pl.BlockSpec((BLK, SSD_BC), lambda i: (ci(i), SSD_INNER // SSD_BC + 1))
    xbc = pl.BlockSpec((BLK, SSD_INNER + 2 * SSD_BC), lambda i: (ci(i), 0))
    col = pl.BlockSpec((BLK, SM_W), lambda i: (ci(i), 0))
    scal = pl.BlockSpec((SSD_H, 1, 1), lambda i: (0, 0, 0))
    nw = pl.BlockSpec((SSD_H, 1, SSD_P), lambda i: (0, 0, 0))
    st = pl.BlockSpec((SSD_H, 1, SSD_P, SSD_N), lambda i: (0, ci(i), 0, 0))
    return wide, bmat, cmat, xbc, col, scal, nw, st


def _ssd_fwd_call(xbc, z, small, dt_bias, a_log, d_skip, norm_w, shards=()):
    seq = z.shape[0]
    nc = seq // BLK
    ns = len(shards)
    wide, bmat, cmat, _, col, scal, nw, st = _ssd_specs(nc, False)

    def body(*refs):
        x_ref, b_ref, c_ref, z_ref, sm_ref, db_ref, al_ref, ds_ref, nw_ref = refs[:9]
        y_ref, st_ref = refs[9 + ns:11 + ns]
        h_scr = refs[11 + 2 * ns]
        i = pl.program_id(0)
        if ns:
            start, relay, finish = _gather_phases(refs[9:9 + ns], refs[11 + ns:11 + 2 * ns], *refs[12 + 2 * ns:])
            pl.when(i == 0)(start)
            pl.when(i == nc - 1)(relay)

        @pl.when(i == 0)
        def _():
            h_scr[...] = jnp.zeros_like(h_scr)

        h = h_scr[...]
        st_ref[:, 0] = h
        y, h_new = _ssd_chunk(_split_lanes(x_ref[...], SSD_H, SSD_P), _split_lanes(z_ref[...], SSD_H, SSD_P),
                              _split_lanes(b_ref[...], SSD_G, SSD_N), _split_lanes(c_ref[...], SSD_G, SSD_N),
                              _pick_cols(sm_ref[...], SM_DT, SSD_H), h, db_ref[...], al_ref[...], ds_ref[...],
                              nw_ref[...], _valid_col(i * BLK, BLK))
        y_ref[...] = _join_lanes(y)
        h_scr[...] = h_new
        if ns:
            pl.when(i == nc - 1)(finish)

    return pl.pallas_call(
        body, name="ssd_fwd", grid=(nc,),
        in_specs=[wide, bmat, cmat, wide, col, scal, scal, scal, nw] + [_ANY] * ns,
        out_specs=[wide, st] + [_ANY] * ns,
        out_shape=[jax.ShapeDtypeStruct((seq, SSD_INNER), F32),
                   jax.ShapeDtypeStruct((SSD_H, nc, SSD_P, SSD_N), F32)] + _gather_out_shapes(shards),
        scratch_shapes=[pltpu.VMEM((SSD_H, SSD_P, SSD_N), F32)] + (_gather_sems(ns) if ns else []),
        compiler_params=_cp(("arbitrary",)),
    )(xbc, xbc, xbc, z, small, dt_bias, a_log, d_skip, norm_w, *shards)


def _ssd_bwd_call(xbc, z, small, dt_bias, a_log, d_skip, norm_w, states, dy):
    seq = z.shape[0]
    nc = seq // BLK
    wide, bmat, cmat, xbc_spec, col, scal, nw, st = _ssd_specs(nc, True)

    def body(x_ref, b_ref, c_ref, z_ref, sm_ref, db_ref, al_ref, ds_ref, nw_ref, st_ref, dy_ref,
             dxbc_ref, dz_ref, dsm_ref, ddb_ref, dal_ref, dds_ref, dnw_ref, dh_scr):
        i = pl.program_id(0)

        @pl.when(i == 0)
        def _():
            dh_scr[...] = jnp.zeros_like(dh_scr)
            ddb_ref[...] = jnp.zeros_like(ddb_ref)
            dal_ref[...] = jnp.zeros_like(dal_ref)
            dds_ref[...] = jnp.zeros_like(dds_ref)
            dnw_ref[...] = jnp.zeros_like(dnw_ref)

        fn = functools.partial(_ssd_chunk, valid=_valid_col((nc - 1 - i) * BLK, BLK))
        _, vjp = jax.vjp(fn, _split_lanes(x_ref[...], SSD_H, SSD_P), _split_lanes(z_ref[...], SSD_H, SSD_P),
                         _split_lanes(b_ref[...], SSD_G, SSD_N), _split_lanes(c_ref[...], SSD_G, SSD_N),
                         _pick_cols(sm_ref[...], SM_DT, SSD_H), st_ref[:, 0], db_ref[...], al_ref[...], ds_ref[...],
                         nw_ref[...])
        dx, dz, dbm, dcm, ddt, dh, ddb, dal, dds, dnw = vjp((_split_lanes(dy_ref[...], SSD_H, SSD_P), dh_scr[...]))
        dxbc_ref[:, :SSD_INNER] = _join_lanes(dx)
        dxbc_ref[:, SSD_INNER:SSD_INNER + SSD_BC] = _join_lanes(dbm)
        dxbc_ref[:, SSD_INNER + SSD_BC:] = _join_lanes(dcm)
        dz_ref[...] = _join_lanes(dz)
        dsm_ref[...] = _widen(_spread_cols(ddt, SM_DT), dsm_ref.shape[1])
        dh_scr[...] = dh
        ddb_ref[...] += ddb
        dal_ref[...] += dal
        dds_ref[...] += dds
        dnw_ref[...] += dnw

    sshape = jax.ShapeDtypeStruct((SSD_H, 1, 1), F32)
    return pl.pallas_call(
        body, name="ssd_bwd", grid=(nc,),
        in_specs=[wide, bmat, cmat, wide, col, scal, scal, scal, nw, st, wide],
        out_specs=[xbc_spec, wide, pl.BlockSpec((BLK, small.shape[1]), lambda i: (nc - 1 - i, 0)), scal, scal, scal, nw],
        out_shape=[jax.ShapeDtypeStruct(xbc.shape, F32), jax.ShapeDtypeStruct(z.shape, F32),
                   jax.ShapeDtypeStruct(small.shape, F32), sshape, sshape, sshape,
                   jax.ShapeDtypeStruct((SSD_H, 1, SSD_P), F32)],
        scratch_shapes=[pltpu.VMEM((SSD_H, SSD_P, SSD_N), F32)],
        compiler_params=_cp(("arbitrary",)),
    )(xbc, xbc, xbc, z, small, dt_bias, a_log, d_skip, norm_w, states, dy)


@jax.custom_vjp
def ssd_core(xbc, z, small, dt_bias, a_log, d_skip, norm_w, shards):
    y, _, *gathered = _ssd_fwd_call(xbc, z, small, dt_bias, a_log, d_skip, norm_w, shards)
    return y, tuple(gathered)


def _ssd_core_fwd(*args):
    y, states, *gathered = _ssd_fwd_call(*args)
    return (y, tuple(gathered)), (*args[:-1], states, args[-1])


def _ssd_core_bwd(res, cts):
    *args, shards = res
    return (*_ssd_bwd_call(*args, cts[0]), tuple(jnp.zeros_like(s) for s in shards))


ssd_core.defvjp(_ssd_core_fwd, _ssd_core_bwd)


def _swa_block(q, km, kp, kc, vm, vp, vc, sink, n):
    rows = SWA_REP * BLK
    qs = q.reshape(rows, SWA_D) * (SWA_D ** -0.5)
    s = _lo_nt(qs, jnp.concatenate([km, kp, kc], axis=0))
    i = jnp.bitwise_and(_iota2(rows, 3 * BLK, 0), BLK - 1)
    col = _iota2(rows, 3 * BLK, 1)
    j = jnp.bitwise_and(col, BLK - 1)
    part = jnp.right_shift(col, 7)
    ok_m = (part == 0) & (j >= NPAD) & ((n >= 1) | (j <= i))
    ok_p = (part == 1) & (n >= 2) & (j > i)
    ok_c = (part == 2) & (n >= 1) & (j <= i)
    ok = ok_m | ok_p | ok_c
    s = jnp.where(ok, s, NEG)
    snk = jnp.concatenate([jnp.broadcast_to(sink[r], (BLK, 1)) for r in range(SWA_REP)], axis=0)
    m = lax.stop_gradient(jnp.maximum(jnp.max(s, axis=-1, keepdims=True), snk))
    e = jnp.exp(s - m)
    p = e / (jnp.sum(e, axis=-1, keepdims=True) + jnp.exp(snk - m))
    o = _lo_nn(p, jnp.concatenate([vm, vp, vc], axis=0))
    return o.reshape(SWA_REP, BLK, SWA_D)


SWA_QW = SWA_QH * SWA_D
SWA_KW = SWA_KVH * SWA_D


def _swa_specs(nb, rev):
    ci = (lambda i: nb - 1 - i) if rev else (lambda i: i)
    qsp = pl.BlockSpec((BLK, SWA_QW), lambda i: (ci(i), 0))
    cur = pl.BlockSpec((BLK, 2 * SWA_KW), lambda i: (ci(i), 0))
    prev = pl.BlockSpec((BLK, 2 * SWA_KW), lambda i: (jnp.maximum(ci(i) - 1, 0), 0))
    meta = pl.BlockSpec((BLK, 2 * SWA_KW), lambda i: (0, 0))
    scal = pl.BlockSpec((SWA_QH, 1, 1), lambda i: (0, 0, 0))
    return qsp, cur, prev, meta, scal


def _swa_by_head(q, kvm, kvp, kvc, sink):
    def kv(t):
        return _split_lanes(t[:, :SWA_KW], SWA_KVH, SWA_D), _split_lanes(t[:, SWA_KW:], SWA_KVH, SWA_D)

    (km, vm), (kp, vp), (kc, vc) = kv(kvm), kv(kvp), kv(kvc)
    qh = _split_lanes(q, SWA_QH, SWA_D).reshape(SWA_KVH, SWA_REP, BLK, SWA_D)
    return qh, km, kp, kc, vm, vp, vc, sink.reshape(SWA_KVH, SWA_REP, 1, 1)


def _swa_kv_tile(dk, dv):
    return jnp.concatenate([_join_lanes(dk), _join_lanes(dv)], axis=1)


def _swa_fwd_call(q, kv, sink):
    seq = q.shape[0]
    nb = seq // BLK
    qsp, cur, prev, meta, scal = _swa_specs(nb, False)

    def body(q_ref, m_ref, p_ref, c_ref, s_ref, o_ref):
        fn = jax.vmap(functools.partial(_swa_block, n=pl.program_id(0)))
        o = fn(*_swa_by_head(q_ref[...], m_ref[...], p_ref[...], c_ref[...], s_ref[...]))
        o_ref[...] = _join_lanes(o.reshape(SWA_QH, BLK, SWA_D))

    return pl.pallas_call(
        body, name="swa_fwd", grid=(nb,),
        in_specs=[qsp, meta, prev, cur, scal],
        out_specs=qsp,
        out_shape=jax.ShapeDtypeStruct(q.shape, F32),
        compiler_params=_cp(("parallel",)),
    )(q, kv, kv, kv, sink)


def _swa_bwd_call(q, kv, sink, do):
    seq = q.shape[0]
    nb = seq // BLK
    qsp, cur, prev, meta, scal = _swa_specs(nb, True)

    def body(q_ref, m_ref, p_ref, c_ref, s_ref, do_ref, dq_ref, dkv_ref, ds_ref, prev_scr, meta_scr):
        i = pl.program_id(0)
        n = nb - 1 - i

        @pl.when(i == 0)
        def _():
            prev_scr[...] = jnp.zeros_like(prev_scr)
            meta_scr[...] = jnp.zeros_like(meta_scr)
            ds_ref[...] = jnp.zeros_like(ds_ref)

        fn = jax.vmap(functools.partial(_swa_block, n=n))
        _, vjp = jax.vjp(fn, *_swa_by_head(q_ref[...], m_ref[...], p_ref[...], c_ref[...], s_ref[...]))
        do = _split_lanes(do_ref[...], SWA_QH, SWA_D).reshape(SWA_KVH, SWA_REP, BLK, SWA_D)
        dq, dkm, dkp, dkc, dvm, dvp, dvc, dsk = vjp(do)
        dq_ref[...] = _join_lanes(dq.reshape(SWA_QH, BLK, SWA_D))
        ds_ref[...] += dsk.reshape(SWA_QH, 1, 1)
        meta_scr[...] += _swa_kv_tile(dkm, dvm)
        first = (n == 0).astype(F32)
        dkv_ref[...] = _swa_kv_tile(dkc, dvc) + prev_scr[...] + first * meta_scr[...]
        prev_scr[...] = _swa_kv_tile(dkp, dvp)

    return pl.pallas_call(
        body, name="swa_bwd", grid=(nb,),
        in_specs=[qsp, meta, prev, cur, scal, qsp],
        out_specs=[qsp, cur, scal],
        out_shape=[jax.ShapeDtypeStruct(q.shape, F32), jax.ShapeDtypeStruct(kv.shape, F32),
                   jax.ShapeDtypeStruct(sink.shape, F32)],
        scratch_shapes=[pltpu.VMEM((BLK, 2 * SWA_KW), F32)] * 2,
        compiler_params=_cp(("arbitrary",)),
    )(q, kv, kv, kv, sink, do)


@jax.custom_vjp
def swa_core(q, kv, sink):
    return _swa_fwd_call(q, kv, sink)


def _swa_core_fwd(q, kv, sink):
    return _swa_fwd_call(q, kv, sink), (q, kv, sink)


def _swa_core_bwd(res, do):
    return tuple(_swa_bwd_call(*res, do))


swa_core.defvjp(_swa_core_fwd, _swa_core_bwd)


def _tile(n, pref):
    if n <= pref:
        return n
    best = None
    for t in range(128, pref + 1, 128):
        if n % t == 0:
            best = t
    assert best is not None, (n, pref)
    return best


MM_TILE_BYTES = 9 * 1024 * 1024


def _mm_tiles(m, n, kk, a_bytes, b_bytes):
    if kk > 8192:
        return _tile(m, 2816 // a_bytes), _tile(n, 512), _tile(kk, 4096)
    if kk > 1408 and _tile(m, 1024) * kk * a_bytes <= MM_TILE_BYTES:
        return _tile(m, 1024), _tile(n, 1024 if 1024 * kk * b_bytes <= MM_TILE_BYTES else 512), kk
    return _tile(m, 1408), _tile(n, 1024 if kk <= 1408 else 512), _tile(kk, 1408)


def _mm_call(a, b, name):
    (m, kk), n = a.shape, b.shape[1]
    tm, tn, tk = _mm_tiles(m, n, kk, a.dtype.itemsize, b.dtype.itemsize)
    nk = kk // tk
    a_spec = pl.BlockSpec((tm, tk), lambda i, j, k: (i, k))
    b_spec = pl.BlockSpec((tk, tn), lambda i, j, k: (k, j))

    def body(a_ref, b_ref, o_ref, acc_ref):
        k = pl.program_id(2)
        part = jnp.dot(a_ref[...].astype(BF16), b_ref[...].astype(BF16), preferred_element_type=F32)

        @pl.when(k == 0)
        def _():
            acc_ref[...] = part

        @pl.when(k > 0)
        def _():
            acc_ref[...] += part

        @pl.when(k == nk - 1)
        def _():
            o_ref[...] = acc_ref[...]

    return pl.pallas_call(
        body, name=name, grid=(m // tm, n // tn, nk),
        in_specs=[a_spec, b_spec],
        out_specs=pl.BlockSpec((tm, tn), lambda i, j, k: (i, j)),
        out_shape=jax.ShapeDtypeStruct((m, n), F32),
        scratch_shapes=[pltpu.VMEM((tm, tn), F32)],
        compiler_params=_cp(("parallel", "parallel", "arbitrary")),
    )(a, b)


@jax.custom_vjp
def mm(a, b, b_t, grad_slot):
    return _mm_call(a, b, "mm_fwd")


def _mm_fwd(a, b, b_t, grad_slot):
    return _mm_call(a, b, "mm_fwd"), (a, b, b_t)


def _mm_bwd(res, dc):
    a, b, b_t = res
    return (_mm_call(dc, b_t, "mm_dx"), jnp.zeros_like(b), jnp.zeros_like(b_t),
            _mm_call(a.astype(BF16).T, dc, "mm_dw"))


mm.defvjp(_mm_fwd, _mm_bwd)


_SPLIT = (1024, 1024, 1024, 1024, 1024, 2048, 1024, 512, 3072, 512)


def _split_cols(u):
    offs = [sum(_SPLIT[:i]) for i in range(len(_SPLIT))]
    return tuple(u[:, o:o + s] for o, s in zip(offs, _SPLIT))


@jax.custom_vjp
def mm_split(a, b, b_t, grad_slot):
    return _split_cols(_mm_call(a, b, "mm_fwd"))


def _mm_split_fwd(a, b, b_t, grad_slot):
    return _split_cols(_mm_call(a, b, "mm_fwd")), (a, b, b_t)


def _mm_split_bwd(res, cts):
    return _mm_bwd(res, jnp.concatenate([c.astype(BF16) for c in cts], axis=1))


mm_split.defvjp(_mm_split_fwd, _mm_split_bwd)


def _row_specs(arrs, tr):
    return [pl.BlockSpec((tr, a.shape[1]), lambda i: (i, 0)) for a in arrs]


def _par_specs(arrs):
    return [pl.BlockSpec(a.shape, lambda i: (0, 0)) for a in arrs]


def _row_fwd_call(fn, rows, params, out_cols, tr, name):
    seq = rows[0].shape[0]
    nr = len(rows)

    def body(*refs):
        vals = [r[...] for r in refs[:-1]]
        refs[-1][...] = fn(*vals)

    return pl.pallas_call(
        body, name=name, grid=(seq // tr,),
        in_specs=_row_specs(rows, tr) + _par_specs(params),
        out_specs=pl.BlockSpec((tr, out_cols), lambda i: (i, 0)),
        out_shape=jax.ShapeDtypeStruct((seq, out_cols), F32),
        compiler_params=_cp(("parallel",)),
    )(*rows, *params)


def _row_bwd_call(fn, rows, params, dy, tr, name):
    seq = rows[0].shape[0]
    nr, npar = len(rows), len(params)

    def body(*refs):
        ins = refs[:nr + npar]
        dy_ref = refs[nr + npar]
        outs = refs[nr + npar + 1:]
        _, vjp = jax.vjp(fn, *[r[...] for r in ins])
        cts = vjp(dy_ref[...])
        for o_ref, ct in zip(outs[:nr], cts[:nr]):
            o_ref[...] = ct

        @pl.when(pl.program_id(0) == 0)
        def _():
            for o_ref in outs[nr:]:
                o_ref[...] = jnp.zeros_like(o_ref)

        for o_ref, ct in zip(outs[nr:], cts[nr:]):
            o_ref[...] += ct

    return pl.pallas_call(
        body, name=name, grid=(seq // tr,),
        in_specs=_row_specs(rows, tr) + _par_specs(params) + _row_specs([dy], tr),
        out_specs=_row_specs(rows, tr) + _par_specs(params),
        out_shape=[jax.ShapeDtypeStruct(a.shape, F32) for a in (*rows, *params)],
        compiler_params=_cp(("arbitrary",)),
    )(*rows, *params, dy)


def _make_rowop(fn, nrows, out_cols, tr, name):
    @jax.custom_vjp
    def op(*args):
        return _row_fwd_call(fn, args[:nrows], args[nrows:], out_cols, tr, name + "_fwd")

    def fwd(*args):
        return op(*args), args

    def bwd(args, dy):
        return tuple(_row_bwd_call(fn, args[:nrows], args[nrows:], dy, tr, name + "_bwd"))

    op.defvjp(fwd, bwd)
    return op


def _rms_fn(x, w):
    return x * lax.rsqrt(jnp.mean(x * x, axis=-1, keepdims=True) + RMS_EPS) * w


def _merge_fn(pa, pb, pc, gl):
    d = D_MODEL
    return (jax.nn.sigmoid(gl[:, :d]) * pa + jax.nn.sigmoid(gl[:, d:2 * d]) * pb
            + jax.nn.sigmoid(gl[:, 2 * d:]) * pc)


def _relu2_fn(a):
    r = jnp.maximum(a, 0.0)
    return r * r


rms_op = _make_rowop(_rms_fn, 1, D_MODEL, 384, "rms")
merge_op = _make_rowop(_merge_fn, 4, D_MODEL, 192, "merge")
relu2_op = _make_rowop(_relu2_fn, 1, D_FF, 192, "relu2")


def _conv_taps(xext, w, nrows):
    z = None
    for j in range(CONV_K):
        sh = CONV_K - 1 - j
        xs = pltpu.roll(xext, sh, 0) if sh else xext
        term = w[j:j + 1, :] * xs[8:8 + nrows, :]
        z = term if z is None else z + term
    return z


def _halo(ref, start, ok):
    return jnp.where(ok, ref[pl.ds(pl.multiple_of(start, 8), 8), :], 0.0)


def _conv_fwd_call(x, w, b):
    seq, ch = x.shape
    nb = seq // BLK

    def body(x_ref, w_ref, b_ref, o_ref):
        w = w_ref[...]
        bias = b_ref[...]

        def step(i, carry):
            r0 = pl.multiple_of(i * BLK, BLK)
            xext = jnp.concatenate([_halo(x_ref, jnp.maximum(r0 - 8, 0), i > 0), x_ref[pl.ds(r0, BLK), :]], axis=0)
            o_ref[pl.ds(r0, BLK), :] = _silu(_conv_taps(xext, w, BLK) + bias)
            return carry

        lax.fori_loop(0, nb, step, 0, unroll=3)

    strip = pl.BlockSpec((seq, CONV_W), lambda c: (0, c))
    return pl.pallas_call(
        body, name="conv_fwd", grid=(ch // CONV_W,),
        in_specs=[strip, pl.BlockSpec((CONV_K, CONV_W), lambda c: (0, c)), pl.BlockSpec((1, CONV_W), lambda c: (0, c))],
        out_specs=strip, out_shape=jax.ShapeDtypeStruct(x.shape, F32),
        compiler_params=_cp(("parallel",)),
    )(x, w, b)


def _conv_bwd_call(x, w, b, dy):
    seq, ch = x.shape
    nb = seq // BLK

    def body(x_ref, w_ref, b_ref, dy_ref, dx_ref, dw_ref, db_ref):
        w = w_ref[...]
        bias = b_ref[...]

        def step(i, carry):
            r0 = pl.multiple_of(i * BLK, BLK)
            last = i == nb - 1
            nxt = jnp.minimum(r0 + BLK, seq - 8)
            xext = jnp.concatenate([_halo(x_ref, jnp.maximum(r0 - 8, 0), i > 0), x_ref[pl.ds(r0, BLK), :],
                                    _halo(x_ref, nxt, jnp.logical_not(last))], axis=0)
            dyext = jnp.concatenate([dy_ref[pl.ds(r0, BLK), :], _halo(dy_ref, nxt, jnp.logical_not(last))], axis=0)
            z = _conv_taps(xext, w, BLK + 8) + bias
            sg = jax.nn.sigmoid(z)
            dz = dyext * (sg * (1.0 + z * (1.0 - sg)))
            dx = None
            for j in range(CONV_K):
                sh = CONV_K - 1 - j
                dzs = pltpu.roll(dz, BLK + 8 - sh, 0) if sh else dz
                term = w[j:j + 1, :] * dzs[:BLK, :]
                dx = term if dx is None else dx + term
            dx_ref[pl.ds(r0, BLK), :] = dx
            dzm = dz[:BLK, :]
            out = []
            for j in range(CONV_K):
                sh = CONV_K - 1 - j
                xs = pltpu.roll(xext, sh, 0) if sh else xext
                out.append(carry[j] + jnp.sum(dzm * xs[8:8 + BLK, :], axis=0, keepdims=True))
            out.append(carry[CONV_K] + jnp.sum(dzm, axis=0, keepdims=True))
            return tuple(out)

        zero = jnp.zeros((1, CONV_W), F32)
        acc = lax.fori_loop(0, nb, step, (zero,) * (CONV_K + 1), unroll=3)
        dw_ref[...] = jnp.concatenate(acc[:CONV_K], axis=0)
        db_ref[...] = acc[CONV_K]

    strip = pl.BlockSpec((seq, CONV_W), lambda c: (0, c))
    wsp = pl.BlockSpec((CONV_K, CONV_W), lambda c: (0, c))
    bsp = pl.BlockSpec((1, CONV_W), lambda c: (0, c))
    return pl.pallas_call(
        body, name="conv_bwd", grid=(ch // CONV_W,),
        in_specs=[strip, wsp, bsp, strip],
        out_specs=[strip, wsp, bsp],
        out_shape=[jax.ShapeDtypeStruct(x.shape, F32), jax.ShapeDtypeStruct(w.shape, F32),
                   jax.ShapeDtypeStruct(b.shape, F32)],
        compiler_params=_cp(("parallel",)),
    )(x, w, b, dy)


@jax.custom_vjp
def conv_silu(x, w, b):
    return _conv_fwd_call(x, w, b)


def _conv_silu_fwd(x, w, b):
    return _conv_fwd_call(x, w, b), (x, w, b)


def _conv_silu_bwd(res, dy):
    return tuple(_conv_bwd_call(*res, dy))


conv_silu.defvjp(_conv_silu_fwd, _conv_silu_bwd)


def _loss_call(h, wf, target):
    seq, d = h.shape
    nb = seq // BLK

    def body(h_ref, w_ref, t_ref, loss_ref, dh_ref, dw_ref):
        i = pl.program_id(0)
        live = (i > 0).astype(F32)
        tgt = t_ref[...]

        def fn(hh, ww):
            err = _rms_fn(hh, ww) - tgt
            return 0.5 * live * jnp.sum(jnp.mean(err * err, axis=-1, keepdims=True), axis=0, keepdims=True)

        val, vjp = jax.vjp(fn, h_ref[...], w_ref[...])
        dh, dw = vjp(jnp.ones((1, 1), F32))
        dh_ref[...] = dh

        @pl.when(i == 0)
        def _():
            loss_ref[...] = jnp.zeros_like(loss_ref)
            dw_ref[...] = jnp.zeros_like(dw_ref)

        loss_ref[...] += val
        dw_ref[...] += dw

    return pl.pallas_call(
        body, name="loss_head", grid=(nb,),
        in_specs=[pl.BlockSpec((BLK, d), lambda i: (i, 0)), pl.BlockSpec((1, d), lambda i: (0, 0)),
                  pl.BlockSpec((BLK, d), lambda i: (jnp.maximum(i - 1, 0), 0))],
        out_specs=[pl.BlockSpec((1, 1), lambda i: (0, 0)), pl.BlockSpec((BLK, d), lambda i: (i, 0)),
                   pl.BlockSpec((1, d), lambda i: (0, 0))],
        out_shape=[jax.ShapeDtypeStruct((1, 1), F32), jax.ShapeDtypeStruct(h.shape, F32),
                   jax.ShapeDtypeStruct((1, d), F32)],
        compiler_params=_cp(("arbitrary",)),
    )(h, wf, target)


def _make_loss_head(target):
    @jax.custom_vjp
    def head(h, wf):
        return _loss_call(h, wf, target)[0][0, 0]

    def fwd(h, wf):
        loss, dh, dw = _loss_call(h, wf, target)
        return loss[0, 0], (dh, dw)

    def bwd(res, g):
        return g * res[0], g * res[1]

    head.defvjp(fwd, bwd)
    return head


_IN_SEGS = (("q", 0, 1024), ("k", 1024, 1024), ("v", 2048, 1024), ("gate", 3072, 1024), ("z", 4112, 1024),
            ("xbc", 5136, 2048), ("cq", 7200, 1024), ("ck", 8224, 256), ("cv", 8480, 256), ("gl", 8736, 3072),
            ("b", 4096, 8), ("a", 4104, 8), ("dt", 7184, 16))
_IN_PAD = sum(_SPLIT) - sum(n for _, _, n in _IN_SEGS)


_MATMUL = ("w_in", "w_proj_gdn", "w_proj_ssd", "w_proj_swa", "w_out", "w_up", "w_down")
_LATE = _MATMUL[1:]


def _late_weights(gathered):
    g = dict(zip(_LATE, gathered))
    full = {n: g[n].reshape(D_MODEL, D_MODEL) for n in _LATE[:4]}
    full["w_up"] = g["w_up"].transpose(1, 0, 2).reshape(D_MODEL, D_FF)
    full["w_down"] = g["w_down"].reshape(D_FF, D_MODEL)
    full.update({n + "_t": t.T for n, t in list(full.items())})
    return full


def _layer(h, p, w_in, w_in_t, slot, late_shards, next_shards=(), exchange_slots=()):
    wb = {"w_in": w_in, "w_in_t": w_in_t}

    def proj(t, name):
        return mm(t, wb[name], wb[name + "_t"], slot[name])

    q_pre, k_pre, v_pre, gate, z, xbc_pre, cq, ckv, gl, small = mm_split(
        rms_op(h, p["norm1_w"].reshape(1, -1)), w_in, w_in_t, slot["w_in"])

    gcw = p["gdn_conv_w"]
    nob = jnp.zeros((1, GDN_H * GDN_D), F32)
    qa = conv_silu(q_pre, gcw[:, :1024], nob)
    ka = conv_silu(k_pre, gcw[:, 1024:2048], nob)
    va = conv_silu(v_pre, gcw[:, 2048:], nob)
    y_gdn, late, placeholders = gdn_core(
        qa, ka, va, gate, small, p["gdn_a_log"].reshape(GDN_H, 1, 1), p["gdn_dt_bias"].reshape(GDN_H, 1, 1),
        p["gdn_norm_w"].reshape(1, GDN_D), tuple(late_shards), tuple(exchange_slots))
    wb.update(_late_weights(late))

    xbc = conv_silu(xbc_pre, p["ssd_conv_w"], p["ssd_conv_b"].reshape(1, -1))
    y_ssd, gathered = ssd_core(xbc, z, small, p["ssd_dt_bias"].reshape(SSD_H, 1, 1),
                               p["ssd_a_log"].reshape(SSD_H, 1, 1), p["ssd_d"].reshape(SSD_H, 1, 1),
                               p["ssd_norm_w"].reshape(SSD_H, 1, SSD_P), tuple(next_shards))

    y_swa = swa_core(cq, ckv, p["swa_sinks"].reshape(SWA_QH, 1, 1))

    merged = merge_op(proj(y_gdn, "w_proj_gdn"), proj(y_ssd, "w_proj_ssd"), proj(y_swa, "w_proj_swa"), gl)
    h = h + proj(merged, "w_out")
    a1 = proj(rms_op(h, p["norm2_w"].reshape(1, -1)), "w_up")
    return h + proj(relu2_op(a1), "w_down"), gathered, placeholders


_PER_LAYER = ("norm1_w", "gdn_conv_w", "gdn_a_log", "gdn_dt_bias", "gdn_norm_w", "ssd_conv_w", "ssd_conv_b",
              "ssd_dt_bias", "ssd_a_log", "ssd_d", "ssd_norm_w", "swa_sinks", "norm2_w")


def _embed(x, meta):
    return jnp.concatenate([jnp.zeros((NPAD, D_MODEL), F32), meta, x], axis=0)


_IN_SHARD = 1476


def _in_pieces():
    out = []
    for _, s, n in _IN_SEGS:
        c = s
        while c < s + n:
            d = c // _IN_SHARD
            e = min(s + n, (d + 1) * _IN_SHARD)
            out.append((d, c - d * _IN_SHARD, e - d * _IN_SHARD))
            c = e
    return out


def _in_pieces_back():
    start, off = {}, 0
    for _, s, n in _IN_SEGS:
        start[s] = off
        off += n
    out = [[] for _ in range(N_DEV)]
    for _, s, n in sorted(_IN_SEGS, key=lambda t: t[1]):
        c = s
        while c < s + n:
            d = c // _IN_SHARD
            e = min(s + n, (d + 1) * _IN_SHARD)
            out[d].append((start[s] + c - s, start[s] + e - s))
            c = e
    return out


def _regroup_w_in(stacked):
    parts = [stacked[d, :, lo:hi] for d, lo, hi in _in_pieces()]
    return jnp.concatenate(parts + [jnp.zeros((D_MODEL, _IN_PAD), stacked.dtype)], axis=1)


def _ungroup_w_in(g):
    return [jnp.concatenate([g[:, lo:hi] for lo, hi in pieces], axis=1) for pieces in _in_pieces_back()]


def _position():
    return lax.axis_index("x"), lax.axis_index("y"), lax.axis_index("c")


_ANY = pl.BlockSpec(memory_space=pl.ANY)


def _chip_of(x, y, k):
    return (1 - x if k & 1 else x, 1 - y if k & 2 else y)


def _allgather_call(shards, name):
    n = len(shards)

    def body(*refs):
        start, relay, finish = _gather_phases(refs[:n], refs[n:2 * n], *refs[2 * n:])
        start()
        relay()
        finish()

    return pl.pallas_call(
        body, name=name,
        out_shape=_gather_out_shapes(shards),
        in_specs=[_ANY] * n, out_specs=[_ANY] * n,
        scratch_shapes=_gather_sems(n),
    )(*shards)


def _gather_out_shapes(shards):
    return [jax.ShapeDtypeStruct((N_DEV, *s.shape), s.dtype) for s in shards]


def _gather_sems(n):
    return [pltpu.SemaphoreType.DMA((7 * n,)), pltpu.SemaphoreType.DMA((7 * n,)), pltpu.SemaphoreType.DMA((n,))]


def _gather_phases(x_refs, out_refs, send_sems, recv_sems, local_sems):
    n = len(x_refs)
    x, y, c = _position()
    me, sibling = (x, y, c), (x, y, 1 - c)
    chips = [_chip_of(x, y, k) for k in (1, 2, 3)]

    def slab(a, px, py, pc):
        return out_refs[a].at[4 * px + 2 * py + pc]

    def copy(a, k, block, to, src=None):
        return pltpu.make_async_remote_copy(
            src_ref=slab(a, *block) if src is None else src, dst_ref=slab(a, *block),
            send_sem=send_sems.at[7 * a + k], recv_sem=recv_sems.at[7 * a + k], device_id=to, device_id_type=MESH)

    def mine():
        return [pltpu.make_async_copy(x_refs[a], slab(a, *me), local_sems.at[a]) for a in range(n)]

    def first():
        out = []
        for a in range(n):
            out.append(copy(a, 0, me, sibling, src=x_refs[a]))
            out += [copy(a, 1 + j, me, (*chip, c), src=x_refs[a]) for j, chip in enumerate(chips)]
        return out

    def passed():
        return [copy(a, 4 + j, (*chip, c), sibling) for j, chip in enumerate(chips) for a in range(n)]

    def start():
        for cp in mine() + first():
            cp.start()

    def relay():
        for j, chip in enumerate(chips):
            for a in range(n):
                copy(a, 1 + j, (*chip, c), me).wait_recv()
                copy(a, 4 + j, (*chip, c), sibling).start()

    def finish():
        for a in range(n):
            copy(a, 0, sibling, me).wait_recv()
        for j, chip in enumerate(chips):
            for a in range(n):
                copy(a, 4 + j, (*chip, 1 - c), me).wait_recv()
        for cp in first() + passed():
            cp.wait_send()
        for cp in mine():
            cp.wait()

    return start, relay, finish


def _sibling_exchange_call(for_c0, for_c1, name):
    n = len(for_c0)

    def body(*refs):
        c0_refs, c1_refs, out_refs = refs[:4 * n], refs[4 * n:8 * n], refs[8 * n:9 * n]
        send_sems, recv_sems = refs[9 * n:]
        x, y, c = _position()

        def copies(src_refs):
            return [pltpu.make_async_remote_copy(
                src_ref=src_refs[4 * a + q], dst_ref=out_refs[a].at[q],
                send_sem=send_sems.at[4 * a + q], recv_sem=recv_sems.at[4 * a + q],
                device_id=(x, y, 1 - c), device_id_type=MESH) for a in range(n) for q in range(4)]

        @pl.when(c == 0)
        def _():
            for cp in copies(c1_refs):
                cp.start()

        @pl.when(c == 1)
        def _():
            for cp in copies(c0_refs):
                cp.start()

        waits = copies(c0_refs)
        for cp in waits:
            cp.wait_recv()
        for cp in waits:
            cp.wait_send()

    return pl.pallas_call(
        body, name=name,
        out_shape=[jax.ShapeDtypeStruct((4, *g[0].shape), g[0].dtype) for g in for_c0],
        in_specs=[_ANY] * (8 * n), out_specs=[_ANY] * n,
        scratch_shapes=[pltpu.SemaphoreType.DMA((4 * n,)), pltpu.SemaphoreType.DMA((4 * n,))],
    )(*[t for g in for_c0 for t in g], *[t for g in for_c1 for t in g])


def _chip_exchange_call(partials, name):
    n = len(partials)

    def body(*refs):
        start, finish = _chip_exchange_phases(refs[:n], refs[n:2 * n], *refs[2 * n:])
        start()
        finish()

    return pl.pallas_call(
        body, name=name,
        out_shape=_chip_exchange_out_shapes(partials),
        in_specs=[_ANY] * n, out_specs=[_ANY] * n,
        scratch_shapes=_chip_exchange_sems(n),
    )(*partials)


def _chip_exchange_out_shapes(partials):
    return [jax.ShapeDtypeStruct((3, *p.shape[1:]), p.dtype) for p in partials]


def _chip_exchange_sems(n):
    return [pltpu.SemaphoreType.DMA((3 * n,)), pltpu.SemaphoreType.DMA((3 * n,))]


def _chip_exchange_phases(p_refs, out_refs, send_sems, recv_sems):
    n = len(p_refs)
    x, y, c = _position()

    def copies():
        out = []
        for a in range(n):
            for k in (1, 2, 3):
                px, py = _chip_of(x, y, k)
                out.append(pltpu.make_async_remote_copy(
                    src_ref=p_refs[a].at[2 * px + py], dst_ref=out_refs[a].at[k - 1],
                    send_sem=send_sems.at[3 * a + k - 1], recv_sem=recv_sems.at[3 * a + k - 1],
                    device_id=(px, py, c), device_id_type=MESH))
        return out

    def start():
        for cp in copies():
            cp.start()

    def finish():
        for cp in copies():
            cp.wait_recv()
        for cp in copies():
            cp.wait_send()

    return start, finish


def _chip_partial_call(for_c0, for_c1, sib, tr, name):
    _, r, c = sib.shape

    def body(*refs):
        c0_refs, c1_refs, (s_ref, own_ref, out_ref) = refs[:4], refs[4:8], refs[8:]
        x, y, core = _position()
        own = jnp.zeros((tr, c), F32)
        for q in range(4):
            partial = jnp.where(core == 0, c0_refs[q][...], c1_refs[q][...]) + s_ref[q]
            own = jnp.where(2 * x + y == q, partial, own)
            out_ref[q] = partial.astype(BF16)
        own_ref[...] = own

    one = pl.BlockSpec((tr, c), lambda i: (i, 0))
    four = pl.BlockSpec((4, tr, c), lambda i: (0, i, 0))
    return pl.pallas_call(
        body, name=name, grid=(r // tr,),
        in_specs=[one] * 8 + [four],
        out_specs=[one, four],
        out_shape=[jax.ShapeDtypeStruct((r, c), F32), jax.ShapeDtypeStruct((4, r, c), BF16)],
        compiler_params=_cp(("parallel",)),
    )(*for_c0, *for_c1, sib)


def _adamw_call(parts, w, m, v, tr, name):
    ns, r, c = w.shape
    counts = [len(p) for p in parts]
    flat_parts = [a for p in parts for a in p]

    def body(*refs):
        p_refs = refs[:len(flat_parts)]
        w_ref, m_ref, v_ref, g_ref, d_ref, nm_ref, nv_ref = refs[len(flat_parts):]
        at = 0
        for s in range(ns):
            g = None
            for p_ref in p_refs[at:at + counts[s]]:
                for j in range(p_ref.shape[0]):
                    term = p_ref[j].astype(F32)
                    g = term if g is None else g + term
            at += counts[s]
            nm = ADAM_B1 * m_ref[s] + (1.0 - ADAM_B1) * g
            nv = ADAM_B2 * v_ref[s] + (1.0 - ADAM_B2) * (g * g)
            m_hat = nm / (1.0 - ADAM_B1 ** ADAM_STEP)
            v_hat = nv / (1.0 - ADAM_B2 ** ADAM_STEP)
            g_ref[s] = g
            d_ref[s] = -ADAM_LR * (m_hat / (jnp.sqrt(v_hat) + ADAM_EPS) + ADAM_WD * w_ref[s])
            nm_ref[s] = nm
            nv_ref[s] = nv

    slabs = pl.BlockSpec((ns, tr, c), lambda i: (0, i, 0))
    return pl.pallas_call(
        body, name=name, grid=(r // tr,),
        in_specs=[pl.BlockSpec((a.shape[0], tr, c), lambda i: (0, i, 0)) for a in flat_parts] + [slabs] * 3,
        out_specs=[slabs] * 4,
        out_shape=[jax.ShapeDtypeStruct((ns, r, c), F32)] * 4,
        compiler_params=_cp(("parallel",)),
    )(*flat_parts, w, m, v)


_WEIGHTS = ("meta_tokens", "norm1_w", "w_in", "gdn_conv_w", "gdn_a_log", "gdn_dt_bias", "gdn_norm_w", "ssd_conv_w",
            "ssd_conv_b", "ssd_dt_bias", "ssd_a_log", "ssd_d", "ssd_norm_w", "swa_sinks", "w_proj_gdn", "w_proj_ssd",
            "w_proj_swa", "w_out", "norm2_w", "w_up", "w_down", "final_norm_w")
_SHARD_AXIS = {"meta_tokens": 1, "w_in": 2, "gdn_conv_w": 2, "ssd_conv_w": 2, "w_proj_gdn": 1, "w_proj_ssd": 1,
               "w_proj_swa": 1, "w_out": 1, "w_up": 2, "w_down": 1}
_BIG = tuple(n for n in _WEIGHTS if n in _SHARD_AXIS)
_SMALL = tuple(n for n in _WEIGHTS if n not in _SHARD_AXIS)
FLAT_C = 1024


def _pack(arrs, rows, lead=()):
    flat = jnp.concatenate([a.reshape(*lead, -1) for a in arrs], axis=-1)
    pad = rows * FLAT_C - flat.shape[-1]
    flat = jnp.pad(flat, [(0, 0)] * len(lead) + [(0, pad)])
    return flat.reshape(*lead, rows, FLAT_C)


def _unpack(flat, shapes, lead=()):
    flat = flat.reshape(*lead, -1)
    out, off = [], 0
    for s in shapes:
        n = math.prod(s)
        out.append(flat[..., off:off + n].reshape(*lead, *s))
        off += n
    return out


def _rows_for(shapes):
    n = sum(math.prod(s) for s in shapes)
    return -(-n // (FLAT_C * 8)) * 8


def _rows_tile(r, c):
    if r <= 256:
        return r
    return 128 if c > 1024 else 256


def _join(stacked, axis):
    moved = jnp.moveaxis(stacked, 0, axis)
    return moved.reshape(*moved.shape[:axis], -1, *moved.shape[axis + 2:])


def _unjoin(full, axis):
    cut = full.reshape(*full.shape[:axis], N_DEV, full.shape[axis] // N_DEV, *full.shape[axis + 1:])
    return jnp.moveaxis(cut, axis, 0)


def kernel(x, meta_tokens, norm1_w, w_in, gdn_conv_w, gdn_a_log, gdn_dt_bias, gdn_norm_w, ssd_conv_w, ssd_conv_b,
           ssd_dt_bias, ssd_a_log, ssd_d, ssd_norm_w, swa_sinks, w_proj_gdn, w_proj_ssd, w_proj_swa, w_out, norm2_w,
           w_up, w_down, final_norm_w, loss_target, m_meta_tokens, m_norm1_w, m_w_in, m_gdn_conv_w, m_gdn_a_log,
           m_gdn_dt_bias, m_gdn_norm_w, m_ssd_conv_w, m_ssd_conv_b, m_ssd_dt_bias, m_ssd_a_log, m_ssd_d, m_ssd_norm_w,
           m_swa_sinks, m_w_proj_gdn, m_w_proj_ssd, m_w_proj_swa, m_w_out, m_norm2_w, m_w_up, m_w_down,
           m_final_norm_w, v_meta_tokens, v_norm1_w, v_w_in, v_gdn_conv_w, v_gdn_a_log, v_gdn_dt_bias, v_gdn_norm_w,
           v_ssd_conv_w, v_ssd_conv_b, v_ssd_dt_bias, v_ssd_a_log, v_ssd_d, v_ssd_norm_w, v_swa_sinks, v_w_proj_gdn,
           v_w_proj_ssd, v_w_proj_swa, v_w_out, v_norm2_w, v_w_up, v_w_down, v_final_norm_w):
    args = (meta_tokens, norm1_w, w_in, gdn_conv_w, gdn_a_log, gdn_dt_bias, gdn_norm_w, ssd_conv_w, ssd_conv_b,
            ssd_dt_bias, ssd_a_log, ssd_d, ssd_norm_w, swa_sinks, w_proj_gdn, w_proj_ssd, w_proj_swa, w_out, norm2_w,
            w_up, w_down, final_norm_w, m_meta_tokens, m_norm1_w, m_w_in, m_gdn_conv_w, m_gdn_a_log,
            m_gdn_dt_bias, m_gdn_norm_w, m_ssd_conv_w, m_ssd_conv_b, m_ssd_dt_bias, m_ssd_a_log, m_ssd_d, m_ssd_norm_w,
            m_swa_sinks, m_w_proj_gdn, m_w_proj_ssd, m_w_proj_swa, m_w_out, m_norm2_w, m_w_up, m_w_down,
            m_final_norm_w, v_meta_tokens, v_norm1_w, v_w_in, v_gdn_conv_w, v_gdn_a_log, v_gdn_dt_bias, v_gdn_norm_w,
            v_ssd_conv_w, v_ssd_conv_b, v_ssd_dt_bias, v_ssd_a_log, v_ssd_d, v_ssd_norm_w, v_swa_sinks, v_w_proj_gdn,
            v_w_proj_ssd, v_w_proj_swa, v_w_out, v_norm2_w, v_w_up, v_w_down, v_final_norm_w)
    nw = len(_WEIGHTS)
    w = dict(zip(_WEIGHTS, args[:nw]))
    m = dict(zip(_WEIGHTS, args[nw:2 * nw]))
    v = dict(zip(_WEIGHTS, args[2 * nw:]))

    depth = w["w_in"].shape[0]
    small_shapes = [w[n].shape for n in _SMALL]
    small_rows = _rows_for(small_shapes)

    def flat2(t):
        return t.reshape(-1, t.shape[-1])

    tiny_names = [n for n in _BIG if n not in _MATMUL]

    def shard(n, l):
        return w[n][l].astype(BF16)

    first = _allgather_call([shard("w_in", 0)] + [flat2(w[n]) for n in tiny_names], "gather_weights")
    w_in_stacked = first[0]
    joined = {n: _join(t.reshape(N_DEV, *w[n].shape), _SHARD_AXIS[n]) for n, t in zip(tiny_names, first[1:])}
    slot_shapes = {"w_in": (D_MODEL, sum(_SPLIT)), "w_up": (D_MODEL, D_FF), "w_down": (D_FF, D_MODEL)}
    slot_shapes.update({n: (D_MODEL, D_MODEL) for n in _LATE[:4]})

    def layer_fn(l, w_in_full, late_shards, next_shards):
        w_in_t = w_in_full.T
        if l == 0:
            def fn(x_rows, meta, p, slot, exchange_slots):
                out, g, placeholders = _layer(_embed(x_rows, meta), p, w_in_full, w_in_t, slot, late_shards,
                                              next_shards, exchange_slots)
                return (out, placeholders), g
        else:
            def fn(h_in, p, slot, exchange_slots):
                out, g, placeholders = _layer(h_in, p, w_in_full, w_in_t, slot, late_shards, next_shards,
                                              exchange_slots)
                return (out, placeholders), g
        return fn

    h, vjps = None, []
    for l in range(depth):
        slot = {n: jnp.zeros(s, F32) for n, s in slot_shapes.items()}
        p = {n: (joined[n][l] if n in joined else w[n][l]) for n in _PER_LAYER}
        more = l + 1 < depth
        next_shards = [shard("w_in", l + 1)] if more else []
        exchange_slots = tuple(jnp.zeros((3, *w[n][l + 1].shape), BF16) for n in _MATMUL) if more else ()
        lead = (x[0], joined["meta_tokens"]) if l == 0 else (h,)
        fn = layer_fn(l, _regroup_w_in(w_in_stacked), [shard(n, l) for n in _LATE], next_shards)
        (h, _), vjp, g_next = jax.vjp(fn, *lead, p, slot, exchange_slots, has_aux=True)
        if more:
            w_in_stacked = g_next[0]
        vjps.append(vjp)
    loss, head_vjp = jax.vjp(_make_loss_head(loss_target[0]), h, w["final_norm_w"].reshape(1, -1))
    dh, d_final = head_vjp(jnp.ones((), F32))
    loss = lax.psum(loss, ("x", "y", "c"))

    def by_core(name, g):
        if name == "w_in":
            shards = _ungroup_w_in(g)
            return shards[0::2], shards[1::2]
        if name == "w_up":
            t = g.reshape(D_MODEL, 4, 2, D_FF // N_DEV)
            return [t[:, q, 0] for q in range(4)], [t[:, q, 1] for q in range(4)]
        t = g.reshape(4, 2, -1, g.shape[-1])
        return [t[q, 0] for q in range(4)], [t[q, 1] for q in range(4)]

    own, incoming, layer_grads, outgoing = {}, {}, [None] * depth, ()
    for l in reversed(range(depth)):
        if l == 0:
            gx, d_meta, dp, dslot, arrived = vjps[0]((dh, tuple(outgoing)))
        else:
            dh, dp, dslot, arrived = vjps[l]((dh, tuple(outgoing)))
        incoming.update({(n, l + 1): t for n, t in zip(_MATMUL, arrived)})
        layer_grads[l] = dp
        todo = [((n, l), dslot[n]) for n in _MATMUL]
        if l == 0:
            full_grads = {"meta_tokens": d_meta}
            full_grads.update({n: jnp.stack([layer_grads[k][n] for k in range(depth)]) for n in tiny_names[1:]})
            todo += [((n, None), _unjoin(full_grads[n], _SHARD_AXIS[n]).reshape(N_DEV, -1, w[n].shape[-1]))
                     for n in tiny_names]
        pairs = [by_core(u[0], g) for u, g in todo]
        from_sibling = _sibling_exchange_call([a for a, _ in pairs], [b for _, b in pairs], "grads_to_sibling_%d" % l)
        outgoing = []
        for (u, _), (a0, a1), s in zip(todo, pairs, from_sibling):
            own[u], part = _chip_partial_call(a0, a1, s, _rows_tile(s.shape[1], s.shape[2]), "chip_partial_" + u[0])
            outgoing.append(part)
        if l == 0:
            incoming.update(zip([u for u, _ in todo], _chip_exchange_call(outgoing, "grads_to_chips")))

    g_small = {n: jnp.stack([layer_grads[k][n] for k in range(depth)]) for n in _SMALL if n != "final_norm_w"}
    g_small["final_norm_w"] = d_final.reshape(-1)

    by_name = {}
    for n in _BIG:
        layers = list(range(depth)) if n in _MATMUL else [None]
        parts = [[own[n, l][None], incoming[n, l]] for l in layers]
        r, c = own[n, layers[0]].shape
        stacked = [d[n].reshape(len(layers), r, c) for d in (w, m, v)]
        res = _adamw_call(parts, *stacked, _rows_tile(r, c), "adamw_" + n)
        by_name[n] = [t.reshape(w[n].shape) for t in res]

    small_parts = _allgather_call([_pack([g_small[n] for n in _SMALL], small_rows)], "gather_small_grads")
    small_out = _adamw_call([small_parts], *[_pack([d[n] for n in _SMALL], small_rows)[None] for d in (w, m, v)],
                            small_rows, "adamw_replicated")
    for kind in range(4):
        for n, t in zip(_SMALL, _unpack(small_out[kind][0], small_shapes)):
            by_name.setdefault(n, [None] * 4)[kind] = t

    outs = [by_name[n][kind] for kind in range(4) for n in _WEIGHTS]
    return (loss, gx[None], *outs)
```

```python
import functools
import math

import jax
import jax.numpy as jnp
from jax import lax
from jax.experimental import pallas as pl
from jax.experimental.pallas import tpu as pltpu

F32 = jnp.float32
BF16 = jnp.bfloat16
HI = lax.Precision.HIGH
NEG = -1e30

D_MODEL = 1024
N_META = 16
BLK = 128
NPAD = BLK - N_META
RMS_EPS = 1e-6
L2_EPS = 1e-6
CONV_K = 4
CONV_W = 256

GDN_H, GDN_D, GDN_C = 8, 128, 64
SSD_H, SSD_P, SSD_G, SSD_N = 16, 64, 4, 128
SSD_HPG = SSD_H // SSD_G
SWA_QH, SWA_KVH, SWA_D = 16, 4, 64
SWA_REP = SWA_QH // SWA_KVH
D_FF = 4 * D_MODEL

N_DEV = 8
MESH = pl.DeviceIdType.MESH

ADAM_LR, ADAM_B1, ADAM_B2, ADAM_EPS, ADAM_WD, ADAM_STEP = 0.001, 0.9, 0.999, 1e-08, 0.01, 10

VMEM_LIMIT = 56 * 1024 * 1024


def _cp(sem=None):
    return pltpu.CompilerParams(dimension_semantics=sem, vmem_limit_bytes=VMEM_LIMIT)


def _dot(a, b, ca, cb, prec=HI):
    return lax.dot_general(a, b, (((ca,), (cb,)), ((), ())), precision=prec, preferred_element_type=F32)


def _nn(a, b, prec=HI):
    return _dot(a, b, 1, 0, prec)


def _nt(a, b, prec=HI):
    return _dot(a, b, 1, 1, prec)


def _tn(a, b, prec=HI):
    return _dot(a, b, 0, 0, prec)


def _bdot(a, b, ca, cb):
    return lax.dot_general(a.astype(BF16), b.astype(BF16), (((ca,), (cb,)), ((), ())), preferred_element_type=F32)


@jax.custom_vjp
def _lo_nn(a, b):
    return _bdot(a, b, 1, 0)


_lo_nn.defvjp(lambda a, b: (_bdot(a, b, 1, 0), (a, b)),
              lambda r, d: (_bdot(d, r[1], 1, 1), _bdot(r[0], d, 0, 0)))


@jax.custom_vjp
def _lo_nt(a, b):
    return _bdot(a, b, 1, 1)


_lo_nt.defvjp(lambda a, b: (_bdot(a, b, 1, 1), (a, b)),
              lambda r, d: (_bdot(d, r[1], 1, 0), _bdot(d, r[0], 0, 0)))


@jax.custom_vjp
def _lo_tn(a, b):
    return _bdot(a, b, 0, 0)


_lo_tn.defvjp(lambda a, b: (_bdot(a, b, 0, 0), (a, b)),
              lambda r, d: (_bdot(r[1], d, 1, 1), _bdot(r[0], d, 1, 0)))


def _iota2(n, m, axis):
    return lax.broadcasted_iota(jnp.int32, (n, m), axis)


def _silu(x):
    return x * jax.nn.sigmoid(x)


def _softplus(x):
    return jnp.maximum(x, 0.0) + jnp.log(1.0 + jnp.exp(-jnp.abs(x)))


def _row_of(col):
    n = col.shape[0]
    return jnp.broadcast_to(col, (n, n)).T


def _cumsum_col(col):
    n = col.shape[0]
    tril = (_iota2(n, n, 0) >= _iota2(n, n, 1)).astype(F32)
    return _nn(tril, col)


def _tri_inv(a):
    n = a.shape[0]
    r, c = _iota2(n, n, 0), _iota2(n, n, 1)
    eye = (r == c).astype(F32)
    blk = jnp.right_shift(r, 4) == jnp.right_shift(c, 4)
    d = jnp.where(blk, a, 0.0)
    off = a - d
    d2 = _nn(d, d)
    d4 = _nn(d2, d2)
    d8 = _nn(d4, d4)
    td = _nn(_nn(_nn(eye - d, eye + d2), eye + d4), eye + d8)
    m = _nn(td, off)
    m2 = _nn(m, m)
    return _nn(_nn(eye - m, eye + m2), td)


@jax.custom_vjp
def _tri_solve(a, inv, rhs):
    return _nn(inv, rhs)


def _tri_solve_fwd(a, inv, rhs):
    sol = _nn(inv, rhs)
    return sol, (inv, sol)


def _tri_solve_bwd(res, dsol):
    inv, sol = res
    drhs = _nn(inv.T, dsol)
    return -_nt(drhs, sol), jnp.zeros_like(inv), drhs


_tri_solve.defvjp(_tri_solve_fwd, _tri_solve_bwd)


def _gdn_chunk(qa, ka, va, gate, a_raw, b_raw, s, a_log, dt_bias, norm_w, valid, inv=None, want_inv=False):
    c = qa.shape[0]
    q = qa * lax.rsqrt(jnp.sum(qa * qa, axis=-1, keepdims=True) + L2_EPS) * (GDN_D ** -0.5)
    k = ka * lax.rsqrt(jnp.sum(ka * ka, axis=-1, keepdims=True) + L2_EPS)
    beta = jax.nn.sigmoid(b_raw)
    g = -jnp.exp(a_log) * _softplus(a_raw + dt_bias) * valid
    gam = _cumsum_col(g)
    gam_row = _row_of(gam)
    r, cc = _iota2(c, c, 0), _iota2(c, c, 1)
    decay = jnp.exp(jnp.where(r >= cc, gam - gam_row, NEG))
    kb = k * beta
    a = jnp.where(r > cc, _lo_nt(kb, k) * decay, 0.0)
    egam = jnp.exp(gam)
    if inv is None:
        inv = _tri_inv(lax.stop_gradient(a))
    sol = _tri_solve(a, inv, jnp.concatenate([va * beta, kb * egam], axis=1))
    u = sol[:, :GDN_D]
    w = sol[:, GDN_D:]
    attn = _lo_nt(q, k) * decay
    g_last = jnp.sum(g, axis=0, keepdims=True)
    k_tail = k * jnp.exp(g_last - gam)
    v_new = u - _lo_nn(w, s)
    o = _lo_nn(q * egam, s) + _lo_nn(attn, v_new)
    s_new = s * jnp.exp(g_last) + _lo_tn(k_tail, v_new)
    y = o * lax.rsqrt(jnp.mean(o * o, axis=-1, keepdims=True) + RMS_EPS) * norm_w * _silu(gate)
    return (y, s_new, inv) if want_inv else (y, s_new)


def _valid_col(row0, n):
    return (row0 + _iota2(n, 1, 0) >= NPAD).astype(F32)


GDN_HB = GDN_H

SM_B, SM_A, SM_DT, SM_W = 0, 8, 16, 128


def _pick_cols(sm, first, n):
    return jnp.stack([sm[:, first + j:first + j + 1] for j in range(n)])


def _spread_cols(cols, first):
    lane = _iota2(1, SM_W, 1)
    out = None
    for j in range(cols.shape[0]):
        term = cols[j] * (lane == first + j).astype(F32)
        out = term if out is None else out + term
    return out


def _widen(t, width):
    if width == t.shape[1]:
        return t
    return jnp.concatenate([t, jnp.zeros((t.shape[0], width - t.shape[1]), t.dtype)], axis=1)


def _gdn_specs(nc, rev):
    ci = (lambda i: nc - 1 - i) if rev else (lambda i: i)
    hb = GDN_HB
    tile = pl.BlockSpec((GDN_C, hb * GDN_D), lambda h, i: (ci(i), h))
    col = pl.BlockSpec((GDN_C, SM_W), lambda h, i: (ci(i), 0))
    scal = pl.BlockSpec((hb, 1, 1), lambda h, i: (h, 0, 0))
    nw = pl.BlockSpec((1, GDN_D), lambda h, i: (0, 0))
    st = pl.BlockSpec((hb, 1, GDN_D, GDN_D), lambda h, i: (h, ci(i), 0, 0))
    return tile, col, scal, nw, st


def _lanes(j):
    return slice(j * GDN_D, (j + 1) * GDN_D)


def _by_head(ref):
    return jnp.stack([ref[:, _lanes(j)] for j in range(GDN_HB)])


def _gdn_fwd_call(q, k, v, gate, small, a_log, dt_bias, norm_w, shards=()):
    seq = q.shape[0]
    nc = seq // GDN_C
    ns = len(shards)
    tile, col, scal, nw, st = _gdn_specs(nc, False)

    def body(*refs):
        q_ref, k_ref, v_ref, g_ref, sm_ref, al_ref, dt_ref, nw_ref = refs[:8]
        y_ref, st_ref, inv_ref = refs[8 + ns:11 + ns]
        s_scr = refs[11 + 2 * ns]
        i = pl.program_id(1)
        if ns:
            start, relay, finish = _gather_phases(refs[8:8 + ns], refs[11 + ns:11 + 2 * ns], *refs[12 + 2 * ns:])
            pl.when(i == 0)(start)
            pl.when(i == nc - 1)(relay)

        @pl.when(i == 0)
        def _():
            s_scr[...] = jnp.zeros_like(s_scr)

        s = s_scr[...]
        st_ref[:, 0] = s
        sm = sm_ref[...]
        fn = jax.vmap(functools.partial(_gdn_chunk, valid=_valid_col(i * GDN_C, GDN_C), want_inv=True))
        y, s_new, inv = fn(_by_head(q_ref), _by_head(k_ref), _by_head(v_ref), _by_head(g_ref),
                           _pick_cols(sm, SM_A, GDN_H), _pick_cols(sm, SM_B, GDN_H), s,
                           al_ref[...], dt_ref[...], jnp.broadcast_to(nw_ref[...], (GDN_HB, 1, GDN_D)))
        for j in range(GDN_HB):
            y_ref[:, _lanes(j)] = y[j]
        inv_ref[:, 0] = inv
        s_scr[...] = s_new
        if ns:
            pl.when(i == nc - 1)(finish)

    return pl.pallas_call(
        body, name="gdn_fwd", grid=(GDN_H // GDN_HB, nc),
        in_specs=[tile, tile, tile, tile, col, scal, scal, nw] + [_ANY] * ns,
        out_specs=[tile, st, pl.BlockSpec((GDN_HB, 1, GDN_C, GDN_C), lambda h, i: (h, i, 0, 0))] + [_ANY] * ns,
        out_shape=[jax.ShapeDtypeStruct((seq, GDN_H * GDN_D), F32),
                   jax.ShapeDtypeStruct((GDN_H, nc, GDN_D, GDN_D), F32),
                   jax.ShapeDtypeStruct((GDN_H, nc, GDN_C, GDN_C), F32)] + _gather_out_shapes(shards),
        scratch_shapes=[pltpu.VMEM((GDN_HB, GDN_D, GDN_D), F32)] + (_gather_sems(ns) if ns else []),
        compiler_params=_cp(("parallel", "arbitrary")),
    )(q, k, v, gate, small, a_log, dt_bias, norm_w, *shards)


def _gdn_bwd_call(q, k, v, gate, small, a_log, dt_bias, norm_w, states, invs, dy, outgoing=()):
    seq = q.shape[0]
    nc = seq // GDN_C
    no = len(outgoing)
    tile, col, scal, nw, st = _gdn_specs(nc, True)
    nwh = pl.BlockSpec((GDN_HB, 1, GDN_D), lambda h, i: (h, 0, 0))
    inv_spec = pl.BlockSpec((GDN_HB, 1, GDN_C, GDN_C), lambda h, i: (h, nc - 1 - i, 0, 0))

    def body(*refs):
        q_ref, k_ref, v_ref, g_ref, sm_ref, al_ref, dt_ref, nw_ref, st_ref, inv_ref, dy_ref = refs[:11]
        dq_ref, dk_ref, dv_ref, dg_ref, dsm_ref, dal_ref, ddt_ref, dnw_ref = refs[11 + no:19 + no]
        ds_scr = refs[19 + 2 * no]
        i = pl.program_id(1)
        if no:
            start, finish = _chip_exchange_phases(refs[11:11 + no], refs[19 + no:19 + 2 * no], *refs[20 + 2 * no:])
            pl.when(i == 0)(start)

        @pl.when(i == 0)
        def _():
            ds_scr[...] = jnp.zeros_like(ds_scr)
            dal_ref[...] = jnp.zeros_like(dal_ref)
            ddt_ref[...] = jnp.zeros_like(ddt_ref)
            dnw_ref[...] = jnp.zeros_like(dnw_ref)

        sm = sm_ref[...]
        valid = _valid_col((nc - 1 - i) * GDN_C, GDN_C)
        kept = inv_ref[:, 0]

        def fn(*heads):
            return jax.vmap(lambda *t: _gdn_chunk(*t[:-1], valid=valid, inv=t[-1]))(*heads, kept)

        _, vjp = jax.vjp(fn, _by_head(q_ref), _by_head(k_ref), _by_head(v_ref), _by_head(g_ref),
                         _pick_cols(sm, SM_A, GDN_H), _pick_cols(sm, SM_B, GDN_H), st_ref[:, 0], al_ref[...],
                         dt_ref[...], jnp.broadcast_to(nw_ref[...], (GDN_HB, 1, GDN_D)))
        dq, dk, dv, dg, da, db, ds, dal, ddt, dnw = vjp((_by_head(dy_ref), ds_scr[...]))
        for j in range(GDN_HB):
            dq_ref[:, _lanes(j)] = dq[j]
            dk_ref[:, _lanes(j)] = dk[j]
            dv_ref[:, _lanes(j)] = dv[j]
            dg_ref[:, _lanes(j)] = dg[j]
        dsm_ref[...] = _widen(_spread_cols(da, SM_A) + _spread_cols(db, SM_B), dsm_ref.shape[1])
        ds_scr[...] = ds
        dal_ref[...] += dal
        ddt_ref[...] += ddt
        dnw_ref[...] += dnw
        if no:
            pl.when(i == nc - 1)(finish)

    big = jax.ShapeDtypeStruct((seq, GDN_H * GDN_D), F32)
    return pl.pallas_call(
        body, name="gdn_bwd", grid=(GDN_H // GDN_HB, nc),
        in_specs=[tile, tile, tile, tile, col, scal, scal, nw, st, inv_spec, tile] + [_ANY] * no,
        out_specs=[tile, tile, tile, tile, pl.BlockSpec((GDN_C, small.shape[1]), lambda h, i: (nc - 1 - i, 0)),
                   scal, scal, nwh] + [_ANY] * no,
        out_shape=[big, big, big, big, jax.ShapeDtypeStruct(small.shape, F32),
                   jax.ShapeDtypeStruct((GDN_H, 1, 1), F32), jax.ShapeDtypeStruct((GDN_H, 1, 1), F32),
                   jax.ShapeDtypeStruct((GDN_H, 1, GDN_D), F32)] + _chip_exchange_out_shapes(outgoing),
        scratch_shapes=[pltpu.VMEM((GDN_HB, GDN_D, GDN_D), F32)] + (_chip_exchange_sems(no) if no else []),
        compiler_params=_cp(("parallel", "arbitrary")),
    )(q, k, v, gate, small, a_log, dt_bias, norm_w, states, invs, dy, *outgoing)


@jax.custom_vjp
def gdn_core(q, k, v, gate, small, a_log, dt_bias, norm_w, shards, slots):
    y, _, _, *gathered = _gdn_fwd_call(q, k, v, gate, small, a_log, dt_bias, norm_w, shards)
    return y, tuple(gathered), tuple(jnp.zeros((4, *s.shape[1:]), s.dtype) for s in slots)


def _gdn_core_fwd(q, k, v, gate, small, a_log, dt_bias, norm_w, shards, slots):
    y, states, invs, *gathered = _gdn_fwd_call(q, k, v, gate, small, a_log, dt_bias, norm_w, shards)
    out = (y, tuple(gathered), tuple(jnp.zeros((4, *s.shape[1:]), s.dtype) for s in slots))
    return out, (q, k, v, gate, small, a_log, dt_bias, norm_w, states, invs, shards)


def _gdn_core_bwd(res, cts):
    *args, shards = res
    dy, _, outgoing = cts
    dq, dk, dv, dg, dsm, dal, ddt, dnw, *incoming = _gdn_bwd_call(*args, dy, outgoing)
    return (dq, dk, dv, dg, dsm, dal, ddt, jnp.sum(dnw, axis=0), tuple(jnp.zeros_like(s) for s in shards),
            tuple(incoming))


gdn_core.defvjp(_gdn_core_fwd, _gdn_core_bwd)


def _ssd_head(x, z, dt_raw, h, dt_bias, a_log, d_skip, bm, cm, cb, valid):
    c = bm.shape[0]
    r, cc = _iota2(c, c, 0), _iota2(c, c, 1)
    dtp = _softplus(dt_raw + dt_bias)
    x = x * valid
    adt = -jnp.exp(a_log) * dtp * valid
    xdt = x * dtp
    acum = _cumsum_col(adt)
    lmat = jnp.exp(jnp.where(r >= cc, acum - _row_of(acum), NEG))
    a_last = jnp.sum(adt, axis=0, keepdims=True)
    y = _lo_nn(cb * lmat, xdt) + _lo_nt(cm * jnp.exp(acum), h) + d_skip * x
    h_new = h * jnp.exp(a_last) + _lo_tn(xdt * jnp.exp(a_last - acum), bm)
    return y * _silu(z), h_new


SSD_SIDE = SSD_H


def _ssd_chunk(xs, z, bm, cm, dt_raw, h, dt_bias, a_log, d_skip, norm_w, valid):
    nh, c, p = xs.shape
    ng = bm.shape[0]
    hpg = nh // ng
    bm = bm * valid
    cm = cm * valid
    cb = jax.vmap(_lo_nt)(cm, bm)
    per_head = lambda t: jnp.repeat(t, hpg, axis=0)
    args = (xs, z, dt_raw, h, dt_bias, a_log, d_skip, per_head(bm), per_head(cm), per_head(cb))
    outs = [jax.vmap(functools.partial(_ssd_head, valid=valid))(*[t[s:s + SSD_SIDE] for t in args])
            for s in range(0, nh, SSD_SIDE)]
    ys = jnp.concatenate([o[0] for o in outs], axis=0)
    hs = jnp.concatenate([o[1] for o in outs], axis=0)
    ss = jnp.sum(jnp.sum(ys * ys, axis=-1, keepdims=True).reshape(ng, hpg, c, 1), axis=1, keepdims=True)
    rstd = lax.rsqrt(ss / (hpg * p) + RMS_EPS)
    return (ys.reshape(ng, hpg, c, p) * rstd).reshape(nh, c, p) * norm_w, hs


SSD_INNER = SSD_H * SSD_P
SSD_BC = SSD_G * SSD_N


def _split_lanes(t, n, w):
    return jnp.stack([t[:, j * w:(j + 1) * w] for j in range(n)])


def _join_lanes(t):
    return jnp.concatenate([t[j] for j in range(t.shape[0])], axis=1)


def _ssd_specs(nc, rev):
    ci = (lambda i: nc - 1 - i) if rev else (lambda i: i)
    wide = pl.BlockSpec((BLK, SSD_INNER), lambda i: (ci(i), 0))
    bmat = pl.BlockSpec((BLK, SSD_BC), lambda i: (ci(i), SSD_INNER // SSD_BC))
    cmat = pl.BlockSpec((BLK, SSD_BC), lambda i: (ci(i), SSD_INNER // SSD_BC + 1))
    xbc = pl.BlockSpec((BLK, SSD_INNER + 2 * SSD_BC), lambda i: (ci(i), 0))
    col = pl.BlockSpec((BLK, SM_W), lambda i: (ci(i), 0))
    scal = pl.BlockSpec((SSD_H, 1, 1), lambda i: (0, 0, 0))
    nw = pl.BlockSpec((SSD_H, 1, SSD_P), lambda i: (0, 0, 0))
    st = pl.BlockSpec((SSD_H, 1, SSD_P, SSD_N), lambda i: (0, ci(i), 0, 0))
    return wide, bmat, cmat, xbc, col, scal, nw, st


def _ssd_fwd_call(xbc, z, small, dt_bias, a_log, d_skip, norm_w, shards=()):
    seq = z.shape[0]
    nc = seq // BLK
    ns = len(shards)
    wide, bmat, cmat, _, col, scal, nw, st = _ssd_specs(nc, False)

    def body(*refs):
        x_ref, b_ref, c_ref, z_ref, sm_ref, db_ref, al_ref, ds_ref, nw_ref = refs[:9]
        y_ref, st_ref = refs[9 + ns:11 + ns]
        h_scr = refs[11 + 2 * ns]
        i = pl.program_id(0)
        if ns:
            start, relay, finish = _gather_phases(refs[9:9 + ns], refs[11 + ns:11 + 2 * ns], *refs[12 + 2 * ns:])
            pl.when(i == 0)(start)
            pl.when(i == nc - 1)(relay)

        @pl.when(i == 0)
        def _():
            h_scr[...] = jnp.zeros_like(h_scr)

        h = h_scr[...]
        st_ref[:, 0] = h
        y, h_new = _ssd_chunk(_split_lanes(x_ref[...], SSD_H, SSD_P), _split_lanes(z_ref[...], SSD_H, SSD_P),
                              _split_lanes(b_ref[...], SSD_G, SSD_N), _split_lanes(c_ref[...], SSD_G, SSD_N),
                              _pick_cols(sm_ref[...], SM_DT, SSD_H), h, db_ref[...], al_ref[...], ds_ref[...],
                              nw_ref[...], _valid_col(i * BLK, BLK))
        y_ref[...] = _join_lanes(y)
        h_scr[...] = h_new
        if ns:
            pl.when(i == nc - 1)(finish)

    return pl.pallas_call(
        body, name="ssd_fwd", grid=(nc,),
        in_specs=[wide, bmat, cmat, wide, col, scal, scal, scal, nw] + [_ANY] * ns,
        out_specs=[wide, st] + [_ANY] * ns,
        out_shape=[jax.ShapeDtypeStruct((seq, SSD_INNER), F32),
                   jax.ShapeDtypeStruct((SSD_H, nc, SSD_P, SSD_N), F32)] + _gather_out_shapes(shards),
        scratch_shapes=[pltpu.VMEM((SSD_H, SSD_P, SSD_N), F32)] + (_gather_sems(ns) if ns else []),
        compiler_params=_cp(("arbitrary",)),
    )(xbc, xbc, xbc, z, small, dt_bias, a_log, d_skip, norm_w, *shards)


def _ssd_bwd_call(xbc, z, small, dt_bias, a_log, d_skip, norm_w, states, dy):
    seq = z.shape[0]
    nc = seq // BLK
    wide, bmat, cmat, xbc_spec, col, scal, nw, st = _ssd_specs(nc, True)

    def body(x_ref, b_ref, c_ref, z_ref, sm_ref, db_ref, al_ref, ds_ref, nw_ref, st_ref, dy_ref,
             dxbc_ref, dz_ref, dsm_ref, ddb_ref, dal_ref, dds_ref, dnw_ref, dh_scr):
        i = pl.program_id(0)

        @pl.when(i == 0)
        def _():
            dh_scr[...] = jnp.zeros_like(dh_scr)
            ddb_ref[...] = jnp.zeros_like(ddb_ref)
            dal_ref[...] = jnp.zeros_like(dal_ref)
            dds_ref[...] = jnp.zeros_like(dds_ref)
            dnw_ref[...] = jnp.zeros_like(dnw_ref)

        fn = functools.partial(_ssd_chunk, valid=_valid_col((nc - 1 - i) * BLK, BLK))
        _, vjp = jax.vjp(fn, _split_lanes(x_ref[...], SSD_H, SSD_P), _split_lanes(z_ref[...], SSD_H, SSD_P),
                         _split_lanes(b_ref[...], SSD_G, SSD_N), _split_lanes(c_ref[...], SSD_G, SSD_N),
                         _pick_cols(sm_ref[...], SM_DT, SSD_H), st_ref[:, 0], db_ref[...], al_ref[...], ds_ref[...],
                         nw_ref[...])
        dx, dz, dbm, dcm, ddt, dh, ddb, dal, dds, dnw = vjp((_split_lanes(dy_ref[...], SSD_H, SSD_P), dh_scr[...]))
        dxbc_ref[:, :SSD_INNER] = _join_lanes(dx)
        dxbc_ref[:, SSD_INNER:SSD_INNER + SSD_BC] = _join_lanes(dbm)
        dxbc_ref[:, SSD_INNER + SSD_BC:] = _join_lanes(dcm)
        dz_ref[...] = _join_lanes(dz)
        dsm_ref[...] = _widen(_spread_cols(ddt, SM_DT), dsm_ref.shape[1])
        dh_scr[...] = dh
        ddb_ref[...] += ddb
        dal_ref[...] += dal
        dds_ref[...] += dds
        dnw_ref[...] += dnw

    sshape = jax.ShapeDtypeStruct((SSD_H, 1, 1), F32)
    return pl.pallas_call(
        body, name="ssd_bwd", grid=(nc,),
        in_specs=[wide, bmat, cmat, wide, col, scal, scal, scal, nw, st, wide],
        out_specs=[xbc_spec, wide, pl.BlockSpec((BLK, small.shape[1]), lambda i: (nc - 1 - i, 0)), scal, scal, scal, nw],
        out_shape=[jax.ShapeDtypeStruct(xbc.shape, F32), jax.ShapeDtypeStruct(z.shape, F32),
                   jax.ShapeDtypeStruct(small.shape, F32), sshape, sshape, sshape,
                   jax.ShapeDtypeStruct((SSD_H, 1, SSD_P), F32)],
        scratch_shapes=[pltpu.VMEM((SSD_H, SSD_P, SSD_N), F32)],
        compiler_params=_cp(("arbitrary",)),
    )(xbc, xbc, xbc, z, small, dt_bias, a_log, d_skip, norm_w, states, dy)


@jax.custom_vjp
def ssd_core(xbc, z, small, dt_bias, a_log, d_skip, norm_w, shards):
    y, _, *gathered = _ssd_fwd_call(xbc, z, small, dt_bias, a_log, d_skip, norm_w, shards)
    return y, tuple(gathered)


def _ssd_core_fwd(*args):
    y, states, *gathered = _ssd_fwd_call(*args)
    return (y, tuple(gathered)), (*args[:-1], states, args[-1])


def _ssd_core_bwd(res, cts):
    *args, shards = res
    return (*_ssd_bwd_call(*args, cts[0]), tuple(jnp.zeros_like(s) for s in shards))


ssd_core.defvjp(_ssd_core_fwd, _ssd_core_bwd)


def _swa_block(q, km, kp, kc, vm, vp, vc, sink, n):
    rows = SWA_REP * BLK
    qs = q.reshape(rows, SWA_D) * (SWA_D ** -0.5)
    s = _lo_nt(qs, jnp.concatenate([km, kp, kc], axis=0))
    i = jnp.bitwise_and(_iota2(rows, 3 * BLK, 0), BLK - 1)
    col = _iota2(rows, 3 * BLK, 1)
    j = jnp.bitwise_and(col, BLK - 1)
    part = jnp.right_shift(col, 7)
    ok_m = (part == 0) & (j >= NPAD) & ((n >= 1) | (j <= i))
    ok_p = (part == 1) & (n >= 2) & (j > i)
    ok_c = (part == 2) & (n >= 1) & (j <= i)
    ok = ok_m | ok_p | ok_c
    s = jnp.where(ok, s, NEG)
    snk = jnp.concatenate([jnp.broadcast_to(sink[r], (BLK, 1)) for r in range(SWA_REP)], axis=0)
    m = lax.stop_gradient(jnp.maximum(jnp.max(s, axis=-1, keepdims=True), snk))
    e = jnp.exp(s - m)
    p = e / (jnp.sum(e, axis=-1, keepdims=True) + jnp.exp(snk - m))
    o = _lo_nn(p, jnp.concatenate([vm, vp, vc], axis=0))
    return o.reshape(SWA_REP, BLK, SWA_D)


SWA_QW = SWA_QH * SWA_D
SWA_KW = SWA_KVH * SWA_D


def _swa_specs(nb, rev):
    ci = (lambda i: nb - 1 - i) if rev else (lambda i: i)
    qsp = pl.BlockSpec((BLK, SWA_QW), lambda i: (ci(i), 0))
    cur = pl.BlockSpec((BLK, 2 * SWA_KW), lambda i: (ci(i), 0))
    prev = pl.BlockSpec((BLK, 2 * SWA_KW), lambda i: (jnp.maximum(ci(i) - 1, 0), 0))
    meta = pl.BlockSpec((BLK, 2 * SWA_KW), lambda i: (0, 0))
    scal = pl.BlockSpec((SWA_QH, 1, 1), lambda i: (0, 0, 0))
    return qsp, cur, prev, meta, scal


def _swa_by_head(q, kvm, kvp, kvc, sink):
    def kv(t):
        return _split_lanes(t[:, :SWA_KW], SWA_KVH, SWA_D), _split_lanes(t[:, SWA_KW:], SWA_KVH, SWA_D)

    (km, vm), (kp, vp), (kc, vc) = kv(kvm), kv(kvp), kv(kvc)
    qh = _split_lanes(q, SWA_QH, SWA_D).reshape(SWA_KVH, SWA_REP, BLK, SWA_D)
    return qh, km, kp, kc, vm, vp, vc, sink.reshape(SWA_KVH, SWA_REP, 1, 1)


def _swa_kv_tile(dk, dv):
    return jnp.concatenate([_join_lanes(dk), _join_lanes(dv)], axis=1)


def _swa_fwd_call(q, kv, sink):
    seq = q.shape[0]
    nb = seq // BLK
    qsp, cur, prev, meta, scal = _swa_specs(nb, False)

    def body(q_ref, m_ref, p_ref, c_ref, s_ref, o_ref):
        fn = jax.vmap(functools.partial(_swa_block, n=pl.program_id(0)))
        o = fn(*_swa_by_head(q_ref[...], m_ref[...], p_ref[...], c_ref[...], s_ref[...]))
        o_ref[...] = _join_lanes(o.reshape(SWA_QH, BLK, SWA_D))

    return pl.pallas_call(
        body, name="swa_fwd", grid=(nb,),
        in_specs=[qsp, meta, prev, cur, scal],
        out_specs=qsp,
        out_shape=jax.ShapeDtypeStruct(q.shape, F32),
        compiler_params=_cp(("parallel",)),
    )(q, kv, kv, kv, sink)


def _swa_bwd_call(q, kv, sink, do):
    seq = q.shape[0]
    nb = seq // BLK
    qsp, cur, prev, meta, scal = _swa_specs(nb, True)

    def body(q_ref, m_ref, p_ref, c_ref, s_ref, do_ref, dq_ref, dkv_ref, ds_ref, prev_scr, meta_scr):
        i = pl.program_id(0)
        n = nb - 1 - i

        @pl.when(i == 0)
        def _():
            prev_scr[...] = jnp.zeros_like(prev_scr)
            meta_scr[...] = jnp.zeros_like(meta_scr)
            ds_ref[...] = jnp.zeros_like(ds_ref)

        fn = jax.vmap(functools.partial(_swa_block, n=n))
        _, vjp = jax.vjp(fn, *_swa_by_head(q_ref[...], m_ref[...], p_ref[...], c_ref[...], s_ref[...]))
        do = _split_lanes(do_ref[...], SWA_QH, SWA_D).reshape(SWA_KVH, SWA_REP, BLK, SWA_D)
        dq, dkm, dkp, dkc, dvm, dvp, dvc, dsk = vjp(do)
        dq_ref[...] = _join_lanes(dq.reshape(SWA_QH, BLK, SWA_D))
        ds_ref[...] += dsk.reshape(SWA_QH, 1, 1)
        meta_scr[...] += _swa_kv_tile(dkm, dvm)
        first = (n == 0).astype(F32)
        dkv_ref[...] = _swa_kv_tile(dkc, dvc) + prev_scr[...] + first * meta_scr[...]
        prev_scr[...] = _swa_kv_tile(dkp, dvp)

    return pl.pallas_call(
        body, name="swa_bwd", grid=(nb,),
        in_specs=[qsp, meta, prev, cur, scal, qsp],
        out_specs=[qsp, cur, scal],
        out_shape=[jax.ShapeDtypeStruct(q.shape, F32), jax.ShapeDtypeStruct(kv.shape, F32),
                   jax.ShapeDtypeStruct(sink.shape, F32)],
        scratch_shapes=[pltpu.VMEM((BLK, 2 * SWA_KW), F32)] * 2,
        compiler_params=_cp(("arbitrary",)),
    )(q, kv, kv, kv, sink, do)


@jax.custom_vjp
def swa_core(q, kv, sink):
    return _swa_fwd_call(q, kv, sink)


def _swa_core_fwd(q, kv, sink):
    return _swa_fwd_call(q, kv, sink), (q, kv, sink)


def _swa_core_bwd(res, do):
    return tuple(_swa_bwd_call(*res, do))


swa_core.defvjp(_swa_core_fwd, _swa_core_bwd)


def _tile(n, pref):
    if n <= pref:
        return n
    best = None
    for t in range(128, pref + 1, 128):
        if n % t == 0:
            best = t
    assert best is not None, (n, pref)
    return best


MM_TILE_BYTES = 9 * 1024 * 1024


def _mm_tiles(m, n, kk, a_bytes, b_bytes):
    if kk > 8192:
        return _tile(m, 2816 // a_bytes), _tile(n, 512), _tile(kk, 4096)
    if kk > 1408 and _tile(m, 1024) * kk * a_bytes <= MM_TILE_BYTES:
        return _tile(m, 1024), _tile(n, 1024 if 1024 * kk * b_bytes <= MM_TILE_BYTES else 512), kk
    return _tile(m, 1408), _tile(n, 1024 if kk <= 1408 else 512), _tile(kk, 1408)


def _mm_call(a, b, name):
    (m, kk), n = a.shape, b.shape[1]
    tm, tn, tk = _mm_tiles(m, n, kk, a.dtype.itemsize, b.dtype.itemsize)
    nk = kk // tk
    a_spec = pl.BlockSpec((tm, tk), lambda i, j, k: (i, k))
    b_spec = pl.BlockSpec((tk, tn), lambda i, j, k: (k, j))

    def body(a_ref, b_ref, o_ref, *acc):
        k = pl.program_id(2)
        part = jnp.dot(a_ref[...].astype(BF16), b_ref[...].astype(BF16), preferred_element_type=F32)
        if nk == 1:
            o_ref[...] = part
            return
        acc_ref, = acc

        @pl.when(k == 0)
        def _():
            acc_ref[...] = part

        @pl.when(k > 0)
        def _():
            acc_ref[...] += part

        @pl.when(k == nk - 1)
        def _():
            o_ref[...] = acc_ref[...]

    return pl.pallas_call(
        body, name=name, grid=(m // tm, n // tn, nk),
        in_specs=[a_spec, b_spec],
        out_specs=pl.BlockSpec((tm, tn), lambda i, j, k: (i, j)),
        out_shape=jax.ShapeDtypeStruct((m, n), F32),
        scratch_shapes=[pltpu.VMEM((tm, tn), F32)] if nk > 1 else [],
        compiler_params=_cp(("parallel", "parallel", "arbitrary")),
    )(a, b)


@jax.custom_vjp
def mm(a, b, b_t, grad_slot):
    return _mm_call(a, b, "mm_fwd")


def _mm_fwd(a, b, b_t, grad_slot):
    return _mm_call(a, b, "mm_fwd"), (a, b, b_t)


def _mm_bwd(res, dc):
    a, b, b_t = res
    return (_mm_call(dc, b_t, "mm_dx"), jnp.zeros_like(b), jnp.zeros_like(b_t),
            _mm_call(a.astype(BF16).T, dc, "mm_dw"))


mm.defvjp(_mm_fwd, _mm_bwd)


_SPLIT = (1024, 1024, 1024, 1024, 1024, 2048, 1024, 512, 3072, 512)


def _split_cols(u):
    offs = [sum(_SPLIT[:i]) for i in range(len(_SPLIT))]
    return tuple(u[:, o:o + s] for o, s in zip(offs, _SPLIT))


@jax.custom_vjp
def mm_split(a, b, b_t, grad_slot):
    return _split_cols(_mm_call(a, b, "mm_fwd"))


def _mm_split_fwd(a, b, b_t, grad_slot):
    return _split_cols(_mm_call(a, b, "mm_fwd")), (a, b, b_t)


def _mm_split_bwd(res, cts):
    return _mm_bwd(res, jnp.concatenate([c.astype(BF16) for c in cts], axis=1))


mm_split.defvjp(_mm_split_fwd, _mm_split_bwd)


def _row_specs(arrs, tr):
    return [pl.BlockSpec((tr, a.shape[1]), lambda i: (i, 0)) for a in arrs]


def _par_specs(arrs):
    return [pl.BlockSpec(a.shape, lambda i: (0, 0)) for a in arrs]


def _row_fwd_call(fn, rows, params, out_cols, tr, name):
    seq = rows[0].shape[0]
    nr = len(rows)

    def body(*refs):
        vals = [r[...] for r in refs[:-1]]
        refs[-1][...] = fn(*vals)

    return pl.pallas_call(
        body, name=name, grid=(seq // tr,),
        in_specs=_row_specs(rows, tr) + _par_specs(params),
        out_specs=pl.BlockSpec((tr, out_cols), lambda i: (i, 0)),
        out_shape=jax.ShapeDtypeStruct((seq, out_cols), F32),
        compiler_params=_cp(("parallel",)),
    )(*rows, *params)


def _row_bwd_call(fn, rows, params, dy, tr, name):
    seq = rows[0].shape[0]
    nr, npar = len(rows), len(params)

    def body(*refs):
        ins = refs[:nr + npar]
        dy_ref = refs[nr + npar]
        outs = refs[nr + npar + 1:]
        _, vjp = jax.vjp(fn, *[r[...] for r in ins])
        cts = vjp(dy_ref[...])
        for o_ref, ct in zip(outs[:nr], cts[:nr]):
            o_ref[...] = ct

        @pl.when(pl.program_id(0) == 0)
        def _():
            for o_ref in outs[nr:]:
                o_ref[...] = jnp.zeros_like(o_ref)

        for o_ref, ct in zip(outs[nr:], cts[nr:]):
            o_ref[...] += ct

    return pl.pallas_call(
        body, name=name, grid=(seq // tr,),
        in_specs=_row_specs(rows, tr) + _par_specs(params) + _row_specs([dy], tr),
        out_specs=_row_specs(rows, tr) + _par_specs(params),
        out_shape=[jax.ShapeDtypeStruct(a.shape, F32) for a in (*rows, *params)],
        compiler_params=_cp(("arbitrary",)),
    )(*rows, *params, dy)


def _make_rowop(fn, nrows, out_cols, tr, name):
    @jax.custom_vjp
    def op(*args):
        return _row_fwd_call(fn, args[:nrows], args[nrows:], out_cols, tr, name + "_fwd")

    def fwd(*args):
        return op(*args), args

    def bwd(args, dy):
        return tuple(_row_bwd_call(fn, args[:nrows], args[nrows:], dy, tr, name + "_bwd"))

    op.defvjp(fwd, bwd)
    return op


def _rms_fn(x, w):
    return x * lax.rsqrt(jnp.mean(x * x, axis=-1, keepdims=True) + RMS_EPS) * w


def _merge_fn(pa, pb, pc, gl):
    d = D_MODEL
    return (jax.nn.sigmoid(gl[:, :d]) * pa + jax.nn.sigmoid(gl[:, d:2 * d]) * pb
            + jax.nn.sigmoid(gl[:, 2 * d:]) * pc)


def _relu2_fn(a):
    r = jnp.maximum(a, 0.0)
    return r * r


rms_op = _make_rowop(_rms_fn, 1, D_MODEL, 384, "rms")
merge_op = _make_rowop(_merge_fn, 4, D_MODEL, 192, "merge")
relu2_op = _make_rowop(_relu2_fn, 1, D_FF, 192, "relu2")


def _conv_taps(xext, w, nrows):
    z = None
    for j in range(CONV_K):
        sh = CONV_K - 1 - j
        xs = pltpu.roll(xext, sh, 0) if sh else xext
        term = w[j:j + 1, :] * xs[8:8 + nrows, :]
        z = term if z is None else z + term
    return z


def _halo(ref, start, ok):
    return jnp.where(ok, ref[pl.ds(pl.multiple_of(start, 8), 8), :], 0.0)


def _conv_fwd_call(x, w, b):
    seq, ch = x.shape
    nb = seq // BLK

    def body(x_ref, w_ref, b_ref, o_ref):
        w = w_ref[...]
        bias = b_ref[...]

        def step(i, carry):
            r0 = pl.multiple_of(i * BLK, BLK)
            xext = jnp.concatenate([_halo(x_ref, jnp.maximum(r0 - 8, 0), i > 0), x_ref[pl.ds(r0, BLK), :]], axis=0)
            o_ref[pl.ds(r0, BLK), :] = _silu(_conv_taps(xext, w, BLK) + bias)
            return carry

        lax.fori_loop(0, nb, step, 0, unroll=3)

    strip = pl.BlockSpec((seq, CONV_W), lambda c: (0, c))
    return pl.pallas_call(
        body, name="conv_fwd", grid=(ch // CONV_W,),
        in_specs=[strip, pl.BlockSpec((CONV_K, CONV_W), lambda c: (0, c)), pl.BlockSpec((1, CONV_W), lambda c: (0, c))],
        out_specs=strip, out_shape=jax.ShapeDtypeStruct(x.shape, F32),
        compiler_params=_cp(("parallel",)),
    )(x, w, b)


def _conv_bwd_call(x, w, b, dy):
    seq, ch = x.shape
    nb = seq // BLK

    def body(x_ref, w_ref, b_ref, dy_ref, dx_ref, dw_ref, db_ref):
        w = w_ref[...]
        bias = b_ref[...]

        def step(i, carry):
            r0 = pl.multiple_of(i * BLK, BLK)
            last = i == nb - 1
            nxt = jnp.minimum(r0 + BLK, seq - 8)
            xext = jnp.concatenate([_halo(x_ref, jnp.maximum(r0 - 8, 0), i > 0), x_ref[pl.ds(r0, BLK), :],
                                    _halo(x_ref, nxt, jnp.logical_not(last))], axis=0)
            dyext = jnp.concatenate([dy_ref[pl.ds(r0, BLK), :], _halo(dy_ref, nxt, jnp.logical_not(last))], axis=0)
            z = _conv_taps(xext, w, BLK + 8) + bias
            sg = jax.nn.sigmoid(z)
            dz = dyext * (sg * (1.0 + z * (1.0 - sg)))
            dx = None
            for j in range(CONV_K):
                sh = CONV_K - 1 - j
                dzs = pltpu.roll(dz, BLK + 8 - sh, 0) if sh else dz
                term = w[j:j + 1, :] * dzs[:BLK, :]
                dx = term if dx is None else dx + term
            dx_ref[pl.ds(r0, BLK), :] = dx
            dzm = dz[:BLK, :]
            out = []
            for j in range(CONV_K):
                sh = CONV_K - 1 - j
                xs = pltpu.roll(xext, sh, 0) if sh else xext
                out.append(carry[j] + jnp.sum(dzm * xs[8:8 + BLK, :], axis=0, keepdims=True))
            out.append(carry[CONV_K] + jnp.sum(dzm, axis=0, keepdims=True))
            return tuple(out)

        zero = jnp.zeros((1, CONV_W), F32)
        acc = lax.fori_loop(0, nb, step, (zero,) * (CONV_K + 1), unroll=3)
        dw_ref[...] = jnp.concatenate(acc[:CONV_K], axis=0)
        db_ref[...] = acc[CONV_K]

    strip = pl.BlockSpec((seq, CONV_W), lambda c: (0, c))
    wsp = pl.BlockSpec((CONV_K, CONV_W), lambda c: (0, c))
    bsp = pl.BlockSpec((1, CONV_W), lambda c: (0, c))
    return pl.pallas_call(
        body, name="conv_bwd", grid=(ch // CONV_W,),
        in_specs=[strip, wsp, bsp, strip],
        out_specs=[strip, wsp, bsp],
        out_shape=[jax.ShapeDtypeStruct(x.shape, F32), jax.ShapeDtypeStruct(w.shape, F32),
                   jax.ShapeDtypeStruct(b.shape, F32)],
        compiler_params=_cp(("parallel",)),
    )(x, w, b, dy)


@jax.custom_vjp
def conv_silu(x, w, b):
    return _conv_fwd_call(x, w, b)


def _conv_silu_fwd(x, w, b):
    return _conv_fwd_call(x, w, b), (x, w, b)


def _conv_silu_bwd(res, dy):
    return tuple(_conv_bwd_call(*res, dy))


conv_silu.defvjp(_conv_silu_fwd, _conv_silu_bwd)


def _loss_call(h, wf, target):
    seq, d = h.shape
    nb = seq // BLK

    def body(h_ref, w_ref, t_ref, loss_ref, dh_ref, dw_ref):
        i = pl.program_id(0)
        live = (i > 0).astype(F32)
        tgt = t_ref[...]

        def fn(hh, ww):
            err = _rms_fn(hh, ww) - tgt
            return 0.5 * live * jnp.sum(jnp.mean(err * err, axis=-1, keepdims=True), axis=0, keepdims=True)

        val, vjp = jax.vjp(fn, h_ref[...], w_ref[...])
        dh, dw = vjp(jnp.ones((1, 1), F32))
        dh_ref[...] = dh

        @pl.when(i == 0)
        def _():
            loss_ref[...] = jnp.zeros_like(loss_ref)
            dw_ref[...] = jnp.zeros_like(dw_ref)

        loss_ref[...] += val
        dw_ref[...] += dw

    return pl.pallas_call(
        body, name="loss_head", grid=(nb,),
        in_specs=[pl.BlockSpec((BLK, d), lambda i: (i, 0)), pl.BlockSpec((1, d), lambda i: (0, 0)),
                  pl.BlockSpec((BLK, d), lambda i: (jnp.maximum(i - 1, 0), 0))],
        out_specs=[pl.BlockSpec((1, 1), lambda i: (0, 0)), pl.BlockSpec((BLK, d), lambda i: (i, 0)),
                   pl.BlockSpec((1, d), lambda i: (0, 0))],
        out_shape=[jax.ShapeDtypeStruct((1, 1), F32), jax.ShapeDtypeStruct(h.shape, F32),
                   jax.ShapeDtypeStruct((1, d), F32)],
        compiler_params=_cp(("arbitrary",)),
    )(h, wf, target)


def _make_loss_head(target):
    @jax.custom_vjp
    def head(h, wf):
        return _loss_call(h, wf, target)[0][0, 0]

    def fwd(h, wf):
        loss, dh, dw = _loss_call(h, wf, target)
        return loss[0, 0], (dh, dw)

    def bwd(res, g):
        return g * res[0], g * res[1]

    head.defvjp(fwd, bwd)
    return head


_IN_SEGS = (("q", 0, 1024), ("k", 1024, 1024), ("v", 2048, 1024), ("gate", 3072, 1024), ("z", 4112, 1024),
            ("xbc", 5136, 2048), ("cq", 7200, 1024), ("ck", 8224, 256), ("cv", 8480, 256), ("gl", 8736, 3072),
            ("b", 4096, 8), ("a", 4104, 8), ("dt", 7184, 16))
_IN_PAD = sum(_SPLIT) - sum(n for _, _, n in _IN_SEGS)


_MATMUL = ("w_in", "w_proj_gdn", "w_proj_ssd", "w_proj_swa", "w_out", "w_up", "w_down")
_LATE = _MATMUL[1:]


def _late_weights(gathered):
    g = dict(zip(_LATE, gathered))
    full = {n: g[n].reshape(D_MODEL, D_MODEL) for n in _LATE[:4]}
    full["w_up"] = g["w_up"].transpose(1, 0, 2).reshape(D_MODEL, D_FF)
    full["w_down"] = g["w_down"].reshape(D_FF, D_MODEL)
    full.update({n + "_t": t.T for n, t in list(full.items())})
    return full


def _layer(h, p, w_in, w_in_t, slot, late_shards, next_shards=(), exchange_slots=()):
    wb = {"w_in": w_in, "w_in_t": w_in_t}

    def proj(t, name):
        return mm(t, wb[name], wb[name + "_t"], slot[name])

    q_pre, k_pre, v_pre, gate, z, xbc_pre, cq, ckv, gl, small = mm_split(
        rms_op(h, p["norm1_w"].reshape(1, -1)), w_in, w_in_t, slot["w_in"])

    gcw = p["gdn_conv_w"]
    nob = jnp.zeros((1, GDN_H * GDN_D), F32)
    qa = conv_silu(q_pre, gcw[:, :1024], nob)
    ka = conv_silu(k_pre, gcw[:, 1024:2048], nob)
    va = conv_silu(v_pre, gcw[:, 2048:], nob)
    y_gdn, late, placeholders = gdn_core(
        qa, ka, va, gate, small, p["gdn_a_log"].reshape(GDN_H, 1, 1), p["gdn_dt_bias"].reshape(GDN_H, 1, 1),
        p["gdn_norm_w"].reshape(1, GDN_D), tuple(late_shards), tuple(exchange_slots))
    wb.update(_late_weights(late))

    xbc = conv_silu(xbc_pre, p["ssd_conv_w"], p["ssd_conv_b"].reshape(1, -1))
    y_ssd, gathered = ssd_core(xbc, z, small, p["ssd_dt_bias"].reshape(SSD_H, 1, 1),
                               p["ssd_a_log"].reshape(SSD_H, 1, 1), p["ssd_d"].reshape(SSD_H, 1, 1),
                               p["ssd_norm_w"].reshape(SSD_H, 1, SSD_P), tuple(next_shards))

    y_swa = swa_core(cq, ckv, p["swa_sinks"].reshape(SWA_QH, 1, 1))

    merged = merge_op(proj(y_gdn, "w_proj_gdn"), proj(y_ssd, "w_proj_ssd"), proj(y_swa, "w_proj_swa"), gl)
    h = h + proj(merged, "w_out")
    a1 = proj(rms_op(h, p["norm2_w"].reshape(1, -1)), "w_up")
    return h + proj(relu2_op(a1), "w_down"), gathered, placeholders


_PER_LAYER = ("norm1_w", "gdn_conv_w", "gdn_a_log", "gdn_dt_bias", "gdn_norm_w", "ssd_conv_w", "ssd_conv_b",
              "ssd_dt_bias", "ssd_a_log", "ssd_d", "ssd_norm_w", "swa_sinks", "norm2_w")


def _embed(x, meta):
    return jnp.concatenate([jnp.zeros((NPAD, D_MODEL), F32), meta, x], axis=0)


_IN_SHARD = 1476


def _in_pieces():
    out = []
    for _, s, n in _IN_SEGS:
        c = s
        while c < s + n:
            d = c // _IN_SHARD
            e = min(s + n, (d + 1) * _IN_SHARD)
            out.append((d, c - d * _IN_SHARD, e - d * _IN_SHARD))
            c = e
    return out


def _in_pieces_back():
    start, off = {}, 0
    for _, s, n in _IN_SEGS:
        start[s] = off
        off += n
    out = [[] for _ in range(N_DEV)]
    for _, s, n in sorted(_IN_SEGS, key=lambda t: t[1]):
        c = s
        while c < s + n:
            d = c // _IN_SHARD
            e = min(s + n, (d + 1) * _IN_SHARD)
            out[d].append((start[s] + c - s, start[s] + e - s))
            c = e
    return out


def _regroup_w_in(stacked):
    parts = [stacked[d, :, lo:hi] for d, lo, hi in _in_pieces()]
    return jnp.concatenate(parts + [jnp.zeros((D_MODEL, _IN_PAD), stacked.dtype)], axis=1)


def _ungroup_w_in(g):
    return [jnp.concatenate([g[:, lo:hi] for lo, hi in pieces], axis=1) for pieces in _in_pieces_back()]


def _position():
    return lax.axis_index("x"), lax.axis_index("y"), lax.axis_index("c")


_ANY = pl.BlockSpec(memory_space=pl.ANY)


def _chip_of(x, y, k):
    return (1 - x if k & 1 else x, 1 - y if k & 2 else y)


def _allgather_call(shards, name):
    n = len(shards)

    def body(*refs):
        start, relay, finish = _gather_phases(refs[:n], refs[n:2 * n], *refs[2 * n:])
        start()
        relay()
        finish()

    return pl.pallas_call(
        body, name=name,
        out_shape=_gather_out_shapes(shards),
        in_specs=[_ANY] * n, out_specs=[_ANY] * n,
        scratch_shapes=_gather_sems(n),
    )(*shards)


def _gather_out_shapes(shards):
    return [jax.ShapeDtypeStruct((N_DEV, *s.shape), s.dtype) for s in shards]


def _gather_sems(n):
    return [pltpu.SemaphoreType.DMA((7 * n,)), pltpu.SemaphoreType.DMA((7 * n,)), pltpu.SemaphoreType.DMA((n,))]


def _gather_phases(x_refs, out_refs, send_sems, recv_sems, local_sems):
    n = len(x_refs)
    x, y, c = _position()
    me, sibling = (x, y, c), (x, y, 1 - c)
    chips = [_chip_of(x, y, k) for k in (1, 2, 3)]

    def slab(a, px, py, pc):
        return out_refs[a].at[4 * px + 2 * py + pc]

    def copy(a, k, block, to, src=None):
        return pltpu.make_async_remote_copy(
            src_ref=slab(a, *block) if src is None else src, dst_ref=slab(a, *block),
            send_sem=send_sems.at[7 * a + k], recv_sem=recv_sems.at[7 * a + k], device_id=to, device_id_type=MESH)

    def mine():
        return [pltpu.make_async_copy(x_refs[a], slab(a, *me), local_sems.at[a]) for a in range(n)]

    def first():
        out = []
        for a in range(n):
            out.append(copy(a, 0, me, sibling, src=x_refs[a]))
            out += [copy(a, 1 + j, me, (*chip, c), src=x_refs[a]) for j, chip in enumerate(chips)]
        return out

    def passed():
        return [copy(a, 4 + j, (*chip, c), sibling) for j, chip in enumerate(chips) for a in range(n)]

    def start():
        for cp in mine() + first():
            cp.start()

    def relay():
        for j, chip in enumerate(chips):
            for a in range(n):
                copy(a, 1 + j, (*chip, c), me).wait_recv()
                copy(a, 4 + j, (*chip, c), sibling).start()

    def finish():
        for a in range(n):
            copy(a, 0, sibling, me).wait_recv()
        for j, chip in enumerate(chips):
            for a in range(n):
                copy(a, 4 + j, (*chip, 1 - c), me).wait_recv()
        for cp in first() + passed():
            cp.wait_send()
        for cp in mine():
            cp.wait()

    return start, relay, finish


def _sibling_exchange_call(for_c0, for_c1, name):
    n = len(for_c0)

    def body(*refs):
        c0_refs, c1_refs, out_refs = refs[:4 * n], refs[4 * n:8 * n], refs[8 * n:9 * n]
        send_sems, recv_sems = refs[9 * n:]
        x, y, c = _position()

        def copies(src_refs):
            return [pltpu.make_async_remote_copy(
                src_ref=src_refs[4 * a + q], dst_ref=out_refs[a].at[q],
                send_sem=send_sems.at[4 * a + q], recv_sem=recv_sems.at[4 * a + q],
                device_id=(x, y, 1 - c), device_id_type=MESH) for a in range(n) for q in range(4)]

        @pl.when(c == 0)
        def _():
            for cp in copies(c1_refs):
                cp.start()

        @pl.when(c == 1)
        def _():
            for cp in copies(c0_refs):
                cp.start()

        waits = copies(c0_refs)
        for cp in waits:
            cp.wait_recv()
        for cp in waits:
            cp.wait_send()

    return pl.pallas_call(
        body, name=name,
        out_shape=[jax.ShapeDtypeStruct((4, *g[0].shape), g[0].dtype) for g in for_c0],
        in_specs=[_ANY] * (8 * n), out_specs=[_ANY] * n,
        scratch_shapes=[pltpu.SemaphoreType.DMA((4 * n,)), pltpu.SemaphoreType.DMA((4 * n,))],
    )(*[t for g in for_c0 for t in g], *[t for g in for_c1 for t in g])


def _chip_exchange_call(partials, name):
    n = len(partials)

    def body(*refs):
        start, finish = _chip_exchange_phases(refs[:n], refs[n:2 * n], *refs[2 * n:])
        start()
        finish()

    return pl.pallas_call(
        body, name=name,
        out_shape=_chip_exchange_out_shapes(partials),
        in_specs=[_ANY] * n, out_specs=[_ANY] * n,
        scratch_shapes=_chip_exchange_sems(n),
    )(*partials)


def _chip_exchange_out_shapes(partials):
    return [jax.ShapeDtypeStruct((3, *p.shape[1:]), p.dtype) for p in partials]


def _chip_exchange_sems(n):
    return [pltpu.SemaphoreType.DMA((3 * n,)), pltpu.SemaphoreType.DMA((3 * n,))]


def _chip_exchange_phases(p_refs, out_refs, send_sems, recv_sems):
    n = len(p_refs)
    x, y, c = _position()

    def copies():
        out = []
        for a in range(n):
            for k in (1, 2, 3):
                px, py = _chip_of(x, y, k)
                out.append(pltpu.make_async_remote_copy(
                    src_ref=p_refs[a].at[2 * px + py], dst_ref=out_refs[a].at[k - 1],
                    send_sem=send_sems.at[3 * a + k - 1], recv_sem=recv_sems.at[3 * a + k - 1],
                    device_id=(px, py, c), device_id_type=MESH))
        return out

    def start():
        for cp in copies():
            cp.start()

    def finish():
        for cp in copies():
            cp.wait_recv()
        for cp in copies():
            cp.wait_send()

    return start, finish


def _chip_partial_call(for_c0, for_c1, sib, tr, name):
    _, r, c = sib.shape

    def body(*refs):
        c0_refs, c1_refs, (s_ref, own_ref, out_ref) = refs[:4], refs[4:8], refs[8:]
        x, y, core = _position()
        own = jnp.zeros((tr, c), F32)
        for q in range(4):
            partial = jnp.where(core == 0, c0_refs[q][...], c1_refs[q][...]) + s_ref[q]
            own = jnp.where(2 * x + y == q, partial, own)
            out_ref[q] = partial.astype(BF16)
        own_ref[...] = own

    one = pl.BlockSpec((tr, c), lambda i: (i, 0))
    four = pl.BlockSpec((4, tr, c), lambda i: (0, i, 0))
    return pl.pallas_call(
        body, name=name, grid=(r // tr,),
        in_specs=[one] * 8 + [four],
        out_specs=[one, four],
        out_shape=[jax.ShapeDtypeStruct((r, c), F32), jax.ShapeDtypeStruct((4, r, c), BF16)],
        compiler_params=_cp(("parallel",)),
    )(*for_c0, *for_c1, sib)


def _adamw_call(parts, w, m, v, tr, name):
    ns, r, c = w.shape
    counts = [len(p) for p in parts]
    flat_parts = [a for p in parts for a in p]

    def body(*refs):
        p_refs = refs[:len(flat_parts)]
        w_ref, m_ref, v_ref, g_ref, d_ref, nm_ref, nv_ref = refs[len(flat_parts):]
        at = 0
        for s in range(ns):
            g = None
            for p_ref in p_refs[at:at + counts[s]]:
                for j in range(p_ref.shape[0]):
                    term = p_ref[j].astype(F32)
                    g = term if g is None else g + term
            at += counts[s]
            nm = ADAM_B1 * m_ref[s] + (1.0 - ADAM_B1) * g
            nv = ADAM_B2 * v_ref[s] + (1.0 - ADAM_B2) * (g * g)
            m_hat = nm / (1.0 - ADAM_B1 ** ADAM_STEP)
            v_hat = nv / (1.0 - ADAM_B2 ** ADAM_STEP)
            g_ref[s] = g
            d_ref[s] = -ADAM_LR * (m_hat / (jnp.sqrt(v_hat) + ADAM_EPS) + ADAM_WD * w_ref[s])
            nm_ref[s] = nm
            nv_ref[s] = nv

    slabs = pl.BlockSpec((ns, tr, c), lambda i: (0, i, 0))
    return pl.pallas_call(
        body, name=name, grid=(r // tr,),
        in_specs=[pl.BlockSpec((a.shape[0], tr, c), lambda i: (0, i, 0)) for a in flat_parts] + [slabs] * 3,
        out_specs=[slabs] * 4,
        out_shape=[jax.ShapeDtypeStruct((ns, r, c), F32)] * 4,
        compiler_params=_cp(("parallel",)),
    )(*flat_parts, w, m, v)


_WEIGHTS = ("meta_tokens", "norm1_w", "w_in", "gdn_conv_w", "gdn_a_log", "gdn_dt_bias", "gdn_norm_w", "ssd_conv_w",
            "ssd_conv_b", "ssd_dt_bias", "ssd_a_log", "ssd_d", "ssd_norm_w", "swa_sinks", "w_proj_gdn", "w_proj_ssd",
            "w_proj_swa", "w_out", "norm2_w", "w_up", "w_down", "final_norm_w")
_SHARD_AXIS = {"meta_tokens": 1, "w_in": 2, "gdn_conv_w": 2, "ssd_conv_w": 2, "w_proj_gdn": 1, "w_proj_ssd": 1,
               "w_proj_swa": 1, "w_out": 1, "w_up": 2, "w_down": 1}
_BIG = tuple(n for n in _WEIGHTS if n in _SHARD_AXIS)
_SMALL = tuple(n for n in _WEIGHTS if n not in _SHARD_AXIS)
FLAT_C = 1024


def _pack(arrs, rows, lead=()):
    flat = jnp.concatenate([a.reshape(*lead, -1) for a in arrs], axis=-1)
    pad = rows * FLAT_C - flat.shape[-1]
    flat = jnp.pad(flat, [(0, 0)] * len(lead) + [(0, pad)])
    return flat.reshape(*lead, rows, FLAT_C)


def _unpack(flat, shapes, lead=()):
    flat = flat.reshape(*lead, -1)
    out, off = [], 0
    for s in shapes:
        n = math.prod(s)
        out.append(flat[..., off:off + n].reshape(*lead, *s))
        off += n
    return out


def _rows_for(shapes):
    n = sum(math.prod(s) for s in shapes)
    return -(-n // (FLAT_C * 8)) * 8


def _rows_tile(r, c):
    if r <= 256:
        return r
    return 128 if c > 1024 else 256


def _join(stacked, axis):
    moved = jnp.moveaxis(stacked, 0, axis)
    return moved.reshape(*moved.shape[:axis], -1, *moved.shape[axis + 2:])


def _unjoin(full, axis):
    cut = full.reshape(*full.shape[:axis], N_DEV, full.shape[axis] // N_DEV, *full.shape[axis + 1:])
    return jnp.moveaxis(cut, axis, 0)


def kernel(x, meta_tokens, norm1_w, w_in, gdn_conv_w, gdn_a_log, gdn_dt_bias, gdn_norm_w, ssd_conv_w, ssd_conv_b,
           ssd_dt_bias, ssd_a_log, ssd_d, ssd_norm_w, swa_sinks, w_proj_gdn, w_proj_ssd, w_proj_swa, w_out, norm2_w,
           w_up, w_down, final_norm_w, loss_target, m_meta_tokens, m_norm1_w, m_w_in, m_gdn_conv_w, m_gdn_a_log,
           m_gdn_dt_bias, m_gdn_norm_w, m_ssd_conv_w, m_ssd_conv_b, m_ssd_dt_bias, m_ssd_a_log, m_ssd_d, m_ssd_norm_w,
           m_swa_sinks, m_w_proj_gdn, m_w_proj_ssd, m_w_proj_swa, m_w_out, m_norm2_w, m_w_up, m_w_down,
           m_final_norm_w, v_meta_tokens, v_norm1_w, v_w_in, v_gdn_conv_w, v_gdn_a_log, v_gdn_dt_bias, v_gdn_norm_w,
           v_ssd_conv_w, v_ssd_conv_b, v_ssd_dt_bias, v_ssd_a_log, v_ssd_d, v_ssd_norm_w, v_swa_sinks, v_w_proj_gdn,
           v_w_proj_ssd, v_w_proj_swa, v_w_out, v_norm2_w, v_w_up, v_w_down, v_final_norm_w):
    args = (meta_tokens, norm1_w, w_in, gdn_conv_w, gdn_a_log, gdn_dt_bias, gdn_norm_w, ssd_conv_w, ssd_conv_b,
            ssd_dt_bias, ssd_a_log, ssd_d, ssd_norm_w, swa_sinks, w_proj_gdn, w_proj_ssd, w_proj_swa, w_out, norm2_w,
            w_up, w_down, final_norm_w, m_meta_tokens, m_norm1_w, m_w_in, m_gdn_conv_w, m_gdn_a_log,
            m_gdn_dt_bias, m_gdn_norm_w, m_ssd_conv_w, m_ssd_conv_b, m_ssd_dt_bias, m_ssd_a_log, m_ssd_d, m_ssd_norm_w,
            m_swa_sinks, m_w_proj_gdn, m_w_proj_ssd, m_w_proj_swa, m_w_out, m_norm2_w, m_w_up, m_w_down,
            m_final_norm_w, v_meta_tokens, v_norm1_w, v_w_in, v_gdn_conv_w, v_gdn_a_log, v_gdn_dt_bias, v_gdn_norm_w,
            v_ssd_conv_w, v_ssd_conv_b, v_ssd_dt_bias, v_ssd_a_log, v_ssd_d, v_ssd_norm_w, v_swa_sinks, v_w_proj_gdn,
            v_w_proj_ssd, v_w_proj_swa, v_w_out, v_norm2_w, v_w_up, v_w_down, v_final_norm_w)
    nw = len(_WEIGHTS)
    w = dict(zip(_WEIGHTS, args[:nw]))
    m = dict(zip(_WEIGHTS, args[nw:2 * nw]))
    v = dict(zip(_WEIGHTS, args[2 * nw:]))

    depth = w["w_in"].shape[0]
    small_shapes = [w[n].shape for n in _SMALL]
    small_rows = _rows_for(small_shapes)

    def flat2(t):
        return t.reshape(-1, t.shape[-1])

    tiny_names = [n for n in _BIG if n not in _MATMUL]

    def shard(n, l):
        return w[n][l].astype(BF16)

    first = _allgather_call([shard("w_in", 0)] + [flat2(w[n]) for n in tiny_names], "gather_weights")
    w_in_stacked = first[0]
    joined = {n: _join(t.reshape(N_DEV, *w[n].shape), _SHARD_AXIS[n]) for n, t in zip(tiny_names, first[1:])}
    slot_shapes = {"w_in": (D_MODEL, sum(_SPLIT)), "w_up": (D_MODEL, D_FF), "w_down": (D_FF, D_MODEL)}
    slot_shapes.update({n: (D_MODEL, D_MODEL) for n in _LATE[:4]})

    def layer_fn(l, w_in_full, late_shards, next_shards):
        w_in_t = w_in_full.T
        if l == 0:
            def fn(x_rows, meta, p, slot, exchange_slots):
                out, g, placeholders = _layer(_embed(x_rows, meta), p, w_in_full, w_in_t, slot, late_shards,
                                              next_shards, exchange_slots)
                return (out, placeholders), g
        else:
            def fn(h_in, p, slot, exchange_slots):
                out, g, placeholders = _layer(h_in, p, w_in_full, w_in_t, slot, late_shards, next_shards,
                                              exchange_slots)
                return (out, placeholders), g
        return fn

    h, vjps = None, []
    for l in range(depth):
        slot = {n: jnp.zeros(s, F32) for n, s in slot_shapes.items()}
        p = {n: (joined[n][l] if n in joined else w[n][l]) for n in _PER_LAYER}
        more = l + 1 < depth
        next_shards = [shard("w_in", l + 1)] if more else []
        exchange_slots = tuple(jnp.zeros((3, *w[n][l + 1].shape), BF16) for n in _MATMUL) if more else ()
        lead = (x[0], joined["meta_tokens"]) if l == 0 else (h,)
        fn = layer_fn(l, _regroup_w_in(w_in_stacked), [shard(n, l) for n in _LATE], next_shards)
        (h, _), vjp, g_next = jax.vjp(fn, *lead, p, slot, exchange_slots, has_aux=True)
        if more:
            w_in_stacked = g_next[0]
        vjps.append(vjp)
    loss, head_vjp = jax.vjp(_make_loss_head(loss_target[0]), h, w["final_norm_w"].reshape(1, -1))
    dh, d_final = head_vjp(jnp.ones((), F32))
    loss = lax.psum(loss, ("x", "y", "c"))

    def by_core(name, g):
        if name == "w_in":
            shards = _ungroup_w_in(g)
            return shards[0::2], shards[1::2]
        if name == "w_up":
            t = g.reshape(D_MODEL, 4, 2, D_FF // N_DEV)
            return [t[:, q, 0] for q in range(4)], [t[:, q, 1] for q in range(4)]
        t = g.reshape(4, 2, -1, g.shape[-1])
        return [t[q, 0] for q in range(4)], [t[q, 1] for q in range(4)]

    own, incoming, layer_grads, outgoing = {}, {}, [None] * depth, ()
    for l in reversed(range(depth)):
        if l == 0:
            gx, d_meta, dp, dslot, arrived = vjps[0]((dh, tuple(outgoing)))
        else:
            dh, dp, dslot, arrived = vjps[l]((dh, tuple(outgoing)))
        incoming.update({(n, l + 1): t for n, t in zip(_MATMUL, arrived)})
        layer_grads[l] = dp
        todo = [((n, l), dslot[n]) for n in _MATMUL]
        if l == 0:
            full_grads = {"meta_tokens": d_meta}
            full_grads.update({n: jnp.stack([layer_grads[k][n] for k in range(depth)]) for n in tiny_names[1:]})
            todo += [((n, None), _unjoin(full_grads[n], _SHARD_AXIS[n]).reshape(N_DEV, -1, w[n].shape[-1]))
                     for n in tiny_names]
        pairs = [by_core(u[0], g) for u, g in todo]
        from_sibling = _sibling_exchange_call([a for a, _ in pairs], [b for _, b in pairs], "grads_to_sibling_%d" % l)
        outgoing = []
        for (u, _), (a0, a1), s in zip(todo, pairs, from_sibling):
            own[u], part = _chip_partial_call(a0, a1, s, _rows_tile(s.shape[1], s.shape[2]), "chip_partial_" + u[0])
            outgoing.append(part)
        if l == 0:
            incoming.update(zip([u for u, _ in todo], _chip_exchange_call(outgoing, "grads_to_chips")))

    g_small = {n: jnp.stack([layer_grads[k][n] for k in range(depth)]) for n in _SMALL if n != "final_norm_w"}
    g_small["final_norm_w"] = d_final.reshape(-1)

    by_name = {}
    for n in _BIG:
        layers = list(range(depth)) if n in _MATMUL else [None]
        parts = [[own[n, l][None], incoming[n, l]] for l in layers]
        r, c = own[n, layers[0]].shape
        stacked = [d[n].reshape(len(layers), r, c) for d in (w, m, v)]
        res = _adamw_call(parts, *stacked, _rows_tile(r, c), "adamw_" + n)
        by_name[n] = [t.reshape(w[n].shape) for t in res]

    small_parts = _allgather_call([_pack([g_small[n] for n in _SMALL], small_rows)], "gather_small_grads")
    small_out = _adamw_call([small_parts], *[_pack([d[n] for n in _SMALL], small_rows)[None] for d in (w, m, v)],
                            small_rows, "adamw_replicated")
    for kind in range(4):
        for n, t in zip(_SMALL, _unpack(small_out[kind][0], small_shapes)):
            by_name.setdefault(n, [None] * 4)[kind] = t

    outs = [by_name[n][kind] for kind in range(4) for n in _WEIGHTS]
    return (loss, gx[None], *outs)
```
